```python
import math
import jax, jax.numpy as jnp
from jax import lax
import numpy as np

D_MODEL = 1024
BATCH = 32
SEQ = 256
DEPTH = 2
DEC_BATCH = 4
DEC_SEQ = 4096
PAST_LEN = 256

GRID_W = 64
HEAD_DIM = 64
ROPE_BASE = 10000.0
EPS = 1e-6
NEG_INF = -1e30
A_HEADS = 8
A_KV_HEADS = 2
A_WINDOW = 128
A_BLOCK = 128
B_HEADS = 8
RET_CHUNK = 128
A_Q = A_HEADS * HEAD_DIM
A_KV = A_KV_HEADS * HEAD_DIM
B_W = B_HEADS * HEAD_DIM
EVEN_IN = A_Q + 2 * A_KV + 4 * B_W
EVEN_MIX = A_Q + B_W
C_HEADS = 16
C_W = C_HEADS * HEAD_DIM
NA_KH = 8
NA_KW = 16
N_EXPERTS = 64
TOP_K = 8
N_GROUPS = 8
TOPK_GROUPS = 4
EXPERT_FF = 256
SHARED_FF = 256
ROUTED_SCALE = 2.5
MOE_BLOCK = 256
N_EVEN = (DEPTH + 1) // 2
N_ODD = DEPTH // 2

kernel_name = "hybrid_diffusion_trunk_step"


def rms_norm(x, g):
    x32 = x.astype(jnp.float32)
    y = x32 * lax.rsqrt(jnp.mean(x32 * x32, axis=-1, keepdims=True) + EPS)
    return (y * g.astype(jnp.float32)).astype(x.dtype)


def rope_2d(x):
    L = x.shape[1]
    t = jnp.arange(L)
    half = HEAD_DIM // 2
    inv = ROPE_BASE ** (-jnp.arange(0, half, 2, dtype=jnp.float32) / half)

    def rot(xh, pos):
        ang = pos.astype(jnp.float32)[:, None] * inv[None]
        cos = jnp.cos(ang)[None, :, None]
        sin = jnp.sin(ang)[None, :, None]
        x1, x2 = jnp.split(xh, 2, axis=-1)
        return jnp.concatenate([x1 * cos - x2 * sin, x1 * sin + x2 * cos], axis=-1)

    xr, xc = jnp.split(x.astype(jnp.float32), 2, axis=-1)
    return jnp.concatenate([rot(xr, t // GRID_W), rot(xc, t % GRID_W)], axis=-1).astype(x.dtype)


def context_attention(q, k, v, sink):
    B, S, Hq, d = q.shape
    Hkv = k.shape[2]
    G = Hq // Hkv
    qg = q.reshape(B, S, Hkv, G, d)
    s = jnp.einsum('bqkgd,bjkd->bkgqj', qg, k, preferred_element_type=jnp.float32) * d ** -0.5
    if sink is not None:
        sk = jnp.broadcast_to(sink.astype(jnp.float32).reshape(Hkv, G)[None, :, :, None, None], s.shape[:-1] + (1,))
        p = jax.nn.softmax(jnp.concatenate([s, sk], axis=-1), axis=-1)[..., :-1]
    else:
        p = jax.nn.softmax(s, axis=-1)
    o = jnp.einsum('bkgqj,bjkd->bqkgd', p.astype(v.dtype), v)
    return o.reshape(B, S, Hq, d)


def window_attention(q, k, v, ck, cv, sink):
    B, L, Hq, d = q.shape
    Hkv = k.shape[2]
    G = Hq // Hkv
    n = L // A_BLOCK
    qb = q.reshape(B, n, A_BLOCK, Hkv, G, d)

    def band(t):
        tp = jnp.pad(t, ((0, 0), (A_BLOCK, A_BLOCK), (0, 0), (0, 0))).reshape(B, n + 2, A_BLOCK, Hkv, d)
        return jnp.concatenate([tp[:, :n], tp[:, 1:n + 1], tp[:, 2:]], axis=2)

    kb, vb = band(k), band(v)
    iq = jnp.arange(A_BLOCK)[:, None]
    jk = jnp.arange(3 * A_BLOCK)[None]
    rel = jk - A_BLOCK - iq
    kpos = (jnp.arange(n)[:, None, None] - 1) * A_BLOCK + jk[None]
    valid = (jnp.abs(rel)[None] <= A_WINDOW) & (kpos >= 0) & (kpos < L)
    scale = d ** -0.5
    s_loc = jnp.einsum('bnqkgd,bnjkd->bnkgqj', qb, kb, preferred_element_type=jnp.float32) * scale
    s_loc = jnp.where(valid[None, :, None, None], s_loc, NEG_INF)
    s_ctx = jnp.einsum('bnqkgd,bpkd->bnkgqp', qb, ck, preferred_element_type=jnp.float32) * scale
    sk = jnp.broadcast_to(sink.astype(jnp.float32).reshape(Hkv, G)[None, None, :, :, None, None], s_ctx.shape[:-1] + (1,))
    p = jax.nn.softmax(jnp.concatenate([s_loc, s_ctx, sk], axis=-1), axis=-1).astype(v.dtype)
    nl = 3 * A_BLOCK
    P = ck.shape[1]
    o = (jnp.einsum('bnkgqj,bnjkd->bnqkgd', p[..., :nl], vb)
         + jnp.einsum('bnkgqp,bpkd->bnqkgd', p[..., nl:nl + P], cv))
    return o.reshape(B, L, Hq, d)


def neighbourhood_attention(q, k, v, ck, cv, rpb):
    B, L, H, d = q.shape
    R = L // GRID_W
    kh = min(NA_KH, R)
    r = jnp.arange(R)
    rs = jnp.clip(r - kh // 2, 0, R - kh)
    rows = rs[:, None] + jnp.arange(kh)[None]
    cq = jnp.arange(GRID_W)
    cs = jnp.clip(cq - NA_KW // 2, 0, GRID_W - NA_KW)
    col_ok = (cq[None] >= cs[:, None]) & (cq[None] < cs[:, None] + NA_KW)
    qg = q.reshape(B, R, GRID_W, H, d)
    kg = k.reshape(B, R, GRID_W, H, d)[:, rows]
    vg = v.reshape(B, R, GRID_W, H, d)[:, rows]
    dr = rows - r[:, None] + NA_KH - 1
    dc = jnp.clip(cq[None] - cq[:, None], -(NA_KW - 1), NA_KW - 1) + NA_KW - 1
    bias = rpb.astype(jnp.float32)[:, dr[:, None, :, None], dc[None, :, None, :]]
    scale = d ** -0.5
    s_nb = jnp.einsum('brqhd,brkwhd->bhrqkw', qg, kg, preferred_element_type=jnp.float32) * scale + bias[None]
    s_nb = jnp.where(col_ok[:, None, :], s_nb, NEG_INF)
    s_ctx = jnp.einsum('brqhd,bphd->bhrqp', qg, ck, preferred_element_type=jnp.float32) * scale
    nk = kh * GRID_W
    s = jnp.concatenate([s_nb.reshape(B, H, R, GRID_W, nk), s_ctx], axis=-1)
    p = jax.nn.softmax(s, axis=-1).astype(v.dtype)
    o = (jnp.einsum('bhrqkw,brkwhd->brqhd', p[..., :nk].reshape(B, H, R, GRID_W, kh, GRID_W), vg)
         + jnp.einsum('bhrqp,bphd->brqhd', p[..., nk:], cv))
    return o.reshape(B, L, H, d)


def retention_dir(q, k, v, log_gamma, s0, include_diag):
    B, L, H, _ = q.shape
    dv = v.shape[-1]
    C = RET_CHUNK
    n = L // C

    def chunks(t):
        return t.reshape(B, n, C, H, t.shape[-1]).transpose(1, 0, 3, 2, 4)

    idx = jnp.arange(C, dtype=jnp.float32)
    diff = idx[:, None] - idx[None]
    mask = (diff >= 0) if include_diag else (diff > 0)
    dmat = jnp.where(mask[None], jnp.exp(jnp.maximum(diff, 0.0)[None] * log_gamma[:, None, None]), 0.0)
    xi = jnp.exp((idx + 1.0)[None] * log_gamma[:, None])
    zeta = jnp.exp((C - 1.0 - idx)[None] * log_gamma[:, None])
    g_chunk = jnp.exp(C * log_gamma)[:, None, None]

    def step(s, inp):
        qc, kc, vc = inp
        inner = jnp.einsum('bhid,bhjd->bhij', qc, kc) * dmat
        o = jnp.einsum('bhij,bhje->bhie', inner, vc) + jnp.einsum('bhid,bhde->bhie', qc, s) * xi[..., None]
        s = g_chunk * s + jnp.einsum('bhjd,bhje->bhde', kc * zeta[..., None], vc)
        return s, o

    s_fin, o = lax.scan(step, s0, (chunks(q), chunks(k), chunks(v)))
    return o.transpose(1, 0, 3, 2, 4).reshape(B, L, H, dv), s_fin


def retention_mixer(q, k, v, g, decay_f, decay_b, s0_f, s0_b, gn_g):
    dtype = v.dtype
    B, L = q.shape[:2]
    q32 = q.astype(jnp.float32)
    k32 = k.astype(jnp.float32) * HEAD_DIM ** -0.5
    v32 = v.astype(jnp.float32)
    lg_f = -jnp.exp(decay_f.astype(jnp.float32))
    lg_b = -jnp.exp(decay_b.astype(jnp.float32))
    o_f, s_f = retention_dir(q32, k32, v32, lg_f, s0_f.astype(jnp.float32), True)
    fl = lambda t: jnp.flip(t, axis=1)
    o_b, s_b = retention_dir(fl(q32), fl(k32), fl(v32), lg_b, s0_b.astype(jnp.float32), False)
    o = o_f + fl(o_b)
    mu = jnp.mean(o, axis=-1, keepdims=True)
    var = jnp.mean(jnp.square(o - mu), axis=-1, keepdims=True)
    o = ((o - mu) * lax.rsqrt(var + EPS)).reshape(B, L, -1) * gn_g.astype(jnp.float32)
    out = jax.nn.silu(g.astype(jnp.float32)) * o
    return out.astype(dtype), s_f.astype(dtype), s_b.astype(dtype)


def split_even(p):
    B, L, _ = p.shape
    offs = [A_Q, A_Q + A_KV, A_Q + 2 * A_KV, A_Q + 2 * A_KV + B_W, A_Q + 2 * A_KV + 2 * B_W, A_Q + 2 * A_KV + 3 * B_W]
    qa, ka, va, qb, kb, vb, gb = jnp.split(p, offs, axis=-1)
    hd = lambda t: t.reshape(B, L, -1, HEAD_DIM)
    return hd(qa), hd(ka), hd(va), hd(qb), hd(kb), hd(vb), gb


def even_context(h, w_in, w_out, sink, decay_f, decay_b, gn_g):
    B, S, _ = h.shape
    qa, ka, va, qb, kb, vb, gb = split_even(h @ w_in)
    oa = context_attention(qa, ka, va, sink)
    zero = jnp.zeros((B, B_HEADS, HEAD_DIM, HEAD_DIM), jnp.float32)
    ob, s_f, s_b = retention_mixer(qb, kb, vb, gb, decay_f, decay_b, zero, zero, gn_g)
    y = jnp.concatenate([oa.reshape(B, S, A_Q), ob], axis=-1) @ w_out
    return y, ka, va, s_f, s_b


def even_latent(h, ck, cv, s0_f, s0_b, w_in, w_out, sink, decay_f, decay_b, gn_g):
    B, L, _ = h.shape
    qa, ka, va, qb, kb, vb, gb = split_even(h @ w_in)
    oa = window_attention(rope_2d(qa), rope_2d(ka), va, ck, cv, sink)
    ob, _, _ = retention_mixer(qb, kb, vb, gb, decay_f, decay_b, s0_f, s0_b, gn_g)
    return jnp.concatenate([oa.reshape(B, L, A_Q), ob], axis=-1) @ w_out


def odd_context(h, w_in, w_out):
    B, S, _ = h.shape
    q, k, v = [t.reshape(B, S, C_HEADS, HEAD_DIM) for t in jnp.split(h @ w_in, 3, axis=-1)]
    o = context_attention(q, k, v, None)
    return o.reshape(B, S, C_W) @ w_out, k, v


def odd_latent(h, ck, cv, w_in, w_out, rpb):
    B, L, _ = h.shape
    q, k, v = [t.reshape(B, L, C_HEADS, HEAD_DIM) for t in jnp.split(h @ w_in, 3, axis=-1)]
    o = neighbourhood_attention(q, k, v, ck, cv, rpb)
    return o.reshape(B, L, C_W) @ w_out


def swiglu(x, w1, w3, w2):
    return (jax.nn.silu(x @ w1) * (x @ w3)) @ w2


def moe_ffn(x, router_w, router_b, w1, w3, w2, sw1, sw3, sw2):
    N, D = x.shape
    scores = jax.nn.sigmoid(jnp.dot(x, router_w, preferred_element_type=jnp.float32))
    biased = scores + router_b.astype(jnp.float32)
    per_group = N_EXPERTS // N_GROUPS
    g_top = lax.top_k(biased.reshape(N, N_GROUPS, per_group), 2)[0].sum(-1)
    g_idx = lax.top_k(g_top, TOPK_GROUPS)[1]
    g_mask = jnp.any(g_idx[..., None] == jnp.arange(N_GROUPS), axis=1)
    e_mask = jnp.repeat(g_mask, per_group, axis=1)
    _, idx = lax.top_k(jnp.where(e_mask, biased, NEG_INF), TOP_K)
    gate = jnp.take_along_axis(scores, idx, axis=1)
    gate = gate / jnp.sum(gate, axis=-1, keepdims=True) * ROUTED_SCALE
    A = N * TOP_K
    e_flat = idx.reshape(A)
    order = jnp.argsort(e_flat)
    e_s = e_flat[order]
    t_s = order // TOP_K
    w_s = gate.reshape(A)[order]
    counts = jnp.bincount(e_flat, length=N_EXPERTS)
    padded = (counts + MOE_BLOCK - 1) // MOE_BLOCK * MOE_BLOCK
    pad_end = jnp.cumsum(padded)
    pad_start = pad_end - padded
    start = jnp.cumsum(counts) - counts
    dest = pad_start[e_s] + jnp.arange(A) - start[e_s]
    nb = (A + N_EXPERTS * (MOE_BLOCK - 1) + MOE_BLOCK - 1) // MOE_BLOCK
    buf = jnp.zeros((nb * MOE_BLOCK, D), x.dtype).at[dest].set(x[t_s])
    block_e = jnp.minimum(jnp.searchsorted(pad_end, jnp.arange(nb) * MOE_BLOCK, side='right'), N_EXPERTS - 1)

    def run_block(args):
        xb, e = args
        return swiglu(xb, w1[e], w3[e], w2[e])

    out = lax.map(run_block, (buf.reshape(nb, MOE_BLOCK, D), block_e)).reshape(nb * MOE_BLOCK, D)
    routed = jnp.zeros((N, D), x.dtype).at[t_s].add(out[dest] * w_s[:, None].astype(x.dtype))
    return routed + swiglu(x, sw1, sw3, sw2)


def setup_inputs(seed: int = 0) -> dict:
    key = jax.random.key(seed)
    ks = iter(jax.random.split(key, 40))
    D = D_MODEL

    def nrm(shape, s):
        return jax.random.normal(next(ks), shape, jnp.float32) * s

    base_decay = -(5.0 + jnp.arange(B_HEADS, dtype=jnp.float32)) * math.log(2.0)
    return {
        "x_prompt": nrm((BATCH, SEQ, D), 1.0),
        "x_sample": nrm((DEC_BATCH, DEC_SEQ, D), 1.0),
        "cache_a_k": nrm((DEC_BATCH, N_EVEN, PAST_LEN, A_KV_HEADS, HEAD_DIM), 1.0),
        "cache_a_v": nrm((DEC_BATCH, N_EVEN, PAST_LEN, A_KV_HEADS, HEAD_DIM), 1.0),
        "state_ret_fwd": nrm((DEC_BATCH, N_EVEN, B_HEADS, HEAD_DIM, HEAD_DIM), 0.5),
        "state_ret_bwd": nrm((DEC_BATCH, N_EVEN, B_HEADS, HEAD_DIM, HEAD_DIM), 0.5),
        "cache_c_k": nrm((DEC_BATCH, N_ODD, PAST_LEN, C_HEADS, HEAD_DIM), 1.0),
        "cache_c_v": nrm((DEC_BATCH, N_ODD, PAST_LEN, C_HEADS, HEAD_DIM), 1.0),
        "c": nrm((DEC_BATCH, D), 1.0),
        "c_ctx": nrm((D,), 1.0),
        "norm1_g": 1.0 + nrm((DEPTH, D), 0.02),
        "norm2_g": 1.0 + nrm((DEPTH, D), 0.02),
        "ada_w": nrm((DEPTH, D, 6 * D), 0.5 * D ** -0.5),
        "ada_b": nrm((DEPTH, 6 * D), 0.02),
        "even_w_in": nrm((N_EVEN, D, EVEN_IN), D ** -0.5),
        "even_w_out": nrm((N_EVEN, EVEN_MIX, D), EVEN_MIX ** -0.5),
        "sink_a": nrm((N_EVEN, A_HEADS), 1.0),
        "ret_decay_fwd": base_decay[None] + nrm((N_EVEN, B_HEADS), 0.1),
        "ret_decay_bwd": base_decay[None] + nrm((N_EVEN, B_HEADS), 0.1),
        "ret_gn_g": 1.0 + nrm((N_EVEN, B_W), 0.02),
        "odd_w_in": nrm((N_ODD, D, 3 * C_W), D ** -0.5),
        "odd_w_out": nrm((N_ODD, C_W, D), C_W ** -0.5),
        "na_rpb": nrm((N_ODD, C_HEADS, 2 * NA_KH - 1, 2 * NA_KW - 1), 0.1),
        "router_w": nrm((DEPTH, D, N_EXPERTS), D ** -0.5),
        "router_b": nrm((DEPTH, N_EXPERTS), 0.01),
        "exp_w1": nrm((DEPTH, N_EXPERTS, D, EXPERT_FF), D ** -0.5),
        "exp_w3": nrm((DEPTH, N_EXPERTS, D, EXPERT_FF), D ** -0.5),
        "exp_w2": nrm((DEPTH, N_EXPERTS, EXPERT_FF, D), EXPERT_FF ** -0.5),
        "sh_w1": nrm((DEPTH, D, SHARED_FF), D ** -0.5),
        "sh_w3": nrm((DEPTH, D, SHARED_FF), D ** -0.5),
        "sh_w2": nrm((DEPTH, SHARED_FF, D), SHARED_FF ** -0.5),
        "final_g": 1.0 + nrm((D,), 0.02),
    }


def reference(x_prompt, x_sample, cache_a_k, cache_a_v, state_ret_fwd, state_ret_bwd, cache_c_k, cache_c_v,
              c, c_ctx, norm1_g, norm2_g, ada_w, ada_b, even_w_in, even_w_out, sink_a, ret_decay_fwd,
              ret_decay_bwd, ret_gn_g, odd_w_in, odd_w_out, na_rpb, router_w, router_b, exp_w1, exp_w3,
              exp_w2, sh_w1, sh_w3, sh_w2, final_g):
    xp, xs = x_prompt, x_sample
    D = xp.shape[-1]
    a_k, a_v, r_f, r_b, c_k, c_v = [], [], [], [], [], []
    for l in range(DEPTH):
        mc = jnp.split((jax.nn.silu(c_ctx) @ ada_w[l] + ada_b[l])[None, None], 6, axis=-1)
        ml = jnp.split((jax.nn.silu(c) @ ada_w[l] + ada_b[l])[:, None], 6, axis=-1)
        hp = rms_norm(xp, norm1_g[l]) * (1.0 + mc[1]) + mc[0]
        hs = rms_norm(xs, norm1_g[l]) * (1.0 + ml[1]) + ml[0]
        e = l // 2
        if l % 2 == 0:
            yp, ka, va, sf, sb = even_context(hp, even_w_in[e], even_w_out[e], sink_a[e], ret_decay_fwd[e],
                                              ret_decay_bwd[e], ret_gn_g[e])
            ys = even_latent(hs, cache_a_k[:, e], cache_a_v[:, e], state_ret_fwd[:, e], state_ret_bwd[:, e],
                             even_w_in[e], even_w_out[e], sink_a[e], ret_decay_fwd[e], ret_decay_bwd[e], ret_gn_g[e])
            a_k.append(ka)
            a_v.append(va)
            r_f.append(sf)
            r_b.append(sb)
        else:
            yp, kc, vc = odd_context(hp, odd_w_in[e], odd_w_out[e])
            ys = odd_latent(hs, cache_c_k[:, e], cache_c_v[:, e], odd_w_in[e], odd_w_out[e], na_rpb[e])
            c_k.append(kc)
            c_v.append(vc)
        xp = xp + mc[2] * yp
        xs = xs + ml[2] * ys
        hp = rms_norm(xp, norm2_g[l]) * (1.0 + mc[4]) + mc[3]
        hs = rms_norm(xs, norm2_g[l]) * (1.0 + ml[4]) + ml[3]
        fp = moe_ffn(hp.reshape(-1, D), router_w[l], router_b[l], exp_w1[l], exp_w3[l], exp_w2[l],
                     sh_w1[l], sh_w3[l], sh_w2[l]).reshape(xp.shape)
        fs = moe_ffn(hs.reshape(-1, D), router_w[l], router_b[l], exp_w1[l], exp_w3[l], exp_w2[l],
                     sh_w1[l], sh_w3[l], sh_w2[l]).reshape(xs.shape)
        xp = xp + mc[5] * fp
        xs = xs + ml[5] * fs
    y_prompt = rms_norm(xp, final_g)
    y_sample = rms_norm(xs, final_g)
    new_a_k = jnp.stack(a_k, axis=1)
    new_a_v = jnp.stack(a_v, axis=1)
    new_ret_fwd = jnp.stack(r_f, axis=1)
    new_ret_bwd = jnp.stack(r_b, axis=1)
    new_c_k = jnp.stack(c_k, axis=1)
    new_c_v = jnp.stack(c_v, axis=1)
    return (y_prompt, y_sample, new_a_k, new_a_v, new_ret_fwd, new_ret_bwd, new_c_k, new_c_v)
```

```python
import functools
import math

import jax
import jax.numpy as jnp
from jax import lax
from jax.experimental import pallas as pl
from jax.experimental.pallas import tpu as pltpu

F32 = jnp.float32
BF16 = jnp.bfloat16
HIGHEST = lax.Precision.HIGHEST

D = 1024
BATCH = 32
SEQ = 256
DEC_BATCH = 4
DEC_SEQ = 4096
PAST = 256
GRID_W = 64
HD = 64
EPS = 1e-6
NEG = -1e30
ROPE_BASE = 10000.0
A_HEADS = 8
A_KV_HEADS = 2
A_Q = A_HEADS * HD
A_KV = A_KV_HEADS * HD
B_HEADS = 8
B_W = B_HEADS * HD
EVEN_IN = A_Q + 2 * A_KV + 4 * B_W
C_HEADS = 16
C_W = C_HEADS * HD
NA_KH = 8
NA_KW = 16
N_EXPERTS = 64
TOP_K = 8
N_GROUPS = 8
TOPK_GROUPS = 4
FF = 256
ROUTED_SCALE = 2.5
MOE_BLOCK = 256
RET_CHUNK = 128
A_BLOCK = 128

NP_TOK = BATCH * SEQ
NS_TOK = DEC_BATCH * DEC_SEQ
N_TOK = NP_TOK + NS_TOK
N_ASSIGN = N_TOK * TOP_K
N_MOE_BLOCKS = (N_ASSIGN + N_EXPERTS * (MOE_BLOCK - 1) + MOE_BLOCK - 1) // MOE_BLOCK

LANES = 128
TM = 512
NA_ROWS = 8
V7X_VMEM_LIMIT = 56 * 1024 * 1024

NT_DIMS = (((1,), (1,)), ((), ()))


def _cparams(n_axes, vmem=V7X_VMEM_LIMIT):
    return pltpu.CompilerParams(dimension_semantics=("arbitrary",) * n_axes, vmem_limit_bytes=vmem)


def _seg_of_block(i, rows):
    row0 = i * rows
    return jnp.where(row0 < NP_TOK, 0, 1 + (row0 - NP_TOK) // DEC_SEQ)


def _mod_spec(chunk, rows=TM):
    return pl.BlockSpec((None, 1, D), lambda i: (_seg_of_block(i, rows), 0, chunk))


def _silu(x):
    return x * jax.nn.sigmoid(x)


def _rms(x, g):
    return x * lax.rsqrt(jnp.mean(x * x, axis=-1, keepdims=True) + EPS) * g


def _lane_lo():
    return lax.broadcasted_iota(jnp.int32, (1, LANES), 1) < HD


def _ada_kernel(c_ref, w_ref, b_ref, o_ref):
    a = _silu(c_ref[...])
    o_ref[...] = jnp.dot(a, w_ref[...], preferred_element_type=F32, precision=HIGHEST) + b_ref[...]


def _ada(cc, w, b):
    tn = 1536
    out = pl.pallas_call(
        _ada_kernel,
        grid=(6 * D // tn,),
        in_specs=[pl.BlockSpec((8, D), lambda j: (0, 0)),
                  pl.BlockSpec((D, tn), lambda j: (0, j)),
                  pl.BlockSpec((1, tn), lambda j: (0, j))],
        out_specs=pl.BlockSpec((8, tn), lambda j: (0, j)),
        out_shape=jax.ShapeDtypeStruct((8, 6 * D), F32),
        compiler_params=_cparams(1),
        name="ada",
    )(cc, w, b.reshape(1, 6 * D))
    return out.reshape(8, 1, 6 * D)


def _inproj_kernel(x_ref, g_ref, shift_ref, scale_ref, w_ref, cos_ref, sin_ref, o_ref, *, rope_cols):
    h = _rms(x_ref[...], g_ref[...]) * (1.0 + scale_ref[...]) + shift_ref[...]
    o = jnp.dot(h.astype(BF16), w_ref[...], preferred_element_type=F32)
    if rope_cols:
        cos = cos_ref[...]
        sin = sin_ref[...]
        lane = lax.broadcasted_iota(jnp.int32, (1, LANES), 1)
        first = (lane % 32) < 16
        for c in range(rope_cols // LANES):
            oc = o[:, c * LANES:(c + 1) * LANES]
            partner = jnp.where(first, pltpu.roll(oc, LANES - 16, 1), pltpu.roll(oc, 16, 1))
            o_ref[:, c * LANES:(c + 1) * LANES] = oc * cos + partner * sin
        o_ref[:, rope_cols:] = o[:, rope_cols:]
    else:
        o_ref[...] = o


def _rope_tables():
    half = HD // 2
    inv = ROPE_BASE ** (-jnp.arange(0, half, 2, dtype=F32) / half)
    t = jnp.arange(DEC_SEQ)
    ang_r = (t // GRID_W).astype(F32)[:, None] * inv[None]
    ang_c = (t % GRID_W).astype(F32)[:, None] * inv[None]

    def head(fn_r, fn_c, sign):
        return jnp.concatenate([sign[0] * fn_r, sign[1] * fn_r, sign[0] * fn_c, sign[1] * fn_c], axis=-1)

    cos = head(jnp.cos(ang_r), jnp.cos(ang_c), (1.0, 1.0))
    sin = head(jnp.sin(ang_r), jnp.sin(ang_c), (-1.0, 1.0))
    cos = jnp.concatenate([jnp.ones((TM, HD), F32), cos], axis=0)
    sin = jnp.concatenate([jnp.zeros((TM, HD), F32), sin], axis=0)
    return jnp.tile(cos, (1, 2)), jnp.tile(sin, (1, 2))


def _inproj(x, g, mod, w_bf16, rope, rope_cols):
    n_out = w_bf16.shape[1]
    npb = NP_TOK // TM
    spb = DEC_SEQ // TM

    def rope_map(i):
        return (jnp.where(i < npb, 0, 1 + (i - npb) % spb), 0)

    return pl.pallas_call(
        functools.partial(_inproj_kernel, rope_cols=rope_cols),
        grid=(N_TOK // TM,),
        in_specs=[pl.BlockSpec((TM, D), lambda i: (i, 0)),
                  pl.BlockSpec((1, D), lambda i: (0, 0)),
                  _mod_spec(0), _mod_spec(1),
                  pl.BlockSpec((D, n_out), lambda i: (0, 0)),
                  pl.BlockSpec((TM, LANES), rope_map),
                  pl.BlockSpec((TM, LANES), rope_map)],
        out_specs=pl.BlockSpec((TM, n_out), lambda i: (i, 0)),
        out_shape=jax.ShapeDtypeStruct((N_TOK, n_out), F32),
        compiler_params=_cparams(1),
        name="inproj",
    )(x, g.reshape(1, D), mod, mod, w_bf16, rope[0], rope[1])


def _softmax_av(s_list, v_list, sink=None):
    mx = s_list[0].max(axis=-1, keepdims=True)
    for s in s_list[1:]:
        mx = jnp.maximum(mx, s.max(axis=-1, keepdims=True))
    if sink is not None:
        mx = jnp.maximum(mx, sink)
    den = jnp.exp(sink - mx) if sink is not None else 0.0
    acc = None
    for s, v in zip(s_list, v_list):
        p = jnp.exp(s - mx)
        den = den + p.sum(axis=-1, keepdims=True)
        pv = jnp.dot(p.astype(BF16), v, preferred_element_type=F32)
        acc = pv if acc is None else acc + pv
    return acc / den


def _dup_half(x, j, lo):
    xr = pltpu.roll(x, HD, 1)
    return jnp.where(lo, x, xr) if j == 0 else jnp.where(lo, xr, x)


def _stack_heads(q_ref, heads, lo, scale):
    parts = []
    for h in heads:
        qp = q_ref[:, (h // 2) * LANES:(h // 2 + 1) * LANES]
        keep = lo if h % 2 == 0 else jnp.logical_not(lo)
        parts.append(jnp.where(keep, qp, 0.0) * scale)
    return jnp.concatenate(parts, axis=0).astype(BF16)


def _sink_column(sink_ref, heads, rows):
    return jnp.concatenate([jnp.full((rows, 1), sink_ref[h], F32) for h in heads], axis=0)


def _ctx_gqa_kernel(sink_ref, q_ref, k_ref, v_ref, o_ref):
    lo = _lane_lo()
    k = k_ref[...]
    v = v_ref[...]
    group = A_HEADS // A_KV_HEADS
    for j in range(A_KV_HEADS):
        heads = list(range(group * j, group * (j + 1)))
        kd = _dup_half(k, j, lo).astype(BF16)
        vd = _dup_half(v, j, lo).astype(BF16)
        q = _stack_heads(q_ref, heads, lo, HD ** -0.5)
        s = lax.dot_general(q, kd, NT_DIMS, preferred_element_type=F32)
        o = _softmax_av([s], [vd], _sink_column(sink_ref, heads, SEQ))
        for t in range(group // 2):
            pair = heads[2 * t] // 2
            o_ref[:, pair * LANES:(pair + 1) * LANES] = jnp.where(
                lo, o[(2 * t) * SEQ:(2 * t + 1) * SEQ], o[(2 * t + 1) * SEQ:(2 * t + 2) * SEQ])


def _ctx_gqa(p, sink):
    return pl.pallas_call(
        _ctx_gqa_kernel,
        grid_spec=pltpu.PrefetchScalarGridSpec(
            num_scalar_prefetch=1,
            grid=(BATCH,),
            in_specs=[pl.BlockSpec((SEQ, A_Q), lambda b, s: (b, 0)),
                      pl.BlockSpec((SEQ, A_KV), lambda b, s: (b, A_Q // A_KV)),
                      pl.BlockSpec((SEQ, A_KV), lambda b, s: (b, A_Q // A_KV + 1))],
            out_specs=pl.BlockSpec((SEQ, A_Q), lambda b, s: (b, 0))),
        out_shape=jax.ShapeDtypeStruct((NP_TOK, A_Q), F32),
        compiler_params=_cparams(1),
        name="ctx_gqa",
    )(sink, p, p, p)


def _win_kernel(sink_ref, q_ref, kp_ref, kc_ref, kn_ref, vp_ref, vc_ref, vn_ref, ck_ref, cv_ref, o_ref):
    i = pl.program_id(1)
    lo = _lane_lo()
    k = jnp.concatenate([kp_ref[...], kc_ref[...], kn_ref[...]], axis=0)
    v = jnp.concatenate([vp_ref[...], vc_ref[...], vn_ref[...]], axis=0)
    ck = ck_ref[...]
    cv = cv_ref[...]
    group = A_HEADS // A_KV_HEADS
    qpos = i * A_BLOCK + lax.broadcasted_iota(jnp.int32, (A_BLOCK, 3 * A_BLOCK), 0)
    kpos = (i - 1) * A_BLOCK + lax.broadcasted_iota(jnp.int32, (A_BLOCK, 3 * A_BLOCK), 1)
    valid = (jnp.abs(kpos - qpos) <= A_BLOCK) & (kpos >= 0) & (kpos < DEC_SEQ)
    valid = jnp.concatenate([valid] * group, axis=0)
    for j in range(A_KV_HEADS):
        heads = list(range(group * j, group * (j + 1)))
        kd = _dup_half(k, j, lo).astype(BF16)
        vd = _dup_half(v, j, lo).astype(BF16)
        ckd = _dup_half(ck, j, lo).astype(BF16)
        cvd = _dup_half(cv, j, lo).astype(BF16)
        q = _stack_heads(q_ref, heads, lo, HD ** -0.5)
        s_loc = jnp.where(valid, lax.dot_general(q, kd, NT_DIMS, preferred_element_type=F32), NEG)
        s_ctx = lax.dot_general(q, ckd, NT_DIMS, preferred_element_type=F32)
        o = _softmax_av([s_loc, s_ctx], [vd, cvd], _sink_column(sink_ref, heads, A_BLOCK))
        for t in range(group // 2):
            pair = heads[2 * t] // 2
            o_ref[:, pair * LANES:(pair + 1) * LANES] = jnp.where(
                lo, o[(2 * t) * A_BLOCK:(2 * t + 1) * A_BLOCK], o[(2 * t + 1) * A_BLOCK:(2 * t + 2) * A_BLOCK])


def _win_attention(p, cache_k, cache_v, sink):
    nblk = DEC_SEQ // A_BLOCK
    base = NP_TOK // A_BLOCK
    kcol = A_Q // A_KV

    def kv_spec(col, off):
        return pl.BlockSpec((A_BLOCK, A_KV),
                            lambda b, i, s: (base + b * nblk + jnp.clip(i + off, 0, nblk - 1), col))

    ctx_spec = pl.BlockSpec((None, PAST, A_KV), lambda b, i, s: (b, 0, 0))
    return pl.pallas_call(
        _win_kernel,
        grid_spec=pltpu.PrefetchScalarGridSpec(
            num_scalar_prefetch=1,
            grid=(DEC_BATCH, nblk),
            in_specs=[pl.BlockSpec((A_BLOCK, A_Q), lambda b, i, s: (base + b * nblk + i, 0)),
                      kv_spec(kcol, -1), kv_spec(kcol, 0), kv_spec(kcol, 1),
                      kv_spec(kcol + 1, -1), kv_spec(kcol + 1, 0), kv_spec(kcol + 1, 1),
                      ctx_spec, ctx_spec],
            out_specs=pl.BlockSpec((A_BLOCK, A_Q), lambda b, i, s: (b * nblk + i, 0))),
        out_shape=jax.ShapeDtypeStruct((NS_TOK, A_Q), F32),
        compiler_params=_cparams(2),
        name="win_attn",
    )(sink, p, p, p, p, p, p, p, cache_k.reshape(DEC_BATCH, PAST, A_KV), cache_v.reshape(DEC_BATCH, PAST, A_KV))


def _ret_kernel(df_ref, db_ref, q_ref, k_ref, v_ref, g_ref, gn_ref, s0f_ref, s0b_ref,
                o_ref, sf_ref, sb_ref, of_scr, *, length):
    c_len = RET_CHUNK
    n = length // c_len
    lo = _lane_lo()
    hi = jnp.logical_not(lo)
    row = lax.broadcasted_iota(jnp.int32, (c_len, c_len), 0)
    col = lax.broadcasted_iota(jnp.int32, (c_len, c_len), 1)
    rowp = lax.broadcasted_iota(jnp.int32, (LANES, LANES), 0)
    colp = lax.broadcasted_iota(jnp.int32, (LANES, LANES), 1)
    blockdiag = (rowp < HD) == (colp < HD)
    idx = lax.broadcasted_iota(jnp.int32, (c_len, 1), 0).astype(F32)

    def direction(dec_ref, forward):
        lg = -jnp.exp(dec_ref[...])
        diff = (row - col) if forward else (col - row)
        keep = (diff >= 0) if forward else (diff > 0)
        dist = jnp.maximum(diff, 0).astype(F32)
        dm = [jnp.where(keep, jnp.exp(dist * lg[:, off:off + 1]), 0.0) for off in (0, HD)]
        if forward:
            xi = jnp.exp((idx + 1.0) * lg)
            zeta = jnp.exp((c_len - 1.0 - idx) * lg)
        else:
            xi = jnp.exp((c_len - idx) * lg)
            zeta = jnp.exp(idx * lg)
        return dm, xi, zeta, jnp.exp(c_len * lg)

    def chunk(c, state, consts):
        dm, xi, zeta, gch = consts
        rows = pl.ds(pl.multiple_of(c * c_len, c_len), c_len)
        qc = q_ref[rows, :]
        kc = k_ref[rows, :] * HD ** -0.5
        vc = v_ref[rows, :].astype(BF16)
        kb = kc.astype(BF16)
        outs = []
        for half, keep in enumerate((lo, hi)):
            qh = jnp.where(keep, qc, 0.0).astype(BF16)
            inner = lax.dot_general(qh, kb, NT_DIMS, preferred_element_type=F32) * dm[half]
            outs.append(jnp.dot(inner.astype(BF16), vc, preferred_element_type=F32))
        cross = jnp.dot(qc.astype(BF16), state.astype(BF16), preferred_element_type=F32) * xi
        o = jnp.where(lo, outs[0], outs[1]) + cross
        kz_t = (kc * zeta).T.astype(BF16)
        upd = jnp.dot(kz_t, vc, preferred_element_type=F32)
        state = gch * state + jnp.where(blockdiag, upd, 0.0)
        return rows, o, state

    cf = direction(df_ref, True)

    def fwd_body(c, state):
        rows, o, state = chunk(c, state, cf)
        of_scr[rows, :] = o
        return state

    sf_ref[...] = lax.fori_loop(0, n, fwd_body, s0f_ref[...])

    cb = direction(db_ref, False)
    gn = gn_ref[...]

    def bwd_body(t, state):
        rows, o, state = chunk(n - 1 - t, state, cb)
        o = o + of_scr[rows, :]

        def per_head(x):
            a = jnp.where(lo, x, 0.0).sum(axis=-1, keepdims=True)
            b = jnp.where(hi, x, 0.0).sum(axis=-1, keepdims=True)
            return jnp.where(lo, a, b) * (1.0 / HD)

        d = o - per_head(o)
        y = d * lax.rsqrt(per_head(d * d) + EPS) * gn
        o_ref[rows, :] = _silu(g_ref[rows, :]) * y
        return state

    sb_ref[...] = lax.fori_loop(0, n, bwd_body, s0b_ref[...])


def _pair_lanes(v):
    return jnp.repeat(v.astype(F32), HD).reshape(B_HEADS // 2, 1, LANES)


def _blockdiag_states(s):
    b = s.shape[0]
    s = s.astype(F32).reshape(b, B_HEADS // 2, 2, HD, HD)
    z = jnp.zeros_like(s[:, :, 0])
    top = jnp.concatenate([s[:, :, 0], z], axis=-1)
    bot = jnp.concatenate([z, s[:, :, 1]], axis=-1)
    return jnp.concatenate([top, bot], axis=-2)


def _diag_states(sp):
    b = sp.shape[0]
    s = jnp.stack([sp[:, :, :HD, :HD], sp[:, :, HD:, HD:]], axis=2)
    return s.reshape(b, B_HEADS, HD, HD)


def _retention(p, row_base, batch, length, dec_f, dec_b, gn_g, s0f, s0b):
    npairs = B_HEADS // 2
    blk0 = row_base // length
    qcol = (A_Q + 2 * A_KV) // LANES

    def col_spec(off):
        return pl.BlockSpec((length, LANES), lambda b, h: (blk0 + b, qcol + off * npairs + h))

    lane_spec = pl.BlockSpec((None, 1, LANES), lambda b, h: (h, 0, 0))
    state_spec = pl.BlockSpec((None, None, LANES, LANES), lambda b, h: (b, h, 0, 0))
    state_shape = jax.ShapeDtypeStruct((batch, npairs, LANES, LANES), F32)
    return pl.pallas_call(
        functools.partial(_ret_kernel, length=length),
        grid=(batch, npairs),
        in_specs=[lane_spec, lane_spec, col_spec(0), col_spec(1), col_spec(2), col_spec(3), lane_spec,
                  state_spec, state_spec],
        out_specs=[pl.BlockSpec((length, LANES), lambda b, h: (b, h)), state_spec, state_spec],
        out_shape=[jax.ShapeDtypeStruct((batch * length, B_W), F32), state_shape, state_shape],
        scratch_shapes=[pltpu.VMEM((length, LANES), F32)],
        compiler_params=_cparams(2),
        name="retention",
    )(_pair_lanes(dec_f), _pair_lanes(dec_b), p, p, p, p, gn_g.reshape(npairs, 1, LANES), s0f, s0b)


def _ctx_mha_kernel(q_ref, k_ref, v_ref, o_ref):
    lo = _lane_lo()
    for pair in range(C_HEADS // 2):
        cols = slice(pair * LANES, (pair + 1) * LANES)
        q = _stack_heads(q_ref, [2 * pair, 2 * pair + 1], lo, HD ** -0.5)
        s = lax.dot_general(q, k_ref[:, cols].astype(BF16), NT_DIMS, preferred_element_type=F32)
        o = _softmax_av([s], [v_ref[:, cols].astype(BF16)])
        o_ref[:, cols] = jnp.where(lo, o[:SEQ], o[SEQ:])


def _ctx_mha(p):
    return pl.pallas_call(
        _ctx_mha_kernel,
        grid=(BATCH,),
        in_specs=[pl.BlockSpec((SEQ, C_W), lambda b: (b, 0)),
                  pl.BlockSpec((SEQ, C_W), lambda b: (b, 1)),
                  pl.BlockSpec((SEQ, C_W), lambda b: (b, 2))],
        out_specs=pl.BlockSpec((SEQ, C_W), lambda b: (b, 0)),
        out_shape=jax.ShapeDtypeStruct((NP_TOK, C_W), F32),
        compiler_params=_cparams(1),
        name="ctx_mha",
    )(p, p, p)


NA_WIN_ROWS = 2 * NA_ROWS
NA_WIN = NA_WIN_ROWS * GRID_W
NA_QROWS = NA_ROWS * GRID_W
NA_PAD_ROWS = NA_KH // 2
NA_TABLE = 1536


def _na_kernel(q_ref, kp_ref, km_ref, kn_ref, vp_ref, vm_ref, vn_ref, ck_ref, cv_ref, ue_ref, uo_ref, o_ref):
    r0 = pl.program_id(2) * NA_ROWS
    n_rows = DEC_SEQ // GRID_W
    lo = _lane_lo()
    k = jnp.concatenate([kp_ref[...], km_ref[...], kn_ref[...]], axis=0).astype(BF16)
    v = jnp.concatenate([vp_ref[...], vm_ref[...], vn_ref[...]], axis=0).astype(BF16)
    ck = ck_ref[...].astype(BF16)
    cv = cv_ref[...].astype(BF16)
    q = q_ref[...] * HD ** -0.5
    klane = lax.broadcasted_iota(jnp.int32, (1, NA_WIN), 1)
    outs = []
    for half, keep in enumerate((lo, jnp.logical_not(lo))):
        qh = jnp.where(keep, q, 0.0).astype(BF16)
        s = lax.dot_general(qh, k, NT_DIMS, preferred_element_type=F32)
        s_ctx = lax.dot_general(qh, ck, NT_DIMS, preferred_element_type=F32)
        pieces = []
        for rq in range(NA_ROWS):
            start = NA_KH - 1 - rq
            if start % 2 == 0:
                u = ue_ref[half, :, start * GRID_W:start * GRID_W + NA_WIN]
            else:
                u = uo_ref[half, :, (start - 1) * GRID_W:(start - 1) * GRID_W + NA_WIN]
            r = r0 + rq
            first = jnp.clip(r - NA_KH // 2, 0, n_rows - NA_KH)
            lane0 = (first - r0 + NA_PAD_ROWS) * GRID_W
            in_rows = (klane >= lane0) & (klane < lane0 + NA_KH * GRID_W)
            pieces.append(jnp.where(in_rows, s[rq * GRID_W:(rq + 1) * GRID_W] + u, NEG))
        outs.append(_softmax_av([jnp.concatenate(pieces, axis=0), s_ctx], [v, cv]))
    o_ref[...] = jnp.where(lo, outs[0], outs[1])


def _na_bias_tables(rpb):
    cq = jnp.arange(GRID_W)
    ck = jnp.arange(GRID_W)
    dc = jnp.clip(ck[None] - cq[:, None], -(NA_KW - 1), NA_KW - 1) + NA_KW - 1
    cs = jnp.clip(cq - NA_KW // 2, 0, GRID_W - NA_KW)
    col_ok = (ck[None] >= cs[:, None]) & (ck[None] < cs[:, None] + NA_KW)
    t = rpb.astype(F32)[:, :, dc]
    t = jnp.where(col_ok[None, None], t, NEG).transpose(0, 2, 1, 3)
    n_dr = 2 * NA_KH - 1
    blocks = NA_TABLE // GRID_W
    t = jnp.pad(t, ((0, 0), (0, 0), (NA_PAD_ROWS, blocks - n_dr - NA_PAD_ROWS), (0, 0)), constant_values=NEG)
    ue = t.reshape(C_HEADS, GRID_W, NA_TABLE)
    uo = jnp.concatenate([ue[..., GRID_W:], jnp.full((C_HEADS, GRID_W, GRID_W), NEG, F32)], axis=-1)
    return ue, uo


def _na_attention(p, cache_k, cache_v, rpb):
    npairs = C_HEADS // 2
    nrb = DEC_SEQ // NA_QROWS
    half = NA_QROWS // 2
    qbase = NP_TOK // NA_QROWS
    hbase = NP_TOK // half
    kcol = C_W // LANES
    ue, uo = _na_bias_tables(rpb)

    def main_spec(col0):
        return pl.BlockSpec((NA_QROWS, LANES), lambda b, h, r: (qbase + b * nrb + r, col0 + h))

    def side_spec(col0, off):
        return pl.BlockSpec((half, LANES),
                            lambda b, h, r: (hbase + b * 2 * nrb + jnp.clip(2 * r + off, 0, 2 * nrb - 1), col0 + h))

    ctx_spec = pl.BlockSpec((None, PAST, LANES), lambda b, h, r: (b, 0, h))
    tab_spec = pl.BlockSpec((2, GRID_W, NA_TABLE), lambda b, h, r: (h, 0, 0))
    return pl.pallas_call(
        _na_kernel,
        grid=(DEC_BATCH, npairs, nrb),
        in_specs=[main_spec(0),
                  side_spec(kcol, -1), main_spec(kcol), side_spec(kcol, 2),
                  side_spec(2 * kcol, -1), main_spec(2 * kcol), side_spec(2 * kcol, 2),
                  ctx_spec, ctx_spec, tab_spec, tab_spec],
        out_specs=pl.BlockSpec((NA_QROWS, LANES), lambda b, h, r: (b * nrb + r, h)),
        out_shape=jax.ShapeDtypeStruct((NS_TOK, C_W), F32),
        compiler_params=_cparams(3),
        name="na_attn",
    )(p, p, p, p, p, p, p, cache_k.reshape(DEC_BATCH, PAST, C_W), cache_v.reshape(DEC_BATCH, PAST, C_W), ue, uo)


def _route(biased, scores):
    t = biased.shape[1]
    per_group = N_EXPERTS // N_GROUPS
    i8 = lax.broadcasted_iota(jnp.int32, (per_group, t), 0)
    g_rows = []
    for g in range(N_GROUPS):
        bg = biased[g * per_group:(g + 1) * per_group]
        m1 = bg.max(axis=0, keepdims=True)
        first = jnp.where(bg == m1, i8, per_group).min(axis=0, keepdims=True)
        m2 = jnp.where(i8 == first, -jnp.inf, bg).max(axis=0, keepdims=True)
        g_rows.append(m1 + m2)
    g_top = jnp.concatenate(g_rows, axis=0)
    gi = lax.broadcasted_iota(jnp.int32, g_top.shape, 0)
    g_sel = jnp.zeros(g_top.shape, jnp.int32)
    cur = g_top
    for _ in range(TOPK_GROUPS):
        m = cur.max(axis=0, keepdims=True)
        hit = gi == jnp.where(cur == m, gi, N_GROUPS).min(axis=0, keepdims=True)
        g_sel = jnp.where(hit, 1, g_sel)
        cur = jnp.where(hit, -jnp.inf, cur)
    e_sel = jnp.concatenate([jnp.broadcast_to(g_sel[g:g + 1], (per_group, t)) for g in range(N_GROUPS)], axis=0)
    cur = jnp.where(e_sel > 0, biased, NEG)
    ei = lax.broadcasted_iota(jnp.int32, cur.shape, 0)
    ids, gates = [], []
    for _ in range(TOP_K):
        m = cur.max(axis=0, keepdims=True)
        f = jnp.where(cur == m, ei, N_EXPERTS).min(axis=0, keepdims=True)
        hit = ei == f
        ids.append(f)
        gates.append(jnp.where(hit, scores, 0.0).sum(axis=0, keepdims=True))
        cur = jnp.where(hit, -jnp.inf, cur)
    gate = jnp.concatenate(gates, axis=0)
    gate = gate / gate.sum(axis=0, keepdims=True) * ROUTED_SCALE
    return jnp.concatenate(ids, axis=0), gate


def _outproj_kernel(*refs, n_parts):
    x_ref = refs[0]
    part_refs = refs[1:1 + 3 * n_parts]
    gate_ref, shift_ref, scale_ref, g2_ref, rw_ref, rb_ref = refs[1 + 3 * n_parts:7 + 3 * n_parts]
    xo_ref, h_ref, idx_ref, wgt_ref = refs[7 + 3 * n_parts:]
    is_prompt = pl.program_id(0) < NP_TOK // TM
    y = None
    for t in range(n_parts):
        ap_ref, as_ref, w_ref = part_refs[3 * t:3 * t + 3]
        a = jnp.where(is_prompt, ap_ref[...], as_ref[...]).astype(BF16)
        d = jnp.dot(a, w_ref[...], preferred_element_type=F32)
        y = d if y is None else y + d
    x = x_ref[...] + gate_ref[...] * y
    xo_ref[...] = x
    h = _rms(x, g2_ref[...]) * (1.0 + scale_ref[...]) + shift_ref[...]
    h_ref[...] = h.astype(BF16)
    logits = lax.dot_general(rw_ref[...], h, NT_DIMS, preferred_element_type=F32, precision=HIGHEST)
    scores = jax.nn.sigmoid(logits)
    idx, gate = _route(scores + rb_ref[...], scores)
    idx_ref[...] = idx
    wgt_ref[...] = gate


def _outproj(x, parts, mod, g2, router_w, router_b):
    npb = NP_TOK // TM
    nsb = NS_TOK // TM
    in_specs = [pl.BlockSpec((TM, D), lambda i: (i, 0))]
    args = [x]
    for ap, a_s, w in parts:
        width = ap.shape[1]
        in_specs += [pl.BlockSpec((TM, width), lambda i: (jnp.minimum(i, npb - 1), 0)),
                     pl.BlockSpec((TM, width), lambda i: (jnp.clip(i - npb, 0, nsb - 1), 0)),
                     pl.BlockSpec((width, D), lambda i: (0, 0))]
        args += [ap, a_s, w]
    in_specs += [_mod_spec(2), _mod_spec(3), _mod_spec(4),
                 pl.BlockSpec((1, D), lambda i: (0, 0)),
                 pl.BlockSpec((N_EXPERTS, D), lambda i: (0, 0)),
                 pl.BlockSpec((N_EXPERTS, 1), lambda i: (0, 0))]
    args += [mod, mod, mod, g2.reshape(1, D), router_w.T, router_b.reshape(N_EXPERTS, 1)]
    return pl.pallas_call(
        functools.partial(_outproj_kernel, n_parts=len(parts)),
        grid=(N_TOK // TM,),
        in_specs=in_specs,
        out_specs=[pl.BlockSpec((TM, D), lambda i: (i, 0)),
                   pl.BlockSpec((TM, D), lambda i: (i, 0)),
                   pl.BlockSpec((TOP_K, TM), lambda i: (0, i)),
                   pl.BlockSpec((TOP_K, TM), lambda i: (0, i))],
        out_shape=[jax.ShapeDtypeStruct((N_TOK, D), F32),
                   jax.ShapeDtypeStruct((N_TOK, D), BF16),
                   jax.ShapeDtypeStruct((TOP_K, N_TOK), jnp.int32),
                   jax.ShapeDtypeStruct((TOP_K, N_TOK), F32)],
        compiler_params=_cparams(1),
        name="outproj_router",
    )(*args)


def _experts_kernel(be_ref, nu_ref, x_ref, ws_ref, w1_ref, w3_ref, w2_ref, o_ref, w1b, w3b, w2b):
    i = pl.program_id(0)
    e = be_ref[i]
    prev = be_ref[jnp.maximum(i - 1, 0)]

    @pl.when((i == 0) | (e != prev))
    def _():
        w1b[...] = w1_ref[...].astype(BF16)
        w3b[...] = w3_ref[...].astype(BF16)
        w2b[...] = w2_ref[...].astype(BF16)

    @pl.when(i < nu_ref[0])
    def _():
        x = x_ref[...]
        a = jnp.dot(x, w1b[...], preferred_element_type=F32)
        b = jnp.dot(x, w3b[...], preferred_element_type=F32)
        h = (_silu(a) * b).astype(BF16)
        o_ref[...] = jnp.dot(h, w2b[...], preferred_element_type=F32) * ws_ref[...]

    @pl.when(i >= nu_ref[0])
    def _():
        o_ref[...] = jnp.zeros_like(o_ref)


def _experts(block_e, n_used, x_sorted, w_sorted, w1, w3, w2):
    return pl.pallas_call(
        _experts_kernel,
        grid_spec=pltpu.PrefetchScalarGridSpec(
            num_scalar_prefetch=2,
            grid=(N_MOE_BLOCKS,),
            in_specs=[pl.BlockSpec((MOE_BLOCK, D), lambda i, be, nu: (i, 0)),
                      pl.BlockSpec((MOE_BLOCK, 1), lambda i, be, nu: (i, 0)),
                      pl.BlockSpec((None, D, FF), lambda i, be, nu: (be[i], 0, 0)),
                      pl.BlockSpec((None, D, FF), lambda i, be, nu: (be[i], 0, 0)),
                      pl.BlockSpec((None, FF, D), lambda i, be, nu: (be[i], 0, 0))],
            out_specs=pl.BlockSpec((MOE_BLOCK, D), lambda i, be, nu: (i, 0)),
            scratch_shapes=[pltpu.VMEM((D, FF), BF16), pltpu.VMEM((D, FF), BF16), pltpu.VMEM((FF, D), BF16)]),
        out_shape=jax.ShapeDtypeStruct((N_MOE_BLOCKS * MOE_BLOCK, D), F32),
        compiler_params=_cparams(1),
        name="experts",
    )(block_e, n_used, x_sorted, w_sorted, w1, w3, w2)


def _dispatch_plan(idx_t, gate_t):
    e_flat = idx_t.T.reshape(N_ASSIGN)
    order = jnp.argsort(e_flat)
    e_s = e_flat[order]
    t_s = (order // TOP_K).astype(jnp.int32)
    w_s = gate_t.T.reshape(N_ASSIGN)[order]
    counts = jnp.bincount(e_flat, length=N_EXPERTS)
    padded = (counts + MOE_BLOCK - 1) // MOE_BLOCK * MOE_BLOCK
    pad_end = jnp.cumsum(padded)
    pad_start = pad_end - padded
    start = jnp.cumsum(counts) - counts
    dest = (pad_start[e_s] + jnp.arange(N_ASSIGN) - start[e_s]).astype(jnp.int32)
    rows = N_MOE_BLOCKS * MOE_BLOCK
    row_tok = jnp.zeros((rows,), jnp.int32).at[dest].set(t_s)
    row_w = jnp.zeros((rows,), F32).at[dest].set(w_s)
    block_e = jnp.minimum(jnp.searchsorted(pad_end, jnp.arange(N_MOE_BLOCKS) * MOE_BLOCK, side='right'),
                          N_EXPERTS - 1).astype(jnp.int32)
    n_used = (pad_end[-1] // MOE_BLOCK).astype(jnp.int32).reshape(1)
    slot_pos = jnp.zeros((N_ASSIGN,), jnp.int32).at[order].set(dest).reshape(N_TOK, TOP_K)
    return row_tok, row_w, block_e, n_used, slot_pos


def _combine_kernel(x_ref, h_ref, r_ref, gate_ref, w1_ref, w3_ref, w2_ref, fg_ref, o_ref, *, final):
    h = h_ref[...]
    a = jnp.dot(h, w1_ref[...], preferred_element_type=F32)
    b = jnp.dot(h, w3_ref[...], preferred_element_type=F32)
    s = jnp.dot((_silu(a) * b).astype(BF16), w2_ref[...], preferred_element_type=F32)
    x = x_ref[...] + gate_ref[...] * (r_ref[...] + s)
    o_ref[...] = _rms(x, fg_ref[...]) if final else x


def _combine(x, h, routed, mod, sw1, sw3, sw2, final_g, final):
    row = pl.BlockSpec((TM, D), lambda i: (i, 0))
    return pl.pallas_call(
        functools.partial(_combine_kernel, final=final),
        grid=(N_TOK // TM,),
        in_specs=[row, row, row, _mod_spec(5),
                  pl.BlockSpec((D, FF), lambda i: (0, 0)),
                  pl.BlockSpec((D, FF), lambda i: (0, 0)),
                  pl.BlockSpec((FF, D), lambda i: (0, 0)),
                  pl.BlockSpec((1, D), lambda i: (0, 0))],
        out_specs=row,
        out_shape=jax.ShapeDtypeStruct((N_TOK, D), F32),
        compiler_params=_cparams(1),
        name="combine",
    )(x, h, routed, mod, sw1.astype(BF16), sw3.astype(BF16), sw2.astype(BF16), final_g.reshape(1, D))


def kernel(x_prompt, x_sample, cache_a_k, cache_a_v, state_ret_fwd, state_ret_bwd, cache_c_k, cache_c_v,
           c, c_ctx, norm1_g, norm2_g, ada_w, ada_b, even_w_in, even_w_out, sink_a, ret_decay_fwd,
           ret_decay_bwd, ret_gn_g, odd_w_in, odd_w_out, na_rpb, router_w, router_b, exp_w1, exp_w3,
           exp_w2, sh_w1, sh_w3, sh_w2, final_g):
    x = jnp.concatenate([x_prompt.reshape(NP_TOK, D), x_sample.reshape(NS_TOK, D)], axis=0)
    cc = jnp.concatenate([c_ctx[None], c, jnp.zeros((8 - 1 - DEC_BATCH, D), F32)], axis=0)
    rope = _rope_tables()
    outs = {}
    for l in range(2):
        mod = _ada(cc, ada_w[l], ada_b[l])
        if l == 0:
            p = _inproj(x, norm1_g[l], mod, even_w_in[0].astype(BF16), rope, A_Q + A_KV)
            oa_p = _ctx_gqa(p, sink_a[0])
            oa_s = _win_attention(p, cache_a_k[:, 0], cache_a_v[:, 0], sink_a[0])
            zero = jnp.zeros((BATCH, B_HEADS // 2, LANES, LANES), F32)
            ob_p, sf, sb = _retention(p, 0, BATCH, SEQ, ret_decay_fwd[0], ret_decay_bwd[0], ret_gn_g[0], zero, zero)
            ob_s, _, _ = _retention(p, NP_TOK, DEC_BATCH, DEC_SEQ, ret_decay_fwd[0], ret_decay_bwd[0], ret_gn_g[0],
                                    _blockdiag_states(state_ret_fwd[:, 0]), _blockdiag_states(state_ret_bwd[:, 0]))
            w_out = even_w_out[0].astype(BF16)
            parts = [(oa_p, oa_s, w_out[:A_Q]), (ob_p, ob_s, w_out[A_Q:])]
            outs["a_k"] = p[:NP_TOK, A_Q:A_Q + A_KV].reshape(BATCH, 1, SEQ, A_KV_HEADS, HD)
            outs["a_v"] = p[:NP_TOK, A_Q + A_KV:A_Q + 2 * A_KV].reshape(BATCH, 1, SEQ, A_KV_HEADS, HD)
            outs["r_f"] = _diag_states(sf).reshape(BATCH, 1, B_HEADS, HD, HD)
            outs["r_b"] = _diag_states(sb).reshape(BATCH, 1, B_HEADS, HD, HD)
        else:
            p = _inproj(x, norm1_g[l], mod, odd_w_in[0].astype(BF16), rope, 0)
            o_p = _ctx_mha(p)
            o_s = _na_attention(p, cache_c_k[:, 0], cache_c_v[:, 0], na_rpb[0])
            parts = [(o_p, o_s, odd_w_out[0].astype(BF16))]
            outs["c_k"] = p[:NP_TOK, C_W:2 * C_W].reshape(BATCH, 1, SEQ, C_HEADS, HD)
            outs["c_v"] = p[:NP_TOK, 2 * C_W:3 * C_W].reshape(BATCH, 1, SEQ, C_HEADS, HD)
        x, h, idx_t, gate_t = _outproj(x, parts, mod, norm2_g[l], router_w[l], router_b[l])
        row_tok, row_w, block_e, n_used, slot_pos = _dispatch_plan(idx_t, gate_t)
        y = _experts(block_e, n_used, h[row_tok], row_w[:, None], exp_w1[l], exp_w3[l], exp_w2[l])
        routed = y[slot_pos].sum(axis=1)
        x = _combine(x, h, routed, mod, sh_w1[l], sh_w3[l], sh_w2[l], final_g, final=(l == 1))
    y_prompt = x[:NP_TOK].reshape(BATCH, SEQ, D)
    y_sample = x[NP_TOK:].reshape(DEC_BATCH, DEC_SEQ, D)
    return (y_prompt, y_sample, outs["a_k"], outs["a_v"], outs["r_f"], outs["r_b"], outs["c_k"], outs["c_v"])
```

```python
import functools
import math

import jax
import jax.numpy as jnp
from jax import lax
from jax.experimental import pallas as pl
from jax.experimental.pallas import tpu as pltpu
from jax.experimental.pallas import tpu_sc as plsc

F32 = jnp.float32
BF16 = jnp.bfloat16
HIGHEST = lax.Precision.HIGHEST

D = 1024
BATCH = 32
SEQ = 256
DEC_BATCH = 4
DEC_SEQ = 4096
PAST = 256
GRID_W = 64
HD = 64
EPS = 1e-6
NEG = -1e30
ROPE_BASE = 10000.0
A_HEADS = 8
A_KV_HEADS = 2
A_Q = A_HEADS * HD
A_KV = A_KV_HEADS * HD
B_HEADS = 8
B_W = B_HEADS * HD
EVEN_IN = A_Q + 2 * A_KV + 4 * B_W
C_HEADS = 16
C_W = C_HEADS * HD
NA_KH = 8
NA_KW = 16
N_EXPERTS = 64
TOP_K = 8
N_GROUPS = 8
TOPK_GROUPS = 4
FF = 256
ROUTED_SCALE = 2.5
MOE_BLOCK = 256
RET_CHUNK = 128
A_BLOCK = 128

NP_TOK = BATCH * SEQ
NS_TOK = DEC_BATCH * DEC_SEQ
N_TOK = NP_TOK + NS_TOK
N_ASSIGN = N_TOK * TOP_K
N_MOE_BLOCKS = (N_ASSIGN + N_EXPERTS * (MOE_BLOCK - 1) + MOE_BLOCK - 1) // MOE_BLOCK

LANES = 128
TM = 512
NA_ROWS = 8
V7X_VMEM_LIMIT = 56 * 1024 * 1024

NT_DIMS = (((1,), (1,)), ((), ()))


def _cparams(n_axes, vmem=V7X_VMEM_LIMIT):
    return pltpu.CompilerParams(dimension_semantics=("arbitrary",) * n_axes, vmem_limit_bytes=vmem)


def _seg_of_block(i, rows):
    row0 = i * rows
    return jnp.where(row0 < NP_TOK, 0, 1 + (row0 - NP_TOK) // DEC_SEQ)


def _mod_spec(chunk, rows=TM):
    return pl.BlockSpec((None, 1, D), lambda i: (_seg_of_block(i, rows), 0, chunk))


def _silu(x):
    return x * jax.nn.sigmoid(x)


def _rms(x, g):
    return x * lax.rsqrt(jnp.mean(x * x, axis=-1, keepdims=True) + EPS) * g


def _lane_lo():
    return lax.broadcasted_iota(jnp.int32, (1, LANES), 1) < HD


def _ada_kernel(c_ref, w_ref, b_ref, o_ref):
    a = _silu(c_ref[...])
    o_ref[...] = jnp.dot(a, w_ref[...], preferred_element_type=F32, precision=HIGHEST) + b_ref[...]


def _ada(cc, w, b):
    tn = 1536
    out = pl.pallas_call(
        _ada_kernel,
        grid=(6 * D // tn,),
        in_specs=[pl.BlockSpec((8, D), lambda j: (0, 0)),
                  pl.BlockSpec((D, tn), lambda j: (0, j)),
                  pl.BlockSpec((1, tn), lambda j: (0, j))],
        out_specs=pl.BlockSpec((8, tn), lambda j: (0, j)),
        out_shape=jax.ShapeDtypeStruct((8, 6 * D), F32),
        compiler_params=_cparams(1),
        name="ada",
    )(cc, w, b.reshape(1, 6 * D))
    return out.reshape(8, 1, 6 * D)


def _inproj_kernel(x_ref, g_ref, shift_ref, scale_ref, w_ref, cos_ref, sin_ref, o_ref, *, rope_cols):
    h = _rms(x_ref[...], g_ref[...]) * (1.0 + scale_ref[...]) + shift_ref[...]
    o = jnp.dot(h.astype(BF16), w_ref[...], preferred_element_type=F32)
    if rope_cols:
        cos = cos_ref[...]
        sin = sin_ref[...]
        lane = lax.broadcasted_iota(jnp.int32, (1, LANES), 1)
        first = (lane % 32) < 16
        for c in range(rope_cols // LANES):
            oc = o[:, c * LANES:(c + 1) * LANES]
            partner = jnp.where(first, pltpu.roll(oc, LANES - 16, 1), pltpu.roll(oc, 16, 1))
            o_ref[:, c * LANES:(c + 1) * LANES] = oc * cos + partner * sin
        o_ref[:, rope_cols:] = o[:, rope_cols:]
    else:
        o_ref[...] = o


def _rope_tables():
    half = HD // 2
    inv = ROPE_BASE ** (-jnp.arange(0, half, 2, dtype=F32) / half)
    t = jnp.arange(DEC_SEQ)
    ang_r = (t // GRID_W).astype(F32)[:, None] * inv[None]
    ang_c = (t % GRID_W).astype(F32)[:, None] * inv[None]

    def head(fn_r, fn_c, sign):
        return jnp.concatenate([sign[0] * fn_r, sign[1] * fn_r, sign[0] * fn_c, sign[1] * fn_c], axis=-1)

    cos = head(jnp.cos(ang_r), jnp.cos(ang_c), (1.0, 1.0))
    sin = head(jnp.sin(ang_r), jnp.sin(ang_c), (-1.0, 1.0))
    cos = jnp.concatenate([jnp.ones((TM, HD), F32), cos], axis=0)
    sin = jnp.concatenate([jnp.zeros((TM, HD), F32), sin], axis=0)
    return jnp.tile(cos, (1, 2)), jnp.tile(sin, (1, 2))


def _inproj(x, g, mod, w_bf16, rope, rope_cols):
    n_out = w_bf16.shape[1]
    npb = NP_TOK // TM
    spb = DEC_SEQ // TM

    def rope_map(i):
        return (jnp.where(i < npb, 0, 1 + (i - npb) % spb), 0)

    return pl.pallas_call(
        functools.partial(_inproj_kernel, rope_cols=rope_cols),
        grid=(N_TOK // TM,),
        in_specs=[pl.BlockSpec((TM, D), lambda i: (i, 0)),
                  pl.BlockSpec((1, D), lambda i: (0, 0)),
                  _mod_spec(0), _mod_spec(1),
                  pl.BlockSpec((D, n_out), lambda i: (0, 0)),
                  pl.BlockSpec((TM, LANES), rope_map),
                  pl.BlockSpec((TM, LANES), rope_map)],
        out_specs=pl.BlockSpec((TM, n_out), lambda i: (i, 0)),
        out_shape=jax.ShapeDtypeStruct((N_TOK, n_out), F32),
        compiler_params=_cparams(1),
        name="inproj",
    )(x, g.reshape(1, D), mod, mod, w_bf16, rope[0], rope[1])


def _softmax_av(s_list, v_list, sink=None):
    mx = s_list[0].max(axis=-1, keepdims=True)
    for s in s_list[1:]:
        mx = jnp.maximum(mx, s.max(axis=-1, keepdims=True))
    if sink is not None:
        mx = jnp.maximum(mx, sink)
    den = jnp.exp(sink - mx) if sink is not None else 0.0
    acc = None
    for s, v in zip(s_list, v_list):
        p = jnp.exp(s - mx)
        den = den + p.sum(axis=-1, keepdims=True)
        pv = jnp.dot(p.astype(BF16), v, preferred_element_type=F32)
        acc = pv if acc is None else acc + pv
    return acc / den


def _dup_half(x, j, lo):
    xr = pltpu.roll(x, HD, 1)
    return jnp.where(lo, x, xr) if j == 0 else jnp.where(lo, xr, x)


def _stack_heads(q_ref, heads, lo, scale):
    parts = []
    for h in heads:
        qp = q_ref[:, (h // 2) * LANES:(h // 2 + 1) * LANES]
        keep = lo if h % 2 == 0 else jnp.logical_not(lo)
        parts.append(jnp.where(keep, qp, 0.0) * scale)
    return jnp.concatenate(parts, axis=0).astype(BF16)


def _sink_column(sink_ref, heads, rows):
    return jnp.concatenate([jnp.full((rows, 1), sink_ref[h], F32) for h in heads], axis=0)


def _ctx_gqa_kernel(sink_ref, q_ref, k_ref, v_ref, o_ref):
    lo = _lane_lo()
    k = k_ref[...]
    v = v_ref[...]
    group = A_HEADS // A_KV_HEADS
    for j in range(A_KV_HEADS):
        heads = list(range(group * j, group * (j + 1)))
        kd = _dup_half(k, j, lo).astype(BF16)
        vd = _dup_half(v, j, lo).astype(BF16)
        q = _stack_heads(q_ref, heads, lo, HD ** -0.5)
        s = lax.dot_general(q, kd, NT_DIMS, preferred_element_type=F32)
        o = _softmax_av([s], [vd], _sink_column(sink_ref, heads, SEQ))
        for t in range(group // 2):
            pair = heads[2 * t] // 2
            o_ref[:, pair * LANES:(pair + 1) * LANES] = jnp.where(
                lo, o[(2 * t) * SEQ:(2 * t + 1) * SEQ], o[(2 * t + 1) * SEQ:(2 * t + 2) * SEQ])


def _ctx_gqa(p, sink):
    return pl.pallas_call(
        _ctx_gqa_kernel,
        grid_spec=pltpu.PrefetchScalarGridSpec(
            num_scalar_prefetch=1,
            grid=(BATCH,),
            in_specs=[pl.BlockSpec((SEQ, A_Q), lambda b, s: (b, 0)),
                      pl.BlockSpec((SEQ, A_KV), lambda b, s: (b, A_Q // A_KV)),
                      pl.BlockSpec((SEQ, A_KV), lambda b, s: (b, A_Q // A_KV + 1))],
            out_specs=pl.BlockSpec((SEQ, A_Q), lambda b, s: (b, 0))),
        out_shape=jax.ShapeDtypeStruct((NP_TOK, A_Q), F32),
        compiler_params=_cparams(1),
        name="ctx_gqa",
    )(sink, p, p, p)


def _win_kernel(sink_ref, q_ref, kp_ref, kc_ref, kn_ref, vp_ref, vc_ref, vn_ref, ck_ref, cv_ref, o_ref):
    i = pl.program_id(1)
    lo = _lane_lo()
    k = jnp.concatenate([kp_ref[...], kc_ref[...], kn_ref[...]], axis=0)
    v = jnp.concatenate([vp_ref[...], vc_ref[...], vn_ref[...]], axis=0)
    ck = ck_ref[...]
    cv = cv_ref[...]
    group = A_HEADS // A_KV_HEADS
    qpos = i * A_BLOCK + lax.broadcasted_iota(jnp.int32, (A_BLOCK, 3 * A_BLOCK), 0)
    kpos = (i - 1) * A_BLOCK + lax.broadcasted_iota(jnp.int32, (A_BLOCK, 3 * A_BLOCK), 1)
    valid = (jnp.abs(kpos - qpos) <= A_BLOCK) & (kpos >= 0) & (kpos < DEC_SEQ)
    valid = jnp.concatenate([valid] * group, axis=0)
    for j in range(A_KV_HEADS):
        heads = list(range(group * j, group * (j + 1)))
        kd = _dup_half(k, j, lo).astype(BF16)
        vd = _dup_half(v, j, lo).astype(BF16)
        ckd = _dup_half(ck, j, lo).astype(BF16)
        cvd = _dup_half(cv, j, lo).astype(BF16)
        q = _stack_heads(q_ref, heads, lo, HD ** -0.5)
        s_loc = jnp.where(valid, lax.dot_general(q, kd, NT_DIMS, preferred_element_type=F32), NEG)
        s_ctx = lax.dot_general(q, ckd, NT_DIMS, preferred_element_type=F32)
        o = _softmax_av([s_loc, s_ctx], [vd, cvd], _sink_column(sink_ref, heads, A_BLOCK))
        for t in range(group // 2):
            pair = heads[2 * t] // 2
            o_ref[:, pair * LANES:(pair + 1) * LANES] = jnp.where(
                lo, o[(2 * t) * A_BLOCK:(2 * t + 1) * A_BLOCK], o[(2 * t + 1) * A_BLOCK:(2 * t + 2) * A_BLOCK])


def _win_attention(p, cache_k, cache_v, sink):
    nblk = DEC_SEQ // A_BLOCK
    base = NP_TOK // A_BLOCK
    kcol = A_Q // A_KV

    def kv_spec(col, off):
        return pl.BlockSpec((A_BLOCK, A_KV),
                            lambda b, i, s: (base + b * nblk + jnp.clip(i + off, 0, nblk - 1), col))

    ctx_spec = pl.BlockSpec((None, PAST, A_KV), lambda b, i, s: (b, 0, 0))
    return pl.pallas_call(
        _win_kernel,
        grid_spec=pltpu.PrefetchScalarGridSpec(
            num_scalar_prefetch=1,
            grid=(DEC_BATCH, nblk),
            in_specs=[pl.BlockSpec((A_BLOCK, A_Q), lambda b, i, s: (base + b * nblk + i, 0)),
                      kv_spec(kcol, -1), kv_spec(kcol, 0), kv_spec(kcol, 1),
                      kv_spec(kcol + 1, -1), kv_spec(kcol + 1, 0), kv_spec(kcol + 1, 1),
                      ctx_spec, ctx_spec],
            out_specs=pl.BlockSpec((A_BLOCK, A_Q), lambda b, i, s: (b * nblk + i, 0))),
        out_shape=jax.ShapeDtypeStruct((NS_TOK, A_Q), F32),
        compiler_params=_cparams(2),
        name="win_attn",
    )(sink, p, p, p, p, p, p, p, cache_k.reshape(DEC_BATCH, PAST, A_KV), cache_v.reshape(DEC_BATCH, PAST, A_KV))


def _ret_kernel(df_ref, db_ref, q_ref, k_ref, v_ref, g_ref, gn_ref, s0f_ref, s0b_ref,
                o_ref, sf_ref, sb_ref, of_scr, *, length):
    c_len = RET_CHUNK
    n = length // c_len
    lo = _lane_lo()
    hi = jnp.logical_not(lo)
    row = lax.broadcasted_iota(jnp.int32, (c_len, c_len), 0)
    col = lax.broadcasted_iota(jnp.int32, (c_len, c_len), 1)
    rowp = lax.broadcasted_iota(jnp.int32, (LANES, LANES), 0)
    colp = lax.broadcasted_iota(jnp.int32, (LANES, LANES), 1)
    blockdiag = (rowp < HD) == (colp < HD)
    idx = lax.broadcasted_iota(jnp.int32, (c_len, 1), 0).astype(F32)

    def direction(dec_ref, forward):
        lg = -jnp.exp(dec_ref[...])
        diff = (row - col) if forward else (col - row)
        keep = (diff >= 0) if forward else (diff > 0)
        dist = jnp.maximum(diff, 0).astype(F32)
        dm = [jnp.where(keep, jnp.exp(dist * lg[:, off:off + 1]), 0.0) for off in (0, HD)]
        if forward:
            xi = jnp.exp((idx + 1.0) * lg)
            zeta = jnp.exp((c_len - 1.0 - idx) * lg)
        else:
            xi = jnp.exp((c_len - idx) * lg)
            zeta = jnp.exp(idx * lg)
        return dm, xi, zeta, jnp.exp(c_len * lg)

    def chunk(c, state, consts):
        dm, xi, zeta, gch = consts
        rows = pl.ds(pl.multiple_of(c * c_len, c_len), c_len)
        qc = q_ref[rows, :]
        kc = k_ref[rows, :] * HD ** -0.5
        vc = v_ref[rows, :].astype(BF16)
        kb = kc.astype(BF16)
        outs = []
        for half, keep in enumerate((lo, hi)):
            qh = jnp.where(keep, qc, 0.0).astype(BF16)
            inner = lax.dot_general(qh, kb, NT_DIMS, preferred_element_type=F32) * dm[half]
            outs.append(jnp.dot(inner.astype(BF16), vc, preferred_element_type=F32))
        cross = jnp.dot(qc.astype(BF16), state.astype(BF16), preferred_element_type=F32) * xi
        o = jnp.where(lo, outs[0], outs[1]) + cross
        kz_t = (kc * zeta).T.astype(BF16)
        upd = jnp.dot(kz_t, vc, preferred_element_type=F32)
        state = gch * state + jnp.where(blockdiag, upd, 0.0)
        return rows, o, state

    cf = direction(df_ref, True)

    def fwd_body(c, state):
        rows, o, state = chunk(c, state, cf)
        of_scr[rows, :] = o
        return state

    sf_ref[...] = lax.fori_loop(0, n, fwd_body, s0f_ref[...])

    cb = direction(db_ref, False)
    gn = gn_ref[...]

    def bwd_body(t, state):
        rows, o, state = chunk(n - 1 - t, state, cb)
        o = o + of_scr[rows, :]

        def per_head(x):
            a = jnp.where(lo, x, 0.0).sum(axis=-1, keepdims=True)
            b = jnp.where(hi, x, 0.0).sum(axis=-1, keepdims=True)
            return jnp.where(lo, a, b) * (1.0 / HD)

        d = o - per_head(o)
        y = d * lax.rsqrt(per_head(d * d) + EPS) * gn
        o_ref[rows, :] = _silu(g_ref[rows, :]) * y
        return state

    sb_ref[...] = lax.fori_loop(0, n, bwd_body, s0b_ref[...])


def _pair_lanes(v):
    return jnp.repeat(v.astype(F32), HD).reshape(B_HEADS // 2, 1, LANES)


def _blockdiag_states(s):
    b = s.shape[0]
    s = s.astype(F32).reshape(b, B_HEADS // 2, 2, HD, HD)
    z = jnp.zeros_like(s[:, :, 0])
    top = jnp.concatenate([s[:, :, 0], z], axis=-1)
    bot = jnp.concatenate([z, s[:, :, 1]], axis=-1)
    return jnp.concatenate([top, bot], axis=-2)


def _diag_states(sp):
    b = sp.shape[0]
    s = jnp.stack([sp[:, :, :HD, :HD], sp[:, :, HD:, HD:]], axis=2)
    return s.reshape(b, B_HEADS, HD, HD)


def _retention(p, row_base, batch, length, dec_f, dec_b, gn_g, s0f, s0b):
    npairs = B_HEADS // 2
    blk0 = row_base // length
    qcol = (A_Q + 2 * A_KV) // LANES

    def col_spec(off):
        return pl.BlockSpec((length, LANES), lambda b, h: (blk0 + b, qcol + off * npairs + h))

    lane_spec = pl.BlockSpec((None, 1, LANES), lambda b, h: (h, 0, 0))
    state_spec = pl.BlockSpec((None, None, LANES, LANES), lambda b, h: (b, h, 0, 0))
    state_shape = jax.ShapeDtypeStruct((batch, npairs, LANES, LANES), F32)
    return pl.pallas_call(
        functools.partial(_ret_kernel, length=length),
        grid=(batch, npairs),
        in_specs=[lane_spec, lane_spec, col_spec(0), col_spec(1), col_spec(2), col_spec(3), lane_spec,
                  state_spec, state_spec],
        out_specs=[pl.BlockSpec((length, LANES), lambda b, h: (b, h)), state_spec, state_spec],
        out_shape=[jax.ShapeDtypeStruct((batch * length, B_W), F32), state_shape, state_shape],
        scratch_shapes=[pltpu.VMEM((length, LANES), F32)],
        compiler_params=_cparams(2),
        name="retention",
    )(_pair_lanes(dec_f), _pair_lanes(dec_b), p, p, p, p, gn_g.reshape(npairs, 1, LANES), s0f, s0b)


def _ctx_mha_kernel(q_ref, k_ref, v_ref, o_ref):
    lo = _lane_lo()
    for pair in range(C_HEADS // 2):
        cols = slice(pair * LANES, (pair + 1) * LANES)
        q = _stack_heads(q_ref, [2 * pair, 2 * pair + 1], lo, HD ** -0.5)
        s = lax.dot_general(q, k_ref[:, cols].astype(BF16), NT_DIMS, preferred_element_type=F32)
        o = _softmax_av([s], [v_ref[:, cols].astype(BF16)])
        o_ref[:, cols] = jnp.where(lo, o[:SEQ], o[SEQ:])


def _ctx_mha(p):
    return pl.pallas_call(
        _ctx_mha_kernel,
        grid=(BATCH,),
        in_specs=[pl.BlockSpec((SEQ, C_W), lambda b: (b, 0)),
                  pl.BlockSpec((SEQ, C_W), lambda b: (b, 1)),
                  pl.BlockSpec((SEQ, C_W), lambda b: (b, 2))],
        out_specs=pl.BlockSpec((SEQ, C_W), lambda b: (b, 0)),
        out_shape=jax.ShapeDtypeStruct((NP_TOK, C_W), F32),
        compiler_params=_cparams(1),
        name="ctx_mha",
    )(p, p, p)


NA_WIN_ROWS = 2 * NA_ROWS
NA_WIN = NA_WIN_ROWS * GRID_W
NA_QROWS = NA_ROWS * GRID_W
NA_PAD_ROWS = NA_KH // 2
NA_TABLE = 1536


def _na_kernel(q_ref, kp_ref, km_ref, kn_ref, vp_ref, vm_ref, vn_ref, ck_ref, cv_ref, ue_ref, uo_ref, o_ref):
    r0 = pl.program_id(2) * NA_ROWS
    n_rows = DEC_SEQ // GRID_W
    lo = _lane_lo()
    k = jnp.concatenate([kp_ref[...], km_ref[...], kn_ref[...]], axis=0).astype(BF16)
    v = jnp.concatenate([vp_ref[...], vm_ref[...], vn_ref[...]], axis=0).astype(BF16)
    ck = ck_ref[...].astype(BF16)
    cv = cv_ref[...].astype(BF16)
    q = q_ref[...] * HD ** -0.5
    klane = lax.broadcasted_iota(jnp.int32, (1, NA_WIN), 1)
    outs = []
    for half, keep in enumerate((lo, jnp.logical_not(lo))):
        qh = jnp.where(keep, q, 0.0).astype(BF16)
        s = lax.dot_general(qh, k, NT_DIMS, preferred_element_type=F32)
        s_ctx = lax.dot_general(qh, ck, NT_DIMS, preferred_element_type=F32)
        pieces = []
        for rq in range(NA_ROWS):
            start = NA_KH - 1 - rq
            if start % 2 == 0:
                u = ue_ref[half, :, start * GRID_W:start * GRID_W + NA_WIN]
            else:
                u = uo_ref[half, :, (start - 1) * GRID_W:(start - 1) * GRID_W + NA_WIN]
            r = r0 + rq
            first = jnp.clip(r - NA_KH // 2, 0, n_rows - NA_KH)
            lane0 = (first - r0 + NA_PAD_ROWS) * GRID_W
            in_rows = (klane >= lane0) & (klane < lane0 + NA_KH * GRID_W)
            pieces.append(jnp.where(in_rows, s[rq * GRID_W:(rq + 1) * GRID_W] + u, NEG))
        outs.append(_softmax_av([jnp.concatenate(pieces, axis=0), s_ctx], [v, cv]))
    o_ref[...] = jnp.where(lo, outs[0], outs[1])


def _na_bias_tables(rpb):
    cq = jnp.arange(GRID_W)
    ck = jnp.arange(GRID_W)
    dc = jnp.clip(ck[None] - cq[:, None], -(NA_KW - 1), NA_KW - 1) + NA_KW - 1
    cs = jnp.clip(cq - NA_KW // 2, 0, GRID_W - NA_KW)
    col_ok = (ck[None] >= cs[:, None]) & (ck[None] < cs[:, None] + NA_KW)
    t = rpb.astype(F32)[:, :, dc]
    t = jnp.where(col_ok[None, None], t, NEG).transpose(0, 2, 1, 3)
    n_dr = 2 * NA_KH - 1
    blocks = NA_TABLE // GRID_W
    t = jnp.pad(t, ((0, 0), (0, 0), (NA_PAD_ROWS, blocks - n_dr - NA_PAD_ROWS), (0, 0)), constant_values=NEG)
    ue = t.reshape(C_HEADS, GRID_W, NA_TABLE)
    uo = jnp.concatenate([ue[..., GRID_W:], jnp.full((C_HEADS, GRID_W, GRID_W), NEG, F32)], axis=-1)
    return ue, uo


def _na_attention(p, cache_k, cache_v, rpb):
    npairs = C_HEADS // 2
    nrb = DEC_SEQ // NA_QROWS
    half = NA_QROWS // 2
    qbase = NP_TOK // NA_QROWS
    hbase = NP_TOK // half
    kcol = C_W // LANES
    ue, uo = _na_bias_tables(rpb)

    def main_spec(col0):
        return pl.BlockSpec((NA_QROWS, LANES), lambda b, h, r: (qbase + b * nrb + r, col0 + h))

    def side_spec(col0, off):
        return pl.BlockSpec((half, LANES),
                            lambda b, h, r: (hbase + b * 2 * nrb + jnp.clip(2 * r + off, 0, 2 * nrb - 1), col0 + h))

    ctx_spec = pl.BlockSpec((None, PAST, LANES), lambda b, h, r: (b, 0, h))
    tab_spec = pl.BlockSpec((2, GRID_W, NA_TABLE), lambda b, h, r: (h, 0, 0))
    return pl.pallas_call(
        _na_kernel,
        grid=(DEC_BATCH, npairs, nrb),
        in_specs=[main_spec(0),
                  side_spec(kcol, -1), main_spec(kcol), side_spec(kcol, 2),
                  side_spec(2 * kcol, -1), main_spec(2 * kcol), side_spec(2 * kcol, 2),
                  ctx_spec, ctx_spec, tab_spec, tab_spec],
        out_specs=pl.BlockSpec((NA_QROWS, LANES), lambda b, h, r: (b * nrb + r, h)),
        out_shape=jax.ShapeDtypeStruct((NS_TOK, C_W), F32),
        compiler_params=_cparams(3),
        name="na_attn",
    )(p, p, p, p, p, p, p, cache_k.reshape(DEC_BATCH, PAST, C_W), cache_v.reshape(DEC_BATCH, PAST, C_W), ue, uo)


def _route(biased, scores):
    t = biased.shape[1]
    per_group = N_EXPERTS // N_GROUPS
    i8 = lax.broadcasted_iota(jnp.int32, (per_group, t), 0)
    g_rows = []
    for g in range(N_GROUPS):
        bg = biased[g * per_group:(g + 1) * per_group]
        m1 = bg.max(axis=0, keepdims=True)
        first = jnp.where(bg == m1, i8, per_group).min(axis=0, keepdims=True)
        m2 = jnp.where(i8 == first, -jnp.inf, bg).max(axis=0, keepdims=True)
        g_rows.append(m1 + m2)
    g_top = jnp.concatenate(g_rows, axis=0)
    gi = lax.broadcasted_iota(jnp.int32, g_top.shape, 0)
    g_sel = jnp.zeros(g_top.shape, jnp.int32)
    cur = g_top
    for _ in range(TOPK_GROUPS):
        m = cur.max(axis=0, keepdims=True)
        hit = gi == jnp.where(cur == m, gi, N_GROUPS).min(axis=0, keepdims=True)
        g_sel = jnp.where(hit, 1, g_sel)
        cur = jnp.where(hit, -jnp.inf, cur)
    e_sel = jnp.concatenate([jnp.broadcast_to(g_sel[g:g + 1], (per_group, t)) for g in range(N_GROUPS)], axis=0)
    cur = jnp.where(e_sel > 0, biased, NEG)
    ei = lax.broadcasted_iota(jnp.int32, cur.shape, 0)
    ids, gates, hits = [], [], []
    for _ in range(TOP_K):
        m = cur.max(axis=0, keepdims=True)
        f = jnp.where(cur == m, ei, N_EXPERTS).min(axis=0, keepdims=True)
        hit = ei == f
        ids.append(f)
        hits.append(hit)
        gates.append(jnp.where(hit, scores, 0.0).sum(axis=0, keepdims=True))
        cur = jnp.where(hit, -jnp.inf, cur)
    gate = jnp.concatenate(gates, axis=0)
    gate = gate / gate.sum(axis=0, keepdims=True) * ROUTED_SCALE
    return jnp.concatenate(ids, axis=0), gate, hits


def _pack_bf16_pairs(h):
    bits = lax.bitcast_convert_type(h.astype(BF16).astype(F32), jnp.uint32)
    return bits[:, :D // 2] | (bits[:, D // 2:] >> 16)


def _unpack_bf16_pairs(xp):
    hi = lax.bitcast_convert_type(xp & jnp.uint32(0xFFFF0000), F32).astype(BF16)
    lo = lax.bitcast_convert_type(xp << 16, F32).astype(BF16)
    return hi, lo


def _dot_halves(hi, lo, w_ref):
    return (jnp.dot(hi, w_ref[:D // 2, :], preferred_element_type=F32)
            + jnp.dot(lo, w_ref[D // 2:, :], preferred_element_type=F32))


def _outproj_kernel(*refs, n_parts):
    x_ref = refs[0]
    part_refs = refs[1:1 + 3 * n_parts]
    gate_ref, shift_ref, scale_ref, g2_ref, rw_ref, rb_ref = refs[1 + 3 * n_parts:7 + 3 * n_parts]
    xo_ref, h_ref, idx_ref, wgt_ref, rank_ref, cnt_ref = refs[7 + 3 * n_parts:]
    is_prompt = pl.program_id(0) < NP_TOK // TM

    @pl.when(pl.program_id(0) == 0)
    def _():
        cnt_ref[...] = jnp.zeros_like(cnt_ref)

    y = None
    for t in range(n_parts):
        ap_ref, as_ref, w_ref = part_refs[3 * t:3 * t + 3]
        a = jnp.where(is_prompt, ap_ref[...], as_ref[...]).astype(BF16)
        d = jnp.dot(a, w_ref[...], preferred_element_type=F32)
        y = d if y is None else y + d
    x = x_ref[...] + gate_ref[...] * y
    xo_ref[...] = x
    h = _rms(x, g2_ref[...]) * (1.0 + scale_ref[...]) + shift_ref[...]
    h_ref[...] = _pack_bf16_pairs(h)
    logits = lax.dot_general(rw_ref[...], h, NT_DIMS, preferred_element_type=F32, precision=HIGHEST)
    scores = jax.nn.sigmoid(logits)
    idx, gate, hits = _route(scores + rb_ref[...], scores)
    idx_ref[...] = idx
    wgt_ref[...] = gate
    chosen = hits[0]
    for hit in hits[1:]:
        chosen = chosen | hit
    m = jnp.where(chosen, 1.0, 0.0)
    before = (lax.broadcasted_iota(jnp.int32, (TM, TM), 0) < lax.broadcasted_iota(jnp.int32, (TM, TM), 1))
    prefix = jnp.dot(m.astype(BF16), jnp.where(before, 1.0, 0.0).astype(BF16), preferred_element_type=F32)
    rank_all = prefix + cnt_ref[...]
    rank_ref[...] = jnp.concatenate(
        [jnp.where(hit, rank_all, 0.0).sum(axis=0, keepdims=True) for hit in hits], axis=0).astype(jnp.int32)
    cnt_ref[...] += m.sum(axis=1, keepdims=True)


def _outproj(x, parts, mod, g2, router_w, router_b):
    npb = NP_TOK // TM
    nsb = NS_TOK // TM
    in_specs = [pl.BlockSpec((TM, D), lambda i: (i, 0))]
    args = [x]
    for ap, a_s, w in parts:
        width = ap.shape[1]
        in_specs += [pl.BlockSpec((TM, width), lambda i: (jnp.minimum(i, npb - 1), 0)),
                     pl.BlockSpec((TM, width), lambda i: (jnp.clip(i - npb, 0, nsb - 1), 0)),
                     pl.BlockSpec((width, D), lambda i: (0, 0))]
        args += [ap, a_s, w]
    in_specs += [_mod_spec(2), _mod_spec(3), _mod_spec(4),
                 pl.BlockSpec((1, D), lambda i: (0, 0)),
                 pl.BlockSpec((N_EXPERTS, D), lambda i: (0, 0)),
                 pl.BlockSpec((N_EXPERTS, 1), lambda i: (0, 0))]
    args += [mod, mod, mod, g2.reshape(1, D), router_w.T, router_b.reshape(N_EXPERTS, 1)]
    return pl.pallas_call(
        functools.partial(_outproj_kernel, n_parts=len(parts)),
        grid=(N_TOK // TM,),
        in_specs=in_specs,
        out_specs=[pl.BlockSpec((TM, D), lambda i: (i, 0)),
                   pl.BlockSpec((TM, D // 2), lambda i: (i, 0)),
                   pl.BlockSpec((TOP_K, TM), lambda i: (0, i)),
                   pl.BlockSpec((TOP_K, TM), lambda i: (0, i)),
                   pl.BlockSpec((TOP_K, TM), lambda i: (0, i)),
                   pl.BlockSpec((N_EXPERTS, 1), lambda i: (0, 0))],
        out_shape=[jax.ShapeDtypeStruct((N_TOK, D), F32),
                   jax.ShapeDtypeStruct((N_TOK, D // 2), jnp.uint32),
                   jax.ShapeDtypeStruct((TOP_K, N_TOK), jnp.int32),
                   jax.ShapeDtypeStruct((TOP_K, N_TOK), F32),
                   jax.ShapeDtypeStruct((TOP_K, N_TOK), jnp.int32),
                   jax.ShapeDtypeStruct((N_EXPERTS, 1), F32)],
        compiler_params=_cparams(1),
        name="outproj_router",
    )(*args)


def _experts_kernel(be_ref, nu_ref, x_ref, w1_ref, w3_ref, w2_ref, o_ref, w1b, w3b, w2b):
    i = pl.program_id(0)
    e = be_ref[i]
    prev = be_ref[jnp.maximum(i - 1, 0)]

    @pl.when((i == 0) | (e != prev))
    def _():
        w1b[...] = w1_ref[...].astype(BF16)
        w3b[...] = w3_ref[...].astype(BF16)
        w2b[...] = w2_ref[...].astype(BF16)

    @pl.when(i < nu_ref[0])
    def _():
        hi, lo = _unpack_bf16_pairs(x_ref[...])
        a = _dot_halves(hi, lo, w1b)
        b = _dot_halves(hi, lo, w3b)
        h = (_silu(a) * b).astype(BF16)
        o_ref[...] = jnp.dot(h, w2b[...], preferred_element_type=F32)

    @pl.when(i >= nu_ref[0])
    def _():
        o_ref[...] = jnp.zeros_like(o_ref)


def _experts(block_e, n_used, x_sorted, w1, w3, w2):
    return pl.pallas_call(
        _experts_kernel,
        grid_spec=pltpu.PrefetchScalarGridSpec(
            num_scalar_prefetch=2,
            grid=(N_MOE_BLOCKS,),
            in_specs=[pl.BlockSpec((MOE_BLOCK, D // 2), lambda i, be, nu: (i, 0)),
                      pl.BlockSpec((None, D, FF), lambda i, be, nu: (be[i], 0, 0)),
                      pl.BlockSpec((None, D, FF), lambda i, be, nu: (be[i], 0, 0)),
                      pl.BlockSpec((None, FF, D), lambda i, be, nu: (be[i], 0, 0))],
            out_specs=pl.BlockSpec((MOE_BLOCK, D), lambda i, be, nu: (i, 0)),
            scratch_shapes=[pltpu.VMEM((D, FF), BF16), pltpu.VMEM((D, FF), BF16), pltpu.VMEM((FF, D), BF16)]),
        out_shape=jax.ShapeDtypeStruct((N_MOE_BLOCKS * MOE_BLOCK, D), F32),
        compiler_params=_cparams(1),
        name="experts",
    )(block_e, n_used, x_sorted, w1, w3, w2)


def _dispatch_plan(idx_t, rank_t, counts):
    counts = counts.astype(jnp.int32)
    padded = (counts + MOE_BLOCK - 1) // MOE_BLOCK * MOE_BLOCK
    pad_end = jnp.cumsum(padded)
    pad_start = pad_end - padded
    start = jnp.cumsum(counts) - counts
    block_row0 = jnp.arange(N_MOE_BLOCKS, dtype=jnp.int32) * MOE_BLOCK
    block_e = jnp.minimum((pad_end[None, :] <= block_row0[:, None]).sum(axis=1), N_EXPERTS - 1).astype(jnp.int32)
    n_used = (pad_end[-1] // MOE_BLOCK).astype(jnp.int32).reshape(1)
    key = idx_t * N_TOK + jnp.arange(N_TOK, dtype=jnp.int32)[None, :]
    tok_sorted = jnp.sort(key.reshape(N_ASSIGN)) % N_TOK
    local = (block_row0 - pad_start[block_e])[:, None] + jnp.arange(MOE_BLOCK, dtype=jnp.int32)[None, :]
    valid = local < counts[block_e][:, None]
    src = jnp.clip(start[block_e][:, None] + local, 0, N_ASSIGN - 1)
    row_tok = jnp.where(valid, tok_sorted[src], 0).reshape(N_MOE_BLOCKS * MOE_BLOCK).astype(jnp.int32)
    e_iota = jnp.arange(N_EXPERTS, dtype=jnp.int32)[:, None, None]
    dest = jnp.where(idx_t[None] == e_iota, pad_start[:, None, None], 0).sum(axis=0) + rank_t
    return row_tok, block_e, n_used, dest.astype(jnp.int32)


SC_CORES = 2
SC_SUBCORES = 16
SC_WORKERS = SC_CORES * SC_SUBCORES
SC_CHUNK_BYTES = 128 * 1024


def _sc_gather(table, idx):
    n_idx = idx.shape[0]
    width = table.shape[1]
    chunk = SC_CHUNK_BYTES // (4 * width)
    per_worker = n_idx // SC_WORKERS
    n_chunks = per_worker // chunk
    assert per_worker * SC_WORKERS == n_idx and n_chunks * chunk == per_worker and n_chunks % 2 == 0
    mesh = plsc.VectorSubcoreMesh(core_axis_name="c", subcore_axis_name="s")

    @functools.partial(
        pl.kernel, mesh=mesh,
        out_type=jax.ShapeDtypeStruct((n_idx, width), table.dtype),
        scratch_types=[pltpu.VMEM((2, chunk), jnp.int32),
                       pltpu.VMEM((2, chunk, width), table.dtype),
                       pltpu.SemaphoreType.DMA((2,)),
                       pltpu.SemaphoreType.DMA((2,))])
    def gather(t_hbm, i_hbm, o_hbm, idx_v, rows_v, gsem, wsem):
        base = (lax.axis_index("s") * SC_CORES + lax.axis_index("c")) * per_worker

        def rows_of(c):
            return pl.ds(pl.multiple_of(base + c * chunk, chunk), chunk)

        def gather_copy(b):
            return pltpu.make_async_copy(t_hbm.at[idx_v.at[b]], rows_v.at[b], gsem.at[b])

        def write_copy(c, b):
            return pltpu.make_async_copy(rows_v.at[b], o_hbm.at[rows_of(c)], wsem.at[b])

        def start_gather(c, b):
            pltpu.sync_copy(i_hbm.at[rows_of(c)], idx_v.at[b])
            gather_copy(b).start()

        start_gather(0, 0)

        @pl.loop(0, n_chunks, step=2)
        def _(c):
            gather_copy(0).wait()

            @pl.when(c > 0)
            def _():
                write_copy(c - 1, 1).wait()

            start_gather(c + 1, 1)
            write_copy(c, 0).start()
            gather_copy(1).wait()
            write_copy(c, 0).wait()

            @pl.when(c + 2 < n_chunks)
            def _():
                start_gather(c + 2, 0)

            write_copy(c + 1, 1).start()

        write_copy(n_chunks - 1, 1).wait()

    return gather(table, idx)


TC = 256


def _combine_kernel(x_ref, h_ref, y_ref, wgt_ref, gate_ref, w1_ref, w3_ref, w2_ref, fg_ref, o_ref, *, final):
    hi, lo = _unpack_bf16_pairs(h_ref[...])
    a = _dot_halves(hi, lo, w1_ref)
    b = _dot_halves(hi, lo, w3_ref)
    ffn = jnp.dot((_silu(a) * b).astype(BF16), w2_ref[...], preferred_element_type=F32)
    wgt = wgt_ref[...]
    for k in range(TOP_K):
        ffn = ffn + y_ref[k] * wgt[:, k:k + 1]
    x = x_ref[...] + gate_ref[...] * ffn
    o_ref[...] = _rms(x, fg_ref[...]) if final else x


def _combine(x, h, y_rows, wgt, mod, sw1, sw3, sw2, final_g, final):
    row = pl.BlockSpec((TC, D), lambda i: (i, 0))
    return pl.pallas_call(
        functools.partial(_combine_kernel, final=final),
        grid=(N_TOK // TC,),
        in_specs=[row,
                  pl.BlockSpec((TC, D // 2), lambda i: (i, 0)),
                  pl.BlockSpec((TOP_K, TC, D), lambda i: (0, i, 0)),
                  pl.BlockSpec((TC, TOP_K), lambda i: (i, 0)),
                  _mod_spec(5, TC),
                  pl.BlockSpec((D, FF), lambda i: (0, 0)),
                  pl.BlockSpec((D, FF), lambda i: (0, 0)),
                  pl.BlockSpec((FF, D), lambda i: (0, 0)),
                  pl.BlockSpec((1, D), lambda i: (0, 0))],
        out_specs=row,
        out_shape=jax.ShapeDtypeStruct((N_TOK, D), F32),
        compiler_params=_cparams(1),
        name="combine",
    )(x, h, y_rows, wgt, mod, sw1.astype(BF16), sw3.astype(BF16), sw2.astype(BF16), final_g.reshape(1, D))


def kernel(x_prompt, x_sample, cache_a_k, cache_a_v, state_ret_fwd, state_ret_bwd, cache_c_k, cache_c_v,
           c, c_ctx, norm1_g, norm2_g, ada_w, ada_b, even_w_in, even_w_out, sink_a, ret_decay_fwd,
           ret_decay_bwd, ret_gn_g, odd_w_in, odd_w_out, na_rpb, router_w, router_b, exp_w1, exp_w3,
           exp_w2, sh_w1, sh_w3, sh_w2, final_g):
    x = jnp.concatenate([x_prompt.reshape(NP_TOK, D), x_sample.reshape(NS_TOK, D)], axis=0)
    cc = jnp.concatenate([c_ctx[None], c, jnp.zeros((8 - 1 - DEC_BATCH, D), F32)], axis=0)
    rope = _rope_tables()
    outs = {}
    for l in range(2):
        mod = _ada(cc, ada_w[l], ada_b[l])
        if l == 0:
            p = _inproj(x, norm1_g[l], mod, even_w_in[0].astype(BF16), rope, A_Q + A_KV)
            oa_p = _ctx_gqa(p, sink_a[0])
            oa_s = _win_attention(p, cache_a_k[:, 0], cache_a_v[:, 0], sink_a[0])
            zero = jnp.zeros((BATCH, B_HEADS // 2, LANES, LANES), F32)
            ob_p, sf, sb = _retention(p, 0, BATCH, SEQ, ret_decay_fwd[0], ret_decay_bwd[0], ret_gn_g[0], zero, zero)
            ob_s, _, _ = _retention(p, NP_TOK, DEC_BATCH, DEC_SEQ, ret_decay_fwd[0], ret_decay_bwd[0], ret_gn_g[0],
                                    _blockdiag_states(state_ret_fwd[:, 0]), _blockdiag_states(state_ret_bwd[:, 0]))
            w_out = even_w_out[0].astype(BF16)
            parts = [(oa_p, oa_s, w_out[:A_Q]), (ob_p, ob_s, w_out[A_Q:])]
            outs["a_k"] = p[:NP_TOK, A_Q:A_Q + A_KV].reshape(BATCH, 1, SEQ, A_KV_HEADS, HD)
            outs["a_v"] = p[:NP_TOK, A_Q + A_KV:A_Q + 2 * A_KV].reshape(BATCH, 1, SEQ, A_KV_HEADS, HD)
            outs["r_f"] = _diag_states(sf).reshape(BATCH, 1, B_HEADS, HD, HD)
            outs["r_b"] = _diag_states(sb).reshape(BATCH, 1, B_HEADS, HD, HD)
        else:
            p = _inproj(x, norm1_g[l], mod, odd_w_in[0].astype(BF16), rope, 0)
            o_p = _ctx_mha(p)
            o_s = _na_attention(p, cache_c_k[:, 0], cache_c_v[:, 0], na_rpb[0])
            parts = [(o_p, o_s, odd_w_out[0].astype(BF16))]
            outs["c_k"] = p[:NP_TOK, C_W:2 * C_W].reshape(BATCH, 1, SEQ, C_HEADS, HD)
            outs["c_v"] = p[:NP_TOK, 2 * C_W:3 * C_W].reshape(BATCH, 1, SEQ, C_HEADS, HD)
        x, h, idx_t, gate_t, rank_t, counts = _outproj(x, parts, mod, norm2_g[l], router_w[l], router_b[l])
        row_tok, block_e, n_used, dest = _dispatch_plan(idx_t, rank_t, counts[:, 0])
        y = _experts(block_e, n_used, _sc_gather(h, row_tok), exp_w1[l], exp_w3[l], exp_w2[l])
        y_rows = _sc_gather(y, dest.reshape(N_ASSIGN)).reshape(TOP_K, N_TOK, D)
        x = _combine(x, h, y_rows, gate_t.T, mod, sh_w1[l], sh_w3[l], sh_w2[l], final_g, final=(l == 1))
    y_prompt = x[:NP_TOK].reshape(BATCH, SEQ, D)
    y_sample = x[NP_TOK:].reshape(DEC_BATCH, DEC_SEQ, D)
    return (y_prompt, y_sample, outs["a_k"], outs["a_v"], outs["r_f"], outs["r_b"], outs["c_k"], outs["c_v"])
```

```python
import functools
import math

import jax
import jax.numpy as jnp
from jax import lax
from jax.experimental import pallas as pl
from jax.experimental.pallas import tpu as pltpu
from jax.experimental.pallas import tpu_sc as plsc

F32 = jnp.float32
BF16 = jnp.bfloat16
HIGHEST = lax.Precision.HIGHEST

D = 1024
BATCH = 32
SEQ = 256
DEC_BATCH = 4
DEC_SEQ = 4096
PAST = 256
GRID_W = 64
HD = 64
EPS = 1e-6
NEG = -1e30
ROPE_BASE = 10000.0
A_HEADS = 8
A_KV_HEADS = 2
A_Q = A_HEADS * HD
A_KV = A_KV_HEADS * HD
B_HEADS = 8
B_W = B_HEADS * HD
EVEN_IN = A_Q + 2 * A_KV + 4 * B_W
C_HEADS = 16
C_W = C_HEADS * HD
NA_KH = 8
NA_KW = 16
N_EXPERTS = 64
TOP_K = 8
N_GROUPS = 8
TOPK_GROUPS = 4
FF = 256
ROUTED_SCALE = 2.5
MOE_BLOCK = 512
RET_CHUNK = 128
A_BLOCK = 128

NP_TOK = BATCH * SEQ
NS_TOK = DEC_BATCH * DEC_SEQ
N_TOK = NP_TOK + NS_TOK
N_ASSIGN = N_TOK * TOP_K
N_MOE_BLOCKS = (N_ASSIGN + N_EXPERTS * (MOE_BLOCK - 1) + MOE_BLOCK - 1) // MOE_BLOCK

LANES = 128
TM = 512
NA_ROWS = 8
V7X_VMEM_LIMIT = 56 * 1024 * 1024

NT_DIMS = (((1,), (1,)), ((), ()))


def _cparams(n_axes, vmem=V7X_VMEM_LIMIT):
    return pltpu.CompilerParams(dimension_semantics=("arbitrary",) * n_axes, vmem_limit_bytes=vmem)


def _seg_of_block(i, rows):
    row0 = i * rows
    return jnp.where(row0 < NP_TOK, 0, 1 + (row0 - NP_TOK) // DEC_SEQ)


def _mod_spec(chunk, rows=TM):
    return pl.BlockSpec((None, 1, D), lambda i: (_seg_of_block(i, rows), 0, chunk))


def _silu(x):
    return x * jax.nn.sigmoid(x)


def _rms(x, g):
    return x * lax.rsqrt(jnp.mean(x * x, axis=-1, keepdims=True) + EPS) * g


def _lane_lo():
    return lax.broadcasted_iota(jnp.int32, (1, LANES), 1) < HD


def _ada_kernel(c_ref, w_ref, b_ref, o_ref):
    a = _silu(c_ref[...])
    o_ref[...] = jnp.dot(a, w_ref[...], preferred_element_type=F32, precision=HIGHEST) + b_ref[...]


def _ada(cc, w, b, layer):
    tn = 1536
    out = pl.pallas_call(
        _ada_kernel,
        grid=(6 * D // tn,),
        in_specs=[pl.BlockSpec((8, D), lambda j: (0, 0)),
                  pl.BlockSpec((None, D, tn), lambda j: (layer, 0, j)),
                  pl.BlockSpec((None, 1, tn), lambda j: (layer, 0, j))],
        out_specs=pl.BlockSpec((8, tn), lambda j: (0, j)),
        out_shape=jax.ShapeDtypeStruct((8, 6 * D), F32),
        compiler_params=_cparams(1),
        name="ada",
    )(cc, w, b.reshape(b.shape[0], 1, 6 * D))
    return out.reshape(8, 1, 6 * D)


def _inproj_kernel(x_ref, g_ref, shift_ref, scale_ref, w_ref, cos_ref, sin_ref, o_ref, *, rope_cols):
    h = _rms(x_ref[...], g_ref[...]) * (1.0 + scale_ref[...]) + shift_ref[...]
    o = jnp.dot(h.astype(BF16), w_ref[...], preferred_element_type=F32)
    if rope_cols:
        cos = cos_ref[...]
        sin = sin_ref[...]
        lane = lax.broadcasted_iota(jnp.int32, (1, LANES), 1)
        first = (lane % 32) < 16
        for c in range(rope_cols // LANES):
            oc = o[:, c * LANES:(c + 1) * LANES]
            partner = jnp.where(first, pltpu.roll(oc, LANES - 16, 1), pltpu.roll(oc, 16, 1))
            o_ref[:, c * LANES:(c + 1) * LANES] = oc * cos + partner * sin
        o_ref[:, rope_cols:] = o[:, rope_cols:]
    else:
        o_ref[...] = o


def _rope_tables():
    half = HD // 2
    inv = ROPE_BASE ** (-jnp.arange(0, half, 2, dtype=F32) / half)
    t = jnp.arange(DEC_SEQ)
    ang_r = (t // GRID_W).astype(F32)[:, None] * inv[None]
    ang_c = (t % GRID_W).astype(F32)[:, None] * inv[None]

    def head(fn_r, fn_c, sign):
        return jnp.concatenate([sign[0] * fn_r, sign[1] * fn_r, sign[0] * fn_c, sign[1] * fn_c], axis=-1)

    cos = head(jnp.cos(ang_r), jnp.cos(ang_c), (1.0, 1.0))
    sin = head(jnp.sin(ang_r), jnp.sin(ang_c), (-1.0, 1.0))
    cos = jnp.concatenate([jnp.ones((TM, HD), F32), cos], axis=0)
    sin = jnp.concatenate([jnp.zeros((TM, HD), F32), sin], axis=0)
    return jnp.tile(cos, (1, 2)), jnp.tile(sin, (1, 2))


def _inproj(x, g, mod, w_bf16, rope, rope_cols):
    n_out = w_bf16.shape[1]
    npb = NP_TOK // TM
    spb = DEC_SEQ // TM

    def rope_map(i):
        return (jnp.where(i < npb, 0, 1 + (i - npb) % spb), 0)

    return pl.pallas_call(
        functools.partial(_inproj_kernel, rope_cols=rope_cols),
        grid=(N_TOK // TM,),
        in_specs=[pl.BlockSpec((TM, D), lambda i: (i, 0)),
                  pl.BlockSpec((1, D), lambda i: (0, 0)),
                  _mod_spec(0), _mod_spec(1),
                  pl.BlockSpec((D, n_out), lambda i: (0, 0)),
                  pl.BlockSpec((TM, LANES), rope_map),
                  pl.BlockSpec((TM, LANES), rope_map)],
        out_specs=pl.BlockSpec((TM, n_out), lambda i: (i, 0)),
        out_shape=jax.ShapeDtypeStruct((N_TOK, n_out), F32),
        compiler_params=_cparams(1),
        name="inproj",
    )(x, g.reshape(1, D), mod, mod, w_bf16, rope[0], rope[1])


def _softmax_av(s_list, v_list, sink=None):
    mx = s_list[0].max(axis=-1, keepdims=True)
    for s in s_list[1:]:
        mx = jnp.maximum(mx, s.max(axis=-1, keepdims=True))
    if sink is not None:
        mx = jnp.maximum(mx, sink)
    den = jnp.exp(sink - mx) if sink is not None else 0.0
    acc = None
    for s, v in zip(s_list, v_list):
        p = jnp.exp(s - mx)
        den = den + p.sum(axis=-1, keepdims=True)
        pv = jnp.dot(p.astype(BF16), v, preferred_element_type=F32)
        acc = pv if acc is None else acc + pv
    return acc / den


def _dup_half(x, j, lo):
    xr = pltpu.roll(x, HD, 1)
    return jnp.where(lo, x, xr) if j == 0 else jnp.where(lo, xr, x)


def _stack_heads(q_ref, heads, lo, scale):
    parts = []
    for h in heads:
        qp = q_ref[:, (h // 2) * LANES:(h // 2 + 1) * LANES]
        keep = lo if h % 2 == 0 else jnp.logical_not(lo)
        parts.append(jnp.where(keep, qp, 0.0) * scale)
    return jnp.concatenate(parts, axis=0).astype(BF16)


def _sink_column(sink_ref, heads, rows):
    return jnp.concatenate([jnp.full((rows, 1), sink_ref[h], F32) for h in heads], axis=0)


def _ctx_gqa_kernel(sink_ref, q_ref, k_ref, v_ref, o_ref):
    lo = _lane_lo()
    k = k_ref[...]
    v = v_ref[...]
    group = A_HEADS // A_KV_HEADS
    for j in range(A_KV_HEADS):
        heads = list(range(group * j, group * (j + 1)))
        kd = _dup_half(k, j, lo).astype(BF16)
        vd = _dup_half(v, j, lo).astype(BF16)
        q = _stack_heads(q_ref, heads, lo, HD ** -0.5)
        s = lax.dot_general(q, kd, NT_DIMS, preferred_element_type=F32)
        o = _softmax_av([s], [vd], _sink_column(sink_ref, heads, SEQ))
        for t in range(group // 2):
            pair = heads[2 * t] // 2
            o_ref[:, pair * LANES:(pair + 1) * LANES] = jnp.where(
                lo, o[(2 * t) * SEQ:(2 * t + 1) * SEQ], o[(2 * t + 1) * SEQ:(2 * t + 2) * SEQ])


def _ctx_gqa(p, sink):
    return pl.pallas_call(
        _ctx_gqa_kernel,
        grid_spec=pltpu.PrefetchScalarGridSpec(
            num_scalar_prefetch=1,
            grid=(BATCH,),
            in_specs=[pl.BlockSpec((SEQ, A_Q), lambda b, s: (b, 0)),
                      pl.BlockSpec((SEQ, A_KV), lambda b, s: (b, A_Q // A_KV)),
                      pl.BlockSpec((SEQ, A_KV), lambda b, s: (b, A_Q // A_KV + 1))],
            out_specs=pl.BlockSpec((SEQ, A_Q), lambda b, s: (b, 0))),
        out_shape=jax.ShapeDtypeStruct((NP_TOK, A_Q), F32),
        compiler_params=_cparams(1),
        name="ctx_gqa",
    )(sink, p, p, p)


def _win_kernel(sink_ref, q_ref, kp_ref, kc_ref, kn_ref, vp_ref, vc_ref, vn_ref, ck_ref, cv_ref, o_ref):
    i = pl.program_id(1)
    lo = _lane_lo()
    k = jnp.concatenate([kp_ref[...], kc_ref[...], kn_ref[...]], axis=0)
    v = jnp.concatenate([vp_ref[...], vc_ref[...], vn_ref[...]], axis=0)
    ck = ck_ref[...]
    cv = cv_ref[...]
    group = A_HEADS // A_KV_HEADS
    qpos = i * A_BLOCK + lax.broadcasted_iota(jnp.int32, (A_BLOCK, 3 * A_BLOCK), 0)
    kpos = (i - 1) * A_BLOCK + lax.broadcasted_iota(jnp.int32, (A_BLOCK, 3 * A_BLOCK), 1)
    valid = (jnp.abs(kpos - qpos) <= A_BLOCK) & (kpos >= 0) & (kpos < DEC_SEQ)
    valid = jnp.concatenate([valid] * group, axis=0)
    for j in range(A_KV_HEADS):
        heads = list(range(group * j, group * (j + 1)))
        kd = _dup_half(k, j, lo).astype(BF16)
        vd = _dup_half(v, j, lo).astype(BF16)
        ckd = _dup_half(ck, j, lo).astype(BF16)
        cvd = _dup_half(cv, j, lo).astype(BF16)
        q = _stack_heads(q_ref, heads, lo, HD ** -0.5)
        s_loc = jnp.where(valid, lax.dot_general(q, kd, NT_DIMS, preferred_element_type=F32), NEG)
        s_ctx = lax.dot_general(q, ckd, NT_DIMS, preferred_element_type=F32)
        o = _softmax_av([s_loc, s_ctx], [vd, cvd], _sink_column(sink_ref, heads, A_BLOCK))
        for t in range(group // 2):
            pair = heads[2 * t] // 2
            o_ref[:, pair * LANES:(pair + 1) * LANES] = jnp.where(
                lo, o[(2 * t) * A_BLOCK:(2 * t + 1) * A_BLOCK], o[(2 * t + 1) * A_BLOCK:(2 * t + 2) * A_BLOCK])


def _win_attention(p, cache_k, cache_v, sink):
    nblk = DEC_SEQ // A_BLOCK
    base = NP_TOK // A_BLOCK
    kcol = A_Q // A_KV

    def kv_spec(col, off):
        return pl.BlockSpec((A_BLOCK, A_KV),
                            lambda b, i, s: (base + b * nblk + jnp.clip(i + off, 0, nblk - 1), col))

    ctx_spec = pl.BlockSpec((None, PAST, A_KV), lambda b, i, s: (b, 0, 0))
    return pl.pallas_call(
        _win_kernel,
        grid_spec=pltpu.PrefetchScalarGridSpec(
            num_scalar_prefetch=1,
            grid=(DEC_BATCH, nblk),
            in_specs=[pl.BlockSpec((A_BLOCK, A_Q), lambda b, i, s: (base + b * nblk + i, 0)),
                      kv_spec(kcol, -1), kv_spec(kcol, 0), kv_spec(kcol, 1),
                      kv_spec(kcol + 1, -1), kv_spec(kcol + 1, 0), kv_spec(kcol + 1, 1),
                      ctx_spec, ctx_spec],
            out_specs=pl.BlockSpec((A_BLOCK, A_Q), lambda b, i, s: (b * nblk + i, 0))),
        out_shape=jax.ShapeDtypeStruct((NS_TOK, A_Q), F32),
        compiler_params=_cparams(2),
        name="win_attn",
    )(sink, p, p, p, p, p, p, p, cache_k.reshape(DEC_BATCH, PAST, A_KV), cache_v.reshape(DEC_BATCH, PAST, A_KV))


def _ret_kernel(df_ref, db_ref, q_ref, k_ref, v_ref, g_ref, gn_ref, s0f_ref, s0b_ref,
                o_ref, sf_ref, sb_ref, of_scr, *, length):
    c_len = RET_CHUNK
    n = length // c_len
    lo = _lane_lo()
    hi = jnp.logical_not(lo)
    row = lax.broadcasted_iota(jnp.int32, (c_len, c_len), 0)
    col = lax.broadcasted_iota(jnp.int32, (c_len, c_len), 1)
    rowp = lax.broadcasted_iota(jnp.int32, (LANES, LANES), 0)
    colp = lax.broadcasted_iota(jnp.int32, (LANES, LANES), 1)
    blockdiag = (rowp < HD) == (colp < HD)
    idx = lax.broadcasted_iota(jnp.int32, (c_len, 1), 0).astype(F32)

    def direction(dec_ref, forward):
        lg = -jnp.exp(dec_ref[...])
        diff = (row - col) if forward else (col - row)
        keep = (diff >= 0) if forward else (diff > 0)
        dist = jnp.maximum(diff, 0).astype(F32)
        dm = [jnp.where(keep, jnp.exp(dist * lg[:, off:off + 1]), 0.0) for off in (0, HD)]
        if forward:
            xi = jnp.exp((idx + 1.0) * lg)
            zeta = jnp.exp((c_len - 1.0 - idx) * lg)
        else:
            xi = jnp.exp((c_len - idx) * lg)
            zeta = jnp.exp(idx * lg)
        return dm, xi, zeta, jnp.exp(c_len * lg)

    def chunk(c, state, consts):
        dm, xi, zeta, gch = consts
        rows = pl.ds(pl.multiple_of(c * c_len, c_len), c_len)
        qc = q_ref[rows, :]
        kc = k_ref[rows, :] * HD ** -0.5
        vc = v_ref[rows, :].astype(BF16)
        kb = kc.astype(BF16)
        outs = []
        for half, keep in enumerate((lo, hi)):
            qh = jnp.where(keep, qc, 0.0).astype(BF16)
            inner = lax.dot_general(qh, kb, NT_DIMS, preferred_element_type=F32) * dm[half]
            outs.append(jnp.dot(inner.astype(BF16), vc, preferred_element_type=F32))
        cross = jnp.dot(qc.astype(BF16), state.astype(BF16), preferred_element_type=F32) * xi
        o = jnp.where(lo, outs[0], outs[1]) + cross
        kz_t = (kc * zeta).T.astype(BF16)
        upd = jnp.dot(kz_t, vc, preferred_element_type=F32)
        state = gch * state + jnp.where(blockdiag, upd, 0.0)
        return rows, o, state

    cf = direction(df_ref, True)

    def fwd_body(c, state):
        rows, o, state = chunk(c, state, cf)
        of_scr[rows, :] = o
        return state

    sf_ref[...] = lax.fori_loop(0, n, fwd_body, s0f_ref[...])

    cb = direction(db_ref, False)
    gn = gn_ref[...]

    def bwd_body(t, state):
        rows, o, state = chunk(n - 1 - t, state, cb)
        o = o + of_scr[rows, :]

        def per_head(x):
            a = jnp.where(lo, x, 0.0).sum(axis=-1, keepdims=True)
            b = jnp.where(hi, x, 0.0).sum(axis=-1, keepdims=True)
            return jnp.where(lo, a, b) * (1.0 / HD)

        d = o - per_head(o)
        y = d * lax.rsqrt(per_head(d * d) + EPS) * gn
        o_ref[rows, :] = _silu(g_ref[rows, :]) * y
        return state

    sb_ref[...] = lax.fori_loop(0, n, bwd_body, s0b_ref[...])


def _pair_lanes(v):
    return jnp.repeat(v.astype(F32), HD).reshape(B_HEADS // 2, 1, LANES)


def _blockdiag_states(s):
    b = s.shape[0]
    s = s.astype(F32).reshape(b, B_HEADS // 2, 2, HD, HD)
    z = jnp.zeros_like(s[:, :, 0])
    top = jnp.concatenate([s[:, :, 0], z], axis=-1)
    bot = jnp.concatenate([z, s[:, :, 1]], axis=-1)
    return jnp.concatenate([top, bot], axis=-2)


def _diag_states(sp):
    b = sp.shape[0]
    s = jnp.stack([sp[:, :, :HD, :HD], sp[:, :, HD:, HD:]], axis=2)
    return s.reshape(b, B_HEADS, HD, HD)


def _retention(p, row_base, batch, length, dec_f, dec_b, gn_g, s0f, s0b):
    npairs = B_HEADS // 2
    blk0 = row_base // length
    qcol = (A_Q + 2 * A_KV) // LANES

    def col_spec(off):
        return pl.BlockSpec((length, LANES), lambda b, h: (blk0 + b, qcol + off * npairs + h))

    lane_spec = pl.BlockSpec((None, 1, LANES), lambda b, h: (h, 0, 0))
    state_spec = pl.BlockSpec((None, None, LANES, LANES), lambda b, h: (b, h, 0, 0))
    state_shape = jax.ShapeDtypeStruct((batch, npairs, LANES, LANES), F32)
    return pl.pallas_call(
        functools.partial(_ret_kernel, length=length),
        grid=(batch, npairs),
        in_specs=[lane_spec, lane_spec, col_spec(0), col_spec(1), col_spec(2), col_spec(3), lane_spec,
                  state_spec, state_spec],
        out_specs=[pl.BlockSpec((length, LANES), lambda b, h: (b, h)), state_spec, state_spec],
        out_shape=[jax.ShapeDtypeStruct((batch * length, B_W), F32), state_shape, state_shape],
        scratch_shapes=[pltpu.VMEM((length, LANES), F32)],
        compiler_params=_cparams(2),
        name="retention",
    )(_pair_lanes(dec_f), _pair_lanes(dec_b), p, p, p, p, gn_g.reshape(npairs, 1, LANES), s0f, s0b)


def _ctx_mha_kernel(q_ref, k_ref, v_ref, o_ref):
    lo = _lane_lo()
    for pair in range(C_HEADS // 2):
        cols = slice(pair * LANES, (pair + 1) * LANES)
        q = _stack_heads(q_ref, [2 * pair, 2 * pair + 1], lo, HD ** -0.5)
        s = lax.dot_general(q, k_ref[:, cols].astype(BF16), NT_DIMS, preferred_element_type=F32)
        o = _softmax_av([s], [v_ref[:, cols].astype(BF16)])
        o_ref[:, cols] = jnp.where(lo, o[:SEQ], o[SEQ:])


def _ctx_mha(p):
    return pl.pallas_call(
        _ctx_mha_kernel,
        grid=(BATCH,),
        in_specs=[pl.BlockSpec((SEQ, C_W), lambda b: (b, 0)),
                  pl.BlockSpec((SEQ, C_W), lambda b: (b, 1)),
                  pl.BlockSpec((SEQ, C_W), lambda b: (b, 2))],
        out_specs=pl.BlockSpec((SEQ, C_W), lambda b: (b, 0)),
        out_shape=jax.ShapeDtypeStruct((NP_TOK, C_W), F32),
        compiler_params=_cparams(1),
        name="ctx_mha",
    )(p, p, p)


NA_WIN_ROWS = 2 * NA_ROWS
NA_WIN = NA_WIN_ROWS * GRID_W
NA_QROWS = NA_ROWS * GRID_W
NA_PAD_ROWS = NA_KH // 2
NA_TABLE = 1536


def _na_kernel(q_ref, kp_ref, km_ref, kn_ref, vp_ref, vm_ref, vn_ref, ck_ref, cv_ref, ue_ref, uo_ref, o_ref):
    r0 = pl.program_id(2) * NA_ROWS
    n_rows = DEC_SEQ // GRID_W
    lo = _lane_lo()
    k = jnp.concatenate([kp_ref[...], km_ref[...], kn_ref[...]], axis=0).astype(BF16)
    v = jnp.concatenate([vp_ref[...], vm_ref[...], vn_ref[...]], axis=0).astype(BF16)
    ck = ck_ref[...].astype(BF16)
    cv = cv_ref[...].astype(BF16)
    q = q_ref[...] * HD ** -0.5
    klane = lax.broadcasted_iota(jnp.int32, (1, NA_WIN), 1)
    outs = []
    for half, keep in enumerate((lo, jnp.logical_not(lo))):
        qh = jnp.where(keep, q, 0.0).astype(BF16)
        s = lax.dot_general(qh, k, NT_DIMS, preferred_element_type=F32)
        s_ctx = lax.dot_general(qh, ck, NT_DIMS, preferred_element_type=F32)
        pieces = []
        for rq in range(NA_ROWS):
            start = NA_KH - 1 - rq
            if start % 2 == 0:
                u = ue_ref[half, :, start * GRID_W:start * GRID_W + NA_WIN]
            else:
                u = uo_ref[half, :, (start - 1) * GRID_W:(start - 1) * GRID_W + NA_WIN]
            r = r0 + rq
            first = jnp.clip(r - NA_KH // 2, 0, n_rows - NA_KH)
            lane0 = (first - r0 + NA_PAD_ROWS) * GRID_W
            in_rows = (klane >= lane0) & (klane < lane0 + NA_KH * GRID_W)
            pieces.append(jnp.where(in_rows, s[rq * GRID_W:(rq + 1) * GRID_W] + u, NEG))
        outs.append(_softmax_av([jnp.concatenate(pieces, axis=0), s_ctx], [v, cv]))
    o_ref[...] = jnp.where(lo, outs[0], outs[1])


def _na_bias_tables(rpb):
    cq = jnp.arange(GRID_W)
    ck = jnp.arange(GRID_W)
    dc = jnp.clip(ck[None] - cq[:, None], -(NA_KW - 1), NA_KW - 1) + NA_KW - 1
    cs = jnp.clip(cq - NA_KW // 2, 0, GRID_W - NA_KW)
    col_ok = (ck[None] >= cs[:, None]) & (ck[None] < cs[:, None] + NA_KW)
    t = rpb.astype(F32)[:, :, dc]
    t = jnp.where(col_ok[None, None], t, NEG).transpose(0, 2, 1, 3)
    n_dr = 2 * NA_KH - 1
    blocks = NA_TABLE // GRID_W
    t = jnp.pad(t, ((0, 0), (0, 0), (NA_PAD_ROWS, blocks - n_dr - NA_PAD_ROWS), (0, 0)), constant_values=NEG)
    ue = t.reshape(C_HEADS, GRID_W, NA_TABLE)
    uo = jnp.concatenate([ue[..., GRID_W:], jnp.full((C_HEADS, GRID_W, GRID_W), NEG, F32)], axis=-1)
    return ue, uo


def _na_attention(p, cache_k, cache_v, rpb):
    npairs = C_HEADS // 2
    nrb = DEC_SEQ // NA_QROWS
    half = NA_QROWS // 2
    qbase = NP_TOK // NA_QROWS
    hbase = NP_TOK // half
    kcol = C_W // LANES
    ue, uo = _na_bias_tables(rpb)

    def main_spec(col0):
        return pl.BlockSpec((NA_QROWS, LANES), lambda b, h, r: (qbase + b * nrb + r, col0 + h))

    def side_spec(col0, off):
        return pl.BlockSpec((half, LANES),
                            lambda b, h, r: (hbase + b * 2 * nrb + jnp.clip(2 * r + off, 0, 2 * nrb - 1), col0 + h))

    ctx_spec = pl.BlockSpec((None, PAST, LANES), lambda b, h, r: (b, 0, h))
    tab_spec = pl.BlockSpec((2, GRID_W, NA_TABLE), lambda b, h, r: (h, 0, 0))
    return pl.pallas_call(
        _na_kernel,
        grid=(DEC_BATCH, npairs, nrb),
        in_specs=[main_spec(0),
                  side_spec(kcol, -1), main_spec(kcol), side_spec(kcol, 2),
                  side_spec(2 * kcol, -1), main_spec(2 * kcol), side_spec(2 * kcol, 2),
                  ctx_spec, ctx_spec, tab_spec, tab_spec],
        out_specs=pl.BlockSpec((NA_QROWS, LANES), lambda b, h, r: (b * nrb + r, h)),
        out_shape=jax.ShapeDtypeStruct((NS_TOK, C_W), F32),
        compiler_params=_cparams(3),
        name="na_attn",
    )(p, p, p, p, p, p, p, cache_k.reshape(DEC_BATCH, PAST, C_W), cache_v.reshape(DEC_BATCH, PAST, C_W), ue, uo)


def _route(biased, scores):
    t = biased.shape[1]
    per_group = N_EXPERTS // N_GROUPS
    i8 = lax.broadcasted_iota(jnp.int32, (per_group, t), 0)
    g_rows = []
    for g in range(N_GROUPS):
        bg = biased[g * per_group:(g + 1) * per_group]
        m1 = bg.max(axis=0, keepdims=True)
        first = jnp.where(bg == m1, i8, per_group).min(axis=0, keepdims=True)
        m2 = jnp.where(i8 == first, -jnp.inf, bg).max(axis=0, keepdims=True)
        g_rows.append(m1 + m2)
    g_top = jnp.concatenate(g_rows, axis=0)
    gi = lax.broadcasted_iota(jnp.int32, g_top.shape, 0)
    g_sel = jnp.zeros(g_top.shape, jnp.int32)
    cur = g_top
    for _ in range(TOPK_GROUPS):
        m = cur.max(axis=0, keepdims=True)
        hit = gi == jnp.where(cur == m, gi, N_GROUPS).min(axis=0, keepdims=True)
        g_sel = jnp.where(hit, 1, g_sel)
        cur = jnp.where(hit, -jnp.inf, cur)
    e_sel = jnp.concatenate([jnp.broadcast_to(g_sel[g:g + 1], (per_group, t)) for g in range(N_GROUPS)], axis=0)
    cur = jnp.where(e_sel > 0, biased, NEG)
    ei = lax.broadcasted_iota(jnp.int32, cur.shape, 0)
    ids, gates, hits = [], [], []
    for _ in range(TOP_K):
        m = cur.max(axis=0, keepdims=True)
        f = jnp.where(cur == m, ei, N_EXPERTS).min(axis=0, keepdims=True)
        hit = ei == f
        ids.append(f)
        hits.append(hit)
        gates.append(jnp.where(hit, scores, 0.0).sum(axis=0, keepdims=True))
        cur = jnp.where(hit, -jnp.inf, cur)
    gate = jnp.concatenate(gates, axis=0)
    gate = gate / gate.sum(axis=0, keepdims=True) * ROUTED_SCALE
    return jnp.concatenate(ids, axis=0), gate, hits


def _pack_bf16_pairs(h):
    bits = lax.bitcast_convert_type(h.astype(BF16).astype(F32), jnp.uint32)
    return bits[:, :D // 2] | (bits[:, D // 2:] >> 16)


def _unpack_bf16_pairs(xp):
    hi = lax.bitcast_convert_type(xp & jnp.uint32(0xFFFF0000), F32).astype(BF16)
    lo = lax.bitcast_convert_type(xp << 16, F32).astype(BF16)
    return hi, lo


def _dot_halves(hi, lo, w_ref):
    return (jnp.dot(hi, w_ref[:D // 2, :], preferred_element_type=F32)
            + jnp.dot(lo, w_ref[D // 2:, :], preferred_element_type=F32))


def _outproj_kernel(*refs, n_parts):
    x_ref = refs[0]
    part_refs = refs[1:1 + 3 * n_parts]
    gate_ref, shift_ref, scale_ref, g2_ref, rw_ref, rb_ref = refs[1 + 3 * n_parts:7 + 3 * n_parts]
    xo_ref, h_ref, idx_ref, wgt_ref, rank_ref, cnt_ref = refs[7 + 3 * n_parts:]
    is_prompt = pl.program_id(0) < NP_TOK // TM

    @pl.when(pl.program_id(0) == 0)
    def _():
        cnt_ref[...] = jnp.zeros_like(cnt_ref)

    y = None
    for t in range(n_parts):
        ap_ref, as_ref, w_ref = part_refs[3 * t:3 * t + 3]
        a = jnp.where(is_prompt, ap_ref[...], as_ref[...]).astype(BF16)
        d = jnp.dot(a, w_ref[...], preferred_element_type=F32)
        y = d if y is None else y + d
    x = x_ref[...] + gate_ref[...] * y
    xo_ref[...] = x
    h = _rms(x, g2_ref[...]) * (1.0 + scale_ref[...]) + shift_ref[...]
    h_ref[...] = _pack_bf16_pairs(h)
    logits = lax.dot_general(rw_ref[...], h, NT_DIMS, preferred_element_type=F32, precision=HIGHEST)
    scores = jax.nn.sigmoid(logits)
    idx, gate, hits = _route(scores + rb_ref[...], scores)
    idx_ref[...] = idx
    wgt_ref[...] = gate
    chosen = hits[0]
    for hit in hits[1:]:
        chosen = chosen | hit
    m = jnp.where(chosen, 1.0, 0.0)
    before = (lax.broadcasted_iota(jnp.int32, (TM, TM), 0) < lax.broadcasted_iota(jnp.int32, (TM, TM), 1))
    prefix = jnp.dot(m.astype(BF16), jnp.where(before, 1.0, 0.0).astype(BF16), preferred_element_type=F32)
    rank_all = prefix + cnt_ref[...]
    rank_ref[...] = jnp.concatenate(
        [jnp.where(hit, rank_all, 0.0).sum(axis=0, keepdims=True) for hit in hits], axis=0).astype(jnp.int32)
    cnt_ref[...] += m.sum(axis=1, keepdims=True)


def _outproj(x, parts, mod, g2, router_w, router_b):
    npb = NP_TOK // TM
    nsb = NS_TOK // TM
    in_specs = [pl.BlockSpec((TM, D), lambda i: (i, 0))]
    args = [x]
    for ap, a_s, w in parts:
        width = ap.shape[1]
        in_specs += [pl.BlockSpec((TM, width), lambda i: (jnp.minimum(i, npb - 1), 0)),
                     pl.BlockSpec((TM, width), lambda i: (jnp.clip(i - npb, 0, nsb - 1), 0)),
                     pl.BlockSpec((width, D), lambda i: (0, 0))]
        args += [ap, a_s, w]
    in_specs += [_mod_spec(2), _mod_spec(3), _mod_spec(4),
                 pl.BlockSpec((1, D), lambda i: (0, 0)),
                 pl.BlockSpec((N_EXPERTS, D), lambda i: (0, 0)),
                 pl.BlockSpec((N_EXPERTS, 1), lambda i: (0, 0))]
    args += [mod, mod, mod, g2.reshape(1, D), router_w.T, router_b.reshape(N_EXPERTS, 1)]
    return pl.pallas_call(
        functools.partial(_outproj_kernel, n_parts=len(parts)),
        grid=(N_TOK // TM,),
        in_specs=in_specs,
        out_specs=[pl.BlockSpec((TM, D), lambda i: (i, 0)),
                   pl.BlockSpec((TM, D // 2), lambda i: (i, 0)),
                   pl.BlockSpec((TOP_K, TM), lambda i: (0, i)),
                   pl.BlockSpec((TOP_K, TM), lambda i: (0, i)),
                   pl.BlockSpec((TOP_K, TM), lambda i: (0, i)),
                   pl.BlockSpec((N_EXPERTS, 1), lambda i: (0, 0))],
        out_shape=[jax.ShapeDtypeStruct((N_TOK, D), F32),
                   jax.ShapeDtypeStruct((N_TOK, D // 2), jnp.uint32),
                   jax.ShapeDtypeStruct((TOP_K, N_TOK), jnp.int32),
                   jax.ShapeDtypeStruct((TOP_K, N_TOK), F32),
                   jax.ShapeDtypeStruct((TOP_K, N_TOK), jnp.int32),
                   jax.ShapeDtypeStruct((N_EXPERTS, 1), F32)],
        compiler_params=_cparams(1),
        name="outproj_router",
    )(*args)


def _experts_kernel(be_ref, nu_ref, x_ref, w1_ref, w3_ref, w2_ref, o_ref, w1b, w3b, w2b):
    i = pl.program_id(0)
    e = be_ref[i]
    prev = be_ref[jnp.maximum(i - 1, 0)]

    @pl.when((i == 0) | (e != prev))
    def _():
        w1b[...] = w1_ref[...].astype(BF16)
        w3b[...] = w3_ref[...].astype(BF16)
        w2b[...] = w2_ref[...].astype(BF16)

    @pl.when(i < nu_ref[0])
    def _():
        hi, lo = _unpack_bf16_pairs(x_ref[...])
        a = _dot_halves(hi, lo, w1b)
        b = _dot_halves(hi, lo, w3b)
        h = (_silu(a) * b).astype(BF16)
        o_ref[...] = jnp.dot(h, w2b[...], preferred_element_type=F32)

    @pl.when(i >= nu_ref[0])
    def _():
        o_ref[...] = jnp.zeros_like(o_ref)


def _experts(block_e, n_used, x_sorted, w1, w3, w2, layer):
    return pl.pallas_call(
        _experts_kernel,
        grid_spec=pltpu.PrefetchScalarGridSpec(
            num_scalar_prefetch=2,
            grid=(N_MOE_BLOCKS,),
            in_specs=[pl.BlockSpec((MOE_BLOCK, D // 2), lambda i, be, nu: (i, 0)),
                      pl.BlockSpec((None, None, D, FF), lambda i, be, nu: (layer, be[i], 0, 0)),
                      pl.BlockSpec((None, None, D, FF), lambda i, be, nu: (layer, be[i], 0, 0)),
                      pl.BlockSpec((None, None, FF, D), lambda i, be, nu: (layer, be[i], 0, 0))],
            out_specs=pl.BlockSpec((MOE_BLOCK, D), lambda i, be, nu: (i, 0)),
            scratch_shapes=[pltpu.VMEM((D, FF), BF16), pltpu.VMEM((D, FF), BF16), pltpu.VMEM((FF, D), BF16)]),
        out_shape=jax.ShapeDtypeStruct((N_MOE_BLOCKS * MOE_BLOCK, D), F32),
        compiler_params=_cparams(1),
        name="experts",
    )(block_e, n_used, x_sorted, w1, w3, w2)


def _dispatch_plan(idx_t, rank_t, counts):
    counts = counts.astype(jnp.int32)
    padded = (counts + MOE_BLOCK - 1) // MOE_BLOCK * MOE_BLOCK
    pad_end = jnp.cumsum(padded)
    pad_start = pad_end - padded
    start = jnp.cumsum(counts) - counts
    block_row0 = jnp.arange(N_MOE_BLOCKS, dtype=jnp.int32) * MOE_BLOCK
    block_e = jnp.minimum((pad_end[None, :] <= block_row0[:, None]).sum(axis=1), N_EXPERTS - 1).astype(jnp.int32)
    n_used = (pad_end[-1] // MOE_BLOCK).astype(jnp.int32).reshape(1)
    key = idx_t * N_TOK + jnp.arange(N_TOK, dtype=jnp.int32)[None, :]
    tok_sorted = jnp.sort(key.reshape(N_ASSIGN)) % N_TOK
    local = (block_row0 - pad_start[block_e])[:, None] + jnp.arange(MOE_BLOCK, dtype=jnp.int32)[None, :]
    valid = local < counts[block_e][:, None]
    src = jnp.clip(start[block_e][:, None] + local, 0, N_ASSIGN - 1)
    row_tok = jnp.where(valid, tok_sorted[src], 0).reshape(N_MOE_BLOCKS * MOE_BLOCK).astype(jnp.int32)
    e_iota = jnp.arange(N_EXPERTS, dtype=jnp.int32)[:, None, None]
    dest = jnp.where(idx_t[None] == e_iota, pad_start[:, None, None], 0).sum(axis=0) + rank_t
    return row_tok, block_e, n_used, dest.astype(jnp.int32)


SC_CORES = 2
SC_SUBCORES = 16
SC_WORKERS = SC_CORES * SC_SUBCORES
SC_CHUNK_BYTES = 128 * 1024


def _sc_gather(table, idx):
    n_idx = idx.shape[0]
    width = table.shape[1]
    chunk = SC_CHUNK_BYTES // (4 * width)
    per_worker = n_idx // SC_WORKERS
    n_chunks = per_worker // chunk
    assert per_worker * SC_WORKERS == n_idx and n_chunks * chunk == per_worker and n_chunks % 2 == 0
    mesh = plsc.VectorSubcoreMesh(core_axis_name="c", subcore_axis_name="s")

    @functools.partial(
        pl.kernel, mesh=mesh,
        out_type=jax.ShapeDtypeStruct((n_idx, width), table.dtype),
        scratch_types=[pltpu.VMEM((n_chunks, chunk), jnp.int32),
                       pltpu.VMEM((2, chunk, width), table.dtype),
                       pltpu.SemaphoreType.DMA((2,)),
                       pltpu.SemaphoreType.DMA((2,))])
    def gather(t_hbm, i_hbm, o_hbm, idx_v, rows_v, gsem, wsem):
        worker = lax.axis_index("s") * SC_CORES + lax.axis_index("c")
        base = worker * per_worker
        pltpu.sync_copy(i_hbm.at[worker], idx_v)

        def gather_copy(c, b):
            return pltpu.make_async_copy(t_hbm.at[idx_v.at[c]], rows_v.at[b], gsem.at[b])

        def write_copy(c, b):
            rows = pl.ds(pl.multiple_of(base + c * chunk, chunk), chunk)
            return pltpu.make_async_copy(rows_v.at[b], o_hbm.at[rows], wsem.at[b])

        gather_copy(0, 0).start()

        @pl.loop(0, n_chunks, step=2)
        def _(c):
            gather_copy(c, 0).wait()

            @pl.when(c > 0)
            def _():
                write_copy(c - 1, 1).wait()

            gather_copy(c + 1, 1).start()
            write_copy(c, 0).start()
            gather_copy(c + 1, 1).wait()
            write_copy(c, 0).wait()

            @pl.when(c + 2 < n_chunks)
            def _():
                gather_copy(c + 2, 0).start()

            write_copy(c + 1, 1).start()

        write_copy(n_chunks - 1, 1).wait()

    return gather(table, idx.reshape(SC_WORKERS, n_chunks, chunk))


TC = 256


def _combine_kernel(x_ref, h_ref, y_ref, wgt_ref, gate_ref, w1_ref, w3_ref, w2_ref, fg_ref, o_ref, *, final):
    hi, lo = _unpack_bf16_pairs(h_ref[...])
    a = _dot_halves(hi, lo, w1_ref)
    b = _dot_halves(hi, lo, w3_ref)
    ffn = jnp.dot((_silu(a) * b).astype(BF16), w2_ref[...], preferred_element_type=F32)
    wgt = wgt_ref[...]
    for k in range(TOP_K):
        ffn = ffn + y_ref[k] * wgt[:, k:k + 1]
    x = x_ref[...] + gate_ref[...] * ffn
    o_ref[...] = _rms(x, fg_ref[...]) if final else x


def _combine(x, h, y_rows, wgt, mod, sw1, sw3, sw2, final_g, final):
    row = pl.BlockSpec((TC, D), lambda i: (i, 0))
    return pl.pallas_call(
        functools.partial(_combine_kernel, final=final),
        grid=(N_TOK // TC,),
        in_specs=[row,
                  pl.BlockSpec((TC, D // 2), lambda i: (i, 0)),
                  pl.BlockSpec((TOP_K, TC, D), lambda i: (0, i, 0)),
                  pl.BlockSpec((TC, TOP_K), lambda i: (i, 0)),
                  _mod_spec(5, TC),
                  pl.BlockSpec((D, FF), lambda i: (0, 0)),
                  pl.BlockSpec((D, FF), lambda i: (0, 0)),
                  pl.BlockSpec((FF, D), lambda i: (0, 0)),
                  pl.BlockSpec((1, D), lambda i: (0, 0))],
        out_specs=row,
        out_shape=jax.ShapeDtypeStruct((N_TOK, D), F32),
        compiler_params=_cparams(1),
        name="combine",
    )(x, h, y_rows, wgt, mod, sw1.astype(BF16), sw3.astype(BF16), sw2.astype(BF16), final_g.reshape(1, D))


def kernel(x_prompt, x_sample, cache_a_k, cache_a_v, state_ret_fwd, state_ret_bwd, cache_c_k, cache_c_v,
           c, c_ctx, norm1_g, norm2_g, ada_w, ada_b, even_w_in, even_w_out, sink_a, ret_decay_fwd,
           ret_decay_bwd, ret_gn_g, odd_w_in, odd_w_out, na_rpb, router_w, router_b, exp_w1, exp_w3,
           exp_w2, sh_w1, sh_w3, sh_w2, final_g):
    x = jnp.concatenate([x_prompt.reshape(NP_TOK, D), x_sample.reshape(NS_TOK, D)], axis=0)
    cc = jnp.concatenate([c_ctx[None], c, jnp.zeros((8 - 1 - DEC_BATCH, D), F32)], axis=0)
    rope = _rope_tables()
    outs = {}
    for l in range(2):
        mod = _ada(cc, ada_w, ada_b, l)
        if l == 0:
            p = _inproj(x, norm1_g[l], mod, even_w_in[0].astype(BF16), rope, A_Q + A_KV)
            oa_p = _ctx_gqa(p, sink_a[0])
            oa_s = _win_attention(p, cache_a_k[:, 0], cache_a_v[:, 0], sink_a[0])
            zero = jnp.zeros((BATCH, B_HEADS // 2, LANES, LANES), F32)
            ob_p, sf, sb = _retention(p, 0, BATCH, SEQ, ret_decay_fwd[0], ret_decay_bwd[0], ret_gn_g[0], zero, zero)
            ob_s, _, _ = _retention(p, NP_TOK, DEC_BATCH, DEC_SEQ, ret_decay_fwd[0], ret_decay_bwd[0], ret_gn_g[0],
                                    _blockdiag_states(state_ret_fwd[:, 0]), _blockdiag_states(state_ret_bwd[:, 0]))
            w_out = even_w_out[0].astype(BF16)
            parts = [(oa_p, oa_s, w_out[:A_Q]), (ob_p, ob_s, w_out[A_Q:])]
            outs["a_k"] = p[:NP_TOK, A_Q:A_Q + A_KV].reshape(BATCH, 1, SEQ, A_KV_HEADS, HD)
            outs["a_v"] = p[:NP_TOK, A_Q + A_KV:A_Q + 2 * A_KV].reshape(BATCH, 1, SEQ, A_KV_HEADS, HD)
            outs["r_f"] = _diag_states(sf).reshape(BATCH, 1, B_HEADS, HD, HD)
            outs["r_b"] = _diag_states(sb).reshape(BATCH, 1, B_HEADS, HD, HD)
        else:
            p = _inproj(x, norm1_g[l], mod, odd_w_in[0].astype(BF16), rope, 0)
            o_p = _ctx_mha(p)
            o_s = _na_attention(p, cache_c_k[:, 0], cache_c_v[:, 0], na_rpb[0])
            parts = [(o_p, o_s, odd_w_out[0].astype(BF16))]
            outs["c_k"] = p[:NP_TOK, C_W:2 * C_W].reshape(BATCH, 1, SEQ, C_HEADS, HD)
            outs["c_v"] = p[:NP_TOK, 2 * C_W:3 * C_W].reshape(BATCH, 1, SEQ, C_HEADS, HD)
        x, h, idx_t, gate_t, rank_t, counts = _outproj(x, parts, mod, norm2_g[l], router_w[l], router_b[l])
        row_tok, block_e, n_used, dest = _dispatch_plan(idx_t, rank_t, counts[:, 0])
        y = _experts(block_e, n_used, _sc_gather(h, row_tok), exp_w1, exp_w3, exp_w2, l)
        y_rows = _sc_gather(y, dest.reshape(N_ASSIGN)).reshape(TOP_K, N_TOK, D)
        x = _combine(x, h, y_rows, gate_t.T, mod, sh_w1[l], sh_w3[l], sh_w2[l], final_g, final=(l == 1))
    y_prompt = x[:NP_TOK].reshape(BATCH, SEQ, D)
    y_sample = x[NP_TOK:].reshape(DEC_BATCH, DEC_SEQ, D)
    return (y_prompt, y_sample, outs["a_k"], outs["a_v"], outs["r_f"], outs["r_b"], outs["c_k"], outs["c_v"])
```

```python
import functools
import math

import jax
import jax.numpy as jnp
from jax import lax
from jax.experimental import pallas as pl
from jax.experimental.pallas import tpu as pltpu
from jax.experimental.pallas import tpu_sc as plsc

F32 = jnp.float32
BF16 = jnp.bfloat16
HIGHEST = lax.Precision.HIGHEST

D = 1024
BATCH = 32
SEQ = 256
DEC_BATCH = 4
DEC_SEQ = 4096
PAST = 256
GRID_W = 64
HD = 64
EPS = 1e-6
NEG = -1e30
ROPE_BASE = 10000.0
A_HEADS = 8
A_KV_HEADS = 2
A_Q = A_HEADS * HD
A_KV = A_KV_HEADS * HD
B_HEADS = 8
B_W = B_HEADS * HD
EVEN_IN = A_Q + 2 * A_KV + 4 * B_W
C_HEADS = 16
C_W = C_HEADS * HD
NA_KH = 8
NA_KW = 16
N_EXPERTS = 64
TOP_K = 8
N_GROUPS = 8
TOPK_GROUPS = 4
FF = 256
ROUTED_SCALE = 2.5
MOE_BLOCK = 512
RET_CHUNK = 128
RET_UNROLL = 2
A_BLOCK = 128

NP_TOK = BATCH * SEQ
NS_TOK = DEC_BATCH * DEC_SEQ
N_TOK = NP_TOK + NS_TOK
N_ASSIGN = N_TOK * TOP_K
N_MOE_BLOCKS = (N_ASSIGN + N_EXPERTS * (MOE_BLOCK - 1) + MOE_BLOCK - 1) // MOE_BLOCK

LANES = 128
TM = 512
NA_ROWS = 8
V7X_VMEM_LIMIT = 56 * 1024 * 1024

NT_DIMS = (((1,), (1,)), ((), ()))


def _cparams(n_axes, vmem=V7X_VMEM_LIMIT):
    return pltpu.CompilerParams(dimension_semantics=("arbitrary",) * n_axes, vmem_limit_bytes=vmem)


def _seg_of_block(i, rows):
    row0 = i * rows
    return jnp.where(row0 < NP_TOK, 0, 1 + (row0 - NP_TOK) // DEC_SEQ)


def _mod_spec(chunk, rows=TM):
    return pl.BlockSpec((None, 1, D), lambda i: (_seg_of_block(i, rows), 0, chunk))


def _silu(x):
    return x * jax.nn.sigmoid(x)


def _rms(x, g):
    return x * lax.rsqrt(jnp.mean(x * x, axis=-1, keepdims=True) + EPS) * g


def _lane_lo():
    return lax.broadcasted_iota(jnp.int32, (1, LANES), 1) < HD


def _ada_kernel(c_ref, w_ref, b_ref, o_ref):
    a = _silu(c_ref[...])
    o_ref[...] = jnp.dot(a, w_ref[...], preferred_element_type=F32, precision=HIGHEST) + b_ref[...]


def _ada(cc, w, b, layer):
    tn = 1536
    out = pl.pallas_call(
        _ada_kernel,
        grid=(6 * D // tn,),
        in_specs=[pl.BlockSpec((8, D), lambda j: (0, 0)),
                  pl.BlockSpec((None, D, tn), lambda j: (layer, 0, j)),
                  pl.BlockSpec((None, 1, tn), lambda j: (layer, 0, j))],
        out_specs=pl.BlockSpec((8, tn), lambda j: (0, j)),
        out_shape=jax.ShapeDtypeStruct((8, 6 * D), F32),
        compiler_params=_cparams(1),
        name="ada",
    )(cc, w, b.reshape(b.shape[0], 1, 6 * D))
    return out.reshape(8, 1, 6 * D)


def _inproj_kernel(x_ref, g_ref, shift_ref, scale_ref, w_ref, cos_ref, sin_ref, o_ref, *, rope_cols):
    h = _rms(x_ref[...], g_ref[...]) * (1.0 + scale_ref[...]) + shift_ref[...]
    o = jnp.dot(h.astype(BF16), w_ref[...], preferred_element_type=F32)
    if rope_cols:
        cos = cos_ref[...]
        sin = sin_ref[...]
        lane = lax.broadcasted_iota(jnp.int32, (1, LANES), 1)
        first = (lane % 32) < 16
        for c in range(rope_cols // LANES):
            oc = o[:, c * LANES:(c + 1) * LANES]
            partner = jnp.where(first, pltpu.roll(oc, LANES - 16, 1), pltpu.roll(oc, 16, 1))
            o_ref[:, c * LANES:(c + 1) * LANES] = oc * cos + partner * sin
        o_ref[:, rope_cols:] = o[:, rope_cols:]
    else:
        o_ref[...] = o


def _rope_tables():
    half = HD // 2
    inv = ROPE_BASE ** (-jnp.arange(0, half, 2, dtype=F32) / half)
    t = jnp.arange(DEC_SEQ)
    ang_r = (t // GRID_W).astype(F32)[:, None] * inv[None]
    ang_c = (t % GRID_W).astype(F32)[:, None] * inv[None]

    def head(fn_r, fn_c, sign):
        return jnp.concatenate([sign[0] * fn_r, sign[1] * fn_r, sign[0] * fn_c, sign[1] * fn_c], axis=-1)

    cos = head(jnp.cos(ang_r), jnp.cos(ang_c), (1.0, 1.0))
    sin = head(jnp.sin(ang_r), jnp.sin(ang_c), (-1.0, 1.0))
    cos = jnp.concatenate([jnp.ones((TM, HD), F32), cos], axis=0)
    sin = jnp.concatenate([jnp.zeros((TM, HD), F32), sin], axis=0)
    return jnp.tile(cos, (1, 2)), jnp.tile(sin, (1, 2))


def _inproj(x, g, mod, w_bf16, rope, rope_cols):
    n_out = w_bf16.shape[1]
    npb = NP_TOK // TM
    spb = DEC_SEQ // TM

    def rope_map(i):
        return (jnp.where(i < npb, 0, 1 + (i - npb) % spb), 0)

    return pl.pallas_call(
        functools.partial(_inproj_kernel, rope_cols=rope_cols),
        grid=(N_TOK // TM,),
        in_specs=[pl.BlockSpec((TM, D), lambda i: (i, 0)),
                  pl.BlockSpec((1, D), lambda i: (0, 0)),
                  _mod_spec(0), _mod_spec(1),
                  pl.BlockSpec((D, n_out), lambda i: (0, 0)),
                  pl.BlockSpec((TM, LANES), rope_map),
                  pl.BlockSpec((TM, LANES), rope_map)],
        out_specs=pl.BlockSpec((TM, n_out), lambda i: (i, 0)),
        out_shape=jax.ShapeDtypeStruct((N_TOK, n_out), F32),
        compiler_params=_cparams(1),
        name="inproj",
    )(x, g.reshape(1, D), mod, mod, w_bf16, rope[0], rope[1])


def _softmax_av(s_list, v_list, sink=None):
    mx = s_list[0].max(axis=-1, keepdims=True)
    for s in s_list[1:]:
        mx = jnp.maximum(mx, s.max(axis=-1, keepdims=True))
    if sink is not None:
        mx = jnp.maximum(mx, sink)
    den = jnp.exp(sink - mx) if sink is not None else 0.0
    acc = None
    for s, v in zip(s_list, v_list):
        p = jnp.exp(s - mx)
        den = den + p.sum(axis=-1, keepdims=True)
        pv = jnp.dot(p.astype(BF16), v, preferred_element_type=F32)
        acc = pv if acc is None else acc + pv
    return acc / den


def _dup_half(x, j, lo):
    xr = pltpu.roll(x, HD, 1)
    return jnp.where(lo, x, xr) if j == 0 else jnp.where(lo, xr, x)


def _stack_heads(q_ref, heads, lo, scale):
    parts = []
    for h in heads:
        qp = q_ref[:, (h // 2) * LANES:(h // 2 + 1) * LANES]
        keep = lo if h % 2 == 0 else jnp.logical_not(lo)
        parts.append(jnp.where(keep, qp, 0.0) * scale)
    return jnp.concatenate(parts, axis=0).astype(BF16)


def _sink_column(sink_ref, heads, rows):
    return jnp.concatenate([jnp.full((rows, 1), sink_ref[h], F32) for h in heads], axis=0)


def _ctx_gqa_kernel(sink_ref, q_ref, k_ref, v_ref, o_ref):
    lo = _lane_lo()
    k = k_ref[...]
    v = v_ref[...]
    group = A_HEADS // A_KV_HEADS
    for j in range(A_KV_HEADS):
        heads = list(range(group * j, group * (j + 1)))
        kd = _dup_half(k, j, lo).astype(BF16)
        vd = _dup_half(v, j, lo).astype(BF16)
        q = _stack_heads(q_ref, heads, lo, HD ** -0.5)
        s = lax.dot_general(q, kd, NT_DIMS, preferred_element_type=F32)
        o = _softmax_av([s], [vd], _sink_column(sink_ref, heads, SEQ))
        for t in range(group // 2):
            pair = heads[2 * t] // 2
            o_ref[:, pair * LANES:(pair + 1) * LANES] = jnp.where(
                lo, o[(2 * t) * SEQ:(2 * t + 1) * SEQ], o[(2 * t + 1) * SEQ:(2 * t + 2) * SEQ])


def _ctx_gqa(p, sink):
    return pl.pallas_call(
        _ctx_gqa_kernel,
        grid_spec=pltpu.PrefetchScalarGridSpec(
            num_scalar_prefetch=1,
            grid=(BATCH,),
            in_specs=[pl.BlockSpec((SEQ, A_Q), lambda b, s: (b, 0)),
                      pl.BlockSpec((SEQ, A_KV), lambda b, s: (b, A_Q // A_KV)),
                      pl.BlockSpec((SEQ, A_KV), lambda b, s: (b, A_Q // A_KV + 1))],
            out_specs=pl.BlockSpec((SEQ, A_Q), lambda b, s: (b, 0))),
        out_shape=jax.ShapeDtypeStruct((NP_TOK, A_Q), F32),
        compiler_params=_cparams(1),
        name="ctx_gqa",
    )(sink, p, p, p)


def _win_kernel(sink_ref, q_ref, kp_ref, kc_ref, kn_ref, vp_ref, vc_ref, vn_ref, ck_ref, cv_ref, o_ref):
    i = pl.program_id(1)
    lo = _lane_lo()
    k = jnp.concatenate([kp_ref[...], kc_ref[...], kn_ref[...]], axis=0)
    v = jnp.concatenate([vp_ref[...], vc_ref[...], vn_ref[...]], axis=0)
    ck = ck_ref[...]
    cv = cv_ref[...]
    group = A_HEADS // A_KV_HEADS
    qpos = i * A_BLOCK + lax.broadcasted_iota(jnp.int32, (A_BLOCK, 3 * A_BLOCK), 0)
    kpos = (i - 1) * A_BLOCK + lax.broadcasted_iota(jnp.int32, (A_BLOCK, 3 * A_BLOCK), 1)
    valid = (jnp.abs(kpos - qpos) <= A_BLOCK) & (kpos >= 0) & (kpos < DEC_SEQ)
    valid = jnp.concatenate([valid] * group, axis=0)
    for j in range(A_KV_HEADS):
        heads = list(range(group * j, group * (j + 1)))
        kd = _dup_half(k, j, lo).astype(BF16)
        vd = _dup_half(v, j, lo).astype(BF16)
        ckd = _dup_half(ck, j, lo).astype(BF16)
        cvd = _dup_half(cv, j, lo).astype(BF16)
        q = _stack_heads(q_ref, heads, lo, HD ** -0.5)
        s_loc = jnp.where(valid, lax.dot_general(q, kd, NT_DIMS, preferred_element_type=F32), NEG)
        s_ctx = lax.dot_general(q, ckd, NT_DIMS, preferred_element_type=F32)
        o = _softmax_av([s_loc, s_ctx], [vd, cvd], _sink_column(sink_ref, heads, A_BLOCK))
        for t in range(group // 2):
            pair = heads[2 * t] // 2
            o_ref[:, pair * LANES:(pair + 1) * LANES] = jnp.where(
                lo, o[(2 * t) * A_BLOCK:(2 * t + 1) * A_BLOCK], o[(2 * t + 1) * A_BLOCK:(2 * t + 2) * A_BLOCK])


def _win_attention(p, cache_k, cache_v, sink):
    nblk = DEC_SEQ // A_BLOCK
    base = NP_TOK // A_BLOCK
    kcol = A_Q // A_KV

    def kv_spec(col, off):
        return pl.BlockSpec((A_BLOCK, A_KV),
                            lambda b, i, s: (base + b * nblk + jnp.clip(i + off, 0, nblk - 1), col))

    ctx_spec = pl.BlockSpec((None, PAST, A_KV), lambda b, i, s: (b, 0, 0))
    return pl.pallas_call(
        _win_kernel,
        grid_spec=pltpu.PrefetchScalarGridSpec(
            num_scalar_prefetch=1,
            grid=(DEC_BATCH, nblk),
            in_specs=[pl.BlockSpec((A_BLOCK, A_Q), lambda b, i, s: (base + b * nblk + i, 0)),
                      kv_spec(kcol, -1), kv_spec(kcol, 0), kv_spec(kcol, 1),
                      kv_spec(kcol + 1, -1), kv_spec(kcol + 1, 0), kv_spec(kcol + 1, 1),
                      ctx_spec, ctx_spec],
            out_specs=pl.BlockSpec((A_BLOCK, A_Q), lambda b, i, s: (b * nblk + i, 0))),
        out_shape=jax.ShapeDtypeStruct((NS_TOK, A_Q), F32),
        compiler_params=_cparams(2),
        name="win_attn",
    )(sink, p, p, p, p, p, p, p, cache_k.reshape(DEC_BATCH, PAST, A_KV), cache_v.reshape(DEC_BATCH, PAST, A_KV))


def _ret_kernel(df_ref, db_ref, q_ref, k_ref, v_ref, g_ref, gn_ref, s0f_ref, s0b_ref,
                o_ref, sf_ref, sb_ref, of_scr, *, length):
    c_len = RET_CHUNK
    n = length // c_len
    lo = _lane_lo()
    hi = jnp.logical_not(lo)
    row = lax.broadcasted_iota(jnp.int32, (c_len, c_len), 0)
    col = lax.broadcasted_iota(jnp.int32, (c_len, c_len), 1)
    rowp = lax.broadcasted_iota(jnp.int32, (LANES, LANES), 0)
    colp = lax.broadcasted_iota(jnp.int32, (LANES, LANES), 1)
    blockdiag = (rowp < HD) == (colp < HD)
    idx = lax.broadcasted_iota(jnp.int32, (c_len, 1), 0).astype(F32)

    def direction(dec_ref, forward):
        lg = -jnp.exp(dec_ref[...])
        diff = (row - col) if forward else (col - row)
        keep = (diff >= 0) if forward else (diff > 0)
        dist = jnp.maximum(diff, 0).astype(F32)
        dm = [jnp.where(keep, jnp.exp(dist * lg[:, off:off + 1]), 0.0) for off in (0, HD)]
        if forward:
            xi = jnp.exp((idx + 1.0) * lg)
            zeta = jnp.exp((c_len - 1.0 - idx) * lg)
        else:
            xi = jnp.exp((c_len - idx) * lg)
            zeta = jnp.exp(idx * lg)
        return dm, xi, zeta, jnp.exp(c_len * lg)

    def chunk(c, state, consts):
        dm, xi, zeta, gch = consts
        rows = pl.ds(pl.multiple_of(c * c_len, c_len), c_len)
        qc = q_ref[rows, :]
        kc = k_ref[rows, :] * HD ** -0.5
        vc = v_ref[rows, :].astype(BF16)
        kb = kc.astype(BF16)
        outs = []
        for half, keep in enumerate((lo, hi)):
            qh = jnp.where(keep, qc, 0.0).astype(BF16)
            inner = lax.dot_general(qh, kb, NT_DIMS, preferred_element_type=F32) * dm[half]
            outs.append(jnp.dot(inner.astype(BF16), vc, preferred_element_type=F32))
        cross = jnp.dot(qc.astype(BF16), state.astype(BF16), preferred_element_type=F32) * xi
        o = jnp.where(lo, outs[0], outs[1]) + cross
        kz_t = (kc * zeta).T.astype(BF16)
        upd = jnp.dot(kz_t, vc, preferred_element_type=F32)
        state = gch * state + jnp.where(blockdiag, upd, 0.0)
        return rows, o, state

    cf = direction(df_ref, True)

    def fwd_body(c, state):
        rows, o, state = chunk(c, state, cf)
        of_scr[rows, :] = o
        return state

    unroll = min(n, RET_UNROLL)
    sf_ref[...] = lax.fori_loop(0, n, fwd_body, s0f_ref[...], unroll=unroll)

    cb = direction(db_ref, False)
    gn = gn_ref[...]

    def bwd_body(t, state):
        rows, o, state = chunk(n - 1 - t, state, cb)
        o = o + of_scr[rows, :]

        def per_head(x):
            a = jnp.where(lo, x, 0.0).sum(axis=-1, keepdims=True)
            b = jnp.where(hi, x, 0.0).sum(axis=-1, keepdims=True)
            return jnp.where(lo, a, b) * (1.0 / HD)

        d = o - per_head(o)
        y = d * lax.rsqrt(per_head(d * d) + EPS) * gn
        o_ref[rows, :] = _silu(g_ref[rows, :]) * y
        return state

    sb_ref[...] = lax.fori_loop(0, n, bwd_body, s0b_ref[...], unroll=unroll)


def _pair_lanes(v):
    return jnp.repeat(v.astype(F32), HD).reshape(B_HEADS // 2, 1, LANES)


def _blockdiag_states(s):
    b = s.shape[0]
    s = s.astype(F32).reshape(b, B_HEADS // 2, 2, HD, HD)
    z = jnp.zeros_like(s[:, :, 0])
    top = jnp.concatenate([s[:, :, 0], z], axis=-1)
    bot = jnp.concatenate([z, s[:, :, 1]], axis=-1)
    return jnp.concatenate([top, bot], axis=-2)


def _diag_states(sp):
    b = sp.shape[0]
    s = jnp.stack([sp[:, :, :HD, :HD], sp[:, :, HD:, HD:]], axis=2)
    return s.reshape(b, B_HEADS, HD, HD)


def _retention(p, row_base, batch, length, dec_f, dec_b, gn_g, s0f, s0b):
    npairs = B_HEADS // 2
    blk0 = row_base // length
    qcol = (A_Q + 2 * A_KV) // LANES

    def col_spec(off):
        return pl.BlockSpec((length, LANES), lambda b, h: (blk0 + b, qcol + off * npairs + h))

    lane_spec = pl.BlockSpec((None, 1, LANES), lambda b, h: (h, 0, 0))
    state_spec = pl.BlockSpec((None, None, LANES, LANES), lambda b, h: (b, h, 0, 0))
    state_shape = jax.ShapeDtypeStruct((batch, npairs, LANES, LANES), F32)
    return pl.pallas_call(
        functools.partial(_ret_kernel, length=length),
        grid=(batch, npairs),
        in_specs=[lane_spec, lane_spec, col_spec(0), col_spec(1), col_spec(2), col_spec(3), lane_spec,
                  state_spec, state_spec],
        out_specs=[pl.BlockSpec((length, LANES), lambda b, h: (b, h)), state_spec, state_spec],
        out_shape=[jax.ShapeDtypeStruct((batch * length, B_W), F32), state_shape, state_shape],
        scratch_shapes=[pltpu.VMEM((length, LANES), F32)],
        compiler_params=_cparams(2),
        name="retention",
    )(_pair_lanes(dec_f), _pair_lanes(dec_b), p, p, p, p, gn_g.reshape(npairs, 1, LANES), s0f, s0b)


def _ctx_mha_kernel(q_ref, k_ref, v_ref, o_ref):
    lo = _lane_lo()
    for pair in range(C_HEADS // 2):
        cols = slice(pair * LANES, (pair + 1) * LANES)
        q = _stack_heads(q_ref, [2 * pair, 2 * pair + 1], lo, HD ** -0.5)
        s = lax.dot_general(q, k_ref[:, cols].astype(BF16), NT_DIMS, preferred_element_type=F32)
        o = _softmax_av([s], [v_ref[:, cols].astype(BF16)])
        o_ref[:, cols] = jnp.where(lo, o[:SEQ], o[SEQ:])


def _ctx_mha(p):
    return pl.pallas_call(
        _ctx_mha_kernel,
        grid=(BATCH,),
        in_specs=[pl.BlockSpec((SEQ, C_W), lambda b: (b, 0)),
                  pl.BlockSpec((SEQ, C_W), lambda b: (b, 1)),
                  pl.BlockSpec((SEQ, C_W), lambda b: (b, 2))],
        out_specs=pl.BlockSpec((SEQ, C_W), lambda b: (b, 0)),
        out_shape=jax.ShapeDtypeStruct((NP_TOK, C_W), F32),
        compiler_params=_cparams(1),
        name="ctx_mha",
    )(p, p, p)


NA_WIN_ROWS = 2 * NA_ROWS
NA_WIN = NA_WIN_ROWS * GRID_W
NA_QROWS = NA_ROWS * GRID_W
NA_PAD_ROWS = NA_KH // 2
NA_TABLE = 1536


def _na_kernel(q_ref, kp_ref, km_ref, kn_ref, vp_ref, vm_ref, vn_ref, ck_ref, cv_ref, ue_ref, uo_ref, o_ref):
    r0 = pl.program_id(2) * NA_ROWS
    n_rows = DEC_SEQ // GRID_W
    lo = _lane_lo()
    k = jnp.concatenate([kp_ref[...], km_ref[...], kn_ref[...]], axis=0).astype(BF16)
    v = jnp.concatenate([vp_ref[...], vm_ref[...], vn_ref[...]], axis=0).astype(BF16)
    ck = ck_ref[...].astype(BF16)
    cv = cv_ref[...].astype(BF16)
    q = q_ref[...] * HD ** -0.5
    klane = lax.broadcasted_iota(jnp.int32, (1, NA_WIN), 1)
    outs = []
    for half, keep in enumerate((lo, jnp.logical_not(lo))):
        qh = jnp.where(keep, q, 0.0).astype(BF16)
        s = lax.dot_general(qh, k, NT_DIMS, preferred_element_type=F32)
        s_ctx = lax.dot_general(qh, ck, NT_DIMS, preferred_element_type=F32)
        pieces = []
        for rq in range(NA_ROWS):
            start = NA_KH - 1 - rq
            if start % 2 == 0:
                u = ue_ref[half, :, start * GRID_W:start * GRID_W + NA_WIN]
            else:
                u = uo_ref[half, :, (start - 1) * GRID_W:(start - 1) * GRID_W + NA_WIN]
            r = r0 + rq
            first = jnp.clip(r - NA_KH // 2, 0, n_rows - NA_KH)
            lane0 = (first - r0 + NA_PAD_ROWS) * GRID_W
            in_rows = (klane >= lane0) & (klane < lane0 + NA_KH * GRID_W)
            pieces.append(jnp.where(in_rows, s[rq * GRID_W:(rq + 1) * GRID_W] + u, NEG))
        outs.append(_softmax_av([jnp.concatenate(pieces, axis=0), s_ctx], [v, cv]))
    o_ref[...] = jnp.where(lo, outs[0], outs[1])


def _na_bias_tables(rpb):
    cq = jnp.arange(GRID_W)
    ck = jnp.arange(GRID_W)
    dc = jnp.clip(ck[None] - cq[:, None], -(NA_KW - 1), NA_KW - 1) + NA_KW - 1
    cs = jnp.clip(cq - NA_KW // 2, 0, GRID_W - NA_KW)
    col_ok = (ck[None] >= cs[:, None]) & (ck[None] < cs[:, None] + NA_KW)
    t = rpb.astype(F32)[:, :, dc]
    t = jnp.where(col_ok[None, None], t, NEG).transpose(0, 2, 1, 3)
    n_dr = 2 * NA_KH - 1
    blocks = NA_TABLE // GRID_W
    t = jnp.pad(t, ((0, 0), (0, 0), (NA_PAD_ROWS, blocks - n_dr - NA_PAD_ROWS), (0, 0)), constant_values=NEG)
    ue = t.reshape(C_HEADS, GRID_W, NA_TABLE)
    uo = jnp.concatenate([ue[..., GRID_W:], jnp.full((C_HEADS, GRID_W, GRID_W), NEG, F32)], axis=-1)
    return ue, uo


def _na_attention(p, cache_k, cache_v, rpb):
    npairs = C_HEADS // 2
    nrb = DEC_SEQ // NA_QROWS
    half = NA_QROWS // 2
    qbase = NP_TOK // NA_QROWS
    hbase = NP_TOK // half
    kcol = C_W // LANES
    ue, uo = _na_bias_tables(rpb)

    def main_spec(col0):
        return pl.BlockSpec((NA_QROWS, LANES), lambda b, h, r: (qbase + b * nrb + r, col0 + h))

    def side_spec(col0, off):
        return pl.BlockSpec((half, LANES),
                            lambda b, h, r: (hbase + b * 2 * nrb + jnp.clip(2 * r + off, 0, 2 * nrb - 1), col0 + h))

    ctx_spec = pl.BlockSpec((None, PAST, LANES), lambda b, h, r: (b, 0, h))
    tab_spec = pl.BlockSpec((2, GRID_W, NA_TABLE), lambda b, h, r: (h, 0, 0))
    return pl.pallas_call(
        _na_kernel,
        grid=(DEC_BATCH, npairs, nrb),
        in_specs=[main_spec(0),
                  side_spec(kcol, -1), main_spec(kcol), side_spec(kcol, 2),
                  side_spec(2 * kcol, -1), main_spec(2 * kcol), side_spec(2 * kcol, 2),
                  ctx_spec, ctx_spec, tab_spec, tab_spec],
        out_specs=pl.BlockSpec((NA_QROWS, LANES), lambda b, h, r: (b * nrb + r, h)),
        out_shape=jax.ShapeDtypeStruct((NS_TOK, C_W), F32),
        compiler_params=_cparams(3),
        name="na_attn",
    )(p, p, p, p, p, p, p, cache_k.reshape(DEC_BATCH, PAST, C_W), cache_v.reshape(DEC_BATCH, PAST, C_W), ue, uo)


def _route(biased, scores):
    t = biased.shape[1]
    per_group = N_EXPERTS // N_GROUPS
    i8 = lax.broadcasted_iota(jnp.int32, (per_group, t), 0)
    g_rows = []
    for g in range(N_GROUPS):
        bg = biased[g * per_group:(g + 1) * per_group]
        m1 = bg.max(axis=0, keepdims=True)
        first = jnp.where(bg == m1, i8, per_group).min(axis=0, keepdims=True)
        m2 = jnp.where(i8 == first, -jnp.inf, bg).max(axis=0, keepdims=True)
        g_rows.append(m1 + m2)
    g_top = jnp.concatenate(g_rows, axis=0)
    gi = lax.broadcasted_iota(jnp.int32, g_top.shape, 0)
    g_sel = jnp.zeros(g_top.shape, jnp.int32)
    cur = g_top
    for _ in range(TOPK_GROUPS):
        m = cur.max(axis=0, keepdims=True)
        hit = gi == jnp.where(cur == m, gi, N_GROUPS).min(axis=0, keepdims=True)
        g_sel = jnp.where(hit, 1, g_sel)
        cur = jnp.where(hit, -jnp.inf, cur)
    e_sel = jnp.concatenate([jnp.broadcast_to(g_sel[g:g + 1], (per_group, t)) for g in range(N_GROUPS)], axis=0)
    cur = jnp.where(e_sel > 0, biased, NEG)
    ei = lax.broadcasted_iota(jnp.int32, cur.shape, 0)
    ids, gates, hits = [], [], []
    for _ in range(TOP_K):
        m = cur.max(axis=0, keepdims=True)
        f = jnp.where(cur == m, ei, N_EXPERTS).min(axis=0, keepdims=True)
        hit = ei == f
        ids.append(f)
        hits.append(hit)
        gates.append(jnp.where(hit, scores, 0.0).sum(axis=0, keepdims=True))
        cur = jnp.where(hit, -jnp.inf, cur)
    gate = jnp.concatenate(gates, axis=0)
    gate = gate / gate.sum(axis=0, keepdims=True) * ROUTED_SCALE
    return jnp.concatenate(ids, axis=0), gate, hits


def _pack_bf16_pairs(h):
    bits = lax.bitcast_convert_type(h.astype(BF16).astype(F32), jnp.uint32)
    return bits[:, :D // 2] | (bits[:, D // 2:] >> 16)


def _unpack_bf16_pairs(xp):
    hi = lax.bitcast_convert_type(xp & jnp.uint32(0xFFFF0000), F32).astype(BF16)
    lo = lax.bitcast_convert_type(xp << 16, F32).astype(BF16)
    return hi, lo


def _dot_halves(hi, lo, w_ref):
    return (jnp.dot(hi, w_ref[:D // 2, :], preferred_element_type=F32)
            + jnp.dot(lo, w_ref[D // 2:, :], preferred_element_type=F32))


def _outproj_kernel(*refs, n_parts):
    x_ref = refs[0]
    part_refs = refs[1:1 + 3 * n_parts]
    gate_ref, shift_ref, scale_ref, g2_ref, rw_ref, rb_ref = refs[1 + 3 * n_parts:7 + 3 * n_parts]
    xo_ref, h_ref, idx_ref, wgt_ref, rank_ref, cnt_ref = refs[7 + 3 * n_parts:]
    is_prompt = pl.program_id(0) < NP_TOK // TM

    @pl.when(pl.program_id(0) == 0)
    def _():
        cnt_ref[...] = jnp.zeros_like(cnt_ref)

    y = None
    for t in range(n_parts):
        ap_ref, as_ref, w_ref = part_refs[3 * t:3 * t + 3]
        a = jnp.where(is_prompt, ap_ref[...], as_ref[...]).astype(BF16)
        d = jnp.dot(a, w_ref[...], preferred_element_type=F32)
        y = d if y is None else y + d
    x = x_ref[...] + gate_ref[...] * y
    xo_ref[...] = x
    h = _rms(x, g2_ref[...]) * (1.0 + scale_ref[...]) + shift_ref[...]
    h_ref[...] = _pack_bf16_pairs(h)
    logits = lax.dot_general(rw_ref[...], h, NT_DIMS, preferred_element_type=F32, precision=HIGHEST)
    scores = jax.nn.sigmoid(logits)
    idx, gate, hits = _route(scores + rb_ref[...], scores)
    idx_ref[...] = idx
    wgt_ref[...] = gate
    chosen = hits[0]
    for hit in hits[1:]:
        chosen = chosen | hit
    m = jnp.where(chosen, 1.0, 0.0)
    before = (lax.broadcasted_iota(jnp.int32, (TM, TM), 0) < lax.broadcasted_iota(jnp.int32, (TM, TM), 1))
    prefix = jnp.dot(m.astype(BF16), jnp.where(before, 1.0, 0.0).astype(BF16), preferred_element_type=F32)
    rank_all = prefix + cnt_ref[...]
    rank_ref[...] = jnp.concatenate(
        [jnp.where(hit, rank_all, 0.0).sum(axis=0, keepdims=True) for hit in hits], axis=0).astype(jnp.int32)
    cnt_ref[...] += m.sum(axis=1, keepdims=True)


def _outproj(x, parts, mod, g2, router_w, router_b):
    npb = NP_TOK // TM
    nsb = NS_TOK // TM
    in_specs = [pl.BlockSpec((TM, D), lambda i: (i, 0))]
    args = [x]
    for ap, a_s, w in parts:
        width = ap.shape[1]
        in_specs += [pl.BlockSpec((TM, width), lambda i: (jnp.minimum(i, npb - 1), 0)),
                     pl.BlockSpec((TM, width), lambda i: (jnp.clip(i - npb, 0, nsb - 1), 0)),
                     pl.BlockSpec((width, D), lambda i: (0, 0))]
        args += [ap, a_s, w]
    in_specs += [_mod_spec(2), _mod_spec(3), _mod_spec(4),
                 pl.BlockSpec((1, D), lambda i: (0, 0)),
                 pl.BlockSpec((N_EXPERTS, D), lambda i: (0, 0)),
                 pl.BlockSpec((N_EXPERTS, 1), lambda i: (0, 0))]
    args += [mod, mod, mod, g2.reshape(1, D), router_w.T, router_b.reshape(N_EXPERTS, 1)]
    return pl.pallas_call(
        functools.partial(_outproj_kernel, n_parts=len(parts)),
        grid=(N_TOK // TM,),
        in_specs=in_specs,
        out_specs=[pl.BlockSpec((TM, D), lambda i: (i, 0)),
                   pl.BlockSpec((TM, D // 2), lambda i: (i, 0)),
                   pl.BlockSpec((TOP_K, TM), lambda i: (0, i)),
                   pl.BlockSpec((TOP_K, TM), lambda i: (0, i)),
                   pl.BlockSpec((TOP_K, TM), lambda i: (0, i)),
                   pl.BlockSpec((N_EXPERTS, 1), lambda i: (0, 0))],
        out_shape=[jax.ShapeDtypeStruct((N_TOK, D), F32),
                   jax.ShapeDtypeStruct((N_TOK, D // 2), jnp.uint32),
                   jax.ShapeDtypeStruct((TOP_K, N_TOK), jnp.int32),
                   jax.ShapeDtypeStruct((TOP_K, N_TOK), F32),
                   jax.ShapeDtypeStruct((TOP_K, N_TOK), jnp.int32),
                   jax.ShapeDtypeStruct((N_EXPERTS, 1), F32)],
        compiler_params=_cparams(1),
        name="outproj_router",
    )(*args)


def _experts_kernel(be_ref, nu_ref, x_ref, w1_ref, w3_ref, w2_ref, o_ref, w1b, w3b, w2b):
    i = pl.program_id(0)
    e = be_ref[i]
    prev = be_ref[jnp.maximum(i - 1, 0)]

    @pl.when((i == 0) | (e != prev))
    def _():
        w1b[...] = w1_ref[...].astype(BF16)
        w3b[...] = w3_ref[...].astype(BF16)
        w2b[...] = w2_ref[...].astype(BF16)

    @pl.when(i < nu_ref[0])
    def _():
        hi, lo = _unpack_bf16_pairs(x_ref[...])
        a = _dot_halves(hi, lo, w1b)
        b = _dot_halves(hi, lo, w3b)
        h = (_silu(a) * b).astype(BF16)
        o_ref[...] = jnp.dot(h, w2b[...], preferred_element_type=F32)

    @pl.when(i >= nu_ref[0])
    def _():
        o_ref[...] = jnp.zeros_like(o_ref)


def _experts(block_e, n_used, x_sorted, w1, w3, w2, layer):
    return pl.pallas_call(
        _experts_kernel,
        grid_spec=pltpu.PrefetchScalarGridSpec(
            num_scalar_prefetch=2,
            grid=(N_MOE_BLOCKS,),
            in_specs=[pl.BlockSpec((MOE_BLOCK, D // 2), lambda i, be, nu: (i, 0)),
                      pl.BlockSpec((None, None, D, FF), lambda i, be, nu: (layer, be[i], 0, 0)),
                      pl.BlockSpec((None, None, D, FF), lambda i, be, nu: (layer, be[i], 0, 0)),
                      pl.BlockSpec((None, None, FF, D), lambda i, be, nu: (layer, be[i], 0, 0))],
            out_specs=pl.BlockSpec((MOE_BLOCK, D), lambda i, be, nu: (i, 0)),
            scratch_shapes=[pltpu.VMEM((D, FF), BF16), pltpu.VMEM((D, FF), BF16), pltpu.VMEM((FF, D), BF16)]),
        out_shape=jax.ShapeDtypeStruct((N_MOE_BLOCKS * MOE_BLOCK, D), F32),
        compiler_params=_cparams(1),
        name="experts",
    )(block_e, n_used, x_sorted, w1, w3, w2)


def _dispatch_plan(idx_t, rank_t, counts):
    counts = counts.astype(jnp.int32)
    padded = (counts + MOE_BLOCK - 1) // MOE_BLOCK * MOE_BLOCK
    pad_end = jnp.cumsum(padded)
    pad_start = pad_end - padded
    start = jnp.cumsum(counts) - counts
    block_row0 = jnp.arange(N_MOE_BLOCKS, dtype=jnp.int32) * MOE_BLOCK
    block_e = jnp.minimum((pad_end[None, :] <= block_row0[:, None]).sum(axis=1), N_EXPERTS - 1).astype(jnp.int32)
    n_used = (pad_end[-1] // MOE_BLOCK).astype(jnp.int32).reshape(1)
    key = idx_t * N_TOK + jnp.arange(N_TOK, dtype=jnp.int32)[None, :]
    tok_sorted = jnp.sort(key.reshape(N_ASSIGN)) % N_TOK
    local = (block_row0 - pad_start[block_e])[:, None] + jnp.arange(MOE_BLOCK, dtype=jnp.int32)[None, :]
    valid = local < counts[block_e][:, None]
    src = jnp.clip(start[block_e][:, None] + local, 0, N_ASSIGN - 1)
    row_tok = jnp.where(valid, tok_sorted[src], 0).reshape(N_MOE_BLOCKS * MOE_BLOCK).astype(jnp.int32)
    e_iota = jnp.arange(N_EXPERTS, dtype=jnp.int32)[:, None, None]
    dest = jnp.where(idx_t[None] == e_iota, pad_start[:, None, None], 0).sum(axis=0) + rank_t
    return row_tok, block_e, n_used, dest.astype(jnp.int32)


SC_CORES = 2
SC_SUBCORES = 16
SC_WORKERS = SC_CORES * SC_SUBCORES
SC_CHUNK_BYTES = 64 * 1024
SC_SLOTS = 4


def _sc_gather(table, idx):
    n_idx = idx.shape[0]
    width = table.shape[1]
    chunk = SC_CHUNK_BYTES // (4 * width)
    per_worker = n_idx // SC_WORKERS
    n_chunks = per_worker // chunk
    ahead = SC_SLOTS - 1
    assert per_worker * SC_WORKERS == n_idx and n_chunks * chunk == per_worker and n_chunks % SC_SLOTS == 0
    mesh = plsc.VectorSubcoreMesh(core_axis_name="c", subcore_axis_name="s")

    @functools.partial(
        pl.kernel, mesh=mesh,
        out_type=jax.ShapeDtypeStruct((n_idx, width), table.dtype),
        scratch_types=[pltpu.VMEM((per_worker,), jnp.int32),
                       pltpu.VMEM((SC_SLOTS, chunk, width), table.dtype),
                       pltpu.SemaphoreType.DMA((SC_SLOTS,)),
                       pltpu.SemaphoreType.DMA((SC_SLOTS,))])
    def gather(t_hbm, i_hbm, o_hbm, idx_v, rows_v, gsem, wsem):
        worker = lax.axis_index("s") * SC_CORES + lax.axis_index("c")
        base = worker * per_worker
        pltpu.sync_copy(i_hbm.at[pl.ds(pl.multiple_of(base, chunk), per_worker)], idx_v)

        def gather_copy(c, b):
            ids = idx_v.at[pl.ds(pl.multiple_of(c * chunk, chunk), chunk)]
            return pltpu.make_async_copy(t_hbm.at[ids], rows_v.at[b], gsem.at[b])

        def write_copy(c, b):
            rows = pl.ds(pl.multiple_of(base + c * chunk, chunk), chunk)
            return pltpu.make_async_copy(rows_v.at[b], o_hbm.at[rows], wsem.at[b])

        for c in range(ahead):
            gather_copy(c, c).start()

        @pl.loop(0, n_chunks, step=SC_SLOTS)
        def _(c0):
            for b in range(SC_SLOTS):
                c = c0 + b
                refill = (b + ahead) % SC_SLOTS
                gather_copy(c, b).wait()
                write_copy(c, b).start()

                @pl.when(c > 0)
                def _():
                    write_copy(c - 1, refill).wait()

                @pl.when(c + ahead < n_chunks)
                def _():
                    gather_copy(c + ahead, refill).start()

        write_copy(n_chunks - 1, (n_chunks - 1) % SC_SLOTS).wait()

    return gather(table, idx)


TC = 256


def _combine_kernel(x_ref, h_ref, y_ref, wgt_ref, gate_ref, w1_ref, w3_ref, w2_ref, fg_ref, o_ref, *, final):
    hi, lo = _unpack_bf16_pairs(h_ref[...])
    a = _dot_halves(hi, lo, w1_ref)
    b = _dot_halves(hi, lo, w3_ref)
    ffn = jnp.dot((_silu(a) * b).astype(BF16), w2_ref[...], preferred_element_type=F32)
    wgt = wgt_ref[...]
    for k in range(TOP_K):
        ffn = ffn + y_ref[k] * wgt[:, k:k + 1]
    x = x_ref[...] + gate_ref[...] * ffn
    o_ref[...] = _rms(x, fg_ref[...]) if final else x


def _combine(x, h, y_rows, wgt, mod, sw1, sw3, sw2, final_g, final):
    row = pl.BlockSpec((TC, D), lambda i: (i, 0))
    return pl.pallas_call(
        functools.partial(_combine_kernel, final=final),
        grid=(N_TOK // TC,),
        in_specs=[row,
                  pl.BlockSpec((TC, D // 2), lambda i: (i, 0)),
                  pl.BlockSpec((TOP_K, TC, D), lambda i: (0, i, 0)),
                  pl.BlockSpec((TC, TOP_K), lambda i: (i, 0)),
                  _mod_spec(5, TC),
                  pl.BlockSpec((D, FF), lambda i: (0, 0)),
                  pl.BlockSpec((D, FF), lambda i: (0, 0)),
                  pl.BlockSpec((FF, D), lambda i: (0, 0)),
                  pl.BlockSpec((1, D), lambda i: (0, 0))],
        out_specs=row,
        out_shape=jax.ShapeDtypeStruct((N_TOK, D), F32),
        compiler_params=_cparams(1),
        name="combine",
    )(x, h, y_rows, wgt, mod, sw1.astype(BF16), sw3.astype(BF16), sw2.astype(BF16), final_g.reshape(1, D))


def kernel(x_prompt, x_sample, cache_a_k, cache_a_v, state_ret_fwd, state_ret_bwd, cache_c_k, cache_c_v,
           c, c_ctx, norm1_g, norm2_g, ada_w, ada_b, even_w_in, even_w_out, sink_a, ret_decay_fwd,
           ret_decay_bwd, ret_gn_g, odd_w_in, odd_w_out, na_rpb, router_w, router_b, exp_w1, exp_w3,
           exp_w2, sh_w1, sh_w3, sh_w2, final_g):
    x = jnp.concatenate([x_prompt.reshape(NP_TOK, D), x_sample.reshape(NS_TOK, D)], axis=0)
    cc = jnp.concatenate([c_ctx[None], c, jnp.zeros((8 - 1 - DEC_BATCH, D), F32)], axis=0)
    rope = _rope_tables()
    outs = {}
    for l in range(2):
        mod = _ada(cc, ada_w, ada_b, l)
        if l == 0:
            p = _inproj(x, norm1_g[l], mod, even_w_in[0].astype(BF16), rope, A_Q + A_KV)
            oa_p = _ctx_gqa(p, sink_a[0])
            oa_s = _win_attention(p, cache_a_k[:, 0], cache_a_v[:, 0], sink_a[0])
            zero = jnp.zeros((BATCH, B_HEADS // 2, LANES, LANES), F32)
            ob_p, sf, sb = _retention(p, 0, BATCH, SEQ, ret_decay_fwd[0], ret_decay_bwd[0], ret_gn_g[0], zero, zero)
            ob_s, _, _ = _retention(p, NP_TOK, DEC_BATCH, DEC_SEQ, ret_decay_fwd[0], ret_decay_bwd[0], ret_gn_g[0],
                                    _blockdiag_states(state_ret_fwd[:, 0]), _blockdiag_states(state_ret_bwd[:, 0]))
            w_out = even_w_out[0].astype(BF16)
            parts = [(oa_p, oa_s, w_out[:A_Q]), (ob_p, ob_s, w_out[A_Q:])]
            outs["a_k"] = p[:NP_TOK, A_Q:A_Q + A_KV].reshape(BATCH, 1, SEQ, A_KV_HEADS, HD)
            outs["a_v"] = p[:NP_TOK, A_Q + A_KV:A_Q + 2 * A_KV].reshape(BATCH, 1, SEQ, A_KV_HEADS, HD)
            outs["r_f"] = _diag_states(sf).reshape(BATCH, 1, B_HEADS, HD, HD)
            outs["r_b"] = _diag_states(sb).reshape(BATCH, 1, B_HEADS, HD, HD)
        else:
            p = _inproj(x, norm1_g[l], mod, odd_w_in[0].astype(BF16), rope, 0)
            o_p = _ctx_mha(p)
            o_s = _na_attention(p, cache_c_k[:, 0], cache_c_v[:, 0], na_rpb[0])
            parts = [(o_p, o_s, odd_w_out[0].astype(BF16))]
            outs["c_k"] = p[:NP_TOK, C_W:2 * C_W].reshape(BATCH, 1, SEQ, C_HEADS, HD)
            outs["c_v"] = p[:NP_TOK, 2 * C_W:3 * C_W].reshape(BATCH, 1, SEQ, C_HEADS, HD)
        x, h, idx_t, gate_t, rank_t, counts = _outproj(x, parts, mod, norm2_g[l], router_w[l], router_b[l])
        row_tok, block_e, n_used, dest = _dispatch_plan(idx_t, rank_t, counts[:, 0])
        y = _experts(block_e, n_used, _sc_gather(h, row_tok), exp_w1, exp_w3, exp_w2, l)
        y_rows = _sc_gather(y, dest.reshape(N_ASSIGN)).reshape(TOP_K, N_TOK, D)
        x = _combine(x, h, y_rows, gate_t.T, mod, sh_w1[l], sh_w3[l], sh_w2[l], final_g, final=(l == 1))
    y_prompt = x[:NP_TOK].reshape(BATCH, SEQ, D)
    y_sample = x[NP_TOK:].reshape(DEC_BATCH, DEC_SEQ, D)
    return (y_prompt, y_sample, outs["a_k"], outs["a_v"], outs["r_f"], outs["r_b"], outs["c_k"], outs["c_v"])
```

```python
import functools
import math

import jax
import jax.numpy as jnp
from jax import lax
from jax.experimental import pallas as pl
from jax.experimental.pallas import tpu as pltpu
from jax.experimental.pallas import tpu_sc as plsc

F32 = jnp.float32
BF16 = jnp.bfloat16
HIGHEST = lax.Precision.HIGHEST

D = 1024
BATCH = 32
SEQ = 256
DEC_BATCH = 4
DEC_SEQ = 4096
PAST = 256
GRID_W = 64
HD = 64
EPS = 1e-6
NEG = -1e30
ROPE_BASE = 10000.0
A_HEADS = 8
A_KV_HEADS = 2
A_Q = A_HEADS * HD
A_KV = A_KV_HEADS * HD
B_HEADS = 8
B_W = B_HEADS * HD
EVEN_IN = A_Q + 2 * A_KV + 4 * B_W
C_HEADS = 16
C_W = C_HEADS * HD
NA_KH = 8
NA_KW = 16
N_EXPERTS = 64
TOP_K = 8
N_GROUPS = 8
TOPK_GROUPS = 4
FF = 256
ROUTED_SCALE = 2.5
MOE_BLOCK = 512
RET_CHUNK = 128
RET_UNROLL = 2
A_BLOCK = 128

NP_TOK = BATCH * SEQ
NS_TOK = DEC_BATCH * DEC_SEQ
N_TOK = NP_TOK + NS_TOK
N_ASSIGN = N_TOK * TOP_K
N_MOE_BLOCKS = (N_ASSIGN + N_EXPERTS * (MOE_BLOCK - 1) + MOE_BLOCK - 1) // MOE_BLOCK

LANES = 128
TM = 512
NA_ROWS = 8
V7X_VMEM_LIMIT = 56 * 1024 * 1024

NT_DIMS = (((1,), (1,)), ((), ()))


def _cparams(n_axes, vmem=V7X_VMEM_LIMIT):
    return pltpu.CompilerParams(dimension_semantics=("arbitrary",) * n_axes, vmem_limit_bytes=vmem)


def _seg_of_block(i, rows):
    row0 = i * rows
    return jnp.where(row0 < NP_TOK, 0, 1 + (row0 - NP_TOK) // DEC_SEQ)


def _mod_spec(chunk, rows=TM):
    return pl.BlockSpec((None, 1, D), lambda i: (_seg_of_block(i, rows), 0, chunk))


def _silu(x):
    return x * jax.nn.sigmoid(x)


def _rms(x, g):
    return x * lax.rsqrt(jnp.mean(x * x, axis=-1, keepdims=True) + EPS) * g


def _lane_lo():
    return lax.broadcasted_iota(jnp.int32, (1, LANES), 1) < HD


def _ada_kernel(c_ref, w_ref, b_ref, o_ref):
    a = _silu(c_ref[...])
    o_ref[...] = jnp.dot(a, w_ref[...], preferred_element_type=F32, precision=HIGHEST) + b_ref[...]


def _ada(cc, w, b, layer):
    tn = 1536
    out = pl.pallas_call(
        _ada_kernel,
        grid=(6 * D // tn,),
        in_specs=[pl.BlockSpec((8, D), lambda j: (0, 0)),
                  pl.BlockSpec((None, D, tn), lambda j: (layer, 0, j)),
                  pl.BlockSpec((None, 1, tn), lambda j: (layer, 0, j))],
        out_specs=pl.BlockSpec((8, tn), lambda j: (0, j)),
        out_shape=jax.ShapeDtypeStruct((8, 6 * D), F32),
        compiler_params=_cparams(1),
        name="ada",
    )(cc, w, b.reshape(b.shape[0], 1, 6 * D))
    return out.reshape(8, 1, 6 * D)


def _inproj_kernel(x_ref, g_ref, shift_ref, scale_ref, w_ref, cos_ref, sin_ref, o_ref, *, rope_cols):
    h = _rms(x_ref[...], g_ref[...]) * (1.0 + scale_ref[...]) + shift_ref[...]
    o = jnp.dot(h.astype(BF16), w_ref[...], preferred_element_type=F32)
    if rope_cols:
        cos = cos_ref[...]
        sin = sin_ref[...]
        lane = lax.broadcasted_iota(jnp.int32, (1, LANES), 1)
        first = (lane % 32) < 16
        for c in range(rope_cols // LANES):
            oc = o[:, c * LANES:(c + 1) * LANES]
            partner = jnp.where(first, pltpu.roll(oc, LANES - 16, 1), pltpu.roll(oc, 16, 1))
            o_ref[:, c * LANES:(c + 1) * LANES] = oc * cos + partner * sin
        o_ref[:, rope_cols:] = o[:, rope_cols:]
    else:
        o_ref[...] = o


def _rope_tables():
    half = HD // 2
    inv = ROPE_BASE ** (-jnp.arange(0, half, 2, dtype=F32) / half)
    t = jnp.arange(DEC_SEQ)
    ang_r = (t // GRID_W).astype(F32)[:, None] * inv[None]
    ang_c = (t % GRID_W).astype(F32)[:, None] * inv[None]

    def head(fn_r, fn_c, sign):
        return jnp.concatenate([sign[0] * fn_r, sign[1] * fn_r, sign[0] * fn_c, sign[1] * fn_c], axis=-1)

    cos = head(jnp.cos(ang_r), jnp.cos(ang_c), (1.0, 1.0))
    sin = head(jnp.sin(ang_r), jnp.sin(ang_c), (-1.0, 1.0))
    cos = jnp.concatenate([jnp.ones((TM, HD), F32), cos], axis=0)
    sin = jnp.concatenate([jnp.zeros((TM, HD), F32), sin], axis=0)
    return jnp.tile(cos, (1, 2)), jnp.tile(sin, (1, 2))


def _inproj(x, g, mod, w_bf16, rope, rope_cols):
    n_out = w_bf16.shape[1]
    npb = NP_TOK // TM
    spb = DEC_SEQ // TM

    def rope_map(i):
        return (jnp.where(i < npb, 0, 1 + (i - npb) % spb), 0)

    return pl.pallas_call(
        functools.partial(_inproj_kernel, rope_cols=rope_cols),
        grid=(N_TOK // TM,),
        in_specs=[pl.BlockSpec((TM, D), lambda i: (i, 0)),
                  pl.BlockSpec((1, D), lambda i: (0, 0)),
                  _mod_spec(0), _mod_spec(1),
                  pl.BlockSpec((D, n_out), lambda i: (0, 0)),
                  pl.BlockSpec((TM, LANES), rope_map),
                  pl.BlockSpec((TM, LANES), rope_map)],
        out_specs=pl.BlockSpec((TM, n_out), lambda i: (i, 0)),
        out_shape=jax.ShapeDtypeStruct((N_TOK, n_out), F32),
        compiler_params=_cparams(1),
        name="inproj",
    )(x, g.reshape(1, D), mod, mod, w_bf16, rope[0], rope[1])


def _softmax_av(s_list, v_list, sink=None):
    mx = s_list[0].max(axis=-1, keepdims=True)
    for s in s_list[1:]:
        mx = jnp.maximum(mx, s.max(axis=-1, keepdims=True))
    if sink is not None:
        mx = jnp.maximum(mx, sink)
    den = jnp.exp(sink - mx) if sink is not None else 0.0
    acc = None
    for s, v in zip(s_list, v_list):
        p = jnp.exp(s - mx)
        den = den + p.sum(axis=-1, keepdims=True)
        pv = jnp.dot(p.astype(BF16), v, preferred_element_type=F32)
        acc = pv if acc is None else acc + pv
    return acc / den


def _dup_half(x, j, lo):
    xr = pltpu.roll(x, HD, 1)
    return jnp.where(lo, x, xr) if j == 0 else jnp.where(lo, xr, x)


def _stack_heads(q_ref, heads, lo, scale):
    parts = []
    for h in heads:
        qp = q_ref[:, (h // 2) * LANES:(h // 2 + 1) * LANES]
        keep = lo if h % 2 == 0 else jnp.logical_not(lo)
        parts.append(jnp.where(keep, qp, 0.0) * scale)
    return jnp.concatenate(parts, axis=0).astype(BF16)


def _sink_column(sink_ref, heads, rows):
    return jnp.concatenate([jnp.full((rows, 1), sink_ref[h], F32) for h in heads], axis=0)


def _ctx_gqa_kernel(sink_ref, q_ref, k_ref, v_ref, o_ref):
    lo = _lane_lo()
    k = k_ref[...]
    v = v_ref[...]
    group = A_HEADS // A_KV_HEADS
    for j in range(A_KV_HEADS):
        heads = list(range(group * j, group * (j + 1)))
        kd = _dup_half(k, j, lo).astype(BF16)
        vd = _dup_half(v, j, lo).astype(BF16)
        q = _stack_heads(q_ref, heads, lo, HD ** -0.5)
        s = lax.dot_general(q, kd, NT_DIMS, preferred_element_type=F32)
        o = _softmax_av([s], [vd], _sink_column(sink_ref, heads, SEQ))
        for t in range(group // 2):
            pair = heads[2 * t] // 2
            o_ref[:, pair * LANES:(pair + 1) * LANES] = jnp.where(
                lo, o[(2 * t) * SEQ:(2 * t + 1) * SEQ], o[(2 * t + 1) * SEQ:(2 * t + 2) * SEQ])


def _ctx_gqa(p, sink):
    return pl.pallas_call(
        _ctx_gqa_kernel,
        grid_spec=pltpu.PrefetchScalarGridSpec(
            num_scalar_prefetch=1,
            grid=(BATCH,),
            in_specs=[pl.BlockSpec((SEQ, A_Q), lambda b, s: (b, 0)),
                      pl.BlockSpec((SEQ, A_KV), lambda b, s: (b, A_Q // A_KV)),
                      pl.BlockSpec((SEQ, A_KV), lambda b, s: (b, A_Q // A_KV + 1))],
            out_specs=pl.BlockSpec((SEQ, A_Q), lambda b, s: (b, 0))),
        out_shape=jax.ShapeDtypeStruct((NP_TOK, A_Q), F32),
        compiler_params=_cparams(1),
        name="ctx_gqa",
    )(sink, p, p, p)


def _win_kernel(sink_ref, q_ref, kp_ref, kc_ref, kn_ref, vp_ref, vc_ref, vn_ref, ck_ref, cv_ref, o_ref):
    i = pl.program_id(1)
    lo = _lane_lo()
    k = jnp.concatenate([kp_ref[...], kc_ref[...], kn_ref[...]], axis=0)
    v = jnp.concatenate([vp_ref[...], vc_ref[...], vn_ref[...]], axis=0)
    ck = ck_ref[...]
    cv = cv_ref[...]
    group = A_HEADS // A_KV_HEADS
    qpos = i * A_BLOCK + lax.broadcasted_iota(jnp.int32, (A_BLOCK, 3 * A_BLOCK), 0)
    kpos = (i - 1) * A_BLOCK + lax.broadcasted_iota(jnp.int32, (A_BLOCK, 3 * A_BLOCK), 1)
    valid = (jnp.abs(kpos - qpos) <= A_BLOCK) & (kpos >= 0) & (kpos < DEC_SEQ)
    valid = jnp.concatenate([valid] * group, axis=0)
    for j in range(A_KV_HEADS):
        heads = list(range(group * j, group * (j + 1)))
        kd = _dup_half(k, j, lo).astype(BF16)
        vd = _dup_half(v, j, lo).astype(BF16)
        ckd = _dup_half(ck, j, lo).astype(BF16)
        cvd = _dup_half(cv, j, lo).astype(BF16)
        q = _stack_heads(q_ref, heads, lo, HD ** -0.5)
        s_loc = jnp.where(valid, lax.dot_general(q, kd, NT_DIMS, preferred_element_type=F32), NEG)
        s_ctx = lax.dot_general(q, ckd, NT_DIMS, preferred_element_type=F32)
        o = _softmax_av([s_loc, s_ctx], [vd, cvd], _sink_column(sink_ref, heads, A_BLOCK))
        for t in range(group // 2):
            pair = heads[2 * t] // 2
            o_ref[:, pair * LANES:(pair + 1) * LANES] = jnp.where(
                lo, o[(2 * t) * A_BLOCK:(2 * t + 1) * A_BLOCK], o[(2 * t + 1) * A_BLOCK:(2 * t + 2) * A_BLOCK])


def _win_attention(p, cache_k, cache_v, sink):
    nblk = DEC_SEQ // A_BLOCK
    base = NP_TOK // A_BLOCK
    kcol = A_Q // A_KV

    def kv_spec(col, off):
        return pl.BlockSpec((A_BLOCK, A_KV),
                            lambda b, i, s: (base + b * nblk + jnp.clip(i + off, 0, nblk - 1), col))

    ctx_spec = pl.BlockSpec((None, PAST, A_KV), lambda b, i, s: (b, 0, 0))
    return pl.pallas_call(
        _win_kernel,
        grid_spec=pltpu.PrefetchScalarGridSpec(
            num_scalar_prefetch=1,
            grid=(DEC_BATCH, nblk),
            in_specs=[pl.BlockSpec((A_BLOCK, A_Q), lambda b, i, s: (base + b * nblk + i, 0)),
                      kv_spec(kcol, -1), kv_spec(kcol, 0), kv_spec(kcol, 1),
                      kv_spec(kcol + 1, -1), kv_spec(kcol + 1, 0), kv_spec(kcol + 1, 1),
                      ctx_spec, ctx_spec],
            out_specs=pl.BlockSpec((A_BLOCK, A_Q), lambda b, i, s: (b * nblk + i, 0))),
        out_shape=jax.ShapeDtypeStruct((NS_TOK, A_Q), F32),
        compiler_params=_cparams(2),
        name="win_attn",
    )(sink, p, p, p, p, p, p, p, cache_k.reshape(DEC_BATCH, PAST, A_KV), cache_v.reshape(DEC_BATCH, PAST, A_KV))


def _ret_kernel(df_ref, db_ref, q_ref, k_ref, v_ref, g_ref, gn_ref, s0f_ref, s0b_ref,
                o_ref, sf_ref, sb_ref, of_scr, *, length):
    c_len = RET_CHUNK
    n = length // c_len
    lo = _lane_lo()
    hi = jnp.logical_not(lo)
    row = lax.broadcasted_iota(jnp.int32, (c_len, c_len), 0)
    col = lax.broadcasted_iota(jnp.int32, (c_len, c_len), 1)
    rowp = lax.broadcasted_iota(jnp.int32, (LANES, LANES), 0)
    colp = lax.broadcasted_iota(jnp.int32, (LANES, LANES), 1)
    blockdiag = (rowp < HD) == (colp < HD)
    idx = lax.broadcasted_iota(jnp.int32, (c_len, 1), 0).astype(F32)

    def direction(dec_ref, forward):
        lg = -jnp.exp(dec_ref[...])
        diff = (row - col) if forward else (col - row)
        keep = (diff >= 0) if forward else (diff > 0)
        dist = jnp.maximum(diff, 0).astype(F32)
        dm = [jnp.where(keep, jnp.exp(dist * lg[:, off:off + 1]), 0.0) for off in (0, HD)]
        if forward:
            xi = jnp.exp((idx + 1.0) * lg)
            zeta = jnp.exp((c_len - 1.0 - idx) * lg)
        else:
            xi = jnp.exp((c_len - idx) * lg)
            zeta = jnp.exp(idx * lg)
        return dm, xi, zeta, jnp.exp(c_len * lg)

    def chunk(c, state, consts):
        dm, xi, zeta, gch = consts
        rows = pl.ds(pl.multiple_of(c * c_len, c_len), c_len)
        qc = q_ref[rows, :]
        kc = k_ref[rows, :] * HD ** -0.5
        vc = v_ref[rows, :].astype(BF16)
        kb = kc.astype(BF16)
        outs = []
        for half, keep in enumerate((lo, hi)):
            qh = jnp.where(keep, qc, 0.0).astype(BF16)
            inner = lax.dot_general(qh, kb, NT_DIMS, preferred_element_type=F32) * dm[half]
            outs.append(jnp.dot(inner.astype(BF16), vc, preferred_element_type=F32))
        cross = jnp.dot(qc.astype(BF16), state.astype(BF16), preferred_element_type=F32) * xi
        o = jnp.where(lo, outs[0], outs[1]) + cross
        kz_t = (kc * zeta).T.astype(BF16)
        upd = jnp.dot(kz_t, vc, preferred_element_type=F32)
        state = gch * state + jnp.where(blockdiag, upd, 0.0)
        return rows, o, state

    cf = direction(df_ref, True)

    def fwd_body(c, state):
        rows, o, state = chunk(c, state, cf)
        of_scr[rows, :] = o
        return state

    unroll = min(n, RET_UNROLL)
    sf_ref[...] = lax.fori_loop(0, n, fwd_body, s0f_ref[...], unroll=unroll)

    cb = direction(db_ref, False)
    gn = gn_ref[...]

    def bwd_body(t, state):
        rows, o, state = chunk(n - 1 - t, state, cb)
        o = o + of_scr[rows, :]

        def per_head(x):
            a = jnp.where(lo, x, 0.0).sum(axis=-1, keepdims=True)
            b = jnp.where(hi, x, 0.0).sum(axis=-1, keepdims=True)
            return jnp.where(lo, a, b) * (1.0 / HD)

        d = o - per_head(o)
        y = d * lax.rsqrt(per_head(d * d) + EPS) * gn
        o_ref[rows, :] = _silu(g_ref[rows, :]) * y
        return state

    sb_ref[...] = lax.fori_loop(0, n, bwd_body, s0b_ref[...], unroll=unroll)


def _pair_lanes(v):
    return jnp.repeat(v.astype(F32), HD).reshape(B_HEADS // 2, 1, LANES)


def _blockdiag_states(s):
    b = s.shape[0]
    s = s.astype(F32).reshape(b, B_HEADS // 2, 2, HD, HD)
    z = jnp.zeros_like(s[:, :, 0])
    top = jnp.concatenate([s[:, :, 0], z], axis=-1)
    bot = jnp.concatenate([z, s[:, :, 1]], axis=-1)
    return jnp.concatenate([top, bot], axis=-2)


def _diag_states(sp):
    b = sp.shape[0]
    s = jnp.stack([sp[:, :, :HD, :HD], sp[:, :, HD:, HD:]], axis=2)
    return s.reshape(b, B_HEADS, HD, HD)


def _retention(p, row_base, batch, length, dec_f, dec_b, gn_g, s0f, s0b):
    npairs = B_HEADS // 2
    blk0 = row_base // length
    qcol = (A_Q + 2 * A_KV) // LANES

    def col_spec(off):
        return pl.BlockSpec((length, LANES), lambda b, h: (blk0 + b, qcol + off * npairs + h))

    lane_spec = pl.BlockSpec((None, 1, LANES), lambda b, h: (h, 0, 0))
    state_spec = pl.BlockSpec((None, None, LANES, LANES), lambda b, h: (b, h, 0, 0))
    state_shape = jax.ShapeDtypeStruct((batch, npairs, LANES, LANES), F32)
    return pl.pallas_call(
        functools.partial(_ret_kernel, length=length),
        grid=(batch, npairs),
        in_specs=[lane_spec, lane_spec, col_spec(0), col_spec(1), col_spec(2), col_spec(3), lane_spec,
                  state_spec, state_spec],
        out_specs=[pl.BlockSpec((length, LANES), lambda b, h: (b, h)), state_spec, state_spec],
        out_shape=[jax.ShapeDtypeStruct((batch * length, B_W), F32), state_shape, state_shape],
        scratch_shapes=[pltpu.VMEM((length, LANES), F32)],
        compiler_params=_cparams(2),
        name="retention",
    )(_pair_lanes(dec_f), _pair_lanes(dec_b), p, p, p, p, gn_g.reshape(npairs, 1, LANES), s0f, s0b)


def _ctx_mha_kernel(q_ref, k_ref, v_ref, o_ref):
    lo = _lane_lo()
    for pair in range(C_HEADS // 2):
        cols = slice(pair * LANES, (pair + 1) * LANES)
        q = _stack_heads(q_ref, [2 * pair, 2 * pair + 1], lo, HD ** -0.5)
        s = lax.dot_general(q, k_ref[:, cols].astype(BF16), NT_DIMS, preferred_element_type=F32)
        o = _softmax_av([s], [v_ref[:, cols].astype(BF16)])
        o_ref[:, cols] = jnp.where(lo, o[:SEQ], o[SEQ:])


def _ctx_mha(p):
    return pl.pallas_call(
        _ctx_mha_kernel,
        grid=(BATCH,),
        in_specs=[pl.BlockSpec((SEQ, C_W), lambda b: (b, 0)),
                  pl.BlockSpec((SEQ, C_W), lambda b: (b, 1)),
                  pl.BlockSpec((SEQ, C_W), lambda b: (b, 2))],
        out_specs=pl.BlockSpec((SEQ, C_W), lambda b: (b, 0)),
        out_shape=jax.ShapeDtypeStruct((NP_TOK, C_W), F32),
        compiler_params=_cparams(1),
        name="ctx_mha",
    )(p, p, p)


NA_WIN_ROWS = 2 * NA_ROWS
NA_WIN = NA_WIN_ROWS * GRID_W
NA_QROWS = NA_ROWS * GRID_W
NA_PAD_ROWS = NA_KH // 2
NA_TABLE = 1536


def _na_kernel(q_ref, kp_ref, km_ref, kn_ref, vp_ref, vm_ref, vn_ref, ck_ref, cv_ref, ue_ref, uo_ref, o_ref):
    r0 = pl.program_id(2) * NA_ROWS
    n_rows = DEC_SEQ // GRID_W
    lo = _lane_lo()
    k = jnp.concatenate([kp_ref[...], km_ref[...], kn_ref[...]], axis=0).astype(BF16)
    v = jnp.concatenate([vp_ref[...], vm_ref[...], vn_ref[...]], axis=0).astype(BF16)
    ck = ck_ref[...].astype(BF16)
    cv = cv_ref[...].astype(BF16)
    q = q_ref[...] * HD ** -0.5
    klane = lax.broadcasted_iota(jnp.int32, (1, NA_WIN), 1)
    outs = []
    for half, keep in enumerate((lo, jnp.logical_not(lo))):
        qh = jnp.where(keep, q, 0.0).astype(BF16)
        s = lax.dot_general(qh, k, NT_DIMS, preferred_element_type=F32)
        s_ctx = lax.dot_general(qh, ck, NT_DIMS, preferred_element_type=F32)
        pieces = []
        for rq in range(NA_ROWS):
            start = NA_KH - 1 - rq
            if start % 2 == 0:
                u = ue_ref[half, :, start * GRID_W:start * GRID_W + NA_WIN]
            else:
                u = uo_ref[half, :, (start - 1) * GRID_W:(start - 1) * GRID_W + NA_WIN]
            r = r0 + rq
            first = jnp.clip(r - NA_KH // 2, 0, n_rows - NA_KH)
            lane0 = (first - r0 + NA_PAD_ROWS) * GRID_W
            in_rows = (klane >= lane0) & (klane < lane0 + NA_KH * GRID_W)
            pieces.append(jnp.where(in_rows, s[rq * GRID_W:(rq + 1) * GRID_W] + u, NEG))
        outs.append(_softmax_av([jnp.concatenate(pieces, axis=0), s_ctx], [v, cv]))
    o_ref[...] = jnp.where(lo, outs[0], outs[1])


def _na_bias_tables(rpb):
    cq = jnp.arange(GRID_W)
    ck = jnp.arange(GRID_W)
    dc = jnp.clip(ck[None] - cq[:, None], -(NA_KW - 1), NA_KW - 1) + NA_KW - 1
    cs = jnp.clip(cq - NA_KW // 2, 0, GRID_W - NA_KW)
    col_ok = (ck[None] >= cs[:, None]) & (ck[None] < cs[:, None] + NA_KW)
    t = rpb.astype(F32)[:, :, dc]
    t = jnp.where(col_ok[None, None], t, NEG).transpose(0, 2, 1, 3)
    n_dr = 2 * NA_KH - 1
    blocks = NA_TABLE // GRID_W
    t = jnp.pad(t, ((0, 0), (0, 0), (NA_PAD_ROWS, blocks - n_dr - NA_PAD_ROWS), (0, 0)), constant_values=NEG)
    ue = t.reshape(C_HEADS, GRID_W, NA_TABLE)
    uo = jnp.concatenate([ue[..., GRID_W:], jnp.full((C_HEADS, GRID_W, GRID_W), NEG, F32)], axis=-1)
    return ue, uo


def _na_attention(p, cache_k, cache_v, rpb):
    npairs = C_HEADS // 2
    nrb = DEC_SEQ // NA_QROWS
    half = NA_QROWS // 2
    qbase = NP_TOK // NA_QROWS
    hbase = NP_TOK // half
    kcol = C_W // LANES
    ue, uo = _na_bias_tables(rpb)

    def main_spec(col0):
        return pl.BlockSpec((NA_QROWS, LANES), lambda b, h, r: (qbase + b * nrb + r, col0 + h))

    def side_spec(col0, off):
        return pl.BlockSpec((half, LANES),
                            lambda b, h, r: (hbase + b * 2 * nrb + jnp.clip(2 * r + off, 0, 2 * nrb - 1), col0 + h))

    ctx_spec = pl.BlockSpec((None, PAST, LANES), lambda b, h, r: (b, 0, h))
    tab_spec = pl.BlockSpec((2, GRID_W, NA_TABLE), lambda b, h, r: (h, 0, 0))
    return pl.pallas_call(
        _na_kernel,
        grid=(DEC_BATCH, npairs, nrb),
        in_specs=[main_spec(0),
                  side_spec(kcol, -1), main_spec(kcol), side_spec(kcol, 2),
                  side_spec(2 * kcol, -1), main_spec(2 * kcol), side_spec(2 * kcol, 2),
                  ctx_spec, ctx_spec, tab_spec, tab_spec],
        out_specs=pl.BlockSpec((NA_QROWS, LANES), lambda b, h, r: (b * nrb + r, h)),
        out_shape=jax.ShapeDtypeStruct((NS_TOK, C_W), F32),
        compiler_params=_cparams(3),
        name="na_attn",
    )(p, p, p, p, p, p, p, cache_k.reshape(DEC_BATCH, PAST, C_W), cache_v.reshape(DEC_BATCH, PAST, C_W), ue, uo)


def _route(biased, scores):
    t = biased.shape[1]
    per_group = N_EXPERTS // N_GROUPS
    i8 = lax.broadcasted_iota(jnp.int32, (per_group, t), 0)
    g_rows = []
    for g in range(N_GROUPS):
        bg = biased[g * per_group:(g + 1) * per_group]
        m1 = bg.max(axis=0, keepdims=True)
        first = jnp.where(bg == m1, i8, per_group).min(axis=0, keepdims=True)
        m2 = jnp.where(i8 == first, -jnp.inf, bg).max(axis=0, keepdims=True)
        g_rows.append(m1 + m2)
    g_top = jnp.concatenate(g_rows, axis=0)
    gi = lax.broadcasted_iota(jnp.int32, g_top.shape, 0)
    g_sel = jnp.zeros(g_top.shape, jnp.int32)
    cur = g_top
    for _ in range(TOPK_GROUPS):
        m = cur.max(axis=0, keepdims=True)
        hit = gi == jnp.where(cur == m, gi, N_GROUPS).min(axis=0, keepdims=True)
        g_sel = jnp.where(hit, 1, g_sel)
        cur = jnp.where(hit, -jnp.inf, cur)
    e_sel = jnp.concatenate([jnp.broadcast_to(g_sel[g:g + 1], (per_group, t)) for g in range(N_GROUPS)], axis=0)
    cur = jnp.where(e_sel > 0, biased, NEG)
    ei = lax.broadcasted_iota(jnp.int32, cur.shape, 0)
    ids, gates, hits = [], [], []
    for _ in range(TOP_K):
        m = cur.max(axis=0, keepdims=True)
        f = jnp.where(cur == m, ei, N_EXPERTS).min(axis=0, keepdims=True)
        hit = ei == f
        ids.append(f)
        hits.append(hit)
        gates.append(jnp.where(hit, scores, 0.0).sum(axis=0, keepdims=True))
        cur = jnp.where(hit, -jnp.inf, cur)
    gate = jnp.concatenate(gates, axis=0)
    gate = gate / gate.sum(axis=0, keepdims=True) * ROUTED_SCALE
    return jnp.concatenate(ids, axis=0), gate, hits


def _pack_bf16_pairs(h):
    bits = lax.bitcast_convert_type(h.astype(BF16).astype(F32), jnp.uint32)
    return bits[:, :D // 2] | (bits[:, D // 2:] >> 16)


def _unpack_bf16_pairs(xp):
    hi = lax.bitcast_convert_type(xp & jnp.uint32(0xFFFF0000), F32).astype(BF16)
    lo = lax.bitcast_convert_type(xp << 16, F32).astype(BF16)
    return hi, lo


def _dot_halves(hi, lo, w_ref):
    return (jnp.dot(hi, w_ref[:D // 2, :], preferred_element_type=F32)
            + jnp.dot(lo, w_ref[D // 2:, :], preferred_element_type=F32))


def _outproj_kernel(*refs, n_parts):
    x_ref = refs[0]
    part_refs = refs[1:1 + 3 * n_parts]
    gate_ref, shift_ref, scale_ref, g2_ref, rw_ref, rb_ref = refs[1 + 3 * n_parts:7 + 3 * n_parts]
    xo_ref, h_ref, idx_ref, wgt_ref, rank_ref, cnt_ref = refs[7 + 3 * n_parts:]
    is_prompt = pl.program_id(0) < NP_TOK // TM

    @pl.when(pl.program_id(0) == 0)
    def _():
        cnt_ref[...] = jnp.zeros_like(cnt_ref)

    y = None
    for t in range(n_parts):
        ap_ref, as_ref, w_ref = part_refs[3 * t:3 * t + 3]
        a = jnp.where(is_prompt, ap_ref[...], as_ref[...]).astype(BF16)
        d = jnp.dot(a, w_ref[...], preferred_element_type=F32)
        y = d if y is None else y + d
    x = x_ref[...] + gate_ref[...] * y
    xo_ref[...] = x
    h = _rms(x, g2_ref[...]) * (1.0 + scale_ref[...]) + shift_ref[...]
    h_ref[...] = _pack_bf16_pairs(h)
    logits = lax.dot_general(rw_ref[...], h, NT_DIMS, preferred_element_type=F32, precision=HIGHEST)
    scores = jax.nn.sigmoid(logits)
    idx, gate, hits = _route(scores + rb_ref[...], scores)
    idx_ref[...] = idx
    wgt_ref[...] = gate
    chosen = hits[0]
    for hit in hits[1:]:
        chosen = chosen | hit
    m = jnp.where(chosen, 1.0, 0.0)
    before = (lax.broadcasted_iota(jnp.int32, (TM, TM), 0) < lax.broadcasted_iota(jnp.int32, (TM, TM), 1))
    prefix = jnp.dot(m.astype(BF16), jnp.where(before, 1.0, 0.0).astype(BF16), preferred_element_type=F32)
    rank_all = prefix + cnt_ref[...]
    rank_ref[...] = jnp.concatenate(
        [jnp.where(hit, rank_all, 0.0).sum(axis=0, keepdims=True) for hit in hits], axis=0).astype(jnp.int32)
    cnt_ref[...] += m.sum(axis=1, keepdims=True)


def _outproj(x, parts, mod, g2, router_w, router_b):
    npb = NP_TOK // TM
    nsb = NS_TOK // TM
    in_specs = [pl.BlockSpec((TM, D), lambda i: (i, 0))]
    args = [x]
    for ap, a_s, w in parts:
        width = ap.shape[1]
        in_specs += [pl.BlockSpec((TM, width), lambda i: (jnp.minimum(i, npb - 1), 0)),
                     pl.BlockSpec((TM, width), lambda i: (jnp.clip(i - npb, 0, nsb - 1), 0)),
                     pl.BlockSpec((width, D), lambda i: (0, 0))]
        args += [ap, a_s, w]
    in_specs += [_mod_spec(2), _mod_spec(3), _mod_spec(4),
                 pl.BlockSpec((1, D), lambda i: (0, 0)),
                 pl.BlockSpec((N_EXPERTS, D), lambda i: (0, 0)),
                 pl.BlockSpec((N_EXPERTS, 1), lambda i: (0, 0))]
    args += [mod, mod, mod, g2.reshape(1, D), router_w.T, router_b.reshape(N_EXPERTS, 1)]
    return pl.pallas_call(
        functools.partial(_outproj_kernel, n_parts=len(parts)),
        grid=(N_TOK // TM,),
        in_specs=in_specs,
        out_specs=[pl.BlockSpec((TM, D), lambda i: (i, 0)),
                   pl.BlockSpec((TM, D // 2), lambda i: (i, 0)),
                   pl.BlockSpec((TOP_K, TM), lambda i: (0, i)),
                   pl.BlockSpec((TOP_K, TM), lambda i: (0, i)),
                   pl.BlockSpec((TOP_K, TM), lambda i: (0, i)),
                   pl.BlockSpec((N_EXPERTS, 1), lambda i: (0, 0))],
        out_shape=[jax.ShapeDtypeStruct((N_TOK, D), F32),
                   jax.ShapeDtypeStruct((N_TOK, D // 2), jnp.uint32),
                   jax.ShapeDtypeStruct((TOP_K, N_TOK), jnp.int32),
                   jax.ShapeDtypeStruct((TOP_K, N_TOK), F32),
                   jax.ShapeDtypeStruct((TOP_K, N_TOK), jnp.int32),
                   jax.ShapeDtypeStruct((N_EXPERTS, 1), F32)],
        compiler_params=_cparams(1),
        name="outproj_router",
    )(*args)


def _experts_kernel(be_ref, nu_ref, x_ref, w1_ref, w3_ref, w2_ref, o_ref, w1b, w3b, w2b):
    i = pl.program_id(0)
    e = be_ref[i]
    prev = be_ref[jnp.maximum(i - 1, 0)]

    @pl.when((i == 0) | (e != prev))
    def _():
        w1b[...] = w1_ref[...].astype(BF16)
        w3b[...] = w3_ref[...].astype(BF16)
        w2b[...] = w2_ref[...].astype(BF16)

    @pl.when(i < nu_ref[0])
    def _():
        hi, lo = _unpack_bf16_pairs(x_ref[...])
        a = _dot_halves(hi, lo, w1b)
        b = _dot_halves(hi, lo, w3b)
        h = (_silu(a) * b).astype(BF16)
        o_ref[...] = jnp.dot(h, w2b[...], preferred_element_type=F32)

    @pl.when(i >= nu_ref[0])
    def _():
        o_ref[...] = jnp.zeros_like(o_ref)


def _experts(block_e, n_used, x_sorted, w1, w3, w2, layer):
    return pl.pallas_call(
        _experts_kernel,
        grid_spec=pltpu.PrefetchScalarGridSpec(
            num_scalar_prefetch=2,
            grid=(N_MOE_BLOCKS,),
            in_specs=[pl.BlockSpec((MOE_BLOCK, D // 2), lambda i, be, nu: (i, 0)),
                      pl.BlockSpec((None, None, D, FF), lambda i, be, nu: (layer, be[i], 0, 0)),
                      pl.BlockSpec((None, None, D, FF), lambda i, be, nu: (layer, be[i], 0, 0)),
                      pl.BlockSpec((None, None, FF, D), lambda i, be, nu: (layer, be[i], 0, 0))],
            out_specs=pl.BlockSpec((MOE_BLOCK, D), lambda i, be, nu: (i, 0)),
            scratch_shapes=[pltpu.VMEM((D, FF), BF16), pltpu.VMEM((D, FF), BF16), pltpu.VMEM((FF, D), BF16)]),
        out_shape=jax.ShapeDtypeStruct((N_MOE_BLOCKS * MOE_BLOCK, D), F32),
        compiler_params=_cparams(1),
        name="experts",
    )(block_e, n_used, x_sorted, w1, w3, w2)


def _dispatch_plan(idx_t, rank_t, counts):
    counts = counts.astype(jnp.int32)
    padded = (counts + MOE_BLOCK - 1) // MOE_BLOCK * MOE_BLOCK
    pad_end = jnp.cumsum(padded)
    pad_start = pad_end - padded
    start = jnp.cumsum(counts) - counts
    block_row0 = jnp.arange(N_MOE_BLOCKS, dtype=jnp.int32) * MOE_BLOCK
    block_e = jnp.minimum((pad_end[None, :] <= block_row0[:, None]).sum(axis=1), N_EXPERTS - 1).astype(jnp.int32)
    n_used = (pad_end[-1] // MOE_BLOCK).astype(jnp.int32).reshape(1)
    key = idx_t * N_TOK + jnp.arange(N_TOK, dtype=jnp.int32)[None, :]
    tok_sorted = jnp.sort(key.reshape(N_ASSIGN)) % N_TOK
    local = (block_row0 - pad_start[block_e])[:, None] + jnp.arange(MOE_BLOCK, dtype=jnp.int32)[None, :]
    valid = local < counts[block_e][:, None]
    src = jnp.clip(start[block_e][:, None] + local, 0, N_ASSIGN - 1)
    filler = (block_row0[:, None] + jnp.arange(MOE_BLOCK, dtype=jnp.int32)[None, :]) % N_TOK
    row_tok = jnp.where(valid, tok_sorted[src], filler).reshape(N_MOE_BLOCKS * MOE_BLOCK).astype(jnp.int32)
    e_iota = jnp.arange(N_EXPERTS, dtype=jnp.int32)[:, None, None]
    dest = jnp.where(idx_t[None] == e_iota, pad_start[:, None, None], 0).sum(axis=0) + rank_t
    return row_tok, block_e, n_used, dest.astype(jnp.int32)


SC_CORES = 2
SC_SUBCORES = 16
SC_WORKERS = SC_CORES * SC_SUBCORES
SC_CHUNK_BYTES = 64 * 1024
SC_SLOTS = 4


def _sc_gather(table, idx):
    n_idx = idx.shape[0]
    width = table.shape[1]
    chunk = SC_CHUNK_BYTES // (4 * width)
    per_worker = n_idx // SC_WORKERS
    n_chunks = per_worker // chunk
    ahead = SC_SLOTS - 1
    assert per_worker * SC_WORKERS == n_idx and n_chunks * chunk == per_worker and n_chunks % SC_SLOTS == 0
    mesh = plsc.VectorSubcoreMesh(core_axis_name="c", subcore_axis_name="s")

    @functools.partial(
        pl.kernel, mesh=mesh,
        out_type=jax.ShapeDtypeStruct((n_idx, width), table.dtype),
        scratch_types=[pltpu.VMEM((per_worker,), jnp.int32),
                       pltpu.VMEM((SC_SLOTS, chunk, width), table.dtype),
                       pltpu.SemaphoreType.DMA((SC_SLOTS,)),
                       pltpu.SemaphoreType.DMA((SC_SLOTS,))])
    def gather(t_hbm, i_hbm, o_hbm, idx_v, rows_v, gsem, wsem):
        worker = lax.axis_index("s") * SC_CORES + lax.axis_index("c")
        base = worker * per_worker
        pltpu.sync_copy(i_hbm.at[pl.ds(pl.multiple_of(base, chunk), per_worker)], idx_v)

        def gather_copy(c, b):
            ids = idx_v.at[pl.ds(pl.multiple_of(c * chunk, chunk), chunk)]
            return pltpu.make_async_copy(t_hbm.at[ids], rows_v.at[b], gsem.at[b])

        def write_copy(c, b):
            rows = pl.ds(pl.multiple_of(base + c * chunk, chunk), chunk)
            return pltpu.make_async_copy(rows_v.at[b], o_hbm.at[rows], wsem.at[b])

        for c in range(ahead):
            gather_copy(c, c).start()

        @pl.loop(0, n_chunks, step=SC_SLOTS)
        def _(c0):
            for b in range(SC_SLOTS):
                c = c0 + b
                refill = (b + ahead) % SC_SLOTS
                gather_copy(c, b).wait()
                write_copy(c, b).start()

                @pl.when(c > 0)
                def _():
                    write_copy(c - 1, refill).wait()

                @pl.when(c + ahead < n_chunks)
                def _():
                    gather_copy(c + ahead, refill).start()

        write_copy(n_chunks - 1, (n_chunks - 1) % SC_SLOTS).wait()

    return gather(table, idx)


TC = 256


def _combine_kernel(x_ref, h_ref, y_ref, wgt_ref, gate_ref, w1_ref, w3_ref, w2_ref, fg_ref, o_ref, *, final):
    hi, lo = _unpack_bf16_pairs(h_ref[...])
    a = _dot_halves(hi, lo, w1_ref)
    b = _dot_halves(hi, lo, w3_ref)
    ffn = jnp.dot((_silu(a) * b).astype(BF16), w2_ref[...], preferred_element_type=F32)
    wgt = wgt_ref[...]
    for k in range(TOP_K):
        ffn = ffn + y_ref[k] * wgt[:, k:k + 1]
    x = x_ref[...] + gate_ref[...] * ffn
    o_ref[...] = _rms(x, fg_ref[...]) if final else x


def _combine(x, h, y_rows, wgt, mod, sw1, sw3, sw2, final_g, final):
    row = pl.BlockSpec((TC, D), lambda i: (i, 0))
    return pl.pallas_call(
        functools.partial(_combine_kernel, final=final),
        grid=(N_TOK // TC,),
        in_specs=[row,
                  pl.BlockSpec((TC, D // 2), lambda i: (i, 0)),
                  pl.BlockSpec((TOP_K, TC, D), lambda i: (0, i, 0)),
                  pl.BlockSpec((TC, TOP_K), lambda i: (i, 0)),
                  _mod_spec(5, TC),
                  pl.BlockSpec((D, FF), lambda i: (0, 0)),
                  pl.BlockSpec((D, FF), lambda i: (0, 0)),
                  pl.BlockSpec((FF, D), lambda i: (0, 0)),
                  pl.BlockSpec((1, D), lambda i: (0, 0))],
        out_specs=row,
        out_shape=jax.ShapeDtypeStruct((N_TOK, D), F32),
        compiler_params=_cparams(1),
        name="combine",
    )(x, h, y_rows, wgt, mod, sw1.astype(BF16), sw3.astype(BF16), sw2.astype(BF16), final_g.reshape(1, D))


def kernel(x_prompt, x_sample, cache_a_k, cache_a_v, state_ret_fwd, state_ret_bwd, cache_c_k, cache_c_v,
           c, c_ctx, norm1_g, norm2_g, ada_w, ada_b, even_w_in, even_w_out, sink_a, ret_decay_fwd,
           ret_decay_bwd, ret_gn_g, odd_w_in, odd_w_out, na_rpb, router_w, router_b, exp_w1, exp_w3,
           exp_w2, sh_w1, sh_w3, sh_w2, final_g):
    x = jnp.concatenate([x_prompt.reshape(NP_TOK, D), x_sample.reshape(NS_TOK, D)], axis=0)
    cc = jnp.concatenate([c_ctx[None], c, jnp.zeros((8 - 1 - DEC_BATCH, D), F32)], axis=0)
    rope = _rope_tables()
    outs = {}
    for l in range(2):
        mod = _ada(cc, ada_w, ada_b, l)
        if l == 0:
            p = _inproj(x, norm1_g[l], mod, even_w_in[0].astype(BF16), rope, A_Q + A_KV)
            oa_p = _ctx_gqa(p, sink_a[0])
            oa_s = _win_attention(p, cache_a_k[:, 0], cache_a_v[:, 0], sink_a[0])
            zero = jnp.zeros((BATCH, B_HEADS // 2, LANES, LANES), F32)
            ob_p, sf, sb = _retention(p, 0, BATCH, SEQ, ret_decay_fwd[0], ret_decay_bwd[0], ret_gn_g[0], zero, zero)
            ob_s, _, _ = _retention(p, NP_TOK, DEC_BATCH, DEC_SEQ, ret_decay_fwd[0], ret_decay_bwd[0], ret_gn_g[0],
                                    _blockdiag_states(state_ret_fwd[:, 0]), _blockdiag_states(state_ret_bwd[:, 0]))
            w_out = even_w_out[0].astype(BF16)
            parts = [(oa_p, oa_s, w_out[:A_Q]), (ob_p, ob_s, w_out[A_Q:])]
            outs["a_k"] = p[:NP_TOK, A_Q:A_Q + A_KV].reshape(BATCH, 1, SEQ, A_KV_HEADS, HD)
            outs["a_v"] = p[:NP_TOK, A_Q + A_KV:A_Q + 2 * A_KV].reshape(BATCH, 1, SEQ, A_KV_HEADS, HD)
            outs["r_f"] = _diag_states(sf).reshape(BATCH, 1, B_HEADS, HD, HD)
            outs["r_b"] = _diag_states(sb).reshape(BATCH, 1, B_HEADS, HD, HD)
        else:
            p = _inproj(x, norm1_g[l], mod, odd_w_in[0].astype(BF16), rope, 0)
            o_p = _ctx_mha(p)
            o_s = _na_attention(p, cache_c_k[:, 0], cache_c_v[:, 0], na_rpb[0])
            parts = [(o_p, o_s, odd_w_out[0].astype(BF16))]
            outs["c_k"] = p[:NP_TOK, C_W:2 * C_W].reshape(BATCH, 1, SEQ, C_HEADS, HD)
            outs["c_v"] = p[:NP_TOK, 2 * C_W:3 * C_W].reshape(BATCH, 1, SEQ, C_HEADS, HD)
        x, h, idx_t, gate_t, rank_t, counts = _outproj(x, parts, mod, norm2_g[l], router_w[l], router_b[l])
        row_tok, block_e, n_used, dest = _dispatch_plan(idx_t, rank_t, counts[:, 0])
        y = _experts(block_e, n_used, _sc_gather(h, row_tok), exp_w1, exp_w3, exp_w2, l)
        y_rows = _sc_gather(y, dest.reshape(N_ASSIGN)).reshape(TOP_K, N_TOK, D)
        x = _combine(x, h, y_rows, gate_t.T, mod, sh_w1[l], sh_w3[l], sh_w2[l], final_g, final=(l == 1))
    y_prompt = x[:NP_TOK].reshape(BATCH, SEQ, D)
    y_sample = x[NP_TOK:].reshape(DEC_BATCH, DEC_SEQ, D)
    return (y_prompt, y_sample, outs["a_k"], outs["a_v"], outs["r_f"], outs["r_b"], outs["c_k"], outs["c_v"])
```

```python
import functools
import math

import jax
import jax.numpy as jnp
from jax import lax
from jax.experimental import pallas as pl
from jax.experimental.pallas import tpu as pltpu
from jax.experimental.pallas import tpu_sc as plsc

F32 = jnp.float32
BF16 = jnp.bfloat16
HIGHEST = lax.Precision.HIGHEST

D = 1024
BATCH = 32
SEQ = 256
DEC_BATCH = 4
DEC_SEQ = 4096
PAST = 256
GRID_W = 64
HD = 64
EPS = 1e-6
NEG = -1e30
ROPE_BASE = 10000.0
A_HEADS = 8
A_KV_HEADS = 2
A_Q = A_HEADS * HD
A_KV = A_KV_HEADS * HD
B_HEADS = 8
B_W = B_HEADS * HD
EVEN_IN = A_Q + 2 * A_KV + 4 * B_W
C_HEADS = 16
C_W = C_HEADS * HD
NA_KH = 8
NA_KW = 16
N_EXPERTS = 64
TOP_K = 8
N_GROUPS = 8
TOPK_GROUPS = 4
FF = 256
ROUTED_SCALE = 2.5
MOE_BLOCK = 512
RET_CHUNK = 128
RET_UNROLL = 2
A_BLOCK = 128

NP_TOK = BATCH * SEQ
NS_TOK = DEC_BATCH * DEC_SEQ
N_TOK = NP_TOK + NS_TOK
N_ASSIGN = N_TOK * TOP_K
N_MOE_BLOCKS = (N_ASSIGN + N_EXPERTS * (MOE_BLOCK - 1) + MOE_BLOCK - 1) // MOE_BLOCK
PLAN_LANES = 512
assert N_TOK % MOE_BLOCK == 0 and N_MOE_BLOCKS <= PLAN_LANES

LANES = 128
TM = 512
NA_ROWS = 8
V7X_VMEM_LIMIT = 56 * 1024 * 1024

NT_DIMS = (((1,), (1,)), ((), ()))


def _cparams(n_axes, vmem=V7X_VMEM_LIMIT):
    return pltpu.CompilerParams(dimension_semantics=("arbitrary",) * n_axes, vmem_limit_bytes=vmem)


def _seg_of_block(i, rows):
    row0 = i * rows
    return jnp.where(row0 < NP_TOK, 0, 1 + (row0 - NP_TOK) // DEC_SEQ)


def _mod_spec(chunk, rows=TM):
    return pl.BlockSpec((None, 1, D), lambda i: (_seg_of_block(i, rows), 0, chunk))


def _silu(x):
    return x * jax.nn.sigmoid(x)


def _rms(x, g):
    return x * lax.rsqrt(jnp.mean(x * x, axis=-1, keepdims=True) + EPS) * g


def _lane_lo():
    return lax.broadcasted_iota(jnp.int32, (1, LANES), 1) < HD


def _ada_kernel(c_ref, w_ref, b_ref, o_ref):
    a = _silu(c_ref[...])
    o_ref[...] = jnp.dot(a, w_ref[...], preferred_element_type=F32, precision=HIGHEST) + b_ref[...]


def _ada(cc, w, b, layer):
    tn = 1536
    out = pl.pallas_call(
        _ada_kernel,
        grid=(6 * D // tn,),
        in_specs=[pl.BlockSpec((8, D), lambda j: (0, 0)),
                  pl.BlockSpec((None, D, tn), lambda j: (layer, 0, j)),
                  pl.BlockSpec((None, 1, tn), lambda j: (layer, 0, j))],
        out_specs=pl.BlockSpec((8, tn), lambda j: (0, j)),
        out_shape=jax.ShapeDtypeStruct((8, 6 * D), F32),
        compiler_params=_cparams(1),
        name="ada",
    )(cc, w, b.reshape(b.shape[0], 1, 6 * D))
    return out.reshape(8, 1, 6 * D)


def _inproj_kernel(x_ref, g_ref, shift_ref, scale_ref, w_ref, cos_ref, sin_ref, o_ref, *, rope_cols):
    h = _rms(x_ref[...], g_ref[...]) * (1.0 + scale_ref[...]) + shift_ref[...]
    o = jnp.dot(h.astype(BF16), w_ref[...], preferred_element_type=F32)
    if rope_cols:
        cos = cos_ref[...]
        sin = sin_ref[...]
        lane = lax.broadcasted_iota(jnp.int32, (1, LANES), 1)
        first = (lane % 32) < 16
        for c in range(rope_cols // LANES):
            oc = o[:, c * LANES:(c + 1) * LANES]
            partner = jnp.where(first, pltpu.roll(oc, LANES - 16, 1), pltpu.roll(oc, 16, 1))
            o_ref[:, c * LANES:(c + 1) * LANES] = oc * cos + partner * sin
        o_ref[:, rope_cols:] = o[:, rope_cols:]
    else:
        o_ref[...] = o


def _rope_tables():
    half = HD // 2
    inv = ROPE_BASE ** (-jnp.arange(0, half, 2, dtype=F32) / half)
    t = jnp.arange(DEC_SEQ)
    ang_r = (t // GRID_W).astype(F32)[:, None] * inv[None]
    ang_c = (t % GRID_W).astype(F32)[:, None] * inv[None]

    def head(fn_r, fn_c, sign):
        return jnp.concatenate([sign[0] * fn_r, sign[1] * fn_r, sign[0] * fn_c, sign[1] * fn_c], axis=-1)

    cos = head(jnp.cos(ang_r), jnp.cos(ang_c), (1.0, 1.0))
    sin = head(jnp.sin(ang_r), jnp.sin(ang_c), (-1.0, 1.0))
    cos = jnp.concatenate([jnp.ones((TM, HD), F32), cos], axis=0)
    sin = jnp.concatenate([jnp.zeros((TM, HD), F32), sin], axis=0)
    return jnp.tile(cos, (1, 2)), jnp.tile(sin, (1, 2))


def _inproj(x, g, mod, w_bf16, rope, rope_cols):
    n_out = w_bf16.shape[1]
    npb = NP_TOK // TM
    spb = DEC_SEQ // TM

    def rope_map(i):
        return (jnp.where(i < npb, 0, 1 + (i - npb) % spb), 0)

    return pl.pallas_call(
        functools.partial(_inproj_kernel, rope_cols=rope_cols),
        grid=(N_TOK // TM,),
        in_specs=[pl.BlockSpec((TM, D), lambda i: (i, 0)),
                  pl.BlockSpec((1, D), lambda i: (0, 0)),
                  _mod_spec(0), _mod_spec(1),
                  pl.BlockSpec((D, n_out), lambda i: (0, 0)),
                  pl.BlockSpec((TM, LANES), rope_map),
                  pl.BlockSpec((TM, LANES), rope_map)],
        out_specs=pl.BlockSpec((TM, n_out), lambda i: (i, 0)),
        out_shape=jax.ShapeDtypeStruct((N_TOK, n_out), F32),
        compiler_params=_cparams(1),
        name="inproj",
    )(x, g.reshape(1, D), mod, mod, w_bf16, rope[0], rope[1])


def _softmax_av(s_list, v_list, sink=None):
    mx = s_list[0].max(axis=-1, keepdims=True)
    for s in s_list[1:]:
        mx = jnp.maximum(mx, s.max(axis=-1, keepdims=True))
    if sink is not None:
        mx = jnp.maximum(mx, sink)
    den = jnp.exp(sink - mx) if sink is not None else 0.0
    acc = None
    for s, v in zip(s_list, v_list):
        p = jnp.exp(s - mx)
        den = den + p.sum(axis=-1, keepdims=True)
        pv = jnp.dot(p.astype(BF16), v, preferred_element_type=F32)
        acc = pv if acc is None else acc + pv
    return acc / den


def _dup_half(x, j, lo):
    xr = pltpu.roll(x, HD, 1)
    return jnp.where(lo, x, xr) if j == 0 else jnp.where(lo, xr, x)


def _stack_heads(q_ref, heads, lo, scale):
    parts = []
    for h in heads:
        qp = q_ref[:, (h // 2) * LANES:(h // 2 + 1) * LANES]
        keep = lo if h % 2 == 0 else jnp.logical_not(lo)
        parts.append(jnp.where(keep, qp, 0.0) * scale)
    return jnp.concatenate(parts, axis=0).astype(BF16)


def _sink_column(sink_ref, heads, rows):
    return jnp.concatenate([jnp.full((rows, 1), sink_ref[h], F32) for h in heads], axis=0)


def _ctx_gqa_kernel(sink_ref, q_ref, k_ref, v_ref, o_ref):
    lo = _lane_lo()
    k = k_ref[...]
    v = v_ref[...]
    group = A_HEADS // A_KV_HEADS
    for j in range(A_KV_HEADS):
        heads = list(range(group * j, group * (j + 1)))
        kd = _dup_half(k, j, lo).astype(BF16)
        vd = _dup_half(v, j, lo).astype(BF16)
        q = _stack_heads(q_ref, heads, lo, HD ** -0.5)
        s = lax.dot_general(q, kd, NT_DIMS, preferred_element_type=F32)
        o = _softmax_av([s], [vd], _sink_column(sink_ref, heads, SEQ))
        for t in range(group // 2):
            pair = heads[2 * t] // 2
            o_ref[:, pair * LANES:(pair + 1) * LANES] = jnp.where(
                lo, o[(2 * t) * SEQ:(2 * t + 1) * SEQ], o[(2 * t + 1) * SEQ:(2 * t + 2) * SEQ])


def _ctx_gqa(p, sink):
    return pl.pallas_call(
        _ctx_gqa_kernel,
        grid_spec=pltpu.PrefetchScalarGridSpec(
            num_scalar_prefetch=1,
            grid=(BATCH,),
            in_specs=[pl.BlockSpec((SEQ, A_Q), lambda b, s: (b, 0)),
                      pl.BlockSpec((SEQ, A_KV), lambda b, s: (b, A_Q // A_KV)),
                      pl.BlockSpec((SEQ, A_KV), lambda b, s: (b, A_Q // A_KV + 1))],
            out_specs=pl.BlockSpec((SEQ, A_Q), lambda b, s: (b, 0))),
        out_shape=jax.ShapeDtypeStruct((NP_TOK, A_Q), F32),
        compiler_params=_cparams(1),
        name="ctx_gqa",
    )(sink, p, p, p)


def _win_kernel(sink_ref, q_ref, kp_ref, kc_ref, kn_ref, vp_ref, vc_ref, vn_ref, ck_ref, cv_ref, o_ref):
    i = pl.program_id(1)
    lo = _lane_lo()
    k = jnp.concatenate([kp_ref[...], kc_ref[...], kn_ref[...]], axis=0)
    v = jnp.concatenate([vp_ref[...], vc_ref[...], vn_ref[...]], axis=0)
    ck = ck_ref[...]
    cv = cv_ref[...]
    group = A_HEADS // A_KV_HEADS
    qpos = i * A_BLOCK + lax.broadcasted_iota(jnp.int32, (A_BLOCK, 3 * A_BLOCK), 0)
    kpos = (i - 1) * A_BLOCK + lax.broadcasted_iota(jnp.int32, (A_BLOCK, 3 * A_BLOCK), 1)
    valid = (jnp.abs(kpos - qpos) <= A_BLOCK) & (kpos >= 0) & (kpos < DEC_SEQ)
    valid = jnp.concatenate([valid] * group, axis=0)
    for j in range(A_KV_HEADS):
        heads = list(range(group * j, group * (j + 1)))
        kd = _dup_half(k, j, lo).astype(BF16)
        vd = _dup_half(v, j, lo).astype(BF16)
        ckd = _dup_half(ck, j, lo).astype(BF16)
        cvd = _dup_half(cv, j, lo).astype(BF16)
        q = _stack_heads(q_ref, heads, lo, HD ** -0.5)
        s_loc = jnp.where(valid, lax.dot_general(q, kd, NT_DIMS, preferred_element_type=F32), NEG)
        s_ctx = lax.dot_general(q, ckd, NT_DIMS, preferred_element_type=F32)
        o = _softmax_av([s_loc, s_ctx], [vd, cvd], _sink_column(sink_ref, heads, A_BLOCK))
        for t in range(group // 2):
            pair = heads[2 * t] // 2
            o_ref[:, pair * LANES:(pair + 1) * LANES] = jnp.where(
                lo, o[(2 * t) * A_BLOCK:(2 * t + 1) * A_BLOCK], o[(2 * t + 1) * A_BLOCK:(2 * t + 2) * A_BLOCK])


def _win_attention(p, cache_k, cache_v, sink):
    nblk = DEC_SEQ // A_BLOCK
    base = NP_TOK // A_BLOCK
    kcol = A_Q // A_KV

    def kv_spec(col, off):
        return pl.BlockSpec((A_BLOCK, A_KV),
                            lambda b, i, s: (base + b * nblk + jnp.clip(i + off, 0, nblk - 1), col))

    ctx_spec = pl.BlockSpec((None, PAST, A_KV), lambda b, i, s: (b, 0, 0))
    return pl.pallas_call(
        _win_kernel,
        grid_spec=pltpu.PrefetchScalarGridSpec(
            num_scalar_prefetch=1,
            grid=(DEC_BATCH, nblk),
            in_specs=[pl.BlockSpec((A_BLOCK, A_Q), lambda b, i, s: (base + b * nblk + i, 0)),
                      kv_spec(kcol, -1), kv_spec(kcol, 0), kv_spec(kcol, 1),
                      kv_spec(kcol + 1, -1), kv_spec(kcol + 1, 0), kv_spec(kcol + 1, 1),
                      ctx_spec, ctx_spec],
            out_specs=pl.BlockSpec((A_BLOCK, A_Q), lambda b, i, s: (b * nblk + i, 0))),
        out_shape=jax.ShapeDtypeStruct((NS_TOK, A_Q), F32),
        compiler_params=_cparams(2),
        name="win_attn",
    )(sink, p, p, p, p, p, p, p, cache_k.reshape(DEC_BATCH, PAST, A_KV), cache_v.reshape(DEC_BATCH, PAST, A_KV))


def _ret_kernel(df_ref, db_ref, q_ref, k_ref, v_ref, g_ref, gn_ref, s0f_ref, s0b_ref,
                o_ref, sf_ref, sb_ref, of_scr, *, length):
    c_len = RET_CHUNK
    n = length // c_len
    lo = _lane_lo()
    hi = jnp.logical_not(lo)
    row = lax.broadcasted_iota(jnp.int32, (c_len, c_len), 0)
    col = lax.broadcasted_iota(jnp.int32, (c_len, c_len), 1)
    rowp = lax.broadcasted_iota(jnp.int32, (LANES, LANES), 0)
    colp = lax.broadcasted_iota(jnp.int32, (LANES, LANES), 1)
    blockdiag = (rowp < HD) == (colp < HD)
    idx = lax.broadcasted_iota(jnp.int32, (c_len, 1), 0).astype(F32)

    def direction(dec_ref, forward):
        lg = -jnp.exp(dec_ref[...])
        diff = (row - col) if forward else (col - row)
        keep = (diff >= 0) if forward else (diff > 0)
        dist = jnp.maximum(diff, 0).astype(F32)
        dm = [jnp.where(keep, jnp.exp(dist * lg[:, off:off + 1]), 0.0) for off in (0, HD)]
        if forward:
            xi = jnp.exp((idx + 1.0) * lg)
            zeta = jnp.exp((c_len - 1.0 - idx) * lg)
        else:
            xi = jnp.exp((c_len - idx) * lg)
            zeta = jnp.exp(idx * lg)
        return dm, xi, zeta, jnp.exp(c_len * lg)

    def chunk(c, state, consts):
        dm, xi, zeta, gch = consts
        rows = pl.ds(pl.multiple_of(c * c_len, c_len), c_len)
        qc = q_ref[rows, :]
        kc = k_ref[rows, :] * HD ** -0.5
        vc = v_ref[rows, :].astype(BF16)
        kb = kc.astype(BF16)
        outs = []
        for half, keep in enumerate((lo, hi)):
            qh = jnp.where(keep, qc, 0.0).astype(BF16)
            inner = lax.dot_general(qh, kb, NT_DIMS, preferred_element_type=F32) * dm[half]
            outs.append(jnp.dot(inner.astype(BF16), vc, preferred_element_type=F32))
        cross = jnp.dot(qc.astype(BF16), state.astype(BF16), preferred_element_type=F32) * xi
        o = jnp.where(lo, outs[0], outs[1]) + cross
        kz_t = (kc * zeta).T.astype(BF16)
        upd = jnp.dot(kz_t, vc, preferred_element_type=F32)
        state = gch * state + jnp.where(blockdiag, upd, 0.0)
        return rows, o, state

    cf = direction(df_ref, True)

    def fwd_body(c, state):
        rows, o, state = chunk(c, state, cf)
        of_scr[rows, :] = o
        return state

    unroll = min(n, RET_UNROLL)
    sf_ref[...] = lax.fori_loop(0, n, fwd_body, s0f_ref[...], unroll=unroll)

    cb = direction(db_ref, False)
    gn = gn_ref[...]

    def bwd_body(t, state):
        rows, o, state = chunk(n - 1 - t, state, cb)
        o = o + of_scr[rows, :]

        def per_head(x):
            a = jnp.where(lo, x, 0.0).sum(axis=-1, keepdims=True)
            b = jnp.where(hi, x, 0.0).sum(axis=-1, keepdims=True)
            return jnp.where(lo, a, b) * (1.0 / HD)

        d = o - per_head(o)
        y = d * lax.rsqrt(per_head(d * d) + EPS) * gn
        o_ref[rows, :] = _silu(g_ref[rows, :]) * y
        return state

    sb_ref[...] = lax.fori_loop(0, n, bwd_body, s0b_ref[...], unroll=unroll)


def _pair_lanes(v):
    return jnp.repeat(v.astype(F32), HD).reshape(B_HEADS // 2, 1, LANES)


def _blockdiag_states(s):
    b = s.shape[0]
    s = s.astype(F32).reshape(b, B_HEADS // 2, 2, HD, HD)
    z = jnp.zeros_like(s[:, :, 0])
    top = jnp.concatenate([s[:, :, 0], z], axis=-1)
    bot = jnp.concatenate([z, s[:, :, 1]], axis=-1)
    return jnp.concatenate([top, bot], axis=-2)


def _diag_states(sp):
    b = sp.shape[0]
    s = jnp.stack([sp[:, :, :HD, :HD], sp[:, :, HD:, HD:]], axis=2)
    return s.reshape(b, B_HEADS, HD, HD)


def _retention(p, row_base, batch, length, dec_f, dec_b, gn_g, s0f, s0b):
    npairs = B_HEADS // 2
    blk0 = row_base // length
    qcol = (A_Q + 2 * A_KV) // LANES

    def col_spec(off):
        return pl.BlockSpec((length, LANES), lambda b, h: (blk0 + b, qcol + off * npairs + h))

    lane_spec = pl.BlockSpec((None, 1, LANES), lambda b, h: (h, 0, 0))
    state_spec = pl.BlockSpec((None, None, LANES, LANES), lambda b, h: (b, h, 0, 0))
    state_shape = jax.ShapeDtypeStruct((batch, npairs, LANES, LANES), F32)
    return pl.pallas_call(
        functools.partial(_ret_kernel, length=length),
        grid=(batch, npairs),
        in_specs=[lane_spec, lane_spec, col_spec(0), col_spec(1), col_spec(2), col_spec(3), lane_spec,
                  state_spec, state_spec],
        out_specs=[pl.BlockSpec((length, LANES), lambda b, h: (b, h)), state_spec, state_spec],
        out_shape=[jax.ShapeDtypeStruct((batch * length, B_W), F32), state_shape, state_shape],
        scratch_shapes=[pltpu.VMEM((length, LANES), F32)],
        compiler_params=_cparams(2),
        name="retention",
    )(_pair_lanes(dec_f), _pair_lanes(dec_b), p, p, p, p, gn_g.reshape(npairs, 1, LANES), s0f, s0b)


def _ctx_mha_kernel(q_ref, k_ref, v_ref, o_ref):
    lo = _lane_lo()
    for pair in range(C_HEADS // 2):
        cols = slice(pair * LANES, (pair + 1) * LANES)
        q = _stack_heads(q_ref, [2 * pair, 2 * pair + 1], lo, HD ** -0.5)
        s = lax.dot_general(q, k_ref[:, cols].astype(BF16), NT_DIMS, preferred_element_type=F32)
        o = _softmax_av([s], [v_ref[:, cols].astype(BF16)])
        o_ref[:, cols] = jnp.where(lo, o[:SEQ], o[SEQ:])


def _ctx_mha(p):
    return pl.pallas_call(
        _ctx_mha_kernel,
        grid=(BATCH,),
        in_specs=[pl.BlockSpec((SEQ, C_W), lambda b: (b, 0)),
                  pl.BlockSpec((SEQ, C_W), lambda b: (b, 1)),
                  pl.BlockSpec((SEQ, C_W), lambda b: (b, 2))],
        out_specs=pl.BlockSpec((SEQ, C_W), lambda b: (b, 0)),
        out_shape=jax.ShapeDtypeStruct((NP_TOK, C_W), F32),
        compiler_params=_cparams(1),
        name="ctx_mha",
    )(p, p, p)


NA_WIN_ROWS = 2 * NA_ROWS
NA_WIN = NA_WIN_ROWS * GRID_W
NA_QROWS = NA_ROWS * GRID_W
NA_PAD_ROWS = NA_KH // 2
NA_TABLE = 1536


def _na_kernel(q_ref, kp_ref, km_ref, kn_ref, vp_ref, vm_ref, vn_ref, ck_ref, cv_ref, ue_ref, uo_ref, o_ref):
    r0 = pl.program_id(2) * NA_ROWS
    n_rows = DEC_SEQ // GRID_W
    lo = _lane_lo()
    k = jnp.concatenate([kp_ref[...], km_ref[...], kn_ref[...]], axis=0).astype(BF16)
    v = jnp.concatenate([vp_ref[...], vm_ref[...], vn_ref[...]], axis=0).astype(BF16)
    ck = ck_ref[...].astype(BF16)
    cv = cv_ref[...].astype(BF16)
    q = q_ref[...] * HD ** -0.5
    klane = lax.broadcasted_iota(jnp.int32, (1, NA_WIN), 1)
    outs = []
    for half, keep in enumerate((lo, jnp.logical_not(lo))):
        qh = jnp.where(keep, q, 0.0).astype(BF16)
        s = lax.dot_general(qh, k, NT_DIMS, preferred_element_type=F32)
        s_ctx = lax.dot_general(qh, ck, NT_DIMS, preferred_element_type=F32)
        pieces = []
        for rq in range(NA_ROWS):
            start = NA_KH - 1 - rq
            if start % 2 == 0:
                u = ue_ref[half, :, start * GRID_W:start * GRID_W + NA_WIN]
            else:
                u = uo_ref[half, :, (start - 1) * GRID_W:(start - 1) * GRID_W + NA_WIN]
            r = r0 + rq
            first = jnp.clip(r - NA_KH // 2, 0, n_rows - NA_KH)
            lane0 = (first - r0 + NA_PAD_ROWS) * GRID_W
            in_rows = (klane >= lane0) & (klane < lane0 + NA_KH * GRID_W)
            pieces.append(jnp.where(in_rows, s[rq * GRID_W:(rq + 1) * GRID_W] + u, NEG))
        outs.append(_softmax_av([jnp.concatenate(pieces, axis=0), s_ctx], [v, cv]))
    o_ref[...] = jnp.where(lo, outs[0], outs[1])


def _na_bias_tables(rpb):
    cq = jnp.arange(GRID_W)
    ck = jnp.arange(GRID_W)
    dc = jnp.clip(ck[None] - cq[:, None], -(NA_KW - 1), NA_KW - 1) + NA_KW - 1
    cs = jnp.clip(cq - NA_KW // 2, 0, GRID_W - NA_KW)
    col_ok = (ck[None] >= cs[:, None]) & (ck[None] < cs[:, None] + NA_KW)
    t = rpb.astype(F32)[:, :, dc]
    t = jnp.where(col_ok[None, None], t, NEG).transpose(0, 2, 1, 3)
    n_dr = 2 * NA_KH - 1
    blocks = NA_TABLE // GRID_W
    t = jnp.pad(t, ((0, 0), (0, 0), (NA_PAD_ROWS, blocks - n_dr - NA_PAD_ROWS), (0, 0)), constant_values=NEG)
    ue = t.reshape(C_HEADS, GRID_W, NA_TABLE)
    uo = jnp.concatenate([ue[..., GRID_W:], jnp.full((C_HEADS, GRID_W, GRID_W), NEG, F32)], axis=-1)
    return ue, uo


def _na_attention(p, cache_k, cache_v, rpb):
    npairs = C_HEADS // 2
    nrb = DEC_SEQ // NA_QROWS
    half = NA_QROWS // 2
    qbase = NP_TOK // NA_QROWS
    hbase = NP_TOK // half
    kcol = C_W // LANES
    ue, uo = _na_bias_tables(rpb)

    def main_spec(col0):
        return pl.BlockSpec((NA_QROWS, LANES), lambda b, h, r: (qbase + b * nrb + r, col0 + h))

    def side_spec(col0, off):
        return pl.BlockSpec((half, LANES),
                            lambda b, h, r: (hbase + b * 2 * nrb + jnp.clip(2 * r + off, 0, 2 * nrb - 1), col0 + h))

    ctx_spec = pl.BlockSpec((None, PAST, LANES), lambda b, h, r: (b, 0, h))
    tab_spec = pl.BlockSpec((2, GRID_W, NA_TABLE), lambda b, h, r: (h, 0, 0))
    return pl.pallas_call(
        _na_kernel,
        grid=(DEC_BATCH, npairs, nrb),
        in_specs=[main_spec(0),
                  side_spec(kcol, -1), main_spec(kcol), side_spec(kcol, 2),
                  side_spec(2 * kcol, -1), main_spec(2 * kcol), side_spec(2 * kcol, 2),
                  ctx_spec, ctx_spec, tab_spec, tab_spec],
        out_specs=pl.BlockSpec((NA_QROWS, LANES), lambda b, h, r: (b * nrb + r, h)),
        out_shape=jax.ShapeDtypeStruct((NS_TOK, C_W), F32),
        compiler_params=_cparams(3),
        name="na_attn",
    )(p, p, p, p, p, p, p, cache_k.reshape(DEC_BATCH, PAST, C_W), cache_v.reshape(DEC_BATCH, PAST, C_W), ue, uo)


def _route(biased, scores):
    t = biased.shape[1]
    per_group = N_EXPERTS // N_GROUPS
    i8 = lax.broadcasted_iota(jnp.int32, (per_group, t), 0)
    g_rows = []
    for g in range(N_GROUPS):
        bg = biased[g * per_group:(g + 1) * per_group]
        m1 = bg.max(axis=0, keepdims=True)
        first = jnp.where(bg == m1, i8, per_group).min(axis=0, keepdims=True)
        m2 = jnp.where(i8 == first, -jnp.inf, bg).max(axis=0, keepdims=True)
        g_rows.append(m1 + m2)
    g_top = jnp.concatenate(g_rows, axis=0)
    gi = lax.broadcasted_iota(jnp.int32, g_top.shape, 0)
    g_sel = jnp.zeros(g_top.shape, jnp.int32)
    cur = g_top
    for _ in range(TOPK_GROUPS):
        m = cur.max(axis=0, keepdims=True)
        hit = gi == jnp.where(cur == m, gi, N_GROUPS).min(axis=0, keepdims=True)
        g_sel = jnp.where(hit, 1, g_sel)
        cur = jnp.where(hit, -jnp.inf, cur)
    e_sel = jnp.concatenate([jnp.broadcast_to(g_sel[g:g + 1], (per_group, t)) for g in range(N_GROUPS)], axis=0)
    cur = jnp.where(e_sel > 0, biased, NEG)
    ei = lax.broadcasted_iota(jnp.int32, cur.shape, 0)
    ids, gates, hits = [], [], []
    for _ in range(TOP_K):
        m = cur.max(axis=0, keepdims=True)
        f = jnp.where(cur == m, ei, N_EXPERTS).min(axis=0, keepdims=True)
        hit = ei == f
        ids.append(f)
        hits.append(hit)
        gates.append(jnp.where(hit, scores, 0.0).sum(axis=0, keepdims=True))
        cur = jnp.where(hit, -jnp.inf, cur)
    gate = jnp.concatenate(gates, axis=0)
    gate = gate / gate.sum(axis=0, keepdims=True) * ROUTED_SCALE
    return jnp.concatenate(ids, axis=0), gate, hits


def _pack_bf16_pairs(h):
    bits = lax.bitcast_convert_type(h.astype(BF16).astype(F32), jnp.uint32)
    return bits[:, :D // 2] | (bits[:, D // 2:] >> 16)


def _unpack_bf16_pairs(xp):
    hi = lax.bitcast_convert_type(xp & jnp.uint32(0xFFFF0000), F32).astype(BF16)
    lo = lax.bitcast_convert_type(xp << 16, F32).astype(BF16)
    return hi, lo


def _dot_halves(hi, lo, w_ref):
    return (jnp.dot(hi, w_ref[:D // 2, :], preferred_element_type=F32)
            + jnp.dot(lo, w_ref[D // 2:, :], preferred_element_type=F32))


def _outproj_kernel(*refs, n_parts):
    x_ref = refs[0]
    part_refs = refs[1:1 + 3 * n_parts]
    gate_ref, shift_ref, scale_ref, g2_ref, rw_ref, rb_ref = refs[1 + 3 * n_parts:7 + 3 * n_parts]
    xo_ref, h_ref, dest_ref, wgt_ref, plan_ref, cnt_ref = refs[7 + 3 * n_parts:]
    step = pl.program_id(0)
    is_prompt = step < NP_TOK // TM

    @pl.when(step == 0)
    def _():
        cnt_ref[...] = jnp.zeros_like(cnt_ref)
        plan_ref[...] = jnp.zeros_like(plan_ref)

    y = None
    for t in range(n_parts):
        ap_ref, as_ref, w_ref = part_refs[3 * t:3 * t + 3]
        a = jnp.where(is_prompt, ap_ref[...], as_ref[...]).astype(BF16)
        d = jnp.dot(a, w_ref[...], preferred_element_type=F32)
        y = d if y is None else y + d
    x = x_ref[...] + gate_ref[...] * y
    xo_ref[...] = x
    h = _rms(x, g2_ref[...]) * (1.0 + scale_ref[...]) + shift_ref[...]
    h_ref[...] = _pack_bf16_pairs(h)
    logits = lax.dot_general(rw_ref[...], h, NT_DIMS, preferred_element_type=F32, precision=HIGHEST)
    scores = jax.nn.sigmoid(logits)
    _, gate, hits = _route(scores + rb_ref[...], scores)
    wgt_ref[...] = gate
    chosen = hits[0]
    for hit in hits[1:]:
        chosen = chosen | hit
    m = jnp.where(chosen, 1.0, 0.0)
    before = (lax.broadcasted_iota(jnp.int32, (TM, TM), 0) < lax.broadcasted_iota(jnp.int32, (TM, TM), 1))
    prefix = jnp.dot(m.astype(BF16), jnp.where(before, 1.0, 0.0).astype(BF16), preferred_element_type=F32)
    e_base = (lax.broadcasted_iota(jnp.int32, (N_EXPERTS, 1), 0) * N_TOK).astype(F32)
    row_all = prefix + (cnt_ref[...] + e_base)
    dest_ref[...] = jnp.concatenate(
        [jnp.where(hit, row_all, 0.0).sum(axis=0, keepdims=True) for hit in hits], axis=0).astype(jnp.int32)
    cnt_ref[...] += m.sum(axis=1, keepdims=True)

    @pl.when(step == pl.num_programs(0) - 1)
    def _():
        _block_plan(cnt_ref[...], plan_ref)


def _block_plan(counts, plan_ref):
    cap_blocks = N_TOK // MOE_BLOCK
    nblk = ((counts.astype(jnp.int32) + (MOE_BLOCK - 1)) // MOE_BLOCK).astype(F32)
    lower = (lax.broadcasted_iota(jnp.int32, (N_EXPERTS, N_EXPERTS), 0)
             >= lax.broadcasted_iota(jnp.int32, (N_EXPERTS, N_EXPERTS), 1))
    cum = jnp.dot(jnp.where(lower, 1.0, 0.0).astype(BF16), jnp.broadcast_to(nblk, (N_EXPERTS, LANES)).astype(BF16),
                  preferred_element_type=F32)[:, :1]
    n_used = cum[N_EXPERTS - 1:, :]
    slot = jnp.minimum(lax.broadcasted_iota(jnp.int32, (1, PLAN_LANES), 1).astype(F32), n_used - 1.0)
    done = cum <= slot
    expert = jnp.where(done, 1.0, 0.0).sum(axis=0, keepdims=True)
    blocks_before = jnp.where(done, nblk, 0.0).sum(axis=0, keepdims=True)
    plan_ref[0:1, :] = (expert * cap_blocks + (slot - blocks_before)).astype(jnp.int32)
    plan_ref[1:2, :] = expert.astype(jnp.int32)
    plan_ref[2:3, :] = jnp.broadcast_to(n_used, (1, PLAN_LANES)).astype(jnp.int32)


def _outproj(x, parts, mod, g2, router_w, router_b):
    npb = NP_TOK // TM
    nsb = NS_TOK // TM
    in_specs = [pl.BlockSpec((TM, D), lambda i: (i, 0))]
    args = [x]
    for ap, a_s, w in parts:
        width = ap.shape[1]
        in_specs += [pl.BlockSpec((TM, width), lambda i: (jnp.minimum(i, npb - 1), 0)),
                     pl.BlockSpec((TM, width), lambda i: (jnp.clip(i - npb, 0, nsb - 1), 0)),
                     pl.BlockSpec((width, D), lambda i: (0, 0))]
        args += [ap, a_s, w]
    in_specs += [_mod_spec(2), _mod_spec(3), _mod_spec(4),
                 pl.BlockSpec((1, D), lambda i: (0, 0)),
                 pl.BlockSpec((N_EXPERTS, D), lambda i: (0, 0)),
                 pl.BlockSpec((N_EXPERTS, 1), lambda i: (0, 0))]
    args += [mod, mod, mod, g2.reshape(1, D), router_w.T, router_b.reshape(N_EXPERTS, 1)]
    return pl.pallas_call(
        functools.partial(_outproj_kernel, n_parts=len(parts)),
        grid=(N_TOK // TM,),
        in_specs=in_specs,
        out_specs=[pl.BlockSpec((TM, D), lambda i: (i, 0)),
                   pl.BlockSpec((TM, D // 2), lambda i: (i, 0)),
                   pl.BlockSpec((TOP_K, TM), lambda i: (0, i)),
                   pl.BlockSpec((TOP_K, TM), lambda i: (0, i)),
                   pl.BlockSpec((8, PLAN_LANES), lambda i: (0, 0))],
        out_shape=[jax.ShapeDtypeStruct((N_TOK, D), F32),
                   jax.ShapeDtypeStruct((N_TOK, D // 2), jnp.uint32),
                   jax.ShapeDtypeStruct((TOP_K, N_TOK), jnp.int32),
                   jax.ShapeDtypeStruct((TOP_K, N_TOK), F32),
                   jax.ShapeDtypeStruct((8, PLAN_LANES), jnp.int32)],
        scratch_shapes=[pltpu.VMEM((N_EXPERTS, 1), F32)],
        compiler_params=_cparams(1),
        name="outproj_router",
    )(*args)


def _experts_kernel(br_ref, be_ref, nu_ref, x_ref, w1_ref, w3_ref, w2_ref, o_ref, w1b, w3b, w2b):
    i = pl.program_id(0)
    e = be_ref[i]
    prev = be_ref[jnp.maximum(i - 1, 0)]

    @pl.when((i == 0) | (e != prev))
    def _():
        w1b[...] = w1_ref[...].astype(BF16)
        w3b[...] = w3_ref[...].astype(BF16)
        w2b[...] = w2_ref[...].astype(BF16)

    @pl.when(i < nu_ref[0])
    def _():
        hi, lo = _unpack_bf16_pairs(x_ref[...])
        a = _dot_halves(hi, lo, w1b)
        b = _dot_halves(hi, lo, w3b)
        h = (_silu(a) * b).astype(BF16)
        o_ref[...] = _pack_bf16_pairs(jnp.dot(h, w2b[...], preferred_element_type=F32))


def _experts(plan, x_rows, w1, w3, w2, layer):
    return pl.pallas_call(
        _experts_kernel,
        grid_spec=pltpu.PrefetchScalarGridSpec(
            num_scalar_prefetch=3,
            grid=(N_MOE_BLOCKS,),
            in_specs=[pl.BlockSpec((MOE_BLOCK, D // 2), lambda i, br, be, nu: (br[i], 0)),
                      pl.BlockSpec((None, None, D, FF), lambda i, br, be, nu: (layer, be[i], 0, 0)),
                      pl.BlockSpec((None, None, D, FF), lambda i, br, be, nu: (layer, be[i], 0, 0)),
                      pl.BlockSpec((None, None, FF, D), lambda i, br, be, nu: (layer, be[i], 0, 0))],
            out_specs=pl.BlockSpec((MOE_BLOCK, D // 2), lambda i, br, be, nu: (br[i], 0)),
            scratch_shapes=[pltpu.VMEM((D, FF), BF16), pltpu.VMEM((D, FF), BF16), pltpu.VMEM((FF, D), BF16)]),
        out_shape=jax.ShapeDtypeStruct(x_rows.shape, jnp.uint32),
        compiler_params=_cparams(1),
        name="experts",
    )(plan[0], plan[1], plan[2, :1], x_rows, w1, w3, w2)


SC_CORES = 2
SC_SUBCORES = 16
SC_WORKERS = SC_CORES * SC_SUBCORES
SC_CHUNK_BYTES = 64 * 1024
SC_SLOTS = 4


def _sc_scatter(rows, dest, n_out):
    n_rows, width = rows.shape
    picks = dest.shape[0]
    chunk = SC_CHUNK_BYTES // (4 * width)
    per_worker = n_rows // SC_WORKERS
    n_chunks = per_worker // chunk
    assert per_worker * SC_WORKERS == n_rows and n_chunks * chunk == per_worker and n_chunks % 2 == 0
    mesh = plsc.VectorSubcoreMesh(core_axis_name="c", subcore_axis_name="s")

    @functools.partial(
        pl.kernel, mesh=mesh,
        out_type=jax.ShapeDtypeStruct((n_out, width), rows.dtype),
        scratch_types=[pltpu.VMEM((picks, n_chunks, chunk), jnp.int32),
                       pltpu.VMEM((2, chunk, width), rows.dtype),
                       pltpu.SemaphoreType.DMA((2,)),
                       pltpu.SemaphoreType.DMA((2,))])
    def scatter(r_hbm, d_hbm, o_hbm, idx_v, rows_v, lsem, ssem):
        worker = lax.axis_index("s") * SC_CORES + lax.axis_index("c")
        base = worker * per_worker
        for k in range(picks):
            pltpu.sync_copy(d_hbm.at[k, worker], idx_v.at[k])

        def load_copy(c, b):
            src = pl.ds(pl.multiple_of(base + c * chunk, chunk), chunk)
            return pltpu.make_async_copy(r_hbm.at[src], rows_v.at[b], lsem.at[b])

        def store_copy(c, b, k):
            return pltpu.make_async_copy(rows_v.at[b], o_hbm.at[idx_v.at[k, c]], ssem.at[b])

        load_copy(0, 0).start()

        @pl.loop(0, n_chunks, step=2)
        def _(c0):
            for b in range(2):
                c = c0 + b
                load_copy(c, b).wait()
                for k in range(picks):
                    store_copy(c, b, k).start()

                @pl.when(c > 0)
                def _():
                    for k in range(picks):
                        store_copy(c - 1, 1 - b, k).wait()

                @pl.when(c + 1 < n_chunks)
                def _():
                    load_copy(c + 1, 1 - b).start()

        for k in range(picks):
            store_copy(n_chunks - 1, 1, k).wait()

    return scatter(rows, dest.reshape(picks, SC_WORKERS, n_chunks, chunk))


def _sc_gather(table, idx):
    n_idx = idx.shape[0]
    width = table.shape[1]
    chunk = SC_CHUNK_BYTES // (4 * width)
    per_worker = n_idx // SC_WORKERS
    n_chunks = per_worker // chunk
    ahead = SC_SLOTS - 1
    assert per_worker * SC_WORKERS == n_idx and n_chunks * chunk == per_worker and n_chunks % SC_SLOTS == 0
    mesh = plsc.VectorSubcoreMesh(core_axis_name="c", subcore_axis_name="s")

    @functools.partial(
        pl.kernel, mesh=mesh,
        out_type=jax.ShapeDtypeStruct((n_idx, width), table.dtype),
        scratch_types=[pltpu.VMEM((per_worker,), jnp.int32),
                       pltpu.VMEM((SC_SLOTS, chunk, width), table.dtype),
                       pltpu.SemaphoreType.DMA((SC_SLOTS,)),
                       pltpu.SemaphoreType.DMA((SC_SLOTS,))])
    def gather(t_hbm, i_hbm, o_hbm, idx_v, rows_v, gsem, wsem):
        worker = lax.axis_index("s") * SC_CORES + lax.axis_index("c")
        base = worker * per_worker
        pltpu.sync_copy(i_hbm.at[pl.ds(pl.multiple_of(base, chunk), per_worker)], idx_v)

        def gather_copy(c, b):
            ids = idx_v.at[pl.ds(pl.multiple_of(c * chunk, chunk), chunk)]
            return pltpu.make_async_copy(t_hbm.at[ids], rows_v.at[b], gsem.at[b])

        def write_copy(c, b):
            rows = pl.ds(pl.multiple_of(base + c * chunk, chunk), chunk)
            return pltpu.make_async_copy(rows_v.at[b], o_hbm.at[rows], wsem.at[b])

        for c in range(ahead):
            gather_copy(c, c).start()

        @pl.loop(0, n_chunks, step=SC_SLOTS)
        def _(c0):
            for b in range(SC_SLOTS):
                c = c0 + b
                refill = (b + ahead) % SC_SLOTS
                gather_copy(c, b).wait()
                write_copy(c, b).start()

                @pl.when(c > 0)
                def _():
                    write_copy(c - 1, refill).wait()

                @pl.when(c + ahead < n_chunks)
                def _():
                    gather_copy(c + ahead, refill).start()

        write_copy(n_chunks - 1, (n_chunks - 1) % SC_SLOTS).wait()

    return gather(table, idx)


TC = 512


def _combine_kernel(x_ref, h_ref, y_ref, wgt_ref, gate_ref, w1_ref, w3_ref, w2_ref, fg_ref, o_ref, *, final):
    hi, lo = _unpack_bf16_pairs(h_ref[...])
    a = _dot_halves(hi, lo, w1_ref)
    b = _dot_halves(hi, lo, w3_ref)
    ffn = jnp.dot((_silu(a) * b).astype(BF16), w2_ref[...], preferred_element_type=F32)
    wgt = wgt_ref[...]
    r_hi = None
    r_lo = None
    for k in range(TOP_K):
        yk = y_ref[k]
        w = wgt[:, k:k + 1]
        t_hi = lax.bitcast_convert_type(yk & jnp.uint32(0xFFFF0000), F32) * w
        t_lo = lax.bitcast_convert_type(yk << 16, F32) * w
        r_hi = t_hi if r_hi is None else r_hi + t_hi
        r_lo = t_lo if r_lo is None else r_lo + t_lo
    x = x_ref[...] + gate_ref[...] * (ffn + jnp.concatenate([r_hi, r_lo], axis=1))
    o_ref[...] = _rms(x, fg_ref[...]) if final else x


def _combine(x, h, y_rows, wgt, mod, sw1, sw3, sw2, final_g, final):
    row = pl.BlockSpec((TC, D), lambda i: (i, 0))
    return pl.pallas_call(
        functools.partial(_combine_kernel, final=final),
        grid=(N_TOK // TC,),
        in_specs=[row,
                  pl.BlockSpec((TC, D // 2), lambda i: (i, 0)),
                  pl.BlockSpec((TOP_K, TC, D // 2), lambda i: (0, i, 0)),
                  pl.BlockSpec((TC, TOP_K), lambda i: (i, 0)),
                  _mod_spec(5, TC),
                  pl.BlockSpec((D, FF), lambda i: (0, 0)),
                  pl.BlockSpec((D, FF), lambda i: (0, 0)),
                  pl.BlockSpec((FF, D), lambda i: (0, 0)),
                  pl.BlockSpec((1, D), lambda i: (0, 0))],
        out_specs=row,
        out_shape=jax.ShapeDtypeStruct((N_TOK, D), F32),
        compiler_params=_cparams(1),
        name="combine",
    )(x, h, y_rows, wgt, mod, sw1.astype(BF16), sw3.astype(BF16), sw2.astype(BF16), final_g.reshape(1, D))


def kernel(x_prompt, x_sample, cache_a_k, cache_a_v, state_ret_fwd, state_ret_bwd, cache_c_k, cache_c_v,
           c, c_ctx, norm1_g, norm2_g, ada_w, ada_b, even_w_in, even_w_out, sink_a, ret_decay_fwd,
           ret_decay_bwd, ret_gn_g, odd_w_in, odd_w_out, na_rpb, router_w, router_b, exp_w1, exp_w3,
           exp_w2, sh_w1, sh_w3, sh_w2, final_g):
    x = jnp.concatenate([x_prompt.reshape(NP_TOK, D), x_sample.reshape(NS_TOK, D)], axis=0)
    cc = jnp.concatenate([c_ctx[None], c, jnp.zeros((8 - 1 - DEC_BATCH, D), F32)], axis=0)
    rope = _rope_tables()
    outs = {}
    for l in range(2):
        mod = _ada(cc, ada_w, ada_b, l)
        if l == 0:
            p = _inproj(x, norm1_g[l], mod, even_w_in[0].astype(BF16), rope, A_Q + A_KV)
            oa_p = _ctx_gqa(p, sink_a[0])
            oa_s = _win_attention(p, cache_a_k[:, 0], cache_a_v[:, 0], sink_a[0])
            zero = jnp.zeros((BATCH, B_HEADS // 2, LANES, LANES), F32)
            ob_p, sf, sb = _retention(p, 0, BATCH, SEQ, ret_decay_fwd[0], ret_decay_bwd[0], ret_gn_g[0], zero, zero)
            ob_s, _, _ = _retention(p, NP_TOK, DEC_BATCH, DEC_SEQ, ret_decay_fwd[0], ret_decay_bwd[0], ret_gn_g[0],
                                    _blockdiag_states(state_ret_fwd[:, 0]), _blockdiag_states(state_ret_bwd[:, 0]))
            w_out = even_w_out[0].astype(BF16)
            parts = [(oa_p, oa_s, w_out[:A_Q]), (ob_p, ob_s, w_out[A_Q:])]
            outs["a_k"] = p[:NP_TOK, A_Q:A_Q + A_KV].reshape(BATCH, 1, SEQ, A_KV_HEADS, HD)
            outs["a_v"] = p[:NP_TOK, A_Q + A_KV:A_Q + 2 * A_KV].reshape(BATCH, 1, SEQ, A_KV_HEADS, HD)
            outs["r_f"] = _diag_states(sf).reshape(BATCH, 1, B_HEADS, HD, HD)
            outs["r_b"] = _diag_states(sb).reshape(BATCH, 1, B_HEADS, HD, HD)
        else:
            p = _inproj(x, norm1_g[l], mod, odd_w_in[0].astype(BF16), rope, 0)
            o_p = _ctx_mha(p)
            o_s = _na_attention(p, cache_c_k[:, 0], cache_c_v[:, 0], na_rpb[0])
            parts = [(o_p, o_s, odd_w_out[0].astype(BF16))]
            outs["c_k"] = p[:NP_TOK, C_W:2 * C_W].reshape(BATCH, 1, SEQ, C_HEADS, HD)
            outs["c_v"] = p[:NP_TOK, 2 * C_W:3 * C_W].reshape(BATCH, 1, SEQ, C_HEADS, HD)
        x, h, dest, gate_t, plan = _outproj(x, parts, mod, norm2_g[l], router_w[l], router_b[l])
        y = _experts(plan, _sc_scatter(h, dest, N_EXPERTS * N_TOK), exp_w1, exp_w3, exp_w2, l)
        y_rows = _sc_gather(y, dest.reshape(N_ASSIGN)).reshape(TOP_K, N_TOK, D // 2)
        x = _combine(x, h, y_rows, gate_t.T, mod, sh_w1[l], sh_w3[l], sh_w2[l], final_g, final=(l == 1))
    y_prompt = x[:NP_TOK].reshape(BATCH, SEQ, D)
    y_sample = x[NP_TOK:].reshape(DEC_BATCH, DEC_SEQ, D)
    return (y_prompt, y_sample, outs["a_k"], outs["a_v"], outs["r_f"], outs["r_b"], outs["c_k"], outs["c_v"])
```

```python
import functools
import math

import jax
import jax.numpy as jnp
from jax import lax
from jax.experimental import pallas as pl
from jax.experimental.pallas import tpu as pltpu
from jax.experimental.pallas import tpu_sc as plsc

F32 = jnp.float32
BF16 = jnp.bfloat16
HIGHEST = lax.Precision.HIGHEST

D = 1024
BATCH = 32
SEQ = 256
DEC_BATCH = 4
DEC_SEQ = 4096
PAST = 256
GRID_W = 64
HD = 64
EPS = 1e-6
NEG = -1e30
ROPE_BASE = 10000.0
A_HEADS = 8
A_KV_HEADS = 2
A_Q = A_HEADS * HD
A_KV = A_KV_HEADS * HD
B_HEADS = 8
B_W = B_HEADS * HD
EVEN_IN = A_Q + 2 * A_KV + 4 * B_W
C_HEADS = 16
C_W = C_HEADS * HD
NA_KH = 8
NA_KW = 16
N_EXPERTS = 64
TOP_K = 8
N_GROUPS = 8
TOPK_GROUPS = 4
FF = 256
ROUTED_SCALE = 2.5
MOE_BLOCK = 512
RET_CHUNK = 128
RET_UNROLL = 2
RET_NORM_ROWS = 256
A_WINDOW = 128

NP_TOK = BATCH * SEQ
NS_TOK = DEC_BATCH * DEC_SEQ
N_TOK = NP_TOK + NS_TOK
N_ASSIGN = N_TOK * TOP_K
N_MOE_BLOCKS = (N_ASSIGN + N_EXPERTS * (MOE_BLOCK - 1) + MOE_BLOCK - 1) // MOE_BLOCK
PLAN_LANES = 512
assert N_TOK % MOE_BLOCK == 0 and N_MOE_BLOCKS <= PLAN_LANES

LANES = 128
TM = 512
NA_ROWS = 8
V7X_VMEM_LIMIT = 56 * 1024 * 1024

NT_DIMS = (((1,), (1,)), ((), ()))


def _cparams(n_axes, vmem=V7X_VMEM_LIMIT):
    return pltpu.CompilerParams(dimension_semantics=("arbitrary",) * n_axes, vmem_limit_bytes=vmem)


def _seg_of_block(i, rows):
    row0 = i * rows
    return jnp.where(row0 < NP_TOK, 0, 1 + (row0 - NP_TOK) // DEC_SEQ)


def _mod_spec(chunk, rows=TM, first_block=0):
    return pl.BlockSpec((None, 1, D), lambda i: (_seg_of_block(i + first_block, rows), 0, chunk))


def _pair_specs(width, rows=TM):
    npb = NP_TOK // rows
    nsb = NS_TOK // rows
    return [pl.BlockSpec((rows, width), lambda i: (jnp.minimum(i, npb - 1), 0)),
            pl.BlockSpec((rows, width), lambda i: (jnp.clip(i - npb, 0, nsb - 1), 0))]


def _pick_rows(p_ref, s_ref, rows=TM):
    return jnp.where(pl.program_id(0) < NP_TOK // rows, p_ref[...], s_ref[...])


def _silu(x):
    return x * jax.nn.sigmoid(x)


def _rms(x, g):
    return x * lax.rsqrt(jnp.mean(x * x, axis=-1, keepdims=True) + EPS) * g


def _lane_lo():
    return lax.broadcasted_iota(jnp.int32, (1, LANES), 1) < HD


def _ada_kernel(c_ref, w_ref, b_ref, o_ref):
    a = _silu(c_ref[...])
    o_ref[...] = jnp.dot(a, w_ref[...], preferred_element_type=F32, precision=HIGHEST) + b_ref[...]


def _ada(cc, w, b, layer):
    tn = 1536
    out = pl.pallas_call(
        _ada_kernel,
        grid=(6 * D // tn,),
        in_specs=[pl.BlockSpec((8, D), lambda j: (0, 0)),
                  pl.BlockSpec((None, D, tn), lambda j: (layer, 0, j)),
                  pl.BlockSpec((None, 1, tn), lambda j: (layer, 0, j))],
        out_specs=pl.BlockSpec((8, tn), lambda j: (0, j)),
        out_shape=jax.ShapeDtypeStruct((8, 6 * D), F32),
        compiler_params=_cparams(1),
        name="ada",
    )(cc, w, b.reshape(b.shape[0], 1, 6 * D))
    return out.reshape(8, 1, 6 * D)


def _inproj_kernel(xp_ref, xs_ref, g_ref, shift_ref, scale_ref, w_ref, cos_ref, sin_ref, o_ref, *, rope_cols):
    h = _rms(_pick_rows(xp_ref, xs_ref), g_ref[...]) * (1.0 + scale_ref[...]) + shift_ref[...]
    o = jnp.dot(h.astype(BF16), w_ref[...], preferred_element_type=F32)
    if rope_cols:
        cos = cos_ref[...]
        sin = sin_ref[...]
        lane = lax.broadcasted_iota(jnp.int32, (1, LANES), 1)
        first = (lane % 32) < 16
        for c in range(rope_cols // LANES):
            oc = o[:, c * LANES:(c + 1) * LANES]
            partner = jnp.where(first, pltpu.roll(oc, LANES - 16, 1), pltpu.roll(oc, 16, 1))
            o_ref[:, c * LANES:(c + 1) * LANES] = oc * cos + partner * sin
        o_ref[:, rope_cols:] = o[:, rope_cols:]
    else:
        o_ref[...] = o


def _rope_tables():
    half = HD // 2
    inv = ROPE_BASE ** (-jnp.arange(0, half, 2, dtype=F32) / half)
    t = jnp.arange(DEC_SEQ)
    ang_r = (t // GRID_W).astype(F32)[:, None] * inv[None]
    ang_c = (t % GRID_W).astype(F32)[:, None] * inv[None]

    def head(fn_r, fn_c, sign):
        return jnp.concatenate([sign[0] * fn_r, sign[1] * fn_r, sign[0] * fn_c, sign[1] * fn_c], axis=-1)

    cos = head(jnp.cos(ang_r), jnp.cos(ang_c), (1.0, 1.0))
    sin = head(jnp.sin(ang_r), jnp.sin(ang_c), (-1.0, 1.0))
    cos = jnp.concatenate([jnp.ones((TM, HD), F32), cos], axis=0)
    sin = jnp.concatenate([jnp.zeros((TM, HD), F32), sin], axis=0)
    return jnp.tile(cos, (1, 2)), jnp.tile(sin, (1, 2))


def _inproj(x, g, mod, w_bf16, rope, rope_cols):
    n_out = w_bf16.shape[1]
    npb = NP_TOK // TM
    spb = DEC_SEQ // TM

    def rope_map(i):
        return (jnp.where(i < npb, 0, 1 + (i - npb) % spb), 0)

    return pl.pallas_call(
        functools.partial(_inproj_kernel, rope_cols=rope_cols),
        grid=(N_TOK // TM,),
        in_specs=_pair_specs(D) + [
                  pl.BlockSpec((1, D), lambda i: (0, 0)),
                  _mod_spec(0), _mod_spec(1),
                  pl.BlockSpec((D, n_out), lambda i: (0, 0)),
                  pl.BlockSpec((TM, LANES), rope_map),
                  pl.BlockSpec((TM, LANES), rope_map)],
        out_specs=pl.BlockSpec((TM, n_out), lambda i: (i, 0)),
        out_shape=jax.ShapeDtypeStruct((N_TOK, n_out), F32),
        compiler_params=_cparams(1),
        name="inproj",
    )(x[0], x[1], g.reshape(1, D), mod, mod, w_bf16, rope[0], rope[1])


def _softmax_av(s_list, v_list, sink=None):
    mx = s_list[0].max(axis=-1, keepdims=True)
    for s in s_list[1:]:
        mx = jnp.maximum(mx, s.max(axis=-1, keepdims=True))
    if sink is not None:
        mx = jnp.maximum(mx, sink)
    den = jnp.exp(sink - mx) if sink is not None else 0.0
    acc = None
    for s, v in zip(s_list, v_list):
        p = jnp.exp(s - mx)
        den = den + p.sum(axis=-1, keepdims=True)
        pv = jnp.dot(p.astype(BF16), v, preferred_element_type=F32)
        acc = pv if acc is None else acc + pv
    return acc / den


def _dup_half(x, j, lo):
    xr = pltpu.roll(x, HD, 1)
    return jnp.where(lo, x, xr) if j == 0 else jnp.where(lo, xr, x)


def _stack_heads(q_ref, heads, lo, scale):
    parts = []
    for h in heads:
        qp = q_ref[:, (h // 2) * LANES:(h // 2 + 1) * LANES]
        keep = lo if h % 2 == 0 else jnp.logical_not(lo)
        parts.append(jnp.where(keep, qp, 0.0) * scale)
    return jnp.concatenate(parts, axis=0).astype(BF16)


def _sink_column(sink_ref, heads, rows):
    return jnp.concatenate([jnp.full((rows, 1), sink_ref[h], F32) for h in heads], axis=0)


def _ctx_gqa_kernel(sink_ref, q_ref, k_ref, v_ref, o_ref):
    lo = _lane_lo()
    k = k_ref[...]
    v = v_ref[...]
    group = A_HEADS // A_KV_HEADS
    for j in range(A_KV_HEADS):
        heads = list(range(group * j, group * (j + 1)))
        kd = _dup_half(k, j, lo).astype(BF16)
        vd = _dup_half(v, j, lo).astype(BF16)
        q = _stack_heads(q_ref, heads, lo, HD ** -0.5)
        s = lax.dot_general(q, kd, NT_DIMS, preferred_element_type=F32)
        o = _softmax_av([s], [vd], _sink_column(sink_ref, heads, SEQ))
        for t in range(group // 2):
            pair = heads[2 * t] // 2
            o_ref[:, pair * LANES:(pair + 1) * LANES] = jnp.where(
                lo, o[(2 * t) * SEQ:(2 * t + 1) * SEQ], o[(2 * t + 1) * SEQ:(2 * t + 2) * SEQ])


def _ctx_gqa(p, sink):
    return pl.pallas_call(
        _ctx_gqa_kernel,
        grid_spec=pltpu.PrefetchScalarGridSpec(
            num_scalar_prefetch=1,
            grid=(BATCH,),
            in_specs=[pl.BlockSpec((SEQ, A_Q), lambda b, s: (b, 0)),
                      pl.BlockSpec((SEQ, A_KV), lambda b, s: (b, A_Q // A_KV)),
                      pl.BlockSpec((SEQ, A_KV), lambda b, s: (b, A_Q // A_KV + 1))],
            out_specs=pl.BlockSpec((SEQ, A_Q), lambda b, s: (b, 0))),
        out_shape=jax.ShapeDtypeStruct((NP_TOK, A_Q), F32),
        compiler_params=_cparams(1),
        name="ctx_gqa",
    )(sink, p, p, p)


def _win_kernel(sink_ref, q_ref, kp_ref, kc_ref, kn_ref, vp_ref, vc_ref, vn_ref, ck_ref, cv_ref, o_ref):
    i = pl.program_id(1)
    lo = _lane_lo()
    k = jnp.concatenate([kp_ref[...], kc_ref[...], kn_ref[...]], axis=0)
    v = jnp.concatenate([vp_ref[...], vc_ref[...], vn_ref[...]], axis=0)
    ck = ck_ref[...]
    cv = cv_ref[...]
    group = A_HEADS // A_KV_HEADS
    n_keys = WIN_Q + 2 * A_WINDOW
    qpos = i * WIN_Q + lax.broadcasted_iota(jnp.int32, (WIN_Q, n_keys), 0)
    kpos = i * WIN_Q - A_WINDOW + lax.broadcasted_iota(jnp.int32, (WIN_Q, n_keys), 1)
    valid = (jnp.abs(kpos - qpos) <= A_WINDOW) & (kpos >= 0) & (kpos < DEC_SEQ)
    valid = jnp.concatenate([valid] * group, axis=0)
    s_loc, s_ctx, values = [], [], []
    for j in range(A_KV_HEADS):
        heads = list(range(group * j, group * (j + 1)))
        kd = _dup_half(k, j, lo).astype(BF16)
        ckd = _dup_half(ck, j, lo).astype(BF16)
        values.append((_dup_half(v, j, lo).astype(BF16), _dup_half(cv, j, lo).astype(BF16)))
        q = _stack_heads(q_ref, heads, lo, HD ** -0.5)
        s_loc.append(jnp.where(valid, lax.dot_general(q, kd, NT_DIMS, preferred_element_type=F32), NEG))
        s_ctx.append(lax.dot_general(q, ckd, NT_DIMS, preferred_element_type=F32))
    s_loc = jnp.concatenate(s_loc, axis=0)
    s_ctx = jnp.concatenate(s_ctx, axis=0)
    sink = _sink_column(sink_ref, list(range(A_HEADS)), WIN_Q)
    mx = jnp.maximum(jnp.maximum(s_loc.max(axis=-1, keepdims=True), s_ctx.max(axis=-1, keepdims=True)), sink)
    p_loc = jnp.exp(s_loc - mx)
    p_ctx = jnp.exp(s_ctx - mx)
    den = p_loc.sum(axis=-1, keepdims=True) + p_ctx.sum(axis=-1, keepdims=True) + jnp.exp(sink - mx)
    p_loc = p_loc.astype(BF16)
    p_ctx = p_ctx.astype(BF16)
    rows_per_group = group * WIN_Q
    for j, (vd, cvd) in enumerate(values):
        rows = slice(j * rows_per_group, (j + 1) * rows_per_group)
        o = (jnp.dot(p_loc[rows], vd, preferred_element_type=F32)
             + jnp.dot(p_ctx[rows], cvd, preferred_element_type=F32)) / den[rows]
        for t in range(group // 2):
            pair = (group * j) // 2 + t
            o_ref[:, pair * LANES:(pair + 1) * LANES] = jnp.where(
                lo, o[(2 * t) * WIN_Q:(2 * t + 1) * WIN_Q], o[(2 * t + 1) * WIN_Q:(2 * t + 2) * WIN_Q])


WIN_Q = 256


def _win_attention(p, cache_k, cache_v, sink):
    nblk = DEC_SEQ // WIN_Q
    side = WIN_Q // A_WINDOW
    nside = DEC_SEQ // A_WINDOW
    base = NP_TOK // WIN_Q
    side_base = NP_TOK // A_WINDOW
    kcol = A_Q // A_KV

    def main_spec(col):
        return pl.BlockSpec((WIN_Q, A_KV), lambda b, i, s: (base + b * nblk + i, col))

    def side_spec(col, off):
        return pl.BlockSpec((A_WINDOW, A_KV),
                            lambda b, i, s: (side_base + b * nside + jnp.clip(side * i + off, 0, nside - 1), col))

    ctx_spec = pl.BlockSpec((None, PAST, A_KV), lambda b, i, s: (b, 0, 0))
    return pl.pallas_call(
        _win_kernel,
        grid_spec=pltpu.PrefetchScalarGridSpec(
            num_scalar_prefetch=1,
            grid=(DEC_BATCH, nblk),
            in_specs=[pl.BlockSpec((WIN_Q, A_Q), lambda b, i, s: (base + b * nblk + i, 0)),
                      side_spec(kcol, -1), main_spec(kcol), side_spec(kcol, side),
                      side_spec(kcol + 1, -1), main_spec(kcol + 1), side_spec(kcol + 1, side),
                      ctx_spec, ctx_spec],
            out_specs=pl.BlockSpec((WIN_Q, A_Q), lambda b, i, s: (b * nblk + i, 0))),
        out_shape=jax.ShapeDtypeStruct((NS_TOK, A_Q), F32),
        compiler_params=_cparams(2),
        name="win_attn",
    )(sink, p, p, p, p, p, p, p, cache_k.reshape(DEC_BATCH, PAST, A_KV), cache_v.reshape(DEC_BATCH, PAST, A_KV))


def _ret_kernel(df_ref, db_ref, q_ref, k_ref, v_ref, g_ref, gn_ref, s0f_ref, s0b_ref,
                o_ref, sf_ref, sb_ref, of_scr, ob_scr, *, length):
    c_len = RET_CHUNK
    n = length // c_len
    lo = _lane_lo()
    hi = jnp.logical_not(lo)
    row = lax.broadcasted_iota(jnp.int32, (c_len, c_len), 0)
    col = lax.broadcasted_iota(jnp.int32, (c_len, c_len), 1)
    rowp = lax.broadcasted_iota(jnp.int32, (LANES, LANES), 0)
    colp = lax.broadcasted_iota(jnp.int32, (LANES, LANES), 1)
    blockdiag = (rowp < HD) == (colp < HD)
    idx = lax.broadcasted_iota(jnp.int32, (c_len, 1), 0).astype(F32)

    def direction(dec_ref, forward):
        lg = -jnp.exp(dec_ref[...])
        diff = (row - col) if forward else (col - row)
        keep = (diff >= 0) if forward else (diff > 0)
        dist = jnp.maximum(diff, 0).astype(F32)
        dm = [jnp.where(keep, jnp.exp(dist * lg[:, off:off + 1]), 0.0) for off in (0, HD)]
        if forward:
            xi = jnp.exp((idx + 1.0) * lg)
            zeta = jnp.exp((c_len - 1.0 - idx) * lg)
        else:
            xi = jnp.exp((c_len - idx) * lg)
            zeta = jnp.exp(idx * lg)
        return dm, xi, zeta, jnp.exp(c_len * lg)

    def chunk(c, state, consts):
        dm, xi, zeta, gch = consts
        rows = pl.ds(pl.multiple_of(c * c_len, c_len), c_len)
        qc = q_ref[rows, :]
        kc = k_ref[rows, :] * HD ** -0.5
        vc = v_ref[rows, :].astype(BF16)
        kb = kc.astype(BF16)
        outs = []
        for half, keep in enumerate((lo, hi)):
            qh = jnp.where(keep, qc, 0.0).astype(BF16)
            inner = lax.dot_general(qh, kb, NT_DIMS, preferred_element_type=F32) * dm[half]
            outs.append(jnp.dot(inner.astype(BF16), vc, preferred_element_type=F32))
        cross = jnp.dot(qc.astype(BF16), state.astype(BF16), preferred_element_type=F32) * xi
        o = jnp.where(lo, outs[0], outs[1]) + cross
        kz_t = (kc * zeta).T.astype(BF16)
        upd = jnp.dot(kz_t, vc, preferred_element_type=F32)
        state = gch * state + jnp.where(blockdiag, upd, 0.0)
        return rows, o, state

    cf = direction(df_ref, True)
    cb = direction(db_ref, False)

    def scan_body(t, states):
        rows_f, o_f, state_f = chunk(t, states[0], cf)
        of_scr[rows_f, :] = o_f
        rows_b, o_b, state_b = chunk(n - 1 - t, states[1], cb)
        ob_scr[rows_b, :] = o_b
        return state_f, state_b

    state_f, state_b = lax.fori_loop(0, n, scan_body, (s0f_ref[...], s0b_ref[...]), unroll=min(n, RET_UNROLL))
    sf_ref[...] = state_f
    sb_ref[...] = state_b

    gn = gn_ref[...]
    n_norm = length // RET_NORM_ROWS

    def per_head(x):
        a = jnp.where(lo, x, 0.0).sum(axis=-1, keepdims=True)
        b = jnp.where(hi, x, 0.0).sum(axis=-1, keepdims=True)
        return jnp.where(lo, a, b) * (1.0 / HD)

    def norm_body(t, carry):
        rows = pl.ds(pl.multiple_of(t * RET_NORM_ROWS, RET_NORM_ROWS), RET_NORM_ROWS)
        o = of_scr[rows, :] + ob_scr[rows, :]
        d = o - per_head(o)
        y = d * lax.rsqrt(per_head(d * d) + EPS) * gn
        o_ref[rows, :] = _silu(g_ref[rows, :]) * y
        return carry

    lax.fori_loop(0, n_norm, norm_body, 0)


def _pair_lanes(v):
    return jnp.repeat(v.astype(F32), HD).reshape(B_HEADS // 2, 1, LANES)


def _blockdiag_states(s):
    b = s.shape[0]
    s = s.astype(F32).reshape(b, B_HEADS // 2, 2, HD, HD)
    z = jnp.zeros_like(s[:, :, 0])
    top = jnp.concatenate([s[:, :, 0], z], axis=-1)
    bot = jnp.concatenate([z, s[:, :, 1]], axis=-1)
    return jnp.concatenate([top, bot], axis=-2)


def _diag_states(sp):
    b = sp.shape[0]
    s = jnp.stack([sp[:, :, :HD, :HD], sp[:, :, HD:, HD:]], axis=2)
    return s.reshape(b, B_HEADS, HD, HD)


def _retention(p, row_base, batch, length, dec_f, dec_b, gn_g, s0f, s0b):
    npairs = B_HEADS // 2
    blk0 = row_base // length
    qcol = (A_Q + 2 * A_KV) // LANES

    def col_spec(off):
        return pl.BlockSpec((length, LANES), lambda b, h: (blk0 + b, qcol + off * npairs + h))

    lane_spec = pl.BlockSpec((None, 1, LANES), lambda b, h: (h, 0, 0))
    state_spec = pl.BlockSpec((None, None, LANES, LANES), lambda b, h: (b, h, 0, 0))
    state_shape = jax.ShapeDtypeStruct((batch, npairs, LANES, LANES), F32)
    return pl.pallas_call(
        functools.partial(_ret_kernel, length=length),
        grid=(batch, npairs),
        in_specs=[lane_spec, lane_spec, col_spec(0), col_spec(1), col_spec(2), col_spec(3), lane_spec,
                  state_spec, state_spec],
        out_specs=[pl.BlockSpec((length, LANES), lambda b, h: (b, h)), state_spec, state_spec],
        out_shape=[jax.ShapeDtypeStruct((batch * length, B_W), F32), state_shape, state_shape],
        scratch_shapes=[pltpu.VMEM((length, LANES), F32), pltpu.VMEM((length, LANES), F32)],
        compiler_params=_cparams(2),
        name="retention",
    )(_pair_lanes(dec_f), _pair_lanes(dec_b), p, p, p, p, gn_g.reshape(npairs, 1, LANES), s0f, s0b)


def _ctx_mha_kernel(q_ref, k_ref, v_ref, o_ref):
    lo = _lane_lo()
    for pair in range(C_HEADS // 2):
        cols = slice(pair * LANES, (pair + 1) * LANES)
        q = _stack_heads(q_ref, [2 * pair, 2 * pair + 1], lo, HD ** -0.5)
        s = lax.dot_general(q, k_ref[:, cols].astype(BF16), NT_DIMS, preferred_element_type=F32)
        o = _softmax_av([s], [v_ref[:, cols].astype(BF16)])
        o_ref[:, cols] = jnp.where(lo, o[:SEQ], o[SEQ:])


def _ctx_mha(p):
    return pl.pallas_call(
        _ctx_mha_kernel,
        grid=(BATCH,),
        in_specs=[pl.BlockSpec((SEQ, C_W), lambda b: (b, 0)),
                  pl.BlockSpec((SEQ, C_W), lambda b: (b, 1)),
                  pl.BlockSpec((SEQ, C_W), lambda b: (b, 2))],
        out_specs=pl.BlockSpec((SEQ, C_W), lambda b: (b, 0)),
        out_shape=jax.ShapeDtypeStruct((NP_TOK, C_W), F32),
        compiler_params=_cparams(1),
        name="ctx_mha",
    )(p, p, p)


NA_WIN_ROWS = 2 * NA_ROWS
NA_WIN = NA_WIN_ROWS * GRID_W
NA_QROWS = NA_ROWS * GRID_W
NA_PAD_ROWS = NA_KH // 2
NA_TABLE = 1536


def _na_kernel(q_ref, kp_ref, km_ref, kn_ref, vp_ref, vm_ref, vn_ref, ck_ref, cv_ref, ue_ref, uo_ref, o_ref):
    r0 = pl.program_id(2) * NA_ROWS
    n_rows = DEC_SEQ // GRID_W
    lo = _lane_lo()
    k = jnp.concatenate([kp_ref[...], km_ref[...], kn_ref[...]], axis=0).astype(BF16)
    v = jnp.concatenate([vp_ref[...], vm_ref[...], vn_ref[...]], axis=0).astype(BF16)
    ck = ck_ref[...].astype(BF16)
    cv = cv_ref[...].astype(BF16)
    q = q_ref[...] * HD ** -0.5
    klane = lax.broadcasted_iota(jnp.int32, (1, NA_WIN), 1)
    outs = []
    for half, keep in enumerate((lo, jnp.logical_not(lo))):
        qh = jnp.where(keep, q, 0.0).astype(BF16)
        s = lax.dot_general(qh, k, NT_DIMS, preferred_element_type=F32)
        s_ctx = lax.dot_general(qh, ck, NT_DIMS, preferred_element_type=F32)
        pieces = []
        for rq in range(NA_ROWS):
            start = NA_KH - 1 - rq
            if start % 2 == 0:
                u = ue_ref[half, :, start * GRID_W:start * GRID_W + NA_WIN]
            else:
                u = uo_ref[half, :, (start - 1) * GRID_W:(start - 1) * GRID_W + NA_WIN]
            r = r0 + rq
            first = jnp.clip(r - NA_KH // 2, 0, n_rows - NA_KH)
            lane0 = (first - r0 + NA_PAD_ROWS) * GRID_W
            in_rows = (klane >= lane0) & (klane < lane0 + NA_KH * GRID_W)
            pieces.append(jnp.where(in_rows, s[rq * GRID_W:(rq + 1) * GRID_W] + u, NEG))
        outs.append(_softmax_av([jnp.concatenate(pieces, axis=0), s_ctx], [v, cv]))
    o_ref[...] = jnp.where(lo, outs[0], outs[1])


def _na_bias_tables(rpb):
    cq = jnp.arange(GRID_W)
    ck = jnp.arange(GRID_W)
    dc = jnp.clip(ck[None] - cq[:, None], -(NA_KW - 1), NA_KW - 1) + NA_KW - 1
    cs = jnp.clip(cq - NA_KW // 2, 0, GRID_W - NA_KW)
    col_ok = (ck[None] >= cs[:, None]) & (ck[None] < cs[:, None] + NA_KW)
    t = rpb.astype(F32)[:, :, dc]
    t = jnp.where(col_ok[None, None], t, NEG).transpose(0, 2, 1, 3)
    n_dr = 2 * NA_KH - 1
    blocks = NA_TABLE // GRID_W
    t = jnp.pad(t, ((0, 0), (0, 0), (NA_PAD_ROWS, blocks - n_dr - NA_PAD_ROWS), (0, 0)), constant_values=NEG)
    ue = t.reshape(C_HEADS, GRID_W, NA_TABLE)
    uo = jnp.concatenate([ue[..., GRID_W:], jnp.full((C_HEADS, GRID_W, GRID_W), NEG, F32)], axis=-1)
    return ue, uo


def _na_attention(p, cache_k, cache_v, rpb):
    npairs = C_HEADS // 2
    nrb = DEC_SEQ // NA_QROWS
    half = NA_QROWS // 2
    qbase = NP_TOK // NA_QROWS
    hbase = NP_TOK // half
    kcol = C_W // LANES
    ue, uo = _na_bias_tables(rpb)

    def main_spec(col0):
        return pl.BlockSpec((NA_QROWS, LANES), lambda b, h, r: (qbase + b * nrb + r, col0 + h))

    def side_spec(col0, off):
        return pl.BlockSpec((half, LANES),
                            lambda b, h, r: (hbase + b * 2 * nrb + jnp.clip(2 * r + off, 0, 2 * nrb - 1), col0 + h))

    ctx_spec = pl.BlockSpec((None, PAST, LANES), lambda b, h, r: (b, 0, h))
    tab_spec = pl.BlockSpec((2, GRID_W, NA_TABLE), lambda b, h, r: (h, 0, 0))
    return pl.pallas_call(
        _na_kernel,
        grid=(DEC_BATCH, npairs, nrb),
        in_specs=[main_spec(0),
                  side_spec(kcol, -1), main_spec(kcol), side_spec(kcol, 2),
                  side_spec(2 * kcol, -1), main_spec(2 * kcol), side_spec(2 * kcol, 2),
                  ctx_spec, ctx_spec, tab_spec, tab_spec],
        out_specs=pl.BlockSpec((NA_QROWS, LANES), lambda b, h, r: (b * nrb + r, h)),
        out_shape=jax.ShapeDtypeStruct((NS_TOK, C_W), F32),
        compiler_params=_cparams(3),
        name="na_attn",
    )(p, p, p, p, p, p, p, cache_k.reshape(DEC_BATCH, PAST, C_W), cache_v.reshape(DEC_BATCH, PAST, C_W), ue, uo)


def _route(biased, scores):
    t = biased.shape[1]
    per_group = N_EXPERTS // N_GROUPS
    i8 = lax.broadcasted_iota(jnp.int32, (per_group, t), 0)
    g_rows = []
    for g in range(N_GROUPS):
        bg = biased[g * per_group:(g + 1) * per_group]
        m1 = bg.max(axis=0, keepdims=True)
        first = jnp.where(bg == m1, i8, per_group).min(axis=0, keepdims=True)
        m2 = jnp.where(i8 == first, -jnp.inf, bg).max(axis=0, keepdims=True)
        g_rows.append(m1 + m2)
    g_top = jnp.concatenate(g_rows, axis=0)
    gi = lax.broadcasted_iota(jnp.int32, g_top.shape, 0)
    g_sel = jnp.zeros(g_top.shape, jnp.int32)
    cur = g_top
    for _ in range(TOPK_GROUPS):
        m = cur.max(axis=0, keepdims=True)
        hit = gi == jnp.where(cur == m, gi, N_GROUPS).min(axis=0, keepdims=True)
        g_sel = jnp.where(hit, 1, g_sel)
        cur = jnp.where(hit, -jnp.inf, cur)
    e_sel = jnp.concatenate([jnp.broadcast_to(g_sel[g:g + 1], (per_group, t)) for g in range(N_GROUPS)], axis=0)
    cur = jnp.where(e_sel > 0, biased, NEG)
    ei = lax.broadcasted_iota(jnp.int32, cur.shape, 0)
    ids, gates, hits = [], [], []
    for _ in range(TOP_K):
        m = cur.max(axis=0, keepdims=True)
        f = jnp.where(cur == m, ei, N_EXPERTS).min(axis=0, keepdims=True)
        hit = ei == f
        ids.append(f)
        hits.append(hit)
        gates.append(jnp.where(hit, scores, 0.0).sum(axis=0, keepdims=True))
        cur = jnp.where(hit, -jnp.inf, cur)
    gate = jnp.concatenate(gates, axis=0)
    gate = gate / gate.sum(axis=0, keepdims=True) * ROUTED_SCALE
    return jnp.concatenate(ids, axis=0), gate, hits


def _pack_bf16_pairs(h):
    bits = lax.bitcast_convert_type(h.astype(BF16).astype(F32), jnp.uint32)
    return bits[:, :D // 2] | (bits[:, D // 2:] >> 16)


def _unpack_bf16_pairs(xp):
    hi = lax.bitcast_convert_type(xp & jnp.uint32(0xFFFF0000), F32).astype(BF16)
    lo = lax.bitcast_convert_type(xp << 16, F32).astype(BF16)
    return hi, lo


def _dot_halves(hi, lo, w_ref):
    return (jnp.dot(hi, w_ref[:D // 2, :], preferred_element_type=F32)
            + jnp.dot(lo, w_ref[D // 2:, :], preferred_element_type=F32))


def _outproj_kernel(*refs, n_parts):
    xp_ref, xs_ref = refs[:2]
    part_refs = refs[2:2 + 3 * n_parts]
    gate_ref, shift_ref, scale_ref, g2_ref, rw_ref, rb_ref = refs[2 + 3 * n_parts:8 + 3 * n_parts]
    xo_ref, h_ref, dest_ref, wgt_ref, plan_ref, cnt_ref = refs[8 + 3 * n_parts:]
    step = pl.program_id(0)

    @pl.when(step == 0)
    def _():
        cnt_ref[...] = jnp.zeros_like(cnt_ref)
        plan_ref[...] = jnp.zeros_like(plan_ref)

    y = None
    for t in range(n_parts):
        ap_ref, as_ref, w_ref = part_refs[3 * t:3 * t + 3]
        d = jnp.dot(_pick_rows(ap_ref, as_ref).astype(BF16), w_ref[...], preferred_element_type=F32)
        y = d if y is None else y + d
    x = _pick_rows(xp_ref, xs_ref) + gate_ref[...] * y
    xo_ref[...] = x
    h = _rms(x, g2_ref[...]) * (1.0 + scale_ref[...]) + shift_ref[...]
    h_ref[...] = _pack_bf16_pairs(h)
    logits = lax.dot_general(rw_ref[...], h, NT_DIMS, preferred_element_type=F32, precision=HIGHEST)
    scores = jax.nn.sigmoid(logits)
    _, gate, hits = _route(scores + rb_ref[...], scores)
    wgt_ref[...] = gate
    chosen = hits[0]
    for hit in hits[1:]:
        chosen = chosen | hit
    m = jnp.where(chosen, 1.0, 0.0)
    before = (lax.broadcasted_iota(jnp.int32, (TM, TM), 0) < lax.broadcasted_iota(jnp.int32, (TM, TM), 1))
    prefix = jnp.dot(m.astype(BF16), jnp.where(before, 1.0, 0.0).astype(BF16), preferred_element_type=F32)
    e_base = (lax.broadcasted_iota(jnp.int32, (N_EXPERTS, 1), 0) * N_TOK).astype(F32)
    row_all = prefix + (cnt_ref[...] + e_base)
    dest_ref[...] = jnp.concatenate(
        [jnp.where(hit, row_all, 0.0).sum(axis=0, keepdims=True) for hit in hits], axis=0).astype(jnp.int32)
    cnt_ref[...] += m.sum(axis=1, keepdims=True)

    @pl.when(step == pl.num_programs(0) - 1)
    def _():
        _block_plan(cnt_ref[...], plan_ref)


def _block_plan(counts, plan_ref):
    cap_blocks = N_TOK // MOE_BLOCK
    nblk = ((counts.astype(jnp.int32) + (MOE_BLOCK - 1)) // MOE_BLOCK).astype(F32)
    lower = (lax.broadcasted_iota(jnp.int32, (N_EXPERTS, N_EXPERTS), 0)
             >= lax.broadcasted_iota(jnp.int32, (N_EXPERTS, N_EXPERTS), 1))
    cum = jnp.dot(jnp.where(lower, 1.0, 0.0).astype(BF16), jnp.broadcast_to(nblk, (N_EXPERTS, LANES)).astype(BF16),
                  preferred_element_type=F32)[:, :1]
    n_used = cum[N_EXPERTS - 1:, :]
    slot = jnp.minimum(lax.broadcasted_iota(jnp.int32, (1, PLAN_LANES), 1).astype(F32), n_used - 1.0)
    done = cum <= slot
    expert = jnp.where(done, 1.0, 0.0).sum(axis=0, keepdims=True)
    blocks_before = jnp.where(done, nblk, 0.0).sum(axis=0, keepdims=True)
    plan_ref[0:1, :] = (expert * cap_blocks + (slot - blocks_before)).astype(jnp.int32)
    plan_ref[1:2, :] = expert.astype(jnp.int32)
    plan_ref[2:3, :] = jnp.broadcast_to(n_used, (1, PLAN_LANES)).astype(jnp.int32)


def _outproj(x, parts, mod, g2, router_w, router_b):
    in_specs = _pair_specs(D)
    args = [x[0], x[1]]
    for ap, a_s, w in parts:
        width = ap.shape[1]
        in_specs += _pair_specs(width) + [pl.BlockSpec((width, D), lambda i: (0, 0))]
        args += [ap, a_s, w]
    in_specs += [_mod_spec(2), _mod_spec(3), _mod_spec(4),
                 pl.BlockSpec((1, D), lambda i: (0, 0)),
                 pl.BlockSpec((N_EXPERTS, D), lambda i: (0, 0)),
                 pl.BlockSpec((N_EXPERTS, 1), lambda i: (0, 0))]
    args += [mod, mod, mod, g2.reshape(1, D), router_w.T, router_b.reshape(N_EXPERTS, 1)]
    return pl.pallas_call(
        functools.partial(_outproj_kernel, n_parts=len(parts)),
        grid=(N_TOK // TM,),
        in_specs=in_specs,
        out_specs=[pl.BlockSpec((TM, D), lambda i: (i, 0)),
                   pl.BlockSpec((TM, D // 2), lambda i: (i, 0)),
                   pl.BlockSpec((TOP_K, TM), lambda i: (0, i)),
                   pl.BlockSpec((TOP_K, TM), lambda i: (0, i)),
                   pl.BlockSpec((8, PLAN_LANES), lambda i: (0, 0))],
        out_shape=[jax.ShapeDtypeStruct((N_TOK, D), F32),
                   jax.ShapeDtypeStruct((N_TOK, D // 2), jnp.uint32),
                   jax.ShapeDtypeStruct((TOP_K, N_TOK), jnp.int32),
                   jax.ShapeDtypeStruct((TOP_K, N_TOK), F32),
                   jax.ShapeDtypeStruct((8, PLAN_LANES), jnp.int32)],
        scratch_shapes=[pltpu.VMEM((N_EXPERTS, 1), F32)],
        compiler_params=_cparams(1),
        name="outproj_router",
    )(*args)


def _experts_kernel(br_ref, be_ref, nu_ref, x_ref, w1_ref, w3_ref, w2_ref, o_ref, w1b, w3b, w2b):
    i = pl.program_id(0)
    e = be_ref[i]
    prev = be_ref[jnp.maximum(i - 1, 0)]

    @pl.when((i == 0) | (e != prev))
    def _():
        w1b[...] = w1_ref[...].astype(BF16)
        w3b[...] = w3_ref[...].astype(BF16)
        w2b[...] = w2_ref[...].astype(BF16)

    @pl.when(i < nu_ref[0])
    def _():
        hi, lo = _unpack_bf16_pairs(x_ref[...])
        a = _dot_halves(hi, lo, w1b)
        b = _dot_halves(hi, lo, w3b)
        h = (_silu(a) * b).astype(BF16)
        o_ref[...] = _pack_bf16_pairs(jnp.dot(h, w2b[...], preferred_element_type=F32))


def _experts(plan, x_rows, w1, w3, w2, layer):
    return pl.pallas_call(
        _experts_kernel,
        grid_spec=pltpu.PrefetchScalarGridSpec(
            num_scalar_prefetch=3,
            grid=(N_MOE_BLOCKS,),
            in_specs=[pl.BlockSpec((MOE_BLOCK, D // 2), lambda i, br, be, nu: (br[i], 0)),
                      pl.BlockSpec((None, None, D, FF), lambda i, br, be, nu: (layer, be[i], 0, 0)),
                      pl.BlockSpec((None, None, D, FF), lambda i, br, be, nu: (layer, be[i], 0, 0)),
                      pl.BlockSpec((None, None, FF, D), lambda i, br, be, nu: (layer, be[i], 0, 0))],
            out_specs=pl.BlockSpec((MOE_BLOCK, D // 2), lambda i, br, be, nu: (br[i], 0)),
            scratch_shapes=[pltpu.VMEM((D, FF), BF16), pltpu.VMEM((D, FF), BF16), pltpu.VMEM((FF, D), BF16)]),
        out_shape=jax.ShapeDtypeStruct(x_rows.shape, jnp.uint32),
        compiler_params=_cparams(1),
        name="experts",
    )(plan[0], plan[1], plan[2, :1], x_rows, w1, w3, w2)


SC_CORES = 2
SC_SUBCORES = 16
SC_WORKERS = SC_CORES * SC_SUBCORES
SC_CHUNK_BYTES = 64 * 1024
SC_SLOTS = 4


def _sc_scatter(rows, dest, n_out):
    n_rows, width = rows.shape
    picks = dest.shape[0]
    chunk = SC_CHUNK_BYTES // (4 * width)
    per_worker = n_rows // SC_WORKERS
    n_chunks = per_worker // chunk
    assert per_worker * SC_WORKERS == n_rows and n_chunks * chunk == per_worker and n_chunks % 2 == 0
    mesh = plsc.VectorSubcoreMesh(core_axis_name="c", subcore_axis_name="s")

    @functools.partial(
        pl.kernel, mesh=mesh,
        out_type=jax.ShapeDtypeStruct((n_out, width), rows.dtype),
        scratch_types=[pltpu.VMEM((picks, n_chunks, chunk), jnp.int32),
                       pltpu.VMEM((2, chunk, width), rows.dtype),
                       pltpu.SemaphoreType.DMA((2,)),
                       pltpu.SemaphoreType.DMA((2,))])
    def scatter(r_hbm, d_hbm, o_hbm, idx_v, rows_v, lsem, ssem):
        worker = lax.axis_index("s") * SC_CORES + lax.axis_index("c")
        base = worker * per_worker
        for k in range(picks):
            pltpu.sync_copy(d_hbm.at[k, worker], idx_v.at[k])

        def load_copy(c, b):
            src = pl.ds(pl.multiple_of(base + c * chunk, chunk), chunk)
            return pltpu.make_async_copy(r_hbm.at[src], rows_v.at[b], lsem.at[b])

        def store_copy(c, b, k):
            return pltpu.make_async_copy(rows_v.at[b], o_hbm.at[idx_v.at[k, c]], ssem.at[b])

        load_copy(0, 0).start()

        @pl.loop(0, n_chunks, step=2)
        def _(c0):
            for b in range(2):
                c = c0 + b
                load_copy(c, b).wait()
                for k in range(picks):
                    store_copy(c, b, k).start()

                @pl.when(c > 0)
                def _():
                    for k in range(picks):
                        store_copy(c - 1, 1 - b, k).wait()

                @pl.when(c + 1 < n_chunks)
                def _():
                    load_copy(c + 1, 1 - b).start()

        for k in range(picks):
            store_copy(n_chunks - 1, 1, k).wait()

    return scatter(rows, dest.reshape(picks, SC_WORKERS, n_chunks, chunk))


def _sc_gather(table, idx):
    n_idx = idx.shape[0]
    width = table.shape[1]
    chunk = SC_CHUNK_BYTES // (4 * width)
    per_worker = n_idx // SC_WORKERS
    n_chunks = per_worker // chunk
    ahead = SC_SLOTS - 1
    assert per_worker * SC_WORKERS == n_idx and n_chunks * chunk == per_worker and n_chunks % SC_SLOTS == 0
    mesh = plsc.VectorSubcoreMesh(core_axis_name="c", subcore_axis_name="s")

    @functools.partial(
        pl.kernel, mesh=mesh,
        out_type=jax.ShapeDtypeStruct((n_idx, width), table.dtype),
        scratch_types=[pltpu.VMEM((per_worker,), jnp.int32),
                       pltpu.VMEM((SC_SLOTS, chunk, width), table.dtype),
                       pltpu.SemaphoreType.DMA((SC_SLOTS,)),
                       pltpu.SemaphoreType.DMA((SC_SLOTS,))])
    def gather(t_hbm, i_hbm, o_hbm, idx_v, rows_v, gsem, wsem):
        worker = lax.axis_index("s") * SC_CORES + lax.axis_index("c")
        base = worker * per_worker
        pltpu.sync_copy(i_hbm.at[pl.ds(pl.multiple_of(base, chunk), per_worker)], idx_v)

        def gather_copy(c, b):
            ids = idx_v.at[pl.ds(pl.multiple_of(c * chunk, chunk), chunk)]
            return pltpu.make_async_copy(t_hbm.at[ids], rows_v.at[b], gsem.at[b])

        def write_copy(c, b):
            rows = pl.ds(pl.multiple_of(base + c * chunk, chunk), chunk)
            return pltpu.make_async_copy(rows_v.at[b], o_hbm.at[rows], wsem.at[b])

        for c in range(ahead):
            gather_copy(c, c).start()

        @pl.loop(0, n_chunks, step=SC_SLOTS)
        def _(c0):
            for b in range(SC_SLOTS):
                c = c0 + b
                refill = (b + ahead) % SC_SLOTS
                gather_copy(c, b).wait()
                write_copy(c, b).start()

                @pl.when(c > 0)
                def _():
                    write_copy(c - 1, refill).wait()

                @pl.when(c + ahead < n_chunks)
                def _():
                    gather_copy(c + ahead, refill).start()

        write_copy(n_chunks - 1, (n_chunks - 1) % SC_SLOTS).wait()

    return gather(table, idx)


TC = 512


def _combine_kernel(x_ref, h_ref, y_ref, wgt_ref, gate_ref, w1_ref, w3_ref, w2_ref, fg_ref, o_ref, *, final):
    hi, lo = _unpack_bf16_pairs(h_ref[...])
    a = _dot_halves(hi, lo, w1_ref)
    b = _dot_halves(hi, lo, w3_ref)
    ffn = jnp.dot((_silu(a) * b).astype(BF16), w2_ref[...], preferred_element_type=F32)
    wgt = wgt_ref[...]
    r_hi = None
    r_lo = None
    for k in range(TOP_K):
        yk = y_ref[k]
        w = wgt[:, k:k + 1]
        t_hi = lax.bitcast_convert_type(yk & jnp.uint32(0xFFFF0000), F32) * w
        t_lo = lax.bitcast_convert_type(yk << 16, F32) * w
        r_hi = t_hi if r_hi is None else r_hi + t_hi
        r_lo = t_lo if r_lo is None else r_lo + t_lo
    x = x_ref[...] + gate_ref[...] * (ffn + jnp.concatenate([r_hi, r_lo], axis=1))
    o_ref[...] = _rms(x, fg_ref[...]) if final else x


def _combine(x, h, y_rows, wgt, mod, sw1, sw3, sw2, final_g, final):
    weights = (sw1.astype(BF16), sw3.astype(BF16), sw2.astype(BF16), final_g.reshape(1, D))

    def rows_from(first_row, n_rows):
        b0 = first_row // TC
        return pl.pallas_call(
            functools.partial(_combine_kernel, final=final),
            grid=(n_rows // TC,),
            in_specs=[pl.BlockSpec((TC, D), lambda i: (i + b0, 0)),
                      pl.BlockSpec((TC, D // 2), lambda i: (i + b0, 0)),
                      pl.BlockSpec((TOP_K, TC, D // 2), lambda i: (0, i + b0, 0)),
                      pl.BlockSpec((TC, TOP_K), lambda i: (i + b0, 0)),
                      _mod_spec(5, TC, b0),
                      pl.BlockSpec((D, FF), lambda i: (0, 0)),
                      pl.BlockSpec((D, FF), lambda i: (0, 0)),
                      pl.BlockSpec((FF, D), lambda i: (0, 0)),
                      pl.BlockSpec((1, D), lambda i: (0, 0))],
            out_specs=pl.BlockSpec((TC, D), lambda i: (i, 0)),
            out_shape=jax.ShapeDtypeStruct((n_rows, D), F32),
            compiler_params=_cparams(1),
            name="combine",
        )(x, h, y_rows, wgt, mod, *weights)

    return rows_from(0, NP_TOK), rows_from(NP_TOK, NS_TOK)


def kernel(x_prompt, x_sample, cache_a_k, cache_a_v, state_ret_fwd, state_ret_bwd, cache_c_k, cache_c_v,
           c, c_ctx, norm1_g, norm2_g, ada_w, ada_b, even_w_in, even_w_out, sink_a, ret_decay_fwd,
           ret_decay_bwd, ret_gn_g, odd_w_in, odd_w_out, na_rpb, router_w, router_b, exp_w1, exp_w3,
           exp_w2, sh_w1, sh_w3, sh_w2, final_g):
    x = (x_prompt.reshape(NP_TOK, D), x_sample.reshape(NS_TOK, D))
    cc = jnp.concatenate([c_ctx[None], c, jnp.zeros((8 - 1 - DEC_BATCH, D), F32)], axis=0)
    rope = _rope_tables()
    outs = {}
    for l in range(2):
        mod = _ada(cc, ada_w, ada_b, l)
        if l == 0:
            p = _inproj(x, norm1_g[l], mod, even_w_in[0].astype(BF16), rope, A_Q + A_KV)
            oa_p = _ctx_gqa(p, sink_a[0])
            oa_s = _win_attention(p, cache_a_k[:, 0], cache_a_v[:, 0], sink_a[0])
            zero = jnp.zeros((BATCH, B_HEADS // 2, LANES, LANES), F32)
            ob_p, sf, sb = _retention(p, 0, BATCH, SEQ, ret_decay_fwd[0], ret_decay_bwd[0], ret_gn_g[0], zero, zero)
            ob_s, _, _ = _retention(p, NP_TOK, DEC_BATCH, DEC_SEQ, ret_decay_fwd[0], ret_decay_bwd[0], ret_gn_g[0],
                                    _blockdiag_states(state_ret_fwd[:, 0]), _blockdiag_states(state_ret_bwd[:, 0]))
            w_out = even_w_out[0].astype(BF16)
            parts = [(oa_p, oa_s, w_out[:A_Q]), (ob_p, ob_s, w_out[A_Q:])]
            outs["a_k"] = p[:NP_TOK, A_Q:A_Q + A_KV].reshape(BATCH, 1, SEQ, A_KV_HEADS, HD)
            outs["a_v"] = p[:NP_TOK, A_Q + A_KV:A_Q + 2 * A_KV].reshape(BATCH, 1, SEQ, A_KV_HEADS, HD)
            outs["r_f"] = _diag_states(sf).reshape(BATCH, 1, B_HEADS, HD, HD)
            outs["r_b"] = _diag_states(sb).reshape(BATCH, 1, B_HEADS, HD, HD)
        else:
            p = _inproj(x, norm1_g[l], mod, odd_w_in[0].astype(BF16), rope, 0)
            o_p = _ctx_mha(p)
            o_s = _na_attention(p, cache_c_k[:, 0], cache_c_v[:, 0], na_rpb[0])
            parts = [(o_p, o_s, odd_w_out[0].astype(BF16))]
            outs["c_k"] = p[:NP_TOK, C_W:2 * C_W].reshape(BATCH, 1, SEQ, C_HEADS, HD)
            outs["c_v"] = p[:NP_TOK, 2 * C_W:3 * C_W].reshape(BATCH, 1, SEQ, C_HEADS, HD)
        x_mid, h, dest, gate_t, plan = _outproj(x, parts, mod, norm2_g[l], router_w[l], router_b[l])
        y = _experts(plan, _sc_scatter(h, dest, N_EXPERTS * N_TOK), exp_w1, exp_w3, exp_w2, l)
        y_rows = _sc_gather(y, dest.reshape(N_ASSIGN)).reshape(TOP_K, N_TOK, D // 2)
        x = _combine(x_mid, h, y_rows, gate_t.T, mod, sh_w1[l], sh_w3[l], sh_w2[l], final_g, final=(l == 1))
    y_prompt = x[0].reshape(BATCH, SEQ, D)
    y_sample = x[1].reshape(DEC_BATCH, DEC_SEQ, D)
    return (y_prompt, y_sample, outs["a_k"], outs["a_v"], outs["r_f"], outs["r_b"], outs["c_k"], outs["c_v"])
```

```python
import functools
import math

import jax
import jax.numpy as jnp
from jax import lax
from jax.experimental import pallas as pl
from jax.experimental.pallas import tpu as pltpu
from jax.experimental.pallas import tpu_sc as plsc

F32 = jnp.float32
BF16 = jnp.bfloat16
HIGHEST = lax.Precision.HIGHEST

D = 1024
BATCH = 32
SEQ = 256
DEC_BATCH = 4
DEC_SEQ = 4096
PAST = 256
GRID_W = 64
HD = 64
EPS = 1e-6
NEG = -1e30
ROPE_BASE = 10000.0
A_HEADS = 8
A_KV_HEADS = 2
A_Q = A_HEADS * HD
A_KV = A_KV_HEADS * HD
B_HEADS = 8
B_W = B_HEADS * HD
EVEN_IN = A_Q + 2 * A_KV + 4 * B_W
C_HEADS = 16
C_W = C_HEADS * HD
NA_KH = 8
NA_KW = 16
N_EXPERTS = 64
TOP_K = 8
N_GROUPS = 8
TOPK_GROUPS = 4
FF = 256
ROUTED_SCALE = 2.5
MOE_BLOCK = 512
RET_CHUNK = 128
RET_UNROLL = 2
RET_NORM_ROWS = 256
A_WINDOW = 128

NP_TOK = BATCH * SEQ
NS_TOK = DEC_BATCH * DEC_SEQ
N_TOK = NP_TOK + NS_TOK
N_ASSIGN = N_TOK * TOP_K
N_MOE_BLOCKS = (N_ASSIGN + N_EXPERTS * (MOE_BLOCK - 1) + MOE_BLOCK - 1) // MOE_BLOCK
PLAN_LANES = 512
assert N_TOK % MOE_BLOCK == 0 and N_MOE_BLOCKS <= PLAN_LANES

LANES = 128
TM = 512
NA_ROWS = 8
V7X_VMEM_LIMIT = 56 * 1024 * 1024

NT_DIMS = (((1,), (1,)), ((), ()))


def _cparams(n_axes, vmem=V7X_VMEM_LIMIT):
    return pltpu.CompilerParams(dimension_semantics=("arbitrary",) * n_axes, vmem_limit_bytes=vmem)


def _seg_of_block(i, rows):
    row0 = i * rows
    return jnp.where(row0 < NP_TOK, 0, 1 + (row0 - NP_TOK) // DEC_SEQ)


def _mod_spec(chunk, rows=TM, first_block=0):
    return pl.BlockSpec((None, 1, D), lambda i: (_seg_of_block(i + first_block, rows), 0, chunk))


def _pair_specs(width, rows=TM):
    npb = NP_TOK // rows
    nsb = NS_TOK // rows
    return [pl.BlockSpec((rows, width), lambda i: (jnp.minimum(i, npb - 1), 0)),
            pl.BlockSpec((rows, width), lambda i: (jnp.clip(i - npb, 0, nsb - 1), 0))]


def _pick_rows(p_ref, s_ref, rows=TM):
    return jnp.where(pl.program_id(0) < NP_TOK // rows, p_ref[...], s_ref[...])


def _silu(x):
    return x * jax.nn.sigmoid(x)


def _rms(x, g):
    return x * lax.rsqrt(jnp.mean(x * x, axis=-1, keepdims=True) + EPS) * g


def _lane_lo():
    return lax.broadcasted_iota(jnp.int32, (1, LANES), 1) < HD


def _ada_kernel(c_ref, w_ref, b_ref, o_ref):
    a = _silu(c_ref[...])
    o_ref[...] = jnp.dot(a, w_ref[...], preferred_element_type=F32, precision=HIGHEST) + b_ref[...]


def _ada(cc, w, b, layer):
    tn = 1536
    out = pl.pallas_call(
        _ada_kernel,
        grid=(6 * D // tn,),
        in_specs=[pl.BlockSpec((8, D), lambda j: (0, 0)),
                  pl.BlockSpec((None, D, tn), lambda j: (layer, 0, j)),
                  pl.BlockSpec((None, 1, tn), lambda j: (layer, 0, j))],
        out_specs=pl.BlockSpec((8, tn), lambda j: (0, j)),
        out_shape=jax.ShapeDtypeStruct((8, 6 * D), F32),
        compiler_params=_cparams(1),
        name="ada",
    )(cc, w, b.reshape(b.shape[0], 1, 6 * D))
    return out.reshape(8, 1, 6 * D)


def _inproj_kernel(xp_ref, xs_ref, g_ref, shift_ref, scale_ref, w_ref, cos_ref, sin_ref, o_ref, *, rope_cols):
    h = _rms(_pick_rows(xp_ref, xs_ref), g_ref[...]) * (1.0 + scale_ref[...]) + shift_ref[...]
    o = jnp.dot(h.astype(BF16), w_ref[...], preferred_element_type=F32)
    if rope_cols:
        cos = cos_ref[...]
        sin = sin_ref[...]
        lane = lax.broadcasted_iota(jnp.int32, (1, LANES), 1)
        first = (lane % 32) < 16
        for c in range(rope_cols // LANES):
            oc = o[:, c * LANES:(c + 1) * LANES]
            partner = jnp.where(first, pltpu.roll(oc, LANES - 16, 1), pltpu.roll(oc, 16, 1))
            o_ref[:, c * LANES:(c + 1) * LANES] = oc * cos + partner * sin
        o_ref[:, rope_cols:] = o[:, rope_cols:]
    else:
        o_ref[...] = o


def _rope_tables():
    half = HD // 2
    inv = ROPE_BASE ** (-jnp.arange(0, half, 2, dtype=F32) / half)
    t = jnp.arange(DEC_SEQ)
    ang_r = (t // GRID_W).astype(F32)[:, None] * inv[None]
    ang_c = (t % GRID_W).astype(F32)[:, None] * inv[None]

    def head(fn_r, fn_c, sign):
        return jnp.concatenate([sign[0] * fn_r, sign[1] * fn_r, sign[0] * fn_c, sign[1] * fn_c], axis=-1)

    cos = head(jnp.cos(ang_r), jnp.cos(ang_c), (1.0, 1.0))
    sin = head(jnp.sin(ang_r), jnp.sin(ang_c), (-1.0, 1.0))
    cos = jnp.concatenate([jnp.ones((TM, HD), F32), cos], axis=0)
    sin = jnp.concatenate([jnp.zeros((TM, HD), F32), sin], axis=0)
    return jnp.tile(cos, (1, 2)), jnp.tile(sin, (1, 2))


def _inproj(x, g, mod, w_bf16, rope, rope_cols):
    n_out = w_bf16.shape[1]
    npb = NP_TOK // TM
    spb = DEC_SEQ // TM

    def rope_map(i):
        return (jnp.where(i < npb, 0, 1 + (i - npb) % spb), 0)

    return pl.pallas_call(
        functools.partial(_inproj_kernel, rope_cols=rope_cols),
        grid=(N_TOK // TM,),
        in_specs=_pair_specs(D) + [
                  pl.BlockSpec((1, D), lambda i: (0, 0)),
                  _mod_spec(0), _mod_spec(1),
                  pl.BlockSpec((D, n_out), lambda i: (0, 0)),
                  pl.BlockSpec((TM, LANES), rope_map),
                  pl.BlockSpec((TM, LANES), rope_map)],
        out_specs=pl.BlockSpec((TM, n_out), lambda i: (i, 0)),
        out_shape=jax.ShapeDtypeStruct((N_TOK, n_out), F32),
        compiler_params=_cparams(1),
        name="inproj",
    )(x[0], x[1], g.reshape(1, D), mod, mod, w_bf16, rope[0], rope[1])


def _softmax_av(s_list, v_list, sink=None):
    mx = s_list[0].max(axis=-1, keepdims=True)
    for s in s_list[1:]:
        mx = jnp.maximum(mx, s.max(axis=-1, keepdims=True))
    if sink is not None:
        mx = jnp.maximum(mx, sink)
    den = jnp.exp(sink - mx) if sink is not None else 0.0
    acc = None
    for s, v in zip(s_list, v_list):
        p = jnp.exp(s - mx)
        den = den + p.sum(axis=-1, keepdims=True)
        pv = jnp.dot(p.astype(BF16), v, preferred_element_type=F32)
        acc = pv if acc is None else acc + pv
    return acc / den


def _dup_half(x, j, lo):
    xr = pltpu.roll(x, HD, 1)
    return jnp.where(lo, x, xr) if j == 0 else jnp.where(lo, xr, x)


def _stack_heads(q_ref, heads, lo, scale):
    parts = []
    for h in heads:
        qp = q_ref[:, (h // 2) * LANES:(h // 2 + 1) * LANES]
        keep = lo if h % 2 == 0 else jnp.logical_not(lo)
        parts.append(jnp.where(keep, qp, 0.0) * scale)
    return jnp.concatenate(parts, axis=0).astype(BF16)


def _sink_column(sink_ref, heads, rows):
    return jnp.concatenate([jnp.full((rows, 1), sink_ref[h], F32) for h in heads], axis=0)


def _ctx_gqa_kernel(sink_ref, q_ref, k_ref, v_ref, o_ref):
    lo = _lane_lo()
    k = k_ref[...]
    v = v_ref[...]
    group = A_HEADS // A_KV_HEADS
    for j in range(A_KV_HEADS):
        heads = list(range(group * j, group * (j + 1)))
        kd = _dup_half(k, j, lo).astype(BF16)
        vd = _dup_half(v, j, lo).astype(BF16)
        q = _stack_heads(q_ref, heads, lo, HD ** -0.5)
        s = lax.dot_general(q, kd, NT_DIMS, preferred_element_type=F32)
        o = _softmax_av([s], [vd], _sink_column(sink_ref, heads, SEQ))
        for t in range(group // 2):
            pair = heads[2 * t] // 2
            o_ref[:, pair * LANES:(pair + 1) * LANES] = jnp.where(
                lo, o[(2 * t) * SEQ:(2 * t + 1) * SEQ], o[(2 * t + 1) * SEQ:(2 * t + 2) * SEQ])


def _ctx_gqa(p, sink):
    return pl.pallas_call(
        _ctx_gqa_kernel,
        grid_spec=pltpu.PrefetchScalarGridSpec(
            num_scalar_prefetch=1,
            grid=(BATCH,),
            in_specs=[pl.BlockSpec((SEQ, A_Q), lambda b, s: (b, 0)),
                      pl.BlockSpec((SEQ, A_KV), lambda b, s: (b, A_Q // A_KV)),
                      pl.BlockSpec((SEQ, A_KV), lambda b, s: (b, A_Q // A_KV + 1))],
            out_specs=pl.BlockSpec((SEQ, A_Q), lambda b, s: (b, 0))),
        out_shape=jax.ShapeDtypeStruct((NP_TOK, A_Q), F32),
        compiler_params=_cparams(1),
        name="ctx_gqa",
    )(sink, p, p, p)


def _win_kernel(sink_ref, q_ref, kp_ref, kc_ref, kn_ref, vp_ref, vc_ref, vn_ref, ck_ref, cv_ref, o_ref):
    i = pl.program_id(1)
    lo = _lane_lo()
    k = jnp.concatenate([kp_ref[...], kc_ref[...], kn_ref[...]], axis=0)
    v = jnp.concatenate([vp_ref[...], vc_ref[...], vn_ref[...]], axis=0)
    ck = ck_ref[...]
    cv = cv_ref[...]
    group = A_HEADS // A_KV_HEADS
    n_keys = WIN_Q + 2 * A_WINDOW
    qpos = i * WIN_Q + lax.broadcasted_iota(jnp.int32, (WIN_Q, n_keys), 0)
    kpos = i * WIN_Q - A_WINDOW + lax.broadcasted_iota(jnp.int32, (WIN_Q, n_keys), 1)
    valid = (jnp.abs(kpos - qpos) <= A_WINDOW) & (kpos >= 0) & (kpos < DEC_SEQ)
    valid = jnp.concatenate([valid] * group, axis=0)
    s_loc, s_ctx, values = [], [], []
    for j in range(A_KV_HEADS):
        heads = list(range(group * j, group * (j + 1)))
        kd = _dup_half(k, j, lo).astype(BF16)
        ckd = _dup_half(ck, j, lo).astype(BF16)
        values.append((_dup_half(v, j, lo).astype(BF16), _dup_half(cv, j, lo).astype(BF16)))
        q = _stack_heads(q_ref, heads, lo, HD ** -0.5)
        s_loc.append(jnp.where(valid, lax.dot_general(q, kd, NT_DIMS, preferred_element_type=F32), NEG))
        s_ctx.append(lax.dot_general(q, ckd, NT_DIMS, preferred_element_type=F32))
    s_loc = jnp.concatenate(s_loc, axis=0)
    s_ctx = jnp.concatenate(s_ctx, axis=0)
    sink = _sink_column(sink_ref, list(range(A_HEADS)), WIN_Q)
    mx = jnp.maximum(jnp.maximum(s_loc.max(axis=-1, keepdims=True), s_ctx.max(axis=-1, keepdims=True)), sink)
    p_loc = jnp.exp(s_loc - mx)
    p_ctx = jnp.exp(s_ctx - mx)
    den = p_loc.sum(axis=-1, keepdims=True) + p_ctx.sum(axis=-1, keepdims=True) + jnp.exp(sink - mx)
    p_loc = p_loc.astype(BF16)
    p_ctx = p_ctx.astype(BF16)
    rows_per_group = group * WIN_Q
    for j, (vd, cvd) in enumerate(values):
        rows = slice(j * rows_per_group, (j + 1) * rows_per_group)
        o = (jnp.dot(p_loc[rows], vd, preferred_element_type=F32)
             + jnp.dot(p_ctx[rows], cvd, preferred_element_type=F32)) / den[rows]
        for t in range(group // 2):
            pair = (group * j) // 2 + t
            o_ref[:, pair * LANES:(pair + 1) * LANES] = jnp.where(
                lo, o[(2 * t) * WIN_Q:(2 * t + 1) * WIN_Q], o[(2 * t + 1) * WIN_Q:(2 * t + 2) * WIN_Q])


WIN_Q = 256


def _win_attention(p, cache_k, cache_v, sink):
    nblk = DEC_SEQ // WIN_Q
    side = WIN_Q // A_WINDOW
    nside = DEC_SEQ // A_WINDOW
    base = NP_TOK // WIN_Q
    side_base = NP_TOK // A_WINDOW
    kcol = A_Q // A_KV

    def main_spec(col):
        return pl.BlockSpec((WIN_Q, A_KV), lambda b, i, s: (base + b * nblk + i, col))

    def side_spec(col, off):
        return pl.BlockSpec((A_WINDOW, A_KV),
                            lambda b, i, s: (side_base + b * nside + jnp.clip(side * i + off, 0, nside - 1), col))

    ctx_spec = pl.BlockSpec((None, PAST, A_KV), lambda b, i, s: (b, 0, 0))
    return pl.pallas_call(
        _win_kernel,
        grid_spec=pltpu.PrefetchScalarGridSpec(
            num_scalar_prefetch=1,
            grid=(DEC_BATCH, nblk),
            in_specs=[pl.BlockSpec((WIN_Q, A_Q), lambda b, i, s: (base + b * nblk + i, 0)),
                      side_spec(kcol, -1), main_spec(kcol), side_spec(kcol, side),
                      side_spec(kcol + 1, -1), main_spec(kcol + 1), side_spec(kcol + 1, side),
                      ctx_spec, ctx_spec],
            out_specs=pl.BlockSpec((WIN_Q, A_Q), lambda b, i, s: (b * nblk + i, 0))),
        out_shape=jax.ShapeDtypeStruct((NS_TOK, A_Q), F32),
        compiler_params=_cparams(2),
        name="win_attn",
    )(sink, p, p, p, p, p, p, p, cache_k.reshape(DEC_BATCH, PAST, A_KV), cache_v.reshape(DEC_BATCH, PAST, A_KV))


def _ret_kernel(df_ref, db_ref, q_ref, k_ref, v_ref, g_ref, gn_ref, s0f_ref, s0b_ref,
                o_ref, sf_ref, sb_ref, of_scr, ob_scr, *, length):
    c_len = RET_CHUNK
    n = length // c_len
    lo = _lane_lo()
    hi = jnp.logical_not(lo)
    row = lax.broadcasted_iota(jnp.int32, (c_len, c_len), 0)
    col = lax.broadcasted_iota(jnp.int32, (c_len, c_len), 1)
    rowp = lax.broadcasted_iota(jnp.int32, (LANES, LANES), 0)
    colp = lax.broadcasted_iota(jnp.int32, (LANES, LANES), 1)
    blockdiag = (rowp < HD) == (colp < HD)
    idx = lax.broadcasted_iota(jnp.int32, (c_len, 1), 0).astype(F32)

    def direction(dec_ref, forward):
        lg = -jnp.exp(dec_ref[...])
        diff = (row - col) if forward else (col - row)
        keep = (diff >= 0) if forward else (diff > 0)
        dist = jnp.maximum(diff, 0).astype(F32)
        dm = jnp.concatenate([jnp.where(keep, jnp.exp(dist * lg[:, off:off + 1]), 0.0) for off in (0, HD)], axis=0)
        if forward:
            xi = jnp.exp((idx + 1.0) * lg)
            zeta = jnp.exp((c_len - 1.0 - idx) * lg)
        else:
            xi = jnp.exp((c_len - idx) * lg)
            zeta = jnp.exp(idx * lg)
        return dm, xi, zeta, jnp.exp(c_len * lg)

    def chunk(c, state, consts):
        dm, xi, zeta, gch = consts
        rows = pl.ds(pl.multiple_of(c * c_len, c_len), c_len)
        qc = q_ref[rows, :]
        kc = k_ref[rows, :] * HD ** -0.5
        vc = v_ref[rows, :].astype(BF16)
        kb = kc.astype(BF16)
        q2 = jnp.concatenate([jnp.where(lo, qc, 0.0), jnp.where(hi, qc, 0.0)], axis=0).astype(BF16)
        inner = lax.dot_general(q2, kb, NT_DIMS, preferred_element_type=F32) * dm
        kz_t = (kc * zeta).T
        res = jnp.dot(jnp.concatenate([inner, kz_t], axis=0).astype(BF16), vc, preferred_element_type=F32)
        cross = jnp.dot(qc.astype(BF16), state.astype(BF16), preferred_element_type=F32) * xi
        o = jnp.where(lo, res[:c_len], res[c_len:2 * c_len]) + cross
        state = gch * state + jnp.where(blockdiag, res[2 * c_len:], 0.0)
        return rows, o, state

    cf = direction(df_ref, True)
    cb = direction(db_ref, False)

    def scan_body(t, states):
        rows_f, o_f, state_f = chunk(t, states[0], cf)
        of_scr[rows_f, :] = o_f
        rows_b, o_b, state_b = chunk(n - 1 - t, states[1], cb)
        ob_scr[rows_b, :] = o_b
        return state_f, state_b

    state_f, state_b = lax.fori_loop(0, n, scan_body, (s0f_ref[...], s0b_ref[...]), unroll=min(n, RET_UNROLL))
    sf_ref[...] = state_f
    sb_ref[...] = state_b

    gn = gn_ref[...]
    n_norm = length // RET_NORM_ROWS

    def per_head(x):
        a = jnp.where(lo, x, 0.0).sum(axis=-1, keepdims=True)
        b = jnp.where(hi, x, 0.0).sum(axis=-1, keepdims=True)
        return jnp.where(lo, a, b) * (1.0 / HD)

    def norm_body(t, carry):
        rows = pl.ds(pl.multiple_of(t * RET_NORM_ROWS, RET_NORM_ROWS), RET_NORM_ROWS)
        o = of_scr[rows, :] + ob_scr[rows, :]
        d = o - per_head(o)
        y = d * lax.rsqrt(per_head(d * d) + EPS) * gn
        o_ref[rows, :] = _silu(g_ref[rows, :]) * y
        return carry

    lax.fori_loop(0, n_norm, norm_body, 0)


def _pair_lanes(v):
    return jnp.repeat(v.astype(F32), HD).reshape(B_HEADS // 2, 1, LANES)


def _blockdiag_states(s):
    b = s.shape[0]
    s = s.astype(F32).reshape(b, B_HEADS // 2, 2, HD, HD)
    z = jnp.zeros_like(s[:, :, 0])
    top = jnp.concatenate([s[:, :, 0], z], axis=-1)
    bot = jnp.concatenate([z, s[:, :, 1]], axis=-1)
    return jnp.concatenate([top, bot], axis=-2)


def _diag_states(sp):
    b = sp.shape[0]
    s = jnp.stack([sp[:, :, :HD, :HD], sp[:, :, HD:, HD:]], axis=2)
    return s.reshape(b, B_HEADS, HD, HD)


def _retention(p, row_base, batch, length, dec_f, dec_b, gn_g, s0f, s0b):
    npairs = B_HEADS // 2
    blk0 = row_base // length
    qcol = (A_Q + 2 * A_KV) // LANES

    def col_spec(off):
        return pl.BlockSpec((length, LANES), lambda b, h: (blk0 + b, qcol + off * npairs + h))

    lane_spec = pl.BlockSpec((None, 1, LANES), lambda b, h: (h, 0, 0))
    state_spec = pl.BlockSpec((None, None, LANES, LANES), lambda b, h: (b, h, 0, 0))
    state_shape = jax.ShapeDtypeStruct((batch, npairs, LANES, LANES), F32)
    return pl.pallas_call(
        functools.partial(_ret_kernel, length=length),
        grid=(batch, npairs),
        in_specs=[lane_spec, lane_spec, col_spec(0), col_spec(1), col_spec(2), col_spec(3), lane_spec,
                  state_spec, state_spec],
        out_specs=[pl.BlockSpec((length, LANES), lambda b, h: (b, h)), state_spec, state_spec],
        out_shape=[jax.ShapeDtypeStruct((batch * length, B_W), F32), state_shape, state_shape],
        scratch_shapes=[pltpu.VMEM((length, LANES), F32), pltpu.VMEM((length, LANES), F32)],
        compiler_params=_cparams(2),
        name="retention",
    )(_pair_lanes(dec_f), _pair_lanes(dec_b), p, p, p, p, gn_g.reshape(npairs, 1, LANES), s0f, s0b)


def _ctx_mha_kernel(q_ref, k_ref, v_ref, o_ref):
    lo = _lane_lo()
    for pair in range(C_HEADS // 2):
        cols = slice(pair * LANES, (pair + 1) * LANES)
        q = _stack_heads(q_ref, [2 * pair, 2 * pair + 1], lo, HD ** -0.5)
        s = lax.dot_general(q, k_ref[:, cols].astype(BF16), NT_DIMS, preferred_element_type=F32)
        o = _softmax_av([s], [v_ref[:, cols].astype(BF16)])
        o_ref[:, cols] = jnp.where(lo, o[:SEQ], o[SEQ:])


def _ctx_mha(p):
    return pl.pallas_call(
        _ctx_mha_kernel,
        grid=(BATCH,),
        in_specs=[pl.BlockSpec((SEQ, C_W), lambda b: (b, 0)),
                  pl.BlockSpec((SEQ, C_W), lambda b: (b, 1)),
                  pl.BlockSpec((SEQ, C_W), lambda b: (b, 2))],
        out_specs=pl.BlockSpec((SEQ, C_W), lambda b: (b, 0)),
        out_shape=jax.ShapeDtypeStruct((NP_TOK, C_W), F32),
        compiler_params=_cparams(1),
        name="ctx_mha",
    )(p, p, p)


NA_WIN_ROWS = 2 * NA_ROWS
NA_WIN = NA_WIN_ROWS * GRID_W
NA_QROWS = NA_ROWS * GRID_W
NA_PAD_ROWS = NA_KH // 2
NA_TABLE = 1536


def _na_kernel(q_ref, kp_ref, km_ref, kn_ref, vp_ref, vm_ref, vn_ref, ck_ref, cv_ref, ue_ref, uo_ref, o_ref):
    r0 = pl.program_id(2) * NA_ROWS
    n_rows = DEC_SEQ // GRID_W
    lo = _lane_lo()
    k = jnp.concatenate([kp_ref[...], km_ref[...], kn_ref[...]], axis=0).astype(BF16)
    v = jnp.concatenate([vp_ref[...], vm_ref[...], vn_ref[...]], axis=0).astype(BF16)
    ck = ck_ref[...].astype(BF16)
    cv = cv_ref[...].astype(BF16)
    q = q_ref[...] * HD ** -0.5
    klane = lax.broadcasted_iota(jnp.int32, (1, NA_WIN), 1)
    q2 = jnp.concatenate([jnp.where(lo, q, 0.0), jnp.where(lo, 0.0, q)], axis=0).astype(BF16)
    s = lax.dot_general(q2, k, NT_DIMS, preferred_element_type=F32)
    s_ctx = lax.dot_general(q2, ck, NT_DIMS, preferred_element_type=F32)
    pieces = []
    for half in range(2):
        for rq in range(NA_ROWS):
            start = NA_KH - 1 - rq
            if start % 2 == 0:
                u = ue_ref[half, :, start * GRID_W:start * GRID_W + NA_WIN]
            else:
                u = uo_ref[half, :, (start - 1) * GRID_W:(start - 1) * GRID_W + NA_WIN]
            r = r0 + rq
            first = jnp.clip(r - NA_KH // 2, 0, n_rows - NA_KH)
            lane0 = (first - r0 + NA_PAD_ROWS) * GRID_W
            in_rows = (klane >= lane0) & (klane < lane0 + NA_KH * GRID_W)
            row0 = half * NA_QROWS + rq * GRID_W
            pieces.append(jnp.where(in_rows, s[row0:row0 + GRID_W] + u, NEG))
    o = _softmax_av([jnp.concatenate(pieces, axis=0), s_ctx], [v, cv])
    o_ref[...] = jnp.where(lo, o[:NA_QROWS], o[NA_QROWS:])


def _na_bias_tables(rpb):
    cq = jnp.arange(GRID_W)
    ck = jnp.arange(GRID_W)
    dc = jnp.clip(ck[None] - cq[:, None], -(NA_KW - 1), NA_KW - 1) + NA_KW - 1
    cs = jnp.clip(cq - NA_KW // 2, 0, GRID_W - NA_KW)
    col_ok = (ck[None] >= cs[:, None]) & (ck[None] < cs[:, None] + NA_KW)
    t = rpb.astype(F32)[:, :, dc]
    t = jnp.where(col_ok[None, None], t, NEG).transpose(0, 2, 1, 3)
    n_dr = 2 * NA_KH - 1
    blocks = NA_TABLE // GRID_W
    t = jnp.pad(t, ((0, 0), (0, 0), (NA_PAD_ROWS, blocks - n_dr - NA_PAD_ROWS), (0, 0)), constant_values=NEG)
    ue = t.reshape(C_HEADS, GRID_W, NA_TABLE)
    uo = jnp.concatenate([ue[..., GRID_W:], jnp.full((C_HEADS, GRID_W, GRID_W), NEG, F32)], axis=-1)
    return ue, uo


def _na_attention(p, cache_k, cache_v, rpb):
    npairs = C_HEADS // 2
    nrb = DEC_SEQ // NA_QROWS
    half = NA_QROWS // 2
    qbase = NP_TOK // NA_QROWS
    hbase = NP_TOK // half
    kcol = C_W // LANES
    ue, uo = _na_bias_tables(rpb)

    def main_spec(col0):
        return pl.BlockSpec((NA_QROWS, LANES), lambda b, h, r: (qbase + b * nrb + r, col0 + h))

    def side_spec(col0, off):
        return pl.BlockSpec((half, LANES),
                            lambda b, h, r: (hbase + b * 2 * nrb + jnp.clip(2 * r + off, 0, 2 * nrb - 1), col0 + h))

    ctx_spec = pl.BlockSpec((None, PAST, LANES), lambda b, h, r: (b, 0, h))
    tab_spec = pl.BlockSpec((2, GRID_W, NA_TABLE), lambda b, h, r: (h, 0, 0))
    return pl.pallas_call(
        _na_kernel,
        grid=(DEC_BATCH, npairs, nrb),
        in_specs=[main_spec(0),
                  side_spec(kcol, -1), main_spec(kcol), side_spec(kcol, 2),
                  side_spec(2 * kcol, -1), main_spec(2 * kcol), side_spec(2 * kcol, 2),
                  ctx_spec, ctx_spec, tab_spec, tab_spec],
        out_specs=pl.BlockSpec((NA_QROWS, LANES), lambda b, h, r: (b * nrb + r, h)),
        out_shape=jax.ShapeDtypeStruct((NS_TOK, C_W), F32),
        compiler_params=_cparams(3),
        name="na_attn",
    )(p, p, p, p, p, p, p, cache_k.reshape(DEC_BATCH, PAST, C_W), cache_v.reshape(DEC_BATCH, PAST, C_W), ue, uo)


def _route(biased, scores):
    t = biased.shape[1]
    per_group = N_EXPERTS // N_GROUPS
    i8 = lax.broadcasted_iota(jnp.int32, (per_group, t), 0)
    g_rows = []
    for g in range(N_GROUPS):
        bg = biased[g * per_group:(g + 1) * per_group]
        m1 = bg.max(axis=0, keepdims=True)
        first = jnp.where(bg == m1, i8, per_group).min(axis=0, keepdims=True)
        m2 = jnp.where(i8 == first, -jnp.inf, bg).max(axis=0, keepdims=True)
        g_rows.append(m1 + m2)
    g_top = jnp.concatenate(g_rows, axis=0)
    gi = lax.broadcasted_iota(jnp.int32, g_top.shape, 0)
    g_sel = jnp.zeros(g_top.shape, jnp.int32)
    cur = g_top
    for _ in range(TOPK_GROUPS):
        m = cur.max(axis=0, keepdims=True)
        hit = gi == jnp.where(cur == m, gi, N_GROUPS).min(axis=0, keepdims=True)
        g_sel = jnp.where(hit, 1, g_sel)
        cur = jnp.where(hit, -jnp.inf, cur)
    e_sel = jnp.concatenate([jnp.broadcast_to(g_sel[g:g + 1], (per_group, t)) for g in range(N_GROUPS)], axis=0)
    cur = jnp.where(e_sel > 0, biased, NEG)
    ei = lax.broadcasted_iota(jnp.int32, cur.shape, 0)
    ids, gates, hits = [], [], []
    for _ in range(TOP_K):
        m = cur.max(axis=0, keepdims=True)
        f = jnp.where(cur == m, ei, N_EXPERTS).min(axis=0, keepdims=True)
        hit = ei == f
        ids.append(f)
        hits.append(hit)
        gates.append(jnp.where(hit, scores, 0.0).sum(axis=0, keepdims=True))
        cur = jnp.where(hit, -jnp.inf, cur)
    gate = jnp.concatenate(gates, axis=0)
    gate = gate / gate.sum(axis=0, keepdims=True) * ROUTED_SCALE
    return jnp.concatenate(ids, axis=0), gate, hits


def _pack_bf16_pairs(h):
    bits = lax.bitcast_convert_type(h.astype(BF16).astype(F32), jnp.uint32)
    return bits[:, :D // 2] | (bits[:, D // 2:] >> 16)


def _unpack_bf16_pairs(xp):
    hi = lax.bitcast_convert_type(xp & jnp.uint32(0xFFFF0000), F32).astype(BF16)
    lo = lax.bitcast_convert_type(xp << 16, F32).astype(BF16)
    return hi, lo


def _dot_halves(hi, lo, w_ref):
    return (jnp.dot(hi, w_ref[:D // 2, :], preferred_element_type=F32)
            + jnp.dot(lo, w_ref[D // 2:, :], preferred_element_type=F32))


def _outproj_kernel(*refs, n_parts):
    xp_ref, xs_ref = refs[:2]
    part_refs = refs[2:2 + 3 * n_parts]
    gate_ref, shift_ref, scale_ref, g2_ref, rw_ref, rb_ref = refs[2 + 3 * n_parts:8 + 3 * n_parts]
    xo_ref, h_ref, dest_ref, wgt_ref, plan_ref, cnt_ref = refs[8 + 3 * n_parts:]
    step = pl.program_id(0)

    @pl.when(step == 0)
    def _():
        cnt_ref[...] = jnp.zeros_like(cnt_ref)
        plan_ref[...] = jnp.zeros_like(plan_ref)

    y = None
    for t in range(n_parts):
        ap_ref, as_ref, w_ref = part_refs[3 * t:3 * t + 3]
        d = jnp.dot(_pick_rows(ap_ref, as_ref).astype(BF16), w_ref[...], preferred_element_type=F32)
        y = d if y is None else y + d
    x = _pick_rows(xp_ref, xs_ref) + gate_ref[...] * y
    xo_ref[...] = x
    h = _rms(x, g2_ref[...]) * (1.0 + scale_ref[...]) + shift_ref[...]
    h_ref[...] = _pack_bf16_pairs(h)
    logits = lax.dot_general(rw_ref[...], h, NT_DIMS, preferred_element_type=F32, precision=HIGHEST)
    scores = jax.nn.sigmoid(logits)
    _, gate, hits = _route(scores + rb_ref[...], scores)
    wgt_ref[...] = gate
    chosen = hits[0]
    for hit in hits[1:]:
        chosen = chosen | hit
    m = jnp.where(chosen, 1.0, 0.0)
    before = (lax.broadcasted_iota(jnp.int32, (TM, TM), 0) < lax.broadcasted_iota(jnp.int32, (TM, TM), 1))
    prefix = jnp.dot(m.astype(BF16), jnp.where(before, 1.0, 0.0).astype(BF16), preferred_element_type=F32)
    e_base = (lax.broadcasted_iota(jnp.int32, (N_EXPERTS, 1), 0) * N_TOK).astype(F32)
    row_all = prefix + (cnt_ref[...] + e_base)
    dest_ref[...] = jnp.concatenate(
        [jnp.where(hit, row_all, 0.0).sum(axis=0, keepdims=True) for hit in hits], axis=0).astype(jnp.int32)
    cnt_ref[...] += m.sum(axis=1, keepdims=True)

    @pl.when(step == pl.num_programs(0) - 1)
    def _():
        _block_plan(cnt_ref[...], plan_ref)


def _block_plan(counts, plan_ref):
    cap_blocks = N_TOK // MOE_BLOCK
    nblk = ((counts.astype(jnp.int32) + (MOE_BLOCK - 1)) // MOE_BLOCK).astype(F32)
    lower = (lax.broadcasted_iota(jnp.int32, (N_EXPERTS, N_EXPERTS), 0)
             >= lax.broadcasted_iota(jnp.int32, (N_EXPERTS, N_EXPERTS), 1))
    cum = jnp.dot(jnp.where(lower, 1.0, 0.0).astype(BF16), jnp.broadcast_to(nblk, (N_EXPERTS, LANES)).astype(BF16),
                  preferred_element_type=F32)[:, :1]
    n_used = cum[N_EXPERTS - 1:, :]
    slot = jnp.minimum(lax.broadcasted_iota(jnp.int32, (1, PLAN_LANES), 1).astype(F32), n_used - 1.0)
    done = cum <= slot
    expert = jnp.where(done, 1.0, 0.0).sum(axis=0, keepdims=True)
    blocks_before = jnp.where(done, nblk, 0.0).sum(axis=0, keepdims=True)
    plan_ref[0:1, :] = (expert * cap_blocks + (slot - blocks_before)).astype(jnp.int32)
    plan_ref[1:2, :] = expert.astype(jnp.int32)
    plan_ref[2:3, :] = jnp.broadcast_to(n_used, (1, PLAN_LANES)).astype(jnp.int32)


def _outproj(x, parts, mod, g2, router_w, router_b):
    in_specs = _pair_specs(D)
    args = [x[0], x[1]]
    for ap, a_s, w in parts:
        width = ap.shape[1]
        in_specs += _pair_specs(width) + [pl.BlockSpec((width, D), lambda i: (0, 0))]
        args += [ap, a_s, w]
    in_specs += [_mod_spec(2), _mod_spec(3), _mod_spec(4),
                 pl.BlockSpec((1, D), lambda i: (0, 0)),
                 pl.BlockSpec((N_EXPERTS, D), lambda i: (0, 0)),
                 pl.BlockSpec((N_EXPERTS, 1), lambda i: (0, 0))]
    args += [mod, mod, mod, g2.reshape(1, D), router_w.T, router_b.reshape(N_EXPERTS, 1)]
    return pl.pallas_call(
        functools.partial(_outproj_kernel, n_parts=len(parts)),
        grid=(N_TOK // TM,),
        in_specs=in_specs,
        out_specs=[pl.BlockSpec((TM, D), lambda i: (i, 0)),
                   pl.BlockSpec((TM, D // 2), lambda i: (i, 0)),
                   pl.BlockSpec((TOP_K, TM), lambda i: (0, i)),
                   pl.BlockSpec((TOP_K, TM), lambda i: (0, i)),
                   pl.BlockSpec((8, PLAN_LANES), lambda i: (0, 0))],
        out_shape=[jax.ShapeDtypeStruct((N_TOK, D), F32),
                   jax.ShapeDtypeStruct((N_TOK, D // 2), jnp.uint32),
                   jax.ShapeDtypeStruct((TOP_K, N_TOK), jnp.int32),
                   jax.ShapeDtypeStruct((TOP_K, N_TOK), F32),
                   jax.ShapeDtypeStruct((8, PLAN_LANES), jnp.int32)],
        scratch_shapes=[pltpu.VMEM((N_EXPERTS, 1), F32)],
        compiler_params=_cparams(1),
        name="outproj_router",
    )(*args)


def _experts_kernel(br_ref, be_ref, nu_ref, x_ref, w1_ref, w3_ref, w2_ref, o_ref, w1b, w3b, w2b):
    i = pl.program_id(0)
    e = be_ref[i]
    prev = be_ref[jnp.maximum(i - 1, 0)]

    @pl.when((i == 0) | (e != prev))
    def _():
        w1b[...] = w1_ref[...].astype(BF16)
        w3b[...] = w3_ref[...].astype(BF16)
        w2b[...] = w2_ref[...].astype(BF16)

    @pl.when(i < nu_ref[0])
    def _():
        hi, lo = _unpack_bf16_pairs(x_ref[...])
        a = _dot_halves(hi, lo, w1b)
        b = _dot_halves(hi, lo, w3b)
        h = (_silu(a) * b).astype(BF16)
        o_ref[...] = _pack_bf16_pairs(jnp.dot(h, w2b[...], preferred_element_type=F32))


def _experts(plan, x_rows, w1, w3, w2, layer):
    return pl.pallas_call(
        _experts_kernel,
        grid_spec=pltpu.PrefetchScalarGridSpec(
            num_scalar_prefetch=3,
            grid=(N_MOE_BLOCKS,),
            in_specs=[pl.BlockSpec((MOE_BLOCK, D // 2), lambda i, br, be, nu: (br[i], 0)),
                      pl.BlockSpec((None, None, D, FF), lambda i, br, be, nu: (layer, be[i], 0, 0)),
                      pl.BlockSpec((None, None, D, FF), lambda i, br, be, nu: (layer, be[i], 0, 0)),
                      pl.BlockSpec((None, None, FF, D), lambda i, br, be, nu: (layer, be[i], 0, 0))],
            out_specs=pl.BlockSpec((MOE_BLOCK, D // 2), lambda i, br, be, nu: (br[i], 0)),
            scratch_shapes=[pltpu.VMEM((D, FF), BF16), pltpu.VMEM((D, FF), BF16), pltpu.VMEM((FF, D), BF16)]),
        out_shape=jax.ShapeDtypeStruct(x_rows.shape, jnp.uint32),
        compiler_params=_cparams(1),
        name="experts",
    )(plan[0], plan[1], plan[2, :1], x_rows, w1, w3, w2)


SC_CORES = 2
SC_SUBCORES = 16
SC_WORKERS = SC_CORES * SC_SUBCORES
SC_CHUNK_BYTES = 64 * 1024
SC_SLOTS = 4


def _sc_scatter(rows, dest, n_out):
    n_rows, width = rows.shape
    picks = dest.shape[0]
    chunk = SC_CHUNK_BYTES // (4 * width)
    per_worker = n_rows // SC_WORKERS
    n_chunks = per_worker // chunk
    assert per_worker * SC_WORKERS == n_rows and n_chunks * chunk == per_worker and n_chunks % 2 == 0
    mesh = plsc.VectorSubcoreMesh(core_axis_name="c", subcore_axis_name="s")

    @functools.partial(
        pl.kernel, mesh=mesh,
        out_type=jax.ShapeDtypeStruct((n_out, width), rows.dtype),
        scratch_types=[pltpu.VMEM((picks, n_chunks, chunk), jnp.int32),
                       pltpu.VMEM((2, chunk, width), rows.dtype),
                       pltpu.SemaphoreType.DMA((2,)),
                       pltpu.SemaphoreType.DMA((2,))])
    def scatter(r_hbm, d_hbm, o_hbm, idx_v, rows_v, lsem, ssem):
        worker = lax.axis_index("s") * SC_CORES + lax.axis_index("c")
        base = worker * per_worker
        for k in range(picks):
            pltpu.sync_copy(d_hbm.at[k, worker], idx_v.at[k])

        def load_copy(c, b):
            src = pl.ds(pl.multiple_of(base + c * chunk, chunk), chunk)
            return pltpu.make_async_copy(r_hbm.at[src], rows_v.at[b], lsem.at[b])

        def store_copy(c, b, k):
            return pltpu.make_async_copy(rows_v.at[b], o_hbm.at[idx_v.at[k, c]], ssem.at[b])

        load_copy(0, 0).start()

        @pl.loop(0, n_chunks, step=2)
        def _(c0):
            for b in range(2):
                c = c0 + b
                load_copy(c, b).wait()
                for k in range(picks):
                    store_copy(c, b, k).start()

                @pl.when(c > 0)
                def _():
                    for k in range(picks):
                        store_copy(c - 1, 1 - b, k).wait()

                @pl.when(c + 1 < n_chunks)
                def _():
                    load_copy(c + 1, 1 - b).start()

        for k in range(picks):
            store_copy(n_chunks - 1, 1, k).wait()

    return scatter(rows, dest.reshape(picks, SC_WORKERS, n_chunks, chunk))


def _sc_gather(table, idx):
    n_idx = idx.shape[0]
    width = table.shape[1]
    chunk = SC_CHUNK_BYTES // (4 * width)
    per_worker = n_idx // SC_WORKERS
    n_chunks = per_worker // chunk
    ahead = SC_SLOTS - 1
    assert per_worker * SC_WORKERS == n_idx and n_chunks * chunk == per_worker and n_chunks % SC_SLOTS == 0
    mesh = plsc.VectorSubcoreMesh(core_axis_name="c", subcore_axis_name="s")

    @functools.partial(
        pl.kernel, mesh=mesh,
        out_type=jax.ShapeDtypeStruct((n_idx, width), table.dtype),
        scratch_types=[pltpu.VMEM((per_worker,), jnp.int32),
                       pltpu.VMEM((SC_SLOTS, chunk, width), table.dtype),
                       pltpu.SemaphoreType.DMA((SC_SLOTS,)),
                       pltpu.SemaphoreType.DMA((SC_SLOTS,))])
    def gather(t_hbm, i_hbm, o_hbm, idx_v, rows_v, gsem, wsem):
        worker = lax.axis_index("s") * SC_CORES + lax.axis_index("c")
        base = worker * per_worker
        pltpu.sync_copy(i_hbm.at[pl.ds(pl.multiple_of(base, chunk), per_worker)], idx_v)

        def gather_copy(c, b):
            ids = idx_v.at[pl.ds(pl.multiple_of(c * chunk, chunk), chunk)]
            return pltpu.make_async_copy(t_hbm.at[ids], rows_v.at[b], gsem.at[b])

        def write_copy(c, b):
            rows = pl.ds(pl.multiple_of(base + c * chunk, chunk), chunk)
            return pltpu.make_async_copy(rows_v.at[b], o_hbm.at[rows], wsem.at[b])

        for c in range(ahead):
            gather_copy(c, c).start()

        @pl.loop(0, n_chunks, step=SC_SLOTS)
        def _(c0):
            for b in range(SC_SLOTS):
                c = c0 + b
                refill = (b + ahead) % SC_SLOTS
                gather_copy(c, b).wait()
                write_copy(c, b).start()

                @pl.when(c > 0)
                def _():
                    write_copy(c - 1, refill).wait()

                @pl.when(c + ahead < n_chunks)
                def _():
                    gather_copy(c + ahead, refill).start()

        write_copy(n_chunks - 1, (n_chunks - 1) % SC_SLOTS).wait()

    return gather(table, idx)


TC = 512


def _combine_kernel(x_ref, h_ref, y_ref, wgt_ref, gate_ref, w1_ref, w3_ref, w2_ref, fg_ref, o_ref, *, final):
    hi, lo = _unpack_bf16_pairs(h_ref[...])
    a = _dot_halves(hi, lo, w1_ref)
    b = _dot_halves(hi, lo, w3_ref)
    ffn = jnp.dot((_silu(a) * b).astype(BF16), w2_ref[...], preferred_element_type=F32)
    wgt = wgt_ref[...]
    r_hi = None
    r_lo = None
    for k in range(TOP_K):
        yk = y_ref[k]
        w = wgt[:, k:k + 1]
        t_hi = lax.bitcast_convert_type(yk & jnp.uint32(0xFFFF0000), F32) * w
        t_lo = lax.bitcast_convert_type(yk << 16, F32) * w
        r_hi = t_hi if r_hi is None else r_hi + t_hi
        r_lo = t_lo if r_lo is None else r_lo + t_lo
    x = x_ref[...] + gate_ref[...] * (ffn + jnp.concatenate([r_hi, r_lo], axis=1))
    o_ref[...] = _rms(x, fg_ref[...]) if final else x


def _combine(x, h, y_rows, wgt, mod, sw1, sw3, sw2, final_g, final):
    weights = (sw1.astype(BF16), sw3.astype(BF16), sw2.astype(BF16), final_g.reshape(1, D))

    def rows_from(first_row, n_rows):
        b0 = first_row // TC
        return pl.pallas_call(
            functools.partial(_combine_kernel, final=final),
            grid=(n_rows // TC,),
            in_specs=[pl.BlockSpec((TC, D), lambda i: (i + b0, 0)),
                      pl.BlockSpec((TC, D // 2), lambda i: (i + b0, 0)),
                      pl.BlockSpec((TOP_K, TC, D // 2), lambda i: (0, i + b0, 0)),
                      pl.BlockSpec((TC, TOP_K), lambda i: (i + b0, 0)),
                      _mod_spec(5, TC, b0),
                      pl.BlockSpec((D, FF), lambda i: (0, 0)),
                      pl.BlockSpec((D, FF), lambda i: (0, 0)),
                      pl.BlockSpec((FF, D), lambda i: (0, 0)),
                      pl.BlockSpec((1, D), lambda i: (0, 0))],
            out_specs=pl.BlockSpec((TC, D), lambda i: (i, 0)),
            out_shape=jax.ShapeDtypeStruct((n_rows, D), F32),
            compiler_params=_cparams(1),
            name="combine",
        )(x, h, y_rows, wgt, mod, *weights)

    return rows_from(0, NP_TOK), rows_from(NP_TOK, NS_TOK)


def kernel(x_prompt, x_sample, cache_a_k, cache_a_v, state_ret_fwd, state_ret_bwd, cache_c_k, cache_c_v,
           c, c_ctx, norm1_g, norm2_g, ada_w, ada_b, even_w_in, even_w_out, sink_a, ret_decay_fwd,
           ret_decay_bwd, ret_gn_g, odd_w_in, odd_w_out, na_rpb, router_w, router_b, exp_w1, exp_w3,
           exp_w2, sh_w1, sh_w3, sh_w2, final_g):
    x = (x_prompt.reshape(NP_TOK, D), x_sample.reshape(NS_TOK, D))
    cc = jnp.concatenate([c_ctx[None], c, jnp.zeros((8 - 1 - DEC_BATCH, D), F32)], axis=0)
    rope = _rope_tables()
    outs = {}
    for l in range(2):
        mod = _ada(cc, ada_w, ada_b, l)
        if l == 0:
            p = _inproj(x, norm1_g[l], mod, even_w_in[0].astype(BF16), rope, A_Q + A_KV)
            oa_p = _ctx_gqa(p, sink_a[0])
            oa_s = _win_attention(p, cache_a_k[:, 0], cache_a_v[:, 0], sink_a[0])
            zero = jnp.zeros((BATCH, B_HEADS // 2, LANES, LANES), F32)
            ob_p, sf, sb = _retention(p, 0, BATCH, SEQ, ret_decay_fwd[0], ret_decay_bwd[0], ret_gn_g[0], zero, zero)
            ob_s, _, _ = _retention(p, NP_TOK, DEC_BATCH, DEC_SEQ, ret_decay_fwd[0], ret_decay_bwd[0], ret_gn_g[0],
                                    _blockdiag_states(state_ret_fwd[:, 0]), _blockdiag_states(state_ret_bwd[:, 0]))
            w_out = even_w_out[0].astype(BF16)
            parts = [(oa_p, oa_s, w_out[:A_Q]), (ob_p, ob_s, w_out[A_Q:])]
            outs["a_k"] = p[:NP_TOK, A_Q:A_Q + A_KV].reshape(BATCH, 1, SEQ, A_KV_HEADS, HD)
            outs["a_v"] = p[:NP_TOK, A_Q + A_KV:A_Q + 2 * A_KV].reshape(BATCH, 1, SEQ, A_KV_HEADS, HD)
            outs["r_f"] = _diag_states(sf).reshape(BATCH, 1, B_HEADS, HD, HD)
            outs["r_b"] = _diag_states(sb).reshape(BATCH, 1, B_HEADS, HD, HD)
        else:
            p = _inproj(x, norm1_g[l], mod, odd_w_in[0].astype(BF16), rope, 0)
            o_p = _ctx_mha(p)
            o_s = _na_attention(p, cache_c_k[:, 0], cache_c_v[:, 0], na_rpb[0])
            parts = [(o_p, o_s, odd_w_out[0].astype(BF16))]
            outs["c_k"] = p[:NP_TOK, C_W:2 * C_W].reshape(BATCH, 1, SEQ, C_HEADS, HD)
            outs["c_v"] = p[:NP_TOK, 2 * C_W:3 * C_W].reshape(BATCH, 1, SEQ, C_HEADS, HD)
        x_mid, h, dest, gate_t, plan = _outproj(x, parts, mod, norm2_g[l], router_w[l], router_b[l])
        y = _experts(plan, _sc_scatter(h, dest, N_EXPERTS * N_TOK), exp_w1, exp_w3, exp_w2, l)
        y_rows = _sc_gather(y, dest.reshape(N_ASSIGN)).reshape(TOP_K, N_TOK, D // 2)
        x = _combine(x_mid, h, y_rows, gate_t.T, mod, sh_w1[l], sh_w3[l], sh_w2[l], final_g, final=(l == 1))
    y_prompt = x[0].reshape(BATCH, SEQ, D)
    y_sample = x[1].reshape(DEC_BATCH, DEC_SEQ, D)
    return (y_prompt, y_sample, outs["a_k"], outs["a_v"], outs["r_f"], outs["r_b"], outs["c_k"], outs["c_v"])
```

```python
import functools
import math

import jax
import jax.numpy as jnp
from jax import lax
from jax.experimental import pallas as pl
from jax.experimental.pallas import tpu as pltpu
from jax.experimental.pallas import tpu_sc as plsc

F32 = jnp.float32
BF16 = jnp.bfloat16
HIGHEST = lax.Precision.HIGHEST

D = 1024
BATCH = 32
SEQ = 256
DEC_BATCH = 4
DEC_SEQ = 4096
PAST = 256
GRID_W = 64
HD = 64
EPS = 1e-6
NEG = -1e30
ROPE_BASE = 10000.0
A_HEADS = 8
A_KV_HEADS = 2
A_Q = A_HEADS * HD
A_KV = A_KV_HEADS * HD
B_HEADS = 8
B_W = B_HEADS * HD
EVEN_IN = A_Q + 2 * A_KV + 4 * B_W
C_HEADS = 16
C_W = C_HEADS * HD
NA_KH = 8
NA_KW = 16
N_EXPERTS = 64
TOP_K = 8
N_GROUPS = 8
TOPK_GROUPS = 4
FF = 256
ROUTED_SCALE = 2.5
MOE_BLOCK = 1024
RET_CHUNK = 128
RET_UNROLL = 2
RET_NORM_ROWS = 256
A_WINDOW = 128

NP_TOK = BATCH * SEQ
NS_TOK = DEC_BATCH * DEC_SEQ
N_TOK = NP_TOK + NS_TOK
N_ASSIGN = N_TOK * TOP_K
N_MOE_BLOCKS = (N_ASSIGN + N_EXPERTS * (MOE_BLOCK - 1) + MOE_BLOCK - 1) // MOE_BLOCK
PLAN_LANES = 512
assert N_TOK % MOE_BLOCK == 0 and N_MOE_BLOCKS <= PLAN_LANES

LANES = 128
TM = 512
NA_ROWS = 8
V7X_VMEM_LIMIT = 56 * 1024 * 1024

NT_DIMS = (((1,), (1,)), ((), ()))


def _cparams(n_axes, vmem=V7X_VMEM_LIMIT):
    return pltpu.CompilerParams(dimension_semantics=("arbitrary",) * n_axes, vmem_limit_bytes=vmem)


def _seg_of_block(i, rows):
    row0 = i * rows
    return jnp.where(row0 < NP_TOK, 0, 1 + (row0 - NP_TOK) // DEC_SEQ)


def _mod_spec(chunk, rows=TM, first_block=0):
    return pl.BlockSpec((None, 1, D), lambda i: (_seg_of_block(i + first_block, rows), 0, chunk))


def _pair_specs(width, rows=TM):
    npb = NP_TOK // rows
    nsb = NS_TOK // rows
    return [pl.BlockSpec((rows, width), lambda i: (jnp.minimum(i, npb - 1), 0)),
            pl.BlockSpec((rows, width), lambda i: (jnp.clip(i - npb, 0, nsb - 1), 0))]


def _pick_rows(p_ref, s_ref, rows=TM):
    return jnp.where(pl.program_id(0) < NP_TOK // rows, p_ref[...], s_ref[...])


def _silu(x):
    return x * jax.nn.sigmoid(x)


def _rms(x, g):
    return x * lax.rsqrt(jnp.mean(x * x, axis=-1, keepdims=True) + EPS) * g


def _lane_lo():
    return lax.broadcasted_iota(jnp.int32, (1, LANES), 1) < HD


def _ada_kernel(c_ref, w_ref, b_ref, o_ref):
    a = _silu(c_ref[...])
    o_ref[...] = jnp.dot(a, w_ref[...], preferred_element_type=F32, precision=HIGHEST) + b_ref[...]


def _ada(cc, w, b, layer):
    tn = 1536
    out = pl.pallas_call(
        _ada_kernel,
        grid=(6 * D // tn,),
        in_specs=[pl.BlockSpec((8, D), lambda j: (0, 0)),
                  pl.BlockSpec((None, D, tn), lambda j: (layer, 0, j)),
                  pl.BlockSpec((None, 1, tn), lambda j: (layer, 0, j))],
        out_specs=pl.BlockSpec((8, tn), lambda j: (0, j)),
        out_shape=jax.ShapeDtypeStruct((8, 6 * D), F32),
        compiler_params=_cparams(1),
        name="ada",
    )(cc, w, b.reshape(b.shape[0], 1, 6 * D))
    return out.reshape(8, 1, 6 * D)


def _inproj_kernel(xp_ref, xs_ref, g_ref, shift_ref, scale_ref, w_ref, cos_ref, sin_ref, o_ref, *, rope_cols):
    h = _rms(_pick_rows(xp_ref, xs_ref), g_ref[...]) * (1.0 + scale_ref[...]) + shift_ref[...]
    o = jnp.dot(h.astype(BF16), w_ref[...], preferred_element_type=F32)
    if rope_cols:
        cos = cos_ref[...]
        sin = sin_ref[...]
        lane = lax.broadcasted_iota(jnp.int32, (1, LANES), 1)
        first = (lane % 32) < 16
        for c in range(rope_cols // LANES):
            oc = o[:, c * LANES:(c + 1) * LANES]
            partner = jnp.where(first, pltpu.roll(oc, LANES - 16, 1), pltpu.roll(oc, 16, 1))
            o_ref[:, c * LANES:(c + 1) * LANES] = oc * cos + partner * sin
        o_ref[:, rope_cols:] = o[:, rope_cols:]
    else:
        o_ref[...] = o


def _rope_tables():
    half = HD // 2
    inv = ROPE_BASE ** (-jnp.arange(0, half, 2, dtype=F32) / half)
    t = jnp.arange(DEC_SEQ)
    ang_r = (t // GRID_W).astype(F32)[:, None] * inv[None]
    ang_c = (t % GRID_W).astype(F32)[:, None] * inv[None]

    def head(fn_r, fn_c, sign):
        return jnp.concatenate([sign[0] * fn_r, sign[1] * fn_r, sign[0] * fn_c, sign[1] * fn_c], axis=-1)

    cos = head(jnp.cos(ang_r), jnp.cos(ang_c), (1.0, 1.0))
    sin = head(jnp.sin(ang_r), jnp.sin(ang_c), (-1.0, 1.0))
    cos = jnp.concatenate([jnp.ones((TM, HD), F32), cos], axis=0)
    sin = jnp.concatenate([jnp.zeros((TM, HD), F32), sin], axis=0)
    return jnp.tile(cos, (1, 2)), jnp.tile(sin, (1, 2))


def _inproj(x, g, mod, w_bf16, rope, rope_cols):
    n_out = w_bf16.shape[1]
    npb = NP_TOK // TM
    spb = DEC_SEQ // TM

    def rope_map(i):
        return (jnp.where(i < npb, 0, 1 + (i - npb) % spb), 0)

    return pl.pallas_call(
        functools.partial(_inproj_kernel, rope_cols=rope_cols),
        grid=(N_TOK // TM,),
        in_specs=_pair_specs(D) + [
                  pl.BlockSpec((1, D), lambda i: (0, 0)),
                  _mod_spec(0), _mod_spec(1),
                  pl.BlockSpec((D, n_out), lambda i: (0, 0)),
                  pl.BlockSpec((TM, LANES), rope_map),
                  pl.BlockSpec((TM, LANES), rope_map)],
        out_specs=pl.BlockSpec((TM, n_out), lambda i: (i, 0)),
        out_shape=jax.ShapeDtypeStruct((N_TOK, n_out), F32),
        compiler_params=_cparams(1),
        name="inproj",
    )(x[0], x[1], g.reshape(1, D), mod, mod, w_bf16, rope[0], rope[1])


def _softmax_av(s_list, v_list, sink=None):
    mx = s_list[0].max(axis=-1, keepdims=True)
    for s in s_list[1:]:
        mx = jnp.maximum(mx, s.max(axis=-1, keepdims=True))
    if sink is not None:
        mx = jnp.maximum(mx, sink)
    den = jnp.exp(sink - mx) if sink is not None else 0.0
    acc = None
    for s, v in zip(s_list, v_list):
        p = jnp.exp(s - mx)
        den = den + p.sum(axis=-1, keepdims=True)
        pv = jnp.dot(p.astype(BF16), v, preferred_element_type=F32)
        acc = pv if acc is None else acc + pv
    return acc / den


def _dup_half(x, j, lo):
    xr = pltpu.roll(x, HD, 1)
    return jnp.where(lo, x, xr) if j == 0 else jnp.where(lo, xr, x)


def _stack_heads(q_ref, heads, lo, scale):
    parts = []
    for h in heads:
        qp = q_ref[:, (h // 2) * LANES:(h // 2 + 1) * LANES]
        keep = lo if h % 2 == 0 else jnp.logical_not(lo)
        parts.append(jnp.where(keep, qp, 0.0) * scale)
    return jnp.concatenate(parts, axis=0).astype(BF16)


def _sink_column(sink_ref, heads, rows):
    return jnp.concatenate([jnp.full((rows, 1), sink_ref[h], F32) for h in heads], axis=0)


def _ctx_gqa_kernel(sink_ref, q_ref, k_ref, v_ref, o_ref):
    lo = _lane_lo()
    k = k_ref[...]
    v = v_ref[...]
    group = A_HEADS // A_KV_HEADS
    for j in range(A_KV_HEADS):
        heads = list(range(group * j, group * (j + 1)))
        kd = _dup_half(k, j, lo).astype(BF16)
        vd = _dup_half(v, j, lo).astype(BF16)
        q = _stack_heads(q_ref, heads, lo, HD ** -0.5)
        s = lax.dot_general(q, kd, NT_DIMS, preferred_element_type=F32)
        o = _softmax_av([s], [vd], _sink_column(sink_ref, heads, SEQ))
        for t in range(group // 2):
            pair = heads[2 * t] // 2
            o_ref[:, pair * LANES:(pair + 1) * LANES] = jnp.where(
                lo, o[(2 * t) * SEQ:(2 * t + 1) * SEQ], o[(2 * t + 1) * SEQ:(2 * t + 2) * SEQ])


def _ctx_gqa(p, sink):
    return pl.pallas_call(
        _ctx_gqa_kernel,
        grid_spec=pltpu.PrefetchScalarGridSpec(
            num_scalar_prefetch=1,
            grid=(BATCH,),
            in_specs=[pl.BlockSpec((SEQ, A_Q), lambda b, s: (b, 0)),
                      pl.BlockSpec((SEQ, A_KV), lambda b, s: (b, A_Q // A_KV)),
                      pl.BlockSpec((SEQ, A_KV), lambda b, s: (b, A_Q // A_KV + 1))],
            out_specs=pl.BlockSpec((SEQ, A_Q), lambda b, s: (b, 0))),
        out_shape=jax.ShapeDtypeStruct((NP_TOK, A_Q), F32),
        compiler_params=_cparams(1),
        name="ctx_gqa",
    )(sink, p, p, p)


def _win_kernel(sink_ref, q_ref, kp_ref, kc_ref, kn_ref, vp_ref, vc_ref, vn_ref, ck_ref, cv_ref, o_ref):
    i = pl.program_id(1)
    lo = _lane_lo()
    k = jnp.concatenate([kp_ref[...], kc_ref[...], kn_ref[...]], axis=0)
    v = jnp.concatenate([vp_ref[...], vc_ref[...], vn_ref[...]], axis=0)
    ck = ck_ref[...]
    cv = cv_ref[...]
    group = A_HEADS // A_KV_HEADS
    n_keys = WIN_Q + 2 * A_WINDOW
    qpos = i * WIN_Q + lax.broadcasted_iota(jnp.int32, (WIN_Q, n_keys), 0)
    kpos = i * WIN_Q - A_WINDOW + lax.broadcasted_iota(jnp.int32, (WIN_Q, n_keys), 1)
    valid = (jnp.abs(kpos - qpos) <= A_WINDOW) & (kpos >= 0) & (kpos < DEC_SEQ)
    valid = jnp.concatenate([valid] * group, axis=0)
    s_loc, s_ctx, values = [], [], []
    for j in range(A_KV_HEADS):
        heads = list(range(group * j, group * (j + 1)))
        kd = _dup_half(k, j, lo).astype(BF16)
        ckd = _dup_half(ck, j, lo).astype(BF16)
        values.append((_dup_half(v, j, lo).astype(BF16), _dup_half(cv, j, lo).astype(BF16)))
        q = _stack_heads(q_ref, heads, lo, HD ** -0.5)
        s_loc.append(jnp.where(valid, lax.dot_general(q, kd, NT_DIMS, preferred_element_type=F32), NEG))
        s_ctx.append(lax.dot_general(q, ckd, NT_DIMS, preferred_element_type=F32))
    s_loc = jnp.concatenate(s_loc, axis=0)
    s_ctx = jnp.concatenate(s_ctx, axis=0)
    sink = _sink_column(sink_ref, list(range(A_HEADS)), WIN_Q)
    mx = jnp.maximum(jnp.maximum(s_loc.max(axis=-1, keepdims=True), s_ctx.max(axis=-1, keepdims=True)), sink)
    p_loc = jnp.exp(s_loc - mx)
    p_ctx = jnp.exp(s_ctx - mx)
    den = p_loc.sum(axis=-1, keepdims=True) + p_ctx.sum(axis=-1, keepdims=True) + jnp.exp(sink - mx)
    p_loc = p_loc.astype(BF16)
    p_ctx = p_ctx.astype(BF16)
    rows_per_group = group * WIN_Q
    for j, (vd, cvd) in enumerate(values):
        rows = slice(j * rows_per_group, (j + 1) * rows_per_group)
        o = (jnp.dot(p_loc[rows], vd, preferred_element_type=F32)
             + jnp.dot(p_ctx[rows], cvd, preferred_element_type=F32)) / den[rows]
        for t in range(group // 2):
            pair = (group * j) // 2 + t
            o_ref[:, pair * LANES:(pair + 1) * LANES] = jnp.where(
                lo, o[(2 * t) * WIN_Q:(2 * t + 1) * WIN_Q], o[(2 * t + 1) * WIN_Q:(2 * t + 2) * WIN_Q])


WIN_Q = 256


def _win_attention(p, cache_k, cache_v, sink):
    nblk = DEC_SEQ // WIN_Q
    side = WIN_Q // A_WINDOW
    nside = DEC_SEQ // A_WINDOW
    base = NP_TOK // WIN_Q
    side_base = NP_TOK // A_WINDOW
    kcol = A_Q // A_KV

    def main_spec(col):
        return pl.BlockSpec((WIN_Q, A_KV), lambda b, i, s: (base + b * nblk + i, col))

    def side_spec(col, off):
        return pl.BlockSpec((A_WINDOW, A_KV),
                            lambda b, i, s: (side_base + b * nside + jnp.clip(side * i + off, 0, nside - 1), col))

    ctx_spec = pl.BlockSpec((None, PAST, A_KV), lambda b, i, s: (b, 0, 0))
    return pl.pallas_call(
        _win_kernel,
        grid_spec=pltpu.PrefetchScalarGridSpec(
            num_scalar_prefetch=1,
            grid=(DEC_BATCH, nblk),
            in_specs=[pl.BlockSpec((WIN_Q, A_Q), lambda b, i, s: (base + b * nblk + i, 0)),
                      side_spec(kcol, -1), main_spec(kcol), side_spec(kcol, side),
                      side_spec(kcol + 1, -1), main_spec(kcol + 1), side_spec(kcol + 1, side),
                      ctx_spec, ctx_spec],
            out_specs=pl.BlockSpec((WIN_Q, A_Q), lambda b, i, s: (b * nblk + i, 0))),
        out_shape=jax.ShapeDtypeStruct((NS_TOK, A_Q), F32),
        compiler_params=_cparams(2),
        name="win_attn",
    )(sink, p, p, p, p, p, p, p, cache_k.reshape(DEC_BATCH, PAST, A_KV), cache_v.reshape(DEC_BATCH, PAST, A_KV))


def _ret_kernel(df_ref, db_ref, q_ref, k_ref, v_ref, g_ref, gn_ref, s0f_ref, s0b_ref,
                o_ref, sf_ref, sb_ref, of_scr, ob_scr, *, length):
    c_len = RET_CHUNK
    n = length // c_len
    lo = _lane_lo()
    hi = jnp.logical_not(lo)
    row = lax.broadcasted_iota(jnp.int32, (c_len, c_len), 0)
    col = lax.broadcasted_iota(jnp.int32, (c_len, c_len), 1)
    rowp = lax.broadcasted_iota(jnp.int32, (LANES, LANES), 0)
    colp = lax.broadcasted_iota(jnp.int32, (LANES, LANES), 1)
    blockdiag = (rowp < HD) == (colp < HD)
    idx = lax.broadcasted_iota(jnp.int32, (c_len, 1), 0).astype(F32)

    def direction(dec_ref, forward):
        lg = -jnp.exp(dec_ref[...])
        diff = (row - col) if forward else (col - row)
        keep = (diff >= 0) if forward else (diff > 0)
        dist = jnp.maximum(diff, 0).astype(F32)
        dm = jnp.concatenate([jnp.where(keep, jnp.exp(dist * lg[:, off:off + 1]), 0.0) for off in (0, HD)], axis=0)
        if forward:
            xi = jnp.exp((idx + 1.0) * lg)
            zeta = jnp.exp((c_len - 1.0 - idx) * lg)
        else:
            xi = jnp.exp((c_len - idx) * lg)
            zeta = jnp.exp(idx * lg)
        return dm, xi, zeta, jnp.exp(c_len * lg)

    def chunk(c, state, consts):
        dm, xi, zeta, gch = consts
        rows = pl.ds(pl.multiple_of(c * c_len, c_len), c_len)
        qc = q_ref[rows, :]
        kc = k_ref[rows, :] * HD ** -0.5
        vc = v_ref[rows, :].astype(BF16)
        kb = kc.astype(BF16)
        q2 = jnp.concatenate([jnp.where(lo, qc, 0.0), jnp.where(hi, qc, 0.0)], axis=0).astype(BF16)
        inner = lax.dot_general(q2, kb, NT_DIMS, preferred_element_type=F32) * dm
        kz_t = (kc * zeta).T
        res = jnp.dot(jnp.concatenate([inner, kz_t], axis=0).astype(BF16), vc, preferred_element_type=F32)
        cross = jnp.dot(qc.astype(BF16), state.astype(BF16), preferred_element_type=F32) * xi
        o = jnp.where(lo, res[:c_len], res[c_len:2 * c_len]) + cross
        state = gch * state + jnp.where(blockdiag, res[2 * c_len:], 0.0)
        return rows, o, state

    cf = direction(df_ref, True)
    cb = direction(db_ref, False)

    def scan_body(t, states):
        rows_f, o_f, state_f = chunk(t, states[0], cf)
        of_scr[rows_f, :] = o_f
        rows_b, o_b, state_b = chunk(n - 1 - t, states[1], cb)
        ob_scr[rows_b, :] = o_b
        return state_f, state_b

    state_f, state_b = lax.fori_loop(0, n, scan_body, (s0f_ref[...], s0b_ref[...]), unroll=min(n, RET_UNROLL))
    sf_ref[...] = state_f
    sb_ref[...] = state_b

    gn = gn_ref[...]
    n_norm = length // RET_NORM_ROWS

    def per_head(x):
        a = jnp.where(lo, x, 0.0).sum(axis=-1, keepdims=True)
        b = jnp.where(hi, x, 0.0).sum(axis=-1, keepdims=True)
        return jnp.where(lo, a, b) * (1.0 / HD)

    def norm_body(t, carry):
        rows = pl.ds(pl.multiple_of(t * RET_NORM_ROWS, RET_NORM_ROWS), RET_NORM_ROWS)
        o = of_scr[rows, :] + ob_scr[rows, :]
        d = o - per_head(o)
        y = d * lax.rsqrt(per_head(d * d) + EPS) * gn
        o_ref[rows, :] = _silu(g_ref[rows, :]) * y
        return carry

    lax.fori_loop(0, n_norm, norm_body, 0)


def _pair_lanes(v):
    return jnp.repeat(v.astype(F32), HD).reshape(B_HEADS // 2, 1, LANES)


def _blockdiag_states(s):
    b = s.shape[0]
    s = s.astype(F32).reshape(b, B_HEADS // 2, 2, HD, HD)
    z = jnp.zeros_like(s[:, :, 0])
    top = jnp.concatenate([s[:, :, 0], z], axis=-1)
    bot = jnp.concatenate([z, s[:, :, 1]], axis=-1)
    return jnp.concatenate([top, bot], axis=-2)


def _diag_states(sp):
    b = sp.shape[0]
    s = jnp.stack([sp[:, :, :HD, :HD], sp[:, :, HD:, HD:]], axis=2)
    return s.reshape(b, B_HEADS, HD, HD)


def _retention(p, row_base, batch, length, dec_f, dec_b, gn_g, s0f, s0b):
    npairs = B_HEADS // 2
    blk0 = row_base // length
    qcol = (A_Q + 2 * A_KV) // LANES

    def col_spec(off):
        return pl.BlockSpec((length, LANES), lambda b, h: (blk0 + b, qcol + off * npairs + h))

    lane_spec = pl.BlockSpec((None, 1, LANES), lambda b, h: (h, 0, 0))
    state_spec = pl.BlockSpec((None, None, LANES, LANES), lambda b, h: (b, h, 0, 0))
    state_shape = jax.ShapeDtypeStruct((batch, npairs, LANES, LANES), F32)
    return pl.pallas_call(
        functools.partial(_ret_kernel, length=length),
        grid=(batch, npairs),
        in_specs=[lane_spec, lane_spec, col_spec(0), col_spec(1), col_spec(2), col_spec(3), lane_spec,
                  state_spec, state_spec],
        out_specs=[pl.BlockSpec((length, LANES), lambda b, h: (b, h)), state_spec, state_spec],
        out_shape=[jax.ShapeDtypeStruct((batch * length, B_W), F32), state_shape, state_shape],
        scratch_shapes=[pltpu.VMEM((length, LANES), F32), pltpu.VMEM((length, LANES), F32)],
        compiler_params=_cparams(2),
        name="retention",
    )(_pair_lanes(dec_f), _pair_lanes(dec_b), p, p, p, p, gn_g.reshape(npairs, 1, LANES), s0f, s0b)


def _ctx_mha_kernel(q_ref, k_ref, v_ref, o_ref):
    lo = _lane_lo()
    for pair in range(C_HEADS // 2):
        cols = slice(pair * LANES, (pair + 1) * LANES)
        q = _stack_heads(q_ref, [2 * pair, 2 * pair + 1], lo, HD ** -0.5)
        s = lax.dot_general(q, k_ref[:, cols].astype(BF16), NT_DIMS, preferred_element_type=F32)
        o = _softmax_av([s], [v_ref[:, cols].astype(BF16)])
        o_ref[:, cols] = jnp.where(lo, o[:SEQ], o[SEQ:])


def _ctx_mha(p):
    return pl.pallas_call(
        _ctx_mha_kernel,
        grid=(BATCH,),
        in_specs=[pl.BlockSpec((SEQ, C_W), lambda b: (b, 0)),
                  pl.BlockSpec((SEQ, C_W), lambda b: (b, 1)),
                  pl.BlockSpec((SEQ, C_W), lambda b: (b, 2))],
        out_specs=pl.BlockSpec((SEQ, C_W), lambda b: (b, 0)),
        out_shape=jax.ShapeDtypeStruct((NP_TOK, C_W), F32),
        compiler_params=_cparams(1),
        name="ctx_mha",
    )(p, p, p)


NA_WIN_ROWS = 2 * NA_ROWS
NA_WIN = NA_WIN_ROWS * GRID_W
NA_QROWS = NA_ROWS * GRID_W
NA_PAD_ROWS = NA_KH // 2
NA_TABLE = 1536


def _na_kernel(q_ref, kp_ref, km_ref, kn_ref, vp_ref, vm_ref, vn_ref, ck_ref, cv_ref, ue_ref, uo_ref, o_ref):
    r0 = pl.program_id(2) * NA_ROWS
    n_rows = DEC_SEQ // GRID_W
    lo = _lane_lo()
    k = jnp.concatenate([kp_ref[...], km_ref[...], kn_ref[...]], axis=0).astype(BF16)
    v = jnp.concatenate([vp_ref[...], vm_ref[...], vn_ref[...]], axis=0).astype(BF16)
    ck = ck_ref[...].astype(BF16)
    cv = cv_ref[...].astype(BF16)
    q = q_ref[...] * HD ** -0.5
    klane = lax.broadcasted_iota(jnp.int32, (1, NA_WIN), 1)
    outs = []
    for half, keep in enumerate((lo, jnp.logical_not(lo))):
        qh = jnp.where(keep, q, 0.0).astype(BF16)
        s = lax.dot_general(qh, k, NT_DIMS, preferred_element_type=F32)
        s_ctx = lax.dot_general(qh, ck, NT_DIMS, preferred_element_type=F32)
        p_loc, p_ctx, den = [], [], []
        for rq in range(NA_ROWS):
            rows = slice(rq * GRID_W, (rq + 1) * GRID_W)
            start = NA_KH - 1 - rq
            if start % 2 == 0:
                u = ue_ref[half, :, start * GRID_W:start * GRID_W + NA_WIN]
            else:
                u = uo_ref[half, :, (start - 1) * GRID_W:(start - 1) * GRID_W + NA_WIN]
            r = r0 + rq
            first = jnp.clip(r - NA_KH // 2, 0, n_rows - NA_KH)
            lane0 = (first - r0 + NA_PAD_ROWS) * GRID_W
            in_rows = (klane >= lane0) & (klane < lane0 + NA_KH * GRID_W)
            sl = jnp.where(in_rows, s[rows] + u, NEG)
            sc = s_ctx[rows]
            mx = jnp.maximum(sl.max(axis=-1, keepdims=True), sc.max(axis=-1, keepdims=True))
            el = jnp.exp(sl - mx)
            ec = jnp.exp(sc - mx)
            den.append(el.sum(axis=-1, keepdims=True) + ec.sum(axis=-1, keepdims=True))
            p_loc.append(el.astype(BF16))
            p_ctx.append(ec.astype(BF16))
        acc = (jnp.dot(jnp.concatenate(p_loc, axis=0), v, preferred_element_type=F32)
               + jnp.dot(jnp.concatenate(p_ctx, axis=0), cv, preferred_element_type=F32))
        outs.append(acc / jnp.concatenate(den, axis=0))
    o_ref[...] = jnp.where(lo, outs[0], outs[1])


def _na_bias_tables(rpb):
    cq = jnp.arange(GRID_W)
    ck = jnp.arange(GRID_W)
    dc = jnp.clip(ck[None] - cq[:, None], -(NA_KW - 1), NA_KW - 1) + NA_KW - 1
    cs = jnp.clip(cq - NA_KW // 2, 0, GRID_W - NA_KW)
    col_ok = (ck[None] >= cs[:, None]) & (ck[None] < cs[:, None] + NA_KW)
    t = rpb.astype(F32)[:, :, dc]
    t = jnp.where(col_ok[None, None], t, NEG).transpose(0, 2, 1, 3)
    n_dr = 2 * NA_KH - 1
    blocks = NA_TABLE // GRID_W
    t = jnp.pad(t, ((0, 0), (0, 0), (NA_PAD_ROWS, blocks - n_dr - NA_PAD_ROWS), (0, 0)), constant_values=NEG)
    ue = t.reshape(C_HEADS, GRID_W, NA_TABLE)
    uo = jnp.concatenate([ue[..., GRID_W:], jnp.full((C_HEADS, GRID_W, GRID_W), NEG, F32)], axis=-1)
    return ue, uo


def _na_attention(p, cache_k, cache_v, rpb):
    npairs = C_HEADS // 2
    nrb = DEC_SEQ // NA_QROWS
    half = NA_QROWS // 2
    qbase = NP_TOK // NA_QROWS
    hbase = NP_TOK // half
    kcol = C_W // LANES
    ue, uo = _na_bias_tables(rpb)

    def main_spec(col0):
        return pl.BlockSpec((NA_QROWS, LANES), lambda b, h, r: (qbase + b * nrb + r, col0 + h))

    def side_spec(col0, off):
        return pl.BlockSpec((half, LANES),
                            lambda b, h, r: (hbase + b * 2 * nrb + jnp.clip(2 * r + off, 0, 2 * nrb - 1), col0 + h))

    ctx_spec = pl.BlockSpec((None, PAST, LANES), lambda b, h, r: (b, 0, h))
    tab_spec = pl.BlockSpec((2, GRID_W, NA_TABLE), lambda b, h, r: (h, 0, 0))
    return pl.pallas_call(
        _na_kernel,
        grid=(DEC_BATCH, npairs, nrb),
        in_specs=[main_spec(0),
                  side_spec(kcol, -1), main_spec(kcol), side_spec(kcol, 2),
                  side_spec(2 * kcol, -1), main_spec(2 * kcol), side_spec(2 * kcol, 2),
                  ctx_spec, ctx_spec, tab_spec, tab_spec],
        out_specs=pl.BlockSpec((NA_QROWS, LANES), lambda b, h, r: (b * nrb + r, h)),
        out_shape=jax.ShapeDtypeStruct((NS_TOK, C_W), F32),
        compiler_params=_cparams(3),
        name="na_attn",
    )(p, p, p, p, p, p, p, cache_k.reshape(DEC_BATCH, PAST, C_W), cache_v.reshape(DEC_BATCH, PAST, C_W), ue, uo)


def _route(biased, scores):
    t = biased.shape[1]
    per_group = N_EXPERTS // N_GROUPS
    i8 = lax.broadcasted_iota(jnp.int32, (per_group, t), 0)
    g_rows = []
    for g in range(N_GROUPS):
        bg = biased[g * per_group:(g + 1) * per_group]
        m1 = bg.max(axis=0, keepdims=True)
        first = jnp.where(bg == m1, i8, per_group).min(axis=0, keepdims=True)
        m2 = jnp.where(i8 == first, -jnp.inf, bg).max(axis=0, keepdims=True)
        g_rows.append(m1 + m2)
    g_top = jnp.concatenate(g_rows, axis=0)
    gi = lax.broadcasted_iota(jnp.int32, g_top.shape, 0)
    g_sel = jnp.zeros(g_top.shape, jnp.int32)
    cur = g_top
    for _ in range(TOPK_GROUPS):
        m = cur.max(axis=0, keepdims=True)
        hit = gi == jnp.where(cur == m, gi, N_GROUPS).min(axis=0, keepdims=True)
        g_sel = jnp.where(hit, 1, g_sel)
        cur = jnp.where(hit, -jnp.inf, cur)
    e_sel = jnp.concatenate([jnp.broadcast_to(g_sel[g:g + 1], (per_group, t)) for g in range(N_GROUPS)], axis=0)
    cur = jnp.where(e_sel > 0, biased, NEG)
    ei = lax.broadcasted_iota(jnp.int32, cur.shape, 0)
    ids, gates, hits = [], [], []
    for _ in range(TOP_K):
        m = cur.max(axis=0, keepdims=True)
        f = jnp.where(cur == m, ei, N_EXPERTS).min(axis=0, keepdims=True)
        hit = ei == f
        ids.append(f)
        hits.append(hit)
        gates.append(jnp.where(hit, scores, 0.0).sum(axis=0, keepdims=True))
        cur = jnp.where(hit, -jnp.inf, cur)
    gate = jnp.concatenate(gates, axis=0)
    gate = gate / gate.sum(axis=0, keepdims=True) * ROUTED_SCALE
    return jnp.concatenate(ids, axis=0), gate, hits


def _pack_bf16_pairs(h):
    bits = lax.bitcast_convert_type(h.astype(BF16).astype(F32), jnp.uint32)
    return bits[:, :D // 2] | (bits[:, D // 2:] >> 16)


def _unpack_bf16_pairs(xp):
    hi = lax.bitcast_convert_type(xp & jnp.uint32(0xFFFF0000), F32).astype(BF16)
    lo = lax.bitcast_convert_type(xp << 16, F32).astype(BF16)
    return hi, lo


def _dot_halves(hi, lo, w_ref):
    return (jnp.dot(hi, w_ref[:D // 2, :], preferred_element_type=F32)
            + jnp.dot(lo, w_ref[D // 2:, :], preferred_element_type=F32))


def _outproj_kernel(*refs, n_parts):
    xp_ref, xs_ref = refs[:2]
    part_refs = refs[2:2 + 3 * n_parts]
    gate_ref, shift_ref, scale_ref, g2_ref, rw_ref, rb_ref = refs[2 + 3 * n_parts:8 + 3 * n_parts]
    xo_ref, h_ref, dest_ref, wgt_ref, plan_ref, cnt_ref = refs[8 + 3 * n_parts:]
    step = pl.program_id(0)

    @pl.when(step == 0)
    def _():
        cnt_ref[...] = jnp.zeros_like(cnt_ref)
        plan_ref[...] = jnp.zeros_like(plan_ref)

    y = None
    for t in range(n_parts):
        ap_ref, as_ref, w_ref = part_refs[3 * t:3 * t + 3]
        d = jnp.dot(_pick_rows(ap_ref, as_ref).astype(BF16), w_ref[...], preferred_element_type=F32)
        y = d if y is None else y + d
    x = _pick_rows(xp_ref, xs_ref) + gate_ref[...] * y
    xo_ref[...] = x
    h = _rms(x, g2_ref[...]) * (1.0 + scale_ref[...]) + shift_ref[...]
    h_ref[...] = _pack_bf16_pairs(h)
    logits = lax.dot_general(rw_ref[...], h, NT_DIMS, preferred_element_type=F32, precision=HIGHEST)
    scores = jax.nn.sigmoid(logits)
    _, gate, hits = _route(scores + rb_ref[...], scores)
    wgt_ref[...] = gate
    chosen = hits[0]
    for hit in hits[1:]:
        chosen = chosen | hit
    m = jnp.where(chosen, 1.0, 0.0)
    before = (lax.broadcasted_iota(jnp.int32, (TM, TM), 0) < lax.broadcasted_iota(jnp.int32, (TM, TM), 1))
    prefix = jnp.dot(m.astype(BF16), jnp.where(before, 1.0, 0.0).astype(BF16), preferred_element_type=F32)
    e_base = (lax.broadcasted_iota(jnp.int32, (N_EXPERTS, 1), 0) * N_TOK).astype(F32)
    row_all = prefix + (cnt_ref[...] + e_base)
    dest_ref[...] = jnp.concatenate(
        [jnp.where(hit, row_all, 0.0).sum(axis=0, keepdims=True) for hit in hits], axis=0).astype(jnp.int32)
    cnt_ref[...] += m.sum(axis=1, keepdims=True)

    @pl.when(step == pl.num_programs(0) - 1)
    def _():
        _block_plan(cnt_ref[...], plan_ref)


def _block_plan(counts, plan_ref):
    cap_blocks = N_TOK // MOE_BLOCK
    nblk = ((counts.astype(jnp.int32) + (MOE_BLOCK - 1)) // MOE_BLOCK).astype(F32)
    lower = (lax.broadcasted_iota(jnp.int32, (N_EXPERTS, N_EXPERTS), 0)
             >= lax.broadcasted_iota(jnp.int32, (N_EXPERTS, N_EXPERTS), 1))
    cum = jnp.dot(jnp.where(lower, 1.0, 0.0).astype(BF16), jnp.broadcast_to(nblk, (N_EXPERTS, LANES)).astype(BF16),
                  preferred_element_type=F32)[:, :1]
    n_used = cum[N_EXPERTS - 1:, :]
    slot = jnp.minimum(lax.broadcasted_iota(jnp.int32, (1, PLAN_LANES), 1).astype(F32), n_used - 1.0)
    done = cum <= slot
    expert = jnp.where(done, 1.0, 0.0).sum(axis=0, keepdims=True)
    blocks_before = jnp.where(done, nblk, 0.0).sum(axis=0, keepdims=True)
    plan_ref[0:1, :] = (expert * cap_blocks + (slot - blocks_before)).astype(jnp.int32)
    plan_ref[1:2, :] = expert.astype(jnp.int32)
    plan_ref[2:3, :] = jnp.broadcast_to(n_used, (1, PLAN_LANES)).astype(jnp.int32)


def _outproj(x, parts, mod, g2, router_w, router_b):
    in_specs = _pair_specs(D)
    args = [x[0], x[1]]
    for ap, a_s, w in parts:
        width = ap.shape[1]
        in_specs += _pair_specs(width) + [pl.BlockSpec((width, D), lambda i: (0, 0))]
        args += [ap, a_s, w]
    in_specs += [_mod_spec(2), _mod_spec(3), _mod_spec(4),
                 pl.BlockSpec((1, D), lambda i: (0, 0)),
                 pl.BlockSpec((N_EXPERTS, D), lambda i: (0, 0)),
                 pl.BlockSpec((N_EXPERTS, 1), lambda i: (0, 0))]
    args += [mod, mod, mod, g2.reshape(1, D), router_w.T, router_b.reshape(N_EXPERTS, 1)]
    return pl.pallas_call(
        functools.partial(_outproj_kernel, n_parts=len(parts)),
        grid=(N_TOK // TM,),
        in_specs=in_specs,
        out_specs=[pl.BlockSpec((TM, D), lambda i: (i, 0)),
                   pl.BlockSpec((TM, D // 2), lambda i: (i, 0)),
                   pl.BlockSpec((TOP_K, TM), lambda i: (0, i)),
                   pl.BlockSpec((TOP_K, TM), lambda i: (0, i)),
                   pl.BlockSpec((8, PLAN_LANES), lambda i: (0, 0))],
        out_shape=[jax.ShapeDtypeStruct((N_TOK, D), F32),
                   jax.ShapeDtypeStruct((N_TOK, D // 2), jnp.uint32),
                   jax.ShapeDtypeStruct((TOP_K, N_TOK), jnp.int32),
                   jax.ShapeDtypeStruct((TOP_K, N_TOK), F32),
                   jax.ShapeDtypeStruct((8, PLAN_LANES), jnp.int32)],
        scratch_shapes=[pltpu.VMEM((N_EXPERTS, 1), F32)],
        compiler_params=_cparams(1),
        name="outproj_router",
    )(*args)


def _experts_kernel(br_ref, be_ref, nu_ref, x_ref, w1_ref, w3_ref, w2_ref, o_ref, w1b, w3b, w2b):
    i = pl.program_id(0)
    e = be_ref[i]
    prev = be_ref[jnp.maximum(i - 1, 0)]

    @pl.when((i == 0) | (e != prev))
    def _():
        w1b[...] = w1_ref[...].astype(BF16)
        w3b[...] = w3_ref[...].astype(BF16)
        w2b[...] = w2_ref[...].astype(BF16)

    @pl.when(i < nu_ref[0])
    def _():
        hi, lo = _unpack_bf16_pairs(x_ref[...])
        a = _dot_halves(hi, lo, w1b)
        b = _dot_halves(hi, lo, w3b)
        h = (_silu(a) * b).astype(BF16)
        o_ref[...] = _pack_bf16_pairs(jnp.dot(h, w2b[...], preferred_element_type=F32))


def _experts(plan, x_rows, w1, w3, w2, layer):
    return pl.pallas_call(
        _experts_kernel,
        grid_spec=pltpu.PrefetchScalarGridSpec(
            num_scalar_prefetch=3,
            grid=(N_MOE_BLOCKS,),
            in_specs=[pl.BlockSpec((MOE_BLOCK, D // 2), lambda i, br, be, nu: (br[i], 0)),
                      pl.BlockSpec((None, None, D, FF), lambda i, br, be, nu: (layer, be[i], 0, 0)),
                      pl.BlockSpec((None, None, D, FF), lambda i, br, be, nu: (layer, be[i], 0, 0)),
                      pl.BlockSpec((None, None, FF, D), lambda i, br, be, nu: (layer, be[i], 0, 0))],
            out_specs=pl.BlockSpec((MOE_BLOCK, D // 2), lambda i, br, be, nu: (br[i], 0)),
            scratch_shapes=[pltpu.VMEM((D, FF), BF16), pltpu.VMEM((D, FF), BF16), pltpu.VMEM((FF, D), BF16)]),
        out_shape=jax.ShapeDtypeStruct(x_rows.shape, jnp.uint32),
        compiler_params=_cparams(1),
        name="experts",
    )(plan[0], plan[1], plan[2, :1], x_rows, w1, w3, w2)


SC_CORES = 2
SC_SUBCORES = 16
SC_WORKERS = SC_CORES * SC_SUBCORES
SC_CHUNK_BYTES = 64 * 1024
SC_SLOTS = 4


def _sc_scatter(rows, dest, n_out):
    n_rows, width = rows.shape
    picks = dest.shape[0]
    chunk = SC_CHUNK_BYTES // (4 * width)
    per_worker = n_rows // SC_WORKERS
    n_chunks = per_worker // chunk
    assert per_worker * SC_WORKERS == n_rows and n_chunks * chunk == per_worker and n_chunks % 2 == 0
    mesh = plsc.VectorSubcoreMesh(core_axis_name="c", subcore_axis_name="s")

    @functools.partial(
        pl.kernel, mesh=mesh,
        out_type=jax.ShapeDtypeStruct((n_out, width), rows.dtype),
        scratch_types=[pltpu.VMEM((picks, n_chunks, chunk), jnp.int32),
                       pltpu.VMEM((2, chunk, width), rows.dtype),
                       pltpu.SemaphoreType.DMA((2,)),
                       pltpu.SemaphoreType.DMA((2,))])
    def scatter(r_hbm, d_hbm, o_hbm, idx_v, rows_v, lsem, ssem):
        worker = lax.axis_index("s") * SC_CORES + lax.axis_index("c")
        base = worker * per_worker
        for k in range(picks):
            pltpu.sync_copy(d_hbm.at[k, worker], idx_v.at[k])

        def load_copy(c, b):
            src = pl.ds(pl.multiple_of(base + c * chunk, chunk), chunk)
            return pltpu.make_async_copy(r_hbm.at[src], rows_v.at[b], lsem.at[b])

        def store_copy(c, b, k):
            return pltpu.make_async_copy(rows_v.at[b], o_hbm.at[idx_v.at[k, c]], ssem.at[b])

        load_copy(0, 0).start()

        @pl.loop(0, n_chunks, step=2)
        def _(c0):
            for b in range(2):
                c = c0 + b
                load_copy(c, b).wait()
                for k in range(picks):
                    store_copy(c, b, k).start()

                @pl.when(c > 0)
                def _():
                    for k in range(picks):
                        store_copy(c - 1, 1 - b, k).wait()

                @pl.when(c + 1 < n_chunks)
                def _():
                    load_copy(c + 1, 1 - b).start()

        for k in range(picks):
            store_copy(n_chunks - 1, 1, k).wait()

    return scatter(rows, dest.reshape(picks, SC_WORKERS, n_chunks, chunk))


def _sc_gather(table, idx):
    n_idx = idx.shape[0]
    width = table.shape[1]
    chunk = SC_CHUNK_BYTES // (4 * width)
    per_worker = n_idx // SC_WORKERS
    n_chunks = per_worker // chunk
    ahead = SC_SLOTS - 1
    assert per_worker * SC_WORKERS == n_idx and n_chunks * chunk == per_worker and n_chunks % SC_SLOTS == 0
    mesh = plsc.VectorSubcoreMesh(core_axis_name="c", subcore_axis_name="s")

    @functools.partial(
        pl.kernel, mesh=mesh,
        out_type=jax.ShapeDtypeStruct((n_idx, width), table.dtype),
        scratch_types=[pltpu.VMEM((per_worker,), jnp.int32),
                       pltpu.VMEM((SC_SLOTS, chunk, width), table.dtype),
                       pltpu.SemaphoreType.DMA((SC_SLOTS,)),
                       pltpu.SemaphoreType.DMA((SC_SLOTS,))])
    def gather(t_hbm, i_hbm, o_hbm, idx_v, rows_v, gsem, wsem):
        worker = lax.axis_index("s") * SC_CORES + lax.axis_index("c")
        base = worker * per_worker
        pltpu.sync_copy(i_hbm.at[pl.ds(pl.multiple_of(base, chunk), per_worker)], idx_v)

        def gather_copy(c, b):
            ids = idx_v.at[pl.ds(pl.multiple_of(c * chunk, chunk), chunk)]
            return pltpu.make_async_copy(t_hbm.at[ids], rows_v.at[b], gsem.at[b])

        def write_copy(c, b):
            rows = pl.ds(pl.multiple_of(base + c * chunk, chunk), chunk)
            return pltpu.make_async_copy(rows_v.at[b], o_hbm.at[rows], wsem.at[b])

        for c in range(ahead):
            gather_copy(c, c).start()

        @pl.loop(0, n_chunks, step=SC_SLOTS)
        def _(c0):
            for b in range(SC_SLOTS):
                c = c0 + b
                refill = (b + ahead) % SC_SLOTS
                gather_copy(c, b).wait()
                write_copy(c, b).start()

                @pl.when(c > 0)
                def _():
                    write_copy(c - 1, refill).wait()

                @pl.when(c + ahead < n_chunks)
                def _():
                    gather_copy(c + ahead, refill).start()

        write_copy(n_chunks - 1, (n_chunks - 1) % SC_SLOTS).wait()

    return gather(table, idx)


TC = 512


def _combine_kernel(x_ref, h_ref, y_ref, wgt_ref, gate_ref, w1_ref, w3_ref, w2_ref, fg_ref, o_ref, *, final):
    hi, lo = _unpack_bf16_pairs(h_ref[...])
    a = _dot_halves(hi, lo, w1_ref)
    b = _dot_halves(hi, lo, w3_ref)
    ffn = jnp.dot((_silu(a) * b).astype(BF16), w2_ref[...], preferred_element_type=F32)
    wgt = wgt_ref[...]
    r_hi = None
    r_lo = None
    for k in range(TOP_K):
        yk = y_ref[k]
        w = wgt[:, k:k + 1]
        t_hi = lax.bitcast_convert_type(yk & jnp.uint32(0xFFFF0000), F32) * w
        t_lo = lax.bitcast_convert_type(yk << 16, F32) * w
        r_hi = t_hi if r_hi is None else r_hi + t_hi
        r_lo = t_lo if r_lo is None else r_lo + t_lo
    x = x_ref[...] + gate_ref[...] * (ffn + jnp.concatenate([r_hi, r_lo], axis=1))
    o_ref[...] = _rms(x, fg_ref[...]) if final else x


def _combine(x, h, y_rows, wgt, mod, sw1, sw3, sw2, final_g, final):
    weights = (sw1.astype(BF16), sw3.astype(BF16), sw2.astype(BF16), final_g.reshape(1, D))

    def rows_from(first_row, n_rows):
        b0 = first_row // TC
        return pl.pallas_call(
            functools.partial(_combine_kernel, final=final),
            grid=(n_rows // TC,),
            in_specs=[pl.BlockSpec((TC, D), lambda i: (i + b0, 0)),
                      pl.BlockSpec((TC, D // 2), lambda i: (i + b0, 0)),
                      pl.BlockSpec((TOP_K, TC, D // 2), lambda i: (0, i + b0, 0)),
                      pl.BlockSpec((TC, TOP_K), lambda i: (i + b0, 0)),
                      _mod_spec(5, TC, b0),
                      pl.BlockSpec((D, FF), lambda i: (0, 0)),
                      pl.BlockSpec((D, FF), lambda i: (0, 0)),
                      pl.BlockSpec((FF, D), lambda i: (0, 0)),
                      pl.BlockSpec((1, D), lambda i: (0, 0))],
            out_specs=pl.BlockSpec((TC, D), lambda i: (i, 0)),
            out_shape=jax.ShapeDtypeStruct((n_rows, D), F32),
            compiler_params=_cparams(1),
            name="combine",
        )(x, h, y_rows, wgt, mod, *weights)

    return rows_from(0, NP_TOK), rows_from(NP_TOK, NS_TOK)


def kernel(x_prompt, x_sample, cache_a_k, cache_a_v, state_ret_fwd, state_ret_bwd, cache_c_k, cache_c_v,
           c, c_ctx, norm1_g, norm2_g, ada_w, ada_b, even_w_in, even_w_out, sink_a, ret_decay_fwd,
           ret_decay_bwd, ret_gn_g, odd_w_in, odd_w_out, na_rpb, router_w, router_b, exp_w1, exp_w3,
           exp_w2, sh_w1, sh_w3, sh_w2, final_g):
    x = (x_prompt.reshape(NP_TOK, D), x_sample.reshape(NS_TOK, D))
    cc = jnp.concatenate([c_ctx[None], c, jnp.zeros((8 - 1 - DEC_BATCH, D), F32)], axis=0)
    rope = _rope_tables()
    outs = {}
    for l in range(2):
        mod = _ada(cc, ada_w, ada_b, l)
        if l == 0:
            p = _inproj(x, norm1_g[l], mod, even_w_in[0].astype(BF16), rope, A_Q + A_KV)
            oa_p = _ctx_gqa(p, sink_a[0])
            oa_s = _win_attention(p, cache_a_k[:, 0], cache_a_v[:, 0], sink_a[0])
            zero = jnp.zeros((BATCH, B_HEADS // 2, LANES, LANES), F32)
            ob_p, sf, sb = _retention(p, 0, BATCH, SEQ, ret_decay_fwd[0], ret_decay_bwd[0], ret_gn_g[0], zero, zero)
            ob_s, _, _ = _retention(p, NP_TOK, DEC_BATCH, DEC_SEQ, ret_decay_fwd[0], ret_decay_bwd[0], ret_gn_g[0],
                                    _blockdiag_states(state_ret_fwd[:, 0]), _blockdiag_states(state_ret_bwd[:, 0]))
            w_out = even_w_out[0].astype(BF16)
            parts = [(oa_p, oa_s, w_out[:A_Q]), (ob_p, ob_s, w_out[A_Q:])]
            outs["a_k"] = p[:NP_TOK, A_Q:A_Q + A_KV].reshape(BATCH, 1, SEQ, A_KV_HEADS, HD)
            outs["a_v"] = p[:NP_TOK, A_Q + A_KV:A_Q + 2 * A_KV].reshape(BATCH, 1, SEQ, A_KV_HEADS, HD)
            outs["r_f"] = _diag_states(sf).reshape(BATCH, 1, B_HEADS, HD, HD)
            outs["r_b"] = _diag_states(sb).reshape(BATCH, 1, B_HEADS, HD, HD)
        else:
            p = _inproj(x, norm1_g[l], mod, odd_w_in[0].astype(BF16), rope, 0)
            o_p = _ctx_mha(p)
            o_s = _na_attention(p, cache_c_k[:, 0], cache_c_v[:, 0], na_rpb[0])
            parts = [(o_p, o_s, odd_w_out[0].astype(BF16))]
            outs["c_k"] = p[:NP_TOK, C_W:2 * C_W].reshape(BATCH, 1, SEQ, C_HEADS, HD)
            outs["c_v"] = p[:NP_TOK, 2 * C_W:3 * C_W].reshape(BATCH, 1, SEQ, C_HEADS, HD)
        x_mid, h, dest, gate_t, plan = _outproj(x, parts, mod, norm2_g[l], router_w[l], router_b[l])
        y = _experts(plan, _sc_scatter(h, dest, N_EXPERTS * N_TOK), exp_w1, exp_w3, exp_w2, l)
        y_rows = _sc_gather(y, dest.reshape(N_ASSIGN)).reshape(TOP_K, N_TOK, D // 2)
        x = _combine(x_mid, h, y_rows, gate_t.T, mod, sh_w1[l], sh_w3[l], sh_w2[l], final_g, final=(l == 1))
    y_prompt = x[0].reshape(BATCH, SEQ, D)
    y_sample = x[1].reshape(DEC_BATCH, DEC_SEQ, D)
    return (y_prompt, y_sample, outs["a_k"], outs["a_v"], outs["r_f"], outs["r_b"], outs["c_k"], outs["c_v"])
```

```python
import functools
import math

import jax
import jax.numpy as jnp
from jax import lax
from jax.experimental import pallas as pl
from jax.experimental.pallas import tpu as pltpu
from jax.experimental.pallas import tpu_sc as plsc

F32 = jnp.float32
BF16 = jnp.bfloat16
HIGHEST = lax.Precision.HIGHEST

D = 1024
BATCH = 32
SEQ = 256
DEC_BATCH = 4
DEC_SEQ = 4096
PAST = 256
GRID_W = 64
HD = 64
EPS = 1e-6
NEG = -1e30
ROPE_BASE = 10000.0
A_HEADS = 8
A_KV_HEADS = 2
A_Q = A_HEADS * HD
A_KV = A_KV_HEADS * HD
B_HEADS = 8
B_W = B_HEADS * HD
EVEN_IN = A_Q + 2 * A_KV + 4 * B_W
C_HEADS = 16
C_W = C_HEADS * HD
NA_KH = 8
NA_KW = 16
N_EXPERTS = 64
TOP_K = 8
N_GROUPS = 8
TOPK_GROUPS = 4
FF = 256
ROUTED_SCALE = 2.5
MOE_BLOCK = 2048
RET_CHUNK = 128
RET_UNROLL = 2
RET_NORM_ROWS = 256
A_WINDOW = 128

NP_TOK = BATCH * SEQ
NS_TOK = DEC_BATCH * DEC_SEQ
N_TOK = NP_TOK + NS_TOK
N_ASSIGN = N_TOK * TOP_K
N_MOE_BLOCKS = (N_ASSIGN + N_EXPERTS * (MOE_BLOCK - 1) + MOE_BLOCK - 1) // MOE_BLOCK
PLAN_LANES = 512
assert N_TOK % MOE_BLOCK == 0 and N_MOE_BLOCKS <= PLAN_LANES

LANES = 128
TM = 512
NA_ROWS = 8
V7X_VMEM_LIMIT = 56 * 1024 * 1024

NT_DIMS = (((1,), (1,)), ((), ()))


def _cparams(n_axes, vmem=V7X_VMEM_LIMIT):
    return pltpu.CompilerParams(dimension_semantics=("arbitrary",) * n_axes, vmem_limit_bytes=vmem)


def _seg_of_block(i, rows):
    row0 = i * rows
    return jnp.where(row0 < NP_TOK, 0, 1 + (row0 - NP_TOK) // DEC_SEQ)


def _mod_spec(chunk, rows=TM, first_block=0):
    return pl.BlockSpec((None, 1, D), lambda i: (_seg_of_block(i + first_block, rows), 0, chunk))


def _pair_specs(width, rows=TM):
    npb = NP_TOK // rows
    nsb = NS_TOK // rows
    return [pl.BlockSpec((rows, width), lambda i: (jnp.minimum(i, npb - 1), 0)),
            pl.BlockSpec((rows, width), lambda i: (jnp.clip(i - npb, 0, nsb - 1), 0))]


def _pick_rows(p_ref, s_ref, rows=TM):
    return jnp.where(pl.program_id(0) < NP_TOK // rows, p_ref[...], s_ref[...])


def _silu(x):
    return x * jax.nn.sigmoid(x)


def _rms(x, g):
    return x * lax.rsqrt(jnp.mean(x * x, axis=-1, keepdims=True) + EPS) * g


def _lane_lo():
    return lax.broadcasted_iota(jnp.int32, (1, LANES), 1) < HD


def _ada_kernel(c_ref, w_ref, b_ref, o_ref):
    a = _silu(c_ref[...])
    o_ref[...] = jnp.dot(a, w_ref[...], preferred_element_type=F32, precision=HIGHEST) + b_ref[...]


def _ada(cc, w, b, layer):
    tn = 1536
    out = pl.pallas_call(
        _ada_kernel,
        grid=(6 * D // tn,),
        in_specs=[pl.BlockSpec((8, D), lambda j: (0, 0)),
                  pl.BlockSpec((None, D, tn), lambda j: (layer, 0, j)),
                  pl.BlockSpec((None, 1, tn), lambda j: (layer, 0, j))],
        out_specs=pl.BlockSpec((8, tn), lambda j: (0, j)),
        out_shape=jax.ShapeDtypeStruct((8, 6 * D), F32),
        compiler_params=_cparams(1),
        name="ada",
    )(cc, w, b.reshape(b.shape[0], 1, 6 * D))
    return out.reshape(8, 1, 6 * D)


def _inproj_kernel(xp_ref, xs_ref, g_ref, shift_ref, scale_ref, w_ref, cos_ref, sin_ref, o_ref, *, rope_cols):
    h = _rms(_pick_rows(xp_ref, xs_ref), g_ref[...]) * (1.0 + scale_ref[...]) + shift_ref[...]
    o = jnp.dot(h.astype(BF16), w_ref[...], preferred_element_type=F32)
    if rope_cols:
        cos = cos_ref[...]
        sin = sin_ref[...]
        lane = lax.broadcasted_iota(jnp.int32, (1, LANES), 1)
        first = (lane % 32) < 16
        for c in range(rope_cols // LANES):
            oc = o[:, c * LANES:(c + 1) * LANES]
            partner = jnp.where(first, pltpu.roll(oc, LANES - 16, 1), pltpu.roll(oc, 16, 1))
            o_ref[:, c * LANES:(c + 1) * LANES] = oc * cos + partner * sin
        o_ref[:, rope_cols:] = o[:, rope_cols:]
    else:
        o_ref[...] = o


def _rope_tables():
    half = HD // 2
    inv = ROPE_BASE ** (-jnp.arange(0, half, 2, dtype=F32) / half)
    t = jnp.arange(DEC_SEQ)
    ang_r = (t // GRID_W).astype(F32)[:, None] * inv[None]
    ang_c = (t % GRID_W).astype(F32)[:, None] * inv[None]

    def head(fn_r, fn_c, sign):
        return jnp.concatenate([sign[0] * fn_r, sign[1] * fn_r, sign[0] * fn_c, sign[1] * fn_c], axis=-1)

    cos = head(jnp.cos(ang_r), jnp.cos(ang_c), (1.0, 1.0))
    sin = head(jnp.sin(ang_r), jnp.sin(ang_c), (-1.0, 1.0))
    cos = jnp.concatenate([jnp.ones((TM, HD), F32), cos], axis=0)
    sin = jnp.concatenate([jnp.zeros((TM, HD), F32), sin], axis=0)
    return jnp.tile(cos, (1, 2)), jnp.tile(sin, (1, 2))


def _inproj(x, g, mod, w_bf16, rope, rope_cols):
    n_out = w_bf16.shape[1]
    npb = NP_TOK // TM
    spb = DEC_SEQ // TM

    def rope_map(i):
        return (jnp.where(i < npb, 0, 1 + (i - npb) % spb), 0)

    return pl.pallas_call(
        functools.partial(_inproj_kernel, rope_cols=rope_cols),
        grid=(N_TOK // TM,),
        in_specs=_pair_specs(D) + [
                  pl.BlockSpec((1, D), lambda i: (0, 0)),
                  _mod_spec(0), _mod_spec(1),
                  pl.BlockSpec((D, n_out), lambda i: (0, 0)),
                  pl.BlockSpec((TM, LANES), rope_map),
                  pl.BlockSpec((TM, LANES), rope_map)],
        out_specs=pl.BlockSpec((TM, n_out), lambda i: (i, 0)),
        out_shape=jax.ShapeDtypeStruct((N_TOK, n_out), F32),
        compiler_params=_cparams(1),
        name="inproj",
    )(x[0], x[1], g.reshape(1, D), mod, mod, w_bf16, rope[0], rope[1])


def _softmax_av(s_list, v_list, sink=None):
    mx = s_list[0].max(axis=-1, keepdims=True)
    for s in s_list[1:]:
        mx = jnp.maximum(mx, s.max(axis=-1, keepdims=True))
    if sink is not None:
        mx = jnp.maximum(mx, sink)
    den = jnp.exp(sink - mx) if sink is not None else 0.0
    acc = None
    for s, v in zip(s_list, v_list):
        p = jnp.exp(s - mx)
        den = den + p.sum(axis=-1, keepdims=True)
        pv = jnp.dot(p.astype(BF16), v, preferred_element_type=F32)
        acc = pv if acc is None else acc + pv
    return acc / den


def _dup_half(x, j, lo):
    xr = pltpu.roll(x, HD, 1)
    return jnp.where(lo, x, xr) if j == 0 else jnp.where(lo, xr, x)


def _stack_heads(q_ref, heads, lo, scale):
    parts = []
    for h in heads:
        qp = q_ref[:, (h // 2) * LANES:(h // 2 + 1) * LANES]
        keep = lo if h % 2 == 0 else jnp.logical_not(lo)
        parts.append(jnp.where(keep, qp, 0.0) * scale)
    return jnp.concatenate(parts, axis=0).astype(BF16)


def _sink_column(sink_ref, heads, rows):
    return jnp.concatenate([jnp.full((rows, 1), sink_ref[h], F32) for h in heads], axis=0)


def _ctx_gqa_kernel(sink_ref, q_ref, k_ref, v_ref, o_ref):
    lo = _lane_lo()
    k = k_ref[...]
    v = v_ref[...]
    group = A_HEADS // A_KV_HEADS
    scores = []
    for j in range(A_KV_HEADS):
        q = _stack_heads(q_ref, list(range(group * j, group * (j + 1))), lo, HD ** -0.5)
        scores.append(lax.dot_general(q, _dup_half(k, j, lo).astype(BF16), NT_DIMS, preferred_element_type=F32))
    s = jnp.concatenate(scores, axis=0)
    sink = _sink_column(sink_ref, list(range(A_HEADS)), SEQ)
    mx = jnp.maximum(s.max(axis=-1, keepdims=True), sink)
    e = jnp.exp(s - mx)
    den = jnp.exp(sink - mx) + e.sum(axis=-1, keepdims=True)
    e = e.astype(BF16)
    rows_per_group = group * SEQ
    for j in range(A_KV_HEADS):
        rows = slice(j * rows_per_group, (j + 1) * rows_per_group)
        o = jnp.dot(e[rows], _dup_half(v, j, lo).astype(BF16), preferred_element_type=F32) / den[rows]
        for t in range(group // 2):
            pair = (group * j) // 2 + t
            o_ref[:, pair * LANES:(pair + 1) * LANES] = jnp.where(
                lo, o[(2 * t) * SEQ:(2 * t + 1) * SEQ], o[(2 * t + 1) * SEQ:(2 * t + 2) * SEQ])


def _ctx_gqa(p, sink):
    return pl.pallas_call(
        _ctx_gqa_kernel,
        grid_spec=pltpu.PrefetchScalarGridSpec(
            num_scalar_prefetch=1,
            grid=(BATCH,),
            in_specs=[pl.BlockSpec((SEQ, A_Q), lambda b, s: (b, 0)),
                      pl.BlockSpec((SEQ, A_KV), lambda b, s: (b, A_Q // A_KV)),
                      pl.BlockSpec((SEQ, A_KV), lambda b, s: (b, A_Q // A_KV + 1))],
            out_specs=pl.BlockSpec((SEQ, A_Q), lambda b, s: (b, 0))),
        out_shape=jax.ShapeDtypeStruct((NP_TOK, A_Q), F32),
        compiler_params=_cparams(1),
        name="ctx_gqa",
    )(sink, p, p, p)


def _win_kernel(sink_ref, q_ref, kp_ref, kc_ref, kn_ref, vp_ref, vc_ref, vn_ref, ck_ref, cv_ref, o_ref):
    i = pl.program_id(1)
    lo = _lane_lo()
    k = jnp.concatenate([kp_ref[...], kc_ref[...], kn_ref[...]], axis=0)
    v = jnp.concatenate([vp_ref[...], vc_ref[...], vn_ref[...]], axis=0)
    ck = ck_ref[...]
    cv = cv_ref[...]
    group = A_HEADS // A_KV_HEADS
    n_keys = WIN_Q + 2 * A_WINDOW
    qpos = i * WIN_Q + lax.broadcasted_iota(jnp.int32, (WIN_Q, n_keys), 0)
    kpos = i * WIN_Q - A_WINDOW + lax.broadcasted_iota(jnp.int32, (WIN_Q, n_keys), 1)
    valid = (jnp.abs(kpos - qpos) <= A_WINDOW) & (kpos >= 0) & (kpos < DEC_SEQ)
    valid = jnp.concatenate([valid] * group, axis=0)
    s_loc, s_ctx, values = [], [], []
    for j in range(A_KV_HEADS):
        heads = list(range(group * j, group * (j + 1)))
        kd = _dup_half(k, j, lo).astype(BF16)
        ckd = _dup_half(ck, j, lo).astype(BF16)
        values.append((_dup_half(v, j, lo).astype(BF16), _dup_half(cv, j, lo).astype(BF16)))
        q = _stack_heads(q_ref, heads, lo, HD ** -0.5)
        s_loc.append(jnp.where(valid, lax.dot_general(q, kd, NT_DIMS, preferred_element_type=F32), NEG))
        s_ctx.append(lax.dot_general(q, ckd, NT_DIMS, preferred_element_type=F32))
    s_loc = jnp.concatenate(s_loc, axis=0)
    s_ctx = jnp.concatenate(s_ctx, axis=0)
    sink = _sink_column(sink_ref, list(range(A_HEADS)), WIN_Q)
    mx = jnp.maximum(jnp.maximum(s_loc.max(axis=-1, keepdims=True), s_ctx.max(axis=-1, keepdims=True)), sink)
    p_loc = jnp.exp(s_loc - mx)
    p_ctx = jnp.exp(s_ctx - mx)
    den = p_loc.sum(axis=-1, keepdims=True) + p_ctx.sum(axis=-1, keepdims=True) + jnp.exp(sink - mx)
    p_loc = p_loc.astype(BF16)
    p_ctx = p_ctx.astype(BF16)
    rows_per_group = group * WIN_Q
    for j, (vd, cvd) in enumerate(values):
        rows = slice(j * rows_per_group, (j + 1) * rows_per_group)
        o = (jnp.dot(p_loc[rows], vd, preferred_element_type=F32)
             + jnp.dot(p_ctx[rows], cvd, preferred_element_type=F32)) / den[rows]
        for t in range(group // 2):
            pair = (group * j) // 2 + t
            o_ref[:, pair * LANES:(pair + 1) * LANES] = jnp.where(
                lo, o[(2 * t) * WIN_Q:(2 * t + 1) * WIN_Q], o[(2 * t + 1) * WIN_Q:(2 * t + 2) * WIN_Q])


WIN_Q = 512


def _win_attention(p, cache_k, cache_v, sink):
    nblk = DEC_SEQ // WIN_Q
    side = WIN_Q // A_WINDOW
    nside = DEC_SEQ // A_WINDOW
    base = NP_TOK // WIN_Q
    side_base = NP_TOK // A_WINDOW
    kcol = A_Q // A_KV

    def main_spec(col):
        return pl.BlockSpec((WIN_Q, A_KV), lambda b, i, s: (base + b * nblk + i, col))

    def side_spec(col, off):
        return pl.BlockSpec((A_WINDOW, A_KV),
                            lambda b, i, s: (side_base + b * nside + jnp.clip(side * i + off, 0, nside - 1), col))

    ctx_spec = pl.BlockSpec((None, PAST, A_KV), lambda b, i, s: (b, 0, 0))
    return pl.pallas_call(
        _win_kernel,
        grid_spec=pltpu.PrefetchScalarGridSpec(
            num_scalar_prefetch=1,
            grid=(DEC_BATCH, nblk),
            in_specs=[pl.BlockSpec((WIN_Q, A_Q), lambda b, i, s: (base + b * nblk + i, 0)),
                      side_spec(kcol, -1), main_spec(kcol), side_spec(kcol, side),
                      side_spec(kcol + 1, -1), main_spec(kcol + 1), side_spec(kcol + 1, side),
                      ctx_spec, ctx_spec],
            out_specs=pl.BlockSpec((WIN_Q, A_Q), lambda b, i, s: (b * nblk + i, 0))),
        out_shape=jax.ShapeDtypeStruct((NS_TOK, A_Q), F32),
        compiler_params=_cparams(2),
        name="win_attn",
    )(sink, p, p, p, p, p, p, p, cache_k.reshape(DEC_BATCH, PAST, A_KV), cache_v.reshape(DEC_BATCH, PAST, A_KV))


def _ret_kernel(df_ref, db_ref, q_ref, k_ref, v_ref, g_ref, gn_ref, s0f_ref, s0b_ref,
                o_ref, sf_ref, sb_ref, of_scr, ob_scr, *, length):
    c_len = RET_CHUNK
    n = length // c_len
    lo = _lane_lo()
    hi = jnp.logical_not(lo)
    row = lax.broadcasted_iota(jnp.int32, (c_len, c_len), 0)
    col = lax.broadcasted_iota(jnp.int32, (c_len, c_len), 1)
    rowp = lax.broadcasted_iota(jnp.int32, (LANES, LANES), 0)
    colp = lax.broadcasted_iota(jnp.int32, (LANES, LANES), 1)
    blockdiag = (rowp < HD) == (colp < HD)
    idx = lax.broadcasted_iota(jnp.int32, (c_len, 1), 0).astype(F32)

    def direction(dec_ref, forward):
        lg = -jnp.exp(dec_ref[...])
        diff = (row - col) if forward else (col - row)
        keep = (diff >= 0) if forward else (diff > 0)
        dist = jnp.maximum(diff, 0).astype(F32)
        dm = jnp.concatenate([jnp.where(keep, jnp.exp(dist * lg[:, off:off + 1]), 0.0) for off in (0, HD)], axis=0)
        if forward:
            xi = jnp.exp((idx + 1.0) * lg)
            zeta = jnp.exp((c_len - 1.0 - idx) * lg)
        else:
            xi = jnp.exp((c_len - idx) * lg)
            zeta = jnp.exp(idx * lg)
        return dm, xi, zeta, jnp.exp(c_len * lg)

    def chunk(c, state, consts):
        dm, xi, zeta, gch = consts
        rows = pl.ds(pl.multiple_of(c * c_len, c_len), c_len)
        qc = q_ref[rows, :]
        kc = k_ref[rows, :] * HD ** -0.5
        vc = v_ref[rows, :].astype(BF16)
        kb = kc.astype(BF16)
        q2 = jnp.concatenate([jnp.where(lo, qc, 0.0), jnp.where(hi, qc, 0.0)], axis=0).astype(BF16)
        inner = lax.dot_general(q2, kb, NT_DIMS, preferred_element_type=F32) * dm
        kz_t = (kc * zeta).T
        res = jnp.dot(jnp.concatenate([inner, kz_t], axis=0).astype(BF16), vc, preferred_element_type=F32)
        cross = jnp.dot(qc.astype(BF16), state.astype(BF16), preferred_element_type=F32) * xi
        o = jnp.where(lo, res[:c_len], res[c_len:2 * c_len]) + cross
        state = gch * state + jnp.where(blockdiag, res[2 * c_len:], 0.0)
        return rows, o, state

    cf = direction(df_ref, True)
    cb = direction(db_ref, False)

    def scan_body(t, states):
        rows_f, o_f, state_f = chunk(t, states[0], cf)
        of_scr[rows_f, :] = o_f
        rows_b, o_b, state_b = chunk(n - 1 - t, states[1], cb)
        ob_scr[rows_b, :] = o_b
        return state_f, state_b

    state_f, state_b = lax.fori_loop(0, n, scan_body, (s0f_ref[...], s0b_ref[...]), unroll=min(n, RET_UNROLL))
    sf_ref[...] = state_f
    sb_ref[...] = state_b

    gn = gn_ref[...]
    n_norm = length // RET_NORM_ROWS

    def per_head(x):
        a = jnp.where(lo, x, 0.0).sum(axis=-1, keepdims=True)
        b = jnp.where(hi, x, 0.0).sum(axis=-1, keepdims=True)
        return jnp.where(lo, a, b) * (1.0 / HD)

    def norm_body(t, carry):
        rows = pl.ds(pl.multiple_of(t * RET_NORM_ROWS, RET_NORM_ROWS), RET_NORM_ROWS)
        o = of_scr[rows, :] + ob_scr[rows, :]
        d = o - per_head(o)
        y = d * lax.rsqrt(per_head(d * d) + EPS) * gn
        o_ref[rows, :] = _silu(g_ref[rows, :]) * y
        return carry

    lax.fori_loop(0, n_norm, norm_body, 0)


def _pair_lanes(v):
    return jnp.repeat(v.astype(F32), HD).reshape(B_HEADS // 2, 1, LANES)


def _blockdiag_states(s):
    b = s.shape[0]
    s = s.astype(F32).reshape(b, B_HEADS // 2, 2, HD, HD)
    z = jnp.zeros_like(s[:, :, 0])
    top = jnp.concatenate([s[:, :, 0], z], axis=-1)
    bot = jnp.concatenate([z, s[:, :, 1]], axis=-1)
    return jnp.concatenate([top, bot], axis=-2)


def _diag_states(sp):
    b = sp.shape[0]
    s = jnp.stack([sp[:, :, :HD, :HD], sp[:, :, HD:, HD:]], axis=2)
    return s.reshape(b, B_HEADS, HD, HD)


def _retention(p, row_base, batch, length, dec_f, dec_b, gn_g, s0f, s0b):
    npairs = B_HEADS // 2
    blk0 = row_base // length
    qcol = (A_Q + 2 * A_KV) // LANES

    def col_spec(off):
        return pl.BlockSpec((length, LANES), lambda b, h: (blk0 + b, qcol + off * npairs + h))

    lane_spec = pl.BlockSpec((None, 1, LANES), lambda b, h: (h, 0, 0))
    state_spec = pl.BlockSpec((None, None, LANES, LANES), lambda b, h: (b, h, 0, 0))
    state_shape = jax.ShapeDtypeStruct((batch, npairs, LANES, LANES), F32)
    return pl.pallas_call(
        functools.partial(_ret_kernel, length=length),
        grid=(batch, npairs),
        in_specs=[lane_spec, lane_spec, col_spec(0), col_spec(1), col_spec(2), col_spec(3), lane_spec,
                  state_spec, state_spec],
        out_specs=[pl.BlockSpec((length, LANES), lambda b, h: (b, h)), state_spec, state_spec],
        out_shape=[jax.ShapeDtypeStruct((batch * length, B_W), F32), state_shape, state_shape],
        scratch_shapes=[pltpu.VMEM((length, LANES), F32), pltpu.VMEM((length, LANES), F32)],
        compiler_params=_cparams(2),
        name="retention",
    )(_pair_lanes(dec_f), _pair_lanes(dec_b), p, p, p, p, gn_g.reshape(npairs, 1, LANES), s0f, s0b)


def _ctx_mha_kernel(q_ref, k_ref, v_ref, o_ref):
    lo = _lane_lo()
    for pair in range(C_HEADS // 2):
        cols = slice(pair * LANES, (pair + 1) * LANES)
        q = _stack_heads(q_ref, [2 * pair, 2 * pair + 1], lo, HD ** -0.5)
        s = lax.dot_general(q, k_ref[:, cols].astype(BF16), NT_DIMS, preferred_element_type=F32)
        o = _softmax_av([s], [v_ref[:, cols].astype(BF16)])
        o_ref[:, cols] = jnp.where(lo, o[:SEQ], o[SEQ:])


def _ctx_mha(p):
    return pl.pallas_call(
        _ctx_mha_kernel,
        grid=(BATCH,),
        in_specs=[pl.BlockSpec((SEQ, C_W), lambda b: (b, 0)),
                  pl.BlockSpec((SEQ, C_W), lambda b: (b, 1)),
                  pl.BlockSpec((SEQ, C_W), lambda b: (b, 2))],
        out_specs=pl.BlockSpec((SEQ, C_W), lambda b: (b, 0)),
        out_shape=jax.ShapeDtypeStruct((NP_TOK, C_W), F32),
        compiler_params=_cparams(1),
        name="ctx_mha",
    )(p, p, p)


NA_WIN_ROWS = 2 * NA_ROWS
NA_WIN = NA_WIN_ROWS * GRID_W
NA_QROWS = NA_ROWS * GRID_W
NA_PAD_ROWS = NA_KH // 2
NA_TABLE = 1536


def _na_kernel(q_ref, kp_ref, km_ref, kn_ref, vp_ref, vm_ref, vn_ref, ck_ref, cv_ref, ue_ref, uo_ref, o_ref):
    rb = pl.program_id(2)
    r0 = rb * NA_ROWS
    n_rows = DEC_SEQ // GRID_W
    lo = _lane_lo()
    kw = jnp.concatenate([kp_ref[...], km_ref[...], kn_ref[...]], axis=0).astype(BF16)
    vw = jnp.concatenate([vp_ref[...], vm_ref[...], vn_ref[...]], axis=0).astype(BF16)
    ck = ck_ref[...].astype(BF16)
    cv = cv_ref[...].astype(BF16)
    q = q_ref[...] * HD ** -0.5
    interior = (rb > 0) & (rb < DEC_SEQ // NA_QROWS - 1)
    n_local = NA_KH * GRID_W

    def softmax_piece(sl, sc):
        mx = jnp.maximum(sl.max(axis=-1, keepdims=True), sc.max(axis=-1, keepdims=True))
        el = jnp.exp(sl - mx)
        ec = jnp.exp(sc - mx)
        return el.astype(BF16), ec.astype(BF16), el.sum(axis=-1, keepdims=True) + ec.sum(axis=-1, keepdims=True)

    @pl.when(interior)
    def _():
        outs = []
        for half, keep in enumerate((lo, jnp.logical_not(lo))):
            qh = jnp.where(keep, q, 0.0).astype(BF16)
            s_ctx = lax.dot_general(qh, ck, NT_DIMS, preferred_element_type=F32)
            u = uo_ref[half, :, (NA_KH - 2) * GRID_W:(NA_KH - 2) * GRID_W + n_local]
            o_rows = [None] * NA_ROWS
            for parity in range(2):
                rqs = list(range(parity, NA_ROWS, 2))
                kk = kw[parity * GRID_W:parity * GRID_W + NA_WIN]
                vv = vw[parity * GRID_W:parity * GRID_W + NA_WIN]
                qsel = jnp.concatenate([qh[rq * GRID_W:(rq + 1) * GRID_W] for rq in rqs], axis=0)
                s = lax.dot_general(qsel, kk, NT_DIMS, preferred_element_type=F32)
                p_loc, p_ctx, den = [], [], []
                for t, rq in enumerate(rqs):
                    lane0 = (rq // 2) * LANES
                    el, ec, dn = softmax_piece(s[t * GRID_W:(t + 1) * GRID_W, lane0:lane0 + n_local] + u,
                                               s_ctx[rq * GRID_W:(rq + 1) * GRID_W])
                    row = [el]
                    if lane0:
                        row.insert(0, jnp.zeros((GRID_W, lane0), BF16))
                    if NA_WIN - lane0 - n_local:
                        row.append(jnp.zeros((GRID_W, NA_WIN - lane0 - n_local), BF16))
                    p_loc.append(jnp.concatenate(row, axis=1))
                    p_ctx.append(ec)
                    den.append(dn)
                acc = (jnp.dot(jnp.concatenate(p_loc, axis=0), vv, preferred_element_type=F32)
                       + jnp.dot(jnp.concatenate(p_ctx, axis=0), cv, preferred_element_type=F32))
                o = acc / jnp.concatenate(den, axis=0)
                for t, rq in enumerate(rqs):
                    o_rows[rq] = o[t * GRID_W:(t + 1) * GRID_W]
            outs.append(jnp.concatenate(o_rows, axis=0))
        o_ref[...] = jnp.where(lo, outs[0], outs[1])

    @pl.when(jnp.logical_not(interior))
    def _():
        k = kw[:NA_WIN]
        v = vw[:NA_WIN]
        klane = lax.broadcasted_iota(jnp.int32, (1, NA_WIN), 1)
        outs = []
        for half, keep in enumerate((lo, jnp.logical_not(lo))):
            qh = jnp.where(keep, q, 0.0).astype(BF16)
            s = lax.dot_general(qh, k, NT_DIMS, preferred_element_type=F32)
            s_ctx = lax.dot_general(qh, ck, NT_DIMS, preferred_element_type=F32)
            p_loc, p_ctx, den = [], [], []
            for rq in range(NA_ROWS):
                rows = slice(rq * GRID_W, (rq + 1) * GRID_W)
                start = NA_KH - 1 - rq
                if start % 2 == 0:
                    u = ue_ref[half, :, start * GRID_W:start * GRID_W + NA_WIN]
                else:
                    u = uo_ref[half, :, (start - 1) * GRID_W:(start - 1) * GRID_W + NA_WIN]
                r = r0 + rq
                first = jnp.clip(r - NA_KH // 2, 0, n_rows - NA_KH)
                lane0 = (first - r0 + NA_PAD_ROWS) * GRID_W
                in_rows = (klane >= lane0) & (klane < lane0 + n_local)
                el, ec, dn = softmax_piece(jnp.where(in_rows, s[rows] + u, NEG), s_ctx[rows])
                p_loc.append(el)
                p_ctx.append(ec)
                den.append(dn)
            acc = (jnp.dot(jnp.concatenate(p_loc, axis=0), v, preferred_element_type=F32)
                   + jnp.dot(jnp.concatenate(p_ctx, axis=0), cv, preferred_element_type=F32))
            outs.append(acc / jnp.concatenate(den, axis=0))
        o_ref[...] = jnp.where(lo, outs[0], outs[1])


def _na_bias_tables(rpb):
    cq = jnp.arange(GRID_W)
    ck = jnp.arange(GRID_W)
    dc = jnp.clip(ck[None] - cq[:, None], -(NA_KW - 1), NA_KW - 1) + NA_KW - 1
    cs = jnp.clip(cq - NA_KW // 2, 0, GRID_W - NA_KW)
    col_ok = (ck[None] >= cs[:, None]) & (ck[None] < cs[:, None] + NA_KW)
    t = rpb.astype(F32)[:, :, dc]
    t = jnp.where(col_ok[None, None], t, NEG).transpose(0, 2, 1, 3)
    n_dr = 2 * NA_KH - 1
    blocks = NA_TABLE // GRID_W
    t = jnp.pad(t, ((0, 0), (0, 0), (NA_PAD_ROWS, blocks - n_dr - NA_PAD_ROWS), (0, 0)), constant_values=NEG)
    ue = t.reshape(C_HEADS, GRID_W, NA_TABLE)
    uo = jnp.concatenate([ue[..., GRID_W:], jnp.full((C_HEADS, GRID_W, GRID_W), NEG, F32)], axis=-1)
    return ue, uo


def _na_attention(p, cache_k, cache_v, rpb):
    npairs = C_HEADS // 2
    nrb = DEC_SEQ // NA_QROWS
    half = NA_QROWS // 2
    qbase = NP_TOK // NA_QROWS
    hbase = NP_TOK // half
    kcol = C_W // LANES
    ue, uo = _na_bias_tables(rpb)

    def main_spec(col0):
        return pl.BlockSpec((NA_QROWS, LANES), lambda b, h, r: (qbase + b * nrb + r, col0 + h))

    def prev_spec(col0):
        return pl.BlockSpec((half, LANES),
                            lambda b, h, r: (hbase + b * 2 * nrb + jnp.maximum(2 * r - 1, 0), col0 + h))

    def next_spec(col0):
        return pl.BlockSpec((NA_QROWS, LANES),
                            lambda b, h, r: (qbase + b * nrb + jnp.minimum(r + 1, nrb - 1), col0 + h))

    ctx_spec = pl.BlockSpec((None, PAST, LANES), lambda b, h, r: (b, 0, h))
    tab_spec = pl.BlockSpec((2, GRID_W, NA_TABLE), lambda b, h, r: (h, 0, 0))
    return pl.pallas_call(
        _na_kernel,
        grid=(DEC_BATCH, npairs, nrb),
        in_specs=[main_spec(0),
                  prev_spec(kcol), main_spec(kcol), next_spec(kcol),
                  prev_spec(2 * kcol), main_spec(2 * kcol), next_spec(2 * kcol),
                  ctx_spec, ctx_spec, tab_spec, tab_spec],
        out_specs=pl.BlockSpec((NA_QROWS, LANES), lambda b, h, r: (b * nrb + r, h)),
        out_shape=jax.ShapeDtypeStruct((NS_TOK, C_W), F32),
        compiler_params=_cparams(3),
        name="na_attn",
    )(p, p, p, p, p, p, p, cache_k.reshape(DEC_BATCH, PAST, C_W), cache_v.reshape(DEC_BATCH, PAST, C_W), ue, uo)


def _route(biased, scores):
    t = biased.shape[1]
    per_group = N_EXPERTS // N_GROUPS
    i8 = lax.broadcasted_iota(jnp.int32, (per_group, t), 0)
    g_rows = []
    for g in range(N_GROUPS):
        bg = biased[g * per_group:(g + 1) * per_group]
        m1 = bg.max(axis=0, keepdims=True)
        first = jnp.where(bg == m1, i8, per_group).min(axis=0, keepdims=True)
        m2 = jnp.where(i8 == first, -jnp.inf, bg).max(axis=0, keepdims=True)
        g_rows.append(m1 + m2)
    g_top = jnp.concatenate(g_rows, axis=0)
    gi = lax.broadcasted_iota(jnp.int32, g_top.shape, 0)
    g_sel = jnp.zeros(g_top.shape, jnp.int32)
    cur = g_top
    for _ in range(TOPK_GROUPS):
        m = cur.max(axis=0, keepdims=True)
        hit = gi == jnp.where(cur == m, gi, N_GROUPS).min(axis=0, keepdims=True)
        g_sel = jnp.where(hit, 1, g_sel)
        cur = jnp.where(hit, -jnp.inf, cur)
    e_sel = jnp.concatenate([jnp.broadcast_to(g_sel[g:g + 1], (per_group, t)) for g in range(N_GROUPS)], axis=0)
    cur = jnp.where(e_sel > 0, biased, NEG)
    ei = lax.broadcasted_iota(jnp.int32, cur.shape, 0)
    ids, gates, hits = [], [], []
    for _ in range(TOP_K):
        m = cur.max(axis=0, keepdims=True)
        f = jnp.where(cur == m, ei, N_EXPERTS).min(axis=0, keepdims=True)
        hit = ei == f
        ids.append(f)
        hits.append(hit)
        gates.append(jnp.where(hit, scores, 0.0).sum(axis=0, keepdims=True))
        cur = jnp.where(hit, -jnp.inf, cur)
    gate = jnp.concatenate(gates, axis=0)
    gate = gate / gate.sum(axis=0, keepdims=True) * ROUTED_SCALE
    return jnp.concatenate(ids, axis=0), gate, hits


def _pack_bf16_pairs(h):
    bits = lax.bitcast_convert_type(h.astype(BF16).astype(F32), jnp.uint32)
    return bits[:, :D // 2] | (bits[:, D // 2:] >> 16)


def _unpack_bf16_pairs(xp):
    hi = lax.bitcast_convert_type(xp & jnp.uint32(0xFFFF0000), F32).astype(BF16)
    lo = lax.bitcast_convert_type(xp << 16, F32).astype(BF16)
    return hi, lo


def _dot_halves(hi, lo, w_ref):
    return (jnp.dot(hi, w_ref[:D // 2, :], preferred_element_type=F32)
            + jnp.dot(lo, w_ref[D // 2:, :], preferred_element_type=F32))


def _outproj_kernel(*refs, n_parts):
    xp_ref, xs_ref = refs[:2]
    part_refs = refs[2:2 + 3 * n_parts]
    gate_ref, shift_ref, scale_ref, g2_ref, rw_ref, rb_ref = refs[2 + 3 * n_parts:8 + 3 * n_parts]
    xo_ref, h_ref, dest_ref, wgt_ref, plan_ref, cnt_ref = refs[8 + 3 * n_parts:]
    step = pl.program_id(0)

    @pl.when(step == 0)
    def _():
        cnt_ref[...] = jnp.zeros_like(cnt_ref)
        plan_ref[...] = jnp.zeros_like(plan_ref)

    y = None
    for t in range(n_parts):
        ap_ref, as_ref, w_ref = part_refs[3 * t:3 * t + 3]
        d = jnp.dot(_pick_rows(ap_ref, as_ref).astype(BF16), w_ref[...], preferred_element_type=F32)
        y = d if y is None else y + d
    x = _pick_rows(xp_ref, xs_ref) + gate_ref[...] * y
    xo_ref[...] = x
    h = _rms(x, g2_ref[...]) * (1.0 + scale_ref[...]) + shift_ref[...]
    h_ref[...] = _pack_bf16_pairs(h)
    logits = lax.dot_general(rw_ref[...], h, NT_DIMS, preferred_element_type=F32, precision=HIGHEST)
    scores = jax.nn.sigmoid(logits)
    _, gate, hits = _route(scores + rb_ref[...], scores)
    wgt_ref[...] = gate
    chosen = hits[0]
    for hit in hits[1:]:
        chosen = chosen | hit
    m = jnp.where(chosen, 1.0, 0.0)
    before = (lax.broadcasted_iota(jnp.int32, (TM, TM), 0) < lax.broadcasted_iota(jnp.int32, (TM, TM), 1))
    prefix = jnp.dot(m.astype(BF16), jnp.where(before, 1.0, 0.0).astype(BF16), preferred_element_type=F32)
    e_base = (lax.broadcasted_iota(jnp.int32, (N_EXPERTS, 1), 0) * N_TOK).astype(F32)
    row_all = prefix + (cnt_ref[...] + e_base)
    dest_ref[...] = jnp.concatenate(
        [jnp.where(hit, row_all, 0.0).sum(axis=0, keepdims=True) for hit in hits], axis=0).astype(jnp.int32)
    cnt_ref[...] += m.sum(axis=1, keepdims=True)

    @pl.when(step == pl.num_programs(0) - 1)
    def _():
        _block_plan(cnt_ref[...], plan_ref)


def _block_plan(counts, plan_ref):
    cap_blocks = N_TOK // MOE_BLOCK
    nblk = ((counts.astype(jnp.int32) + (MOE_BLOCK - 1)) // MOE_BLOCK).astype(F32)
    lower = (lax.broadcasted_iota(jnp.int32, (N_EXPERTS, N_EXPERTS), 0)
             >= lax.broadcasted_iota(jnp.int32, (N_EXPERTS, N_EXPERTS), 1))
    cum = jnp.dot(jnp.where(lower, 1.0, 0.0).astype(BF16), jnp.broadcast_to(nblk, (N_EXPERTS, LANES)).astype(BF16),
                  preferred_element_type=F32)[:, :1]
    n_used = cum[N_EXPERTS - 1:, :]
    slot = jnp.minimum(lax.broadcasted_iota(jnp.int32, (1, PLAN_LANES), 1).astype(F32), n_used - 1.0)
    done = cum <= slot
    expert = jnp.where(done, 1.0, 0.0).sum(axis=0, keepdims=True)
    blocks_before = jnp.where(done, nblk, 0.0).sum(axis=0, keepdims=True)
    plan_ref[0:1, :] = (expert * cap_blocks + (slot - blocks_before)).astype(jnp.int32)
    plan_ref[1:2, :] = expert.astype(jnp.int32)
    plan_ref[2:3, :] = jnp.broadcast_to(n_used, (1, PLAN_LANES)).astype(jnp.int32)


def _outproj(x, parts, mod, g2, router_w, router_b):
    in_specs = _pair_specs(D)
    args = [x[0], x[1]]
    for ap, a_s, w in parts:
        width = ap.shape[1]
        in_specs += _pair_specs(width) + [pl.BlockSpec((width, D), lambda i: (0, 0))]
        args += [ap, a_s, w]
    in_specs += [_mod_spec(2), _mod_spec(3), _mod_spec(4),
                 pl.BlockSpec((1, D), lambda i: (0, 0)),
                 pl.BlockSpec((N_EXPERTS, D), lambda i: (0, 0)),
                 pl.BlockSpec((N_EXPERTS, 1), lambda i: (0, 0))]
    args += [mod, mod, mod, g2.reshape(1, D), router_w.T, router_b.reshape(N_EXPERTS, 1)]
    return pl.pallas_call(
        functools.partial(_outproj_kernel, n_parts=len(parts)),
        grid=(N_TOK // TM,),
        in_specs=in_specs,
        out_specs=[pl.BlockSpec((TM, D), lambda i: (i, 0)),
                   pl.BlockSpec((TM, D // 2), lambda i: (i, 0)),
                   pl.BlockSpec((TOP_K, TM), lambda i: (0, i)),
                   pl.BlockSpec((TOP_K, TM), lambda i: (0, i)),
                   pl.BlockSpec((8, PLAN_LANES), lambda i: (0, 0))],
        out_shape=[jax.ShapeDtypeStruct((N_TOK, D), F32),
                   jax.ShapeDtypeStruct((N_TOK, D // 2), jnp.uint32),
                   jax.ShapeDtypeStruct((TOP_K, N_TOK), jnp.int32),
                   jax.ShapeDtypeStruct((TOP_K, N_TOK), F32),
                   jax.ShapeDtypeStruct((8, PLAN_LANES), jnp.int32)],
        scratch_shapes=[pltpu.VMEM((N_EXPERTS, 1), F32)],
        compiler_params=_cparams(1),
        name="outproj_router",
    )(*args)


def _experts_kernel(br_ref, be_ref, nu_ref, x_ref, w1_ref, w3_ref, w2_ref, o_ref, w1b, w3b, w2b):
    i = pl.program_id(0)
    e = be_ref[i]
    prev = be_ref[jnp.maximum(i - 1, 0)]

    @pl.when((i == 0) | (e != prev))
    def _():
        w1b[...] = w1_ref[...].astype(BF16)
        w3b[...] = w3_ref[...].astype(BF16)
        w2b[...] = w2_ref[...].astype(BF16)

    @pl.when(i < nu_ref[0])
    def _():
        hi, lo = _unpack_bf16_pairs(x_ref[...])
        a = _dot_halves(hi, lo, w1b)
        b = _dot_halves(hi, lo, w3b)
        h = (_silu(a) * b).astype(BF16)
        o_ref[...] = _pack_bf16_pairs(jnp.dot(h, w2b[...], preferred_element_type=F32))


def _experts(plan, x_rows, w1, w3, w2, layer):
    return pl.pallas_call(
        _experts_kernel,
        grid_spec=pltpu.PrefetchScalarGridSpec(
            num_scalar_prefetch=3,
            grid=(N_MOE_BLOCKS,),
            in_specs=[pl.BlockSpec((MOE_BLOCK, D // 2), lambda i, br, be, nu: (br[i], 0)),
                      pl.BlockSpec((None, None, D, FF), lambda i, br, be, nu: (layer, be[i], 0, 0)),
                      pl.BlockSpec((None, None, D, FF), lambda i, br, be, nu: (layer, be[i], 0, 0)),
                      pl.BlockSpec((None, None, FF, D), lambda i, br, be, nu: (layer, be[i], 0, 0))],
            out_specs=pl.BlockSpec((MOE_BLOCK, D // 2), lambda i, br, be, nu: (br[i], 0)),
            scratch_shapes=[pltpu.VMEM((D, FF), BF16), pltpu.VMEM((D, FF), BF16), pltpu.VMEM((FF, D), BF16)]),
        out_shape=jax.ShapeDtypeStruct(x_rows.shape, jnp.uint32),
        compiler_params=_cparams(1),
        name="experts",
    )(plan[0], plan[1], plan[2, :1], x_rows, w1, w3, w2)


SC_CORES = 2
SC_SUBCORES = 16
SC_WORKERS = SC_CORES * SC_SUBCORES
SC_CHUNK_BYTES = 64 * 1024
SC_SLOTS = 4


def _sc_scatter(rows, dest, n_out):
    n_rows, width = rows.shape
    picks = dest.shape[0]
    chunk = SC_CHUNK_BYTES // (4 * width)
    per_worker = n_rows // SC_WORKERS
    n_chunks = per_worker // chunk
    assert per_worker * SC_WORKERS == n_rows and n_chunks * chunk == per_worker and n_chunks % 2 == 0
    mesh = plsc.VectorSubcoreMesh(core_axis_name="c", subcore_axis_name="s")

    @functools.partial(
        pl.kernel, mesh=mesh,
        out_type=jax.ShapeDtypeStruct((n_out, width), rows.dtype),
        scratch_types=[pltpu.VMEM((picks, n_chunks, chunk), jnp.int32),
                       pltpu.VMEM((2, chunk, width), rows.dtype),
                       pltpu.SemaphoreType.DMA((2,)),
                       pltpu.SemaphoreType.DMA((2,))])
    def scatter(r_hbm, d_hbm, o_hbm, idx_v, rows_v, lsem, ssem):
        worker = lax.axis_index("s") * SC_CORES + lax.axis_index("c")
        base = worker * per_worker
        for k in range(picks):
            pltpu.sync_copy(d_hbm.at[k, worker], idx_v.at[k])

        def load_copy(c, b):
            src = pl.ds(pl.multiple_of(base + c * chunk, chunk), chunk)
            return pltpu.make_async_copy(r_hbm.at[src], rows_v.at[b], lsem.at[b])

        def store_copy(c, b, k):
            return pltpu.make_async_copy(rows_v.at[b], o_hbm.at[idx_v.at[k, c]], ssem.at[b])

        load_copy(0, 0).start()

        @pl.loop(0, n_chunks, step=2)
        def _(c0):
            for b in range(2):
                c = c0 + b
                load_copy(c, b).wait()
                for k in range(picks):
                    store_copy(c, b, k).start()

                @pl.when(c > 0)
                def _():
                    for k in range(picks):
                        store_copy(c - 1, 1 - b, k).wait()

                @pl.when(c + 1 < n_chunks)
                def _():
                    load_copy(c + 1, 1 - b).start()

        for k in range(picks):
            store_copy(n_chunks - 1, 1, k).wait()

    return scatter(rows, dest.reshape(picks, SC_WORKERS, n_chunks, chunk))


def _sc_gather(table, idx):
    n_idx = idx.shape[0]
    width = table.shape[1]
    chunk = SC_CHUNK_BYTES // (4 * width)
    per_worker = n_idx // SC_WORKERS
    n_chunks = per_worker // chunk
    ahead = SC_SLOTS - 1
    assert per_worker * SC_WORKERS == n_idx and n_chunks * chunk == per_worker and n_chunks % SC_SLOTS == 0
    mesh = plsc.VectorSubcoreMesh(core_axis_name="c", subcore_axis_name="s")

    @functools.partial(
        pl.kernel, mesh=mesh,
        out_type=jax.ShapeDtypeStruct((n_idx, width), table.dtype),
        scratch_types=[pltpu.VMEM((per_worker,), jnp.int32),
                       pltpu.VMEM((SC_SLOTS, chunk, width), table.dtype),
                       pltpu.SemaphoreType.DMA((SC_SLOTS,)),
                       pltpu.SemaphoreType.DMA((SC_SLOTS,))])
    def gather(t_hbm, i_hbm, o_hbm, idx_v, rows_v, gsem, wsem):
        worker = lax.axis_index("s") * SC_CORES + lax.axis_index("c")
        base = worker * per_worker
        pltpu.sync_copy(i_hbm.at[pl.ds(pl.multiple_of(base, chunk), per_worker)], idx_v)

        def gather_copy(c, b):
            ids = idx_v.at[pl.ds(pl.multiple_of(c * chunk, chunk), chunk)]
            return pltpu.make_async_copy(t_hbm.at[ids], rows_v.at[b], gsem.at[b])

        def write_copy(c, b):
            rows = pl.ds(pl.multiple_of(base + c * chunk, chunk), chunk)
            return pltpu.make_async_copy(rows_v.at[b], o_hbm.at[rows], wsem.at[b])

        for c in range(ahead):
            gather_copy(c, c).start()

        @pl.loop(0, n_chunks, step=SC_SLOTS)
        def _(c0):
            for b in range(SC_SLOTS):
                c = c0 + b
                refill = (b + ahead) % SC_SLOTS
                gather_copy(c, b).wait()
                write_copy(c, b).start()

                @pl.when(c > 0)
                def _():
                    write_copy(c - 1, refill).wait()

                @pl.when(c + ahead < n_chunks)
                def _():
                    gather_copy(c + ahead, refill).start()

        write_copy(n_chunks - 1, (n_chunks - 1) % SC_SLOTS).wait()

    return gather(table, idx)


TC = 512


def _combine_kernel(x_ref, h_ref, y_ref, wgt_ref, gate_ref, w1_ref, w3_ref, w2_ref, fg_ref, o_ref, *, final):
    hi, lo = _unpack_bf16_pairs(h_ref[...])
    a = _dot_halves(hi, lo, w1_ref)
    b = _dot_halves(hi, lo, w3_ref)
    ffn = jnp.dot((_silu(a) * b).astype(BF16), w2_ref[...], preferred_element_type=F32)
    wgt = wgt_ref[...]
    r_hi = None
    r_lo = None
    for k in range(TOP_K):
        yk = y_ref[k]
        w = wgt[:, k:k + 1]
        t_hi = lax.bitcast_convert_type(yk & jnp.uint32(0xFFFF0000), F32) * w
        t_lo = lax.bitcast_convert_type(yk << 16, F32) * w
        r_hi = t_hi if r_hi is None else r_hi + t_hi
        r_lo = t_lo if r_lo is None else r_lo + t_lo
    x = x_ref[...] + gate_ref[...] * (ffn + jnp.concatenate([r_hi, r_lo], axis=1))
    o_ref[...] = _rms(x, fg_ref[...]) if final else x


def _combine(x, h, y_rows, wgt, mod, sw1, sw3, sw2, final_g, final):
    weights = (sw1.astype(BF16), sw3.astype(BF16), sw2.astype(BF16), final_g.reshape(1, D))

    def rows_from(first_row, n_rows):
        b0 = first_row // TC
        return pl.pallas_call(
            functools.partial(_combine_kernel, final=final),
            grid=(n_rows // TC,),
            in_specs=[pl.BlockSpec((TC, D), lambda i: (i + b0, 0)),
                      pl.BlockSpec((TC, D // 2), lambda i: (i + b0, 0)),
                      pl.BlockSpec((TOP_K, TC, D // 2), lambda i: (0, i + b0, 0)),
                      pl.BlockSpec((TC, TOP_K), lambda i: (i + b0, 0)),
                      _mod_spec(5, TC, b0),
                      pl.BlockSpec((D, FF), lambda i: (0, 0)),
                      pl.BlockSpec((D, FF), lambda i: (0, 0)),
                      pl.BlockSpec((FF, D), lambda i: (0, 0)),
                      pl.BlockSpec((1, D), lambda i: (0, 0))],
            out_specs=pl.BlockSpec((TC, D), lambda i: (i, 0)),
            out_shape=jax.ShapeDtypeStruct((n_rows, D), F32),
            compiler_params=_cparams(1),
            name="combine",
        )(x, h, y_rows, wgt, mod, *weights)

    return rows_from(0, NP_TOK), rows_from(NP_TOK, NS_TOK)


def kernel(x_prompt, x_sample, cache_a_k, cache_a_v, state_ret_fwd, state_ret_bwd, cache_c_k, cache_c_v,
           c, c_ctx, norm1_g, norm2_g, ada_w, ada_b, even_w_in, even_w_out, sink_a, ret_decay_fwd,
           ret_decay_bwd, ret_gn_g, odd_w_in, odd_w_out, na_rpb, router_w, router_b, exp_w1, exp_w3,
           exp_w2, sh_w1, sh_w3, sh_w2, final_g):
    x = (x_prompt.reshape(NP_TOK, D), x_sample.reshape(NS_TOK, D))
    cc = jnp.concatenate([c_ctx[None], c, jnp.zeros((8 - 1 - DEC_BATCH, D), F32)], axis=0)
    rope = _rope_tables()
    outs = {}
    for l in range(2):
        mod = _ada(cc, ada_w, ada_b, l)
        if l == 0:
            p = _inproj(x, norm1_g[l], mod, even_w_in[0].astype(BF16), rope, A_Q + A_KV)
            oa_p = _ctx_gqa(p, sink_a[0])
            oa_s = _win_attention(p, cache_a_k[:, 0], cache_a_v[:, 0], sink_a[0])
            zero = jnp.zeros((BATCH, B_HEADS // 2, LANES, LANES), F32)
            ob_p, sf, sb = _retention(p, 0, BATCH, SEQ, ret_decay_fwd[0], ret_decay_bwd[0], ret_gn_g[0], zero, zero)
            ob_s, _, _ = _retention(p, NP_TOK, DEC_BATCH, DEC_SEQ, ret_decay_fwd[0], ret_decay_bwd[0], ret_gn_g[0],
                                    _blockdiag_states(state_ret_fwd[:, 0]), _blockdiag_states(state_ret_bwd[:, 0]))
            w_out = even_w_out[0].astype(BF16)
            parts = [(oa_p, oa_s, w_out[:A_Q]), (ob_p, ob_s, w_out[A_Q:])]
            outs["a_k"] = p[:NP_TOK, A_Q:A_Q + A_KV].reshape(BATCH, 1, SEQ, A_KV_HEADS, HD)
            outs["a_v"] = p[:NP_TOK, A_Q + A_KV:A_Q + 2 * A_KV].reshape(BATCH, 1, SEQ, A_KV_HEADS, HD)
            outs["r_f"] = _diag_states(sf).reshape(BATCH, 1, B_HEADS, HD, HD)
            outs["r_b"] = _diag_states(sb).reshape(BATCH, 1, B_HEADS, HD, HD)
        else:
            p = _inproj(x, norm1_g[l], mod, odd_w_in[0].astype(BF16), rope, 0)
            o_p = _ctx_mha(p)
            o_s = _na_attention(p, cache_c_k[:, 0], cache_c_v[:, 0], na_rpb[0])
            parts = [(o_p, o_s, odd_w_out[0].astype(BF16))]
            outs["c_k"] = p[:NP_TOK, C_W:2 * C_W].reshape(BATCH, 1, SEQ, C_HEADS, HD)
            outs["c_v"] = p[:NP_TOK, 2 * C_W:3 * C_W].reshape(BATCH, 1, SEQ, C_HEADS, HD)
        x_mid, h, dest, gate_t, plan = _outproj(x, parts, mod, norm2_g[l], router_w[l], router_b[l])
        y = _experts(plan, _sc_scatter(h, dest, N_EXPERTS * N_TOK), exp_w1, exp_w3, exp_w2, l)
        y_rows = _sc_gather(y, dest.reshape(N_ASSIGN)).reshape(TOP_K, N_TOK, D // 2)
        x = _combine(x_mid, h, y_rows, gate_t.T, mod, sh_w1[l], sh_w3[l], sh_w2[l], final_g, final=(l == 1))
    y_prompt = x[0].reshape(BATCH, SEQ, D)
    y_sample = x[1].reshape(DEC_BATCH, DEC_SEQ, D)
    return (y_prompt, y_sample, outs["a_k"], outs["a_v"], outs["r_f"], outs["r_b"], outs["c_k"], outs["c_v"])
```

```python
import functools
import math

import jax
import jax.numpy as jnp
from jax import lax
from jax.experimental import pallas as pl
from jax.experimental.pallas import tpu as pltpu
from jax.experimental.pallas import tpu_sc as plsc

F32 = jnp.float32
BF16 = jnp.bfloat16
HIGHEST = lax.Precision.HIGHEST

D = 1024
BATCH = 32
SEQ = 256
DEC_BATCH = 4
DEC_SEQ = 4096
PAST = 256
GRID_W = 64
HD = 64
EPS = 1e-6
NEG = -1e30
ROPE_BASE = 10000.0
A_HEADS = 8
A_KV_HEADS = 2
A_Q = A_HEADS * HD
A_KV = A_KV_HEADS * HD
B_HEADS = 8
B_W = B_HEADS * HD
EVEN_IN = A_Q + 2 * A_KV + 4 * B_W
C_HEADS = 16
C_W = C_HEADS * HD
NA_KH = 8
NA_KW = 16
N_EXPERTS = 64
TOP_K = 8
N_GROUPS = 8
TOPK_GROUPS = 4
FF = 256
ROUTED_SCALE = 2.5
MOE_BLOCK = 1024
RET_CHUNK = 128
RET_UNROLL = 2
RET_NORM_ROWS = 256
A_WINDOW = 128

NP_TOK = BATCH * SEQ
NS_TOK = DEC_BATCH * DEC_SEQ
N_TOK = NP_TOK + NS_TOK
N_ASSIGN = N_TOK * TOP_K
N_MOE_BLOCKS = (N_ASSIGN + N_EXPERTS * (MOE_BLOCK - 1) + MOE_BLOCK - 1) // MOE_BLOCK
PLAN_LANES = 512
assert N_TOK % MOE_BLOCK == 0 and N_MOE_BLOCKS <= PLAN_LANES

LANES = 128
TM = 512
NA_ROWS = 8
V7X_VMEM_LIMIT = 56 * 1024 * 1024

NT_DIMS = (((1,), (1,)), ((), ()))


def _cparams(n_axes, vmem=V7X_VMEM_LIMIT):
    return pltpu.CompilerParams(dimension_semantics=("arbitrary",) * n_axes, vmem_limit_bytes=vmem)


def _seg_of_block(i, rows):
    row0 = i * rows
    return jnp.where(row0 < NP_TOK, 0, 1 + (row0 - NP_TOK) // DEC_SEQ)


def _mod_spec(chunk, rows=TM, first_block=0):
    return pl.BlockSpec((None, 1, D), lambda i: (_seg_of_block(i + first_block, rows), 0, chunk))


def _pair_specs(width, rows=TM):
    npb = NP_TOK // rows
    nsb = NS_TOK // rows
    return [pl.BlockSpec((rows, width), lambda i: (jnp.minimum(i, npb - 1), 0)),
            pl.BlockSpec((rows, width), lambda i: (jnp.clip(i - npb, 0, nsb - 1), 0))]


def _pick_rows(p_ref, s_ref, rows=TM):
    return jnp.where(pl.program_id(0) < NP_TOK // rows, p_ref[...], s_ref[...])


def _silu(x):
    return x * jax.nn.sigmoid(x)


def _rms(x, g):
    return x * lax.rsqrt(jnp.mean(x * x, axis=-1, keepdims=True) + EPS) * g


def _lane_lo():
    return lax.broadcasted_iota(jnp.int32, (1, LANES), 1) < HD


def _ada_kernel(c_ref, w_ref, b_ref, o_ref):
    a = _silu(c_ref[...])
    o_ref[...] = jnp.dot(a, w_ref[...], preferred_element_type=F32, precision=HIGHEST) + b_ref[...]


def _ada(cc, w, b, layer):
    tn = 1536
    out = pl.pallas_call(
        _ada_kernel,
        grid=(6 * D // tn,),
        in_specs=[pl.BlockSpec((8, D), lambda j: (0, 0)),
                  pl.BlockSpec((None, D, tn), lambda j: (layer, 0, j)),
                  pl.BlockSpec((None, 1, tn), lambda j: (layer, 0, j))],
        out_specs=pl.BlockSpec((8, tn), lambda j: (0, j)),
        out_shape=jax.ShapeDtypeStruct((8, 6 * D), F32),
        compiler_params=_cparams(1),
        name="ada",
    )(cc, w, b.reshape(b.shape[0], 1, 6 * D))
    return out.reshape(8, 1, 6 * D)


def _inproj_kernel(xp_ref, xs_ref, g_ref, shift_ref, scale_ref, w_ref, cos_ref, sin_ref, o_ref, *, rope_cols):
    h = _rms(_pick_rows(xp_ref, xs_ref), g_ref[...]) * (1.0 + scale_ref[...]) + shift_ref[...]
    o = jnp.dot(h.astype(BF16), w_ref[...], preferred_element_type=F32)
    if rope_cols:
        cos = cos_ref[...]
        sin = sin_ref[...]
        lane = lax.broadcasted_iota(jnp.int32, (1, LANES), 1)
        first = (lane % 32) < 16
        for c in range(rope_cols // LANES):
            oc = o[:, c * LANES:(c + 1) * LANES]
            partner = jnp.where(first, pltpu.roll(oc, LANES - 16, 1), pltpu.roll(oc, 16, 1))
            o_ref[:, c * LANES:(c + 1) * LANES] = oc * cos + partner * sin
        o_ref[:, rope_cols:] = o[:, rope_cols:]
    else:
        o_ref[...] = o


def _rope_tables():
    half = HD // 2
    inv = ROPE_BASE ** (-jnp.arange(0, half, 2, dtype=F32) / half)
    t = jnp.arange(DEC_SEQ)
    ang_r = (t // GRID_W).astype(F32)[:, None] * inv[None]
    ang_c = (t % GRID_W).astype(F32)[:, None] * inv[None]

    def head(fn_r, fn_c, sign):
        return jnp.concatenate([sign[0] * fn_r, sign[1] * fn_r, sign[0] * fn_c, sign[1] * fn_c], axis=-1)

    cos = head(jnp.cos(ang_r), jnp.cos(ang_c), (1.0, 1.0))
    sin = head(jnp.sin(ang_r), jnp.sin(ang_c), (-1.0, 1.0))
    cos = jnp.concatenate([jnp.ones((TM, HD), F32), cos], axis=0)
    sin = jnp.concatenate([jnp.zeros((TM, HD), F32), sin], axis=0)
    return jnp.tile(cos, (1, 2)), jnp.tile(sin, (1, 2))


def _inproj(x, g, mod, w_bf16, rope, rope_cols):
    n_out = w_bf16.shape[1]
    npb = NP_TOK // TM
    spb = DEC_SEQ // TM

    def rope_map(i):
        return (jnp.where(i < npb, 0, 1 + (i - npb) % spb), 0)

    return pl.pallas_call(
        functools.partial(_inproj_kernel, rope_cols=rope_cols),
        grid=(N_TOK // TM,),
        in_specs=_pair_specs(D) + [
                  pl.BlockSpec((1, D), lambda i: (0, 0)),
                  _mod_spec(0), _mod_spec(1),
                  pl.BlockSpec((D, n_out), lambda i: (0, 0)),
                  pl.BlockSpec((TM, LANES), rope_map),
                  pl.BlockSpec((TM, LANES), rope_map)],
        out_specs=pl.BlockSpec((TM, n_out), lambda i: (i, 0)),
        out_shape=jax.ShapeDtypeStruct((N_TOK, n_out), F32),
        compiler_params=_cparams(1),
        name="inproj",
    )(x[0], x[1], g.reshape(1, D), mod, mod, w_bf16, rope[0], rope[1])


def _softmax_av(s_list, v_list, sink=None):
    mx = s_list[0].max(axis=-1, keepdims=True)
    for s in s_list[1:]:
        mx = jnp.maximum(mx, s.max(axis=-1, keepdims=True))
    if sink is not None:
        mx = jnp.maximum(mx, sink)
    den = jnp.exp(sink - mx) if sink is not None else 0.0
    acc = None
    for s, v in zip(s_list, v_list):
        p = jnp.exp(s - mx)
        den = den + p.sum(axis=-1, keepdims=True)
        pv = jnp.dot(p.astype(BF16), v, preferred_element_type=F32)
        acc = pv if acc is None else acc + pv
    return acc / den


def _dup_half(x, j, lo):
    xr = pltpu.roll(x, HD, 1)
    return jnp.where(lo, x, xr) if j == 0 else jnp.where(lo, xr, x)


def _stack_heads(q_ref, heads, lo, scale):
    parts = []
    for h in heads:
        qp = q_ref[:, (h // 2) * LANES:(h // 2 + 1) * LANES]
        keep = lo if h % 2 == 0 else jnp.logical_not(lo)
        parts.append(jnp.where(keep, qp, 0.0) * scale)
    return jnp.concatenate(parts, axis=0).astype(BF16)


def _sink_column(sink_ref, heads, rows):
    return jnp.concatenate([jnp.full((rows, 1), sink_ref[h], F32) for h in heads], axis=0)


def _ctx_gqa_kernel(sink_ref, q_ref, k_ref, v_ref, o_ref):
    lo = _lane_lo()
    k = k_ref[...]
    v = v_ref[...]
    group = A_HEADS // A_KV_HEADS
    scores = []
    for j in range(A_KV_HEADS):
        q = _stack_heads(q_ref, list(range(group * j, group * (j + 1))), lo, HD ** -0.5)
        scores.append(lax.dot_general(q, _dup_half(k, j, lo).astype(BF16), NT_DIMS, preferred_element_type=F32))
    s = jnp.concatenate(scores, axis=0)
    sink = _sink_column(sink_ref, list(range(A_HEADS)), SEQ)
    mx = jnp.maximum(s.max(axis=-1, keepdims=True), sink)
    e = jnp.exp(s - mx)
    den = jnp.exp(sink - mx) + e.sum(axis=-1, keepdims=True)
    e = e.astype(BF16)
    rows_per_group = group * SEQ
    for j in range(A_KV_HEADS):
        rows = slice(j * rows_per_group, (j + 1) * rows_per_group)
        o = jnp.dot(e[rows], _dup_half(v, j, lo).astype(BF16), preferred_element_type=F32) / den[rows]
        for t in range(group // 2):
            pair = (group * j) // 2 + t
            o_ref[:, pair * LANES:(pair + 1) * LANES] = jnp.where(
                lo, o[(2 * t) * SEQ:(2 * t + 1) * SEQ], o[(2 * t + 1) * SEQ:(2 * t + 2) * SEQ])


def _ctx_gqa(p, sink):
    return pl.pallas_call(
        _ctx_gqa_kernel,
        grid_spec=pltpu.PrefetchScalarGridSpec(
            num_scalar_prefetch=1,
            grid=(BATCH,),
            in_specs=[pl.BlockSpec((SEQ, A_Q), lambda b, s: (b, 0)),
                      pl.BlockSpec((SEQ, A_KV), lambda b, s: (b, A_Q // A_KV)),
                      pl.BlockSpec((SEQ, A_KV), lambda b, s: (b, A_Q // A_KV + 1))],
            out_specs=pl.BlockSpec((SEQ, A_Q), lambda b, s: (b, 0))),
        out_shape=jax.ShapeDtypeStruct((NP_TOK, A_Q), F32),
        compiler_params=_cparams(1),
        name="ctx_gqa",
    )(sink, p, p, p)


def _win_kernel(sink_ref, q_ref, kp_ref, kc_ref, kn_ref, vp_ref, vc_ref, vn_ref, ck_ref, cv_ref, o_ref):
    i = pl.program_id(1)
    lo = _lane_lo()
    k = jnp.concatenate([kp_ref[...], kc_ref[...], kn_ref[...]], axis=0)
    v = jnp.concatenate([vp_ref[...], vc_ref[...], vn_ref[...]], axis=0)
    ck = ck_ref[...]
    cv = cv_ref[...]
    group = A_HEADS // A_KV_HEADS
    n_keys = WIN_Q + 2 * A_WINDOW
    qpos = i * WIN_Q + lax.broadcasted_iota(jnp.int32, (WIN_Q, n_keys), 0)
    kpos = i * WIN_Q - A_WINDOW + lax.broadcasted_iota(jnp.int32, (WIN_Q, n_keys), 1)
    valid = (jnp.abs(kpos - qpos) <= A_WINDOW) & (kpos >= 0) & (kpos < DEC_SEQ)
    valid = jnp.concatenate([valid] * group, axis=0)
    s_loc, s_ctx, values = [], [], []
    for j in range(A_KV_HEADS):
        heads = list(range(group * j, group * (j + 1)))
        kd = _dup_half(k, j, lo).astype(BF16)
        ckd = _dup_half(ck, j, lo).astype(BF16)
        values.append((_dup_half(v, j, lo).astype(BF16), _dup_half(cv, j, lo).astype(BF16)))
        q = _stack_heads(q_ref, heads, lo, HD ** -0.5)
        s_loc.append(jnp.where(valid, lax.dot_general(q, kd, NT_DIMS, preferred_element_type=F32), NEG))
        s_ctx.append(lax.dot_general(q, ckd, NT_DIMS, preferred_element_type=F32))
    s_loc = jnp.concatenate(s_loc, axis=0)
    s_ctx = jnp.concatenate(s_ctx, axis=0)
    sink = _sink_column(sink_ref, list(range(A_HEADS)), WIN_Q)
    mx = jnp.maximum(jnp.maximum(s_loc.max(axis=-1, keepdims=True), s_ctx.max(axis=-1, keepdims=True)), sink)
    p_loc = jnp.exp(s_loc - mx)
    p_ctx = jnp.exp(s_ctx - mx)
    den = p_loc.sum(axis=-1, keepdims=True) + p_ctx.sum(axis=-1, keepdims=True) + jnp.exp(sink - mx)
    p_loc = p_loc.astype(BF16)
    p_ctx = p_ctx.astype(BF16)
    rows_per_group = group * WIN_Q
    for j, (vd, cvd) in enumerate(values):
        rows = slice(j * rows_per_group, (j + 1) * rows_per_group)
        o = (jnp.dot(p_loc[rows], vd, preferred_element_type=F32)
             + jnp.dot(p_ctx[rows], cvd, preferred_element_type=F32)) / den[rows]
        for t in range(group // 2):
            pair = (group * j) // 2 + t
            o_ref[:, pair * LANES:(pair + 1) * LANES] = jnp.where(
                lo, o[(2 * t) * WIN_Q:(2 * t + 1) * WIN_Q], o[(2 * t + 1) * WIN_Q:(2 * t + 2) * WIN_Q])


WIN_Q = 256


def _win_attention(p, cache_k, cache_v, sink):
    nblk = DEC_SEQ // WIN_Q
    side = WIN_Q // A_WINDOW
    nside = DEC_SEQ // A_WINDOW
    base = NP_TOK // WIN_Q
    side_base = NP_TOK // A_WINDOW
    kcol = A_Q // A_KV

    def main_spec(col):
        return pl.BlockSpec((WIN_Q, A_KV), lambda b, i, s: (base + b * nblk + i, col))

    def side_spec(col, off):
        return pl.BlockSpec((A_WINDOW, A_KV),
                            lambda b, i, s: (side_base + b * nside + jnp.clip(side * i + off, 0, nside - 1), col))

    ctx_spec = pl.BlockSpec((None, PAST, A_KV), lambda b, i, s: (b, 0, 0))
    return pl.pallas_call(
        _win_kernel,
        grid_spec=pltpu.PrefetchScalarGridSpec(
            num_scalar_prefetch=1,
            grid=(DEC_BATCH, nblk),
            in_specs=[pl.BlockSpec((WIN_Q, A_Q), lambda b, i, s: (base + b * nblk + i, 0)),
                      side_spec(kcol, -1), main_spec(kcol), side_spec(kcol, side),
                      side_spec(kcol + 1, -1), main_spec(kcol + 1), side_spec(kcol + 1, side),
                      ctx_spec, ctx_spec],
            out_specs=pl.BlockSpec((WIN_Q, A_Q), lambda b, i, s: (b * nblk + i, 0))),
        out_shape=jax.ShapeDtypeStruct((NS_TOK, A_Q), F32),
        compiler_params=_cparams(2),
        name="win_attn",
    )(sink, p, p, p, p, p, p, p, cache_k.reshape(DEC_BATCH, PAST, A_KV), cache_v.reshape(DEC_BATCH, PAST, A_KV))


def _ret_kernel(df_ref, db_ref, q_ref, k_ref, v_ref, g_ref, gn_ref, s0f_ref, s0b_ref,
                o_ref, sf_ref, sb_ref, of_scr, ob_scr, *, length):
    c_len = RET_CHUNK
    n = length // c_len
    lo = _lane_lo()
    hi = jnp.logical_not(lo)
    row = lax.broadcasted_iota(jnp.int32, (c_len, c_len), 0)
    col = lax.broadcasted_iota(jnp.int32, (c_len, c_len), 1)
    rowp = lax.broadcasted_iota(jnp.int32, (LANES, LANES), 0)
    colp = lax.broadcasted_iota(jnp.int32, (LANES, LANES), 1)
    blockdiag = (rowp < HD) == (colp < HD)
    idx = lax.broadcasted_iota(jnp.int32, (c_len, 1), 0).astype(F32)

    def direction(dec_ref, forward):
        lg = -jnp.exp(dec_ref[...])
        diff = (row - col) if forward else (col - row)
        keep = (diff >= 0) if forward else (diff > 0)
        dist = jnp.maximum(diff, 0).astype(F32)
        dm = jnp.concatenate([jnp.where(keep, jnp.exp(dist * lg[:, off:off + 1]), 0.0) for off in (0, HD)], axis=0)
        if forward:
            xi = jnp.exp((idx + 1.0) * lg)
            zeta = jnp.exp((c_len - 1.0 - idx) * lg)
        else:
            xi = jnp.exp((c_len - idx) * lg)
            zeta = jnp.exp(idx * lg)
        return dm, xi, zeta, jnp.exp(c_len * lg)

    def chunk(c, state, consts):
        dm, xi, zeta, gch = consts
        rows = pl.ds(pl.multiple_of(c * c_len, c_len), c_len)
        qc = q_ref[rows, :]
        kc = k_ref[rows, :] * HD ** -0.5
        vc = v_ref[rows, :].astype(BF16)
        kb = kc.astype(BF16)
        q2 = jnp.concatenate([jnp.where(lo, qc, 0.0), jnp.where(hi, qc, 0.0)], axis=0).astype(BF16)
        inner = lax.dot_general(q2, kb, NT_DIMS, preferred_element_type=F32) * dm
        kz_t = (kc * zeta).T
        res = jnp.dot(jnp.concatenate([inner, kz_t], axis=0).astype(BF16), vc, preferred_element_type=F32)
        cross = jnp.dot(qc.astype(BF16), state.astype(BF16), preferred_element_type=F32) * xi
        o = jnp.where(lo, res[:c_len], res[c_len:2 * c_len]) + cross
        state = gch * state + jnp.where(blockdiag, res[2 * c_len:], 0.0)
        return rows, o, state

    cf = direction(df_ref, True)
    cb = direction(db_ref, False)

    def scan_body(t, states):
        rows_f, o_f, state_f = chunk(t, states[0], cf)
        of_scr[rows_f, :] = o_f
        rows_b, o_b, state_b = chunk(n - 1 - t, states[1], cb)
        ob_scr[rows_b, :] = o_b
        return state_f, state_b

    state_f, state_b = lax.fori_loop(0, n, scan_body, (s0f_ref[...], s0b_ref[...]), unroll=min(n, RET_UNROLL))
    sf_ref[...] = state_f
    sb_ref[...] = state_b

    gn = gn_ref[...]
    n_norm = length // RET_NORM_ROWS

    def per_head(x):
        a = jnp.where(lo, x, 0.0).sum(axis=-1, keepdims=True)
        b = jnp.where(hi, x, 0.0).sum(axis=-1, keepdims=True)
        return jnp.where(lo, a, b) * (1.0 / HD)

    def norm_body(t, carry):
        rows = pl.ds(pl.multiple_of(t * RET_NORM_ROWS, RET_NORM_ROWS), RET_NORM_ROWS)
        o = of_scr[rows, :] + ob_scr[rows, :]
        d = o - per_head(o)
        y = d * lax.rsqrt(per_head(d * d) + EPS) * gn
        o_ref[rows, :] = _silu(g_ref[rows, :]) * y
        return carry

    lax.fori_loop(0, n_norm, norm_body, 0)


def _pair_lanes(v):
    return jnp.repeat(v.astype(F32), HD).reshape(B_HEADS // 2, 1, LANES)


def _blockdiag_states(s):
    b = s.shape[0]
    s = s.astype(F32).reshape(b, B_HEADS // 2, 2, HD, HD)
    z = jnp.zeros_like(s[:, :, 0])
    top = jnp.concatenate([s[:, :, 0], z], axis=-1)
    bot = jnp.concatenate([z, s[:, :, 1]], axis=-1)
    return jnp.concatenate([top, bot], axis=-2)


def _diag_states(sp):
    b = sp.shape[0]
    s = jnp.stack([sp[:, :, :HD, :HD], sp[:, :, HD:, HD:]], axis=2)
    return s.reshape(b, B_HEADS, HD, HD)


def _retention(p, row_base, batch, length, dec_f, dec_b, gn_g, s0f, s0b):
    npairs = B_HEADS // 2
    blk0 = row_base // length
    qcol = (A_Q + 2 * A_KV) // LANES

    def col_spec(off):
        return pl.BlockSpec((length, LANES), lambda b, h: (blk0 + b, qcol + off * npairs + h))

    lane_spec = pl.BlockSpec((None, 1, LANES), lambda b, h: (h, 0, 0))
    state_spec = pl.BlockSpec((None, None, LANES, LANES), lambda b, h: (b, h, 0, 0))
    state_shape = jax.ShapeDtypeStruct((batch, npairs, LANES, LANES), F32)
    return pl.pallas_call(
        functools.partial(_ret_kernel, length=length),
        grid=(batch, npairs),
        in_specs=[lane_spec, lane_spec, col_spec(0), col_spec(1), col_spec(2), col_spec(3), lane_spec,
                  state_spec, state_spec],
        out_specs=[pl.BlockSpec((length, LANES), lambda b, h: (b, h)), state_spec, state_spec],
        out_shape=[jax.ShapeDtypeStruct((batch * length, B_W), F32), state_shape, state_shape],
        scratch_shapes=[pltpu.VMEM((length, LANES), F32), pltpu.VMEM((length, LANES), F32)],
        compiler_params=_cparams(2),
        name="retention",
    )(_pair_lanes(dec_f), _pair_lanes(dec_b), p, p, p, p, gn_g.reshape(npairs, 1, LANES), s0f, s0b)


def _ctx_mha_kernel(q_ref, k_ref, v_ref, o_ref):
    lo = _lane_lo()
    for pair in range(C_HEADS // 2):
        cols = slice(pair * LANES, (pair + 1) * LANES)
        q = _stack_heads(q_ref, [2 * pair, 2 * pair + 1], lo, HD ** -0.5)
        s = lax.dot_general(q, k_ref[:, cols].astype(BF16), NT_DIMS, preferred_element_type=F32)
        o = _softmax_av([s], [v_ref[:, cols].astype(BF16)])
        o_ref[:, cols] = jnp.where(lo, o[:SEQ], o[SEQ:])


def _ctx_mha(p):
    return pl.pallas_call(
        _ctx_mha_kernel,
        grid=(BATCH,),
        in_specs=[pl.BlockSpec((SEQ, C_W), lambda b: (b, 0)),
                  pl.BlockSpec((SEQ, C_W), lambda b: (b, 1)),
                  pl.BlockSpec((SEQ, C_W), lambda b: (b, 2))],
        out_specs=pl.BlockSpec((SEQ, C_W), lambda b: (b, 0)),
        out_shape=jax.ShapeDtypeStruct((NP_TOK, C_W), F32),
        compiler_params=_cparams(1),
        name="ctx_mha",
    )(p, p, p)


NA_WIN_ROWS = 2 * NA_ROWS
NA_WIN = NA_WIN_ROWS * GRID_W
NA_QROWS = NA_ROWS * GRID_W
NA_PAD_ROWS = NA_KH // 2
NA_TABLE = 1536


def _na_kernel(q_ref, kp_ref, km_ref, kn_ref, vp_ref, vm_ref, vn_ref, ck_ref, cv_ref, ue_ref, uo_ref, o_ref):
    r0 = pl.program_id(2) * NA_ROWS
    n_rows = DEC_SEQ // GRID_W
    lo = _lane_lo()
    k = jnp.concatenate([kp_ref[...], km_ref[...], kn_ref[...]], axis=0).astype(BF16)
    v = jnp.concatenate([vp_ref[...], vm_ref[...], vn_ref[...]], axis=0).astype(BF16)
    ck = ck_ref[...].astype(BF16)
    cv = cv_ref[...].astype(BF16)
    q = q_ref[...] * HD ** -0.5
    klane = lax.broadcasted_iota(jnp.int32, (1, NA_WIN), 1)
    outs = []
    for half, keep in enumerate((lo, jnp.logical_not(lo))):
        qh = jnp.where(keep, q, 0.0).astype(BF16)
        s = lax.dot_general(qh, k, NT_DIMS, preferred_element_type=F32)
        s_ctx = lax.dot_general(qh, ck, NT_DIMS, preferred_element_type=F32)
        p_loc, p_ctx, den = [], [], []
        for rq in range(NA_ROWS):
            rows = slice(rq * GRID_W, (rq + 1) * GRID_W)
            start = NA_KH - 1 - rq
            if start % 2 == 0:
                u = ue_ref[half, :, start * GRID_W:start * GRID_W + NA_WIN]
            else:
                u = uo_ref[half, :, (start - 1) * GRID_W:(start - 1) * GRID_W + NA_WIN]
            r = r0 + rq
            first = jnp.clip(r - NA_KH // 2, 0, n_rows - NA_KH)
            lane0 = (first - r0 + NA_PAD_ROWS) * GRID_W
            in_rows = (klane >= lane0) & (klane < lane0 + NA_KH * GRID_W)
            sl = jnp.where(in_rows, s[rows] + u, NEG)
            sc = s_ctx[rows]
            mx = jnp.maximum(sl.max(axis=-1, keepdims=True), sc.max(axis=-1, keepdims=True))
            el = jnp.exp(sl - mx)
            ec = jnp.exp(sc - mx)
            den.append(el.sum(axis=-1, keepdims=True) + ec.sum(axis=-1, keepdims=True))
            p_loc.append(el.astype(BF16))
            p_ctx.append(ec.astype(BF16))
        acc = (jnp.dot(jnp.concatenate(p_loc, axis=0), v, preferred_element_type=F32)
               + jnp.dot(jnp.concatenate(p_ctx, axis=0), cv, preferred_element_type=F32))
        outs.append(acc / jnp.concatenate(den, axis=0))
    o_ref[...] = jnp.where(lo, outs[0], outs[1])


def _na_bias_tables(rpb):
    cq = jnp.arange(GRID_W)
    ck = jnp.arange(GRID_W)
    dc = jnp.clip(ck[None] - cq[:, None], -(NA_KW - 1), NA_KW - 1) + NA_KW - 1
    cs = jnp.clip(cq - NA_KW // 2, 0, GRID_W - NA_KW)
    col_ok = (ck[None] >= cs[:, None]) & (ck[None] < cs[:, None] + NA_KW)
    t = rpb.astype(F32)[:, :, dc]
    t = jnp.where(col_ok[None, None], t, NEG).transpose(0, 2, 1, 3)
    n_dr = 2 * NA_KH - 1
    blocks = NA_TABLE // GRID_W
    t = jnp.pad(t, ((0, 0), (0, 0), (NA_PAD_ROWS, blocks - n_dr - NA_PAD_ROWS), (0, 0)), constant_values=NEG)
    ue = t.reshape(C_HEADS, GRID_W, NA_TABLE)
    uo = jnp.concatenate([ue[..., GRID_W:], jnp.full((C_HEADS, GRID_W, GRID_W), NEG, F32)], axis=-1)
    return ue, uo


def _na_attention(p, cache_k, cache_v, rpb):
    npairs = C_HEADS // 2
    nrb = DEC_SEQ // NA_QROWS
    half = NA_QROWS // 2
    qbase = NP_TOK // NA_QROWS
    hbase = NP_TOK // half
    kcol = C_W // LANES
    ue, uo = _na_bias_tables(rpb)

    def main_spec(col0):
        return pl.BlockSpec((NA_QROWS, LANES), lambda b, h, r: (qbase + b * nrb + r, col0 + h))

    def side_spec(col0, off):
        return pl.BlockSpec((half, LANES),
                            lambda b, h, r: (hbase + b * 2 * nrb + jnp.clip(2 * r + off, 0, 2 * nrb - 1), col0 + h))

    ctx_spec = pl.BlockSpec((None, PAST, LANES), lambda b, h, r: (b, 0, h))
    tab_spec = pl.BlockSpec((2, GRID_W, NA_TABLE), lambda b, h, r: (h, 0, 0))
    return pl.pallas_call(
        _na_kernel,
        grid=(DEC_BATCH, npairs, nrb),
        in_specs=[main_spec(0),
                  side_spec(kcol, -1), main_spec(kcol), side_spec(kcol, 2),
                  side_spec(2 * kcol, -1), main_spec(2 * kcol), side_spec(2 * kcol, 2),
                  ctx_spec, ctx_spec, tab_spec, tab_spec],
        out_specs=pl.BlockSpec((NA_QROWS, LANES), lambda b, h, r: (b * nrb + r, h)),
        out_shape=jax.ShapeDtypeStruct((NS_TOK, C_W), F32),
        compiler_params=_cparams(3),
        name="na_attn",
    )(p, p, p, p, p, p, p, cache_k.reshape(DEC_BATCH, PAST, C_W), cache_v.reshape(DEC_BATCH, PAST, C_W), ue, uo)


def _route(biased, scores):
    t = biased.shape[1]
    per_group = N_EXPERTS // N_GROUPS
    i8 = lax.broadcasted_iota(jnp.int32, (per_group, t), 0)
    g_rows = []
    for g in range(N_GROUPS):
        bg = biased[g * per_group:(g + 1) * per_group]
        m1 = bg.max(axis=0, keepdims=True)
        first = jnp.where(bg == m1, i8, per_group).min(axis=0, keepdims=True)
        m2 = jnp.where(i8 == first, -jnp.inf, bg).max(axis=0, keepdims=True)
        g_rows.append(m1 + m2)
    g_top = jnp.concatenate(g_rows, axis=0)
    gi = lax.broadcasted_iota(jnp.int32, g_top.shape, 0)
    g_sel = jnp.zeros(g_top.shape, jnp.int32)
    cur = g_top
    for _ in range(TOPK_GROUPS):
        m = cur.max(axis=0, keepdims=True)
        hit = gi == jnp.where(cur == m, gi, N_GROUPS).min(axis=0, keepdims=True)
        g_sel = jnp.where(hit, 1, g_sel)
        cur = jnp.where(hit, -jnp.inf, cur)
    e_sel = jnp.concatenate([jnp.broadcast_to(g_sel[g:g + 1], (per_group, t)) for g in range(N_GROUPS)], axis=0)
    cur = jnp.where(e_sel > 0, biased, NEG)
    ei = lax.broadcasted_iota(jnp.int32, cur.shape, 0)
    ids, gates, hits = [], [], []
    for _ in range(TOP_K):
        m = cur.max(axis=0, keepdims=True)
        f = jnp.where(cur == m, ei, N_EXPERTS).min(axis=0, keepdims=True)
        hit = ei == f
        ids.append(f)
        hits.append(hit)
        gates.append(jnp.where(hit, scores, 0.0).sum(axis=0, keepdims=True))
        cur = jnp.where(hit, -jnp.inf, cur)
    gate = jnp.concatenate(gates, axis=0)
    gate = gate / gate.sum(axis=0, keepdims=True) * ROUTED_SCALE
    return jnp.concatenate(ids, axis=0), gate, hits


def _pack_bf16_pairs(h):
    bits = lax.bitcast_convert_type(h.astype(BF16).astype(F32), jnp.uint32)
    return bits[:, :D // 2] | (bits[:, D // 2:] >> 16)


def _unpack_bf16_pairs(xp):
    hi = lax.bitcast_convert_type(xp & jnp.uint32(0xFFFF0000), F32).astype(BF16)
    lo = lax.bitcast_convert_type(xp << 16, F32).astype(BF16)
    return hi, lo


def _dot_halves(hi, lo, w_ref):
    return (jnp.dot(hi, w_ref[:D // 2, :], preferred_element_type=F32)
            + jnp.dot(lo, w_ref[D // 2:, :], preferred_element_type=F32))


def _outproj_kernel(*refs, n_parts):
    xp_ref, xs_ref = refs[:2]
    part_refs = refs[2:2 + 3 * n_parts]
    gate_ref, shift_ref, scale_ref, g2_ref, rw_ref, rb_ref = refs[2 + 3 * n_parts:8 + 3 * n_parts]
    xo_ref, h_ref, dest_ref, wgt_ref, plan_ref, cnt_ref = refs[8 + 3 * n_parts:]
    step = pl.program_id(0)

    @pl.when(step == 0)
    def _():
        cnt_ref[...] = jnp.zeros_like(cnt_ref)
        plan_ref[...] = jnp.zeros_like(plan_ref)

    y = None
    for t in range(n_parts):
        ap_ref, as_ref, w_ref = part_refs[3 * t:3 * t + 3]
        d = jnp.dot(_pick_rows(ap_ref, as_ref).astype(BF16), w_ref[...], preferred_element_type=F32)
        y = d if y is None else y + d
    x = _pick_rows(xp_ref, xs_ref) + gate_ref[...] * y
    xo_ref[...] = x
    h = _rms(x, g2_ref[...]) * (1.0 + scale_ref[...]) + shift_ref[...]
    h_ref[...] = _pack_bf16_pairs(h)
    logits = lax.dot_general(rw_ref[...], h, NT_DIMS, preferred_element_type=F32, precision=HIGHEST)
    scores = jax.nn.sigmoid(logits)
    _, gate, hits = _route(scores + rb_ref[...], scores)
    wgt_ref[...] = gate
    chosen = hits[0]
    for hit in hits[1:]:
        chosen = chosen | hit
    m = jnp.where(chosen, 1.0, 0.0)
    before = (lax.broadcasted_iota(jnp.int32, (TM, TM), 0) < lax.broadcasted_iota(jnp.int32, (TM, TM), 1))
    prefix = jnp.dot(m.astype(BF16), jnp.where(before, 1.0, 0.0).astype(BF16), preferred_element_type=F32)
    e_base = (lax.broadcasted_iota(jnp.int32, (N_EXPERTS, 1), 0) * N_TOK).astype(F32)
    row_all = prefix + (cnt_ref[...] + e_base)
    dest_ref[...] = jnp.concatenate(
        [jnp.where(hit, row_all, 0.0).sum(axis=0, keepdims=True) for hit in hits], axis=0).astype(jnp.int32)
    cnt_ref[...] += m.sum(axis=1, keepdims=True)

    @pl.when(step == pl.num_programs(0) - 1)
    def _():
        _block_plan(cnt_ref[...], plan_ref)


def _block_plan(counts, plan_ref):
    cap_blocks = N_TOK // MOE_BLOCK
    nblk = ((counts.astype(jnp.int32) + (MOE_BLOCK - 1)) // MOE_BLOCK).astype(F32)
    lower = (lax.broadcasted_iota(jnp.int32, (N_EXPERTS, N_EXPERTS), 0)
             >= lax.broadcasted_iota(jnp.int32, (N_EXPERTS, N_EXPERTS), 1))
    cum = jnp.dot(jnp.where(lower, 1.0, 0.0).astype(BF16), jnp.broadcast_to(nblk, (N_EXPERTS, LANES)).astype(BF16),
                  preferred_element_type=F32)[:, :1]
    n_used = cum[N_EXPERTS - 1:, :]
    slot = jnp.minimum(lax.broadcasted_iota(jnp.int32, (1, PLAN_LANES), 1).astype(F32), n_used - 1.0)
    done = cum <= slot
    expert = jnp.where(done, 1.0, 0.0).sum(axis=0, keepdims=True)
    blocks_before = jnp.where(done, nblk, 0.0).sum(axis=0, keepdims=True)
    plan_ref[0:1, :] = (expert * cap_blocks + (slot - blocks_before)).astype(jnp.int32)
    plan_ref[1:2, :] = expert.astype(jnp.int32)
    plan_ref[2:3, :] = jnp.broadcast_to(n_used, (1, PLAN_LANES)).astype(jnp.int32)
    current = (cum > slot) & (cum - nblk <= slot)
    rows_left = jnp.where(current, counts, 0.0).sum(axis=0, keepdims=True) - (slot - blocks_before) * MOE_BLOCK
    plan_ref[3:4, :] = rows_left.astype(jnp.int32)


def _outproj(x, parts, mod, g2, router_w, router_b):
    in_specs = _pair_specs(D)
    args = [x[0], x[1]]
    for ap, a_s, w in parts:
        width = ap.shape[1]
        in_specs += _pair_specs(width) + [pl.BlockSpec((width, D), lambda i: (0, 0))]
        args += [ap, a_s, w]
    in_specs += [_mod_spec(2), _mod_spec(3), _mod_spec(4),
                 pl.BlockSpec((1, D), lambda i: (0, 0)),
                 pl.BlockSpec((N_EXPERTS, D), lambda i: (0, 0)),
                 pl.BlockSpec((N_EXPERTS, 1), lambda i: (0, 0))]
    args += [mod, mod, mod, g2.reshape(1, D), router_w.T, router_b.reshape(N_EXPERTS, 1)]
    return pl.pallas_call(
        functools.partial(_outproj_kernel, n_parts=len(parts)),
        grid=(N_TOK // TM,),
        in_specs=in_specs,
        out_specs=[pl.BlockSpec((TM, D), lambda i: (i, 0)),
                   pl.BlockSpec((TM, D // 2), lambda i: (i, 0)),
                   pl.BlockSpec((TOP_K, TM), lambda i: (0, i)),
                   pl.BlockSpec((TOP_K, TM), lambda i: (0, i)),
                   pl.BlockSpec((8, PLAN_LANES), lambda i: (0, 0))],
        out_shape=[jax.ShapeDtypeStruct((N_TOK, D), F32),
                   jax.ShapeDtypeStruct((N_TOK, D // 2), jnp.uint32),
                   jax.ShapeDtypeStruct((TOP_K, N_TOK), jnp.int32),
                   jax.ShapeDtypeStruct((TOP_K, N_TOK), F32),
                   jax.ShapeDtypeStruct((8, PLAN_LANES), jnp.int32)],
        scratch_shapes=[pltpu.VMEM((N_EXPERTS, 1), F32)],
        compiler_params=_cparams(1),
        name="outproj_router",
    )(*args)


def _experts_kernel(br_ref, be_ref, nu_ref, nr_ref, x_ref, w1_ref, w3_ref, w2_ref, o_ref, w1b, w3b, w2b):
    i = pl.program_id(0)
    e = be_ref[i]
    prev = be_ref[jnp.maximum(i - 1, 0)]

    @pl.when((i == 0) | (e != prev))
    def _():
        w1b[...] = w1_ref[...].astype(BF16)
        w3b[...] = w3_ref[...].astype(BF16)
        w2b[...] = w2_ref[...].astype(BF16)

    def swiglu(rows):
        hi, lo = _unpack_bf16_pairs(x_ref[rows, :])
        a = _dot_halves(hi, lo, w1b)
        b = _dot_halves(hi, lo, w3b)
        h = (_silu(a) * b).astype(BF16)
        o_ref[rows, :] = _pack_bf16_pairs(jnp.dot(h, w2b[...], preferred_element_type=F32))

    live = i < nu_ref[0]
    half = MOE_BLOCK // 2

    @pl.when(live)
    def _():
        swiglu(pl.ds(0, half))

    @pl.when(live & (nr_ref[i] > half))
    def _():
        swiglu(pl.ds(half, half))


def _experts(plan, x_rows, w1, w3, w2, layer):
    return pl.pallas_call(
        _experts_kernel,
        grid_spec=pltpu.PrefetchScalarGridSpec(
            num_scalar_prefetch=4,
            grid=(N_MOE_BLOCKS,),
            in_specs=[pl.BlockSpec((MOE_BLOCK, D // 2), lambda i, br, be, nu, nr: (br[i], 0)),
                      pl.BlockSpec((None, None, D, FF), lambda i, br, be, nu, nr: (layer, be[i], 0, 0)),
                      pl.BlockSpec((None, None, D, FF), lambda i, br, be, nu, nr: (layer, be[i], 0, 0)),
                      pl.BlockSpec((None, None, FF, D), lambda i, br, be, nu, nr: (layer, be[i], 0, 0))],
            out_specs=pl.BlockSpec((MOE_BLOCK, D // 2), lambda i, br, be, nu, nr: (br[i], 0)),
            scratch_shapes=[pltpu.VMEM((D, FF), BF16), pltpu.VMEM((D, FF), BF16), pltpu.VMEM((FF, D), BF16)]),
        out_shape=jax.ShapeDtypeStruct(x_rows.shape, jnp.uint32),
        compiler_params=_cparams(1),
        name="experts",
    )(plan[0], plan[1], plan[2, :1], plan[3], x_rows, w1, w3, w2)


SC_CORES = 2
SC_SUBCORES = 16
SC_WORKERS = SC_CORES * SC_SUBCORES
SC_CHUNK_BYTES = 64 * 1024
SC_SLOTS = 4


def _sc_scatter(rows, dest, n_out):
    n_rows, width = rows.shape
    picks = dest.shape[0]
    chunk = SC_CHUNK_BYTES // (4 * width)
    per_worker = n_rows // SC_WORKERS
    n_chunks = per_worker // chunk
    assert per_worker * SC_WORKERS == n_rows and n_chunks * chunk == per_worker and n_chunks % 2 == 0
    mesh = plsc.VectorSubcoreMesh(core_axis_name="c", subcore_axis_name="s")

    @functools.partial(
        pl.kernel, mesh=mesh,
        out_type=jax.ShapeDtypeStruct((n_out, width), rows.dtype),
        scratch_types=[pltpu.VMEM((picks, n_chunks, chunk), jnp.int32),
                       pltpu.VMEM((2, chunk, width), rows.dtype),
                       pltpu.SemaphoreType.DMA((2,)),
                       pltpu.SemaphoreType.DMA((2,))])
    def scatter(r_hbm, d_hbm, o_hbm, idx_v, rows_v, lsem, ssem):
        worker = lax.axis_index("s") * SC_CORES + lax.axis_index("c")
        base = worker * per_worker
        for k in range(picks):
            pltpu.sync_copy(d_hbm.at[k, worker], idx_v.at[k])

        def load_copy(c, b):
            src = pl.ds(pl.multiple_of(base + c * chunk, chunk), chunk)
            return pltpu.make_async_copy(r_hbm.at[src], rows_v.at[b], lsem.at[b])

        def store_copy(c, b, k):
            return pltpu.make_async_copy(rows_v.at[b], o_hbm.at[idx_v.at[k, c]], ssem.at[b])

        load_copy(0, 0).start()

        @pl.loop(0, n_chunks, step=2)
        def _(c0):
            for b in range(2):
                c = c0 + b
                load_copy(c, b).wait()
                for k in range(picks):
                    store_copy(c, b, k).start()

                @pl.when(c > 0)
                def _():
                    for k in range(picks):
                        store_copy(c - 1, 1 - b, k).wait()

                @pl.when(c + 1 < n_chunks)
                def _():
                    load_copy(c + 1, 1 - b).start()

        for k in range(picks):
            store_copy(n_chunks - 1, 1, k).wait()

    return scatter(rows, dest.reshape(picks, SC_WORKERS, n_chunks, chunk))


def _sc_gather(table, idx):
    n_idx = idx.shape[0]
    width = table.shape[1]
    chunk = SC_CHUNK_BYTES // (4 * width)
    per_worker = n_idx // SC_WORKERS
    n_chunks = per_worker // chunk
    ahead = SC_SLOTS - 1
    assert per_worker * SC_WORKERS == n_idx and n_chunks * chunk == per_worker and n_chunks % SC_SLOTS == 0
    mesh = plsc.VectorSubcoreMesh(core_axis_name="c", subcore_axis_name="s")

    @functools.partial(
        pl.kernel, mesh=mesh,
        out_type=jax.ShapeDtypeStruct((n_idx, width), table.dtype),
        scratch_types=[pltpu.VMEM((per_worker,), jnp.int32),
                       pltpu.VMEM((SC_SLOTS, chunk, width), table.dtype),
                       pltpu.SemaphoreType.DMA((SC_SLOTS,)),
                       pltpu.SemaphoreType.DMA((SC_SLOTS,))])
    def gather(t_hbm, i_hbm, o_hbm, idx_v, rows_v, gsem, wsem):
        worker = lax.axis_index("s") * SC_CORES + lax.axis_index("c")
        base = worker * per_worker
        pltpu.sync_copy(i_hbm.at[pl.ds(pl.multiple_of(base, chunk), per_worker)], idx_v)

        def gather_copy(c, b):
            ids = idx_v.at[pl.ds(pl.multiple_of(c * chunk, chunk), chunk)]
            return pltpu.make_async_copy(t_hbm.at[ids], rows_v.at[b], gsem.at[b])

        def write_copy(c, b):
            rows = pl.ds(pl.multiple_of(base + c * chunk, chunk), chunk)
            return pltpu.make_async_copy(rows_v.at[b], o_hbm.at[rows], wsem.at[b])

        for c in range(ahead):
            gather_copy(c, c).start()

        @pl.loop(0, n_chunks, step=SC_SLOTS)
        def _(c0):
            for b in range(SC_SLOTS):
                c = c0 + b
                refill = (b + ahead) % SC_SLOTS
                gather_copy(c, b).wait()
                write_copy(c, b).start()

                @pl.when(c > 0)
                def _():
                    write_copy(c - 1, refill).wait()

                @pl.when(c + ahead < n_chunks)
                def _():
                    gather_copy(c + ahead, refill).start()

        write_copy(n_chunks - 1, (n_chunks - 1) % SC_SLOTS).wait()

    return gather(table, idx)


TC = 512


def _combine_kernel(x_ref, h_ref, y_ref, wgt_ref, gate_ref, w1_ref, w3_ref, w2_ref, fg_ref, o_ref, *, final):
    hi, lo = _unpack_bf16_pairs(h_ref[...])
    a = _dot_halves(hi, lo, w1_ref)
    b = _dot_halves(hi, lo, w3_ref)
    ffn = jnp.dot((_silu(a) * b).astype(BF16), w2_ref[...], preferred_element_type=F32)
    wgt = wgt_ref[...]
    r_hi = None
    r_lo = None
    for k in range(TOP_K):
        yk = y_ref[k]
        w = wgt[:, k:k + 1]
        t_hi = lax.bitcast_convert_type(yk & jnp.uint32(0xFFFF0000), F32) * w
        t_lo = lax.bitcast_convert_type(yk << 16, F32) * w
        r_hi = t_hi if r_hi is None else r_hi + t_hi
        r_lo = t_lo if r_lo is None else r_lo + t_lo
    x = x_ref[...] + gate_ref[...] * (ffn + jnp.concatenate([r_hi, r_lo], axis=1))
    o_ref[...] = _rms(x, fg_ref[...]) if final else x


def _combine(x, h, y_rows, wgt, mod, sw1, sw3, sw2, final_g, final):
    weights = (sw1.astype(BF16), sw3.astype(BF16), sw2.astype(BF16), final_g.reshape(1, D))

    def rows_from(first_row, n_rows):
        b0 = first_row // TC
        return pl.pallas_call(
            functools.partial(_combine_kernel, final=final),
            grid=(n_rows // TC,),
            in_specs=[pl.BlockSpec((TC, D), lambda i: (i + b0, 0)),
                      pl.BlockSpec((TC, D // 2), lambda i: (i + b0, 0)),
                      pl.BlockSpec((TOP_K, TC, D // 2), lambda i: (0, i + b0, 0)),
                      pl.BlockSpec((TC, TOP_K), lambda i: (i + b0, 0)),
                      _mod_spec(5, TC, b0),
                      pl.BlockSpec((D, FF), lambda i: (0, 0)),
                      pl.BlockSpec((D, FF), lambda i: (0, 0)),
                      pl.BlockSpec((FF, D), lambda i: (0, 0)),
                      pl.BlockSpec((1, D), lambda i: (0, 0))],
            out_specs=pl.BlockSpec((TC, D), lambda i: (i, 0)),
            out_shape=jax.ShapeDtypeStruct((n_rows, D), F32),
            compiler_params=_cparams(1),
            name="combine",
        )(x, h, y_rows, wgt, mod, *weights)

    return rows_from(0, NP_TOK), rows_from(NP_TOK, NS_TOK)


def kernel(x_prompt, x_sample, cache_a_k, cache_a_v, state_ret_fwd, state_ret_bwd, cache_c_k, cache_c_v,
           c, c_ctx, norm1_g, norm2_g, ada_w, ada_b, even_w_in, even_w_out, sink_a, ret_decay_fwd,
           ret_decay_bwd, ret_gn_g, odd_w_in, odd_w_out, na_rpb, router_w, router_b, exp_w1, exp_w3,
           exp_w2, sh_w1, sh_w3, sh_w2, final_g):
    x = (x_prompt.reshape(NP_TOK, D), x_sample.reshape(NS_TOK, D))
    cc = jnp.concatenate([c_ctx[None], c, jnp.zeros((8 - 1 - DEC_BATCH, D), F32)], axis=0)
    rope = _rope_tables()
    outs = {}
    for l in range(2):
        mod = _ada(cc, ada_w, ada_b, l)
        if l == 0:
            p = _inproj(x, norm1_g[l], mod, even_w_in[0].astype(BF16), rope, A_Q + A_KV)
            oa_p = _ctx_gqa(p, sink_a[0])
            oa_s = _win_attention(p, cache_a_k[:, 0], cache_a_v[:, 0], sink_a[0])
            zero = jnp.zeros((BATCH, B_HEADS // 2, LANES, LANES), F32)
            ob_p, sf, sb = _retention(p, 0, BATCH, SEQ, ret_decay_fwd[0], ret_decay_bwd[0], ret_gn_g[0], zero, zero)
            ob_s, _, _ = _retention(p, NP_TOK, DEC_BATCH, DEC_SEQ, ret_decay_fwd[0], ret_decay_bwd[0], ret_gn_g[0],
                                    _blockdiag_states(state_ret_fwd[:, 0]), _blockdiag_states(state_ret_bwd[:, 0]))
            w_out = even_w_out[0].astype(BF16)
            parts = [(oa_p, oa_s, w_out[:A_Q]), (ob_p, ob_s, w_out[A_Q:])]
            outs["a_k"] = p[:NP_TOK, A_Q:A_Q + A_KV].reshape(BATCH, 1, SEQ, A_KV_HEADS, HD)
            outs["a_v"] = p[:NP_TOK, A_Q + A_KV:A_Q + 2 * A_KV].reshape(BATCH, 1, SEQ, A_KV_HEADS, HD)
            outs["r_f"] = _diag_states(sf).reshape(BATCH, 1, B_HEADS, HD, HD)
            outs["r_b"] = _diag_states(sb).reshape(BATCH, 1, B_HEADS, HD, HD)
        else:
            p = _inproj(x, norm1_g[l], mod, odd_w_in[0].astype(BF16), rope, 0)
            o_p = _ctx_mha(p)
            o_s = _na_attention(p, cache_c_k[:, 0], cache_c_v[:, 0], na_rpb[0])
            parts = [(o_p, o_s, odd_w_out[0].astype(BF16))]
            outs["c_k"] = p[:NP_TOK, C_W:2 * C_W].reshape(BATCH, 1, SEQ, C_HEADS, HD)
            outs["c_v"] = p[:NP_TOK, 2 * C_W:3 * C_W].reshape(BATCH, 1, SEQ, C_HEADS, HD)
        x_mid, h, dest, gate_t, plan = _outproj(x, parts, mod, norm2_g[l], router_w[l], router_b[l])
        y = _experts(plan, _sc_scatter(h, dest, N_EXPERTS * N_TOK), exp_w1, exp_w3, exp_w2, l)
        y_rows = _sc_gather(y, dest.reshape(N_ASSIGN)).reshape(TOP_K, N_TOK, D // 2)
        x = _combine(x_mid, h, y_rows, gate_t.T, mod, sh_w1[l], sh_w3[l], sh_w2[l], final_g, final=(l == 1))
    y_prompt = x[0].reshape(BATCH, SEQ, D)
    y_sample = x[1].reshape(DEC_BATCH, DEC_SEQ, D)
    return (y_prompt, y_sample, outs["a_k"], outs["a_v"], outs["r_f"], outs["r_b"], outs["c_k"], outs["c_v"])
```

```python
import functools
import math

import jax
import jax.numpy as jnp
from jax import lax
from jax.experimental import pallas as pl
from jax.experimental.pallas import tpu as pltpu
from jax.experimental.pallas import tpu_sc as plsc

F32 = jnp.float32
BF16 = jnp.bfloat16
HIGHEST = lax.Precision.HIGHEST

D = 1024
BATCH = 32
SEQ = 256
DEC_BATCH = 4
DEC_SEQ = 4096
PAST = 256
GRID_W = 64
HD = 64
EPS = 1e-6
NEG = -1e30
ROPE_BASE = 10000.0
A_HEADS = 8
A_KV_HEADS = 2
A_Q = A_HEADS * HD
A_KV = A_KV_HEADS * HD
B_HEADS = 8
B_W = B_HEADS * HD
EVEN_IN = A_Q + 2 * A_KV + 4 * B_W
C_HEADS = 16
C_W = C_HEADS * HD
NA_KH = 8
NA_KW = 16
N_EXPERTS = 64
TOP_K = 8
N_GROUPS = 8
TOPK_GROUPS = 4
FF = 256
ROUTED_SCALE = 2.5
MOE_BLOCK = 1024
RET_CHUNK = 128
RET_UNROLL = 2
RET_NORM_ROWS = 256
A_WINDOW = 128

NP_TOK = BATCH * SEQ
NS_TOK = DEC_BATCH * DEC_SEQ
N_TOK = NP_TOK + NS_TOK
N_ASSIGN = N_TOK * TOP_K
N_MOE_BLOCKS = (N_ASSIGN + N_EXPERTS * (MOE_BLOCK - 1) + MOE_BLOCK - 1) // MOE_BLOCK
PLAN_LANES = 512
assert N_TOK % MOE_BLOCK == 0 and N_MOE_BLOCKS <= PLAN_LANES

LANES = 128
TM = 512
ROUTE_SPLIT = 2
NA_ROWS = 8
V7X_VMEM_LIMIT = 56 * 1024 * 1024

NT_DIMS = (((1,), (1,)), ((), ()))


def _cparams(n_axes, vmem=V7X_VMEM_LIMIT):
    return pltpu.CompilerParams(dimension_semantics=("arbitrary",) * n_axes, vmem_limit_bytes=vmem)


def _seg_of_block(i, rows):
    row0 = i * rows
    return jnp.where(row0 < NP_TOK, 0, 1 + (row0 - NP_TOK) // DEC_SEQ)


def _mod_spec(chunk, rows=TM, first_block=0):
    return pl.BlockSpec((None, 1, D), lambda i: (_seg_of_block(i + first_block, rows), 0, chunk))


def _pair_specs(width, rows=TM):
    npb = NP_TOK // rows
    nsb = NS_TOK // rows
    return [pl.BlockSpec((rows, width), lambda i: (jnp.minimum(i, npb - 1), 0)),
            pl.BlockSpec((rows, width), lambda i: (jnp.clip(i - npb, 0, nsb - 1), 0))]


def _pick_rows(p_ref, s_ref, rows=TM):
    return jnp.where(pl.program_id(0) < NP_TOK // rows, p_ref[...], s_ref[...])


def _silu(x):
    return x * jax.nn.sigmoid(x)


def _rms(x, g):
    return x * lax.rsqrt(jnp.mean(x * x, axis=-1, keepdims=True) + EPS) * g


def _lane_lo():
    return lax.broadcasted_iota(jnp.int32, (1, LANES), 1) < HD


def _ada_kernel(c_ref, w_ref, b_ref, o_ref):
    a = _silu(c_ref[...])
    o_ref[...] = jnp.dot(a, w_ref[...], preferred_element_type=F32, precision=HIGHEST) + b_ref[...]


def _ada(cc, w, b, layer):
    tn = 1536
    out = pl.pallas_call(
        _ada_kernel,
        grid=(6 * D // tn,),
        in_specs=[pl.BlockSpec((8, D), lambda j: (0, 0)),
                  pl.BlockSpec((None, D, tn), lambda j: (layer, 0, j)),
                  pl.BlockSpec((None, 1, tn), lambda j: (layer, 0, j))],
        out_specs=pl.BlockSpec((8, tn), lambda j: (0, j)),
        out_shape=jax.ShapeDtypeStruct((8, 6 * D), F32),
        compiler_params=_cparams(1),
        name="ada",
    )(cc, w, b.reshape(b.shape[0], 1, 6 * D))
    return out.reshape(8, 1, 6 * D)


def _inproj_kernel(xp_ref, xs_ref, g_ref, shift_ref, scale_ref, w_ref, cos_ref, sin_ref, o_ref, *, rope_cols):
    h = _rms(_pick_rows(xp_ref, xs_ref), g_ref[...]) * (1.0 + scale_ref[...]) + shift_ref[...]
    o = jnp.dot(h.astype(BF16), w_ref[...], preferred_element_type=F32)
    if rope_cols:
        cos = cos_ref[...]
        sin = sin_ref[...]
        lane = lax.broadcasted_iota(jnp.int32, (1, LANES), 1)
        first = (lane % 32) < 16
        for c in range(rope_cols // LANES):
            oc = o[:, c * LANES:(c + 1) * LANES]
            partner = jnp.where(first, pltpu.roll(oc, LANES - 16, 1), pltpu.roll(oc, 16, 1))
            o_ref[:, c * LANES:(c + 1) * LANES] = oc * cos + partner * sin
        o_ref[:, rope_cols:] = o[:, rope_cols:]
    else:
        o_ref[...] = o


def _rope_tables():
    half = HD // 2
    inv = ROPE_BASE ** (-jnp.arange(0, half, 2, dtype=F32) / half)
    t = jnp.arange(DEC_SEQ)
    ang_r = (t // GRID_W).astype(F32)[:, None] * inv[None]
    ang_c = (t % GRID_W).astype(F32)[:, None] * inv[None]

    def head(fn_r, fn_c, sign):
        return jnp.concatenate([sign[0] * fn_r, sign[1] * fn_r, sign[0] * fn_c, sign[1] * fn_c], axis=-1)

    cos = head(jnp.cos(ang_r), jnp.cos(ang_c), (1.0, 1.0))
    sin = head(jnp.sin(ang_r), jnp.sin(ang_c), (-1.0, 1.0))
    cos = jnp.concatenate([jnp.ones((TM, HD), F32), cos], axis=0)
    sin = jnp.concatenate([jnp.zeros((TM, HD), F32), sin], axis=0)
    return jnp.tile(cos, (1, 2)), jnp.tile(sin, (1, 2))


def _inproj(x, g, mod, w_bf16, rope, rope_cols):
    n_out = w_bf16.shape[1]
    npb = NP_TOK // TM
    spb = DEC_SEQ // TM

    def rope_map(i):
        return (jnp.where(i < npb, 0, 1 + (i - npb) % spb), 0)

    return pl.pallas_call(
        functools.partial(_inproj_kernel, rope_cols=rope_cols),
        grid=(N_TOK // TM,),
        in_specs=_pair_specs(D) + [
                  pl.BlockSpec((1, D), lambda i: (0, 0)),
                  _mod_spec(0), _mod_spec(1),
                  pl.BlockSpec((D, n_out), lambda i: (0, 0)),
                  pl.BlockSpec((TM, LANES), rope_map),
                  pl.BlockSpec((TM, LANES), rope_map)],
        out_specs=pl.BlockSpec((TM, n_out), lambda i: (i, 0)),
        out_shape=jax.ShapeDtypeStruct((N_TOK, n_out), F32),
        compiler_params=_cparams(1),
        name="inproj",
    )(x[0], x[1], g.reshape(1, D), mod, mod, w_bf16, rope[0], rope[1])


def _softmax_av(s_list, v_list, sink=None):
    mx = s_list[0].max(axis=-1, keepdims=True)
    for s in s_list[1:]:
        mx = jnp.maximum(mx, s.max(axis=-1, keepdims=True))
    if sink is not None:
        mx = jnp.maximum(mx, sink)
    den = jnp.exp(sink - mx) if sink is not None else 0.0
    acc = None
    for s, v in zip(s_list, v_list):
        p = jnp.exp(s - mx)
        den = den + p.sum(axis=-1, keepdims=True)
        pv = jnp.dot(p.astype(BF16), v, preferred_element_type=F32)
        acc = pv if acc is None else acc + pv
    return acc / den


def _dup_half(x, j, lo):
    xr = pltpu.roll(x, HD, 1)
    return jnp.where(lo, x, xr) if j == 0 else jnp.where(lo, xr, x)


def _stack_heads(q_ref, heads, lo, scale):
    parts = []
    for h in heads:
        qp = q_ref[:, (h // 2) * LANES:(h // 2 + 1) * LANES]
        keep = lo if h % 2 == 0 else jnp.logical_not(lo)
        parts.append(jnp.where(keep, qp, 0.0) * scale)
    return jnp.concatenate(parts, axis=0).astype(BF16)


def _sink_column(sink_ref, heads, rows):
    return jnp.concatenate([jnp.full((rows, 1), sink_ref[h], F32) for h in heads], axis=0)


def _ctx_gqa_kernel(sink_ref, q_ref, k_ref, v_ref, o_ref):
    lo = _lane_lo()
    k = k_ref[...]
    v = v_ref[...]
    group = A_HEADS // A_KV_HEADS
    scores = []
    for j in range(A_KV_HEADS):
        q = _stack_heads(q_ref, list(range(group * j, group * (j + 1))), lo, HD ** -0.5)
        scores.append(lax.dot_general(q, _dup_half(k, j, lo).astype(BF16), NT_DIMS, preferred_element_type=F32))
    s = jnp.concatenate(scores, axis=0)
    sink = _sink_column(sink_ref, list(range(A_HEADS)), SEQ)
    mx = jnp.maximum(s.max(axis=-1, keepdims=True), sink)
    e = jnp.exp(s - mx)
    den = jnp.exp(sink - mx) + e.sum(axis=-1, keepdims=True)
    e = e.astype(BF16)
    rows_per_group = group * SEQ
    for j in range(A_KV_HEADS):
        rows = slice(j * rows_per_group, (j + 1) * rows_per_group)
        o = jnp.dot(e[rows], _dup_half(v, j, lo).astype(BF16), preferred_element_type=F32) / den[rows]
        for t in range(group // 2):
            pair = (group * j) // 2 + t
            o_ref[:, pair * LANES:(pair + 1) * LANES] = jnp.where(
                lo, o[(2 * t) * SEQ:(2 * t + 1) * SEQ], o[(2 * t + 1) * SEQ:(2 * t + 2) * SEQ])


def _ctx_gqa(p, sink):
    return pl.pallas_call(
        _ctx_gqa_kernel,
        grid_spec=pltpu.PrefetchScalarGridSpec(
            num_scalar_prefetch=1,
            grid=(BATCH,),
            in_specs=[pl.BlockSpec((SEQ, A_Q), lambda b, s: (b, 0)),
                      pl.BlockSpec((SEQ, A_KV), lambda b, s: (b, A_Q // A_KV)),
                      pl.BlockSpec((SEQ, A_KV), lambda b, s: (b, A_Q // A_KV + 1))],
            out_specs=pl.BlockSpec((SEQ, A_Q), lambda b, s: (b, 0))),
        out_shape=jax.ShapeDtypeStruct((NP_TOK, A_Q), F32),
        compiler_params=_cparams(1),
        name="ctx_gqa",
    )(sink, p, p, p)


def _win_kernel(sink_ref, q_ref, kp_ref, kc_ref, kn_ref, vp_ref, vc_ref, vn_ref, ck_ref, cv_ref, o_ref):
    i = pl.program_id(1)
    lo = _lane_lo()
    k = jnp.concatenate([kp_ref[...], kc_ref[...], kn_ref[...]], axis=0)
    v = jnp.concatenate([vp_ref[...], vc_ref[...], vn_ref[...]], axis=0)
    ck = ck_ref[...]
    cv = cv_ref[...]
    group = A_HEADS // A_KV_HEADS
    n_keys = WIN_Q + 2 * A_WINDOW
    qpos = i * WIN_Q + lax.broadcasted_iota(jnp.int32, (WIN_Q, n_keys), 0)
    kpos = i * WIN_Q - A_WINDOW + lax.broadcasted_iota(jnp.int32, (WIN_Q, n_keys), 1)
    valid = (jnp.abs(kpos - qpos) <= A_WINDOW) & (kpos >= 0) & (kpos < DEC_SEQ)
    valid = jnp.concatenate([valid] * group, axis=0)
    s_loc, s_ctx, values = [], [], []
    for j in range(A_KV_HEADS):
        heads = list(range(group * j, group * (j + 1)))
        kd = _dup_half(k, j, lo).astype(BF16)
        ckd = _dup_half(ck, j, lo).astype(BF16)
        values.append((_dup_half(v, j, lo).astype(BF16), _dup_half(cv, j, lo).astype(BF16)))
        q = _stack_heads(q_ref, heads, lo, HD ** -0.5)
        s_loc.append(jnp.where(valid, lax.dot_general(q, kd, NT_DIMS, preferred_element_type=F32), NEG))
        s_ctx.append(lax.dot_general(q, ckd, NT_DIMS, preferred_element_type=F32))
    s_loc = jnp.concatenate(s_loc, axis=0)
    s_ctx = jnp.concatenate(s_ctx, axis=0)
    sink = _sink_column(sink_ref, list(range(A_HEADS)), WIN_Q)
    mx = jnp.maximum(jnp.maximum(s_loc.max(axis=-1, keepdims=True), s_ctx.max(axis=-1, keepdims=True)), sink)
    p_loc = jnp.exp(s_loc - mx)
    p_ctx = jnp.exp(s_ctx - mx)
    den = p_loc.sum(axis=-1, keepdims=True) + p_ctx.sum(axis=-1, keepdims=True) + jnp.exp(sink - mx)
    p_loc = p_loc.astype(BF16)
    p_ctx = p_ctx.astype(BF16)
    rows_per_group = group * WIN_Q
    for j, (vd, cvd) in enumerate(values):
        rows = slice(j * rows_per_group, (j + 1) * rows_per_group)
        o = (jnp.dot(p_loc[rows], vd, preferred_element_type=F32)
             + jnp.dot(p_ctx[rows], cvd, preferred_element_type=F32)) / den[rows]
        for t in range(group // 2):
            pair = (group * j) // 2 + t
            o_ref[:, pair * LANES:(pair + 1) * LANES] = jnp.where(
                lo, o[(2 * t) * WIN_Q:(2 * t + 1) * WIN_Q], o[(2 * t + 1) * WIN_Q:(2 * t + 2) * WIN_Q])


WIN_Q = 256


def _win_attention(p, cache_k, cache_v, sink):
    nblk = DEC_SEQ // WIN_Q
    side = WIN_Q // A_WINDOW
    nside = DEC_SEQ // A_WINDOW
    base = NP_TOK // WIN_Q
    side_base = NP_TOK // A_WINDOW
    kcol = A_Q // A_KV

    def main_spec(col):
        return pl.BlockSpec((WIN_Q, A_KV), lambda b, i, s: (base + b * nblk + i, col))

    def side_spec(col, off):
        return pl.BlockSpec((A_WINDOW, A_KV),
                            lambda b, i, s: (side_base + b * nside + jnp.clip(side * i + off, 0, nside - 1), col))

    ctx_spec = pl.BlockSpec((None, PAST, A_KV), lambda b, i, s: (b, 0, 0))
    return pl.pallas_call(
        _win_kernel,
        grid_spec=pltpu.PrefetchScalarGridSpec(
            num_scalar_prefetch=1,
            grid=(DEC_BATCH, nblk),
            in_specs=[pl.BlockSpec((WIN_Q, A_Q), lambda b, i, s: (base + b * nblk + i, 0)),
                      side_spec(kcol, -1), main_spec(kcol), side_spec(kcol, side),
                      side_spec(kcol + 1, -1), main_spec(kcol + 1), side_spec(kcol + 1, side),
                      ctx_spec, ctx_spec],
            out_specs=pl.BlockSpec((WIN_Q, A_Q), lambda b, i, s: (b * nblk + i, 0))),
        out_shape=jax.ShapeDtypeStruct((NS_TOK, A_Q), F32),
        compiler_params=_cparams(2),
        name="win_attn",
    )(sink, p, p, p, p, p, p, p, cache_k.reshape(DEC_BATCH, PAST, A_KV), cache_v.reshape(DEC_BATCH, PAST, A_KV))


def _ret_kernel(df_ref, db_ref, q_ref, k_ref, v_ref, g_ref, gn_ref, s0f_ref, s0b_ref,
                o_ref, sf_ref, sb_ref, of_scr, ob_scr, *, length):
    c_len = RET_CHUNK
    n = length // c_len
    lo = _lane_lo()
    hi = jnp.logical_not(lo)
    row = lax.broadcasted_iota(jnp.int32, (c_len, c_len), 0)
    col = lax.broadcasted_iota(jnp.int32, (c_len, c_len), 1)
    rowp = lax.broadcasted_iota(jnp.int32, (LANES, LANES), 0)
    colp = lax.broadcasted_iota(jnp.int32, (LANES, LANES), 1)
    blockdiag = (rowp < HD) == (colp < HD)
    idx = lax.broadcasted_iota(jnp.int32, (c_len, 1), 0).astype(F32)

    def direction(dec_ref, forward):
        lg = -jnp.exp(dec_ref[...])
        diff = (row - col) if forward else (col - row)
        keep = (diff >= 0) if forward else (diff > 0)
        dist = jnp.maximum(diff, 0).astype(F32)
        dm = jnp.concatenate([jnp.where(keep, jnp.exp(dist * lg[:, off:off + 1]), 0.0) for off in (0, HD)], axis=0)
        if forward:
            xi = jnp.exp((idx + 1.0) * lg)
            zeta = jnp.exp((c_len - 1.0 - idx) * lg)
        else:
            xi = jnp.exp((c_len - idx) * lg)
            zeta = jnp.exp(idx * lg)
        return dm, xi, zeta, jnp.exp(c_len * lg)

    def chunk(c, state, consts):
        dm, xi, zeta, gch = consts
        rows = pl.ds(pl.multiple_of(c * c_len, c_len), c_len)
        qc = q_ref[rows, :]
        kc = k_ref[rows, :] * HD ** -0.5
        vc = v_ref[rows, :].astype(BF16)
        kb = kc.astype(BF16)
        q2 = jnp.concatenate([jnp.where(lo, qc, 0.0), jnp.where(hi, qc, 0.0)], axis=0).astype(BF16)
        inner = lax.dot_general(q2, kb, NT_DIMS, preferred_element_type=F32) * dm
        kz_t = (kc * zeta).T
        res = jnp.dot(jnp.concatenate([inner, kz_t], axis=0).astype(BF16), vc, preferred_element_type=F32)
        cross = jnp.dot(qc.astype(BF16), state.astype(BF16), preferred_element_type=F32) * xi
        o = jnp.where(lo, res[:c_len], res[c_len:2 * c_len]) + cross
        state = gch * state + jnp.where(blockdiag, res[2 * c_len:], 0.0)
        return rows, o, state

    cf = direction(df_ref, True)
    cb = direction(db_ref, False)

    def scan_body(t, states):
        rows_f, o_f, state_f = chunk(t, states[0], cf)
        of_scr[rows_f, :] = o_f
        rows_b, o_b, state_b = chunk(n - 1 - t, states[1], cb)
        ob_scr[rows_b, :] = o_b
        return state_f, state_b

    state_f, state_b = lax.fori_loop(0, n, scan_body, (s0f_ref[...], s0b_ref[...]), unroll=min(n, RET_UNROLL))
    sf_ref[...] = state_f
    sb_ref[...] = state_b

    gn = gn_ref[...]
    n_norm = length // RET_NORM_ROWS

    def per_head(x):
        a = jnp.where(lo, x, 0.0).sum(axis=-1, keepdims=True)
        b = jnp.where(hi, x, 0.0).sum(axis=-1, keepdims=True)
        return jnp.where(lo, a, b) * (1.0 / HD)

    def norm_body(t, carry):
        rows = pl.ds(pl.multiple_of(t * RET_NORM_ROWS, RET_NORM_ROWS), RET_NORM_ROWS)
        o = of_scr[rows, :] + ob_scr[rows, :]
        d = o - per_head(o)
        y = d * lax.rsqrt(per_head(d * d) + EPS) * gn
        o_ref[rows, :] = _silu(g_ref[rows, :]) * y
        return carry

    lax.fori_loop(0, n_norm, norm_body, 0)


def _pair_lanes(v):
    return jnp.repeat(v.astype(F32), HD).reshape(B_HEADS // 2, 1, LANES)


def _blockdiag_states(s):
    b = s.shape[0]
    s = s.astype(F32).reshape(b, B_HEADS // 2, 2, HD, HD)
    z = jnp.zeros_like(s[:, :, 0])
    top = jnp.concatenate([s[:, :, 0], z], axis=-1)
    bot = jnp.concatenate([z, s[:, :, 1]], axis=-1)
    return jnp.concatenate([top, bot], axis=-2)


def _diag_states(sp):
    b = sp.shape[0]
    s = jnp.stack([sp[:, :, :HD, :HD], sp[:, :, HD:, HD:]], axis=2)
    return s.reshape(b, B_HEADS, HD, HD)


def _retention(p, row_base, batch, length, dec_f, dec_b, gn_g, s0f, s0b):
    npairs = B_HEADS // 2
    blk0 = row_base // length
    qcol = (A_Q + 2 * A_KV) // LANES

    def col_spec(off):
        return pl.BlockSpec((length, LANES), lambda b, h: (blk0 + b, qcol + off * npairs + h))

    lane_spec = pl.BlockSpec((None, 1, LANES), lambda b, h: (h, 0, 0))
    state_spec = pl.BlockSpec((None, None, LANES, LANES), lambda b, h: (b, h, 0, 0))
    state_shape = jax.ShapeDtypeStruct((batch, npairs, LANES, LANES), F32)
    return pl.pallas_call(
        functools.partial(_ret_kernel, length=length),
        grid=(batch, npairs),
        in_specs=[lane_spec, lane_spec, col_spec(0), col_spec(1), col_spec(2), col_spec(3), lane_spec,
                  state_spec, state_spec],
        out_specs=[pl.BlockSpec((length, LANES), lambda b, h: (b, h)), state_spec, state_spec],
        out_shape=[jax.ShapeDtypeStruct((batch * length, B_W), F32), state_shape, state_shape],
        scratch_shapes=[pltpu.VMEM((length, LANES), F32), pltpu.VMEM((length, LANES), F32)],
        compiler_params=_cparams(2),
        name="retention",
    )(_pair_lanes(dec_f), _pair_lanes(dec_b), p, p, p, p, gn_g.reshape(npairs, 1, LANES), s0f, s0b)


def _ctx_mha_kernel(q_ref, k_ref, v_ref, o_ref):
    lo = _lane_lo()
    for pair in range(C_HEADS // 2):
        cols = slice(pair * LANES, (pair + 1) * LANES)
        q = _stack_heads(q_ref, [2 * pair, 2 * pair + 1], lo, HD ** -0.5)
        s = lax.dot_general(q, k_ref[:, cols].astype(BF16), NT_DIMS, preferred_element_type=F32)
        o = _softmax_av([s], [v_ref[:, cols].astype(BF16)])
        o_ref[:, cols] = jnp.where(lo, o[:SEQ], o[SEQ:])


def _ctx_mha(p):
    return pl.pallas_call(
        _ctx_mha_kernel,
        grid=(BATCH,),
        in_specs=[pl.BlockSpec((SEQ, C_W), lambda b: (b, 0)),
                  pl.BlockSpec((SEQ, C_W), lambda b: (b, 1)),
                  pl.BlockSpec((SEQ, C_W), lambda b: (b, 2))],
        out_specs=pl.BlockSpec((SEQ, C_W), lambda b: (b, 0)),
        out_shape=jax.ShapeDtypeStruct((NP_TOK, C_W), F32),
        compiler_params=_cparams(1),
        name="ctx_mha",
    )(p, p, p)


NA_WIN_ROWS = 2 * NA_ROWS
NA_WIN = NA_WIN_ROWS * GRID_W
NA_QROWS = NA_ROWS * GRID_W
NA_PAD_ROWS = NA_KH // 2
NA_TABLE = 1536


def _na_kernel(q_ref, kp_ref, km_ref, kn_ref, vp_ref, vm_ref, vn_ref, ck_ref, cv_ref, ue_ref, uo_ref, o_ref):
    r0 = pl.program_id(2) * NA_ROWS
    n_rows = DEC_SEQ // GRID_W
    lo = _lane_lo()
    k = jnp.concatenate([kp_ref[...], km_ref[...], kn_ref[...]], axis=0).astype(BF16)
    v = jnp.concatenate([vp_ref[...], vm_ref[...], vn_ref[...]], axis=0).astype(BF16)
    ck = ck_ref[...].astype(BF16)
    cv = cv_ref[...].astype(BF16)
    q = q_ref[...] * HD ** -0.5
    klane = lax.broadcasted_iota(jnp.int32, (1, NA_WIN), 1)
    outs = []
    for half, keep in enumerate((lo, jnp.logical_not(lo))):
        qh = jnp.where(keep, q, 0.0).astype(BF16)
        s = lax.dot_general(qh, k, NT_DIMS, preferred_element_type=F32)
        s_ctx = lax.dot_general(qh, ck, NT_DIMS, preferred_element_type=F32)
        p_loc, p_ctx, den = [], [], []
        for rq in range(NA_ROWS):
            rows = slice(rq * GRID_W, (rq + 1) * GRID_W)
            start = NA_KH - 1 - rq
            if start % 2 == 0:
                u = ue_ref[half, :, start * GRID_W:start * GRID_W + NA_WIN]
            else:
                u = uo_ref[half, :, (start - 1) * GRID_W:(start - 1) * GRID_W + NA_WIN]
            r = r0 + rq
            first = jnp.clip(r - NA_KH // 2, 0, n_rows - NA_KH)
            lane0 = (first - r0 + NA_PAD_ROWS) * GRID_W
            in_rows = (klane >= lane0) & (klane < lane0 + NA_KH * GRID_W)
            sl = jnp.where(in_rows, s[rows] + u, NEG)
            sc = s_ctx[rows]
            mx = jnp.maximum(sl.max(axis=-1, keepdims=True), sc.max(axis=-1, keepdims=True))
            el = jnp.exp(sl - mx)
            ec = jnp.exp(sc - mx)
            den.append(el.sum(axis=-1, keepdims=True) + ec.sum(axis=-1, keepdims=True))
            p_loc.append(el.astype(BF16))
            p_ctx.append(ec.astype(BF16))
        acc = (jnp.dot(jnp.concatenate(p_loc, axis=0), v, preferred_element_type=F32)
               + jnp.dot(jnp.concatenate(p_ctx, axis=0), cv, preferred_element_type=F32))
        outs.append(acc / jnp.concatenate(den, axis=0))
    o_ref[...] = jnp.where(lo, outs[0], outs[1])


def _na_bias_tables(rpb):
    cq = jnp.arange(GRID_W)
    ck = jnp.arange(GRID_W)
    dc = jnp.clip(ck[None] - cq[:, None], -(NA_KW - 1), NA_KW - 1) + NA_KW - 1
    cs = jnp.clip(cq - NA_KW // 2, 0, GRID_W - NA_KW)
    col_ok = (ck[None] >= cs[:, None]) & (ck[None] < cs[:, None] + NA_KW)
    t = rpb.astype(F32)[:, :, dc]
    t = jnp.where(col_ok[None, None], t, NEG).transpose(0, 2, 1, 3)
    n_dr = 2 * NA_KH - 1
    blocks = NA_TABLE // GRID_W
    t = jnp.pad(t, ((0, 0), (0, 0), (NA_PAD_ROWS, blocks - n_dr - NA_PAD_ROWS), (0, 0)), constant_values=NEG)
    ue = t.reshape(C_HEADS, GRID_W, NA_TABLE)
    uo = jnp.concatenate([ue[..., GRID_W:], jnp.full((C_HEADS, GRID_W, GRID_W), NEG, F32)], axis=-1)
    return ue, uo


def _na_attention(p, cache_k, cache_v, rpb):
    npairs = C_HEADS // 2
    nrb = DEC_SEQ // NA_QROWS
    half = NA_QROWS // 2
    qbase = NP_TOK // NA_QROWS
    hbase = NP_TOK // half
    kcol = C_W // LANES
    ue, uo = _na_bias_tables(rpb)

    def main_spec(col0):
        return pl.BlockSpec((NA_QROWS, LANES), lambda b, h, r: (qbase + b * nrb + r, col0 + h))

    def side_spec(col0, off):
        return pl.BlockSpec((half, LANES),
                            lambda b, h, r: (hbase + b * 2 * nrb + jnp.clip(2 * r + off, 0, 2 * nrb - 1), col0 + h))

    ctx_spec = pl.BlockSpec((None, PAST, LANES), lambda b, h, r: (b, 0, h))
    tab_spec = pl.BlockSpec((2, GRID_W, NA_TABLE), lambda b, h, r: (h, 0, 0))
    return pl.pallas_call(
        _na_kernel,
        grid=(DEC_BATCH, npairs, nrb),
        in_specs=[main_spec(0),
                  side_spec(kcol, -1), main_spec(kcol), side_spec(kcol, 2),
                  side_spec(2 * kcol, -1), main_spec(2 * kcol), side_spec(2 * kcol, 2),
                  ctx_spec, ctx_spec, tab_spec, tab_spec],
        out_specs=pl.BlockSpec((NA_QROWS, LANES), lambda b, h, r: (b * nrb + r, h)),
        out_shape=jax.ShapeDtypeStruct((NS_TOK, C_W), F32),
        compiler_params=_cparams(3),
        name="na_attn",
    )(p, p, p, p, p, p, p, cache_k.reshape(DEC_BATCH, PAST, C_W), cache_v.reshape(DEC_BATCH, PAST, C_W), ue, uo)


def _route(biased, scores):
    t = biased.shape[1]
    per_group = N_EXPERTS // N_GROUPS
    i8 = lax.broadcasted_iota(jnp.int32, (per_group, t), 0)
    g_rows = []
    for g in range(N_GROUPS):
        bg = biased[g * per_group:(g + 1) * per_group]
        m1 = bg.max(axis=0, keepdims=True)
        first = jnp.where(bg == m1, i8, per_group).min(axis=0, keepdims=True)
        m2 = jnp.where(i8 == first, -jnp.inf, bg).max(axis=0, keepdims=True)
        g_rows.append(m1 + m2)
    g_top = jnp.concatenate(g_rows, axis=0)
    gi = lax.broadcasted_iota(jnp.int32, g_top.shape, 0)
    g_sel = jnp.zeros(g_top.shape, jnp.int32)
    cur = g_top
    for _ in range(TOPK_GROUPS):
        m = cur.max(axis=0, keepdims=True)
        hit = gi == jnp.where(cur == m, gi, N_GROUPS).min(axis=0, keepdims=True)
        g_sel = jnp.where(hit, 1, g_sel)
        cur = jnp.where(hit, -jnp.inf, cur)
    e_sel = jnp.concatenate([jnp.broadcast_to(g_sel[g:g + 1], (per_group, t)) for g in range(N_GROUPS)], axis=0)
    cur = jnp.where(e_sel > 0, biased, NEG)
    ei = lax.broadcasted_iota(jnp.int32, cur.shape, 0)
    ids, gates, hits = [], [], []
    for _ in range(TOP_K):
        m = cur.max(axis=0, keepdims=True)
        f = jnp.where(cur == m, ei, N_EXPERTS).min(axis=0, keepdims=True)
        hit = ei == f
        ids.append(f)
        hits.append(hit)
        gates.append(jnp.where(hit, scores, 0.0).sum(axis=0, keepdims=True))
        cur = jnp.where(hit, -jnp.inf, cur)
    gate = jnp.concatenate(gates, axis=0)
    gate = gate / gate.sum(axis=0, keepdims=True) * ROUTED_SCALE
    return jnp.concatenate(ids, axis=0), gate, hits


def _pack_bf16_pairs(h):
    bits = lax.bitcast_convert_type(h.astype(BF16).astype(F32), jnp.uint32)
    return bits[:, :D // 2] | (bits[:, D // 2:] >> 16)


def _unpack_bf16_pairs(xp):
    hi = lax.bitcast_convert_type(xp & jnp.uint32(0xFFFF0000), F32).astype(BF16)
    lo = lax.bitcast_convert_type(xp << 16, F32).astype(BF16)
    return hi, lo


def _dot_halves(hi, lo, w_ref):
    return (jnp.dot(hi, w_ref[:D // 2, :], preferred_element_type=F32)
            + jnp.dot(lo, w_ref[D // 2:, :], preferred_element_type=F32))


def _outproj_kernel(*refs, n_parts):
    xp_ref, xs_ref = refs[:2]
    part_refs = refs[2:2 + 3 * n_parts]
    gate_ref, shift_ref, scale_ref, g2_ref, rw_ref, rb_ref = refs[2 + 3 * n_parts:8 + 3 * n_parts]
    xo_ref, h_ref, dest_ref, wgt_ref, plan_ref, cnt_ref = refs[8 + 3 * n_parts:]
    step = pl.program_id(0)

    @pl.when(step == 0)
    def _():
        cnt_ref[...] = jnp.zeros_like(cnt_ref)
        plan_ref[...] = jnp.zeros_like(plan_ref)

    is_prompt = step < NP_TOK // TM
    sub = TM // ROUTE_SPLIT
    before = (lax.broadcasted_iota(jnp.int32, (sub, sub), 0) < lax.broadcasted_iota(jnp.int32, (sub, sub), 1))
    before = jnp.where(before, 1.0, 0.0).astype(BF16)
    e_base = (lax.broadcasted_iota(jnp.int32, (N_EXPERTS, 1), 0) * N_TOK).astype(F32)
    counts = cnt_ref[...]
    for g in range(ROUTE_SPLIT):
        rows = slice(g * sub, (g + 1) * sub)
        y = None
        for t in range(n_parts):
            ap_ref, as_ref, w_ref = part_refs[3 * t:3 * t + 3]
            a = jnp.where(is_prompt, ap_ref[rows, :], as_ref[rows, :]).astype(BF16)
            d = jnp.dot(a, w_ref[...], preferred_element_type=F32)
            y = d if y is None else y + d
        x = jnp.where(is_prompt, xp_ref[rows, :], xs_ref[rows, :]) + gate_ref[...] * y
        xo_ref[rows, :] = x
        h = _rms(x, g2_ref[...]) * (1.0 + scale_ref[...]) + shift_ref[...]
        h_ref[rows, :] = _pack_bf16_pairs(h)
        logits = lax.dot_general(rw_ref[...], h, NT_DIMS, preferred_element_type=F32, precision=HIGHEST)
        scores = jax.nn.sigmoid(logits)
        _, gate, hits = _route(scores + rb_ref[...], scores)
        wgt_ref[:, rows] = gate
        chosen = hits[0]
        for hit in hits[1:]:
            chosen = chosen | hit
        m = jnp.where(chosen, 1.0, 0.0)
        prefix = jnp.dot(m.astype(BF16), before, preferred_element_type=F32)
        row_all = prefix + (counts + e_base)
        dest_ref[:, rows] = jnp.concatenate(
            [jnp.where(hit, row_all, 0.0).sum(axis=0, keepdims=True) for hit in hits], axis=0).astype(jnp.int32)
        counts = counts + m.sum(axis=1, keepdims=True)
    cnt_ref[...] = counts

    @pl.when(step == pl.num_programs(0) - 1)
    def _():
        _block_plan(cnt_ref[...], plan_ref)


def _block_plan(counts, plan_ref):
    cap_blocks = N_TOK // MOE_BLOCK
    nblk = ((counts.astype(jnp.int32) + (MOE_BLOCK - 1)) // MOE_BLOCK).astype(F32)
    lower = (lax.broadcasted_iota(jnp.int32, (N_EXPERTS, N_EXPERTS), 0)
             >= lax.broadcasted_iota(jnp.int32, (N_EXPERTS, N_EXPERTS), 1))
    cum = jnp.dot(jnp.where(lower, 1.0, 0.0).astype(BF16), jnp.broadcast_to(nblk, (N_EXPERTS, LANES)).astype(BF16),
                  preferred_element_type=F32)[:, :1]
    n_used = cum[N_EXPERTS - 1:, :]
    slot = jnp.minimum(lax.broadcasted_iota(jnp.int32, (1, PLAN_LANES), 1).astype(F32), n_used - 1.0)
    done = cum <= slot
    expert = jnp.where(done, 1.0, 0.0).sum(axis=0, keepdims=True)
    blocks_before = jnp.where(done, nblk, 0.0).sum(axis=0, keepdims=True)
    plan_ref[0:1, :] = (expert * cap_blocks + (slot - blocks_before)).astype(jnp.int32)
    plan_ref[1:2, :] = expert.astype(jnp.int32)
    plan_ref[2:3, :] = jnp.broadcast_to(n_used, (1, PLAN_LANES)).astype(jnp.int32)
    current = (cum > slot) & (cum - nblk <= slot)
    rows_left = jnp.where(current, counts, 0.0).sum(axis=0, keepdims=True) - (slot - blocks_before) * MOE_BLOCK
    plan_ref[3:4, :] = rows_left.astype(jnp.int32)


def _outproj(x, parts, mod, g2, router_w, router_b):
    in_specs = _pair_specs(D)
    args = [x[0], x[1]]
    for ap, a_s, w in parts:
        width = ap.shape[1]
        in_specs += _pair_specs(width) + [pl.BlockSpec((width, D), lambda i: (0, 0))]
        args += [ap, a_s, w]
    in_specs += [_mod_spec(2), _mod_spec(3), _mod_spec(4),
                 pl.BlockSpec((1, D), lambda i: (0, 0)),
                 pl.BlockSpec((N_EXPERTS, D), lambda i: (0, 0)),
                 pl.BlockSpec((N_EXPERTS, 1), lambda i: (0, 0))]
    args += [mod, mod, mod, g2.reshape(1, D), router_w.T, router_b.reshape(N_EXPERTS, 1)]
    return pl.pallas_call(
        functools.partial(_outproj_kernel, n_parts=len(parts)),
        grid=(N_TOK // TM,),
        in_specs=in_specs,
        out_specs=[pl.BlockSpec((TM, D), lambda i: (i, 0)),
                   pl.BlockSpec((TM, D // 2), lambda i: (i, 0)),
                   pl.BlockSpec((TOP_K, TM), lambda i: (0, i)),
                   pl.BlockSpec((TOP_K, TM), lambda i: (0, i)),
                   pl.BlockSpec((8, PLAN_LANES), lambda i: (0, 0))],
        out_shape=[jax.ShapeDtypeStruct((N_TOK, D), F32),
                   jax.ShapeDtypeStruct((N_TOK, D // 2), jnp.uint32),
                   jax.ShapeDtypeStruct((TOP_K, N_TOK), jnp.int32),
                   jax.ShapeDtypeStruct((TOP_K, N_TOK), F32),
                   jax.ShapeDtypeStruct((8, PLAN_LANES), jnp.int32)],
        scratch_shapes=[pltpu.VMEM((N_EXPERTS, 1), F32)],
        compiler_params=_cparams(1),
        name="outproj_router",
    )(*args)


def _experts_kernel(br_ref, be_ref, nu_ref, nr_ref, x_ref, w1_ref, w3_ref, w2_ref, o_ref, w1b, w3b, w2b):
    i = pl.program_id(0)
    e = be_ref[i]
    prev = be_ref[jnp.maximum(i - 1, 0)]

    @pl.when((i == 0) | (e != prev))
    def _():
        w1b[...] = w1_ref[...].astype(BF16)
        w3b[...] = w3_ref[...].astype(BF16)
        w2b[...] = w2_ref[...].astype(BF16)

    def swiglu(rows):
        hi, lo = _unpack_bf16_pairs(x_ref[rows, :])
        a = _dot_halves(hi, lo, w1b)
        b = _dot_halves(hi, lo, w3b)
        h = (_silu(a) * b).astype(BF16)
        o_ref[rows, :] = _pack_bf16_pairs(jnp.dot(h, w2b[...], preferred_element_type=F32))

    live = i < nu_ref[0]
    half = MOE_BLOCK // 2

    @pl.when(live & (nr_ref[i] > half))
    def _():
        swiglu(pl.ds(0, MOE_BLOCK))

    @pl.when(live & (nr_ref[i] <= half))
    def _():
        swiglu(pl.ds(0, half))


def _experts(plan, x_rows, w1, w3, w2, layer):
    return pl.pallas_call(
        _experts_kernel,
        grid_spec=pltpu.PrefetchScalarGridSpec(
            num_scalar_prefetch=4,
            grid=(N_MOE_BLOCKS,),
            in_specs=[pl.BlockSpec((MOE_BLOCK, D // 2), lambda i, br, be, nu, nr: (br[i], 0)),
                      pl.BlockSpec((None, None, D, FF), lambda i, br, be, nu, nr: (layer, be[i], 0, 0)),
                      pl.BlockSpec((None, None, D, FF), lambda i, br, be, nu, nr: (layer, be[i], 0, 0)),
                      pl.BlockSpec((None, None, FF, D), lambda i, br, be, nu, nr: (layer, be[i], 0, 0))],
            out_specs=pl.BlockSpec((MOE_BLOCK, D // 2), lambda i, br, be, nu, nr: (br[i], 0)),
            scratch_shapes=[pltpu.VMEM((D, FF), BF16), pltpu.VMEM((D, FF), BF16), pltpu.VMEM((FF, D), BF16)]),
        out_shape=jax.ShapeDtypeStruct(x_rows.shape, jnp.uint32),
        compiler_params=_cparams(1),
        name="experts",
    )(plan[0], plan[1], plan[2, :1], plan[3], x_rows, w1, w3, w2)


SC_CORES = 2
SC_SUBCORES = 16
SC_WORKERS = SC_CORES * SC_SUBCORES
SC_CHUNK_BYTES = 64 * 1024
SC_SLOTS = 4


def _sc_scatter(rows, dest, n_out):
    n_rows, width = rows.shape
    picks = dest.shape[0]
    chunk = SC_CHUNK_BYTES // (4 * width)
    per_worker = n_rows // SC_WORKERS
    n_chunks = per_worker // chunk
    assert per_worker * SC_WORKERS == n_rows and n_chunks * chunk == per_worker and n_chunks % 2 == 0
    mesh = plsc.VectorSubcoreMesh(core_axis_name="c", subcore_axis_name="s")

    @functools.partial(
        pl.kernel, mesh=mesh,
        out_type=jax.ShapeDtypeStruct((n_out, width), rows.dtype),
        scratch_types=[pltpu.VMEM((picks, n_chunks, chunk), jnp.int32),
                       pltpu.VMEM((2, chunk, width), rows.dtype),
                       pltpu.SemaphoreType.DMA((2,)),
                       pltpu.SemaphoreType.DMA((2,))])
    def scatter(r_hbm, d_hbm, o_hbm, idx_v, rows_v, lsem, ssem):
        worker = lax.axis_index("s") * SC_CORES + lax.axis_index("c")
        base = worker * per_worker
        for k in range(picks):
            pltpu.sync_copy(d_hbm.at[k, worker], idx_v.at[k])

        def load_copy(c, b):
            src = pl.ds(pl.multiple_of(base + c * chunk, chunk), chunk)
            return pltpu.make_async_copy(r_hbm.at[src], rows_v.at[b], lsem.at[b])

        def store_copy(c, b, k):
            return pltpu.make_async_copy(rows_v.at[b], o_hbm.at[idx_v.at[k, c]], ssem.at[b])

        load_copy(0, 0).start()

        @pl.loop(0, n_chunks, step=2)
        def _(c0):
            for b in range(2):
                c = c0 + b
                load_copy(c, b).wait()
                for k in range(picks):
                    store_copy(c, b, k).start()

                @pl.when(c > 0)
                def _():
                    for k in range(picks):
                        store_copy(c - 1, 1 - b, k).wait()

                @pl.when(c + 1 < n_chunks)
                def _():
                    load_copy(c + 1, 1 - b).start()

        for k in range(picks):
            store_copy(n_chunks - 1, 1, k).wait()

    return scatter(rows, dest.reshape(picks, SC_WORKERS, n_chunks, chunk))


def _sc_gather(table, idx):
    n_idx = idx.shape[0]
    width = table.shape[1]
    chunk = SC_CHUNK_BYTES // (4 * width)
    per_worker = n_idx // SC_WORKERS
    n_chunks = per_worker // chunk
    ahead = SC_SLOTS - 1
    assert per_worker * SC_WORKERS == n_idx and n_chunks * chunk == per_worker and n_chunks % SC_SLOTS == 0
    mesh = plsc.VectorSubcoreMesh(core_axis_name="c", subcore_axis_name="s")

    @functools.partial(
        pl.kernel, mesh=mesh,
        out_type=jax.ShapeDtypeStruct((n_idx, width), table.dtype),
        scratch_types=[pltpu.VMEM((per_worker,), jnp.int32),
                       pltpu.VMEM((SC_SLOTS, chunk, width), table.dtype),
                       pltpu.SemaphoreType.DMA((SC_SLOTS,)),
                       pltpu.SemaphoreType.DMA((SC_SLOTS,))])
    def gather(t_hbm, i_hbm, o_hbm, idx_v, rows_v, gsem, wsem):
        worker = lax.axis_index("s") * SC_CORES + lax.axis_index("c")
        base = worker * per_worker
        pltpu.sync_copy(i_hbm.at[pl.ds(pl.multiple_of(base, chunk), per_worker)], idx_v)

        def gather_copy(c, b):
            ids = idx_v.at[pl.ds(pl.multiple_of(c * chunk, chunk), chunk)]
            return pltpu.make_async_copy(t_hbm.at[ids], rows_v.at[b], gsem.at[b])

        def write_copy(c, b):
            rows = pl.ds(pl.multiple_of(base + c * chunk, chunk), chunk)
            return pltpu.make_async_copy(rows_v.at[b], o_hbm.at[rows], wsem.at[b])

        for c in range(ahead):
            gather_copy(c, c).start()

        @pl.loop(0, n_chunks, step=SC_SLOTS)
        def _(c0):
            for b in range(SC_SLOTS):
                c = c0 + b
                refill = (b + ahead) % SC_SLOTS
                gather_copy(c, b).wait()
                write_copy(c, b).start()

                @pl.when(c > 0)
                def _():
                    write_copy(c - 1, refill).wait()

                @pl.when(c + ahead < n_chunks)
                def _():
                    gather_copy(c + ahead, refill).start()

        write_copy(n_chunks - 1, (n_chunks - 1) % SC_SLOTS).wait()

    return gather(table, idx)


TC = 512


def _combine_kernel(x_ref, h_ref, y_ref, wgt_ref, gate_ref, w1_ref, w3_ref, w2_ref, fg_ref, o_ref, *, final):
    hi, lo = _unpack_bf16_pairs(h_ref[...])
    a = _dot_halves(hi, lo, w1_ref)
    b = _dot_halves(hi, lo, w3_ref)
    ffn = jnp.dot((_silu(a) * b).astype(BF16), w2_ref[...], preferred_element_type=F32)
    wgt = wgt_ref[...]
    r_hi = None
    r_lo = None
    for k in range(TOP_K):
        yk = y_ref[k]
        w = wgt[:, k:k + 1]
        t_hi = lax.bitcast_convert_type(yk & jnp.uint32(0xFFFF0000), F32) * w
        t_lo = lax.bitcast_convert_type(yk << 16, F32) * w
        r_hi = t_hi if r_hi is None else r_hi + t_hi
        r_lo = t_lo if r_lo is None else r_lo + t_lo
    x = x_ref[...] + gate_ref[...] * (ffn + jnp.concatenate([r_hi, r_lo], axis=1))
    o_ref[...] = _rms(x, fg_ref[...]) if final else x


def _combine(x, h, y_rows, wgt, mod, sw1, sw3, sw2, final_g, final):
    weights = (sw1.astype(BF16), sw3.astype(BF16), sw2.astype(BF16), final_g.reshape(1, D))

    def rows_from(first_row, n_rows):
        b0 = first_row // TC
        return pl.pallas_call(
            functools.partial(_combine_kernel, final=final),
            grid=(n_rows // TC,),
            in_specs=[pl.BlockSpec((TC, D), lambda i: (i + b0, 0)),
                      pl.BlockSpec((TC, D // 2), lambda i: (i + b0, 0)),
                      pl.BlockSpec((TOP_K, TC, D // 2), lambda i: (0, i + b0, 0)),
                      pl.BlockSpec((TC, TOP_K), lambda i: (i + b0, 0)),
                      _mod_spec(5, TC, b0),
                      pl.BlockSpec((D, FF), lambda i: (0, 0)),
                      pl.BlockSpec((D, FF), lambda i: (0, 0)),
                      pl.BlockSpec((FF, D), lambda i: (0, 0)),
                      pl.BlockSpec((1, D), lambda i: (0, 0))],
            out_specs=pl.BlockSpec((TC, D), lambda i: (i, 0)),
            out_shape=jax.ShapeDtypeStruct((n_rows, D), F32),
            compiler_params=_cparams(1),
            name="combine",
        )(x, h, y_rows, wgt, mod, *weights)

    return rows_from(0, NP_TOK), rows_from(NP_TOK, NS_TOK)


def kernel(x_prompt, x_sample, cache_a_k, cache_a_v, state_ret_fwd, state_ret_bwd, cache_c_k, cache_c_v,
           c, c_ctx, norm1_g, norm2_g, ada_w, ada_b, even_w_in, even_w_out, sink_a, ret_decay_fwd,
           ret_decay_bwd, ret_gn_g, odd_w_in, odd_w_out, na_rpb, router_w, router_b, exp_w1, exp_w3,
           exp_w2, sh_w1, sh_w3, sh_w2, final_g):
    x = (x_prompt.reshape(NP_TOK, D), x_sample.reshape(NS_TOK, D))
    cc = jnp.concatenate([c_ctx[None], c, jnp.zeros((8 - 1 - DEC_BATCH, D), F32)], axis=0)
    rope = _rope_tables()
    outs = {}
    for l in range(2):
        mod = _ada(cc, ada_w, ada_b, l)
        if l == 0:
            p = _inproj(x, norm1_g[l], mod, even_w_in[0].astype(BF16), rope, A_Q + A_KV)
            oa_p = _ctx_gqa(p, sink_a[0])
            oa_s = _win_attention(p, cache_a_k[:, 0], cache_a_v[:, 0], sink_a[0])
            zero = jnp.zeros((BATCH, B_HEADS // 2, LANES, LANES), F32)
            ob_p, sf, sb = _retention(p, 0, BATCH, SEQ, ret_decay_fwd[0], ret_decay_bwd[0], ret_gn_g[0], zero, zero)
            ob_s, _, _ = _retention(p, NP_TOK, DEC_BATCH, DEC_SEQ, ret_decay_fwd[0], ret_decay_bwd[0], ret_gn_g[0],
                                    _blockdiag_states(state_ret_fwd[:, 0]), _blockdiag_states(state_ret_bwd[:, 0]))
            w_out = even_w_out[0].astype(BF16)
            parts = [(oa_p, oa_s, w_out[:A_Q]), (ob_p, ob_s, w_out[A_Q:])]
            outs["a_k"] = p[:NP_TOK, A_Q:A_Q + A_KV].reshape(BATCH, 1, SEQ, A_KV_HEADS, HD)
            outs["a_v"] = p[:NP_TOK, A_Q + A_KV:A_Q + 2 * A_KV].reshape(BATCH, 1, SEQ, A_KV_HEADS, HD)
            outs["r_f"] = _diag_states(sf).reshape(BATCH, 1, B_HEADS, HD, HD)
            outs["r_b"] = _diag_states(sb).reshape(BATCH, 1, B_HEADS, HD, HD)
        else:
            p = _inproj(x, norm1_g[l], mod, odd_w_in[0].astype(BF16), rope, 0)
            o_p = _ctx_mha(p)
            o_s = _na_attention(p, cache_c_k[:, 0], cache_c_v[:, 0], na_rpb[0])
            parts = [(o_p, o_s, odd_w_out[0].astype(BF16))]
            outs["c_k"] = p[:NP_TOK, C_W:2 * C_W].reshape(BATCH, 1, SEQ, C_HEADS, HD)
            outs["c_v"] = p[:NP_TOK, 2 * C_W:3 * C_W].reshape(BATCH, 1, SEQ, C_HEADS, HD)
        x_mid, h, dest, gate_t, plan = _outproj(x, parts, mod, norm2_g[l], router_w[l], router_b[l])
        y = _experts(plan, _sc_scatter(h, dest, N_EXPERTS * N_TOK), exp_w1, exp_w3, exp_w2, l)
        y_rows = _sc_gather(y, dest.reshape(N_ASSIGN)).reshape(TOP_K, N_TOK, D // 2)
        x = _combine(x_mid, h, y_rows, gate_t.T, mod, sh_w1[l], sh_w3[l], sh_w2[l], final_g, final=(l == 1))
    y_prompt = x[0].reshape(BATCH, SEQ, D)
    y_sample = x[1].reshape(DEC_BATCH, DEC_SEQ, D)
    return (y_prompt, y_sample, outs["a_k"], outs["a_v"], outs["r_f"], outs["r_b"], outs["c_k"], outs["c_v"])
```

```python
import functools
import math

import jax
import jax.numpy as jnp
from jax import lax
from jax.experimental import pallas as pl
from jax.experimental.pallas import tpu as pltpu
from jax.experimental.pallas import tpu_sc as plsc

F32 = jnp.float32
BF16 = jnp.bfloat16
HIGHEST = lax.Precision.HIGHEST

D = 1024
BATCH = 32
SEQ = 256
DEC_BATCH = 4
DEC_SEQ = 4096
PAST = 256
GRID_W = 64
HD = 64
EPS = 1e-6
NEG = -1e30
ROPE_BASE = 10000.0
A_HEADS = 8
A_KV_HEADS = 2
A_Q = A_HEADS * HD
A_KV = A_KV_HEADS * HD
B_HEADS = 8
B_W = B_HEADS * HD
EVEN_IN = A_Q + 2 * A_KV + 4 * B_W
C_HEADS = 16
C_W = C_HEADS * HD
NA_KH = 8
NA_KW = 16
N_EXPERTS = 64
TOP_K = 8
N_GROUPS = 8
TOPK_GROUPS = 4
FF = 256
ROUTED_SCALE = 2.5
MOE_BLOCK = 1024
RET_CHUNK = 128
RET_UNROLL = 2
RET_NORM_ROWS = 256
A_WINDOW = 128

NP_TOK = BATCH * SEQ
NS_TOK = DEC_BATCH * DEC_SEQ
N_TOK = NP_TOK + NS_TOK
N_ASSIGN = N_TOK * TOP_K
N_MOE_BLOCKS = (N_ASSIGN + N_EXPERTS * (MOE_BLOCK - 1) + MOE_BLOCK - 1) // MOE_BLOCK
PLAN_LANES = 512
assert N_TOK % MOE_BLOCK == 0 and N_MOE_BLOCKS <= PLAN_LANES

LANES = 128
TM = 512
NA_ROWS = 8
V7X_VMEM_LIMIT = 56 * 1024 * 1024

NT_DIMS = (((1,), (1,)), ((), ()))


def _cparams(n_axes, vmem=V7X_VMEM_LIMIT):
    return pltpu.CompilerParams(dimension_semantics=("arbitrary",) * n_axes, vmem_limit_bytes=vmem)


def _seg_of_block(i, rows):
    row0 = i * rows
    return jnp.where(row0 < NP_TOK, 0, 1 + (row0 - NP_TOK) // DEC_SEQ)


def _mod_spec(chunk, rows=TM, first_block=0):
    return pl.BlockSpec((None, 1, D), lambda i: (_seg_of_block(i + first_block, rows), 0, chunk))


def _pair_specs(width, rows=TM):
    npb = NP_TOK // rows
    nsb = NS_TOK // rows
    return [pl.BlockSpec((rows, width), lambda i: (jnp.minimum(i, npb - 1), 0)),
            pl.BlockSpec((rows, width), lambda i: (jnp.clip(i - npb, 0, nsb - 1), 0))]


def _pick_rows(p_ref, s_ref, rows=TM):
    return jnp.where(pl.program_id(0) < NP_TOK // rows, p_ref[...], s_ref[...])


def _silu(x):
    return x * jax.nn.sigmoid(x)


def _rms(x, g):
    return x * lax.rsqrt(jnp.mean(x * x, axis=-1, keepdims=True) + EPS) * g


def _lane_lo():
    return lax.broadcasted_iota(jnp.int32, (1, LANES), 1) < HD


def _ada_kernel(c_ref, w_ref, b_ref, o_ref):
    a = _silu(c_ref[...])
    o_ref[...] = jnp.dot(a, w_ref[...], preferred_element_type=F32, precision=HIGHEST) + b_ref[...]


def _ada(cc, w, b, layer):
    tn = 1536
    out = pl.pallas_call(
        _ada_kernel,
        grid=(6 * D // tn,),
        in_specs=[pl.BlockSpec((8, D), lambda j: (0, 0)),
                  pl.BlockSpec((None, D, tn), lambda j: (layer, 0, j)),
                  pl.BlockSpec((None, 1, tn), lambda j: (layer, 0, j))],
        out_specs=pl.BlockSpec((8, tn), lambda j: (0, j)),
        out_shape=jax.ShapeDtypeStruct((8, 6 * D), F32),
        compiler_params=_cparams(1),
        name="ada",
    )(cc, w, b.reshape(b.shape[0], 1, 6 * D))
    return out.reshape(8, 1, 6 * D)


def _inproj_kernel(xp_ref, xs_ref, g_ref, shift_ref, scale_ref, w_ref, cos_ref, sin_ref, o_ref, *, rope_cols):
    h = _rms(_pick_rows(xp_ref, xs_ref), g_ref[...]) * (1.0 + scale_ref[...]) + shift_ref[...]
    o = jnp.dot(h.astype(BF16), w_ref[...], preferred_element_type=F32)
    if rope_cols:
        cos = cos_ref[...]
        sin = sin_ref[...]
        lane = lax.broadcasted_iota(jnp.int32, (1, LANES), 1)
        first = (lane % 32) < 16
        for c in range(rope_cols // LANES):
            oc = o[:, c * LANES:(c + 1) * LANES]
            partner = jnp.where(first, pltpu.roll(oc, LANES - 16, 1), pltpu.roll(oc, 16, 1))
            o_ref[:, c * LANES:(c + 1) * LANES] = oc * cos + partner * sin
        o_ref[:, rope_cols:] = o[:, rope_cols:]
    else:
        o_ref[...] = o


def _rope_tables():
    half = HD // 2
    inv = ROPE_BASE ** (-jnp.arange(0, half, 2, dtype=F32) / half)
    t = jnp.arange(DEC_SEQ)
    ang_r = (t // GRID_W).astype(F32)[:, None] * inv[None]
    ang_c = (t % GRID_W).astype(F32)[:, None] * inv[None]

    def head(fn_r, fn_c, sign):
        return jnp.concatenate([sign[0] * fn_r, sign[1] * fn_r, sign[0] * fn_c, sign[1] * fn_c], axis=-1)

    cos = head(jnp.cos(ang_r), jnp.cos(ang_c), (1.0, 1.0))
    sin = head(jnp.sin(ang_r), jnp.sin(ang_c), (-1.0, 1.0))
    cos = jnp.concatenate([jnp.ones((TM, HD), F32), cos], axis=0)
    sin = jnp.concatenate([jnp.zeros((TM, HD), F32), sin], axis=0)
    return jnp.tile(cos, (1, 2)), jnp.tile(sin, (1, 2))


def _inproj(x, g, mod, w_bf16, rope, rope_cols):
    n_out = w_bf16.shape[1]
    npb = NP_TOK // TM
    spb = DEC_SEQ // TM

    def rope_map(i):
        return (jnp.where(i < npb, 0, 1 + (i - npb) % spb), 0)

    return pl.pallas_call(
        functools.partial(_inproj_kernel, rope_cols=rope_cols),
        grid=(N_TOK // TM,),
        in_specs=_pair_specs(D) + [
                  pl.BlockSpec((1, D), lambda i: (0, 0)),
                  _mod_spec(0), _mod_spec(1),
                  pl.BlockSpec((D, n_out), lambda i: (0, 0)),
                  pl.BlockSpec((TM, LANES), rope_map),
                  pl.BlockSpec((TM, LANES), rope_map)],
        out_specs=pl.BlockSpec((TM, n_out), lambda i: (i, 0)),
        out_shape=jax.ShapeDtypeStruct((N_TOK, n_out), F32),
        compiler_params=_cparams(1),
        name="inproj",
    )(x[0], x[1], g.reshape(1, D), mod, mod, w_bf16, rope[0], rope[1])


def _softmax_av(s_list, v_list, sink=None):
    mx = s_list[0].max(axis=-1, keepdims=True)
    for s in s_list[1:]:
        mx = jnp.maximum(mx, s.max(axis=-1, keepdims=True))
    if sink is not None:
        mx = jnp.maximum(mx, sink)
    den = jnp.exp(sink - mx) if sink is not None else 0.0
    acc = None
    for s, v in zip(s_list, v_list):
        p = jnp.exp(s - mx)
        den = den + p.sum(axis=-1, keepdims=True)
        pv = jnp.dot(p.astype(BF16), v, preferred_element_type=F32)
        acc = pv if acc is None else acc + pv
    return acc / den


def _dup_half(x, j, lo):
    xr = pltpu.roll(x, HD, 1)
    return jnp.where(lo, x, xr) if j == 0 else jnp.where(lo, xr, x)


def _stack_heads(q_ref, heads, lo, scale):
    parts = []
    for h in heads:
        qp = q_ref[:, (h // 2) * LANES:(h // 2 + 1) * LANES]
        keep = lo if h % 2 == 0 else jnp.logical_not(lo)
        parts.append(jnp.where(keep, qp, 0.0) * scale)
    return jnp.concatenate(parts, axis=0).astype(BF16)


def _sink_column(sink_ref, heads, rows):
    return jnp.concatenate([jnp.full((rows, 1), sink_ref[h], F32) for h in heads], axis=0)


def _ctx_gqa_kernel(sink_ref, q_ref, k_ref, v_ref, o_ref):
    lo = _lane_lo()
    k = k_ref[...]
    v = v_ref[...]
    group = A_HEADS // A_KV_HEADS
    scores = []
    for j in range(A_KV_HEADS):
        q = _stack_heads(q_ref, list(range(group * j, group * (j + 1))), lo, HD ** -0.5)
        scores.append(lax.dot_general(q, _dup_half(k, j, lo).astype(BF16), NT_DIMS, preferred_element_type=F32))
    s = jnp.concatenate(scores, axis=0)
    sink = _sink_column(sink_ref, list(range(A_HEADS)), SEQ)
    mx = jnp.maximum(s.max(axis=-1, keepdims=True), sink)
    e = jnp.exp(s - mx)
    den = jnp.exp(sink - mx) + e.sum(axis=-1, keepdims=True)
    e = e.astype(BF16)
    rows_per_group = group * SEQ
    for j in range(A_KV_HEADS):
        rows = slice(j * rows_per_group, (j + 1) * rows_per_group)
        o = jnp.dot(e[rows], _dup_half(v, j, lo).astype(BF16), preferred_element_type=F32) / den[rows]
        for t in range(group // 2):
            pair = (group * j) // 2 + t
            o_ref[:, pair * LANES:(pair + 1) * LANES] = jnp.where(
                lo, o[(2 * t) * SEQ:(2 * t + 1) * SEQ], o[(2 * t + 1) * SEQ:(2 * t + 2) * SEQ])


def _ctx_gqa(p, sink):
    return pl.pallas_call(
        _ctx_gqa_kernel,
        grid_spec=pltpu.PrefetchScalarGridSpec(
            num_scalar_prefetch=1,
            grid=(BATCH,),
            in_specs=[pl.BlockSpec((SEQ, A_Q), lambda b, s: (b, 0)),
                      pl.BlockSpec((SEQ, A_KV), lambda b, s: (b, A_Q // A_KV)),
                      pl.BlockSpec((SEQ, A_KV), lambda b, s: (b, A_Q // A_KV + 1))],
            out_specs=pl.BlockSpec((SEQ, A_Q), lambda b, s: (b, 0))),
        out_shape=jax.ShapeDtypeStruct((NP_TOK, A_Q), F32),
        compiler_params=_cparams(1),
        name="ctx_gqa",
    )(sink, p, p, p)


def _win_kernel(sink_ref, q_ref, kp_ref, kc_ref, kn_ref, vp_ref, vc_ref, vn_ref, ck_ref, cv_ref, o_ref):
    i = pl.program_id(1)
    lo = _lane_lo()
    k = jnp.concatenate([kp_ref[...], kc_ref[...], kn_ref[...]], axis=0)
    v = jnp.concatenate([vp_ref[...], vc_ref[...], vn_ref[...]], axis=0)
    ck = ck_ref[...]
    cv = cv_ref[...]
    group = A_HEADS // A_KV_HEADS
    n_keys = WIN_Q + 2 * A_WINDOW
    qpos = i * WIN_Q + lax.broadcasted_iota(jnp.int32, (WIN_Q, n_keys), 0)
    kpos = i * WIN_Q - A_WINDOW + lax.broadcasted_iota(jnp.int32, (WIN_Q, n_keys), 1)
    valid = (jnp.abs(kpos - qpos) <= A_WINDOW) & (kpos >= 0) & (kpos < DEC_SEQ)
    valid = jnp.concatenate([valid] * group, axis=0)
    s_loc, s_ctx, values = [], [], []
    for j in range(A_KV_HEADS):
        heads = list(range(group * j, group * (j + 1)))
        kd = _dup_half(k, j, lo).astype(BF16)
        ckd = _dup_half(ck, j, lo).astype(BF16)
        values.append((_dup_half(v, j, lo).astype(BF16), _dup_half(cv, j, lo).astype(BF16)))
        q = _stack_heads(q_ref, heads, lo, HD ** -0.5)
        s_loc.append(jnp.where(valid, lax.dot_general(q, kd, NT_DIMS, preferred_element_type=F32), NEG))
        s_ctx.append(lax.dot_general(q, ckd, NT_DIMS, preferred_element_type=F32))
    s_loc = jnp.concatenate(s_loc, axis=0)
    s_ctx = jnp.concatenate(s_ctx, axis=0)
    sink = _sink_column(sink_ref, list(range(A_HEADS)), WIN_Q)
    mx = jnp.maximum(jnp.maximum(s_loc.max(axis=-1, keepdims=True), s_ctx.max(axis=-1, keepdims=True)), sink)
    p_loc = jnp.exp(s_loc - mx)
    p_ctx = jnp.exp(s_ctx - mx)
    den = p_loc.sum(axis=-1, keepdims=True) + p_ctx.sum(axis=-1, keepdims=True) + jnp.exp(sink - mx)
    p_loc = p_loc.astype(BF16)
    p_ctx = p_ctx.astype(BF16)
    rows_per_group = group * WIN_Q
    for j, (vd, cvd) in enumerate(values):
        rows = slice(j * rows_per_group, (j + 1) * rows_per_group)
        o = (jnp.dot(p_loc[rows], vd, preferred_element_type=F32)
             + jnp.dot(p_ctx[rows], cvd, preferred_element_type=F32)) / den[rows]
        for t in range(group // 2):
            pair = (group * j) // 2 + t
            o_ref[:, pair * LANES:(pair + 1) * LANES] = jnp.where(
                lo, o[(2 * t) * WIN_Q:(2 * t + 1) * WIN_Q], o[(2 * t + 1) * WIN_Q:(2 * t + 2) * WIN_Q])


WIN_Q = 256


def _win_attention(p, cache_k, cache_v, sink):
    nblk = DEC_SEQ // WIN_Q
    side = WIN_Q // A_WINDOW
    nside = DEC_SEQ // A_WINDOW
    base = NP_TOK // WIN_Q
    side_base = NP_TOK // A_WINDOW
    kcol = A_Q // A_KV

    def main_spec(col):
        return pl.BlockSpec((WIN_Q, A_KV), lambda b, i, s: (base + b * nblk + i, col))

    def side_spec(col, off):
        return pl.BlockSpec((A_WINDOW, A_KV),
                            lambda b, i, s: (side_base + b * nside + jnp.clip(side * i + off, 0, nside - 1), col))

    ctx_spec = pl.BlockSpec((None, PAST, A_KV), lambda b, i, s: (b, 0, 0))
    return pl.pallas_call(
        _win_kernel,
        grid_spec=pltpu.PrefetchScalarGridSpec(
            num_scalar_prefetch=1,
            grid=(DEC_BATCH, nblk),
            in_specs=[pl.BlockSpec((WIN_Q, A_Q), lambda b, i, s: (base + b * nblk + i, 0)),
                      side_spec(kcol, -1), main_spec(kcol), side_spec(kcol, side),
                      side_spec(kcol + 1, -1), main_spec(kcol + 1), side_spec(kcol + 1, side),
                      ctx_spec, ctx_spec],
            out_specs=pl.BlockSpec((WIN_Q, A_Q), lambda b, i, s: (b * nblk + i, 0))),
        out_shape=jax.ShapeDtypeStruct((NS_TOK, A_Q), F32),
        compiler_params=_cparams(2),
        name="win_attn",
    )(sink, p, p, p, p, p, p, p, cache_k.reshape(DEC_BATCH, PAST, A_KV), cache_v.reshape(DEC_BATCH, PAST, A_KV))


def _ret_kernel(df_ref, db_ref, q_ref, k_ref, v_ref, g_ref, gn_ref, s0f_ref, s0b_ref,
                o_ref, sf_ref, sb_ref, of_scr, ob_scr, *, length):
    c_len = RET_CHUNK
    n = length // c_len
    lo = _lane_lo()
    hi = jnp.logical_not(lo)
    row = lax.broadcasted_iota(jnp.int32, (c_len, c_len), 0)
    col = lax.broadcasted_iota(jnp.int32, (c_len, c_len), 1)
    rowp = lax.broadcasted_iota(jnp.int32, (LANES, LANES), 0)
    colp = lax.broadcasted_iota(jnp.int32, (LANES, LANES), 1)
    blockdiag = (rowp < HD) == (colp < HD)
    idx = lax.broadcasted_iota(jnp.int32, (c_len, 1), 0).astype(F32)

    def direction(dec_ref, forward):
        lg = -jnp.exp(dec_ref[...])
        diff = (row - col) if forward else (col - row)
        keep = (diff >= 0) if forward else (diff > 0)
        dist = jnp.maximum(diff, 0).astype(F32)
        dm = jnp.concatenate([jnp.where(keep, jnp.exp(dist * lg[:, off:off + 1]), 0.0) for off in (0, HD)], axis=0)
        if forward:
            xi = jnp.exp((idx + 1.0) * lg)
            zeta = jnp.exp((c_len - 1.0 - idx) * lg)
        else:
            xi = jnp.exp((c_len - idx) * lg)
            zeta = jnp.exp(idx * lg)
        return dm, xi, zeta, jnp.exp(c_len * lg)

    def chunk(c, state, consts):
        dm, xi, zeta, gch = consts
        rows = pl.ds(pl.multiple_of(c * c_len, c_len), c_len)
        qc = q_ref[rows, :]
        kc = k_ref[rows, :] * HD ** -0.5
        vc = v_ref[rows, :].astype(BF16)
        kb = kc.astype(BF16)
        q2 = jnp.concatenate([jnp.where(lo, qc, 0.0), jnp.where(hi, qc, 0.0)], axis=0).astype(BF16)
        inner = lax.dot_general(q2, kb, NT_DIMS, preferred_element_type=F32) * dm
        kz_t = (kc * zeta).T
        res = jnp.dot(jnp.concatenate([inner, kz_t], axis=0).astype(BF16), vc, preferred_element_type=F32)
        cross = jnp.dot(qc.astype(BF16), state.astype(BF16), preferred_element_type=F32) * xi
        o = jnp.where(lo, res[:c_len], res[c_len:2 * c_len]) + cross
        state = gch * state + jnp.where(blockdiag, res[2 * c_len:], 0.0)
        return rows, o, state

    cf = direction(df_ref, True)
    cb = direction(db_ref, False)

    def scan_body(t, states):
        rows_f, o_f, state_f = chunk(t, states[0], cf)
        of_scr[rows_f, :] = o_f
        rows_b, o_b, state_b = chunk(n - 1 - t, states[1], cb)
        ob_scr[rows_b, :] = o_b
        return state_f, state_b

    state_f, state_b = lax.fori_loop(0, n, scan_body, (s0f_ref[...], s0b_ref[...]), unroll=min(n, RET_UNROLL))
    sf_ref[...] = state_f
    sb_ref[...] = state_b

    gn = gn_ref[...]
    n_norm = length // RET_NORM_ROWS

    def per_head(x):
        a = jnp.where(lo, x, 0.0).sum(axis=-1, keepdims=True)
        b = jnp.where(hi, x, 0.0).sum(axis=-1, keepdims=True)
        return jnp.where(lo, a, b) * (1.0 / HD)

    def norm_body(t, carry):
        rows = pl.ds(pl.multiple_of(t * RET_NORM_ROWS, RET_NORM_ROWS), RET_NORM_ROWS)
        o = of_scr[rows, :] + ob_scr[rows, :]
        d = o - per_head(o)
        y = d * lax.rsqrt(per_head(d * d) + EPS) * gn
        o_ref[rows, :] = _silu(g_ref[rows, :]) * y
        return carry

    lax.fori_loop(0, n_norm, norm_body, 0)


def _pair_lanes(v):
    return jnp.repeat(v.astype(F32), HD).reshape(B_HEADS // 2, 1, LANES)


def _blockdiag_states(s):
    b = s.shape[0]
    s = s.astype(F32).reshape(b, B_HEADS // 2, 2, HD, HD)
    z = jnp.zeros_like(s[:, :, 0])
    top = jnp.concatenate([s[:, :, 0], z], axis=-1)
    bot = jnp.concatenate([z, s[:, :, 1]], axis=-1)
    return jnp.concatenate([top, bot], axis=-2)


def _diag_states(sp):
    b = sp.shape[0]
    s = jnp.stack([sp[:, :, :HD, :HD], sp[:, :, HD:, HD:]], axis=2)
    return s.reshape(b, B_HEADS, HD, HD)


def _retention(p, row_base, batch, length, dec_f, dec_b, gn_g, s0f, s0b):
    npairs = B_HEADS // 2
    blk0 = row_base // length
    qcol = (A_Q + 2 * A_KV) // LANES

    def col_spec(off):
        return pl.BlockSpec((length, LANES), lambda b, h: (blk0 + b, qcol + off * npairs + h))

    lane_spec = pl.BlockSpec((None, 1, LANES), lambda b, h: (h, 0, 0))
    state_spec = pl.BlockSpec((None, None, LANES, LANES), lambda b, h: (b, h, 0, 0))
    state_shape = jax.ShapeDtypeStruct((batch, npairs, LANES, LANES), F32)
    return pl.pallas_call(
        functools.partial(_ret_kernel, length=length),
        grid=(batch, npairs),
        in_specs=[lane_spec, lane_spec, col_spec(0), col_spec(1), col_spec(2), col_spec(3), lane_spec,
                  state_spec, state_spec],
        out_specs=[pl.BlockSpec((length, LANES), lambda b, h: (b, h)), state_spec, state_spec],
        out_shape=[jax.ShapeDtypeStruct((batch * length, B_W), F32), state_shape, state_shape],
        scratch_shapes=[pltpu.VMEM((length, LANES), F32), pltpu.VMEM((length, LANES), F32)],
        compiler_params=_cparams(2),
        name="retention",
    )(_pair_lanes(dec_f), _pair_lanes(dec_b), p, p, p, p, gn_g.reshape(npairs, 1, LANES), s0f, s0b)


def _ctx_mha_kernel(q_ref, k_ref, v_ref, o_ref):
    lo = _lane_lo()
    for pair in range(C_HEADS // 2):
        cols = slice(pair * LANES, (pair + 1) * LANES)
        q = _stack_heads(q_ref, [2 * pair, 2 * pair + 1], lo, HD ** -0.5)
        s = lax.dot_general(q, k_ref[:, cols].astype(BF16), NT_DIMS, preferred_element_type=F32)
        o = _softmax_av([s], [v_ref[:, cols].astype(BF16)])
        o_ref[:, cols] = jnp.where(lo, o[:SEQ], o[SEQ:])


def _ctx_mha(p):
    return pl.pallas_call(
        _ctx_mha_kernel,
        grid=(BATCH,),
        in_specs=[pl.BlockSpec((SEQ, C_W), lambda b: (b, 0)),
                  pl.BlockSpec((SEQ, C_W), lambda b: (b, 1)),
                  pl.BlockSpec((SEQ, C_W), lambda b: (b, 2))],
        out_specs=pl.BlockSpec((SEQ, C_W), lambda b: (b, 0)),
        out_shape=jax.ShapeDtypeStruct((NP_TOK, C_W), F32),
        compiler_params=_cparams(1),
        name="ctx_mha",
    )(p, p, p)


NA_WIN_ROWS = 2 * NA_ROWS
NA_WIN = NA_WIN_ROWS * GRID_W
NA_QROWS = NA_ROWS * GRID_W
NA_PAD_ROWS = NA_KH // 2
NA_TABLE = 1536


def _na_kernel(q_ref, kp_ref, km_ref, kn_ref, vp_ref, vm_ref, vn_ref, ck_ref, cv_ref, ue_ref, uo_ref, o_ref):
    r0 = pl.program_id(2) * NA_ROWS
    n_rows = DEC_SEQ // GRID_W
    lo = _lane_lo()
    k = jnp.concatenate([kp_ref[...], km_ref[...], kn_ref[...]], axis=0).astype(BF16)
    v = jnp.concatenate([vp_ref[...], vm_ref[...], vn_ref[...]], axis=0).astype(BF16)
    ck = ck_ref[...].astype(BF16)
    cv = cv_ref[...].astype(BF16)
    q = q_ref[...] * HD ** -0.5
    klane = lax.broadcasted_iota(jnp.int32, (1, NA_WIN), 1)
    outs = []
    for half, keep in enumerate((lo, jnp.logical_not(lo))):
        qh = jnp.where(keep, q, 0.0).astype(BF16)
        s = lax.dot_general(qh, k, NT_DIMS, preferred_element_type=F32)
        s_ctx = lax.dot_general(qh, ck, NT_DIMS, preferred_element_type=F32)
        p_loc, p_ctx, den = [], [], []
        for rq in range(NA_ROWS):
            rows = slice(rq * GRID_W, (rq + 1) * GRID_W)
            start = NA_KH - 1 - rq
            if start % 2 == 0:
                u = ue_ref[half, :, start * GRID_W:start * GRID_W + NA_WIN]
            else:
                u = uo_ref[half, :, (start - 1) * GRID_W:(start - 1) * GRID_W + NA_WIN]
            r = r0 + rq
            first = jnp.clip(r - NA_KH // 2, 0, n_rows - NA_KH)
            lane0 = (first - r0 + NA_PAD_ROWS) * GRID_W
            in_rows = (klane >= lane0) & (klane < lane0 + NA_KH * GRID_W)
            sl = jnp.where(in_rows, s[rows] + u, NEG)
            sc = s_ctx[rows]
            mx = jnp.maximum(sl.max(axis=-1, keepdims=True), sc.max(axis=-1, keepdims=True))
            el = jnp.exp(sl - mx)
            ec = jnp.exp(sc - mx)
            den.append(el.sum(axis=-1, keepdims=True) + ec.sum(axis=-1, keepdims=True))
            p_loc.append(el.astype(BF16))
            p_ctx.append(ec.astype(BF16))
        acc = (jnp.dot(jnp.concatenate(p_loc, axis=0), v, preferred_element_type=F32)
               + jnp.dot(jnp.concatenate(p_ctx, axis=0), cv, preferred_element_type=F32))
        outs.append(acc / jnp.concatenate(den, axis=0))
    o_ref[...] = jnp.where(lo, outs[0], outs[1])


def _na_bias_tables(rpb):
    cq = jnp.arange(GRID_W)
    ck = jnp.arange(GRID_W)
    dc = jnp.clip(ck[None] - cq[:, None], -(NA_KW - 1), NA_KW - 1) + NA_KW - 1
    cs = jnp.clip(cq - NA_KW // 2, 0, GRID_W - NA_KW)
    col_ok = (ck[None] >= cs[:, None]) & (ck[None] < cs[:, None] + NA_KW)
    t = rpb.astype(F32)[:, :, dc]
    t = jnp.where(col_ok[None, None], t, NEG).transpose(0, 2, 1, 3)
    n_dr = 2 * NA_KH - 1
    blocks = NA_TABLE // GRID_W
    t = jnp.pad(t, ((0, 0), (0, 0), (NA_PAD_ROWS, blocks - n_dr - NA_PAD_ROWS), (0, 0)), constant_values=NEG)
    ue = t.reshape(C_HEADS, GRID_W, NA_TABLE)
    uo = jnp.concatenate([ue[..., GRID_W:], jnp.full((C_HEADS, GRID_W, GRID_W), NEG, F32)], axis=-1)
    return ue, uo


def _na_attention(p, cache_k, cache_v, rpb):
    npairs = C_HEADS // 2
    nrb = DEC_SEQ // NA_QROWS
    half = NA_QROWS // 2
    qbase = NP_TOK // NA_QROWS
    hbase = NP_TOK // half
    kcol = C_W // LANES
    ue, uo = _na_bias_tables(rpb)

    def main_spec(col0):
        return pl.BlockSpec((NA_QROWS, LANES), lambda b, h, r: (qbase + b * nrb + r, col0 + h))

    def side_spec(col0, off):
        return pl.BlockSpec((half, LANES),
                            lambda b, h, r: (hbase + b * 2 * nrb + jnp.clip(2 * r + off, 0, 2 * nrb - 1), col0 + h))

    ctx_spec = pl.BlockSpec((None, PAST, LANES), lambda b, h, r: (b, 0, h))
    tab_spec = pl.BlockSpec((2, GRID_W, NA_TABLE), lambda b, h, r: (h, 0, 0))
    return pl.pallas_call(
        _na_kernel,
        grid=(DEC_BATCH, npairs, nrb),
        in_specs=[main_spec(0),
                  side_spec(kcol, -1), main_spec(kcol), side_spec(kcol, 2),
                  side_spec(2 * kcol, -1), main_spec(2 * kcol), side_spec(2 * kcol, 2),
                  ctx_spec, ctx_spec, tab_spec, tab_spec],
        out_specs=pl.BlockSpec((NA_QROWS, LANES), lambda b, h, r: (b * nrb + r, h)),
        out_shape=jax.ShapeDtypeStruct((NS_TOK, C_W), F32),
        compiler_params=_cparams(3),
        name="na_attn",
    )(p, p, p, p, p, p, p, cache_k.reshape(DEC_BATCH, PAST, C_W), cache_v.reshape(DEC_BATCH, PAST, C_W), ue, uo)


def _route(biased, scores):
    t = biased.shape[1]
    per_group = N_EXPERTS // N_GROUPS
    i8 = lax.broadcasted_iota(jnp.int32, (per_group, t), 0)
    g_rows = []
    for g in range(N_GROUPS):
        bg = biased[g * per_group:(g + 1) * per_group]
        m1 = bg.max(axis=0, keepdims=True)
        first = jnp.where(bg == m1, i8, per_group).min(axis=0, keepdims=True)
        m2 = jnp.where(i8 == first, -jnp.inf, bg).max(axis=0, keepdims=True)
        g_rows.append(m1 + m2)
    g_top = jnp.concatenate(g_rows, axis=0)
    gi = lax.broadcasted_iota(jnp.int32, g_top.shape, 0)
    g_sel = jnp.zeros(g_top.shape, jnp.int32)
    cur = g_top
    for _ in range(TOPK_GROUPS):
        m = cur.max(axis=0, keepdims=True)
        hit = gi == jnp.where(cur == m, gi, N_GROUPS).min(axis=0, keepdims=True)
        g_sel = jnp.where(hit, 1, g_sel)
        cur = jnp.where(hit, -jnp.inf, cur)
    e_sel = jnp.concatenate([jnp.broadcast_to(g_sel[g:g + 1], (per_group, t)) for g in range(N_GROUPS)], axis=0)
    cur = jnp.where(e_sel > 0, biased, NEG)
    ei = lax.broadcasted_iota(jnp.int32, cur.shape, 0)
    ids, gates, hits = [], [], []
    for _ in range(TOP_K):
        m = cur.max(axis=0, keepdims=True)
        f = jnp.where(cur == m, ei, N_EXPERTS).min(axis=0, keepdims=True)
        hit = ei == f
        ids.append(f)
        hits.append(hit)
        gates.append(jnp.where(hit, scores, 0.0).sum(axis=0, keepdims=True))
        cur = jnp.where(hit, -jnp.inf, cur)
    gate = jnp.concatenate(gates, axis=0)
    gate = gate / gate.sum(axis=0, keepdims=True) * ROUTED_SCALE
    return jnp.concatenate(ids, axis=0), gate, hits


def _pack_bf16_pairs(h):
    bits = lax.bitcast_convert_type(h.astype(BF16).astype(F32), jnp.uint32)
    return bits[:, :D // 2] | (bits[:, D // 2:] >> 16)


def _unpack_bf16_pairs(xp):
    hi = lax.bitcast_convert_type(xp & jnp.uint32(0xFFFF0000), F32).astype(BF16)
    lo = lax.bitcast_convert_type(xp << 16, F32).astype(BF16)
    return hi, lo


def _dot_halves(hi, lo, w_ref):
    return (jnp.dot(hi, w_ref[:D // 2, :], preferred_element_type=F32)
            + jnp.dot(lo, w_ref[D // 2:, :], preferred_element_type=F32))


def _outproj_kernel(*refs, n_parts):
    xp_ref, xs_ref = refs[:2]
    part_refs = refs[2:2 + 3 * n_parts]
    gate_ref, shift_ref, scale_ref, g2_ref, rw_ref, rb_ref = refs[2 + 3 * n_parts:8 + 3 * n_parts]
    xo_ref, h_ref, dest_ref, wgt_ref, plan_ref, cnt_ref = refs[8 + 3 * n_parts:]
    step = pl.program_id(0)

    @pl.when(step == 0)
    def _():
        cnt_ref[...] = jnp.zeros_like(cnt_ref)
        plan_ref[...] = jnp.zeros_like(plan_ref)

    y = None
    for t in range(n_parts):
        ap_ref, as_ref, w_ref = part_refs[3 * t:3 * t + 3]
        d = jnp.dot(_pick_rows(ap_ref, as_ref).astype(BF16), w_ref[...], preferred_element_type=F32)
        y = d if y is None else y + d
    x = _pick_rows(xp_ref, xs_ref) + gate_ref[...] * y
    xo_ref[...] = x
    h = _rms(x, g2_ref[...]) * (1.0 + scale_ref[...]) + shift_ref[...]
    h_ref[...] = _pack_bf16_pairs(h)
    h_hi = h.astype(BF16)
    h_lo = (h - h_hi.astype(F32)).astype(BF16)
    rw = rw_ref[...]
    rw_hi = rw.astype(BF16)
    rw_lo = (rw - rw_hi.astype(F32)).astype(BF16)
    logits = (lax.dot_general(rw_hi, h_hi, NT_DIMS, preferred_element_type=F32)
              + lax.dot_general(rw_hi, h_lo, NT_DIMS, preferred_element_type=F32)
              + lax.dot_general(rw_lo, h_hi, NT_DIMS, preferred_element_type=F32))
    scores = jax.nn.sigmoid(logits)
    _, gate, hits = _route(scores + rb_ref[...], scores)
    wgt_ref[...] = gate
    chosen = hits[0]
    for hit in hits[1:]:
        chosen = chosen | hit
    m = jnp.where(chosen, 1.0, 0.0)
    before = (lax.broadcasted_iota(jnp.int32, (TM, TM), 0) < lax.broadcasted_iota(jnp.int32, (TM, TM), 1))
    prefix = jnp.dot(m.astype(BF16), jnp.where(before, 1.0, 0.0).astype(BF16), preferred_element_type=F32)
    e_base = (lax.broadcasted_iota(jnp.int32, (N_EXPERTS, 1), 0) * N_TOK).astype(F32)
    row_all = prefix + (cnt_ref[...] + e_base)
    dest_ref[...] = jnp.concatenate(
        [jnp.where(hit, row_all, 0.0).sum(axis=0, keepdims=True) for hit in hits], axis=0).astype(jnp.int32)
    cnt_ref[...] += m.sum(axis=1, keepdims=True)

    @pl.when(step == pl.num_programs(0) - 1)
    def _():
        _block_plan(cnt_ref[...], plan_ref)


def _block_plan(counts, plan_ref):
    cap_blocks = N_TOK // MOE_BLOCK
    nblk = ((counts.astype(jnp.int32) + (MOE_BLOCK - 1)) // MOE_BLOCK).astype(F32)
    lower = (lax.broadcasted_iota(jnp.int32, (N_EXPERTS, N_EXPERTS), 0)
             >= lax.broadcasted_iota(jnp.int32, (N_EXPERTS, N_EXPERTS), 1))
    cum = jnp.dot(jnp.where(lower, 1.0, 0.0).astype(BF16), jnp.broadcast_to(nblk, (N_EXPERTS, LANES)).astype(BF16),
                  preferred_element_type=F32)[:, :1]
    n_used = cum[N_EXPERTS - 1:, :]
    slot = jnp.minimum(lax.broadcasted_iota(jnp.int32, (1, PLAN_LANES), 1).astype(F32), n_used - 1.0)
    done = cum <= slot
    expert = jnp.where(done, 1.0, 0.0).sum(axis=0, keepdims=True)
    blocks_before = jnp.where(done, nblk, 0.0).sum(axis=0, keepdims=True)
    plan_ref[0:1, :] = (expert * cap_blocks + (slot - blocks_before)).astype(jnp.int32)
    plan_ref[1:2, :] = expert.astype(jnp.int32)
    plan_ref[2:3, :] = jnp.broadcast_to(n_used, (1, PLAN_LANES)).astype(jnp.int32)


def _outproj(x, parts, mod, g2, router_w, router_b):
    in_specs = _pair_specs(D)
    args = [x[0], x[1]]
    for ap, a_s, w in parts:
        width = ap.shape[1]
        in_specs += _pair_specs(width) + [pl.BlockSpec((width, D), lambda i: (0, 0))]
        args += [ap, a_s, w]
    in_specs += [_mod_spec(2), _mod_spec(3), _mod_spec(4),
                 pl.BlockSpec((1, D), lambda i: (0, 0)),
                 pl.BlockSpec((N_EXPERTS, D), lambda i: (0, 0)),
                 pl.BlockSpec((N_EXPERTS, 1), lambda i: (0, 0))]
    args += [mod, mod, mod, g2.reshape(1, D), router_w.T, router_b.reshape(N_EXPERTS, 1)]
    return pl.pallas_call(
        functools.partial(_outproj_kernel, n_parts=len(parts)),
        grid=(N_TOK // TM,),
        in_specs=in_specs,
        out_specs=[pl.BlockSpec((TM, D), lambda i: (i, 0)),
                   pl.BlockSpec((TM, D // 2), lambda i: (i, 0)),
                   pl.BlockSpec((TOP_K, TM), lambda i: (0, i)),
                   pl.BlockSpec((TOP_K, TM), lambda i: (0, i)),
                   pl.BlockSpec((8, PLAN_LANES), lambda i: (0, 0))],
        out_shape=[jax.ShapeDtypeStruct((N_TOK, D), F32),
                   jax.ShapeDtypeStruct((N_TOK, D // 2), jnp.uint32),
                   jax.ShapeDtypeStruct((TOP_K, N_TOK), jnp.int32),
                   jax.ShapeDtypeStruct((TOP_K, N_TOK), F32),
                   jax.ShapeDtypeStruct((8, PLAN_LANES), jnp.int32)],
        scratch_shapes=[pltpu.VMEM((N_EXPERTS, 1), F32)],
        compiler_params=_cparams(1),
        name="outproj_router",
    )(*args)


def _experts_kernel(br_ref, be_ref, nu_ref, x_ref, w1_ref, w3_ref, w2_ref, o_ref, w1b, w3b, w2b):
    i = pl.program_id(0)
    e = be_ref[i]
    prev = be_ref[jnp.maximum(i - 1, 0)]

    @pl.when((i == 0) | (e != prev))
    def _():
        w1b[...] = w1_ref[...].astype(BF16)
        w3b[...] = w3_ref[...].astype(BF16)
        w2b[...] = w2_ref[...].astype(BF16)

    @pl.when(i < nu_ref[0])
    def _():
        hi, lo = _unpack_bf16_pairs(x_ref[...])
        a = _dot_halves(hi, lo, w1b)
        b = _dot_halves(hi, lo, w3b)
        h = (_silu(a) * b).astype(BF16)
        o_ref[...] = _pack_bf16_pairs(jnp.dot(h, w2b[...], preferred_element_type=F32))


def _experts(plan, x_rows, w1, w3, w2, layer):
    return pl.pallas_call(
        _experts_kernel,
        grid_spec=pltpu.PrefetchScalarGridSpec(
            num_scalar_prefetch=3,
            grid=(N_MOE_BLOCKS,),
            in_specs=[pl.BlockSpec((MOE_BLOCK, D // 2), lambda i, br, be, nu: (br[i], 0)),
                      pl.BlockSpec((None, None, D, FF), lambda i, br, be, nu: (layer, be[i], 0, 0)),
                      pl.BlockSpec((None, None, D, FF), lambda i, br, be, nu: (layer, be[i], 0, 0)),
                      pl.BlockSpec((None, None, FF, D), lambda i, br, be, nu: (layer, be[i], 0, 0))],
            out_specs=pl.BlockSpec((MOE_BLOCK, D // 2), lambda i, br, be, nu: (br[i], 0)),
            scratch_shapes=[pltpu.VMEM((D, FF), BF16), pltpu.VMEM((D, FF), BF16), pltpu.VMEM((FF, D), BF16)]),
        out_shape=jax.ShapeDtypeStruct(x_rows.shape, jnp.uint32),
        compiler_params=_cparams(1),
        name="experts",
    )(plan[0], plan[1], plan[2, :1], x_rows, w1, w3, w2)


SC_CORES = 2
SC_SUBCORES = 16
SC_WORKERS = SC_CORES * SC_SUBCORES
SC_CHUNK_BYTES = 64 * 1024
SC_SLOTS = 4


def _sc_scatter(rows, dest, n_out):
    n_rows, width = rows.shape
    picks = dest.shape[0]
    chunk = SC_CHUNK_BYTES // (4 * width)
    per_worker = n_rows // SC_WORKERS
    n_chunks = per_worker // chunk
    assert per_worker * SC_WORKERS == n_rows and n_chunks * chunk == per_worker and n_chunks % 2 == 0
    mesh = plsc.VectorSubcoreMesh(core_axis_name="c", subcore_axis_name="s")

    @functools.partial(
        pl.kernel, mesh=mesh,
        out_type=jax.ShapeDtypeStruct((n_out, width), rows.dtype),
        scratch_types=[pltpu.VMEM((picks, n_chunks, chunk), jnp.int32),
                       pltpu.VMEM((2, chunk, width), rows.dtype),
                       pltpu.SemaphoreType.DMA((2,)),
                       pltpu.SemaphoreType.DMA((2,))])
    def scatter(r_hbm, d_hbm, o_hbm, idx_v, rows_v, lsem, ssem):
        worker = lax.axis_index("s") * SC_CORES + lax.axis_index("c")
        base = worker * per_worker
        for k in range(picks):
            pltpu.sync_copy(d_hbm.at[k, worker], idx_v.at[k])

        def load_copy(c, b):
            src = pl.ds(pl.multiple_of(base + c * chunk, chunk), chunk)
            return pltpu.make_async_copy(r_hbm.at[src], rows_v.at[b], lsem.at[b])

        def store_copy(c, b, k):
            return pltpu.make_async_copy(rows_v.at[b], o_hbm.at[idx_v.at[k, c]], ssem.at[b])

        load_copy(0, 0).start()

        @pl.loop(0, n_chunks, step=2)
        def _(c0):
            for b in range(2):
                c = c0 + b
                load_copy(c, b).wait()
                for k in range(picks):
                    store_copy(c, b, k).start()

                @pl.when(c > 0)
                def _():
                    for k in range(picks):
                        store_copy(c - 1, 1 - b, k).wait()

                @pl.when(c + 1 < n_chunks)
                def _():
                    load_copy(c + 1, 1 - b).start()

        for k in range(picks):
            store_copy(n_chunks - 1, 1, k).wait()

    return scatter(rows, dest.reshape(picks, SC_WORKERS, n_chunks, chunk))


def _sc_gather(table, idx):
    n_idx = idx.shape[0]
    width = table.shape[1]
    chunk = SC_CHUNK_BYTES // (4 * width)
    per_worker = n_idx // SC_WORKERS
    n_chunks = per_worker // chunk
    ahead = SC_SLOTS - 1
    assert per_worker * SC_WORKERS == n_idx and n_chunks * chunk == per_worker and n_chunks % SC_SLOTS == 0
    mesh = plsc.VectorSubcoreMesh(core_axis_name="c", subcore_axis_name="s")

    @functools.partial(
        pl.kernel, mesh=mesh,
        out_type=jax.ShapeDtypeStruct((n_idx, width), table.dtype),
        scratch_types=[pltpu.VMEM((per_worker,), jnp.int32),
                       pltpu.VMEM((SC_SLOTS, chunk, width), table.dtype),
                       pltpu.SemaphoreType.DMA((SC_SLOTS,)),
                       pltpu.SemaphoreType.DMA((SC_SLOTS,))])
    def gather(t_hbm, i_hbm, o_hbm, idx_v, rows_v, gsem, wsem):
        worker = lax.axis_index("s") * SC_CORES + lax.axis_index("c")
        base = worker * per_worker
        pltpu.sync_copy(i_hbm.at[pl.ds(pl.multiple_of(base, chunk), per_worker)], idx_v)

        def gather_copy(c, b):
            ids = idx_v.at[pl.ds(pl.multiple_of(c * chunk, chunk), chunk)]
            return pltpu.make_async_copy(t_hbm.at[ids], rows_v.at[b], gsem.at[b])

        def write_copy(c, b):
            rows = pl.ds(pl.multiple_of(base + c * chunk, chunk), chunk)
            return pltpu.make_async_copy(rows_v.at[b], o_hbm.at[rows], wsem.at[b])

        for c in range(ahead):
            gather_copy(c, c).start()

        @pl.loop(0, n_chunks, step=SC_SLOTS)
        def _(c0):
            for b in range(SC_SLOTS):
                c = c0 + b
                refill = (b + ahead) % SC_SLOTS
                gather_copy(c, b).wait()
                write_copy(c, b).start()

                @pl.when(c > 0)
                def _():
                    write_copy(c - 1, refill).wait()

                @pl.when(c + ahead < n_chunks)
                def _():
                    gather_copy(c + ahead, refill).start()

        write_copy(n_chunks - 1, (n_chunks - 1) % SC_SLOTS).wait()

    return gather(table, idx)


TC = 512


def _combine_kernel(x_ref, h_ref, y_ref, wgt_ref, gate_ref, w1_ref, w3_ref, w2_ref, fg_ref, o_ref, *, final):
    hi, lo = _unpack_bf16_pairs(h_ref[...])
    a = _dot_halves(hi, lo, w1_ref)
    b = _dot_halves(hi, lo, w3_ref)
    ffn = jnp.dot((_silu(a) * b).astype(BF16), w2_ref[...], preferred_element_type=F32)
    wgt = wgt_ref[...]
    r_hi = None
    r_lo = None
    for k in range(TOP_K):
        yk = y_ref[k]
        w = wgt[:, k:k + 1]
        t_hi = lax.bitcast_convert_type(yk & jnp.uint32(0xFFFF0000), F32) * w
        t_lo = lax.bitcast_convert_type(yk << 16, F32) * w
        r_hi = t_hi if r_hi is None else r_hi + t_hi
        r_lo = t_lo if r_lo is None else r_lo + t_lo
    x = x_ref[...] + gate_ref[...] * (ffn + jnp.concatenate([r_hi, r_lo], axis=1))
    o_ref[...] = _rms(x, fg_ref[...]) if final else x


def _combine(x, h, y_rows, wgt, mod, sw1, sw3, sw2, final_g, final):
    weights = (sw1.astype(BF16), sw3.astype(BF16), sw2.astype(BF16), final_g.reshape(1, D))

    def rows_from(first_row, n_rows):
        b0 = first_row // TC
        return pl.pallas_call(
            functools.partial(_combine_kernel, final=final),
            grid=(n_rows // TC,),
            in_specs=[pl.BlockSpec((TC, D), lambda i: (i + b0, 0)),
                      pl.BlockSpec((TC, D // 2), lambda i: (i + b0, 0)),
                      pl.BlockSpec((TOP_K, TC, D // 2), lambda i: (0, i + b0, 0)),
                      pl.BlockSpec((TC, TOP_K), lambda i: (i + b0, 0)),
                      _mod_spec(5, TC, b0),
                      pl.BlockSpec((D, FF), lambda i: (0, 0)),
                      pl.BlockSpec((D, FF), lambda i: (0, 0)),
                      pl.BlockSpec((FF, D), lambda i: (0, 0)),
                      pl.BlockSpec((1, D), lambda i: (0, 0))],
            out_specs=pl.BlockSpec((TC, D), lambda i: (i, 0)),
            out_shape=jax.ShapeDtypeStruct((n_rows, D), F32),
            compiler_params=_cparams(1),
            name="combine",
        )(x, h, y_rows, wgt, mod, *weights)

    return rows_from(0, NP_TOK), rows_from(NP_TOK, NS_TOK)


def kernel(x_prompt, x_sample, cache_a_k, cache_a_v, state_ret_fwd, state_ret_bwd, cache_c_k, cache_c_v,
           c, c_ctx, norm1_g, norm2_g, ada_w, ada_b, even_w_in, even_w_out, sink_a, ret_decay_fwd,
           ret_decay_bwd, ret_gn_g, odd_w_in, odd_w_out, na_rpb, router_w, router_b, exp_w1, exp_w3,
           exp_w2, sh_w1, sh_w3, sh_w2, final_g):
    x = (x_prompt.reshape(NP_TOK, D), x_sample.reshape(NS_TOK, D))
    cc = jnp.concatenate([c_ctx[None], c, jnp.zeros((8 - 1 - DEC_BATCH, D), F32)], axis=0)
    rope = _rope_tables()
    outs = {}
    for l in range(2):
        mod = _ada(cc, ada_w, ada_b, l)
        if l == 0:
            p = _inproj(x, norm1_g[l], mod, even_w_in[0].astype(BF16), rope, A_Q + A_KV)
            oa_p = _ctx_gqa(p, sink_a[0])
            oa_s = _win_attention(p, cache_a_k[:, 0], cache_a_v[:, 0], sink_a[0])
            zero = jnp.zeros((BATCH, B_HEADS // 2, LANES, LANES), F32)
            ob_p, sf, sb = _retention(p, 0, BATCH, SEQ, ret_decay_fwd[0], ret_decay_bwd[0], ret_gn_g[0], zero, zero)
            ob_s, _, _ = _retention(p, NP_TOK, DEC_BATCH, DEC_SEQ, ret_decay_fwd[0], ret_decay_bwd[0], ret_gn_g[0],
                                    _blockdiag_states(state_ret_fwd[:, 0]), _blockdiag_states(state_ret_bwd[:, 0]))
            w_out = even_w_out[0].astype(BF16)
            parts = [(oa_p, oa_s, w_out[:A_Q]), (ob_p, ob_s, w_out[A_Q:])]
            outs["a_k"] = p[:NP_TOK, A_Q:A_Q + A_KV].reshape(BATCH, 1, SEQ, A_KV_HEADS, HD)
            outs["a_v"] = p[:NP_TOK, A_Q + A_KV:A_Q + 2 * A_KV].reshape(BATCH, 1, SEQ, A_KV_HEADS, HD)
            outs["r_f"] = _diag_states(sf).reshape(BATCH, 1, B_HEADS, HD, HD)
            outs["r_b"] = _diag_states(sb).reshape(BATCH, 1, B_HEADS, HD, HD)
        else:
            p = _inproj(x, norm1_g[l], mod, odd_w_in[0].astype(BF16), rope, 0)
            o_p = _ctx_mha(p)
            o_s = _na_attention(p, cache_c_k[:, 0], cache_c_v[:, 0], na_rpb[0])
            parts = [(o_p, o_s, odd_w_out[0].astype(BF16))]
            outs["c_k"] = p[:NP_TOK, C_W:2 * C_W].reshape(BATCH, 1, SEQ, C_HEADS, HD)
            outs["c_v"] = p[:NP_TOK, 2 * C_W:3 * C_W].reshape(BATCH, 1, SEQ, C_HEADS, HD)
        x_mid, h, dest, gate_t, plan = _outproj(x, parts, mod, norm2_g[l], router_w[l], router_b[l])
        y = _experts(plan, _sc_scatter(h, dest, N_EXPERTS * N_TOK), exp_w1, exp_w3, exp_w2, l)
        y_rows = _sc_gather(y, dest.reshape(N_ASSIGN)).reshape(TOP_K, N_TOK, D // 2)
        x = _combine(x_mid, h, y_rows, gate_t.T, mod, sh_w1[l], sh_w3[l], sh_w2[l], final_g, final=(l == 1))
    y_prompt = x[0].reshape(BATCH, SEQ, D)
    y_sample = x[1].reshape(DEC_BATCH, DEC_SEQ, D)
    return (y_prompt, y_sample, outs["a_k"], outs["a_v"], outs["r_f"], outs["r_b"], outs["c_k"], outs["c_v"])
```

```python
import functools
import math

import jax
import jax.numpy as jnp
from jax import lax
from jax.experimental import pallas as pl
from jax.experimental.pallas import tpu as pltpu
from jax.experimental.pallas import tpu_sc as plsc

F32 = jnp.float32
BF16 = jnp.bfloat16
HIGHEST = lax.Precision.HIGHEST

D = 1024
BATCH = 32
SEQ = 256
DEC_BATCH = 4
DEC_SEQ = 4096
PAST = 256
GRID_W = 64
HD = 64
EPS = 1e-6
NEG = -1e30
ROPE_BASE = 10000.0
A_HEADS = 8
A_KV_HEADS = 2
A_Q = A_HEADS * HD
A_KV = A_KV_HEADS * HD
B_HEADS = 8
B_W = B_HEADS * HD
EVEN_IN = A_Q + 2 * A_KV + 4 * B_W
C_HEADS = 16
C_W = C_HEADS * HD
NA_KH = 8
NA_KW = 16
N_EXPERTS = 64
TOP_K = 8
N_GROUPS = 8
TOPK_GROUPS = 4
FF = 256
ROUTED_SCALE = 2.5
MOE_BLOCK = 1024
RET_CHUNK = 256
RET_UNROLL = 4
RET_NORM_ROWS = 1024
A_WINDOW = 128

NP_TOK = BATCH * SEQ
NS_TOK = DEC_BATCH * DEC_SEQ
N_TOK = NP_TOK + NS_TOK
N_ASSIGN = N_TOK * TOP_K
N_MOE_BLOCKS = (N_ASSIGN + N_EXPERTS * (MOE_BLOCK - 1) + MOE_BLOCK - 1) // MOE_BLOCK
PLAN_LANES = 512
assert N_TOK % MOE_BLOCK == 0 and N_MOE_BLOCKS <= PLAN_LANES

LANES = 128
TM = 512
NA_ROWS = 8
V7X_VMEM_LIMIT = 56 * 1024 * 1024

NT_DIMS = (((1,), (1,)), ((), ()))


def _cparams(n_axes, vmem=V7X_VMEM_LIMIT):
    return pltpu.CompilerParams(dimension_semantics=("arbitrary",) * n_axes, vmem_limit_bytes=vmem)


def _seg_of_block(i, rows):
    row0 = i * rows
    return jnp.where(row0 < NP_TOK, 0, 1 + (row0 - NP_TOK) // DEC_SEQ)


def _mod_spec(chunk, rows=TM, first_block=0):
    return pl.BlockSpec((None, 1, D), lambda i: (_seg_of_block(i + first_block, rows), 0, chunk))


def _pair_specs(width, rows=TM):
    npb = NP_TOK // rows
    nsb = NS_TOK // rows
    return [pl.BlockSpec((rows, width), lambda i: (jnp.minimum(i, npb - 1), 0)),
            pl.BlockSpec((rows, width), lambda i: (jnp.clip(i - npb, 0, nsb - 1), 0))]


def _pick_rows(p_ref, s_ref, rows=TM):
    return jnp.where(pl.program_id(0) < NP_TOK // rows, p_ref[...], s_ref[...])


def _silu(x):
    return x * jax.nn.sigmoid(x)


def _rms(x, g):
    return x * lax.rsqrt(jnp.mean(x * x, axis=-1, keepdims=True) + EPS) * g


def _lane_lo():
    return lax.broadcasted_iota(jnp.int32, (1, LANES), 1) < HD


def _ada_kernel(c_ref, w_ref, b_ref, o_ref):
    a = _silu(c_ref[...])
    o_ref[...] = jnp.dot(a, w_ref[...], preferred_element_type=F32, precision=HIGHEST) + b_ref[...]


def _ada(cc, w, b, layer):
    tn = 1536
    out = pl.pallas_call(
        _ada_kernel,
        grid=(6 * D // tn,),
        in_specs=[pl.BlockSpec((8, D), lambda j: (0, 0)),
                  pl.BlockSpec((None, D, tn), lambda j: (layer, 0, j)),
                  pl.BlockSpec((None, 1, tn), lambda j: (layer, 0, j))],
        out_specs=pl.BlockSpec((8, tn), lambda j: (0, j)),
        out_shape=jax.ShapeDtypeStruct((8, 6 * D), F32),
        compiler_params=_cparams(1),
        name="ada",
    )(cc, w, b.reshape(b.shape[0], 1, 6 * D))
    return out.reshape(8, 1, 6 * D)


def _inproj_kernel(xp_ref, xs_ref, g_ref, shift_ref, scale_ref, w_ref, cos_ref, sin_ref, o_ref, *, rope_cols):
    h = _rms(_pick_rows(xp_ref, xs_ref), g_ref[...]) * (1.0 + scale_ref[...]) + shift_ref[...]
    o = jnp.dot(h.astype(BF16), w_ref[...], preferred_element_type=F32)
    if rope_cols:
        cos = cos_ref[...]
        sin = sin_ref[...]
        lane = lax.broadcasted_iota(jnp.int32, (1, LANES), 1)
        first = (lane % 32) < 16
        for c in range(rope_cols // LANES):
            oc = o[:, c * LANES:(c + 1) * LANES]
            partner = jnp.where(first, pltpu.roll(oc, LANES - 16, 1), pltpu.roll(oc, 16, 1))
            o_ref[:, c * LANES:(c + 1) * LANES] = oc * cos + partner * sin
        o_ref[:, rope_cols:] = o[:, rope_cols:]
    else:
        o_ref[...] = o


def _rope_tables():
    half = HD // 2
    inv = ROPE_BASE ** (-jnp.arange(0, half, 2, dtype=F32) / half)
    t = jnp.arange(DEC_SEQ)
    ang_r = (t // GRID_W).astype(F32)[:, None] * inv[None]
    ang_c = (t % GRID_W).astype(F32)[:, None] * inv[None]

    def head(fn_r, fn_c, sign):
        return jnp.concatenate([sign[0] * fn_r, sign[1] * fn_r, sign[0] * fn_c, sign[1] * fn_c], axis=-1)

    cos = head(jnp.cos(ang_r), jnp.cos(ang_c), (1.0, 1.0))
    sin = head(jnp.sin(ang_r), jnp.sin(ang_c), (-1.0, 1.0))
    cos = jnp.concatenate([jnp.ones((TM, HD), F32), cos], axis=0)
    sin = jnp.concatenate([jnp.zeros((TM, HD), F32), sin], axis=0)
    return jnp.tile(cos, (1, 2)), jnp.tile(sin, (1, 2))


def _inproj(x, g, mod, w_bf16, rope, rope_cols):
    n_out = w_bf16.shape[1]
    npb = NP_TOK // TM
    spb = DEC_SEQ // TM

    def rope_map(i):
        return (jnp.where(i < npb, 0, 1 + (i - npb) % spb), 0)

    return pl.pallas_call(
        functools.partial(_inproj_kernel, rope_cols=rope_cols),
        grid=(N_TOK // TM,),
        in_specs=_pair_specs(D) + [
                  pl.BlockSpec((1, D), lambda i: (0, 0)),
                  _mod_spec(0), _mod_spec(1),
                  pl.BlockSpec((D, n_out), lambda i: (0, 0)),
                  pl.BlockSpec((TM, LANES), rope_map),
                  pl.BlockSpec((TM, LANES), rope_map)],
        out_specs=pl.BlockSpec((TM, n_out), lambda i: (i, 0)),
        out_shape=jax.ShapeDtypeStruct((N_TOK, n_out), F32),
        compiler_params=_cparams(1),
        name="inproj",
    )(x[0], x[1], g.reshape(1, D), mod, mod, w_bf16, rope[0], rope[1])


def _softmax_av(s_list, v_list, sink=None):
    mx = s_list[0].max(axis=-1, keepdims=True)
    for s in s_list[1:]:
        mx = jnp.maximum(mx, s.max(axis=-1, keepdims=True))
    if sink is not None:
        mx = jnp.maximum(mx, sink)
    den = jnp.exp(sink - mx) if sink is not None else 0.0
    acc = None
    for s, v in zip(s_list, v_list):
        p = jnp.exp(s - mx)
        den = den + p.sum(axis=-1, keepdims=True)
        pv = jnp.dot(p.astype(BF16), v, preferred_element_type=F32)
        acc = pv if acc is None else acc + pv
    return acc / den


def _dup_half(x, j, lo):
    xr = pltpu.roll(x, HD, 1)
    return jnp.where(lo, x, xr) if j == 0 else jnp.where(lo, xr, x)


def _stack_heads(q_ref, heads, lo, scale):
    parts = []
    for h in heads:
        qp = q_ref[:, (h // 2) * LANES:(h // 2 + 1) * LANES]
        keep = lo if h % 2 == 0 else jnp.logical_not(lo)
        parts.append(jnp.where(keep, qp, 0.0) * scale)
    return jnp.concatenate(parts, axis=0).astype(BF16)


def _sink_column(sink_ref, heads, rows):
    return jnp.concatenate([jnp.full((rows, 1), sink_ref[h], F32) for h in heads], axis=0)


def _ctx_gqa_kernel(sink_ref, q_ref, k_ref, v_ref, o_ref):
    lo = _lane_lo()
    k = k_ref[...]
    v = v_ref[...]
    group = A_HEADS // A_KV_HEADS
    scores = []
    for j in range(A_KV_HEADS):
        q = _stack_heads(q_ref, list(range(group * j, group * (j + 1))), lo, HD ** -0.5)
        scores.append(lax.dot_general(q, _dup_half(k, j, lo).astype(BF16), NT_DIMS, preferred_element_type=F32))
    s = jnp.concatenate(scores, axis=0)
    sink = _sink_column(sink_ref, list(range(A_HEADS)), SEQ)
    mx = jnp.maximum(s.max(axis=-1, keepdims=True), sink)
    e = jnp.exp(s - mx)
    den = jnp.exp(sink - mx) + e.sum(axis=-1, keepdims=True)
    e = e.astype(BF16)
    rows_per_group = group * SEQ
    for j in range(A_KV_HEADS):
        rows = slice(j * rows_per_group, (j + 1) * rows_per_group)
        o = jnp.dot(e[rows], _dup_half(v, j, lo).astype(BF16), preferred_element_type=F32) / den[rows]
        for t in range(group // 2):
            pair = (group * j) // 2 + t
            o_ref[:, pair * LANES:(pair + 1) * LANES] = jnp.where(
                lo, o[(2 * t) * SEQ:(2 * t + 1) * SEQ], o[(2 * t + 1) * SEQ:(2 * t + 2) * SEQ])


def _ctx_gqa(p, sink):
    return pl.pallas_call(
        _ctx_gqa_kernel,
        grid_spec=pltpu.PrefetchScalarGridSpec(
            num_scalar_prefetch=1,
            grid=(BATCH,),
            in_specs=[pl.BlockSpec((SEQ, A_Q), lambda b, s: (b, 0)),
                      pl.BlockSpec((SEQ, A_KV), lambda b, s: (b, A_Q // A_KV)),
                      pl.BlockSpec((SEQ, A_KV), lambda b, s: (b, A_Q // A_KV + 1))],
            out_specs=pl.BlockSpec((SEQ, A_Q), lambda b, s: (b, 0))),
        out_shape=jax.ShapeDtypeStruct((NP_TOK, A_Q), F32),
        compiler_params=_cparams(1),
        name="ctx_gqa",
    )(sink, p, p, p)


def _win_kernel(sink_ref, q_ref, kp_ref, kc_ref, kn_ref, vp_ref, vc_ref, vn_ref, ck_ref, cv_ref, o_ref):
    i = pl.program_id(1)
    lo = _lane_lo()
    k = jnp.concatenate([kp_ref[...], kc_ref[...], kn_ref[...]], axis=0)
    v = jnp.concatenate([vp_ref[...], vc_ref[...], vn_ref[...]], axis=0)
    ck = ck_ref[...]
    cv = cv_ref[...]
    group = A_HEADS // A_KV_HEADS
    n_keys = WIN_Q + 2 * A_WINDOW
    qpos = i * WIN_Q + lax.broadcasted_iota(jnp.int32, (WIN_Q, n_keys), 0)
    kpos = i * WIN_Q - A_WINDOW + lax.broadcasted_iota(jnp.int32, (WIN_Q, n_keys), 1)
    valid = (jnp.abs(kpos - qpos) <= A_WINDOW) & (kpos >= 0) & (kpos < DEC_SEQ)
    valid = jnp.concatenate([valid] * group, axis=0)
    s_loc, s_ctx, values = [], [], []
    for j in range(A_KV_HEADS):
        heads = list(range(group * j, group * (j + 1)))
        kd = _dup_half(k, j, lo).astype(BF16)
        ckd = _dup_half(ck, j, lo).astype(BF16)
        values.append((_dup_half(v, j, lo).astype(BF16), _dup_half(cv, j, lo).astype(BF16)))
        q = _stack_heads(q_ref, heads, lo, HD ** -0.5)
        s_loc.append(jnp.where(valid, lax.dot_general(q, kd, NT_DIMS, preferred_element_type=F32), NEG))
        s_ctx.append(lax.dot_general(q, ckd, NT_DIMS, preferred_element_type=F32))
    s_loc = jnp.concatenate(s_loc, axis=0)
    s_ctx = jnp.concatenate(s_ctx, axis=0)
    sink = _sink_column(sink_ref, list(range(A_HEADS)), WIN_Q)
    mx = jnp.maximum(jnp.maximum(s_loc.max(axis=-1, keepdims=True), s_ctx.max(axis=-1, keepdims=True)), sink)
    p_loc = jnp.exp(s_loc - mx)
    p_ctx = jnp.exp(s_ctx - mx)
    den = p_loc.sum(axis=-1, keepdims=True) + p_ctx.sum(axis=-1, keepdims=True) + jnp.exp(sink - mx)
    p_loc = p_loc.astype(BF16)
    p_ctx = p_ctx.astype(BF16)
    rows_per_group = group * WIN_Q
    for j, (vd, cvd) in enumerate(values):
        rows = slice(j * rows_per_group, (j + 1) * rows_per_group)
        o = (jnp.dot(p_loc[rows], vd, preferred_element_type=F32)
             + jnp.dot(p_ctx[rows], cvd, preferred_element_type=F32)) / den[rows]
        for t in range(group // 2):
            pair = (group * j) // 2 + t
            o_ref[:, pair * LANES:(pair + 1) * LANES] = jnp.where(
                lo, o[(2 * t) * WIN_Q:(2 * t + 1) * WIN_Q], o[(2 * t + 1) * WIN_Q:(2 * t + 2) * WIN_Q])


WIN_Q = 256


def _win_attention(p, cache_k, cache_v, sink):
    nblk = DEC_SEQ // WIN_Q
    side = WIN_Q // A_WINDOW
    nside = DEC_SEQ // A_WINDOW
    base = NP_TOK // WIN_Q
    side_base = NP_TOK // A_WINDOW
    kcol = A_Q // A_KV

    def main_spec(col):
        return pl.BlockSpec((WIN_Q, A_KV), lambda b, i, s: (base + b * nblk + i, col))

    def side_spec(col, off):
        return pl.BlockSpec((A_WINDOW, A_KV),
                            lambda b, i, s: (side_base + b * nside + jnp.clip(side * i + off, 0, nside - 1), col))

    ctx_spec = pl.BlockSpec((None, PAST, A_KV), lambda b, i, s: (b, 0, 0))
    return pl.pallas_call(
        _win_kernel,
        grid_spec=pltpu.PrefetchScalarGridSpec(
            num_scalar_prefetch=1,
            grid=(DEC_BATCH, nblk),
            in_specs=[pl.BlockSpec((WIN_Q, A_Q), lambda b, i, s: (base + b * nblk + i, 0)),
                      side_spec(kcol, -1), main_spec(kcol), side_spec(kcol, side),
                      side_spec(kcol + 1, -1), main_spec(kcol + 1), side_spec(kcol + 1, side),
                      ctx_spec, ctx_spec],
            out_specs=pl.BlockSpec((WIN_Q, A_Q), lambda b, i, s: (b * nblk + i, 0))),
        out_shape=jax.ShapeDtypeStruct((NS_TOK, A_Q), F32),
        compiler_params=_cparams(2),
        name="win_attn",
    )(sink, p, p, p, p, p, p, p, cache_k.reshape(DEC_BATCH, PAST, A_KV), cache_v.reshape(DEC_BATCH, PAST, A_KV))


def _ret_kernel(df_ref, db_ref, q_ref, k_ref, v_ref, g_ref, gn_ref, s0f_ref, s0b_ref,
                o_ref, sf_ref, sb_ref, of_scr, ob_scr, *, length):
    c_len = RET_CHUNK
    n = length // c_len
    lo = _lane_lo()
    hi = jnp.logical_not(lo)
    row = lax.broadcasted_iota(jnp.int32, (c_len, c_len), 0)
    col = lax.broadcasted_iota(jnp.int32, (c_len, c_len), 1)
    rowp = lax.broadcasted_iota(jnp.int32, (LANES, LANES), 0)
    colp = lax.broadcasted_iota(jnp.int32, (LANES, LANES), 1)
    blockdiag = (rowp < HD) == (colp < HD)
    idx = lax.broadcasted_iota(jnp.int32, (c_len, 1), 0).astype(F32)

    def direction(dec_ref, forward):
        lg = -jnp.exp(dec_ref[...])
        diff = (row - col) if forward else (col - row)
        keep = (diff >= 0) if forward else (diff > 0)
        dist = jnp.maximum(diff, 0).astype(F32)
        dm = jnp.concatenate([jnp.where(keep, jnp.exp(dist * lg[:, off:off + 1]), 0.0) for off in (0, HD)], axis=0)
        if forward:
            xi = jnp.exp((idx + 1.0) * lg)
            zeta = jnp.exp((c_len - 1.0 - idx) * lg)
        else:
            xi = jnp.exp((c_len - idx) * lg)
            zeta = jnp.exp(idx * lg)
        return dm, xi, zeta, jnp.exp(c_len * lg)

    def chunk(c, state, consts):
        dm, xi, zeta, gch = consts
        rows = pl.ds(pl.multiple_of(c * c_len, c_len), c_len)
        qc = q_ref[rows, :]
        kc = k_ref[rows, :] * HD ** -0.5
        vc = v_ref[rows, :].astype(BF16)
        kb = kc.astype(BF16)
        q2 = jnp.concatenate([jnp.where(lo, qc, 0.0), jnp.where(hi, qc, 0.0)], axis=0).astype(BF16)
        inner = lax.dot_general(q2, kb, NT_DIMS, preferred_element_type=F32) * dm
        kz_t = (kc * zeta).T
        res = jnp.dot(jnp.concatenate([inner, kz_t], axis=0).astype(BF16), vc, preferred_element_type=F32)
        cross = jnp.dot(qc.astype(BF16), state.astype(BF16), preferred_element_type=F32) * xi
        o = jnp.where(lo, res[:c_len], res[c_len:2 * c_len]) + cross
        state = gch * state + jnp.where(blockdiag, res[2 * c_len:], 0.0)
        return rows, o, state

    cf = direction(df_ref, True)
    cb = direction(db_ref, False)

    def scan_body(t, states):
        rows_f, o_f, state_f = chunk(t, states[0], cf)
        of_scr[rows_f, :] = o_f
        rows_b, o_b, state_b = chunk(n - 1 - t, states[1], cb)
        ob_scr[rows_b, :] = o_b
        return state_f, state_b

    state_f, state_b = lax.fori_loop(0, n, scan_body, (s0f_ref[...], s0b_ref[...]), unroll=min(n, RET_UNROLL))
    sf_ref[...] = state_f
    sb_ref[...] = state_b

    gn = gn_ref[...]
    norm_rows = min(RET_NORM_ROWS, length)
    n_norm = length // norm_rows

    def per_head(x):
        a = jnp.where(lo, x, 0.0).sum(axis=-1, keepdims=True)
        b = jnp.where(hi, x, 0.0).sum(axis=-1, keepdims=True)
        return jnp.where(lo, a, b) * (1.0 / HD)

    def norm_body(t, carry):
        rows = pl.ds(pl.multiple_of(t * norm_rows, norm_rows), norm_rows)
        o = of_scr[rows, :] + ob_scr[rows, :]
        d = o - per_head(o)
        y = d * lax.rsqrt(per_head(d * d) + EPS) * gn
        o_ref[rows, :] = _silu(g_ref[rows, :]) * y
        return carry

    lax.fori_loop(0, n_norm, norm_body, 0)


def _pair_lanes(v):
    return jnp.repeat(v.astype(F32), HD).reshape(B_HEADS // 2, 1, LANES)


def _blockdiag_states(s):
    b = s.shape[0]
    s = s.astype(F32).reshape(b, B_HEADS // 2, 2, HD, HD)
    z = jnp.zeros_like(s[:, :, 0])
    top = jnp.concatenate([s[:, :, 0], z], axis=-1)
    bot = jnp.concatenate([z, s[:, :, 1]], axis=-1)
    return jnp.concatenate([top, bot], axis=-2)


def _diag_states(sp):
    b = sp.shape[0]
    s = jnp.stack([sp[:, :, :HD, :HD], sp[:, :, HD:, HD:]], axis=2)
    return s.reshape(b, B_HEADS, HD, HD)


def _retention(p, row_base, batch, length, dec_f, dec_b, gn_g, s0f, s0b):
    npairs = B_HEADS // 2
    blk0 = row_base // length
    qcol = (A_Q + 2 * A_KV) // LANES

    def col_spec(off):
        return pl.BlockSpec((length, LANES), lambda b, h: (blk0 + b, qcol + off * npairs + h))

    lane_spec = pl.BlockSpec((None, 1, LANES), lambda b, h: (h, 0, 0))
    state_spec = pl.BlockSpec((None, None, LANES, LANES), lambda b, h: (b, h, 0, 0))
    state_shape = jax.ShapeDtypeStruct((batch, npairs, LANES, LANES), F32)
    return pl.pallas_call(
        functools.partial(_ret_kernel, length=length),
        grid=(batch, npairs),
        in_specs=[lane_spec, lane_spec, col_spec(0), col_spec(1), col_spec(2), col_spec(3), lane_spec,
                  state_spec, state_spec],
        out_specs=[pl.BlockSpec((length, LANES), lambda b, h: (b, h)), state_spec, state_spec],
        out_shape=[jax.ShapeDtypeStruct((batch * length, B_W), F32), state_shape, state_shape],
        scratch_shapes=[pltpu.VMEM((length, LANES), F32), pltpu.VMEM((length, LANES), F32)],
        compiler_params=_cparams(2),
        name="retention",
    )(_pair_lanes(dec_f), _pair_lanes(dec_b), p, p, p, p, gn_g.reshape(npairs, 1, LANES), s0f, s0b)


def _ctx_mha_kernel(q_ref, k_ref, v_ref, o_ref):
    lo = _lane_lo()
    for pair in range(C_HEADS // 2):
        cols = slice(pair * LANES, (pair + 1) * LANES)
        q = _stack_heads(q_ref, [2 * pair, 2 * pair + 1], lo, HD ** -0.5)
        s = lax.dot_general(q, k_ref[:, cols].astype(BF16), NT_DIMS, preferred_element_type=F32)
        o = _softmax_av([s], [v_ref[:, cols].astype(BF16)])
        o_ref[:, cols] = jnp.where(lo, o[:SEQ], o[SEQ:])


def _ctx_mha(p):
    return pl.pallas_call(
        _ctx_mha_kernel,
        grid=(BATCH,),
        in_specs=[pl.BlockSpec((SEQ, C_W), lambda b: (b, 0)),
                  pl.BlockSpec((SEQ, C_W), lambda b: (b, 1)),
                  pl.BlockSpec((SEQ, C_W), lambda b: (b, 2))],
        out_specs=pl.BlockSpec((SEQ, C_W), lambda b: (b, 0)),
        out_shape=jax.ShapeDtypeStruct((NP_TOK, C_W), F32),
        compiler_params=_cparams(1),
        name="ctx_mha",
    )(p, p, p)


NA_WIN_ROWS = 2 * NA_ROWS
NA_WIN = NA_WIN_ROWS * GRID_W
NA_QROWS = NA_ROWS * GRID_W
NA_PAD_ROWS = NA_KH // 2
NA_TABLE = 1536


def _na_kernel(q_ref, kp_ref, km_ref, kn_ref, vp_ref, vm_ref, vn_ref, ck_ref, cv_ref, ue_ref, uo_ref, o_ref):
    r0 = pl.program_id(2) * NA_ROWS
    n_rows = DEC_SEQ // GRID_W
    lo = _lane_lo()
    k = jnp.concatenate([kp_ref[...], km_ref[...], kn_ref[...]], axis=0).astype(BF16)
    v = jnp.concatenate([vp_ref[...], vm_ref[...], vn_ref[...]], axis=0).astype(BF16)
    ck = ck_ref[...].astype(BF16)
    cv = cv_ref[...].astype(BF16)
    q = q_ref[...] * HD ** -0.5
    klane = lax.broadcasted_iota(jnp.int32, (1, NA_WIN), 1)
    outs = []
    for half, keep in enumerate((lo, jnp.logical_not(lo))):
        qh = jnp.where(keep, q, 0.0).astype(BF16)
        s = lax.dot_general(qh, k, NT_DIMS, preferred_element_type=F32)
        s_ctx = lax.dot_general(qh, ck, NT_DIMS, preferred_element_type=F32)
        p_loc, p_ctx, den = [], [], []
        for rq in range(NA_ROWS):
            rows = slice(rq * GRID_W, (rq + 1) * GRID_W)
            start = NA_KH - 1 - rq
            if start % 2 == 0:
                u = ue_ref[half, :, start * GRID_W:start * GRID_W + NA_WIN]
            else:
                u = uo_ref[half, :, (start - 1) * GRID_W:(start - 1) * GRID_W + NA_WIN]
            r = r0 + rq
            first = jnp.clip(r - NA_KH // 2, 0, n_rows - NA_KH)
            lane0 = (first - r0 + NA_PAD_ROWS) * GRID_W
            in_rows = (klane >= lane0) & (klane < lane0 + NA_KH * GRID_W)
            sl = jnp.where(in_rows, s[rows] + u, NEG)
            sc = s_ctx[rows]
            mx = jnp.maximum(sl.max(axis=-1, keepdims=True), sc.max(axis=-1, keepdims=True))
            el = jnp.exp(sl - mx)
            ec = jnp.exp(sc - mx)
            den.append(el.sum(axis=-1, keepdims=True) + ec.sum(axis=-1, keepdims=True))
            p_loc.append(el.astype(BF16))
            p_ctx.append(ec.astype(BF16))
        acc = (jnp.dot(jnp.concatenate(p_loc, axis=0), v, preferred_element_type=F32)
               + jnp.dot(jnp.concatenate(p_ctx, axis=0), cv, preferred_element_type=F32))
        outs.append(acc / jnp.concatenate(den, axis=0))
    o_ref[...] = jnp.where(lo, outs[0], outs[1])


def _na_bias_tables(rpb):
    cq = jnp.arange(GRID_W)
    ck = jnp.arange(GRID_W)
    dc = jnp.clip(ck[None] - cq[:, None], -(NA_KW - 1), NA_KW - 1) + NA_KW - 1
    cs = jnp.clip(cq - NA_KW // 2, 0, GRID_W - NA_KW)
    col_ok = (ck[None] >= cs[:, None]) & (ck[None] < cs[:, None] + NA_KW)
    t = rpb.astype(F32)[:, :, dc]
    t = jnp.where(col_ok[None, None], t, NEG).transpose(0, 2, 1, 3)
    n_dr = 2 * NA_KH - 1
    blocks = NA_TABLE // GRID_W
    t = jnp.pad(t, ((0, 0), (0, 0), (NA_PAD_ROWS, blocks - n_dr - NA_PAD_ROWS), (0, 0)), constant_values=NEG)
    ue = t.reshape(C_HEADS, GRID_W, NA_TABLE)
    uo = jnp.concatenate([ue[..., GRID_W:], jnp.full((C_HEADS, GRID_W, GRID_W), NEG, F32)], axis=-1)
    return ue, uo


def _na_attention(p, cache_k, cache_v, rpb):
    npairs = C_HEADS // 2
    nrb = DEC_SEQ // NA_QROWS
    half = NA_QROWS // 2
    qbase = NP_TOK // NA_QROWS
    hbase = NP_TOK // half
    kcol = C_W // LANES
    ue, uo = _na_bias_tables(rpb)

    def main_spec(col0):
        return pl.BlockSpec((NA_QROWS, LANES), lambda b, h, r: (qbase + b * nrb + r, col0 + h))

    def side_spec(col0, off):
        return pl.BlockSpec((half, LANES),
                            lambda b, h, r: (hbase + b * 2 * nrb + jnp.clip(2 * r + off, 0, 2 * nrb - 1), col0 + h))

    ctx_spec = pl.BlockSpec((None, PAST, LANES), lambda b, h, r: (b, 0, h))
    tab_spec = pl.BlockSpec((2, GRID_W, NA_TABLE), lambda b, h, r: (h, 0, 0))
    return pl.pallas_call(
        _na_kernel,
        grid=(DEC_BATCH, npairs, nrb),
        in_specs=[main_spec(0),
                  side_spec(kcol, -1), main_spec(kcol), side_spec(kcol, 2),
                  side_spec(2 * kcol, -1), main_spec(2 * kcol), side_spec(2 * kcol, 2),
                  ctx_spec, ctx_spec, tab_spec, tab_spec],
        out_specs=pl.BlockSpec((NA_QROWS, LANES), lambda b, h, r: (b * nrb + r, h)),
        out_shape=jax.ShapeDtypeStruct((NS_TOK, C_W), F32),
        compiler_params=_cparams(3),
        name="na_attn",
    )(p, p, p, p, p, p, p, cache_k.reshape(DEC_BATCH, PAST, C_W), cache_v.reshape(DEC_BATCH, PAST, C_W), ue, uo)


def _route(biased, scores):
    t = biased.shape[1]
    per_group = N_EXPERTS // N_GROUPS
    i8 = lax.broadcasted_iota(jnp.int32, (per_group, t), 0)
    g_rows = []
    for g in range(N_GROUPS):
        bg = biased[g * per_group:(g + 1) * per_group]
        m1 = bg.max(axis=0, keepdims=True)
        first = jnp.where(bg == m1, i8, per_group).min(axis=0, keepdims=True)
        m2 = jnp.where(i8 == first, -jnp.inf, bg).max(axis=0, keepdims=True)
        g_rows.append(m1 + m2)
    g_top = jnp.concatenate(g_rows, axis=0)
    gi = lax.broadcasted_iota(jnp.int32, g_top.shape, 0)
    g_sel = jnp.zeros(g_top.shape, jnp.int32)
    cur = g_top
    for _ in range(TOPK_GROUPS):
        m = cur.max(axis=0, keepdims=True)
        hit = gi == jnp.where(cur == m, gi, N_GROUPS).min(axis=0, keepdims=True)
        g_sel = jnp.where(hit, 1, g_sel)
        cur = jnp.where(hit, -jnp.inf, cur)
    e_sel = jnp.concatenate([jnp.broadcast_to(g_sel[g:g + 1], (per_group, t)) for g in range(N_GROUPS)], axis=0)
    cur = jnp.where(e_sel > 0, biased, NEG)
    ei = lax.broadcasted_iota(jnp.int32, cur.shape, 0)
    ids, gates, hits = [], [], []
    for _ in range(TOP_K):
        m = cur.max(axis=0, keepdims=True)
        f = jnp.where(cur == m, ei, N_EXPERTS).min(axis=0, keepdims=True)
        hit = ei == f
        ids.append(f)
        hits.append(hit)
        gates.append(jnp.where(hit, scores, 0.0).sum(axis=0, keepdims=True))
        cur = jnp.where(hit, -jnp.inf, cur)
    gate = jnp.concatenate(gates, axis=0)
    gate = gate / gate.sum(axis=0, keepdims=True) * ROUTED_SCALE
    return jnp.concatenate(ids, axis=0), gate, hits


def _pack_bf16_pairs(h):
    bits = lax.bitcast_convert_type(h.astype(BF16).astype(F32), jnp.uint32)
    return bits[:, :D // 2] | (bits[:, D // 2:] >> 16)


def _unpack_bf16_pairs(xp):
    hi = lax.bitcast_convert_type(xp & jnp.uint32(0xFFFF0000), F32).astype(BF16)
    lo = lax.bitcast_convert_type(xp << 16, F32).astype(BF16)
    return hi, lo


def _dot_halves(hi, lo, w_ref):
    return (jnp.dot(hi, w_ref[:D // 2, :], preferred_element_type=F32)
            + jnp.dot(lo, w_ref[D // 2:, :], preferred_element_type=F32))


def _outproj_kernel(*refs, n_parts):
    xp_ref, xs_ref = refs[:2]
    part_refs = refs[2:2 + 3 * n_parts]
    gate_ref, shift_ref, scale_ref, g2_ref, rw_ref, rb_ref = refs[2 + 3 * n_parts:8 + 3 * n_parts]
    xo_ref, h_ref, dest_ref, wgt_ref, plan_ref, cnt_ref = refs[8 + 3 * n_parts:]
    step = pl.program_id(0)

    @pl.when(step == 0)
    def _():
        cnt_ref[...] = jnp.zeros_like(cnt_ref)
        plan_ref[...] = jnp.zeros_like(plan_ref)

    y = None
    for t in range(n_parts):
        ap_ref, as_ref, w_ref = part_refs[3 * t:3 * t + 3]
        d = jnp.dot(_pick_rows(ap_ref, as_ref).astype(BF16), w_ref[...], preferred_element_type=F32)
        y = d if y is None else y + d
    x = _pick_rows(xp_ref, xs_ref) + gate_ref[...] * y
    xo_ref[...] = x
    h = _rms(x, g2_ref[...]) * (1.0 + scale_ref[...]) + shift_ref[...]
    h_ref[...] = _pack_bf16_pairs(h)
    h_hi = h.astype(BF16)
    h_lo = (h - h_hi.astype(F32)).astype(BF16)
    rw = rw_ref[...]
    rw_hi = rw.astype(BF16)
    rw_lo = (rw - rw_hi.astype(F32)).astype(BF16)
    logits = (lax.dot_general(rw_hi, h_hi, NT_DIMS, preferred_element_type=F32)
              + lax.dot_general(rw_hi, h_lo, NT_DIMS, preferred_element_type=F32)
              + lax.dot_general(rw_lo, h_hi, NT_DIMS, preferred_element_type=F32))
    scores = jax.nn.sigmoid(logits)
    _, gate, hits = _route(scores + rb_ref[...], scores)
    wgt_ref[...] = gate
    chosen = hits[0]
    for hit in hits[1:]:
        chosen = chosen | hit
    m = jnp.where(chosen, 1.0, 0.0)
    before = (lax.broadcasted_iota(jnp.int32, (TM, TM), 0) < lax.broadcasted_iota(jnp.int32, (TM, TM), 1))
    prefix = jnp.dot(m.astype(BF16), jnp.where(before, 1.0, 0.0).astype(BF16), preferred_element_type=F32)
    e_base = (lax.broadcasted_iota(jnp.int32, (N_EXPERTS, 1), 0) * N_TOK).astype(F32)
    row_all = prefix + (cnt_ref[...] + e_base)
    dest_ref[...] = jnp.concatenate(
        [jnp.where(hit, row_all, 0.0).sum(axis=0, keepdims=True) for hit in hits], axis=0).astype(jnp.int32)
    cnt_ref[...] += m.sum(axis=1, keepdims=True)

    @pl.when(step == pl.num_programs(0) - 1)
    def _():
        _block_plan(cnt_ref[...], plan_ref)


def _block_plan(counts, plan_ref):
    cap_blocks = N_TOK // MOE_BLOCK
    nblk = ((counts.astype(jnp.int32) + (MOE_BLOCK - 1)) // MOE_BLOCK).astype(F32)
    lower = (lax.broadcasted_iota(jnp.int32, (N_EXPERTS, N_EXPERTS), 0)
             >= lax.broadcasted_iota(jnp.int32, (N_EXPERTS, N_EXPERTS), 1))
    cum = jnp.dot(jnp.where(lower, 1.0, 0.0).astype(BF16), jnp.broadcast_to(nblk, (N_EXPERTS, LANES)).astype(BF16),
                  preferred_element_type=F32)[:, :1]
    n_used = cum[N_EXPERTS - 1:, :]
    slot = jnp.minimum(lax.broadcasted_iota(jnp.int32, (1, PLAN_LANES), 1).astype(F32), n_used - 1.0)
    done = cum <= slot
    expert = jnp.where(done, 1.0, 0.0).sum(axis=0, keepdims=True)
    blocks_before = jnp.where(done, nblk, 0.0).sum(axis=0, keepdims=True)
    plan_ref[0:1, :] = (expert * cap_blocks + (slot - blocks_before)).astype(jnp.int32)
    plan_ref[1:2, :] = expert.astype(jnp.int32)
    plan_ref[2:3, :] = jnp.broadcast_to(n_used, (1, PLAN_LANES)).astype(jnp.int32)


def _outproj(x, parts, mod, g2, router_w, router_b):
    in_specs = _pair_specs(D)
    args = [x[0], x[1]]
    for ap, a_s, w in parts:
        width = ap.shape[1]
        in_specs += _pair_specs(width) + [pl.BlockSpec((width, D), lambda i: (0, 0))]
        args += [ap, a_s, w]
    in_specs += [_mod_spec(2), _mod_spec(3), _mod_spec(4),
                 pl.BlockSpec((1, D), lambda i: (0, 0)),
                 pl.BlockSpec((N_EXPERTS, D), lambda i: (0, 0)),
                 pl.BlockSpec((N_EXPERTS, 1), lambda i: (0, 0))]
    args += [mod, mod, mod, g2.reshape(1, D), router_w.T, router_b.reshape(N_EXPERTS, 1)]
    return pl.pallas_call(
        functools.partial(_outproj_kernel, n_parts=len(parts)),
        grid=(N_TOK // TM,),
        in_specs=in_specs,
        out_specs=[pl.BlockSpec((TM, D), lambda i: (i, 0)),
                   pl.BlockSpec((TM, D // 2), lambda i: (i, 0)),
                   pl.BlockSpec((TOP_K, TM), lambda i: (0, i)),
                   pl.BlockSpec((TOP_K, TM), lambda i: (0, i)),
                   pl.BlockSpec((8, PLAN_LANES), lambda i: (0, 0))],
        out_shape=[jax.ShapeDtypeStruct((N_TOK, D), F32),
                   jax.ShapeDtypeStruct((N_TOK, D // 2), jnp.uint32),
                   jax.ShapeDtypeStruct((TOP_K, N_TOK), jnp.int32),
                   jax.ShapeDtypeStruct((TOP_K, N_TOK), F32),
                   jax.ShapeDtypeStruct((8, PLAN_LANES), jnp.int32)],
        scratch_shapes=[pltpu.VMEM((N_EXPERTS, 1), F32)],
        compiler_params=_cparams(1),
        name="outproj_router",
    )(*args)


def _experts_kernel(br_ref, be_ref, nu_ref, x_ref, w1_ref, w3_ref, w2_ref, o_ref, w1b, w3b, w2b):
    i = pl.program_id(0)
    e = be_ref[i]
    prev = be_ref[jnp.maximum(i - 1, 0)]

    @pl.when((i == 0) | (e != prev))
    def _():
        w1b[...] = w1_ref[...].astype(BF16)
        w3b[...] = w3_ref[...].astype(BF16)
        w2b[...] = w2_ref[...].astype(BF16)

    @pl.when(i < nu_ref[0])
    def _():
        hi, lo = _unpack_bf16_pairs(x_ref[...])
        a = _dot_halves(hi, lo, w1b)
        b = _dot_halves(hi, lo, w3b)
        h = (_silu(a) * b).astype(BF16)
        o_ref[...] = _pack_bf16_pairs(jnp.dot(h, w2b[...], preferred_element_type=F32))


def _experts(plan, x_rows, w1, w3, w2, layer):
    return pl.pallas_call(
        _experts_kernel,
        grid_spec=pltpu.PrefetchScalarGridSpec(
            num_scalar_prefetch=3,
            grid=(N_MOE_BLOCKS,),
            in_specs=[pl.BlockSpec((MOE_BLOCK, D // 2), lambda i, br, be, nu: (br[i], 0)),
                      pl.BlockSpec((None, None, D, FF), lambda i, br, be, nu: (layer, be[i], 0, 0)),
                      pl.BlockSpec((None, None, D, FF), lambda i, br, be, nu: (layer, be[i], 0, 0)),
                      pl.BlockSpec((None, None, FF, D), lambda i, br, be, nu: (layer, be[i], 0, 0))],
            out_specs=pl.BlockSpec((MOE_BLOCK, D // 2), lambda i, br, be, nu: (br[i], 0)),
            scratch_shapes=[pltpu.VMEM((D, FF), BF16), pltpu.VMEM((D, FF), BF16), pltpu.VMEM((FF, D), BF16)]),
        out_shape=jax.ShapeDtypeStruct(x_rows.shape, jnp.uint32),
        compiler_params=_cparams(1),
        name="experts",
    )(plan[0], plan[1], plan[2, :1], x_rows, w1, w3, w2)


SC_CORES = 2
SC_SUBCORES = 16
SC_WORKERS = SC_CORES * SC_SUBCORES
SC_CHUNK_BYTES = 64 * 1024
SC_SLOTS = 4


def _sc_scatter(rows, dest, n_out):
    n_rows, width = rows.shape
    picks = dest.shape[0]
    chunk = SC_CHUNK_BYTES // (4 * width)
    per_worker = n_rows // SC_WORKERS
    n_chunks = per_worker // chunk
    assert per_worker * SC_WORKERS == n_rows and n_chunks * chunk == per_worker and n_chunks % 2 == 0
    mesh = plsc.VectorSubcoreMesh(core_axis_name="c", subcore_axis_name="s")

    @functools.partial(
        pl.kernel, mesh=mesh,
        out_type=jax.ShapeDtypeStruct((n_out, width), rows.dtype),
        scratch_types=[pltpu.VMEM((picks, n_chunks, chunk), jnp.int32),
                       pltpu.VMEM((2, chunk, width), rows.dtype),
                       pltpu.SemaphoreType.DMA((2,)),
                       pltpu.SemaphoreType.DMA((2,))])
    def scatter(r_hbm, d_hbm, o_hbm, idx_v, rows_v, lsem, ssem):
        worker = lax.axis_index("s") * SC_CORES + lax.axis_index("c")
        base = worker * per_worker
        for k in range(picks):
            pltpu.sync_copy(d_hbm.at[k, worker], idx_v.at[k])

        def load_copy(c, b):
            src = pl.ds(pl.multiple_of(base + c * chunk, chunk), chunk)
            return pltpu.make_async_copy(r_hbm.at[src], rows_v.at[b], lsem.at[b])

        def store_copy(c, b, k):
            return pltpu.make_async_copy(rows_v.at[b], o_hbm.at[idx_v.at[k, c]], ssem.at[b])

        load_copy(0, 0).start()

        @pl.loop(0, n_chunks, step=2)
        def _(c0):
            for b in range(2):
                c = c0 + b
                load_copy(c, b).wait()
                for k in range(picks):
                    store_copy(c, b, k).start()

                @pl.when(c > 0)
                def _():
                    for k in range(picks):
                        store_copy(c - 1, 1 - b, k).wait()

                @pl.when(c + 1 < n_chunks)
                def _():
                    load_copy(c + 1, 1 - b).start()

        for k in range(picks):
            store_copy(n_chunks - 1, 1, k).wait()

    return scatter(rows, dest.reshape(picks, SC_WORKERS, n_chunks, chunk))


def _sc_gather(table, idx):
    n_idx = idx.shape[0]
    width = table.shape[1]
    chunk = SC_CHUNK_BYTES // (4 * width)
    per_worker = n_idx // SC_WORKERS
    n_chunks = per_worker // chunk
    ahead = SC_SLOTS - 1
    assert per_worker * SC_WORKERS == n_idx and n_chunks * chunk == per_worker and n_chunks % SC_SLOTS == 0
    mesh = plsc.VectorSubcoreMesh(core_axis_name="c", subcore_axis_name="s")

    @functools.partial(
        pl.kernel, mesh=mesh,
        out_type=jax.ShapeDtypeStruct((n_idx, width), table.dtype),
        scratch_types=[pltpu.VMEM((per_worker,), jnp.int32),
                       pltpu.VMEM((SC_SLOTS, chunk, width), table.dtype),
                       pltpu.SemaphoreType.DMA((SC_SLOTS,)),
                       pltpu.SemaphoreType.DMA((SC_SLOTS,))])
    def gather(t_hbm, i_hbm, o_hbm, idx_v, rows_v, gsem, wsem):
        worker = lax.axis_index("s") * SC_CORES + lax.axis_index("c")
        base = worker * per_worker
        pltpu.sync_copy(i_hbm.at[pl.ds(pl.multiple_of(base, chunk), per_worker)], idx_v)

        def gather_copy(c, b):
            ids = idx_v.at[pl.ds(pl.multiple_of(c * chunk, chunk), chunk)]
            return pltpu.make_async_copy(t_hbm.at[ids], rows_v.at[b], gsem.at[b])

        def write_copy(c, b):
            rows = pl.ds(pl.multiple_of(base + c * chunk, chunk), chunk)
            return pltpu.make_async_copy(rows_v.at[b], o_hbm.at[rows], wsem.at[b])

        for c in range(ahead):
            gather_copy(c, c).start()

        @pl.loop(0, n_chunks, step=SC_SLOTS)
        def _(c0):
            for b in range(SC_SLOTS):
                c = c0 + b
                refill = (b + ahead) % SC_SLOTS
                gather_copy(c, b).wait()
                write_copy(c, b).start()

                @pl.when(c > 0)
                def _():
                    write_copy(c - 1, refill).wait()

                @pl.when(c + ahead < n_chunks)
                def _():
                    gather_copy(c + ahead, refill).start()

        write_copy(n_chunks - 1, (n_chunks - 1) % SC_SLOTS).wait()

    return gather(table, idx)


TC = 512


def _combine_kernel(x_ref, h_ref, y_ref, wgt_ref, gate_ref, w1_ref, w3_ref, w2_ref, fg_ref, o_ref, *, final):
    hi, lo = _unpack_bf16_pairs(h_ref[...])
    a = _dot_halves(hi, lo, w1_ref)
    b = _dot_halves(hi, lo, w3_ref)
    ffn = jnp.dot((_silu(a) * b).astype(BF16), w2_ref[...], preferred_element_type=F32)
    wgt = wgt_ref[...]
    r_hi = None
    r_lo = None
    for k in range(TOP_K):
        yk = y_ref[k]
        w = wgt[:, k:k + 1]
        t_hi = lax.bitcast_convert_type(yk & jnp.uint32(0xFFFF0000), F32) * w
        t_lo = lax.bitcast_convert_type(yk << 16, F32) * w
        r_hi = t_hi if r_hi is None else r_hi + t_hi
        r_lo = t_lo if r_lo is None else r_lo + t_lo
    x = x_ref[...] + gate_ref[...] * (ffn + jnp.concatenate([r_hi, r_lo], axis=1))
    o_ref[...] = _rms(x, fg_ref[...]) if final else x


def _combine(x, h, y_rows, wgt, mod, sw1, sw3, sw2, final_g, final):
    weights = (sw1.astype(BF16), sw3.astype(BF16), sw2.astype(BF16), final_g.reshape(1, D))

    def rows_from(first_row, n_rows):
        b0 = first_row // TC
        return pl.pallas_call(
            functools.partial(_combine_kernel, final=final),
            grid=(n_rows // TC,),
            in_specs=[pl.BlockSpec((TC, D), lambda i: (i + b0, 0)),
                      pl.BlockSpec((TC, D // 2), lambda i: (i + b0, 0)),
                      pl.BlockSpec((TOP_K, TC, D // 2), lambda i: (0, i + b0, 0)),
                      pl.BlockSpec((TC, TOP_K), lambda i: (i + b0, 0)),
                      _mod_spec(5, TC, b0),
                      pl.BlockSpec((D, FF), lambda i: (0, 0)),
                      pl.BlockSpec((D, FF), lambda i: (0, 0)),
                      pl.BlockSpec((FF, D), lambda i: (0, 0)),
                      pl.BlockSpec((1, D), lambda i: (0, 0))],
            out_specs=pl.BlockSpec((TC, D), lambda i: (i, 0)),
            out_shape=jax.ShapeDtypeStruct((n_rows, D), F32),
            compiler_params=_cparams(1),
            name="combine",
        )(x, h, y_rows, wgt, mod, *weights)

    return rows_from(0, NP_TOK), rows_from(NP_TOK, NS_TOK)


def kernel(x_prompt, x_sample, cache_a_k, cache_a_v, state_ret_fwd, state_ret_bwd, cache_c_k, cache_c_v,
           c, c_ctx, norm1_g, norm2_g, ada_w, ada_b, even_w_in, even_w_out, sink_a, ret_decay_fwd,
           ret_decay_bwd, ret_gn_g, odd_w_in, odd_w_out, na_rpb, router_w, router_b, exp_w1, exp_w3,
           exp_w2, sh_w1, sh_w3, sh_w2, final_g):
    x = (x_prompt.reshape(NP_TOK, D), x_sample.reshape(NS_TOK, D))
    cc = jnp.concatenate([c_ctx[None], c, jnp.zeros((8 - 1 - DEC_BATCH, D), F32)], axis=0)
    rope = _rope_tables()
    outs = {}
    for l in range(2):
        mod = _ada(cc, ada_w, ada_b, l)
        if l == 0:
            p = _inproj(x, norm1_g[l], mod, even_w_in[0].astype(BF16), rope, A_Q + A_KV)
            oa_p = _ctx_gqa(p, sink_a[0])
            oa_s = _win_attention(p, cache_a_k[:, 0], cache_a_v[:, 0], sink_a[0])
            zero = jnp.zeros((BATCH, B_HEADS // 2, LANES, LANES), F32)
            ob_p, sf, sb = _retention(p, 0, BATCH, SEQ, ret_decay_fwd[0], ret_decay_bwd[0], ret_gn_g[0], zero, zero)
            ob_s, _, _ = _retention(p, NP_TOK, DEC_BATCH, DEC_SEQ, ret_decay_fwd[0], ret_decay_bwd[0], ret_gn_g[0],
                                    _blockdiag_states(state_ret_fwd[:, 0]), _blockdiag_states(state_ret_bwd[:, 0]))
            w_out = even_w_out[0].astype(BF16)
            parts = [(oa_p, oa_s, w_out[:A_Q]), (ob_p, ob_s, w_out[A_Q:])]
            outs["a_k"] = p[:NP_TOK, A_Q:A_Q + A_KV].reshape(BATCH, 1, SEQ, A_KV_HEADS, HD)
            outs["a_v"] = p[:NP_TOK, A_Q + A_KV:A_Q + 2 * A_KV].reshape(BATCH, 1, SEQ, A_KV_HEADS, HD)
            outs["r_f"] = _diag_states(sf).reshape(BATCH, 1, B_HEADS, HD, HD)
            outs["r_b"] = _diag_states(sb).reshape(BATCH, 1, B_HEADS, HD, HD)
        else:
            p = _inproj(x, norm1_g[l], mod, odd_w_in[0].astype(BF16), rope, 0)
            o_p = _ctx_mha(p)
            o_s = _na_attention(p, cache_c_k[:, 0], cache_c_v[:, 0], na_rpb[0])
            parts = [(o_p, o_s, odd_w_out[0].astype(BF16))]
            outs["c_k"] = p[:NP_TOK, C_W:2 * C_W].reshape(BATCH, 1, SEQ, C_HEADS, HD)
            outs["c_v"] = p[:NP_TOK, 2 * C_W:3 * C_W].reshape(BATCH, 1, SEQ, C_HEADS, HD)
        x_mid, h, dest, gate_t, plan = _outproj(x, parts, mod, norm2_g[l], router_w[l], router_b[l])
        y = _experts(plan, _sc_scatter(h, dest, N_EXPERTS * N_TOK), exp_w1, exp_w3, exp_w2, l)
        y_rows = _sc_gather(y, dest.reshape(N_ASSIGN)).reshape(TOP_K, N_TOK, D // 2)
        x = _combine(x_mid, h, y_rows, gate_t.T, mod, sh_w1[l], sh_w3[l], sh_w2[l], final_g, final=(l == 1))
    y_prompt = x[0].reshape(BATCH, SEQ, D)
    y_sample = x[1].reshape(DEC_BATCH, DEC_SEQ, D)
    return (y_prompt, y_sample, outs["a_k"], outs["a_v"], outs["r_f"], outs["r_b"], outs["c_k"], outs["c_v"])
```

```python
import functools

import jax
import jax.numpy as jnp
from jax import lax
from jax.experimental import pallas as pl
from jax.experimental.pallas import tpu as pltpu
from jax.experimental.pallas import tpu_sc as plsc

F32 = jnp.float32
BF16 = jnp.bfloat16
HIGHEST = lax.Precision.HIGHEST

D = 1024
BATCH = 32
SEQ = 256
DEC_BATCH = 4
DEC_SEQ = 4096
PAST = 256
GRID_W = 64
HD = 64
EPS = 1e-6
NEG = -1e30
ROPE_BASE = 10000.0
A_HEADS = 8
A_KV_HEADS = 2
A_Q = A_HEADS * HD
A_KV = A_KV_HEADS * HD
B_HEADS = 8
B_W = B_HEADS * HD
EVEN_IN = A_Q + 2 * A_KV + 4 * B_W
C_HEADS = 16
C_W = C_HEADS * HD
NA_KH = 8
NA_KW = 16
N_EXPERTS = 64
TOP_K = 8
N_GROUPS = 8
TOPK_GROUPS = 4
FF = 256
ROUTED_SCALE = 2.5
MOE_BLOCK = 1024
RET_CHUNK = 256
RET_UNROLL = 4
RET_NORM_ROWS = 1024
A_WINDOW = 128

NP_TOK = BATCH * SEQ
NS_TOK = DEC_BATCH * DEC_SEQ
N_TOK = NP_TOK + NS_TOK
N_ASSIGN = N_TOK * TOP_K
N_MOE_BLOCKS = (N_ASSIGN + N_EXPERTS * (MOE_BLOCK - 1) + MOE_BLOCK - 1) // MOE_BLOCK
PLAN_LANES = 512
assert N_TOK % MOE_BLOCK == 0 and N_MOE_BLOCKS <= PLAN_LANES

LANES = 128
TM = 512
NA_ROWS = 8
V7X_VMEM_LIMIT = 56 * 1024 * 1024

NT_DIMS = (((1,), (1,)), ((), ()))


def _cparams(n_axes, vmem=V7X_VMEM_LIMIT):
    return pltpu.CompilerParams(dimension_semantics=("arbitrary",) * n_axes, vmem_limit_bytes=vmem)


def _seg_of_block(i, rows):
    row0 = i * rows
    return jnp.where(row0 < NP_TOK, 0, 1 + (row0 - NP_TOK) // DEC_SEQ)


def _mod_spec(chunk, rows=TM, first_block=0):
    return pl.BlockSpec((None, 1, D), lambda i: (_seg_of_block(i + first_block, rows), 0, chunk))


def _pair_specs(width, rows=TM):
    npb = NP_TOK // rows
    nsb = NS_TOK // rows
    return [pl.BlockSpec((rows, width), lambda i: (jnp.minimum(i, npb - 1), 0)),
            pl.BlockSpec((rows, width), lambda i: (jnp.clip(i - npb, 0, nsb - 1), 0))]


def _pick_rows(p_ref, s_ref, rows=TM):
    return jnp.where(pl.program_id(0) < NP_TOK // rows, p_ref[...], s_ref[...])


def _silu(x):
    return x * jax.nn.sigmoid(x)


def _rms(x, g):
    return x * lax.rsqrt(jnp.mean(x * x, axis=-1, keepdims=True) + EPS) * g


def _lane_lo():
    return lax.broadcasted_iota(jnp.int32, (1, LANES), 1) < HD


def _ada_kernel(c_ref, w_ref, b_ref, o_ref):
    a = _silu(c_ref[...])
    o_ref[...] = jnp.dot(a, w_ref[...], preferred_element_type=F32, precision=HIGHEST) + b_ref[...]


def _ada(cc, w, b, layer):
    tn = 1536
    out = pl.pallas_call(
        _ada_kernel,
        grid=(6 * D // tn,),
        in_specs=[pl.BlockSpec((8, D), lambda j: (0, 0)),
                  pl.BlockSpec((None, D, tn), lambda j: (layer, 0, j)),
                  pl.BlockSpec((None, 1, tn), lambda j: (layer, 0, j))],
        out_specs=pl.BlockSpec((8, tn), lambda j: (0, j)),
        out_shape=jax.ShapeDtypeStruct((8, 6 * D), F32),
        compiler_params=_cparams(1),
        name="ada",
    )(cc, w, b.reshape(b.shape[0], 1, 6 * D))
    return out.reshape(8, 1, 6 * D)


def _inproj_kernel(xp_ref, xs_ref, g_ref, shift_ref, scale_ref, w_ref, cos_ref, sin_ref, o_ref, *, rope_cols):
    h = _rms(_pick_rows(xp_ref, xs_ref), g_ref[...]) * (1.0 + scale_ref[...]) + shift_ref[...]
    o = jnp.dot(h.astype(BF16), w_ref[...], preferred_element_type=F32)
    if rope_cols:
        cos = cos_ref[...]
        sin = sin_ref[...]
        lane = lax.broadcasted_iota(jnp.int32, (1, LANES), 1)
        first = (lane % 32) < 16
        for c in range(rope_cols // LANES):
            oc = o[:, c * LANES:(c + 1) * LANES]
            partner = jnp.where(first, pltpu.roll(oc, LANES - 16, 1), pltpu.roll(oc, 16, 1))
            o_ref[:, c * LANES:(c + 1) * LANES] = oc * cos + partner * sin
        o_ref[:, rope_cols:] = o[:, rope_cols:]
    else:
        o_ref[...] = o


def _rope_tables():
    half = HD // 2
    inv = ROPE_BASE ** (-jnp.arange(0, half, 2, dtype=F32) / half)
    t = jnp.arange(DEC_SEQ)
    ang_r = (t // GRID_W).astype(F32)[:, None] * inv[None]
    ang_c = (t % GRID_W).astype(F32)[:, None] * inv[None]

    def head(fn_r, fn_c, sign):
        return jnp.concatenate([sign[0] * fn_r, sign[1] * fn_r, sign[0] * fn_c, sign[1] * fn_c], axis=-1)

    cos = head(jnp.cos(ang_r), jnp.cos(ang_c), (1.0, 1.0))
    sin = head(jnp.sin(ang_r), jnp.sin(ang_c), (-1.0, 1.0))
    cos = jnp.concatenate([jnp.ones((TM, HD), F32), cos], axis=0)
    sin = jnp.concatenate([jnp.zeros((TM, HD), F32), sin], axis=0)
    return jnp.tile(cos, (1, 2)), jnp.tile(sin, (1, 2))


def _inproj(x, g, mod, w_bf16, rope, rope_cols):
    n_out = w_bf16.shape[1]
    npb = NP_TOK // TM
    spb = DEC_SEQ // TM

    def rope_map(i):
        return (jnp.where(i < npb, 0, 1 + (i - npb) % spb), 0)

    return pl.pallas_call(
        functools.partial(_inproj_kernel, rope_cols=rope_cols),
        grid=(N_TOK // TM,),
        in_specs=_pair_specs(D) + [
                  pl.BlockSpec((1, D), lambda i: (0, 0)),
                  _mod_spec(0), _mod_spec(1),
                  pl.BlockSpec((D, n_out), lambda i: (0, 0)),
                  pl.BlockSpec((TM, LANES), rope_map),
                  pl.BlockSpec((TM, LANES), rope_map)],
        out_specs=pl.BlockSpec((TM, n_out), lambda i: (i, 0)),
        out_shape=jax.ShapeDtypeStruct((N_TOK, n_out), F32),
        compiler_params=_cparams(1),
        name="inproj",
    )(x[0], x[1], g.reshape(1, D), mod, mod, w_bf16, rope[0], rope[1])


def _softmax_av(s_list, v_list, sink=None):
    mx = s_list[0].max(axis=-1, keepdims=True)
    for s in s_list[1:]:
        mx = jnp.maximum(mx, s.max(axis=-1, keepdims=True))
    if sink is not None:
        mx = jnp.maximum(mx, sink)
    den = jnp.exp(sink - mx) if sink is not None else 0.0
    acc = None
    for s, v in zip(s_list, v_list):
        p = jnp.exp(s - mx)
        den = den + p.sum(axis=-1, keepdims=True)
        pv = jnp.dot(p.astype(BF16), v, preferred_element_type=F32)
        acc = pv if acc is None else acc + pv
    return acc / den


def _dup_half(x, j, lo):
    xr = pltpu.roll(x, HD, 1)
    return jnp.where(lo, x, xr) if j == 0 else jnp.where(lo, xr, x)


def _stack_heads(q_ref, heads, lo, scale):
    parts = []
    for h in heads:
        qp = q_ref[:, (h // 2) * LANES:(h // 2 + 1) * LANES]
        keep = lo if h % 2 == 0 else jnp.logical_not(lo)
        parts.append(jnp.where(keep, qp, 0.0) * scale)
    return jnp.concatenate(parts, axis=0).astype(BF16)


def _sink_column(sink_ref, heads, rows):
    return jnp.concatenate([jnp.full((rows, 1), sink_ref[h], F32) for h in heads], axis=0)


def _ctx_gqa_kernel(sink_ref, q_ref, k_ref, v_ref, o_ref):
    lo = _lane_lo()
    k = k_ref[...]
    v = v_ref[...]
    group = A_HEADS // A_KV_HEADS
    scores = []
    for j in range(A_KV_HEADS):
        q = _stack_heads(q_ref, list(range(group * j, group * (j + 1))), lo, HD ** -0.5)
        scores.append(lax.dot_general(q, _dup_half(k, j, lo).astype(BF16), NT_DIMS, preferred_element_type=F32))
    s = jnp.concatenate(scores, axis=0)
    sink = _sink_column(sink_ref, list(range(A_HEADS)), SEQ)
    mx = jnp.maximum(s.max(axis=-1, keepdims=True), sink)
    e = jnp.exp(s - mx)
    den = jnp.exp(sink - mx) + e.sum(axis=-1, keepdims=True)
    e = e.astype(BF16)
    rows_per_group = group * SEQ
    for j in range(A_KV_HEADS):
        rows = slice(j * rows_per_group, (j + 1) * rows_per_group)
        o = jnp.dot(e[rows], _dup_half(v, j, lo).astype(BF16), preferred_element_type=F32) / den[rows]
        for t in range(group // 2):
            pair = (group * j) // 2 + t
            o_ref[:, pair * LANES:(pair + 1) * LANES] = jnp.where(
                lo, o[(2 * t) * SEQ:(2 * t + 1) * SEQ], o[(2 * t + 1) * SEQ:(2 * t + 2) * SEQ])


def _ctx_gqa(p, sink):
    return pl.pallas_call(
        _ctx_gqa_kernel,
        grid_spec=pltpu.PrefetchScalarGridSpec(
            num_scalar_prefetch=1,
            grid=(BATCH,),
            in_specs=[pl.BlockSpec((SEQ, A_Q), lambda b, s: (b, 0)),
                      pl.BlockSpec((SEQ, A_KV), lambda b, s: (b, A_Q // A_KV)),
                      pl.BlockSpec((SEQ, A_KV), lambda b, s: (b, A_Q // A_KV + 1))],
            out_specs=pl.BlockSpec((SEQ, A_Q), lambda b, s: (b, 0))),
        out_shape=jax.ShapeDtypeStruct((NP_TOK, A_Q), F32),
        compiler_params=_cparams(1),
        name="ctx_gqa",
    )(sink, p, p, p)


def _win_kernel(sink_ref, q_ref, kp_ref, kc_ref, kn_ref, vp_ref, vc_ref, vn_ref, ck_ref, cv_ref, o_ref):
    i = pl.program_id(1)
    lo = _lane_lo()
    k = jnp.concatenate([kp_ref[...], kc_ref[...], kn_ref[...]], axis=0)
    v = jnp.concatenate([vp_ref[...], vc_ref[...], vn_ref[...]], axis=0)
    ck = ck_ref[...]
    cv = cv_ref[...]
    group = A_HEADS // A_KV_HEADS
    n_keys = WIN_Q + 2 * A_WINDOW
    qpos = i * WIN_Q + lax.broadcasted_iota(jnp.int32, (WIN_Q, n_keys), 0)
    kpos = i * WIN_Q - A_WINDOW + lax.broadcasted_iota(jnp.int32, (WIN_Q, n_keys), 1)
    valid = (jnp.abs(kpos - qpos) <= A_WINDOW) & (kpos >= 0) & (kpos < DEC_SEQ)
    valid = jnp.concatenate([valid] * group, axis=0)
    s_loc, s_ctx, values = [], [], []
    for j in range(A_KV_HEADS):
        heads = list(range(group * j, group * (j + 1)))
        kd = _dup_half(k, j, lo).astype(BF16)
        ckd = _dup_half(ck, j, lo).astype(BF16)
        values.append((_dup_half(v, j, lo).astype(BF16), _dup_half(cv, j, lo).astype(BF16)))
        q = _stack_heads(q_ref, heads, lo, HD ** -0.5)
        s_loc.append(jnp.where(valid, lax.dot_general(q, kd, NT_DIMS, preferred_element_type=F32), NEG))
        s_ctx.append(lax.dot_general(q, ckd, NT_DIMS, preferred_element_type=F32))
    s_loc = jnp.concatenate(s_loc, axis=0)
    s_ctx = jnp.concatenate(s_ctx, axis=0)
    sink = _sink_column(sink_ref, list(range(A_HEADS)), WIN_Q)
    mx = jnp.maximum(jnp.maximum(s_loc.max(axis=-1, keepdims=True), s_ctx.max(axis=-1, keepdims=True)), sink)
    p_loc = jnp.exp(s_loc - mx)
    p_ctx = jnp.exp(s_ctx - mx)
    den = p_loc.sum(axis=-1, keepdims=True) + p_ctx.sum(axis=-1, keepdims=True) + jnp.exp(sink - mx)
    p_loc = p_loc.astype(BF16)
    p_ctx = p_ctx.astype(BF16)
    rows_per_group = group * WIN_Q
    for j, (vd, cvd) in enumerate(values):
        rows = slice(j * rows_per_group, (j + 1) * rows_per_group)
        o = (jnp.dot(p_loc[rows], vd, preferred_element_type=F32)
             + jnp.dot(p_ctx[rows], cvd, preferred_element_type=F32)) / den[rows]
        for t in range(group // 2):
            pair = (group * j) // 2 + t
            o_ref[:, pair * LANES:(pair + 1) * LANES] = jnp.where(
                lo, o[(2 * t) * WIN_Q:(2 * t + 1) * WIN_Q], o[(2 * t + 1) * WIN_Q:(2 * t + 2) * WIN_Q])


WIN_Q = 256


def _win_attention(p, cache_k, cache_v, sink):
    nblk = DEC_SEQ // WIN_Q
    side = WIN_Q // A_WINDOW
    nside = DEC_SEQ // A_WINDOW
    base = NP_TOK // WIN_Q
    side_base = NP_TOK // A_WINDOW
    kcol = A_Q // A_KV

    def main_spec(col):
        return pl.BlockSpec((WIN_Q, A_KV), lambda b, i, s: (base + b * nblk + i, col))

    def side_spec(col, off):
        return pl.BlockSpec((A_WINDOW, A_KV),
                            lambda b, i, s: (side_base + b * nside + jnp.clip(side * i + off, 0, nside - 1), col))

    ctx_spec = pl.BlockSpec((None, PAST, A_KV), lambda b, i, s: (b, 0, 0))
    return pl.pallas_call(
        _win_kernel,
        grid_spec=pltpu.PrefetchScalarGridSpec(
            num_scalar_prefetch=1,
            grid=(DEC_BATCH, nblk),
            in_specs=[pl.BlockSpec((WIN_Q, A_Q), lambda b, i, s: (base + b * nblk + i, 0)),
                      side_spec(kcol, -1), main_spec(kcol), side_spec(kcol, side),
                      side_spec(kcol + 1, -1), main_spec(kcol + 1), side_spec(kcol + 1, side),
                      ctx_spec, ctx_spec],
            out_specs=pl.BlockSpec((WIN_Q, A_Q), lambda b, i, s: (b * nblk + i, 0))),
        out_shape=jax.ShapeDtypeStruct((NS_TOK, A_Q), F32),
        compiler_params=_cparams(2),
        name="win_attn",
    )(sink, p, p, p, p, p, p, p, cache_k.reshape(DEC_BATCH, PAST, A_KV), cache_v.reshape(DEC_BATCH, PAST, A_KV))


def _ret_kernel(df_ref, db_ref, q_ref, k_ref, v_ref, g_ref, gn_ref, s0f_ref, s0b_ref,
                o_ref, sf_ref, sb_ref, of_scr, ob_scr, *, length):
    c_len = RET_CHUNK
    n = length // c_len
    lo = _lane_lo()
    hi = jnp.logical_not(lo)
    row = lax.broadcasted_iota(jnp.int32, (c_len, c_len), 0)
    col = lax.broadcasted_iota(jnp.int32, (c_len, c_len), 1)
    rowp = lax.broadcasted_iota(jnp.int32, (LANES, LANES), 0)
    colp = lax.broadcasted_iota(jnp.int32, (LANES, LANES), 1)
    blockdiag = (rowp < HD) == (colp < HD)
    idx = lax.broadcasted_iota(jnp.int32, (c_len, 1), 0).astype(F32)

    def direction(dec_ref, forward):
        lg = -jnp.exp(dec_ref[...])
        diff = (row - col) if forward else (col - row)
        keep = (diff >= 0) if forward else (diff > 0)
        dist = jnp.maximum(diff, 0).astype(F32)
        dm = jnp.concatenate([jnp.where(keep, jnp.exp(dist * lg[:, off:off + 1]), 0.0) for off in (0, HD)], axis=0)
        if forward:
            xi = jnp.exp((idx + 1.0) * lg)
            zeta = jnp.exp((c_len - 1.0 - idx) * lg)
        else:
            xi = jnp.exp((c_len - idx) * lg)
            zeta = jnp.exp(idx * lg)
        return dm, xi, zeta, jnp.exp(c_len * lg)

    def chunk(c, state, consts):
        dm, xi, zeta, gch = consts
        rows = pl.ds(pl.multiple_of(c * c_len, c_len), c_len)
        qc = q_ref[rows, :]
        kc = k_ref[rows, :] * HD ** -0.5
        vc = v_ref[rows, :].astype(BF16)
        kb = kc.astype(BF16)
        q2 = jnp.concatenate([jnp.where(lo, qc, 0.0), jnp.where(hi, qc, 0.0)], axis=0).astype(BF16)
        inner = lax.dot_general(q2, kb, NT_DIMS, preferred_element_type=F32) * dm
        kz_t = (kc * zeta).T
        res = jnp.dot(jnp.concatenate([inner, kz_t], axis=0).astype(BF16), vc, preferred_element_type=F32)
        cross = jnp.dot(qc.astype(BF16), state.astype(BF16), preferred_element_type=F32) * xi
        o = jnp.where(lo, res[:c_len], res[c_len:2 * c_len]) + cross
        state = gch * state + jnp.where(blockdiag, res[2 * c_len:], 0.0)
        return rows, o, state

    cf = direction(df_ref, True)
    cb = direction(db_ref, False)

    def scan_body(t, states):
        rows_f, o_f, state_f = chunk(t, states[0], cf)
        of_scr[rows_f, :] = o_f
        rows_b, o_b, state_b = chunk(n - 1 - t, states[1], cb)
        ob_scr[rows_b, :] = o_b
        return state_f, state_b

    state_f, state_b = lax.fori_loop(0, n, scan_body, (s0f_ref[...], s0b_ref[...]), unroll=min(n, RET_UNROLL))
    sf_ref[...] = state_f
    sb_ref[...] = state_b

    gn = gn_ref[...]
    norm_rows = min(RET_NORM_ROWS, length)
    n_norm = length // norm_rows

    def per_head(x):
        a = jnp.where(lo, x, 0.0).sum(axis=-1, keepdims=True)
        b = jnp.where(hi, x, 0.0).sum(axis=-1, keepdims=True)
        return jnp.where(lo, a, b) * (1.0 / HD)

    def norm_body(t, carry):
        rows = pl.ds(pl.multiple_of(t * norm_rows, norm_rows), norm_rows)
        o = of_scr[rows, :] + ob_scr[rows, :]
        d = o - per_head(o)
        y = d * lax.rsqrt(per_head(d * d) + EPS) * gn
        o_ref[rows, :] = _silu(g_ref[rows, :]) * y
        return carry

    lax.fori_loop(0, n_norm, norm_body, 0)


def _pair_lanes(v):
    return jnp.repeat(v.astype(F32), HD).reshape(B_HEADS // 2, 1, LANES)


def _blockdiag_states(s):
    b = s.shape[0]
    s = s.astype(F32).reshape(b, B_HEADS // 2, 2, HD, HD)
    z = jnp.zeros_like(s[:, :, 0])
    top = jnp.concatenate([s[:, :, 0], z], axis=-1)
    bot = jnp.concatenate([z, s[:, :, 1]], axis=-1)
    return jnp.concatenate([top, bot], axis=-2)


def _diag_states(sp):
    b = sp.shape[0]
    s = jnp.stack([sp[:, :, :HD, :HD], sp[:, :, HD:, HD:]], axis=2)
    return s.reshape(b, B_HEADS, HD, HD)


def _retention(p, row_base, batch, length, dec_f, dec_b, gn_g, s0f, s0b):
    npairs = B_HEADS // 2
    blk0 = row_base // length
    qcol = (A_Q + 2 * A_KV) // LANES

    def col_spec(off):
        return pl.BlockSpec((length, LANES), lambda b, h: (blk0 + b, qcol + off * npairs + h))

    lane_spec = pl.BlockSpec((None, 1, LANES), lambda b, h: (h, 0, 0))
    state_spec = pl.BlockSpec((None, None, LANES, LANES), lambda b, h: (b, h, 0, 0))
    state_shape = jax.ShapeDtypeStruct((batch, npairs, LANES, LANES), F32)
    return pl.pallas_call(
        functools.partial(_ret_kernel, length=length),
        grid=(batch, npairs),
        in_specs=[lane_spec, lane_spec, col_spec(0), col_spec(1), col_spec(2), col_spec(3), lane_spec,
                  state_spec, state_spec],
        out_specs=[pl.BlockSpec((length, LANES), lambda b, h: (b, h)), state_spec, state_spec],
        out_shape=[jax.ShapeDtypeStruct((batch * length, B_W), F32), state_shape, state_shape],
        scratch_shapes=[pltpu.VMEM((length, LANES), F32), pltpu.VMEM((length, LANES), F32)],
        compiler_params=_cparams(2),
        name="retention",
    )(_pair_lanes(dec_f), _pair_lanes(dec_b), p, p, p, p, gn_g.reshape(npairs, 1, LANES), s0f, s0b)


def _ctx_mha_kernel(q_ref, k_ref, v_ref, o_ref):
    lo = _lane_lo()
    for pair in range(C_HEADS // 2):
        cols = slice(pair * LANES, (pair + 1) * LANES)
        q = _stack_heads(q_ref, [2 * pair, 2 * pair + 1], lo, HD ** -0.5)
        s = lax.dot_general(q, k_ref[:, cols].astype(BF16), NT_DIMS, preferred_element_type=F32)
        o = _softmax_av([s], [v_ref[:, cols].astype(BF16)])
        o_ref[:, cols] = jnp.where(lo, o[:SEQ], o[SEQ:])


def _ctx_mha(p):
    return pl.pallas_call(
        _ctx_mha_kernel,
        grid=(BATCH,),
        in_specs=[pl.BlockSpec((SEQ, C_W), lambda b: (b, 0)),
                  pl.BlockSpec((SEQ, C_W), lambda b: (b, 1)),
                  pl.BlockSpec((SEQ, C_W), lambda b: (b, 2))],
        out_specs=pl.BlockSpec((SEQ, C_W), lambda b: (b, 0)),
        out_shape=jax.ShapeDtypeStruct((NP_TOK, C_W), F32),
        compiler_params=_cparams(1),
        name="ctx_mha",
    )(p, p, p)


NA_WIN_ROWS = 2 * NA_ROWS
NA_WIN = NA_WIN_ROWS * GRID_W
NA_QROWS = NA_ROWS * GRID_W
NA_PAD_ROWS = NA_KH // 2
NA_TABLE = 1536


def _na_kernel(q_ref, kp_ref, km_ref, kn_ref, vp_ref, vm_ref, vn_ref, ck_ref, cv_ref, ue_ref, uo_ref, o_ref):
    r0 = pl.program_id(2) * NA_ROWS
    n_rows = DEC_SEQ // GRID_W
    lo = _lane_lo()
    k = jnp.concatenate([kp_ref[...], km_ref[...], kn_ref[...]], axis=0).astype(BF16)
    v = jnp.concatenate([vp_ref[...], vm_ref[...], vn_ref[...]], axis=0).astype(BF16)
    ck = ck_ref[...].astype(BF16)
    cv = cv_ref[...].astype(BF16)
    q = q_ref[...] * HD ** -0.5
    klane = lax.broadcasted_iota(jnp.int32, (1, NA_WIN), 1)
    outs = []
    for half, keep in enumerate((lo, jnp.logical_not(lo))):
        qh = jnp.where(keep, q, 0.0).astype(BF16)
        s = lax.dot_general(qh, k, NT_DIMS, preferred_element_type=F32)
        s_ctx = lax.dot_general(qh, ck, NT_DIMS, preferred_element_type=F32)
        p_loc, p_ctx, den = [], [], []
        for rq in range(NA_ROWS):
            rows = slice(rq * GRID_W, (rq + 1) * GRID_W)
            start = NA_KH - 1 - rq
            if start % 2 == 0:
                u = ue_ref[half, :, start * GRID_W:start * GRID_W + NA_WIN]
            else:
                u = uo_ref[half, :, (start - 1) * GRID_W:(start - 1) * GRID_W + NA_WIN]
            r = r0 + rq
            first = jnp.clip(r - NA_KH // 2, 0, n_rows - NA_KH)
            lane0 = (first - r0 + NA_PAD_ROWS) * GRID_W
            in_rows = (klane >= lane0) & (klane < lane0 + NA_KH * GRID_W)
            sl = jnp.where(in_rows, s[rows] + u, NEG)
            sc = s_ctx[rows]
            mx = jnp.maximum(sl.max(axis=-1, keepdims=True), sc.max(axis=-1, keepdims=True))
            el = jnp.exp(sl - mx)
            ec = jnp.exp(sc - mx)
            den.append(el.sum(axis=-1, keepdims=True) + ec.sum(axis=-1, keepdims=True))
            p_loc.append(el.astype(BF16))
            p_ctx.append(ec.astype(BF16))
        acc = (jnp.dot(jnp.concatenate(p_loc, axis=0), v, preferred_element_type=F32)
               + jnp.dot(jnp.concatenate(p_ctx, axis=0), cv, preferred_element_type=F32))
        outs.append(acc / jnp.concatenate(den, axis=0))
    o_ref[...] = jnp.where(lo, outs[0], outs[1])


def _na_bias_tables(rpb):
    cq = jnp.arange(GRID_W)
    ck = jnp.arange(GRID_W)
    dc = jnp.clip(ck[None] - cq[:, None], -(NA_KW - 1), NA_KW - 1) + NA_KW - 1
    cs = jnp.clip(cq - NA_KW // 2, 0, GRID_W - NA_KW)
    col_ok = (ck[None] >= cs[:, None]) & (ck[None] < cs[:, None] + NA_KW)
    t = rpb.astype(F32)[:, :, dc]
    t = jnp.where(col_ok[None, None], t, NEG).transpose(0, 2, 1, 3)
    n_dr = 2 * NA_KH - 1
    blocks = NA_TABLE // GRID_W
    t = jnp.pad(t, ((0, 0), (0, 0), (NA_PAD_ROWS, blocks - n_dr - NA_PAD_ROWS), (0, 0)), constant_values=NEG)
    ue = t.reshape(C_HEADS, GRID_W, NA_TABLE)
    uo = jnp.concatenate([ue[..., GRID_W:], jnp.full((C_HEADS, GRID_W, GRID_W), NEG, F32)], axis=-1)
    return ue, uo


def _na_attention(p, cache_k, cache_v, rpb):
    npairs = C_HEADS // 2
    nrb = DEC_SEQ // NA_QROWS
    half = NA_QROWS // 2
    qbase = NP_TOK // NA_QROWS
    hbase = NP_TOK // half
    kcol = C_W // LANES
    ue, uo = _na_bias_tables(rpb)

    def main_spec(col0):
        return pl.BlockSpec((NA_QROWS, LANES), lambda b, h, r: (qbase + b * nrb + r, col0 + h))

    def side_spec(col0, off):
        return pl.BlockSpec((half, LANES),
                            lambda b, h, r: (hbase + b * 2 * nrb + jnp.clip(2 * r + off, 0, 2 * nrb - 1), col0 + h))

    ctx_spec = pl.BlockSpec((None, PAST, LANES), lambda b, h, r: (b, 0, h))
    tab_spec = pl.BlockSpec((2, GRID_W, NA_TABLE), lambda b, h, r: (h, 0, 0))
    return pl.pallas_call(
        _na_kernel,
        grid=(DEC_BATCH, npairs, nrb),
        in_specs=[main_spec(0),
                  side_spec(kcol, -1), main_spec(kcol), side_spec(kcol, 2),
                  side_spec(2 * kcol, -1), main_spec(2 * kcol), side_spec(2 * kcol, 2),
                  ctx_spec, ctx_spec, tab_spec, tab_spec],
        out_specs=pl.BlockSpec((NA_QROWS, LANES), lambda b, h, r: (b * nrb + r, h)),
        out_shape=jax.ShapeDtypeStruct((NS_TOK, C_W), F32),
        compiler_params=_cparams(3),
        name="na_attn",
    )(p, p, p, p, p, p, p, cache_k.reshape(DEC_BATCH, PAST, C_W), cache_v.reshape(DEC_BATCH, PAST, C_W), ue, uo)


def _route(biased, scores):
    t = biased.shape[1]
    per_group = N_EXPERTS // N_GROUPS
    i8 = lax.broadcasted_iota(jnp.int32, (per_group, t), 0)
    g_rows = []
    for g in range(N_GROUPS):
        bg = biased[g * per_group:(g + 1) * per_group]
        m1 = bg.max(axis=0, keepdims=True)
        first = jnp.where(bg == m1, i8, per_group).min(axis=0, keepdims=True)
        m2 = jnp.where(i8 == first, -jnp.inf, bg).max(axis=0, keepdims=True)
        g_rows.append(m1 + m2)
    g_top = jnp.concatenate(g_rows, axis=0)
    gi = lax.broadcasted_iota(jnp.int32, g_top.shape, 0)
    g_sel = jnp.zeros(g_top.shape, jnp.int32)
    cur = g_top
    for _ in range(TOPK_GROUPS):
        m = cur.max(axis=0, keepdims=True)
        hit = gi == jnp.where(cur == m, gi, N_GROUPS).min(axis=0, keepdims=True)
        g_sel = jnp.where(hit, 1, g_sel)
        cur = jnp.where(hit, -jnp.inf, cur)
    e_sel = jnp.concatenate([jnp.broadcast_to(g_sel[g:g + 1], (per_group, t)) for g in range(N_GROUPS)], axis=0)
    cur = jnp.where(e_sel > 0, biased, NEG)
    ei = lax.broadcasted_iota(jnp.int32, cur.shape, 0)
    ids, gates, hits = [], [], []
    for _ in range(TOP_K):
        m = cur.max(axis=0, keepdims=True)
        f = jnp.where(cur == m, ei, N_EXPERTS).min(axis=0, keepdims=True)
        hit = ei == f
        ids.append(f)
        hits.append(hit)
        gates.append(jnp.where(hit, scores, 0.0).sum(axis=0, keepdims=True))
        cur = jnp.where(hit, -jnp.inf, cur)
    gate = jnp.concatenate(gates, axis=0)
    gate = gate / gate.sum(axis=0, keepdims=True) * ROUTED_SCALE
    return jnp.concatenate(ids, axis=0), gate, hits


def _pack_bf16_pairs(h):
    bits = lax.bitcast_convert_type(h.astype(BF16).astype(F32), jnp.uint32)
    return bits[:, :D // 2] | (bits[:, D // 2:] >> 16)


def _unpack_bf16_pairs(xp):
    hi = lax.bitcast_convert_type(xp & jnp.uint32(0xFFFF0000), F32).astype(BF16)
    lo = lax.bitcast_convert_type(xp << 16, F32).astype(BF16)
    return hi, lo


def _dot_halves(hi, lo, w_ref):
    return (jnp.dot(hi, w_ref[:D // 2, :], preferred_element_type=F32)
            + jnp.dot(lo, w_ref[D // 2:, :], preferred_element_type=F32))


def _outproj_kernel(*refs, n_parts):
    xp_ref, xs_ref = refs[:2]
    part_refs = refs[2:2 + 3 * n_parts]
    gate_ref, shift_ref, scale_ref, g2_ref, rw_ref, rb_ref = refs[2 + 3 * n_parts:8 + 3 * n_parts]
    xo_ref, h_ref, dest_ref, wgt_ref, plan_ref, cnt_ref = refs[8 + 3 * n_parts:]
    step = pl.program_id(0)

    @pl.when(step == 0)
    def _():
        cnt_ref[...] = jnp.zeros_like(cnt_ref)
        plan_ref[...] = jnp.zeros_like(plan_ref)

    y = None
    for t in range(n_parts):
        ap_ref, as_ref, w_ref = part_refs[3 * t:3 * t + 3]
        d = jnp.dot(_pick_rows(ap_ref, as_ref).astype(BF16), w_ref[...], preferred_element_type=F32)
        y = d if y is None else y + d
    x = _pick_rows(xp_ref, xs_ref) + gate_ref[...] * y
    xo_ref[...] = x
    h = _rms(x, g2_ref[...]) * (1.0 + scale_ref[...]) + shift_ref[...]
    h_ref[...] = _pack_bf16_pairs(h)
    h_hi = h.astype(BF16)
    h_lo = (h - h_hi.astype(F32)).astype(BF16)
    rw = rw_ref[...]
    rw_hi = rw.astype(BF16)
    rw_lo = (rw - rw_hi.astype(F32)).astype(BF16)
    logits = (lax.dot_general(rw_hi, h_hi, NT_DIMS, preferred_element_type=F32)
              + lax.dot_general(rw_hi, h_lo, NT_DIMS, preferred_element_type=F32)
              + lax.dot_general(rw_lo, h_hi, NT_DIMS, preferred_element_type=F32))
    scores = jax.nn.sigmoid(logits)
    _, gate, hits = _route(scores + rb_ref[...], scores)
    wgt_ref[...] = gate
    chosen = hits[0]
    for hit in hits[1:]:
        chosen = chosen | hit
    m = jnp.where(chosen, 1.0, 0.0)
    before = (lax.broadcasted_iota(jnp.int32, (TM, TM), 0) < lax.broadcasted_iota(jnp.int32, (TM, TM), 1))
    prefix = jnp.dot(m.astype(BF16), jnp.where(before, 1.0, 0.0).astype(BF16), preferred_element_type=F32)
    e_base = (lax.broadcasted_iota(jnp.int32, (N_EXPERTS, 1), 0) * N_TOK).astype(F32)
    row_all = prefix + (cnt_ref[...] + e_base)
    dest_ref[...] = jnp.concatenate(
        [jnp.where(hit, row_all, 0.0).sum(axis=0, keepdims=True) for hit in hits], axis=0).astype(jnp.int32)
    cnt_ref[...] += m.sum(axis=1, keepdims=True)

    @pl.when(step == pl.num_programs(0) - 1)
    def _():
        _block_plan(cnt_ref[...], plan_ref)


def _block_plan(counts, plan_ref):
    cap_blocks = N_TOK // MOE_BLOCK
    nblk = ((counts.astype(jnp.int32) + (MOE_BLOCK - 1)) // MOE_BLOCK).astype(F32)
    lower = (lax.broadcasted_iota(jnp.int32, (N_EXPERTS, N_EXPERTS), 0)
             >= lax.broadcasted_iota(jnp.int32, (N_EXPERTS, N_EXPERTS), 1))
    cum = jnp.dot(jnp.where(lower, 1.0, 0.0).astype(BF16), jnp.broadcast_to(nblk, (N_EXPERTS, LANES)).astype(BF16),
                  preferred_element_type=F32)[:, :1]
    n_used = cum[N_EXPERTS - 1:, :]
    slot = jnp.minimum(lax.broadcasted_iota(jnp.int32, (1, PLAN_LANES), 1).astype(F32), n_used - 1.0)
    done = cum <= slot
    expert = jnp.where(done, 1.0, 0.0).sum(axis=0, keepdims=True)
    blocks_before = jnp.where(done, nblk, 0.0).sum(axis=0, keepdims=True)
    plan_ref[0:1, :] = (expert * cap_blocks + (slot - blocks_before)).astype(jnp.int32)
    plan_ref[1:2, :] = expert.astype(jnp.int32)
    plan_ref[2:3, :] = jnp.broadcast_to(n_used, (1, PLAN_LANES)).astype(jnp.int32)


def _outproj(x, parts, mod, g2, router_w, router_b):
    in_specs = _pair_specs(D)
    args = [x[0], x[1]]
    for ap, a_s, w in parts:
        width = ap.shape[1]
        in_specs += _pair_specs(width) + [pl.BlockSpec((width, D), lambda i: (0, 0))]
        args += [ap, a_s, w]
    in_specs += [_mod_spec(2), _mod_spec(3), _mod_spec(4),
                 pl.BlockSpec((1, D), lambda i: (0, 0)),
                 pl.BlockSpec((N_EXPERTS, D), lambda i: (0, 0)),
                 pl.BlockSpec((N_EXPERTS, 1), lambda i: (0, 0))]
    args += [mod, mod, mod, g2.reshape(1, D), router_w.T, router_b.reshape(N_EXPERTS, 1)]
    return pl.pallas_call(
        functools.partial(_outproj_kernel, n_parts=len(parts)),
        grid=(N_TOK // TM,),
        in_specs=in_specs,
        out_specs=[pl.BlockSpec((TM, D), lambda i: (i, 0)),
                   pl.BlockSpec((TM, D // 2), lambda i: (i, 0)),
                   pl.BlockSpec((TOP_K, TM), lambda i: (0, i)),
                   pl.BlockSpec((TOP_K, TM), lambda i: (0, i)),
                   pl.BlockSpec((8, PLAN_LANES), lambda i: (0, 0))],
        out_shape=[jax.ShapeDtypeStruct((N_TOK, D), F32),
                   jax.ShapeDtypeStruct((N_TOK, D // 2), jnp.uint32),
                   jax.ShapeDtypeStruct((TOP_K, N_TOK), jnp.int32),
                   jax.ShapeDtypeStruct((TOP_K, N_TOK), F32),
                   jax.ShapeDtypeStruct((8, PLAN_LANES), jnp.int32)],
        scratch_shapes=[pltpu.VMEM((N_EXPERTS, 1), F32)],
        compiler_params=_cparams(1),
        name="outproj_router",
    )(*args)


def _experts_kernel(br_ref, be_ref, nu_ref, x_ref, w1_ref, w3_ref, w2_ref, o_ref, w1b, w3b, w2b):
    i = pl.program_id(0)
    e = be_ref[i]
    prev = be_ref[jnp.maximum(i - 1, 0)]

    @pl.when((i == 0) | (e != prev))
    def _():
        w1b[...] = w1_ref[...].astype(BF16)
        w3b[...] = w3_ref[...].astype(BF16)
        w2b[...] = w2_ref[...].astype(BF16)

    @pl.when(i < nu_ref[0])
    def _():
        hi, lo = _unpack_bf16_pairs(x_ref[...])
        a = _dot_halves(hi, lo, w1b)
        b = _dot_halves(hi, lo, w3b)
        h = (_silu(a) * b).astype(BF16)
        o_ref[...] = _pack_bf16_pairs(jnp.dot(h, w2b[...], preferred_element_type=F32))


def _experts(plan, x_rows, w1, w3, w2, layer):
    return pl.pallas_call(
        _experts_kernel,
        grid_spec=pltpu.PrefetchScalarGridSpec(
            num_scalar_prefetch=3,
            grid=(N_MOE_BLOCKS,),
            in_specs=[pl.BlockSpec((MOE_BLOCK, D // 2), lambda i, br, be, nu: (br[i], 0)),
                      pl.BlockSpec((None, None, D, FF), lambda i, br, be, nu: (layer, be[i], 0, 0)),
                      pl.BlockSpec((None, None, D, FF), lambda i, br, be, nu: (layer, be[i], 0, 0)),
                      pl.BlockSpec((None, None, FF, D), lambda i, br, be, nu: (layer, be[i], 0, 0))],
            out_specs=pl.BlockSpec((MOE_BLOCK, D // 2), lambda i, br, be, nu: (br[i], 0)),
            scratch_shapes=[pltpu.VMEM((D, FF), BF16), pltpu.VMEM((D, FF), BF16), pltpu.VMEM((FF, D), BF16)]),
        out_shape=jax.ShapeDtypeStruct(x_rows.shape, jnp.uint32),
        compiler_params=_cparams(1),
        name="experts",
    )(plan[0], plan[1], plan[2, :1], x_rows, w1, w3, w2)


SC_CORES = 2
SC_SUBCORES = 16
SC_WORKERS = SC_CORES * SC_SUBCORES
SC_CHUNK_BYTES = 64 * 1024
SC_SLOTS = 4


def _sc_scatter(rows, dest, n_out):
    n_rows, width = rows.shape
    picks = dest.shape[0]
    chunk = SC_CHUNK_BYTES // (4 * width)
    per_worker = n_rows // SC_WORKERS
    n_chunks = per_worker // chunk
    assert per_worker * SC_WORKERS == n_rows and n_chunks * chunk == per_worker and n_chunks % 2 == 0
    mesh = plsc.VectorSubcoreMesh(core_axis_name="c", subcore_axis_name="s")

    @functools.partial(
        pl.kernel, mesh=mesh,
        out_type=jax.ShapeDtypeStruct((n_out, width), rows.dtype),
        scratch_types=[pltpu.VMEM((picks, n_chunks, chunk), jnp.int32),
                       pltpu.VMEM((2, chunk, width), rows.dtype),
                       pltpu.SemaphoreType.DMA((2,)),
                       pltpu.SemaphoreType.DMA((2,))])
    def scatter(r_hbm, d_hbm, o_hbm, idx_v, rows_v, lsem, ssem):
        worker = lax.axis_index("s") * SC_CORES + lax.axis_index("c")
        base = worker * per_worker
        for k in range(picks):
            pltpu.sync_copy(d_hbm.at[k, worker], idx_v.at[k])

        def load_copy(c, b):
            src = pl.ds(pl.multiple_of(base + c * chunk, chunk), chunk)
            return pltpu.make_async_copy(r_hbm.at[src], rows_v.at[b], lsem.at[b])

        def store_copy(c, b, k):
            return pltpu.make_async_copy(rows_v.at[b], o_hbm.at[idx_v.at[k, c]], ssem.at[b])

        load_copy(0, 0).start()

        @pl.loop(0, n_chunks, step=2)
        def _(c0):
            for b in range(2):
                c = c0 + b
                load_copy(c, b).wait()
                for k in range(picks):
                    store_copy(c, b, k).start()

                @pl.when(c > 0)
                def _():
                    for k in range(picks):
                        store_copy(c - 1, 1 - b, k).wait()

                @pl.when(c + 1 < n_chunks)
                def _():
                    load_copy(c + 1, 1 - b).start()

        for k in range(picks):
            store_copy(n_chunks - 1, 1, k).wait()

    return scatter(rows, dest.reshape(picks, SC_WORKERS, n_chunks, chunk))


def _sc_gather(table, idx):
    n_idx = idx.shape[0]
    width = table.shape[1]
    chunk = SC_CHUNK_BYTES // (4 * width)
    per_worker = n_idx // SC_WORKERS
    n_chunks = per_worker // chunk
    ahead = SC_SLOTS - 1
    assert per_worker * SC_WORKERS == n_idx and n_chunks * chunk == per_worker and n_chunks % SC_SLOTS == 0
    mesh = plsc.VectorSubcoreMesh(core_axis_name="c", subcore_axis_name="s")

    @functools.partial(
        pl.kernel, mesh=mesh,
        out_type=jax.ShapeDtypeStruct((n_idx, width), table.dtype),
        scratch_types=[pltpu.VMEM((per_worker,), jnp.int32),
                       pltpu.VMEM((SC_SLOTS, chunk, width), table.dtype),
                       pltpu.SemaphoreType.DMA((SC_SLOTS,)),
                       pltpu.SemaphoreType.DMA((SC_SLOTS,))])
    def gather(t_hbm, i_hbm, o_hbm, idx_v, rows_v, gsem, wsem):
        worker = lax.axis_index("s") * SC_CORES + lax.axis_index("c")
        base = worker * per_worker
        pltpu.sync_copy(i_hbm.at[pl.ds(pl.multiple_of(base, chunk), per_worker)], idx_v)

        def gather_copy(c, b):
            ids = idx_v.at[pl.ds(pl.multiple_of(c * chunk, chunk), chunk)]
            return pltpu.make_async_copy(t_hbm.at[ids], rows_v.at[b], gsem.at[b])

        def write_copy(c, b):
            rows = pl.ds(pl.multiple_of(base + c * chunk, chunk), chunk)
            return pltpu.make_async_copy(rows_v.at[b], o_hbm.at[rows], wsem.at[b])

        for c in range(ahead):
            gather_copy(c, c).start()

        @pl.loop(0, n_chunks, step=SC_SLOTS)
        def _(c0):
            for b in range(SC_SLOTS):
                c = c0 + b
                refill = (b + ahead) % SC_SLOTS
                gather_copy(c, b).wait()
                write_copy(c, b).start()

                @pl.when(c > 0)
                def _():
                    write_copy(c - 1, refill).wait()

                @pl.when(c + ahead < n_chunks)
                def _():
                    gather_copy(c + ahead, refill).start()

        write_copy(n_chunks - 1, (n_chunks - 1) % SC_SLOTS).wait()

    return gather(table, idx)


TC = 512


def _combine_kernel(x_ref, h_ref, y_ref, wgt_ref, gate_ref, w1_ref, w3_ref, w2_ref, fg_ref, o_ref, *, final):
    hi, lo = _unpack_bf16_pairs(h_ref[...])
    a = _dot_halves(hi, lo, w1_ref)
    b = _dot_halves(hi, lo, w3_ref)
    ffn = jnp.dot((_silu(a) * b).astype(BF16), w2_ref[...], preferred_element_type=F32)
    wgt = wgt_ref[...]
    r_hi = None
    r_lo = None
    for k in range(TOP_K):
        yk = y_ref[k]
        w = wgt[:, k:k + 1]
        t_hi = lax.bitcast_convert_type(yk & jnp.uint32(0xFFFF0000), F32) * w
        t_lo = lax.bitcast_convert_type(yk << 16, F32) * w
        r_hi = t_hi if r_hi is None else r_hi + t_hi
        r_lo = t_lo if r_lo is None else r_lo + t_lo
    x = x_ref[...] + gate_ref[...] * (ffn + jnp.concatenate([r_hi, r_lo], axis=1))
    o_ref[...] = _rms(x, fg_ref[...]) if final else x


def _combine(x, h, y_rows, wgt, mod, sw1, sw3, sw2, final_g, final):
    weights = (sw1.astype(BF16), sw3.astype(BF16), sw2.astype(BF16), final_g.reshape(1, D))

    def rows_from(first_row, n_rows):
        b0 = first_row // TC
        return pl.pallas_call(
            functools.partial(_combine_kernel, final=final),
            grid=(n_rows // TC,),
            in_specs=[pl.BlockSpec((TC, D), lambda i: (i + b0, 0)),
                      pl.BlockSpec((TC, D // 2), lambda i: (i + b0, 0)),
                      pl.BlockSpec((TOP_K, TC, D // 2), lambda i: (0, i + b0, 0)),
                      pl.BlockSpec((TC, TOP_K), lambda i: (i + b0, 0)),
                      _mod_spec(5, TC, b0),
                      pl.BlockSpec((D, FF), lambda i: (0, 0)),
                      pl.BlockSpec((D, FF), lambda i: (0, 0)),
                      pl.BlockSpec((FF, D), lambda i: (0, 0)),
                      pl.BlockSpec((1, D), lambda i: (0, 0))],
            out_specs=pl.BlockSpec((TC, D), lambda i: (i, 0)),
            out_shape=jax.ShapeDtypeStruct((n_rows, D), F32),
            compiler_params=_cparams(1),
            name="combine",
        )(x, h, y_rows, wgt, mod, *weights)

    return rows_from(0, NP_TOK), rows_from(NP_TOK, NS_TOK)


def kernel(x_prompt, x_sample, cache_a_k, cache_a_v, state_ret_fwd, state_ret_bwd, cache_c_k, cache_c_v,
           c, c_ctx, norm1_g, norm2_g, ada_w, ada_b, even_w_in, even_w_out, sink_a, ret_decay_fwd,
           ret_decay_bwd, ret_gn_g, odd_w_in, odd_w_out, na_rpb, router_w, router_b, exp_w1, exp_w3,
           exp_w2, sh_w1, sh_w3, sh_w2, final_g):
    x = (x_prompt.reshape(NP_TOK, D), x_sample.reshape(NS_TOK, D))
    cc = jnp.concatenate([c_ctx[None], c, jnp.zeros((8 - 1 - DEC_BATCH, D), F32)], axis=0)
    rope = _rope_tables()
    outs = {}
    for l in range(2):
        mod = _ada(cc, ada_w, ada_b, l)
        if l == 0:
            p = _inproj(x, norm1_g[l], mod, even_w_in[0].astype(BF16), rope, A_Q + A_KV)
            oa_p = _ctx_gqa(p, sink_a[0])
            oa_s = _win_attention(p, cache_a_k[:, 0], cache_a_v[:, 0], sink_a[0])
            zero = jnp.zeros((BATCH, B_HEADS // 2, LANES, LANES), F32)
            ob_p, sf, sb = _retention(p, 0, BATCH, SEQ, ret_decay_fwd[0], ret_decay_bwd[0], ret_gn_g[0], zero, zero)
            ob_s, _, _ = _retention(p, NP_TOK, DEC_BATCH, DEC_SEQ, ret_decay_fwd[0], ret_decay_bwd[0], ret_gn_g[0],
                                    _blockdiag_states(state_ret_fwd[:, 0]), _blockdiag_states(state_ret_bwd[:, 0]))
            w_out = even_w_out[0].astype(BF16)
            parts = [(oa_p, oa_s, w_out[:A_Q]), (ob_p, ob_s, w_out[A_Q:])]
            outs["a_k"] = p[:NP_TOK, A_Q:A_Q + A_KV].reshape(BATCH, 1, SEQ, A_KV_HEADS, HD)
            outs["a_v"] = p[:NP_TOK, A_Q + A_KV:A_Q + 2 * A_KV].reshape(BATCH, 1, SEQ, A_KV_HEADS, HD)
            outs["r_f"] = _diag_states(sf).reshape(BATCH, 1, B_HEADS, HD, HD)
            outs["r_b"] = _diag_states(sb).reshape(BATCH, 1, B_HEADS, HD, HD)
        else:
            p = _inproj(x, norm1_g[l], mod, odd_w_in[0].astype(BF16), rope, 0)
            o_p = _ctx_mha(p)
            o_s = _na_attention(p, cache_c_k[:, 0], cache_c_v[:, 0], na_rpb[0])
            parts = [(o_p, o_s, odd_w_out[0].astype(BF16))]
            outs["c_k"] = p[:NP_TOK, C_W:2 * C_W].reshape(BATCH, 1, SEQ, C_HEADS, HD)
            outs["c_v"] = p[:NP_TOK, 2 * C_W:3 * C_W].reshape(BATCH, 1, SEQ, C_HEADS, HD)
        x_mid, h, dest, gate_t, plan = _outproj(x, parts, mod, norm2_g[l], router_w[l], router_b[l])
        y = _experts(plan, _sc_scatter(h, dest, N_EXPERTS * N_TOK), exp_w1, exp_w3, exp_w2, l)
        y_rows = _sc_gather(y, dest.reshape(N_ASSIGN)).reshape(TOP_K, N_TOK, D // 2)
        x = _combine(x_mid, h, y_rows, gate_t.T, mod, sh_w1[l], sh_w3[l], sh_w2[l], final_g, final=(l == 1))
    y_prompt = x[0].reshape(BATCH, SEQ, D)
    y_sample = x[1].reshape(DEC_BATCH, DEC_SEQ, D)
    return (y_prompt, y_sample, outs["a_k"], outs["a_v"], outs["r_f"], outs["r_b"], outs["c_k"], outs["c_v"])
```

```python
import functools

import jax
import jax.numpy as jnp
from jax import lax
from jax.experimental import pallas as pl
from jax.experimental.pallas import tpu as pltpu
from jax.experimental.pallas import tpu_sc as plsc

F32 = jnp.float32
BF16 = jnp.bfloat16
HIGHEST = lax.Precision.HIGHEST

D = 1024
BATCH = 32
SEQ = 256
DEC_BATCH = 4
DEC_SEQ = 4096
PAST = 256
GRID_W = 64
HD = 64
EPS = 1e-6
NEG = -1e30
ROPE_BASE = 10000.0
A_HEADS = 8
A_KV_HEADS = 2
A_Q = A_HEADS * HD
A_KV = A_KV_HEADS * HD
B_HEADS = 8
B_W = B_HEADS * HD
EVEN_IN = A_Q + 2 * A_KV + 4 * B_W
C_HEADS = 16
C_W = C_HEADS * HD
NA_KH = 8
NA_KW = 16
N_EXPERTS = 64
TOP_K = 8
N_GROUPS = 8
TOPK_GROUPS = 4
FF = 256
ROUTED_SCALE = 2.5
MOE_BLOCK = 1024
RET_CHUNK = 256
RET_UNROLL = 4
RET_NORM_ROWS = 1024
A_WINDOW = 128

NP_TOK = BATCH * SEQ
NS_TOK = DEC_BATCH * DEC_SEQ
N_TOK = NP_TOK + NS_TOK
N_ASSIGN = N_TOK * TOP_K
N_MOE_BLOCKS = (N_ASSIGN + N_EXPERTS * (MOE_BLOCK - 1) + MOE_BLOCK - 1) // MOE_BLOCK
PLAN_LANES = 512
assert N_TOK % MOE_BLOCK == 0 and N_MOE_BLOCKS <= PLAN_LANES

LANES = 128
TM = 512
NA_ROWS = 8
V7X_VMEM_LIMIT = 56 * 1024 * 1024

NT_DIMS = (((1,), (1,)), ((), ()))


def _cparams(n_axes, vmem=V7X_VMEM_LIMIT):
    return pltpu.CompilerParams(dimension_semantics=("arbitrary",) * n_axes, vmem_limit_bytes=vmem)


def _seg_of_block(i, rows):
    row0 = i * rows
    return jnp.where(row0 < NP_TOK, 0, 1 + (row0 - NP_TOK) // DEC_SEQ)


def _mod_spec(chunk, rows=TM, first_block=0):
    return pl.BlockSpec((None, 1, D), lambda i: (_seg_of_block(i + first_block, rows), 0, chunk))


def _pair_specs(width, rows=TM):
    npb = NP_TOK // rows
    nsb = NS_TOK // rows
    return [pl.BlockSpec((rows, width), lambda i: (jnp.minimum(i, npb - 1), 0)),
            pl.BlockSpec((rows, width), lambda i: (jnp.clip(i - npb, 0, nsb - 1), 0))]


def _pick_rows(p_ref, s_ref, rows=TM):
    return jnp.where(pl.program_id(0) < NP_TOK // rows, p_ref[...], s_ref[...])


def _silu(x):
    return x * jax.nn.sigmoid(x)


def _rms(x, g):
    return x * lax.rsqrt(jnp.mean(x * x, axis=-1, keepdims=True) + EPS) * g


def _lane_lo():
    return lax.broadcasted_iota(jnp.int32, (1, LANES), 1) < HD


def _ada_kernel(c_ref, w_ref, b_ref, o_ref):
    a = _silu(c_ref[...])
    o_ref[...] = jnp.dot(a, w_ref[...], preferred_element_type=F32, precision=HIGHEST) + b_ref[...]


def _ada(cc, w, b, layer):
    tn = 1536
    out = pl.pallas_call(
        _ada_kernel,
        grid=(6 * D // tn,),
        in_specs=[pl.BlockSpec((8, D), lambda j: (0, 0)),
                  pl.BlockSpec((None, D, tn), lambda j: (layer, 0, j)),
                  pl.BlockSpec((None, 1, tn), lambda j: (layer, 0, j))],
        out_specs=pl.BlockSpec((8, tn), lambda j: (0, j)),
        out_shape=jax.ShapeDtypeStruct((8, 6 * D), F32),
        compiler_params=_cparams(1),
        name="ada",
    )(cc, w, b.reshape(b.shape[0], 1, 6 * D))
    return out.reshape(8, 1, 6 * D)


def _inproj_kernel(xp_ref, xs_ref, g_ref, shift_ref, scale_ref, w_ref, cos_ref, sin_ref, o_ref, *, rope_cols):
    h = _rms(_pick_rows(xp_ref, xs_ref), g_ref[...]) * (1.0 + scale_ref[...]) + shift_ref[...]
    o = jnp.dot(h.astype(BF16), w_ref[...], preferred_element_type=F32)
    if rope_cols:
        cos = cos_ref[...]
        sin = sin_ref[...]
        lane = lax.broadcasted_iota(jnp.int32, (1, LANES), 1)
        first = (lane % 32) < 16
        for c in range(rope_cols // LANES):
            oc = o[:, c * LANES:(c + 1) * LANES]
            partner = jnp.where(first, pltpu.roll(oc, LANES - 16, 1), pltpu.roll(oc, 16, 1))
            o_ref[:, c * LANES:(c + 1) * LANES] = oc * cos + partner * sin
        o_ref[:, rope_cols:] = o[:, rope_cols:]
    else:
        o_ref[...] = o


def _rope_tables():
    half = HD // 2
    inv = ROPE_BASE ** (-jnp.arange(0, half, 2, dtype=F32) / half)
    t = jnp.arange(DEC_SEQ)
    ang_r = (t // GRID_W).astype(F32)[:, None] * inv[None]
    ang_c = (t % GRID_W).astype(F32)[:, None] * inv[None]

    def head(fn_r, fn_c, sign):
        return jnp.concatenate([sign[0] * fn_r, sign[1] * fn_r, sign[0] * fn_c, sign[1] * fn_c], axis=-1)

    cos = head(jnp.cos(ang_r), jnp.cos(ang_c), (1.0, 1.0))
    sin = head(jnp.sin(ang_r), jnp.sin(ang_c), (-1.0, 1.0))
    cos = jnp.concatenate([jnp.ones((TM, HD), F32), cos], axis=0)
    sin = jnp.concatenate([jnp.zeros((TM, HD), F32), sin], axis=0)
    return jnp.tile(cos, (1, 2)), jnp.tile(sin, (1, 2))


def _inproj(x, g, mod, w_bf16, rope, rope_cols):
    n_out = w_bf16.shape[1]
    npb = NP_TOK // TM
    spb = DEC_SEQ // TM

    def rope_map(i):
        return (jnp.where(i < npb, 0, 1 + (i - npb) % spb), 0)

    return pl.pallas_call(
        functools.partial(_inproj_kernel, rope_cols=rope_cols),
        grid=(N_TOK // TM,),
        in_specs=_pair_specs(D) + [
                  pl.BlockSpec((1, D), lambda i: (0, 0)),
                  _mod_spec(0), _mod_spec(1),
                  pl.BlockSpec((D, n_out), lambda i: (0, 0)),
                  pl.BlockSpec((TM, LANES), rope_map),
                  pl.BlockSpec((TM, LANES), rope_map)],
        out_specs=pl.BlockSpec((TM, n_out), lambda i: (i, 0)),
        out_shape=jax.ShapeDtypeStruct((N_TOK, n_out), F32),
        compiler_params=_cparams(1),
        name="inproj",
    )(x[0], x[1], g.reshape(1, D), mod, mod, w_bf16, rope[0], rope[1])


def _softmax_av(s_list, v_list, sink=None):
    mx = s_list[0].max(axis=-1, keepdims=True)
    for s in s_list[1:]:
        mx = jnp.maximum(mx, s.max(axis=-1, keepdims=True))
    if sink is not None:
        mx = jnp.maximum(mx, sink)
    den = jnp.exp(sink - mx) if sink is not None else 0.0
    acc = None
    for s, v in zip(s_list, v_list):
        p = jnp.exp(s - mx)
        den = den + p.sum(axis=-1, keepdims=True)
        pv = jnp.dot(p.astype(BF16), v, preferred_element_type=F32)
        acc = pv if acc is None else acc + pv
    return acc / den


def _dup_half(x, j, lo):
    xr = pltpu.roll(x, HD, 1)
    return jnp.where(lo, x, xr) if j == 0 else jnp.where(lo, xr, x)


def _stack_heads(q_ref, heads, lo, scale):
    parts = []
    for h in heads:
        qp = q_ref[:, (h // 2) * LANES:(h // 2 + 1) * LANES]
        keep = lo if h % 2 == 0 else jnp.logical_not(lo)
        parts.append(jnp.where(keep, qp, 0.0) * scale)
    return jnp.concatenate(parts, axis=0).astype(BF16)


def _sink_column(sink_ref, heads, rows):
    return jnp.concatenate([jnp.full((rows, 1), sink_ref[h], F32) for h in heads], axis=0)


def _ctx_gqa_kernel(sink_ref, q_ref, k_ref, v_ref, o_ref):
    lo = _lane_lo()
    k = k_ref[...]
    v = v_ref[...]
    group = A_HEADS // A_KV_HEADS
    scores = []
    for j in range(A_KV_HEADS):
        q = _stack_heads(q_ref, list(range(group * j, group * (j + 1))), lo, HD ** -0.5)
        scores.append(lax.dot_general(q, _dup_half(k, j, lo).astype(BF16), NT_DIMS, preferred_element_type=F32))
    s = jnp.concatenate(scores, axis=0)
    sink = _sink_column(sink_ref, list(range(A_HEADS)), SEQ)
    mx = jnp.maximum(s.max(axis=-1, keepdims=True), sink)
    e = jnp.exp(s - mx)
    den = jnp.exp(sink - mx) + e.sum(axis=-1, keepdims=True)
    e = e.astype(BF16)
    rows_per_group = group * SEQ
    for j in range(A_KV_HEADS):
        rows = slice(j * rows_per_group, (j + 1) * rows_per_group)
        o = jnp.dot(e[rows], _dup_half(v, j, lo).astype(BF16), preferred_element_type=F32) / den[rows]
        for t in range(group // 2):
            pair = (group * j) // 2 + t
            o_ref[:, pair * LANES:(pair + 1) * LANES] = jnp.where(
                lo, o[(2 * t) * SEQ:(2 * t + 1) * SEQ], o[(2 * t + 1) * SEQ:(2 * t + 2) * SEQ])


def _ctx_gqa(p, sink):
    return pl.pallas_call(
        _ctx_gqa_kernel,
        grid_spec=pltpu.PrefetchScalarGridSpec(
            num_scalar_prefetch=1,
            grid=(BATCH,),
            in_specs=[pl.BlockSpec((SEQ, A_Q), lambda b, s: (b, 0)),
                      pl.BlockSpec((SEQ, A_KV), lambda b, s: (b, A_Q // A_KV)),
                      pl.BlockSpec((SEQ, A_KV), lambda b, s: (b, A_Q // A_KV + 1))],
            out_specs=pl.BlockSpec((SEQ, A_Q), lambda b, s: (b, 0))),
        out_shape=jax.ShapeDtypeStruct((NP_TOK, A_Q), F32),
        compiler_params=_cparams(1),
        name="ctx_gqa",
    )(sink, p, p, p)


def _win_kernel(sink_ref, q_ref, kp_ref, kc_ref, kn_ref, vp_ref, vc_ref, vn_ref, ck_ref, cv_ref, o_ref):
    i = pl.program_id(1)
    lo = _lane_lo()
    k = jnp.concatenate([kp_ref[...], kc_ref[...], kn_ref[...]], axis=0)
    v = jnp.concatenate([vp_ref[...], vc_ref[...], vn_ref[...]], axis=0)
    ck = ck_ref[...]
    cv = cv_ref[...]
    group = A_HEADS // A_KV_HEADS
    n_keys = WIN_Q + 2 * A_WINDOW
    qpos = i * WIN_Q + lax.broadcasted_iota(jnp.int32, (WIN_Q, n_keys), 0)
    kpos = i * WIN_Q - A_WINDOW + lax.broadcasted_iota(jnp.int32, (WIN_Q, n_keys), 1)
    valid = (jnp.abs(kpos - qpos) <= A_WINDOW) & (kpos >= 0) & (kpos < DEC_SEQ)
    valid = jnp.concatenate([valid] * group, axis=0)
    s_loc, s_ctx, values = [], [], []
    for j in range(A_KV_HEADS):
        heads = list(range(group * j, group * (j + 1)))
        kd = _dup_half(k, j, lo).astype(BF16)
        ckd = _dup_half(ck, j, lo).astype(BF16)
        values.append((_dup_half(v, j, lo).astype(BF16), _dup_half(cv, j, lo).astype(BF16)))
        q = _stack_heads(q_ref, heads, lo, HD ** -0.5)
        s_loc.append(jnp.where(valid, lax.dot_general(q, kd, NT_DIMS, preferred_element_type=F32), NEG))
        s_ctx.append(lax.dot_general(q, ckd, NT_DIMS, preferred_element_type=F32))
    s_loc = jnp.concatenate(s_loc, axis=0)
    s_ctx = jnp.concatenate(s_ctx, axis=0)
    sink = _sink_column(sink_ref, list(range(A_HEADS)), WIN_Q)
    mx = jnp.maximum(jnp.maximum(s_loc.max(axis=-1, keepdims=True), s_ctx.max(axis=-1, keepdims=True)), sink)
    p_loc = jnp.exp(s_loc - mx)
    p_ctx = jnp.exp(s_ctx - mx)
    den = p_loc.sum(axis=-1, keepdims=True) + p_ctx.sum(axis=-1, keepdims=True) + jnp.exp(sink - mx)
    p_loc = p_loc.astype(BF16)
    p_ctx = p_ctx.astype(BF16)
    rows_per_group = group * WIN_Q
    for j, (vd, cvd) in enumerate(values):
        rows = slice(j * rows_per_group, (j + 1) * rows_per_group)
        o = (jnp.dot(p_loc[rows], vd, preferred_element_type=F32)
             + jnp.dot(p_ctx[rows], cvd, preferred_element_type=F32)) / den[rows]
        for t in range(group // 2):
            pair = (group * j) // 2 + t
            o_ref[:, pair * LANES:(pair + 1) * LANES] = jnp.where(
                lo, o[(2 * t) * WIN_Q:(2 * t + 1) * WIN_Q], o[(2 * t + 1) * WIN_Q:(2 * t + 2) * WIN_Q])


WIN_Q = 256


def _win_attention(p, cache_k, cache_v, sink):
    nblk = DEC_SEQ // WIN_Q
    side = WIN_Q // A_WINDOW
    nside = DEC_SEQ // A_WINDOW
    base = NP_TOK // WIN_Q
    side_base = NP_TOK // A_WINDOW
    kcol = A_Q // A_KV

    def main_spec(col):
        return pl.BlockSpec((WIN_Q, A_KV), lambda b, i, s: (base + b * nblk + i, col))

    def side_spec(col, off):
        return pl.BlockSpec((A_WINDOW, A_KV),
                            lambda b, i, s: (side_base + b * nside + jnp.clip(side * i + off, 0, nside - 1), col))

    ctx_spec = pl.BlockSpec((None, PAST, A_KV), lambda b, i, s: (b, 0, 0))
    return pl.pallas_call(
        _win_kernel,
        grid_spec=pltpu.PrefetchScalarGridSpec(
            num_scalar_prefetch=1,
            grid=(DEC_BATCH, nblk),
            in_specs=[pl.BlockSpec((WIN_Q, A_Q), lambda b, i, s: (base + b * nblk + i, 0)),
                      side_spec(kcol, -1), main_spec(kcol), side_spec(kcol, side),
                      side_spec(kcol + 1, -1), main_spec(kcol + 1), side_spec(kcol + 1, side),
                      ctx_spec, ctx_spec],
            out_specs=pl.BlockSpec((WIN_Q, A_Q), lambda b, i, s: (b * nblk + i, 0))),
        out_shape=jax.ShapeDtypeStruct((NS_TOK, A_Q), F32),
        compiler_params=_cparams(2),
        name="win_attn",
    )(sink, p, p, p, p, p, p, p, cache_k.reshape(DEC_BATCH, PAST, A_KV), cache_v.reshape(DEC_BATCH, PAST, A_KV))


def _ret_kernel(df_ref, db_ref, q_ref, k_ref, v_ref, g_ref, gn_ref, s0f_ref, s0b_ref,
                o_ref, sf_ref, sb_ref, of_scr, ob_scr, *, length):
    c_len = RET_CHUNK
    n = length // c_len
    lo = _lane_lo()
    hi = jnp.logical_not(lo)
    row = lax.broadcasted_iota(jnp.int32, (c_len, c_len), 0)
    col = lax.broadcasted_iota(jnp.int32, (c_len, c_len), 1)
    rowp = lax.broadcasted_iota(jnp.int32, (LANES, LANES), 0)
    colp = lax.broadcasted_iota(jnp.int32, (LANES, LANES), 1)
    blockdiag = (rowp < HD) == (colp < HD)
    idx = lax.broadcasted_iota(jnp.int32, (c_len, 1), 0).astype(F32)

    def direction(dec_ref, forward):
        lg = -jnp.exp(dec_ref[...])
        diff = (row - col) if forward else (col - row)
        keep = (diff >= 0) if forward else (diff > 0)
        dist = jnp.maximum(diff, 0).astype(F32)
        dm = jnp.concatenate([jnp.where(keep, jnp.exp(dist * lg[:, off:off + 1]), 0.0) for off in (0, HD)], axis=0)
        if forward:
            xi = jnp.exp((idx + 1.0) * lg)
            zeta = jnp.exp((c_len - 1.0 - idx) * lg)
        else:
            xi = jnp.exp((c_len - idx) * lg)
            zeta = jnp.exp(idx * lg)
        return dm, xi, zeta, jnp.exp(c_len * lg)

    def chunk(c, state, consts):
        dm, xi, zeta, gch = consts
        rows = pl.ds(pl.multiple_of(c * c_len, c_len), c_len)
        qc = q_ref[rows, :]
        kc = k_ref[rows, :] * HD ** -0.5
        vc = v_ref[rows, :].astype(BF16)
        kb = kc.astype(BF16)
        q2 = jnp.concatenate([jnp.where(lo, qc, 0.0), jnp.where(hi, qc, 0.0)], axis=0).astype(BF16)
        inner = lax.dot_general(q2, kb, NT_DIMS, preferred_element_type=F32) * dm
        kz_t = (kc * zeta).T
        res = jnp.dot(jnp.concatenate([inner, kz_t], axis=0).astype(BF16), vc, preferred_element_type=F32)
        cross = jnp.dot(qc.astype(BF16), state.astype(BF16), preferred_element_type=F32) * xi
        o = jnp.where(lo, res[:c_len], res[c_len:2 * c_len]) + cross
        state = gch * state + jnp.where(blockdiag, res[2 * c_len:], 0.0)
        return rows, o, state

    cf = direction(df_ref, True)
    cb = direction(db_ref, False)

    def scan_body(t, states):
        rows_f, o_f, state_f = chunk(t, states[0], cf)
        of_scr[rows_f, :] = o_f
        rows_b, o_b, state_b = chunk(n - 1 - t, states[1], cb)
        ob_scr[rows_b, :] = o_b
        return state_f, state_b

    state_f, state_b = lax.fori_loop(0, n, scan_body, (s0f_ref[...], s0b_ref[...]), unroll=min(n, RET_UNROLL))
    sf_ref[...] = state_f
    sb_ref[...] = state_b

    gn = gn_ref[...]
    norm_rows = min(RET_NORM_ROWS, length)
    n_norm = length // norm_rows

    def per_head(x):
        a = jnp.where(lo, x, 0.0).sum(axis=-1, keepdims=True)
        b = jnp.where(hi, x, 0.0).sum(axis=-1, keepdims=True)
        return jnp.where(lo, a, b) * (1.0 / HD)

    def norm_body(t, carry):
        rows = pl.ds(pl.multiple_of(t * norm_rows, norm_rows), norm_rows)
        o = of_scr[rows, :] + ob_scr[rows, :]
        d = o - per_head(o)
        y = d * lax.rsqrt(per_head(d * d) + EPS) * gn
        o_ref[rows, :] = _silu(g_ref[rows, :]) * y
        return carry

    lax.fori_loop(0, n_norm, norm_body, 0)


def _pair_lanes(v):
    return jnp.repeat(v.astype(F32), HD).reshape(B_HEADS // 2, 1, LANES)


def _blockdiag_states(s):
    b = s.shape[0]
    s = s.astype(F32).reshape(b, B_HEADS // 2, 2, HD, HD)
    z = jnp.zeros_like(s[:, :, 0])
    top = jnp.concatenate([s[:, :, 0], z], axis=-1)
    bot = jnp.concatenate([z, s[:, :, 1]], axis=-1)
    return jnp.concatenate([top, bot], axis=-2)


def _diag_states(sp):
    b = sp.shape[0]
    s = jnp.stack([sp[:, :, :HD, :HD], sp[:, :, HD:, HD:]], axis=2)
    return s.reshape(b, B_HEADS, HD, HD)


def _retention(p, row_base, batch, length, dec_f, dec_b, gn_g, s0f, s0b):
    npairs = B_HEADS // 2
    blk0 = row_base // length
    qcol = (A_Q + 2 * A_KV) // LANES

    def col_spec(off):
        return pl.BlockSpec((length, LANES), lambda b, h: (blk0 + b, qcol + off * npairs + h))

    lane_spec = pl.BlockSpec((None, 1, LANES), lambda b, h: (h, 0, 0))
    state_spec = pl.BlockSpec((None, None, LANES, LANES), lambda b, h: (b, h, 0, 0))
    state_shape = jax.ShapeDtypeStruct((batch, npairs, LANES, LANES), F32)
    return pl.pallas_call(
        functools.partial(_ret_kernel, length=length),
        grid=(batch, npairs),
        in_specs=[lane_spec, lane_spec, col_spec(0), col_spec(1), col_spec(2), col_spec(3), lane_spec,
                  state_spec, state_spec],
        out_specs=[pl.BlockSpec((length, LANES), lambda b, h: (b, h)), state_spec, state_spec],
        out_shape=[jax.ShapeDtypeStruct((batch * length, B_W), F32), state_shape, state_shape],
        scratch_shapes=[pltpu.VMEM((length, LANES), F32), pltpu.VMEM((length, LANES), F32)],
        compiler_params=_cparams(2),
        name="retention",
    )(_pair_lanes(dec_f), _pair_lanes(dec_b), p, p, p, p, gn_g.reshape(npairs, 1, LANES), s0f, s0b)


def _ctx_mha_kernel(q_ref, k_ref, v_ref, o_ref):
    lo = _lane_lo()
    for pair in range(C_HEADS // 2):
        cols = slice(pair * LANES, (pair + 1) * LANES)
        q = _stack_heads(q_ref, [2 * pair, 2 * pair + 1], lo, HD ** -0.5)
        s = lax.dot_general(q, k_ref[:, cols].astype(BF16), NT_DIMS, preferred_element_type=F32)
        o = _softmax_av([s], [v_ref[:, cols].astype(BF16)])
        o_ref[:, cols] = jnp.where(lo, o[:SEQ], o[SEQ:])


def _ctx_mha(p):
    return pl.pallas_call(
        _ctx_mha_kernel,
        grid=(BATCH,),
        in_specs=[pl.BlockSpec((SEQ, C_W), lambda b: (b, 0)),
                  pl.BlockSpec((SEQ, C_W), lambda b: (b, 1)),
                  pl.BlockSpec((SEQ, C_W), lambda b: (b, 2))],
        out_specs=pl.BlockSpec((SEQ, C_W), lambda b: (b, 0)),
        out_shape=jax.ShapeDtypeStruct((NP_TOK, C_W), F32),
        compiler_params=_cparams(1),
        name="ctx_mha",
    )(p, p, p)


NA_WIN_ROWS = 2 * NA_ROWS
NA_WIN = NA_WIN_ROWS * GRID_W
NA_QROWS = NA_ROWS * GRID_W
NA_PAD_ROWS = NA_KH // 2
NA_TABLE = 1536


def _na_kernel(q_ref, kp_ref, km_ref, kn_ref, vp_ref, vm_ref, vn_ref, ck_ref, cv_ref, ue_ref, uo_ref, o_ref):
    r0 = pl.program_id(2) * NA_ROWS
    n_rows = DEC_SEQ // GRID_W
    lo = _lane_lo()
    k = jnp.concatenate([kp_ref[...], km_ref[...], kn_ref[...]], axis=0).astype(BF16)
    v = jnp.concatenate([vp_ref[...], vm_ref[...], vn_ref[...]], axis=0).astype(BF16)
    ck = ck_ref[...].astype(BF16)
    cv = cv_ref[...].astype(BF16)
    q = q_ref[...] * HD ** -0.5
    klane = lax.broadcasted_iota(jnp.int32, (1, NA_WIN), 1)
    outs = []
    for half, keep in enumerate((lo, jnp.logical_not(lo))):
        qh = jnp.where(keep, q, 0.0).astype(BF16)
        s = lax.dot_general(qh, k, NT_DIMS, preferred_element_type=F32)
        s_ctx = lax.dot_general(qh, ck, NT_DIMS, preferred_element_type=F32)
        p_loc, p_ctx, den = [], [], []
        for rq in range(NA_ROWS):
            rows = slice(rq * GRID_W, (rq + 1) * GRID_W)
            start = NA_KH - 1 - rq
            if start % 2 == 0:
                u = ue_ref[half, :, start * GRID_W:start * GRID_W + NA_WIN]
            else:
                u = uo_ref[half, :, (start - 1) * GRID_W:(start - 1) * GRID_W + NA_WIN]
            r = r0 + rq
            first = jnp.clip(r - NA_KH // 2, 0, n_rows - NA_KH)
            lane0 = (first - r0 + NA_PAD_ROWS) * GRID_W
            in_rows = (klane >= lane0) & (klane < lane0 + NA_KH * GRID_W)
            sl = jnp.where(in_rows, s[rows] + u, NEG)
            sc = s_ctx[rows]
            mx = jnp.maximum(sl.max(axis=-1, keepdims=True), sc.max(axis=-1, keepdims=True))
            el = jnp.exp(sl - mx)
            ec = jnp.exp(sc - mx)
            den.append(el.sum(axis=-1, keepdims=True) + ec.sum(axis=-1, keepdims=True))
            p_loc.append(el.astype(BF16))
            p_ctx.append(ec.astype(BF16))
        acc = (jnp.dot(jnp.concatenate(p_loc, axis=0), v, preferred_element_type=F32)
               + jnp.dot(jnp.concatenate(p_ctx, axis=0), cv, preferred_element_type=F32))
        outs.append(acc / jnp.concatenate(den, axis=0))
    o_ref[...] = jnp.where(lo, outs[0], outs[1])


def _na_bias_tables(rpb):
    cq = jnp.arange(GRID_W)
    ck = jnp.arange(GRID_W)
    dc = jnp.clip(ck[None] - cq[:, None], -(NA_KW - 1), NA_KW - 1) + NA_KW - 1
    cs = jnp.clip(cq - NA_KW // 2, 0, GRID_W - NA_KW)
    col_ok = (ck[None] >= cs[:, None]) & (ck[None] < cs[:, None] + NA_KW)
    t = rpb.astype(F32)[:, :, dc]
    t = jnp.where(col_ok[None, None], t, NEG).transpose(0, 2, 1, 3)
    n_dr = 2 * NA_KH - 1
    blocks = NA_TABLE // GRID_W
    t = jnp.pad(t, ((0, 0), (0, 0), (NA_PAD_ROWS, blocks - n_dr - NA_PAD_ROWS), (0, 0)), constant_values=NEG)
    ue = t.reshape(C_HEADS, GRID_W, NA_TABLE)
    uo = jnp.concatenate([ue[..., GRID_W:], jnp.full((C_HEADS, GRID_W, GRID_W), NEG, F32)], axis=-1)
    return ue, uo


def _na_attention(p, cache_k, cache_v, rpb):
    npairs = C_HEADS // 2
    nrb = DEC_SEQ // NA_QROWS
    half = NA_QROWS // 2
    qbase = NP_TOK // NA_QROWS
    hbase = NP_TOK // half
    kcol = C_W // LANES
    ue, uo = _na_bias_tables(rpb)

    def main_spec(col0):
        return pl.BlockSpec((NA_QROWS, LANES), lambda b, h, r: (qbase + b * nrb + r, col0 + h))

    def side_spec(col0, off):
        return pl.BlockSpec((half, LANES),
                            lambda b, h, r: (hbase + b * 2 * nrb + jnp.clip(2 * r + off, 0, 2 * nrb - 1), col0 + h))

    ctx_spec = pl.BlockSpec((None, PAST, LANES), lambda b, h, r: (b, 0, h))
    tab_spec = pl.BlockSpec((2, GRID_W, NA_TABLE), lambda b, h, r: (h, 0, 0))
    return pl.pallas_call(
        _na_kernel,
        grid=(DEC_BATCH, npairs, nrb),
        in_specs=[main_spec(0),
                  side_spec(kcol, -1), main_spec(kcol), side_spec(kcol, 2),
                  side_spec(2 * kcol, -1), main_spec(2 * kcol), side_spec(2 * kcol, 2),
                  ctx_spec, ctx_spec, tab_spec, tab_spec],
        out_specs=pl.BlockSpec((NA_QROWS, LANES), lambda b, h, r: (b * nrb + r, h)),
        out_shape=jax.ShapeDtypeStruct((NS_TOK, C_W), F32),
        compiler_params=_cparams(3),
        name="na_attn",
    )(p, p, p, p, p, p, p, cache_k.reshape(DEC_BATCH, PAST, C_W), cache_v.reshape(DEC_BATCH, PAST, C_W), ue, uo)


def _route(biased, scores):
    t = biased.shape[1]
    per_group = N_EXPERTS // N_GROUPS
    i8 = lax.broadcasted_iota(jnp.int32, (per_group, t), 0)
    g_rows = []
    for g in range(N_GROUPS):
        bg = biased[g * per_group:(g + 1) * per_group]
        m1 = bg.max(axis=0, keepdims=True)
        first = jnp.where(bg == m1, i8, per_group).min(axis=0, keepdims=True)
        m2 = jnp.where(i8 == first, -jnp.inf, bg).max(axis=0, keepdims=True)
        g_rows.append(m1 + m2)
    g_top = jnp.concatenate(g_rows, axis=0)
    gi = lax.broadcasted_iota(jnp.int32, g_top.shape, 0)
    g_sel = jnp.zeros(g_top.shape, jnp.int32)
    cur = g_top
    for _ in range(TOPK_GROUPS):
        m = cur.max(axis=0, keepdims=True)
        hit = gi == jnp.where(cur == m, gi, N_GROUPS).min(axis=0, keepdims=True)
        g_sel = jnp.where(hit, 1, g_sel)
        cur = jnp.where(hit, -jnp.inf, cur)
    e_sel = jnp.concatenate([jnp.broadcast_to(g_sel[g:g + 1], (per_group, t)) for g in range(N_GROUPS)], axis=0)
    cur = jnp.where(e_sel > 0, biased, NEG)
    ei = lax.broadcasted_iota(jnp.int32, cur.shape, 0)
    ids, gates, hits = [], [], []
    for _ in range(TOP_K):
        m = cur.max(axis=0, keepdims=True)
        f = jnp.where(cur == m, ei, N_EXPERTS).min(axis=0, keepdims=True)
        hit = ei == f
        ids.append(f)
        hits.append(hit)
        gates.append(jnp.where(hit, scores, 0.0).sum(axis=0, keepdims=True))
        cur = jnp.where(hit, -jnp.inf, cur)
    gate = jnp.concatenate(gates, axis=0)
    gate = gate / gate.sum(axis=0, keepdims=True) * ROUTED_SCALE
    return jnp.concatenate(ids, axis=0), gate, hits


def _pack_bf16_pairs(h):
    bits = lax.bitcast_convert_type(h.astype(BF16).astype(F32), jnp.uint32)
    return bits[:, :D // 2] | (bits[:, D // 2:] >> 16)


def _unpack_bf16_pairs(xp):
    hi = lax.bitcast_convert_type(xp & jnp.uint32(0xFFFF0000), F32).astype(BF16)
    lo = lax.bitcast_convert_type(xp << 16, F32).astype(BF16)
    return hi, lo


def _dot_halves(hi, lo, w_ref):
    return (jnp.dot(hi, w_ref[:D // 2, :], preferred_element_type=F32)
            + jnp.dot(lo, w_ref[D // 2:, :], preferred_element_type=F32))


def _outproj_kernel(*refs, n_parts):
    xp_ref, xs_ref = refs[:2]
    part_refs = refs[2:2 + 3 * n_parts]
    gate_ref, shift_ref, scale_ref, g2_ref, rw_ref, rb_ref = refs[2 + 3 * n_parts:8 + 3 * n_parts]
    xo_ref, h_ref, dest_ref, wgt_ref, plan_ref, cnt_ref = refs[8 + 3 * n_parts:]
    step = pl.program_id(0)

    @pl.when(step == 0)
    def _():
        cnt_ref[...] = jnp.zeros_like(cnt_ref)
        plan_ref[...] = jnp.zeros_like(plan_ref)

    y = None
    for t in range(n_parts):
        ap_ref, as_ref, w_ref = part_refs[3 * t:3 * t + 3]
        d = jnp.dot(_pick_rows(ap_ref, as_ref).astype(BF16), w_ref[...], preferred_element_type=F32)
        y = d if y is None else y + d
    x = _pick_rows(xp_ref, xs_ref) + gate_ref[...] * y
    xo_ref[...] = x
    h = _rms(x, g2_ref[...]) * (1.0 + scale_ref[...]) + shift_ref[...]
    h_ref[...] = _pack_bf16_pairs(h)
    h_hi = h.astype(BF16)
    h_lo = (h - h_hi.astype(F32)).astype(BF16)
    rw = rw_ref[...]
    rw_hi = rw.astype(BF16)
    rw_lo = (rw - rw_hi.astype(F32)).astype(BF16)
    logits = (lax.dot_general(rw_hi, h_hi, NT_DIMS, preferred_element_type=F32)
              + lax.dot_general(rw_hi, h_lo, NT_DIMS, preferred_element_type=F32)
              + lax.dot_general(rw_lo, h_hi, NT_DIMS, preferred_element_type=F32))
    scores = jax.nn.sigmoid(logits)
    _, gate, hits = _route(scores + rb_ref[...], scores)
    wgt_ref[...] = gate
    chosen = hits[0]
    for hit in hits[1:]:
        chosen = chosen | hit
    m = jnp.where(chosen, 1.0, 0.0)
    before = (lax.broadcasted_iota(jnp.int32, (TM, TM), 0) < lax.broadcasted_iota(jnp.int32, (TM, TM), 1))
    prefix = jnp.dot(m.astype(BF16), jnp.where(before, 1.0, 0.0).astype(BF16), preferred_element_type=F32)
    e_base = (lax.broadcasted_iota(jnp.int32, (N_EXPERTS, 1), 0) * N_TOK).astype(F32)
    row_all = prefix + (cnt_ref[...] + e_base)
    dest_ref[...] = jnp.concatenate(
        [jnp.where(hit, row_all, 0.0).sum(axis=0, keepdims=True) for hit in hits], axis=0).astype(jnp.int32)
    cnt_ref[...] += m.sum(axis=1, keepdims=True)

    @pl.when(step == pl.num_programs(0) - 1)
    def _():
        _block_plan(cnt_ref[...], plan_ref)


def _block_plan(counts, plan_ref):
    cap_blocks = N_TOK // MOE_BLOCK
    nblk = ((counts.astype(jnp.int32) + (MOE_BLOCK - 1)) // MOE_BLOCK).astype(F32)
    lower = (lax.broadcasted_iota(jnp.int32, (N_EXPERTS, N_EXPERTS), 0)
             >= lax.broadcasted_iota(jnp.int32, (N_EXPERTS, N_EXPERTS), 1))
    cum = jnp.dot(jnp.where(lower, 1.0, 0.0).astype(BF16), jnp.broadcast_to(nblk, (N_EXPERTS, LANES)).astype(BF16),
                  preferred_element_type=F32)[:, :1]
    n_used = cum[N_EXPERTS - 1:, :]
    slot = jnp.minimum(lax.broadcasted_iota(jnp.int32, (1, PLAN_LANES), 1).astype(F32), n_used - 1.0)
    done = cum <= slot
    expert = jnp.where(done, 1.0, 0.0).sum(axis=0, keepdims=True)
    blocks_before = jnp.where(done, nblk, 0.0).sum(axis=0, keepdims=True)
    plan_ref[0:1, :] = (expert * cap_blocks + (slot - blocks_before)).astype(jnp.int32)
    plan_ref[1:2, :] = expert.astype(jnp.int32)
    plan_ref[2:3, :] = jnp.broadcast_to(n_used, (1, PLAN_LANES)).astype(jnp.int32)


def _outproj(x, parts, mod, g2, router_w, router_b):
    in_specs = _pair_specs(D)
    args = [x[0], x[1]]
    for ap, a_s, w in parts:
        width = ap.shape[1]
        in_specs += _pair_specs(width) + [pl.BlockSpec((width, D), lambda i: (0, 0))]
        args += [ap, a_s, w]
    in_specs += [_mod_spec(2), _mod_spec(3), _mod_spec(4),
                 pl.BlockSpec((1, D), lambda i: (0, 0)),
                 pl.BlockSpec((N_EXPERTS, D), lambda i: (0, 0)),
                 pl.BlockSpec((N_EXPERTS, 1), lambda i: (0, 0))]
    args += [mod, mod, mod, g2.reshape(1, D), router_w.T, router_b.reshape(N_EXPERTS, 1)]
    return pl.pallas_call(
        functools.partial(_outproj_kernel, n_parts=len(parts)),
        grid=(N_TOK // TM,),
        in_specs=in_specs,
        out_specs=[pl.BlockSpec((TM, D), lambda i: (i, 0)),
                   pl.BlockSpec((TM, D // 2), lambda i: (i, 0)),
                   pl.BlockSpec((TOP_K, TM), lambda i: (0, i)),
                   pl.BlockSpec((TOP_K, TM), lambda i: (0, i)),
                   pl.BlockSpec((8, PLAN_LANES), lambda i: (0, 0))],
        out_shape=[jax.ShapeDtypeStruct((N_TOK, D), F32),
                   jax.ShapeDtypeStruct((N_TOK, D // 2), jnp.uint32),
                   jax.ShapeDtypeStruct((TOP_K, N_TOK), jnp.int32),
                   jax.ShapeDtypeStruct((TOP_K, N_TOK), F32),
                   jax.ShapeDtypeStruct((8, PLAN_LANES), jnp.int32)],
        scratch_shapes=[pltpu.VMEM((N_EXPERTS, 1), F32)],
        compiler_params=_cparams(1),
        name="outproj_router",
    )(*args)


def _experts_kernel(br_ref, be_ref, nu_ref, x_ref, w1_ref, w3_ref, w2_ref, o_ref, w1b, w3b, w2b):
    i = pl.program_id(0)
    e = be_ref[i]
    prev = be_ref[jnp.maximum(i - 1, 0)]

    @pl.when((i == 0) | (e != prev))
    def _():
        w1b[...] = w1_ref[...].astype(BF16)
        w3b[...] = w3_ref[...].astype(BF16)
        w2b[...] = w2_ref[...].astype(BF16)

    @pl.when(i < nu_ref[0])
    def _():
        hi, lo = _unpack_bf16_pairs(x_ref[...])
        a = _dot_halves(hi, lo, w1b)
        b = _dot_halves(hi, lo, w3b)
        h = (_silu(a) * b).astype(BF16)
        o_ref[...] = _pack_bf16_pairs(jnp.dot(h, w2b[...], preferred_element_type=F32))


def _experts(plan, x_rows, w1, w3, w2, layer):
    return pl.pallas_call(
        _experts_kernel,
        grid_spec=pltpu.PrefetchScalarGridSpec(
            num_scalar_prefetch=3,
            grid=(N_MOE_BLOCKS,),
            in_specs=[pl.BlockSpec((MOE_BLOCK, D // 2), lambda i, br, be, nu: (br[i], 0)),
                      pl.BlockSpec((None, None, D, FF), lambda i, br, be, nu: (layer, be[i], 0, 0)),
                      pl.BlockSpec((None, None, D, FF), lambda i, br, be, nu: (layer, be[i], 0, 0)),
                      pl.BlockSpec((None, None, FF, D), lambda i, br, be, nu: (layer, be[i], 0, 0))],
            out_specs=pl.BlockSpec((MOE_BLOCK, D // 2), lambda i, br, be, nu: (br[i], 0)),
            scratch_shapes=[pltpu.VMEM((D, FF), BF16), pltpu.VMEM((D, FF), BF16), pltpu.VMEM((FF, D), BF16)]),
        out_shape=jax.ShapeDtypeStruct(x_rows.shape, jnp.uint32),
        compiler_params=_cparams(1),
        name="experts",
    )(plan[0], plan[1], plan[2, :1], x_rows, w1, w3, w2)


SC_CORES = 2
SC_SUBCORES = 16
SC_WORKERS = SC_CORES * SC_SUBCORES
SC_CHUNK_BYTES = 64 * 1024
SC_GATHER_CHUNK_BYTES = 32 * 1024
SC_SLOTS = 8


def _sc_scatter(rows, dest, n_out):
    n_rows, width = rows.shape
    picks = dest.shape[0]
    chunk = SC_CHUNK_BYTES // (4 * width)
    per_worker = n_rows // SC_WORKERS
    n_chunks = per_worker // chunk
    assert per_worker * SC_WORKERS == n_rows and n_chunks * chunk == per_worker and n_chunks % 2 == 0
    mesh = plsc.VectorSubcoreMesh(core_axis_name="c", subcore_axis_name="s")

    @functools.partial(
        pl.kernel, mesh=mesh,
        out_type=jax.ShapeDtypeStruct((n_out, width), rows.dtype),
        scratch_types=[pltpu.VMEM((picks, n_chunks, chunk), jnp.int32),
                       pltpu.VMEM((2, chunk, width), rows.dtype),
                       pltpu.SemaphoreType.DMA((2,)),
                       pltpu.SemaphoreType.DMA((2,))])
    def scatter(r_hbm, d_hbm, o_hbm, idx_v, rows_v, lsem, ssem):
        worker = lax.axis_index("s") * SC_CORES + lax.axis_index("c")
        base = worker * per_worker
        for k in range(picks):
            pltpu.sync_copy(d_hbm.at[k, worker], idx_v.at[k])

        def load_copy(c, b):
            src = pl.ds(pl.multiple_of(base + c * chunk, chunk), chunk)
            return pltpu.make_async_copy(r_hbm.at[src], rows_v.at[b], lsem.at[b])

        def store_copy(c, b, k):
            return pltpu.make_async_copy(rows_v.at[b], o_hbm.at[idx_v.at[k, c]], ssem.at[b])

        load_copy(0, 0).start()

        @pl.loop(0, n_chunks, step=2)
        def _(c0):
            for b in range(2):
                c = c0 + b
                load_copy(c, b).wait()
                for k in range(picks):
                    store_copy(c, b, k).start()

                @pl.when(c > 0)
                def _():
                    for k in range(picks):
                        store_copy(c - 1, 1 - b, k).wait()

                @pl.when(c + 1 < n_chunks)
                def _():
                    load_copy(c + 1, 1 - b).start()

        for k in range(picks):
            store_copy(n_chunks - 1, 1, k).wait()

    return scatter(rows, dest.reshape(picks, SC_WORKERS, n_chunks, chunk))


def _sc_gather(table, idx):
    n_idx = idx.shape[0]
    width = table.shape[1]
    chunk = SC_GATHER_CHUNK_BYTES // (4 * width)
    per_worker = n_idx // SC_WORKERS
    n_chunks = per_worker // chunk
    ahead = SC_SLOTS - 1
    assert per_worker * SC_WORKERS == n_idx and n_chunks * chunk == per_worker and n_chunks % SC_SLOTS == 0
    mesh = plsc.VectorSubcoreMesh(core_axis_name="c", subcore_axis_name="s")

    @functools.partial(
        pl.kernel, mesh=mesh,
        out_type=jax.ShapeDtypeStruct((n_idx, width), table.dtype),
        scratch_types=[pltpu.VMEM((per_worker,), jnp.int32),
                       pltpu.VMEM((SC_SLOTS, chunk, width), table.dtype),
                       pltpu.SemaphoreType.DMA((SC_SLOTS,)),
                       pltpu.SemaphoreType.DMA((SC_SLOTS,))])
    def gather(t_hbm, i_hbm, o_hbm, idx_v, rows_v, gsem, wsem):
        worker = lax.axis_index("s") * SC_CORES + lax.axis_index("c")
        base = worker * per_worker
        pltpu.sync_copy(i_hbm.at[pl.ds(pl.multiple_of(base, chunk), per_worker)], idx_v)

        def gather_copy(c, b):
            ids = idx_v.at[pl.ds(pl.multiple_of(c * chunk, chunk), chunk)]
            return pltpu.make_async_copy(t_hbm.at[ids], rows_v.at[b], gsem.at[b])

        def write_copy(c, b):
            rows = pl.ds(pl.multiple_of(base + c * chunk, chunk), chunk)
            return pltpu.make_async_copy(rows_v.at[b], o_hbm.at[rows], wsem.at[b])

        for c in range(ahead):
            gather_copy(c, c).start()

        @pl.loop(0, n_chunks, step=SC_SLOTS)
        def _(c0):
            for b in range(SC_SLOTS):
                c = c0 + b
                refill = (b + ahead) % SC_SLOTS
                gather_copy(c, b).wait()
                write_copy(c, b).start()

                @pl.when(c > 0)
                def _():
                    write_copy(c - 1, refill).wait()

                @pl.when(c + ahead < n_chunks)
                def _():
                    gather_copy(c + ahead, refill).start()

        write_copy(n_chunks - 1, (n_chunks - 1) % SC_SLOTS).wait()

    return gather(table, idx)


TC = 512


def _combine_kernel(x_ref, h_ref, y_ref, wgt_ref, gate_ref, w1_ref, w3_ref, w2_ref, fg_ref, o_ref, *, final):
    hi, lo = _unpack_bf16_pairs(h_ref[...])
    a = _dot_halves(hi, lo, w1_ref)
    b = _dot_halves(hi, lo, w3_ref)
    ffn = jnp.dot((_silu(a) * b).astype(BF16), w2_ref[...], preferred_element_type=F32)
    wgt = wgt_ref[...]
    r_hi = None
    r_lo = None
    for k in range(TOP_K):
        yk = y_ref[k]
        w = wgt[:, k:k + 1]
        t_hi = lax.bitcast_convert_type(yk & jnp.uint32(0xFFFF0000), F32) * w
        t_lo = lax.bitcast_convert_type(yk << 16, F32) * w
        r_hi = t_hi if r_hi is None else r_hi + t_hi
        r_lo = t_lo if r_lo is None else r_lo + t_lo
    x = x_ref[...] + gate_ref[...] * (ffn + jnp.concatenate([r_hi, r_lo], axis=1))
    o_ref[...] = _rms(x, fg_ref[...]) if final else x


def _combine(x, h, y_rows, wgt, mod, sw1, sw3, sw2, final_g, final):
    weights = (sw1.astype(BF16), sw3.astype(BF16), sw2.astype(BF16), final_g.reshape(1, D))

    def rows_from(first_row, n_rows):
        b0 = first_row // TC
        return pl.pallas_call(
            functools.partial(_combine_kernel, final=final),
            grid=(n_rows // TC,),
            in_specs=[pl.BlockSpec((TC, D), lambda i: (i + b0, 0)),
                      pl.BlockSpec((TC, D // 2), lambda i: (i + b0, 0)),
                      pl.BlockSpec((TOP_K, TC, D // 2), lambda i: (0, i + b0, 0)),
                      pl.BlockSpec((TC, TOP_K), lambda i: (i + b0, 0)),
                      _mod_spec(5, TC, b0),
                      pl.BlockSpec((D, FF), lambda i: (0, 0)),
                      pl.BlockSpec((D, FF), lambda i: (0, 0)),
                      pl.BlockSpec((FF, D), lambda i: (0, 0)),
                      pl.BlockSpec((1, D), lambda i: (0, 0))],
            out_specs=pl.BlockSpec((TC, D), lambda i: (i, 0)),
            out_shape=jax.ShapeDtypeStruct((n_rows, D), F32),
            compiler_params=_cparams(1),
            name="combine",
        )(x, h, y_rows, wgt, mod, *weights)

    return rows_from(0, NP_TOK), rows_from(NP_TOK, NS_TOK)


def kernel(x_prompt, x_sample, cache_a_k, cache_a_v, state_ret_fwd, state_ret_bwd, cache_c_k, cache_c_v,
           c, c_ctx, norm1_g, norm2_g, ada_w, ada_b, even_w_in, even_w_out, sink_a, ret_decay_fwd,
           ret_decay_bwd, ret_gn_g, odd_w_in, odd_w_out, na_rpb, router_w, router_b, exp_w1, exp_w3,
           exp_w2, sh_w1, sh_w3, sh_w2, final_g):
    x = (x_prompt.reshape(NP_TOK, D), x_sample.reshape(NS_TOK, D))
    cc = jnp.concatenate([c_ctx[None], c, jnp.zeros((8 - 1 - DEC_BATCH, D), F32)], axis=0)
    rope = _rope_tables()
    outs = {}
    for l in range(2):
        mod = _ada(cc, ada_w, ada_b, l)
        if l == 0:
            p = _inproj(x, norm1_g[l], mod, even_w_in[0].astype(BF16), rope, A_Q + A_KV)
            oa_p = _ctx_gqa(p, sink_a[0])
            oa_s = _win_attention(p, cache_a_k[:, 0], cache_a_v[:, 0], sink_a[0])
            zero = jnp.zeros((BATCH, B_HEADS // 2, LANES, LANES), F32)
            ob_p, sf, sb = _retention(p, 0, BATCH, SEQ, ret_decay_fwd[0], ret_decay_bwd[0], ret_gn_g[0], zero, zero)
            ob_s, _, _ = _retention(p, NP_TOK, DEC_BATCH, DEC_SEQ, ret_decay_fwd[0], ret_decay_bwd[0], ret_gn_g[0],
                                    _blockdiag_states(state_ret_fwd[:, 0]), _blockdiag_states(state_ret_bwd[:, 0]))
            w_out = even_w_out[0].astype(BF16)
            parts = [(oa_p, oa_s, w_out[:A_Q]), (ob_p, ob_s, w_out[A_Q:])]
            outs["a_k"] = p[:NP_TOK, A_Q:A_Q + A_KV].reshape(BATCH, 1, SEQ, A_KV_HEADS, HD)
            outs["a_v"] = p[:NP_TOK, A_Q + A_KV:A_Q + 2 * A_KV].reshape(BATCH, 1, SEQ, A_KV_HEADS, HD)
            outs["r_f"] = _diag_states(sf).reshape(BATCH, 1, B_HEADS, HD, HD)
            outs["r_b"] = _diag_states(sb).reshape(BATCH, 1, B_HEADS, HD, HD)
        else:
            p = _inproj(x, norm1_g[l], mod, odd_w_in[0].astype(BF16), rope, 0)
            o_p = _ctx_mha(p)
            o_s = _na_attention(p, cache_c_k[:, 0], cache_c_v[:, 0], na_rpb[0])
            parts = [(o_p, o_s, odd_w_out[0].astype(BF16))]
            outs["c_k"] = p[:NP_TOK, C_W:2 * C_W].reshape(BATCH, 1, SEQ, C_HEADS, HD)
            outs["c_v"] = p[:NP_TOK, 2 * C_W:3 * C_W].reshape(BATCH, 1, SEQ, C_HEADS, HD)
        x_mid, h, dest, gate_t, plan = _outproj(x, parts, mod, norm2_g[l], router_w[l], router_b[l])
        y = _experts(plan, _sc_scatter(h, dest, N_EXPERTS * N_TOK), exp_w1, exp_w3, exp_w2, l)
        y_rows = _sc_gather(y, dest.reshape(N_ASSIGN)).reshape(TOP_K, N_TOK, D // 2)
        x = _combine(x_mid, h, y_rows, gate_t.T, mod, sh_w1[l], sh_w3[l], sh_w2[l], final_g, final=(l == 1))
    y_prompt = x[0].reshape(BATCH, SEQ, D)
    y_sample = x[1].reshape(DEC_BATCH, DEC_SEQ, D)
    return (y_prompt, y_sample, outs["a_k"], outs["a_v"], outs["r_f"], outs["r_b"], outs["c_k"], outs["c_v"])
```

```python
import functools

import jax
import jax.numpy as jnp
from jax import lax
from jax.experimental import pallas as pl
from jax.experimental.pallas import tpu as pltpu
from jax.experimental.pallas import tpu_sc as plsc

F32 = jnp.float32
BF16 = jnp.bfloat16
HIGHEST = lax.Precision.HIGHEST

D = 1024
BATCH = 32
SEQ = 256
DEC_BATCH = 4
DEC_SEQ = 4096
PAST = 256
GRID_W = 64
HD = 64
EPS = 1e-6
NEG = -1e30
ROPE_BASE = 10000.0
A_HEADS = 8
A_KV_HEADS = 2
A_Q = A_HEADS * HD
A_KV = A_KV_HEADS * HD
B_HEADS = 8
B_W = B_HEADS * HD
EVEN_IN = A_Q + 2 * A_KV + 4 * B_W
C_HEADS = 16
C_W = C_HEADS * HD
NA_KH = 8
NA_KW = 16
N_EXPERTS = 64
TOP_K = 8
N_GROUPS = 8
TOPK_GROUPS = 4
FF = 256
ROUTED_SCALE = 2.5
MOE_BLOCK = 1024
EXPERT_X_SLOTS = 3
RET_CHUNK = 256
RET_UNROLL = 4
RET_NORM_ROWS = 1024
A_WINDOW = 128

NP_TOK = BATCH * SEQ
NS_TOK = DEC_BATCH * DEC_SEQ
N_TOK = NP_TOK + NS_TOK
N_ASSIGN = N_TOK * TOP_K
N_MOE_BLOCKS = (N_ASSIGN + N_EXPERTS * (MOE_BLOCK - 1) + MOE_BLOCK - 1) // MOE_BLOCK
PLAN_LANES = 512
assert N_TOK % MOE_BLOCK == 0 and N_MOE_BLOCKS <= PLAN_LANES

LANES = 128
TM = 512
NA_ROWS = 8
V7X_VMEM_LIMIT = 56 * 1024 * 1024

NT_DIMS = (((1,), (1,)), ((), ()))


def _cparams(n_axes, vmem=V7X_VMEM_LIMIT):
    return pltpu.CompilerParams(dimension_semantics=("arbitrary",) * n_axes, vmem_limit_bytes=vmem)


def _seg_of_block(i, rows):
    row0 = i * rows
    return jnp.where(row0 < NP_TOK, 0, 1 + (row0 - NP_TOK) // DEC_SEQ)


def _mod_spec(chunk, rows=TM, first_block=0):
    return pl.BlockSpec((None, 1, D), lambda i: (_seg_of_block(i + first_block, rows), 0, chunk))


def _pair_specs(width, rows=TM):
    npb = NP_TOK // rows
    nsb = NS_TOK // rows
    return [pl.BlockSpec((rows, width), lambda i: (jnp.minimum(i, npb - 1), 0)),
            pl.BlockSpec((rows, width), lambda i: (jnp.clip(i - npb, 0, nsb - 1), 0))]


def _pick_rows(p_ref, s_ref, rows=TM):
    return jnp.where(pl.program_id(0) < NP_TOK // rows, p_ref[...], s_ref[...])


def _silu(x):
    return x * jax.nn.sigmoid(x)


def _rms(x, g):
    return x * lax.rsqrt(jnp.mean(x * x, axis=-1, keepdims=True) + EPS) * g


def _lane_lo():
    return lax.broadcasted_iota(jnp.int32, (1, LANES), 1) < HD


def _ada_kernel(c_ref, w_ref, b_ref, o_ref):
    a = _silu(c_ref[...])
    o_ref[...] = jnp.dot(a, w_ref[...], preferred_element_type=F32, precision=HIGHEST) + b_ref[...]


def _ada(cc, w, b, layer):
    tn = 1536
    out = pl.pallas_call(
        _ada_kernel,
        grid=(6 * D // tn,),
        in_specs=[pl.BlockSpec((8, D), lambda j: (0, 0)),
                  pl.BlockSpec((None, D, tn), lambda j: (layer, 0, j)),
                  pl.BlockSpec((None, 1, tn), lambda j: (layer, 0, j))],
        out_specs=pl.BlockSpec((8, tn), lambda j: (0, j)),
        out_shape=jax.ShapeDtypeStruct((8, 6 * D), F32),
        compiler_params=_cparams(1),
        name="ada",
    )(cc, w, b.reshape(b.shape[0], 1, 6 * D))
    return out.reshape(8, 1, 6 * D)


def _inproj_kernel(xp_ref, xs_ref, g_ref, shift_ref, scale_ref, w_ref, cos_ref, sin_ref, o_ref, *, rope_cols):
    h = _rms(_pick_rows(xp_ref, xs_ref), g_ref[...]) * (1.0 + scale_ref[...]) + shift_ref[...]
    o = jnp.dot(h.astype(BF16), w_ref[...], preferred_element_type=F32)
    if rope_cols:
        cos = cos_ref[...]
        sin = sin_ref[...]
        lane = lax.broadcasted_iota(jnp.int32, (1, LANES), 1)
        first = (lane % 32) < 16
        for c in range(rope_cols // LANES):
            oc = o[:, c * LANES:(c + 1) * LANES]
            partner = jnp.where(first, pltpu.roll(oc, LANES - 16, 1), pltpu.roll(oc, 16, 1))
            o_ref[:, c * LANES:(c + 1) * LANES] = oc * cos + partner * sin
        o_ref[:, rope_cols:] = o[:, rope_cols:]
    else:
        o_ref[...] = o


def _rope_tables():
    half = HD // 2
    inv = ROPE_BASE ** (-jnp.arange(0, half, 2, dtype=F32) / half)
    t = jnp.arange(DEC_SEQ)
    ang_r = (t // GRID_W).astype(F32)[:, None] * inv[None]
    ang_c = (t % GRID_W).astype(F32)[:, None] * inv[None]

    def head(fn_r, fn_c, sign):
        return jnp.concatenate([sign[0] * fn_r, sign[1] * fn_r, sign[0] * fn_c, sign[1] * fn_c], axis=-1)

    cos = head(jnp.cos(ang_r), jnp.cos(ang_c), (1.0, 1.0))
    sin = head(jnp.sin(ang_r), jnp.sin(ang_c), (-1.0, 1.0))
    cos = jnp.concatenate([jnp.ones((TM, HD), F32), cos], axis=0)
    sin = jnp.concatenate([jnp.zeros((TM, HD), F32), sin], axis=0)
    return jnp.tile(cos, (1, 2)), jnp.tile(sin, (1, 2))


def _inproj(x, g, mod, w_bf16, rope, rope_cols):
    n_out = w_bf16.shape[1]
    npb = NP_TOK // TM
    spb = DEC_SEQ // TM

    def rope_map(i):
        return (jnp.where(i < npb, 0, 1 + (i - npb) % spb), 0)

    return pl.pallas_call(
        functools.partial(_inproj_kernel, rope_cols=rope_cols),
        grid=(N_TOK // TM,),
        in_specs=_pair_specs(D) + [
                  pl.BlockSpec((1, D), lambda i: (0, 0)),
                  _mod_spec(0), _mod_spec(1),
                  pl.BlockSpec((D, n_out), lambda i: (0, 0)),
                  pl.BlockSpec((TM, LANES), rope_map),
                  pl.BlockSpec((TM, LANES), rope_map)],
        out_specs=pl.BlockSpec((TM, n_out), lambda i: (i, 0)),
        out_shape=jax.ShapeDtypeStruct((N_TOK, n_out), F32),
        compiler_params=_cparams(1),
        name="inproj",
    )(x[0], x[1], g.reshape(1, D), mod, mod, w_bf16, rope[0], rope[1])


def _softmax_av(s_list, v_list, sink=None):
    mx = s_list[0].max(axis=-1, keepdims=True)
    for s in s_list[1:]:
        mx = jnp.maximum(mx, s.max(axis=-1, keepdims=True))
    if sink is not None:
        mx = jnp.maximum(mx, sink)
    den = jnp.exp(sink - mx) if sink is not None else 0.0
    acc = None
    for s, v in zip(s_list, v_list):
        p = jnp.exp(s - mx)
        den = den + p.sum(axis=-1, keepdims=True)
        pv = jnp.dot(p.astype(BF16), v, preferred_element_type=F32)
        acc = pv if acc is None else acc + pv
    return acc / den


def _dup_half(x, j, lo):
    xr = pltpu.roll(x, HD, 1)
    return jnp.where(lo, x, xr) if j == 0 else jnp.where(lo, xr, x)


def _stack_heads(q_ref, heads, lo, scale):
    parts = []
    for h in heads:
        qp = q_ref[:, (h // 2) * LANES:(h // 2 + 1) * LANES]
        keep = lo if h % 2 == 0 else jnp.logical_not(lo)
        parts.append(jnp.where(keep, qp, 0.0) * scale)
    return jnp.concatenate(parts, axis=0).astype(BF16)


def _sink_column(sink_ref, heads, rows):
    return jnp.concatenate([jnp.full((rows, 1), sink_ref[h], F32) for h in heads], axis=0)


def _ctx_gqa_kernel(sink_ref, q_ref, k_ref, v_ref, o_ref):
    lo = _lane_lo()
    k = k_ref[...]
    v = v_ref[...]
    group = A_HEADS // A_KV_HEADS
    scores = []
    for j in range(A_KV_HEADS):
        q = _stack_heads(q_ref, list(range(group * j, group * (j + 1))), lo, HD ** -0.5)
        scores.append(lax.dot_general(q, _dup_half(k, j, lo).astype(BF16), NT_DIMS, preferred_element_type=F32))
    s = jnp.concatenate(scores, axis=0)
    sink = _sink_column(sink_ref, list(range(A_HEADS)), SEQ)
    mx = jnp.maximum(s.max(axis=-1, keepdims=True), sink)
    e = jnp.exp(s - mx)
    den = jnp.exp(sink - mx) + e.sum(axis=-1, keepdims=True)
    e = e.astype(BF16)
    rows_per_group = group * SEQ
    for j in range(A_KV_HEADS):
        rows = slice(j * rows_per_group, (j + 1) * rows_per_group)
        o = jnp.dot(e[rows], _dup_half(v, j, lo).astype(BF16), preferred_element_type=F32) / den[rows]
        for t in range(group // 2):
            pair = (group * j) // 2 + t
            o_ref[:, pair * LANES:(pair + 1) * LANES] = jnp.where(
                lo, o[(2 * t) * SEQ:(2 * t + 1) * SEQ], o[(2 * t + 1) * SEQ:(2 * t + 2) * SEQ])


def _ctx_gqa(p, sink):
    return pl.pallas_call(
        _ctx_gqa_kernel,
        grid_spec=pltpu.PrefetchScalarGridSpec(
            num_scalar_prefetch=1,
            grid=(BATCH,),
            in_specs=[pl.BlockSpec((SEQ, A_Q), lambda b, s: (b, 0)),
                      pl.BlockSpec((SEQ, A_KV), lambda b, s: (b, A_Q // A_KV)),
                      pl.BlockSpec((SEQ, A_KV), lambda b, s: (b, A_Q // A_KV + 1))],
            out_specs=pl.BlockSpec((SEQ, A_Q), lambda b, s: (b, 0))),
        out_shape=jax.ShapeDtypeStruct((NP_TOK, A_Q), F32),
        compiler_params=_cparams(1),
        name="ctx_gqa",
    )(sink, p, p, p)


def _win_kernel(sink_ref, q_ref, kp_ref, kc_ref, kn_ref, vp_ref, vc_ref, vn_ref, ck_ref, cv_ref, o_ref):
    i = pl.program_id(1)
    lo = _lane_lo()
    k = jnp.concatenate([kp_ref[...], kc_ref[...], kn_ref[...]], axis=0)
    v = jnp.concatenate([vp_ref[...], vc_ref[...], vn_ref[...]], axis=0)
    ck = ck_ref[...]
    cv = cv_ref[...]
    group = A_HEADS // A_KV_HEADS
    n_keys = WIN_Q + 2 * A_WINDOW
    qpos = i * WIN_Q + lax.broadcasted_iota(jnp.int32, (WIN_Q, n_keys), 0)
    kpos = i * WIN_Q - A_WINDOW + lax.broadcasted_iota(jnp.int32, (WIN_Q, n_keys), 1)
    valid = (jnp.abs(kpos - qpos) <= A_WINDOW) & (kpos >= 0) & (kpos < DEC_SEQ)
    valid = jnp.concatenate([valid] * group, axis=0)
    s_loc, s_ctx, values = [], [], []
    for j in range(A_KV_HEADS):
        heads = list(range(group * j, group * (j + 1)))
        kd = _dup_half(k, j, lo).astype(BF16)
        ckd = _dup_half(ck, j, lo).astype(BF16)
        values.append((_dup_half(v, j, lo).astype(BF16), _dup_half(cv, j, lo).astype(BF16)))
        q = _stack_heads(q_ref, heads, lo, HD ** -0.5)
        s_loc.append(jnp.where(valid, lax.dot_general(q, kd, NT_DIMS, preferred_element_type=F32), NEG))
        s_ctx.append(lax.dot_general(q, ckd, NT_DIMS, preferred_element_type=F32))
    s_loc = jnp.concatenate(s_loc, axis=0)
    s_ctx = jnp.concatenate(s_ctx, axis=0)
    sink = _sink_column(sink_ref, list(range(A_HEADS)), WIN_Q)
    mx = jnp.maximum(jnp.maximum(s_loc.max(axis=-1, keepdims=True), s_ctx.max(axis=-1, keepdims=True)), sink)
    p_loc = jnp.exp(s_loc - mx)
    p_ctx = jnp.exp(s_ctx - mx)
    den = p_loc.sum(axis=-1, keepdims=True) + p_ctx.sum(axis=-1, keepdims=True) + jnp.exp(sink - mx)
    p_loc = p_loc.astype(BF16)
    p_ctx = p_ctx.astype(BF16)
    rows_per_group = group * WIN_Q
    for j, (vd, cvd) in enumerate(values):
        rows = slice(j * rows_per_group, (j + 1) * rows_per_group)
        o = (jnp.dot(p_loc[rows], vd, preferred_element_type=F32)
             + jnp.dot(p_ctx[rows], cvd, preferred_element_type=F32)) / den[rows]
        for t in range(group // 2):
            pair = (group * j) // 2 + t
            o_ref[:, pair * LANES:(pair + 1) * LANES] = jnp.where(
                lo, o[(2 * t) * WIN_Q:(2 * t + 1) * WIN_Q], o[(2 * t + 1) * WIN_Q:(2 * t + 2) * WIN_Q])


WIN_Q = 256


def _win_attention(p, cache_k, cache_v, sink):
    nblk = DEC_SEQ // WIN_Q
    side = WIN_Q // A_WINDOW
    nside = DEC_SEQ // A_WINDOW
    base = NP_TOK // WIN_Q
    side_base = NP_TOK // A_WINDOW
    kcol = A_Q // A_KV

    def main_spec(col):
        return pl.BlockSpec((WIN_Q, A_KV), lambda b, i, s: (base + b * nblk + i, col))

    def side_spec(col, off):
        return pl.BlockSpec((A_WINDOW, A_KV),
                            lambda b, i, s: (side_base + b * nside + jnp.clip(side * i + off, 0, nside - 1), col))

    ctx_spec = pl.BlockSpec((None, PAST, A_KV), lambda b, i, s: (b, 0, 0))
    return pl.pallas_call(
        _win_kernel,
        grid_spec=pltpu.PrefetchScalarGridSpec(
            num_scalar_prefetch=1,
            grid=(DEC_BATCH, nblk),
            in_specs=[pl.BlockSpec((WIN_Q, A_Q), lambda b, i, s: (base + b * nblk + i, 0)),
                      side_spec(kcol, -1), main_spec(kcol), side_spec(kcol, side),
                      side_spec(kcol + 1, -1), main_spec(kcol + 1), side_spec(kcol + 1, side),
                      ctx_spec, ctx_spec],
            out_specs=pl.BlockSpec((WIN_Q, A_Q), lambda b, i, s: (b * nblk + i, 0))),
        out_shape=jax.ShapeDtypeStruct((NS_TOK, A_Q), F32),
        compiler_params=_cparams(2),
        name="win_attn",
    )(sink, p, p, p, p, p, p, p, cache_k.reshape(DEC_BATCH, PAST, A_KV), cache_v.reshape(DEC_BATCH, PAST, A_KV))


def _ret_kernel(df_ref, db_ref, q_ref, k_ref, v_ref, g_ref, gn_ref, s0f_ref, s0b_ref,
                o_ref, sf_ref, sb_ref, of_scr, ob_scr, *, length):
    c_len = RET_CHUNK
    n = length // c_len
    lo = _lane_lo()
    hi = jnp.logical_not(lo)
    row = lax.broadcasted_iota(jnp.int32, (c_len, c_len), 0)
    col = lax.broadcasted_iota(jnp.int32, (c_len, c_len), 1)
    rowp = lax.broadcasted_iota(jnp.int32, (LANES, LANES), 0)
    colp = lax.broadcasted_iota(jnp.int32, (LANES, LANES), 1)
    blockdiag = (rowp < HD) == (colp < HD)
    idx = lax.broadcasted_iota(jnp.int32, (c_len, 1), 0).astype(F32)

    def direction(dec_ref, forward):
        lg = -jnp.exp(dec_ref[...])
        diff = (row - col) if forward else (col - row)
        keep = (diff >= 0) if forward else (diff > 0)
        dist = jnp.maximum(diff, 0).astype(F32)
        dm = jnp.concatenate([jnp.where(keep, jnp.exp(dist * lg[:, off:off + 1]), 0.0) for off in (0, HD)], axis=0)
        if forward:
            xi = jnp.exp((idx + 1.0) * lg)
            zeta = jnp.exp((c_len - 1.0 - idx) * lg)
        else:
            xi = jnp.exp((c_len - idx) * lg)
            zeta = jnp.exp(idx * lg)
        return dm, xi, zeta, jnp.exp(c_len * lg)

    def chunk(c, state, consts):
        dm, xi, zeta, gch = consts
        rows = pl.ds(pl.multiple_of(c * c_len, c_len), c_len)
        qc = q_ref[rows, :]
        kc = k_ref[rows, :] * HD ** -0.5
        vc = v_ref[rows, :].astype(BF16)
        kb = kc.astype(BF16)
        q2 = jnp.concatenate([jnp.where(lo, qc, 0.0), jnp.where(hi, qc, 0.0)], axis=0).astype(BF16)
        inner = lax.dot_general(q2, kb, NT_DIMS, preferred_element_type=F32) * dm
        kz_t = (kc * zeta).T
        res = jnp.dot(jnp.concatenate([inner, kz_t], axis=0).astype(BF16), vc, preferred_element_type=F32)
        cross = jnp.dot(qc.astype(BF16), state.astype(BF16), preferred_element_type=F32) * xi
        o = jnp.where(lo, res[:c_len], res[c_len:2 * c_len]) + cross
        state = gch * state + jnp.where(blockdiag, res[2 * c_len:], 0.0)
        return rows, o, state

    cf = direction(df_ref, True)
    cb = direction(db_ref, False)

    def scan_body(t, states):
        rows_f, o_f, state_f = chunk(t, states[0], cf)
        of_scr[rows_f, :] = o_f
        rows_b, o_b, state_b = chunk(n - 1 - t, states[1], cb)
        ob_scr[rows_b, :] = o_b
        return state_f, state_b

    state_f, state_b = lax.fori_loop(0, n, scan_body, (s0f_ref[...], s0b_ref[...]), unroll=min(n, RET_UNROLL))
    sf_ref[...] = state_f
    sb_ref[...] = state_b

    gn = gn_ref[...]
    norm_rows = min(RET_NORM_ROWS, length)
    n_norm = length // norm_rows

    def per_head(x):
        a = jnp.where(lo, x, 0.0).sum(axis=-1, keepdims=True)
        b = jnp.where(hi, x, 0.0).sum(axis=-1, keepdims=True)
        return jnp.where(lo, a, b) * (1.0 / HD)

    def norm_body(t, carry):
        rows = pl.ds(pl.multiple_of(t * norm_rows, norm_rows), norm_rows)
        o = of_scr[rows, :] + ob_scr[rows, :]
        d = o - per_head(o)
        y = d * lax.rsqrt(per_head(d * d) + EPS) * gn
        o_ref[rows, :] = _silu(g_ref[rows, :]) * y
        return carry

    lax.fori_loop(0, n_norm, norm_body, 0)


def _pair_lanes(v):
    return jnp.repeat(v.astype(F32), HD).reshape(B_HEADS // 2, 1, LANES)


def _blockdiag_states(s):
    b = s.shape[0]
    s = s.astype(F32).reshape(b, B_HEADS // 2, 2, HD, HD)
    z = jnp.zeros_like(s[:, :, 0])
    top = jnp.concatenate([s[:, :, 0], z], axis=-1)
    bot = jnp.concatenate([z, s[:, :, 1]], axis=-1)
    return jnp.concatenate([top, bot], axis=-2)


def _diag_states(sp):
    b = sp.shape[0]
    s = jnp.stack([sp[:, :, :HD, :HD], sp[:, :, HD:, HD:]], axis=2)
    return s.reshape(b, B_HEADS, HD, HD)


def _retention(p, row_base, batch, length, dec_f, dec_b, gn_g, s0f, s0b):
    npairs = B_HEADS // 2
    blk0 = row_base // length
    qcol = (A_Q + 2 * A_KV) // LANES

    def col_spec(off):
        return pl.BlockSpec((length, LANES), lambda b, h: (blk0 + b, qcol + off * npairs + h))

    lane_spec = pl.BlockSpec((None, 1, LANES), lambda b, h: (h, 0, 0))
    state_spec = pl.BlockSpec((None, None, LANES, LANES), lambda b, h: (b, h, 0, 0))
    state_shape = jax.ShapeDtypeStruct((batch, npairs, LANES, LANES), F32)
    return pl.pallas_call(
        functools.partial(_ret_kernel, length=length),
        grid=(batch, npairs),
        in_specs=[lane_spec, lane_spec, col_spec(0), col_spec(1), col_spec(2), col_spec(3), lane_spec,
                  state_spec, state_spec],
        out_specs=[pl.BlockSpec((length, LANES), lambda b, h: (b, h)), state_spec, state_spec],
        out_shape=[jax.ShapeDtypeStruct((batch * length, B_W), F32), state_shape, state_shape],
        scratch_shapes=[pltpu.VMEM((length, LANES), F32), pltpu.VMEM((length, LANES), F32)],
        compiler_params=_cparams(2),
        name="retention",
    )(_pair_lanes(dec_f), _pair_lanes(dec_b), p, p, p, p, gn_g.reshape(npairs, 1, LANES), s0f, s0b)


def _ctx_mha_kernel(q_ref, k_ref, v_ref, o_ref):
    lo = _lane_lo()
    for pair in range(C_HEADS // 2):
        cols = slice(pair * LANES, (pair + 1) * LANES)
        q = _stack_heads(q_ref, [2 * pair, 2 * pair + 1], lo, HD ** -0.5)
        s = lax.dot_general(q, k_ref[:, cols].astype(BF16), NT_DIMS, preferred_element_type=F32)
        o = _softmax_av([s], [v_ref[:, cols].astype(BF16)])
        o_ref[:, cols] = jnp.where(lo, o[:SEQ], o[SEQ:])


def _ctx_mha(p):
    return pl.pallas_call(
        _ctx_mha_kernel,
        grid=(BATCH,),
        in_specs=[pl.BlockSpec((SEQ, C_W), lambda b: (b, 0)),
                  pl.BlockSpec((SEQ, C_W), lambda b: (b, 1)),
                  pl.BlockSpec((SEQ, C_W), lambda b: (b, 2))],
        out_specs=pl.BlockSpec((SEQ, C_W), lambda b: (b, 0)),
        out_shape=jax.ShapeDtypeStruct((NP_TOK, C_W), F32),
        compiler_params=_cparams(1),
        name="ctx_mha",
    )(p, p, p)


NA_WIN_ROWS = 2 * NA_ROWS
NA_WIN = NA_WIN_ROWS * GRID_W
NA_QROWS = NA_ROWS * GRID_W
NA_PAD_ROWS = NA_KH // 2
NA_TABLE = 1536


def _na_kernel(q_ref, kp_ref, km_ref, kn_ref, vp_ref, vm_ref, vn_ref, ck_ref, cv_ref, ue_ref, uo_ref, o_ref):
    r0 = pl.program_id(2) * NA_ROWS
    n_rows = DEC_SEQ // GRID_W
    lo = _lane_lo()
    k = jnp.concatenate([kp_ref[...], km_ref[...], kn_ref[...]], axis=0).astype(BF16)
    v = jnp.concatenate([vp_ref[...], vm_ref[...], vn_ref[...]], axis=0).astype(BF16)
    ck = ck_ref[...].astype(BF16)
    cv = cv_ref[...].astype(BF16)
    q = q_ref[...] * HD ** -0.5
    klane = lax.broadcasted_iota(jnp.int32, (1, NA_WIN), 1)
    outs = []
    for half, keep in enumerate((lo, jnp.logical_not(lo))):
        qh = jnp.where(keep, q, 0.0).astype(BF16)
        s = lax.dot_general(qh, k, NT_DIMS, preferred_element_type=F32)
        s_ctx = lax.dot_general(qh, ck, NT_DIMS, preferred_element_type=F32)
        p_loc, p_ctx, den = [], [], []
        for rq in range(NA_ROWS):
            rows = slice(rq * GRID_W, (rq + 1) * GRID_W)
            start = NA_KH - 1 - rq
            if start % 2 == 0:
                u = ue_ref[half, :, start * GRID_W:start * GRID_W + NA_WIN]
            else:
                u = uo_ref[half, :, (start - 1) * GRID_W:(start - 1) * GRID_W + NA_WIN]
            r = r0 + rq
            first = jnp.clip(r - NA_KH // 2, 0, n_rows - NA_KH)
            lane0 = (first - r0 + NA_PAD_ROWS) * GRID_W
            in_rows = (klane >= lane0) & (klane < lane0 + NA_KH * GRID_W)
            sl = jnp.where(in_rows, s[rows] + u, NEG)
            sc = s_ctx[rows]
            mx = jnp.maximum(sl.max(axis=-1, keepdims=True), sc.max(axis=-1, keepdims=True))
            el = jnp.exp(sl - mx)
            ec = jnp.exp(sc - mx)
            den.append(el.sum(axis=-1, keepdims=True) + ec.sum(axis=-1, keepdims=True))
            p_loc.append(el.astype(BF16))
            p_ctx.append(ec.astype(BF16))
        acc = (jnp.dot(jnp.concatenate(p_loc, axis=0), v, preferred_element_type=F32)
               + jnp.dot(jnp.concatenate(p_ctx, axis=0), cv, preferred_element_type=F32))
        outs.append(acc / jnp.concatenate(den, axis=0))
    o_ref[...] = jnp.where(lo, outs[0], outs[1])


def _na_bias_tables(rpb):
    cq = jnp.arange(GRID_W)
    ck = jnp.arange(GRID_W)
    dc = jnp.clip(ck[None] - cq[:, None], -(NA_KW - 1), NA_KW - 1) + NA_KW - 1
    cs = jnp.clip(cq - NA_KW // 2, 0, GRID_W - NA_KW)
    col_ok = (ck[None] >= cs[:, None]) & (ck[None] < cs[:, None] + NA_KW)
    t = rpb.astype(F32)[:, :, dc]
    t = jnp.where(col_ok[None, None], t, NEG).transpose(0, 2, 1, 3)
    n_dr = 2 * NA_KH - 1
    blocks = NA_TABLE // GRID_W
    t = jnp.pad(t, ((0, 0), (0, 0), (NA_PAD_ROWS, blocks - n_dr - NA_PAD_ROWS), (0, 0)), constant_values=NEG)
    ue = t.reshape(C_HEADS, GRID_W, NA_TABLE)
    uo = jnp.concatenate([ue[..., GRID_W:], jnp.full((C_HEADS, GRID_W, GRID_W), NEG, F32)], axis=-1)
    return ue, uo


def _na_attention(p, cache_k, cache_v, rpb):
    npairs = C_HEADS // 2
    nrb = DEC_SEQ // NA_QROWS
    half = NA_QROWS // 2
    qbase = NP_TOK // NA_QROWS
    hbase = NP_TOK // half
    kcol = C_W // LANES
    ue, uo = _na_bias_tables(rpb)

    def main_spec(col0):
        return pl.BlockSpec((NA_QROWS, LANES), lambda b, h, r: (qbase + b * nrb + r, col0 + h))

    def side_spec(col0, off):
        return pl.BlockSpec((half, LANES),
                            lambda b, h, r: (hbase + b * 2 * nrb + jnp.clip(2 * r + off, 0, 2 * nrb - 1), col0 + h))

    ctx_spec = pl.BlockSpec((None, PAST, LANES), lambda b, h, r: (b, 0, h))
    tab_spec = pl.BlockSpec((2, GRID_W, NA_TABLE), lambda b, h, r: (h, 0, 0))
    return pl.pallas_call(
        _na_kernel,
        grid=(DEC_BATCH, npairs, nrb),
        in_specs=[main_spec(0),
                  side_spec(kcol, -1), main_spec(kcol), side_spec(kcol, 2),
                  side_spec(2 * kcol, -1), main_spec(2 * kcol), side_spec(2 * kcol, 2),
                  ctx_spec, ctx_spec, tab_spec, tab_spec],
        out_specs=pl.BlockSpec((NA_QROWS, LANES), lambda b, h, r: (b * nrb + r, h)),
        out_shape=jax.ShapeDtypeStruct((NS_TOK, C_W), F32),
        compiler_params=_cparams(3),
        name="na_attn",
    )(p, p, p, p, p, p, p, cache_k.reshape(DEC_BATCH, PAST, C_W), cache_v.reshape(DEC_BATCH, PAST, C_W), ue, uo)


def _route(biased, scores):
    t = biased.shape[1]
    per_group = N_EXPERTS // N_GROUPS
    i8 = lax.broadcasted_iota(jnp.int32, (per_group, t), 0)
    g_rows = []
    for g in range(N_GROUPS):
        bg = biased[g * per_group:(g + 1) * per_group]
        m1 = bg.max(axis=0, keepdims=True)
        first = jnp.where(bg == m1, i8, per_group).min(axis=0, keepdims=True)
        m2 = jnp.where(i8 == first, -jnp.inf, bg).max(axis=0, keepdims=True)
        g_rows.append(m1 + m2)
    g_top = jnp.concatenate(g_rows, axis=0)
    gi = lax.broadcasted_iota(jnp.int32, g_top.shape, 0)
    g_sel = jnp.zeros(g_top.shape, jnp.int32)
    cur = g_top
    for _ in range(TOPK_GROUPS):
        m = cur.max(axis=0, keepdims=True)
        hit = gi == jnp.where(cur == m, gi, N_GROUPS).min(axis=0, keepdims=True)
        g_sel = jnp.where(hit, 1, g_sel)
        cur = jnp.where(hit, -jnp.inf, cur)
    e_sel = jnp.concatenate([jnp.broadcast_to(g_sel[g:g + 1], (per_group, t)) for g in range(N_GROUPS)], axis=0)
    cur = jnp.where(e_sel > 0, biased, NEG)
    ei = lax.broadcasted_iota(jnp.int32, cur.shape, 0)
    ids, gates, hits = [], [], []
    for _ in range(TOP_K):
        m = cur.max(axis=0, keepdims=True)
        f = jnp.where(cur == m, ei, N_EXPERTS).min(axis=0, keepdims=True)
        hit = ei == f
        ids.append(f)
        hits.append(hit)
        gates.append(jnp.where(hit, scores, 0.0).sum(axis=0, keepdims=True))
        cur = jnp.where(hit, -jnp.inf, cur)
    gate = jnp.concatenate(gates, axis=0)
    gate = gate / gate.sum(axis=0, keepdims=True) * ROUTED_SCALE
    return jnp.concatenate(ids, axis=0), gate, hits


def _pack_bf16_pairs(h):
    bits = lax.bitcast_convert_type(h.astype(BF16).astype(F32), jnp.uint32)
    return bits[:, :D // 2] | (bits[:, D // 2:] >> 16)


def _unpack_bf16_pairs(xp):
    hi = lax.bitcast_convert_type(xp & jnp.uint32(0xFFFF0000), F32).astype(BF16)
    lo = lax.bitcast_convert_type(xp << 16, F32).astype(BF16)
    return hi, lo


def _dot_halves(hi, lo, w_ref):
    return (jnp.dot(hi, w_ref[:D // 2, :], preferred_element_type=F32)
            + jnp.dot(lo, w_ref[D // 2:, :], preferred_element_type=F32))


def _outproj_kernel(*refs, n_parts):
    xp_ref, xs_ref = refs[:2]
    part_refs = refs[2:2 + 3 * n_parts]
    gate_ref, shift_ref, scale_ref, g2_ref, rw_ref, rb_ref = refs[2 + 3 * n_parts:8 + 3 * n_parts]
    xo_ref, h_ref, dest_ref, wgt_ref, plan_ref, cnt_ref = refs[8 + 3 * n_parts:]
    step = pl.program_id(0)

    @pl.when(step == 0)
    def _():
        cnt_ref[...] = jnp.zeros_like(cnt_ref)
        plan_ref[...] = jnp.zeros_like(plan_ref)

    y = None
    for t in range(n_parts):
        ap_ref, as_ref, w_ref = part_refs[3 * t:3 * t + 3]
        d = jnp.dot(_pick_rows(ap_ref, as_ref).astype(BF16), w_ref[...], preferred_element_type=F32)
        y = d if y is None else y + d
    x = _pick_rows(xp_ref, xs_ref) + gate_ref[...] * y
    xo_ref[...] = x
    h = _rms(x, g2_ref[...]) * (1.0 + scale_ref[...]) + shift_ref[...]
    h_ref[...] = _pack_bf16_pairs(h)
    h_hi = h.astype(BF16)
    h_lo = (h - h_hi.astype(F32)).astype(BF16)
    rw = rw_ref[...]
    rw_hi = rw.astype(BF16)
    rw_lo = (rw - rw_hi.astype(F32)).astype(BF16)
    logits = (lax.dot_general(rw_hi, h_hi, NT_DIMS, preferred_element_type=F32)
              + lax.dot_general(rw_hi, h_lo, NT_DIMS, preferred_element_type=F32)
              + lax.dot_general(rw_lo, h_hi, NT_DIMS, preferred_element_type=F32))
    scores = jax.nn.sigmoid(logits)
    _, gate, hits = _route(scores + rb_ref[...], scores)
    wgt_ref[...] = gate
    chosen = hits[0]
    for hit in hits[1:]:
        chosen = chosen | hit
    m = jnp.where(chosen, 1.0, 0.0)
    before = (lax.broadcasted_iota(jnp.int32, (TM, TM), 0) < lax.broadcasted_iota(jnp.int32, (TM, TM), 1))
    prefix = jnp.dot(m.astype(BF16), jnp.where(before, 1.0, 0.0).astype(BF16), preferred_element_type=F32)
    e_base = (lax.broadcasted_iota(jnp.int32, (N_EXPERTS, 1), 0) * N_TOK).astype(F32)
    row_all = prefix + (cnt_ref[...] + e_base)
    dest_ref[...] = jnp.concatenate(
        [jnp.where(hit, row_all, 0.0).sum(axis=0, keepdims=True) for hit in hits], axis=0).astype(jnp.int32)
    cnt_ref[...] += m.sum(axis=1, keepdims=True)

    @pl.when(step == pl.num_programs(0) - 1)
    def _():
        _block_plan(cnt_ref[...], plan_ref)


def _block_plan(counts, plan_ref):
    cap_blocks = N_TOK // MOE_BLOCK
    nblk = ((counts.astype(jnp.int32) + (MOE_BLOCK - 1)) // MOE_BLOCK).astype(F32)
    lower = (lax.broadcasted_iota(jnp.int32, (N_EXPERTS, N_EXPERTS), 0)
             >= lax.broadcasted_iota(jnp.int32, (N_EXPERTS, N_EXPERTS), 1))
    cum = jnp.dot(jnp.where(lower, 1.0, 0.0).astype(BF16), jnp.broadcast_to(nblk, (N_EXPERTS, LANES)).astype(BF16),
                  preferred_element_type=F32)[:, :1]
    n_used = cum[N_EXPERTS - 1:, :]
    slot = jnp.minimum(lax.broadcasted_iota(jnp.int32, (1, PLAN_LANES), 1).astype(F32), n_used - 1.0)
    done = cum <= slot
    expert = jnp.where(done, 1.0, 0.0).sum(axis=0, keepdims=True)
    blocks_before = jnp.where(done, nblk, 0.0).sum(axis=0, keepdims=True)
    plan_ref[0:1, :] = (expert * cap_blocks + (slot - blocks_before)).astype(jnp.int32)
    plan_ref[1:2, :] = expert.astype(jnp.int32)
    plan_ref[2:3, :] = jnp.broadcast_to(n_used, (1, PLAN_LANES)).astype(jnp.int32)


def _outproj(x, parts, mod, g2, router_w, router_b):
    in_specs = _pair_specs(D)
    args = [x[0], x[1]]
    for ap, a_s, w in parts:
        width = ap.shape[1]
        in_specs += _pair_specs(width) + [pl.BlockSpec((width, D), lambda i: (0, 0))]
        args += [ap, a_s, w]
    in_specs += [_mod_spec(2), _mod_spec(3), _mod_spec(4),
                 pl.BlockSpec((1, D), lambda i: (0, 0)),
                 pl.BlockSpec((N_EXPERTS, D), lambda i: (0, 0)),
                 pl.BlockSpec((N_EXPERTS, 1), lambda i: (0, 0))]
    args += [mod, mod, mod, g2.reshape(1, D), router_w.T, router_b.reshape(N_EXPERTS, 1)]
    return pl.pallas_call(
        functools.partial(_outproj_kernel, n_parts=len(parts)),
        grid=(N_TOK // TM,),
        in_specs=in_specs,
        out_specs=[pl.BlockSpec((TM, D), lambda i: (i, 0)),
                   pl.BlockSpec((TM, D // 2), lambda i: (i, 0)),
                   pl.BlockSpec((TOP_K, TM), lambda i: (0, i)),
                   pl.BlockSpec((TOP_K, TM), lambda i: (0, i)),
                   pl.BlockSpec((8, PLAN_LANES), lambda i: (0, 0))],
        out_shape=[jax.ShapeDtypeStruct((N_TOK, D), F32),
                   jax.ShapeDtypeStruct((N_TOK, D // 2), jnp.uint32),
                   jax.ShapeDtypeStruct((TOP_K, N_TOK), jnp.int32),
                   jax.ShapeDtypeStruct((TOP_K, N_TOK), F32),
                   jax.ShapeDtypeStruct((8, PLAN_LANES), jnp.int32)],
        scratch_shapes=[pltpu.VMEM((N_EXPERTS, 1), F32)],
        compiler_params=_cparams(1),
        name="outproj_router",
    )(*args)


def _experts_kernel(br_ref, be_ref, nu_ref, x_hbm, w1_ref, w3_ref, w2_ref, o_ref, w1b, w3b, w2b, xbuf, xsem):
    i = pl.program_id(0)
    n_steps = pl.num_programs(0)
    e = be_ref[i]
    prev = be_ref[jnp.maximum(i - 1, 0)]

    def x_copy(step):
        slot = step % EXPERT_X_SLOTS
        rows = pl.ds(pl.multiple_of(br_ref[step] * MOE_BLOCK, MOE_BLOCK), MOE_BLOCK)
        return pltpu.make_async_copy(x_hbm.at[rows], xbuf.at[slot], xsem.at[slot])

    @pl.when(i == 0)
    def _():
        x_copy(0).start()
        x_copy(1).start()

    @pl.when(i + 2 < n_steps)
    def _():
        x_copy(i + 2).start()

    x_copy(i).wait()

    @pl.when((i == 0) | (e != prev))
    def _():
        w1b[...] = w1_ref[...].astype(BF16)
        w3b[...] = w3_ref[...].astype(BF16)
        w2b[...] = w2_ref[...].astype(BF16)

    @pl.when(i < nu_ref[0])
    def _():
        hi, lo = _unpack_bf16_pairs(xbuf[i % EXPERT_X_SLOTS])
        a = _dot_halves(hi, lo, w1b)
        b = _dot_halves(hi, lo, w3b)
        h = (_silu(a) * b).astype(BF16)
        o_ref[...] = _pack_bf16_pairs(jnp.dot(h, w2b[...], preferred_element_type=F32))


def _experts(plan, x_rows, w1, w3, w2, layer):
    return pl.pallas_call(
        _experts_kernel,
        grid_spec=pltpu.PrefetchScalarGridSpec(
            num_scalar_prefetch=3,
            grid=(N_MOE_BLOCKS,),
            in_specs=[pl.BlockSpec(memory_space=pl.ANY),
                      pl.BlockSpec((None, None, D, FF), lambda i, br, be, nu: (layer, be[i], 0, 0)),
                      pl.BlockSpec((None, None, D, FF), lambda i, br, be, nu: (layer, be[i], 0, 0)),
                      pl.BlockSpec((None, None, FF, D), lambda i, br, be, nu: (layer, be[i], 0, 0))],
            out_specs=pl.BlockSpec((MOE_BLOCK, D // 2), lambda i, br, be, nu: (br[i], 0)),
            scratch_shapes=[pltpu.VMEM((D, FF), BF16), pltpu.VMEM((D, FF), BF16), pltpu.VMEM((FF, D), BF16),
                            pltpu.VMEM((EXPERT_X_SLOTS, MOE_BLOCK, D // 2), jnp.uint32),
                            pltpu.SemaphoreType.DMA((EXPERT_X_SLOTS,))]),
        out_shape=jax.ShapeDtypeStruct(x_rows.shape, jnp.uint32),
        compiler_params=_cparams(1),
        name="experts",
    )(plan[0], plan[1], plan[2, :1], x_rows, w1, w3, w2)


SC_CORES = 2
SC_SUBCORES = 16
SC_WORKERS = SC_CORES * SC_SUBCORES
SC_CHUNK_BYTES = 64 * 1024
SC_SLOTS = 4


def _sc_scatter(rows, dest, n_out):
    n_rows, width = rows.shape
    picks = dest.shape[0]
    chunk = SC_CHUNK_BYTES // (4 * width)
    per_worker = n_rows // SC_WORKERS
    n_chunks = per_worker // chunk
    assert per_worker * SC_WORKERS == n_rows and n_chunks * chunk == per_worker and n_chunks % 2 == 0
    mesh = plsc.VectorSubcoreMesh(core_axis_name="c", subcore_axis_name="s")

    @functools.partial(
        pl.kernel, mesh=mesh,
        out_type=jax.ShapeDtypeStruct((n_out, width), rows.dtype),
        scratch_types=[pltpu.VMEM((picks, n_chunks, chunk), jnp.int32),
                       pltpu.VMEM((2, chunk, width), rows.dtype),
                       pltpu.SemaphoreType.DMA((2,)),
                       pltpu.SemaphoreType.DMA((2,))])
    def scatter(r_hbm, d_hbm, o_hbm, idx_v, rows_v, lsem, ssem):
        worker = lax.axis_index("s") * SC_CORES + lax.axis_index("c")
        base = worker * per_worker
        for k in range(picks):
            pltpu.sync_copy(d_hbm.at[k, worker], idx_v.at[k])

        def load_copy(c, b):
            src = pl.ds(pl.multiple_of(base + c * chunk, chunk), chunk)
            return pltpu.make_async_copy(r_hbm.at[src], rows_v.at[b], lsem.at[b])

        def store_copy(c, b, k):
            return pltpu.make_async_copy(rows_v.at[b], o_hbm.at[idx_v.at[k, c]], ssem.at[b])

        load_copy(0, 0).start()

        @pl.loop(0, n_chunks, step=2)
        def _(c0):
            for b in range(2):
                c = c0 + b
                load_copy(c, b).wait()
                for k in range(picks):
                    store_copy(c, b, k).start()

                @pl.when(c > 0)
                def _():
                    for k in range(picks):
                        store_copy(c - 1, 1 - b, k).wait()

                @pl.when(c + 1 < n_chunks)
                def _():
                    load_copy(c + 1, 1 - b).start()

        for k in range(picks):
            store_copy(n_chunks - 1, 1, k).wait()

    return scatter(rows, dest.reshape(picks, SC_WORKERS, n_chunks, chunk))


def _sc_gather(table, idx):
    n_idx = idx.shape[0]
    width = table.shape[1]
    chunk = SC_CHUNK_BYTES // (4 * width)
    per_worker = n_idx // SC_WORKERS
    n_chunks = per_worker // chunk
    ahead = SC_SLOTS - 1
    assert per_worker * SC_WORKERS == n_idx and n_chunks * chunk == per_worker and n_chunks % SC_SLOTS == 0
    mesh = plsc.VectorSubcoreMesh(core_axis_name="c", subcore_axis_name="s")

    @functools.partial(
        pl.kernel, mesh=mesh,
        out_type=jax.ShapeDtypeStruct((n_idx, width), table.dtype),
        scratch_types=[pltpu.VMEM((per_worker,), jnp.int32),
                       pltpu.VMEM((SC_SLOTS, chunk, width), table.dtype),
                       pltpu.SemaphoreType.DMA((SC_SLOTS,)),
                       pltpu.SemaphoreType.DMA((SC_SLOTS,))])
    def gather(t_hbm, i_hbm, o_hbm, idx_v, rows_v, gsem, wsem):
        worker = lax.axis_index("s") * SC_CORES + lax.axis_index("c")
        base = worker * per_worker
        pltpu.sync_copy(i_hbm.at[pl.ds(pl.multiple_of(base, chunk), per_worker)], idx_v)

        def gather_copy(c, b):
            ids = idx_v.at[pl.ds(pl.multiple_of(c * chunk, chunk), chunk)]
            return pltpu.make_async_copy(t_hbm.at[ids], rows_v.at[b], gsem.at[b])

        def write_copy(c, b):
            rows = pl.ds(pl.multiple_of(base + c * chunk, chunk), chunk)
            return pltpu.make_async_copy(rows_v.at[b], o_hbm.at[rows], wsem.at[b])

        for c in range(ahead):
            gather_copy(c, c).start()

        @pl.loop(0, n_chunks, step=SC_SLOTS)
        def _(c0):
            for b in range(SC_SLOTS):
                c = c0 + b
                refill = (b + ahead) % SC_SLOTS
                gather_copy(c, b).wait()
                write_copy(c, b).start()

                @pl.when(c > 0)
                def _():
                    write_copy(c - 1, refill).wait()

                @pl.when(c + ahead < n_chunks)
                def _():
                    gather_copy(c + ahead, refill).start()

        write_copy(n_chunks - 1, (n_chunks - 1) % SC_SLOTS).wait()

    return gather(table, idx)


TC = 512


def _combine_kernel(x_ref, h_ref, y_ref, wgt_ref, gate_ref, w1_ref, w3_ref, w2_ref, fg_ref, o_ref, *, final):
    hi, lo = _unpack_bf16_pairs(h_ref[...])
    a = _dot_halves(hi, lo, w1_ref)
    b = _dot_halves(hi, lo, w3_ref)
    ffn = jnp.dot((_silu(a) * b).astype(BF16), w2_ref[...], preferred_element_type=F32)
    wgt = wgt_ref[...]
    r_hi = None
    r_lo = None
    for k in range(TOP_K):
        yk = y_ref[k]
        w = wgt[:, k:k + 1]
        t_hi = lax.bitcast_convert_type(yk & jnp.uint32(0xFFFF0000), F32) * w
        t_lo = lax.bitcast_convert_type(yk << 16, F32) * w
        r_hi = t_hi if r_hi is None else r_hi + t_hi
        r_lo = t_lo if r_lo is None else r_lo + t_lo
    x = x_ref[...] + gate_ref[...] * (ffn + jnp.concatenate([r_hi, r_lo], axis=1))
    o_ref[...] = _rms(x, fg_ref[...]) if final else x


def _combine(x, h, y_rows, wgt, mod, sw1, sw3, sw2, final_g, final):
    weights = (sw1.astype(BF16), sw3.astype(BF16), sw2.astype(BF16), final_g.reshape(1, D))

    def rows_from(first_row, n_rows):
        b0 = first_row // TC
        return pl.pallas_call(
            functools.partial(_combine_kernel, final=final),
            grid=(n_rows // TC,),
            in_specs=[pl.BlockSpec((TC, D), lambda i: (i + b0, 0)),
                      pl.BlockSpec((TC, D // 2), lambda i: (i + b0, 0)),
                      pl.BlockSpec((TOP_K, TC, D // 2), lambda i: (0, i + b0, 0)),
                      pl.BlockSpec((TC, TOP_K), lambda i: (i + b0, 0)),
                      _mod_spec(5, TC, b0),
                      pl.BlockSpec((D, FF), lambda i: (0, 0)),
                      pl.BlockSpec((D, FF), lambda i: (0, 0)),
                      pl.BlockSpec((FF, D), lambda i: (0, 0)),
                      pl.BlockSpec((1, D), lambda i: (0, 0))],
            out_specs=pl.BlockSpec((TC, D), lambda i: (i, 0)),
            out_shape=jax.ShapeDtypeStruct((n_rows, D), F32),
            compiler_params=_cparams(1),
            name="combine",
        )(x, h, y_rows, wgt, mod, *weights)

    return rows_from(0, NP_TOK), rows_from(NP_TOK, NS_TOK)


def kernel(x_prompt, x_sample, cache_a_k, cache_a_v, state_ret_fwd, state_ret_bwd, cache_c_k, cache_c_v,
           c, c_ctx, norm1_g, norm2_g, ada_w, ada_b, even_w_in, even_w_out, sink_a, ret_decay_fwd,
           ret_decay_bwd, ret_gn_g, odd_w_in, odd_w_out, na_rpb, router_w, router_b, exp_w1, exp_w3,
           exp_w2, sh_w1, sh_w3, sh_w2, final_g):
    x = (x_prompt.reshape(NP_TOK, D), x_sample.reshape(NS_TOK, D))
    cc = jnp.concatenate([c_ctx[None], c, jnp.zeros((8 - 1 - DEC_BATCH, D), F32)], axis=0)
    rope = _rope_tables()
    outs = {}
    for l in range(2):
        mod = _ada(cc, ada_w, ada_b, l)
        if l == 0:
            p = _inproj(x, norm1_g[l], mod, even_w_in[0].astype(BF16), rope, A_Q + A_KV)
            oa_p = _ctx_gqa(p, sink_a[0])
            oa_s = _win_attention(p, cache_a_k[:, 0], cache_a_v[:, 0], sink_a[0])
            zero = jnp.zeros((BATCH, B_HEADS // 2, LANES, LANES), F32)
            ob_p, sf, sb = _retention(p, 0, BATCH, SEQ, ret_decay_fwd[0], ret_decay_bwd[0], ret_gn_g[0], zero, zero)
            ob_s, _, _ = _retention(p, NP_TOK, DEC_BATCH, DEC_SEQ, ret_decay_fwd[0], ret_decay_bwd[0], ret_gn_g[0],
                                    _blockdiag_states(state_ret_fwd[:, 0]), _blockdiag_states(state_ret_bwd[:, 0]))
            w_out = even_w_out[0].astype(BF16)
            parts = [(oa_p, oa_s, w_out[:A_Q]), (ob_p, ob_s, w_out[A_Q:])]
            outs["a_k"] = p[:NP_TOK, A_Q:A_Q + A_KV].reshape(BATCH, 1, SEQ, A_KV_HEADS, HD)
            outs["a_v"] = p[:NP_TOK, A_Q + A_KV:A_Q + 2 * A_KV].reshape(BATCH, 1, SEQ, A_KV_HEADS, HD)
            outs["r_f"] = _diag_states(sf).reshape(BATCH, 1, B_HEADS, HD, HD)
            outs["r_b"] = _diag_states(sb).reshape(BATCH, 1, B_HEADS, HD, HD)
        else:
            p = _inproj(x, norm1_g[l], mod, odd_w_in[0].astype(BF16), rope, 0)
            o_p = _ctx_mha(p)
            o_s = _na_attention(p, cache_c_k[:, 0], cache_c_v[:, 0], na_rpb[0])
            parts = [(o_p, o_s, odd_w_out[0].astype(BF16))]
            outs["c_k"] = p[:NP_TOK, C_W:2 * C_W].reshape(BATCH, 1, SEQ, C_HEADS, HD)
            outs["c_v"] = p[:NP_TOK, 2 * C_W:3 * C_W].reshape(BATCH, 1, SEQ, C_HEADS, HD)
        x_mid, h, dest, gate_t, plan = _outproj(x, parts, mod, norm2_g[l], router_w[l], router_b[l])
        y = _experts(plan, _sc_scatter(h, dest, N_EXPERTS * N_TOK), exp_w1, exp_w3, exp_w2, l)
        y_rows = _sc_gather(y, dest.reshape(N_ASSIGN)).reshape(TOP_K, N_TOK, D // 2)
        x = _combine(x_mid, h, y_rows, gate_t.T, mod, sh_w1[l], sh_w3[l], sh_w2[l], final_g, final=(l == 1))
    y_prompt = x[0].reshape(BATCH, SEQ, D)
    y_sample = x[1].reshape(DEC_BATCH, DEC_SEQ, D)
    return (y_prompt, y_sample, outs["a_k"], outs["a_v"], outs["r_f"], outs["r_b"], outs["c_k"], outs["c_v"])
```

```python
import functools

import jax
import jax.numpy as jnp
from jax import lax
from jax.experimental import pallas as pl
from jax.experimental.pallas import tpu as pltpu
from jax.experimental.pallas import tpu_sc as plsc

F32 = jnp.float32
BF16 = jnp.bfloat16
HIGHEST = lax.Precision.HIGHEST

D = 1024
BATCH = 32
SEQ = 256
DEC_BATCH = 4
DEC_SEQ = 4096
PAST = 256
GRID_W = 64
HD = 64
EPS = 1e-6
NEG = -1e30
ROPE_BASE = 10000.0
A_HEADS = 8
A_KV_HEADS = 2
A_Q = A_HEADS * HD
A_KV = A_KV_HEADS * HD
B_HEADS = 8
B_W = B_HEADS * HD
EVEN_IN = A_Q + 2 * A_KV + 4 * B_W
C_HEADS = 16
C_W = C_HEADS * HD
NA_KH = 8
NA_KW = 16
N_EXPERTS = 64
TOP_K = 8
N_GROUPS = 8
TOPK_GROUPS = 4
FF = 256
ROUTED_SCALE = 2.5
MOE_BLOCK = 1024
EXPERT_X_SLOTS = 3
RET_CHUNK = 256
RET_UNROLL = 4
RET_NORM_ROWS = 1024
A_WINDOW = 128

NP_TOK = BATCH * SEQ
NS_TOK = DEC_BATCH * DEC_SEQ
N_TOK = NP_TOK + NS_TOK
N_ASSIGN = N_TOK * TOP_K
N_MOE_BLOCKS = (N_ASSIGN + N_EXPERTS * (MOE_BLOCK - 1) + MOE_BLOCK - 1) // MOE_BLOCK
PLAN_LANES = 512
assert N_TOK % MOE_BLOCK == 0 and N_MOE_BLOCKS <= PLAN_LANES

LANES = 128
TM = 512
NA_ROWS = 8
V7X_VMEM_LIMIT = 56 * 1024 * 1024

NT_DIMS = (((1,), (1,)), ((), ()))


def _cparams(n_axes, vmem=V7X_VMEM_LIMIT):
    return pltpu.CompilerParams(dimension_semantics=("arbitrary",) * n_axes, vmem_limit_bytes=vmem)


def _seg_of_block(i, rows):
    row0 = i * rows
    return jnp.where(row0 < NP_TOK, 0, 1 + (row0 - NP_TOK) // DEC_SEQ)


def _mod_spec(chunk, rows=TM, first_block=0):
    return pl.BlockSpec((None, 1, D), lambda i: (_seg_of_block(i + first_block, rows), 0, chunk))


def _pair_specs(width, rows=TM):
    npb = NP_TOK // rows
    nsb = NS_TOK // rows
    return [pl.BlockSpec((rows, width), lambda i: (jnp.minimum(i, npb - 1), 0)),
            pl.BlockSpec((rows, width), lambda i: (jnp.clip(i - npb, 0, nsb - 1), 0))]


def _pick_rows(p_ref, s_ref, rows=TM):
    return jnp.where(pl.program_id(0) < NP_TOK // rows, p_ref[...], s_ref[...])


def _silu(x):
    return x * jax.nn.sigmoid(x)


def _rms(x, g):
    return x * lax.rsqrt(jnp.mean(x * x, axis=-1, keepdims=True) + EPS) * g


def _lane_lo():
    return lax.broadcasted_iota(jnp.int32, (1, LANES), 1) < HD


def _ada_kernel(c_ref, w_ref, b_ref, o_ref):
    a = _silu(c_ref[...])
    o_ref[...] = jnp.dot(a, w_ref[...], preferred_element_type=F32, precision=HIGHEST) + b_ref[...]


def _ada(cc, w, b, layer):
    tn = 1536
    out = pl.pallas_call(
        _ada_kernel,
        grid=(6 * D // tn,),
        in_specs=[pl.BlockSpec((8, D), lambda j: (0, 0)),
                  pl.BlockSpec((None, D, tn), lambda j: (layer, 0, j)),
                  pl.BlockSpec((None, 1, tn), lambda j: (layer, 0, j))],
        out_specs=pl.BlockSpec((8, tn), lambda j: (0, j)),
        out_shape=jax.ShapeDtypeStruct((8, 6 * D), F32),
        compiler_params=_cparams(1),
        name="ada",
    )(cc, w, b.reshape(b.shape[0], 1, 6 * D))
    return out.reshape(8, 1, 6 * D)


def _inproj_kernel(xp_ref, xs_ref, g_ref, shift_ref, scale_ref, w_ref, cos_ref, sin_ref, o_ref, *, rope_cols):
    h = _rms(_pick_rows(xp_ref, xs_ref), g_ref[...]) * (1.0 + scale_ref[...]) + shift_ref[...]
    o = jnp.dot(h.astype(BF16), w_ref[...], preferred_element_type=F32)
    if rope_cols:
        cos = cos_ref[...]
        sin = sin_ref[...]
        lane = lax.broadcasted_iota(jnp.int32, (1, LANES), 1)
        first = (lane % 32) < 16
        for c in range(rope_cols // LANES):
            oc = o[:, c * LANES:(c + 1) * LANES]
            partner = jnp.where(first, pltpu.roll(oc, LANES - 16, 1), pltpu.roll(oc, 16, 1))
            o_ref[:, c * LANES:(c + 1) * LANES] = oc * cos + partner * sin
        o_ref[:, rope_cols:] = o[:, rope_cols:]
    else:
        o_ref[...] = o


def _rope_tables():
    half = HD // 2
    inv = ROPE_BASE ** (-jnp.arange(0, half, 2, dtype=F32) / half)
    t = jnp.arange(DEC_SEQ)
    ang_r = (t // GRID_W).astype(F32)[:, None] * inv[None]
    ang_c = (t % GRID_W).astype(F32)[:, None] * inv[None]

    def head(fn_r, fn_c, sign):
        return jnp.concatenate([sign[0] * fn_r, sign[1] * fn_r, sign[0] * fn_c, sign[1] * fn_c], axis=-1)

    cos = head(jnp.cos(ang_r), jnp.cos(ang_c), (1.0, 1.0))
    sin = head(jnp.sin(ang_r), jnp.sin(ang_c), (-1.0, 1.0))
    cos = jnp.concatenate([jnp.ones((TM, HD), F32), cos], axis=0)
    sin = jnp.concatenate([jnp.zeros((TM, HD), F32), sin], axis=0)
    return jnp.tile(cos, (1, 2)), jnp.tile(sin, (1, 2))


def _inproj(x, g, mod, w_bf16, rope, rope_cols):
    n_out = w_bf16.shape[1]
    npb = NP_TOK // TM
    spb = DEC_SEQ // TM

    def rope_map(i):
        return (jnp.where(i < npb, 0, 1 + (i - npb) % spb), 0)

    return pl.pallas_call(
        functools.partial(_inproj_kernel, rope_cols=rope_cols),
        grid=(N_TOK // TM,),
        in_specs=_pair_specs(D) + [
                  pl.BlockSpec((1, D), lambda i: (0, 0)),
                  _mod_spec(0), _mod_spec(1),
                  pl.BlockSpec((D, n_out), lambda i: (0, 0)),
                  pl.BlockSpec((TM, LANES), rope_map),
                  pl.BlockSpec((TM, LANES), rope_map)],
        out_specs=pl.BlockSpec((TM, n_out), lambda i: (i, 0)),
        out_shape=jax.ShapeDtypeStruct((N_TOK, n_out), F32),
        compiler_params=_cparams(1),
        name="inproj",
    )(x[0], x[1], g.reshape(1, D), mod, mod, w_bf16, rope[0], rope[1])


def _softmax_av(s_list, v_list, sink=None):
    mx = s_list[0].max(axis=-1, keepdims=True)
    for s in s_list[1:]:
        mx = jnp.maximum(mx, s.max(axis=-1, keepdims=True))
    if sink is not None:
        mx = jnp.maximum(mx, sink)
    den = jnp.exp(sink - mx) if sink is not None else 0.0
    acc = None
    for s, v in zip(s_list, v_list):
        p = jnp.exp(s - mx)
        den = den + p.sum(axis=-1, keepdims=True)
        pv = jnp.dot(p.astype(BF16), v, preferred_element_type=F32)
        acc = pv if acc is None else acc + pv
    return acc / den


def _dup_half(x, j, lo):
    xr = pltpu.roll(x, HD, 1)
    return jnp.where(lo, x, xr) if j == 0 else jnp.where(lo, xr, x)


def _stack_heads(q_ref, heads, lo, scale):
    parts = []
    for h in heads:
        qp = q_ref[:, (h // 2) * LANES:(h // 2 + 1) * LANES]
        keep = lo if h % 2 == 0 else jnp.logical_not(lo)
        parts.append(jnp.where(keep, qp, 0.0) * scale)
    return jnp.concatenate(parts, axis=0).astype(BF16)


def _sink_column(sink_ref, heads, rows):
    return jnp.concatenate([jnp.full((rows, 1), sink_ref[h], F32) for h in heads], axis=0)


def _ctx_gqa_kernel(sink_ref, q_ref, k_ref, v_ref, o_ref):
    lo = _lane_lo()
    k = k_ref[...]
    v = v_ref[...]
    group = A_HEADS // A_KV_HEADS
    scores = []
    for j in range(A_KV_HEADS):
        q = _stack_heads(q_ref, list(range(group * j, group * (j + 1))), lo, HD ** -0.5)
        scores.append(lax.dot_general(q, _dup_half(k, j, lo).astype(BF16), NT_DIMS, preferred_element_type=F32))
    s = jnp.concatenate(scores, axis=0)
    sink = _sink_column(sink_ref, list(range(A_HEADS)), SEQ)
    mx = jnp.maximum(s.max(axis=-1, keepdims=True), sink)
    e = jnp.exp(s - mx)
    den = jnp.exp(sink - mx) + e.sum(axis=-1, keepdims=True)
    e = e.astype(BF16)
    rows_per_group = group * SEQ
    for j in range(A_KV_HEADS):
        rows = slice(j * rows_per_group, (j + 1) * rows_per_group)
        o = jnp.dot(e[rows], _dup_half(v, j, lo).astype(BF16), preferred_element_type=F32) / den[rows]
        for t in range(group // 2):
            pair = (group * j) // 2 + t
            o_ref[:, pair * LANES:(pair + 1) * LANES] = jnp.where(
                lo, o[(2 * t) * SEQ:(2 * t + 1) * SEQ], o[(2 * t + 1) * SEQ:(2 * t + 2) * SEQ])


def _ctx_gqa(p, sink):
    return pl.pallas_call(
        _ctx_gqa_kernel,
        grid_spec=pltpu.PrefetchScalarGridSpec(
            num_scalar_prefetch=1,
            grid=(BATCH,),
            in_specs=[pl.BlockSpec((SEQ, A_Q), lambda b, s: (b, 0)),
                      pl.BlockSpec((SEQ, A_KV), lambda b, s: (b, A_Q // A_KV)),
                      pl.BlockSpec((SEQ, A_KV), lambda b, s: (b, A_Q // A_KV + 1))],
            out_specs=pl.BlockSpec((SEQ, A_Q), lambda b, s: (b, 0))),
        out_shape=jax.ShapeDtypeStruct((NP_TOK, A_Q), F32),
        compiler_params=_cparams(1),
        name="ctx_gqa",
    )(sink, p, p, p)


def _win_kernel(sink_ref, q_ref, kp_ref, kc_ref, kn_ref, vp_ref, vc_ref, vn_ref, ck_ref, cv_ref, o_ref):
    i = pl.program_id(1)
    lo = _lane_lo()
    k = jnp.concatenate([kp_ref[...], kc_ref[...], kn_ref[...]], axis=0)
    v = jnp.concatenate([vp_ref[...], vc_ref[...], vn_ref[...]], axis=0)
    ck = ck_ref[...]
    cv = cv_ref[...]
    group = A_HEADS // A_KV_HEADS
    n_keys = WIN_Q + 2 * A_WINDOW
    qpos = i * WIN_Q + lax.broadcasted_iota(jnp.int32, (WIN_Q, n_keys), 0)
    kpos = i * WIN_Q - A_WINDOW + lax.broadcasted_iota(jnp.int32, (WIN_Q, n_keys), 1)
    valid = (jnp.abs(kpos - qpos) <= A_WINDOW) & (kpos >= 0) & (kpos < DEC_SEQ)
    valid = jnp.concatenate([valid] * group, axis=0)
    s_loc, s_ctx, values = [], [], []
    for j in range(A_KV_HEADS):
        heads = list(range(group * j, group * (j + 1)))
        kd = _dup_half(k, j, lo).astype(BF16)
        ckd = _dup_half(ck, j, lo).astype(BF16)
        values.append((_dup_half(v, j, lo).astype(BF16), _dup_half(cv, j, lo).astype(BF16)))
        q = _stack_heads(q_ref, heads, lo, HD ** -0.5)
        s_loc.append(jnp.where(valid, lax.dot_general(q, kd, NT_DIMS, preferred_element_type=F32), NEG))
        s_ctx.append(lax.dot_general(q, ckd, NT_DIMS, preferred_element_type=F32))
    s_loc = jnp.concatenate(s_loc, axis=0)
    s_ctx = jnp.concatenate(s_ctx, axis=0)
    sink = _sink_column(sink_ref, list(range(A_HEADS)), WIN_Q)
    mx = jnp.maximum(jnp.maximum(s_loc.max(axis=-1, keepdims=True), s_ctx.max(axis=-1, keepdims=True)), sink)
    p_loc = jnp.exp(s_loc - mx)
    p_ctx = jnp.exp(s_ctx - mx)
    den = p_loc.sum(axis=-1, keepdims=True) + p_ctx.sum(axis=-1, keepdims=True) + jnp.exp(sink - mx)
    p_loc = p_loc.astype(BF16)
    p_ctx = p_ctx.astype(BF16)
    rows_per_group = group * WIN_Q
    for j, (vd, cvd) in enumerate(values):
        rows = slice(j * rows_per_group, (j + 1) * rows_per_group)
        o = (jnp.dot(p_loc[rows], vd, preferred_element_type=F32)
             + jnp.dot(p_ctx[rows], cvd, preferred_element_type=F32)) / den[rows]
        for t in range(group // 2):
            pair = (group * j) // 2 + t
            o_ref[:, pair * LANES:(pair + 1) * LANES] = jnp.where(
                lo, o[(2 * t) * WIN_Q:(2 * t + 1) * WIN_Q], o[(2 * t + 1) * WIN_Q:(2 * t + 2) * WIN_Q])


WIN_Q = 256


def _win_attention(p, cache_k, cache_v, sink):
    nblk = DEC_SEQ // WIN_Q
    side = WIN_Q // A_WINDOW
    nside = DEC_SEQ // A_WINDOW
    base = NP_TOK // WIN_Q
    side_base = NP_TOK // A_WINDOW
    kcol = A_Q // A_KV

    def main_spec(col):
        return pl.BlockSpec((WIN_Q, A_KV), lambda b, i, s: (base + b * nblk + i, col))

    def side_spec(col, off):
        return pl.BlockSpec((A_WINDOW, A_KV),
                            lambda b, i, s: (side_base + b * nside + jnp.clip(side * i + off, 0, nside - 1), col))

    ctx_spec = pl.BlockSpec((None, PAST, A_KV), lambda b, i, s: (b, 0, 0))
    return pl.pallas_call(
        _win_kernel,
        grid_spec=pltpu.PrefetchScalarGridSpec(
            num_scalar_prefetch=1,
            grid=(DEC_BATCH, nblk),
            in_specs=[pl.BlockSpec((WIN_Q, A_Q), lambda b, i, s: (base + b * nblk + i, 0)),
                      side_spec(kcol, -1), main_spec(kcol), side_spec(kcol, side),
                      side_spec(kcol + 1, -1), main_spec(kcol + 1), side_spec(kcol + 1, side),
                      ctx_spec, ctx_spec],
            out_specs=pl.BlockSpec((WIN_Q, A_Q), lambda b, i, s: (b * nblk + i, 0))),
        out_shape=jax.ShapeDtypeStruct((NS_TOK, A_Q), F32),
        compiler_params=_cparams(2),
        name="win_attn",
    )(sink, p, p, p, p, p, p, p, cache_k.reshape(DEC_BATCH, PAST, A_KV), cache_v.reshape(DEC_BATCH, PAST, A_KV))


def _ret_kernel(df_ref, db_ref, q_ref, k_ref, v_ref, g_ref, gn_ref, s0f_ref, s0b_ref,
                o_ref, sf_ref, sb_ref, of_scr, ob_scr, *, length):
    c_len = RET_CHUNK
    n = length // c_len
    lo = _lane_lo()
    hi = jnp.logical_not(lo)
    row = lax.broadcasted_iota(jnp.int32, (c_len, c_len), 0)
    col = lax.broadcasted_iota(jnp.int32, (c_len, c_len), 1)
    rowp = lax.broadcasted_iota(jnp.int32, (LANES, LANES), 0)
    colp = lax.broadcasted_iota(jnp.int32, (LANES, LANES), 1)
    blockdiag = (rowp < HD) == (colp < HD)
    idx = lax.broadcasted_iota(jnp.int32, (c_len, 1), 0).astype(F32)

    def direction(dec_ref, forward):
        lg = -jnp.exp(dec_ref[...])
        diff = (row - col) if forward else (col - row)
        keep = (diff >= 0) if forward else (diff > 0)
        dist = jnp.maximum(diff, 0).astype(F32)
        dm = jnp.concatenate([jnp.where(keep, jnp.exp(dist * lg[:, off:off + 1]), 0.0) for off in (0, HD)], axis=0)
        if forward:
            xi = jnp.exp((idx + 1.0) * lg)
            zeta = jnp.exp((c_len - 1.0 - idx) * lg)
        else:
            xi = jnp.exp((c_len - idx) * lg)
            zeta = jnp.exp(idx * lg)
        return dm, xi, zeta, jnp.exp(c_len * lg)

    def chunk(c, state, consts):
        dm, xi, zeta, gch = consts
        rows = pl.ds(pl.multiple_of(c * c_len, c_len), c_len)
        qc = q_ref[rows, :]
        kc = k_ref[rows, :] * HD ** -0.5
        vc = v_ref[rows, :].astype(BF16)
        kb = kc.astype(BF16)
        q2 = jnp.concatenate([jnp.where(lo, qc, 0.0), jnp.where(hi, qc, 0.0)], axis=0).astype(BF16)
        inner = lax.dot_general(q2, kb, NT_DIMS, preferred_element_type=F32) * dm
        kz_t = (kc * zeta).T
        res = jnp.dot(jnp.concatenate([inner, kz_t], axis=0).astype(BF16), vc, preferred_element_type=F32)
        cross = jnp.dot(qc.astype(BF16), state.astype(BF16), preferred_element_type=F32) * xi
        o = jnp.where(lo, res[:c_len], res[c_len:2 * c_len]) + cross
        state = gch * state + jnp.where(blockdiag, res[2 * c_len:], 0.0)
        return rows, o, state

    cf = direction(df_ref, True)
    cb = direction(db_ref, False)

    def scan_body(t, states):
        rows_f, o_f, state_f = chunk(t, states[0], cf)
        of_scr[rows_f, :] = o_f
        rows_b, o_b, state_b = chunk(n - 1 - t, states[1], cb)
        ob_scr[rows_b, :] = o_b
        return state_f, state_b

    state_f, state_b = lax.fori_loop(0, n, scan_body, (s0f_ref[...], s0b_ref[...]), unroll=min(n, RET_UNROLL))
    sf_ref[...] = state_f
    sb_ref[...] = state_b

    gn = gn_ref[...]
    norm_rows = min(RET_NORM_ROWS, length)
    n_norm = length // norm_rows

    def per_head(x):
        a = jnp.where(lo, x, 0.0).sum(axis=-1, keepdims=True)
        b = jnp.where(hi, x, 0.0).sum(axis=-1, keepdims=True)
        return jnp.where(lo, a, b) * (1.0 / HD)

    def norm_body(t, carry):
        rows = pl.ds(pl.multiple_of(t * norm_rows, norm_rows), norm_rows)
        o = of_scr[rows, :] + ob_scr[rows, :]
        d = o - per_head(o)
        y = d * lax.rsqrt(per_head(d * d) + EPS) * gn
        o_ref[rows, :] = _silu(g_ref[rows, :]) * y
        return carry

    lax.fori_loop(0, n_norm, norm_body, 0)


def _pair_lanes(v):
    return jnp.repeat(v.astype(F32), HD).reshape(B_HEADS // 2, 1, LANES)


def _blockdiag_states(s):
    b = s.shape[0]
    s = s.astype(F32).reshape(b, B_HEADS // 2, 2, HD, HD)
    z = jnp.zeros_like(s[:, :, 0])
    top = jnp.concatenate([s[:, :, 0], z], axis=-1)
    bot = jnp.concatenate([z, s[:, :, 1]], axis=-1)
    return jnp.concatenate([top, bot], axis=-2)


def _diag_states(sp):
    b = sp.shape[0]
    s = jnp.stack([sp[:, :, :HD, :HD], sp[:, :, HD:, HD:]], axis=2)
    return s.reshape(b, B_HEADS, HD, HD)


def _retention(p, row_base, batch, length, dec_f, dec_b, gn_g, s0f, s0b):
    npairs = B_HEADS // 2
    blk0 = row_base // length
    qcol = (A_Q + 2 * A_KV) // LANES

    def col_spec(off):
        return pl.BlockSpec((length, LANES), lambda b, h: (blk0 + b, qcol + off * npairs + h))

    lane_spec = pl.BlockSpec((None, 1, LANES), lambda b, h: (h, 0, 0))
    state_spec = pl.BlockSpec((None, None, LANES, LANES), lambda b, h: (b, h, 0, 0))
    state_shape = jax.ShapeDtypeStruct((batch, npairs, LANES, LANES), F32)
    return pl.pallas_call(
        functools.partial(_ret_kernel, length=length),
        grid=(batch, npairs),
        in_specs=[lane_spec, lane_spec, col_spec(0), col_spec(1), col_spec(2), col_spec(3), lane_spec,
                  state_spec, state_spec],
        out_specs=[pl.BlockSpec((length, LANES), lambda b, h: (b, h)), state_spec, state_spec],
        out_shape=[jax.ShapeDtypeStruct((batch * length, B_W), F32), state_shape, state_shape],
        scratch_shapes=[pltpu.VMEM((length, LANES), F32), pltpu.VMEM((length, LANES), F32)],
        compiler_params=_cparams(2),
        name="retention",
    )(_pair_lanes(dec_f), _pair_lanes(dec_b), p, p, p, p, gn_g.reshape(npairs, 1, LANES), s0f, s0b)


def _ctx_mha_kernel(q_ref, k_ref, v_ref, o_ref):
    lo = _lane_lo()
    for pair in range(C_HEADS // 2):
        cols = slice(pair * LANES, (pair + 1) * LANES)
        q = _stack_heads(q_ref, [2 * pair, 2 * pair + 1], lo, HD ** -0.5)
        s = lax.dot_general(q, k_ref[:, cols].astype(BF16), NT_DIMS, preferred_element_type=F32)
        o = _softmax_av([s], [v_ref[:, cols].astype(BF16)])
        o_ref[:, cols] = jnp.where(lo, o[:SEQ], o[SEQ:])


def _ctx_mha(p):
    return pl.pallas_call(
        _ctx_mha_kernel,
        grid=(BATCH,),
        in_specs=[pl.BlockSpec((SEQ, C_W), lambda b: (b, 0)),
                  pl.BlockSpec((SEQ, C_W), lambda b: (b, 1)),
                  pl.BlockSpec((SEQ, C_W), lambda b: (b, 2))],
        out_specs=pl.BlockSpec((SEQ, C_W), lambda b: (b, 0)),
        out_shape=jax.ShapeDtypeStruct((NP_TOK, C_W), F32),
        compiler_params=_cparams(1),
        name="ctx_mha",
    )(p, p, p)


NA_WIN_ROWS = 2 * NA_ROWS
NA_WIN = NA_WIN_ROWS * GRID_W
NA_QROWS = NA_ROWS * GRID_W
NA_PAD_ROWS = NA_KH // 2
NA_TABLE = 1536


def _na_kernel(q_ref, kp_ref, km_ref, kn_ref, vp_ref, vm_ref, vn_ref, ck_ref, cv_ref, ue_ref, uo_ref, o_ref):
    r0 = pl.program_id(2) * NA_ROWS
    n_rows = DEC_SEQ // GRID_W
    lo = _lane_lo()
    k = jnp.concatenate([kp_ref[...], km_ref[...], kn_ref[...]], axis=0).astype(BF16)
    v = jnp.concatenate([vp_ref[...], vm_ref[...], vn_ref[...]], axis=0).astype(BF16)
    ck = ck_ref[...].astype(BF16)
    cv = cv_ref[...].astype(BF16)
    q = q_ref[...] * HD ** -0.5
    klane = lax.broadcasted_iota(jnp.int32, (1, NA_WIN), 1)
    outs = []
    for half, keep in enumerate((lo, jnp.logical_not(lo))):
        qh = jnp.where(keep, q, 0.0).astype(BF16)
        s = lax.dot_general(qh, k, NT_DIMS, preferred_element_type=F32)
        s_ctx = lax.dot_general(qh, ck, NT_DIMS, preferred_element_type=F32)
        p_loc, p_ctx, den = [], [], []
        for rq in range(NA_ROWS):
            rows = slice(rq * GRID_W, (rq + 1) * GRID_W)
            start = NA_KH - 1 - rq
            if start % 2 == 0:
                u = ue_ref[half, :, start * GRID_W:start * GRID_W + NA_WIN]
            else:
                u = uo_ref[half, :, (start - 1) * GRID_W:(start - 1) * GRID_W + NA_WIN]
            r = r0 + rq
            first = jnp.clip(r - NA_KH // 2, 0, n_rows - NA_KH)
            lane0 = (first - r0 + NA_PAD_ROWS) * GRID_W
            in_rows = (klane >= lane0) & (klane < lane0 + NA_KH * GRID_W)
            sl = jnp.where(in_rows, s[rows] + u, NEG)
            sc = s_ctx[rows]
            mx = jnp.maximum(sl.max(axis=-1, keepdims=True), sc.max(axis=-1, keepdims=True))
            el = jnp.exp(sl - mx)
            ec = jnp.exp(sc - mx)
            den.append(el.sum(axis=-1, keepdims=True) + ec.sum(axis=-1, keepdims=True))
            p_loc.append(el.astype(BF16))
            p_ctx.append(ec.astype(BF16))
        acc = (jnp.dot(jnp.concatenate(p_loc, axis=0), v, preferred_element_type=F32)
               + jnp.dot(jnp.concatenate(p_ctx, axis=0), cv, preferred_element_type=F32))
        outs.append(acc / jnp.concatenate(den, axis=0))
    o_ref[...] = jnp.where(lo, outs[0], outs[1])


def _na_bias_tables(rpb):
    cq = jnp.arange(GRID_W)
    ck = jnp.arange(GRID_W)
    dc = jnp.clip(ck[None] - cq[:, None], -(NA_KW - 1), NA_KW - 1) + NA_KW - 1
    cs = jnp.clip(cq - NA_KW // 2, 0, GRID_W - NA_KW)
    col_ok = (ck[None] >= cs[:, None]) & (ck[None] < cs[:, None] + NA_KW)
    t = rpb.astype(F32)[:, :, dc]
    t = jnp.where(col_ok[None, None], t, NEG).transpose(0, 2, 1, 3)
    n_dr = 2 * NA_KH - 1
    blocks = NA_TABLE // GRID_W
    t = jnp.pad(t, ((0, 0), (0, 0), (NA_PAD_ROWS, blocks - n_dr - NA_PAD_ROWS), (0, 0)), constant_values=NEG)
    ue = t.reshape(C_HEADS, GRID_W, NA_TABLE)
    uo = jnp.concatenate([ue[..., GRID_W:], jnp.full((C_HEADS, GRID_W, GRID_W), NEG, F32)], axis=-1)
    return ue, uo


def _na_attention(p, cache_k, cache_v, rpb):
    npairs = C_HEADS // 2
    nrb = DEC_SEQ // NA_QROWS
    half = NA_QROWS // 2
    qbase = NP_TOK // NA_QROWS
    hbase = NP_TOK // half
    kcol = C_W // LANES
    ue, uo = _na_bias_tables(rpb)

    def main_spec(col0):
        return pl.BlockSpec((NA_QROWS, LANES), lambda b, h, r: (qbase + b * nrb + r, col0 + h))

    def side_spec(col0, off):
        return pl.BlockSpec((half, LANES),
                            lambda b, h, r: (hbase + b * 2 * nrb + jnp.clip(2 * r + off, 0, 2 * nrb - 1), col0 + h))

    ctx_spec = pl.BlockSpec((None, PAST, LANES), lambda b, h, r: (b, 0, h))
    tab_spec = pl.BlockSpec((2, GRID_W, NA_TABLE), lambda b, h, r: (h, 0, 0))
    return pl.pallas_call(
        _na_kernel,
        grid=(DEC_BATCH, npairs, nrb),
        in_specs=[main_spec(0),
                  side_spec(kcol, -1), main_spec(kcol), side_spec(kcol, 2),
                  side_spec(2 * kcol, -1), main_spec(2 * kcol), side_spec(2 * kcol, 2),
                  ctx_spec, ctx_spec, tab_spec, tab_spec],
        out_specs=pl.BlockSpec((NA_QROWS, LANES), lambda b, h, r: (b * nrb + r, h)),
        out_shape=jax.ShapeDtypeStruct((NS_TOK, C_W), F32),
        compiler_params=_cparams(3),
        name="na_attn",
    )(p, p, p, p, p, p, p, cache_k.reshape(DEC_BATCH, PAST, C_W), cache_v.reshape(DEC_BATCH, PAST, C_W), ue, uo)


def _route(biased, scores):
    t = biased.shape[1]
    per_group = N_EXPERTS // N_GROUPS
    i8 = lax.broadcasted_iota(jnp.int32, (per_group, t), 0)
    g_rows = []
    for g in range(N_GROUPS):
        bg = biased[g * per_group:(g + 1) * per_group]
        m1 = bg.max(axis=0, keepdims=True)
        first = jnp.where(bg == m1, i8, per_group).min(axis=0, keepdims=True)
        m2 = jnp.where(i8 == first, -jnp.inf, bg).max(axis=0, keepdims=True)
        g_rows.append(m1 + m2)
    g_top = jnp.concatenate(g_rows, axis=0)
    gi = lax.broadcasted_iota(jnp.int32, g_top.shape, 0)
    g_sel = jnp.zeros(g_top.shape, jnp.int32)
    cur = g_top
    for _ in range(TOPK_GROUPS):
        m = cur.max(axis=0, keepdims=True)
        hit = gi == jnp.where(cur == m, gi, N_GROUPS).min(axis=0, keepdims=True)
        g_sel = jnp.where(hit, 1, g_sel)
        cur = jnp.where(hit, -jnp.inf, cur)
    e_sel = jnp.concatenate([jnp.broadcast_to(g_sel[g:g + 1], (per_group, t)) for g in range(N_GROUPS)], axis=0)
    cur = jnp.where(e_sel > 0, biased, NEG)
    ei = lax.broadcasted_iota(jnp.int32, cur.shape, 0)
    ids, gates, hits = [], [], []
    for _ in range(TOP_K):
        m = cur.max(axis=0, keepdims=True)
        f = jnp.where(cur == m, ei, N_EXPERTS).min(axis=0, keepdims=True)
        hit = ei == f
        ids.append(f)
        hits.append(hit)
        gates.append(jnp.where(hit, scores, 0.0).sum(axis=0, keepdims=True))
        cur = jnp.where(hit, -jnp.inf, cur)
    gate = jnp.concatenate(gates, axis=0)
    gate = gate / gate.sum(axis=0, keepdims=True) * ROUTED_SCALE
    return jnp.concatenate(ids, axis=0), gate, hits


def _pack_bf16_pairs(h):
    bits = lax.bitcast_convert_type(h.astype(BF16).astype(F32), jnp.uint32)
    return bits[:, :D // 2] | (bits[:, D // 2:] >> 16)


def _unpack_bf16_pairs(xp):
    hi = lax.bitcast_convert_type(xp & jnp.uint32(0xFFFF0000), F32).astype(BF16)
    lo = lax.bitcast_convert_type(xp << 16, F32).astype(BF16)
    return hi, lo


def _dot_halves(hi, lo, w_ref):
    return (jnp.dot(hi, w_ref[:D // 2, :], preferred_element_type=F32)
            + jnp.dot(lo, w_ref[D // 2:, :], preferred_element_type=F32))


def _outproj_kernel(*refs, n_parts):
    xp_ref, xs_ref = refs[:2]
    part_refs = refs[2:2 + 3 * n_parts]
    gate_ref, shift_ref, scale_ref, g2_ref, rw_ref, rb_ref = refs[2 + 3 * n_parts:8 + 3 * n_parts]
    xo_ref, h_ref, dest_ref, wgt_ref, plan_ref, cnt_ref = refs[8 + 3 * n_parts:]
    step = pl.program_id(0)

    @pl.when(step == 0)
    def _():
        cnt_ref[...] = jnp.zeros_like(cnt_ref)
        plan_ref[...] = jnp.zeros_like(plan_ref)

    y = None
    for t in range(n_parts):
        ap_ref, as_ref, w_ref = part_refs[3 * t:3 * t + 3]
        d = jnp.dot(_pick_rows(ap_ref, as_ref).astype(BF16), w_ref[...], preferred_element_type=F32)
        y = d if y is None else y + d
    x = _pick_rows(xp_ref, xs_ref) + gate_ref[...] * y
    xo_ref[...] = x
    h = _rms(x, g2_ref[...]) * (1.0 + scale_ref[...]) + shift_ref[...]
    h_ref[...] = _pack_bf16_pairs(h)
    h_hi = h.astype(BF16)
    h_lo = (h - h_hi.astype(F32)).astype(BF16)
    rw = rw_ref[...]
    rw_hi = rw.astype(BF16)
    rw_lo = (rw - rw_hi.astype(F32)).astype(BF16)
    logits = (lax.dot_general(rw_hi, h_hi, NT_DIMS, preferred_element_type=F32)
              + lax.dot_general(rw_hi, h_lo, NT_DIMS, preferred_element_type=F32)
              + lax.dot_general(rw_lo, h_hi, NT_DIMS, preferred_element_type=F32))
    scores = jax.nn.sigmoid(logits)
    _, gate, hits = _route(scores + rb_ref[...], scores)
    wgt_ref[...] = gate
    chosen = hits[0]
    for hit in hits[1:]:
        chosen = chosen | hit
    m = jnp.where(chosen, 1.0, 0.0)
    before = (lax.broadcasted_iota(jnp.int32, (TM, TM), 0) < lax.broadcasted_iota(jnp.int32, (TM, TM), 1))
    prefix = jnp.dot(m.astype(BF16), jnp.where(before, 1.0, 0.0).astype(BF16), preferred_element_type=F32)
    e_base = (lax.broadcasted_iota(jnp.int32, (N_EXPERTS, 1), 0) * N_TOK).astype(F32)
    row_all = prefix + (cnt_ref[...] + e_base)
    dest_ref[...] = jnp.concatenate(
        [jnp.where(hit, row_all, 0.0).sum(axis=0, keepdims=True) for hit in hits], axis=0).astype(jnp.int32)
    cnt_ref[...] += m.sum(axis=1, keepdims=True)

    @pl.when(step == pl.num_programs(0) - 1)
    def _():
        _block_plan(cnt_ref[...], plan_ref)


def _block_plan(counts, plan_ref):
    cap_blocks = N_TOK // MOE_BLOCK
    nblk = ((counts.astype(jnp.int32) + (MOE_BLOCK - 1)) // MOE_BLOCK).astype(F32)
    lower = (lax.broadcasted_iota(jnp.int32, (N_EXPERTS, N_EXPERTS), 0)
             >= lax.broadcasted_iota(jnp.int32, (N_EXPERTS, N_EXPERTS), 1))
    cum = jnp.dot(jnp.where(lower, 1.0, 0.0).astype(BF16), jnp.broadcast_to(nblk, (N_EXPERTS, LANES)).astype(BF16),
                  preferred_element_type=F32)[:, :1]
    n_used = cum[N_EXPERTS - 1:, :]
    slot = jnp.minimum(lax.broadcasted_iota(jnp.int32, (1, PLAN_LANES), 1).astype(F32), n_used - 1.0)
    done = cum <= slot
    expert = jnp.where(done, 1.0, 0.0).sum(axis=0, keepdims=True)
    blocks_before = jnp.where(done, nblk, 0.0).sum(axis=0, keepdims=True)
    plan_ref[0:1, :] = (expert * cap_blocks + (slot - blocks_before)).astype(jnp.int32)
    plan_ref[1:2, :] = expert.astype(jnp.int32)
    plan_ref[2:3, :] = jnp.broadcast_to(n_used, (1, PLAN_LANES)).astype(jnp.int32)
    run_end = jnp.where(cum > slot, cum, jnp.inf).min(axis=0, keepdims=True)
    plan_ref[3:4, :] = jnp.minimum(jnp.where(cum <= run_end, 1.0, 0.0).sum(axis=0, keepdims=True),
                                   N_EXPERTS - 1.0).astype(jnp.int32)
    plan_ref[4:5, :] = jnp.where(run_end < n_used, 1, 0).astype(jnp.int32)


def _outproj(x, parts, mod, g2, router_w, router_b):
    in_specs = _pair_specs(D)
    args = [x[0], x[1]]
    for ap, a_s, w in parts:
        width = ap.shape[1]
        in_specs += _pair_specs(width) + [pl.BlockSpec((width, D), lambda i: (0, 0))]
        args += [ap, a_s, w]
    in_specs += [_mod_spec(2), _mod_spec(3), _mod_spec(4),
                 pl.BlockSpec((1, D), lambda i: (0, 0)),
                 pl.BlockSpec((N_EXPERTS, D), lambda i: (0, 0)),
                 pl.BlockSpec((N_EXPERTS, 1), lambda i: (0, 0))]
    args += [mod, mod, mod, g2.reshape(1, D), router_w.T, router_b.reshape(N_EXPERTS, 1)]
    return pl.pallas_call(
        functools.partial(_outproj_kernel, n_parts=len(parts)),
        grid=(N_TOK // TM,),
        in_specs=in_specs,
        out_specs=[pl.BlockSpec((TM, D), lambda i: (i, 0)),
                   pl.BlockSpec((TM, D // 2), lambda i: (i, 0)),
                   pl.BlockSpec((TOP_K, TM), lambda i: (0, i)),
                   pl.BlockSpec((TOP_K, TM), lambda i: (0, i)),
                   pl.BlockSpec((8, PLAN_LANES), lambda i: (0, 0))],
        out_shape=[jax.ShapeDtypeStruct((N_TOK, D), F32),
                   jax.ShapeDtypeStruct((N_TOK, D // 2), jnp.uint32),
                   jax.ShapeDtypeStruct((TOP_K, N_TOK), jnp.int32),
                   jax.ShapeDtypeStruct((TOP_K, N_TOK), F32),
                   jax.ShapeDtypeStruct((8, PLAN_LANES), jnp.int32)],
        scratch_shapes=[pltpu.VMEM((N_EXPERTS, 1), F32)],
        compiler_params=_cparams(1),
        name="outproj_router",
    )(*args)


def _experts_kernel(br_ref, be_ref, nu_ref, ne_ref, hn_ref, x_hbm, w1_hbm, w3_hbm, w2_hbm, o_ref,
                    w1b, w3b, w2b, w1f, w3f, w2f, xbuf, xsem, wsem, *, layer):
    i = pl.program_id(0)
    n_steps = pl.num_programs(0)
    e = be_ref[i]
    prev = be_ref[jnp.maximum(i - 1, 0)]

    def x_copy(step):
        slot = step % EXPERT_X_SLOTS
        rows = pl.ds(pl.multiple_of(br_ref[step] * MOE_BLOCK, MOE_BLOCK), MOE_BLOCK)
        return pltpu.make_async_copy(x_hbm.at[rows], xbuf.at[slot], xsem.at[slot])

    @pl.when(i == 0)
    def _():
        x_copy(0).start()
        x_copy(1).start()

    @pl.when(i + 2 < n_steps)
    def _():
        x_copy(i + 2).start()

    x_copy(i).wait()

    def w_copies(expert):
        return [pltpu.make_async_copy(src.at[layer, expert], dst, wsem.at[k])
                for k, (src, dst) in enumerate(((w1_hbm, w1f), (w3_hbm, w3f), (w2_hbm, w2f)))]

    @pl.when(i == 0)
    def _():
        for c in w_copies(e):
            c.start()

    @pl.when((i == 0) | (e != prev))
    def _():
        for c in w_copies(e):
            c.wait()
        w1b[...] = w1f[...].astype(BF16)
        w3b[...] = w3f[...].astype(BF16)
        w2b[...] = w2f[...].astype(BF16)

        @pl.when(hn_ref[i] > 0)
        def _():
            for c in w_copies(ne_ref[i]):
                c.start()

    @pl.when(i < nu_ref[0])
    def _():
        hi, lo = _unpack_bf16_pairs(xbuf[i % EXPERT_X_SLOTS])
        a = _dot_halves(hi, lo, w1b)
        b = _dot_halves(hi, lo, w3b)
        h = (_silu(a) * b).astype(BF16)
        o_ref[...] = _pack_bf16_pairs(jnp.dot(h, w2b[...], preferred_element_type=F32))


def _experts(plan, x_rows, w1, w3, w2, layer):
    any_spec = pl.BlockSpec(memory_space=pl.ANY)
    return pl.pallas_call(
        functools.partial(_experts_kernel, layer=layer),
        grid_spec=pltpu.PrefetchScalarGridSpec(
            num_scalar_prefetch=5,
            grid=(N_MOE_BLOCKS,),
            in_specs=[any_spec, any_spec, any_spec, any_spec],
            out_specs=pl.BlockSpec((MOE_BLOCK, D // 2), lambda i, br, be, nu, ne, hn: (br[i], 0)),
            scratch_shapes=[pltpu.VMEM((D, FF), BF16), pltpu.VMEM((D, FF), BF16), pltpu.VMEM((FF, D), BF16),
                            pltpu.VMEM((D, FF), F32), pltpu.VMEM((D, FF), F32), pltpu.VMEM((FF, D), F32),
                            pltpu.VMEM((EXPERT_X_SLOTS, MOE_BLOCK, D // 2), jnp.uint32),
                            pltpu.SemaphoreType.DMA((EXPERT_X_SLOTS,)),
                            pltpu.SemaphoreType.DMA((3,))]),
        out_shape=jax.ShapeDtypeStruct(x_rows.shape, jnp.uint32),
        compiler_params=_cparams(1),
        name="experts",
    )(plan[0], plan[1], plan[2, :1], plan[3], plan[4], x_rows, w1, w3, w2)


SC_CORES = 2
SC_SUBCORES = 16
SC_WORKERS = SC_CORES * SC_SUBCORES
SC_CHUNK_BYTES = 64 * 1024
SC_SLOTS = 4


def _sc_scatter(rows, dest, n_out):
    n_rows, width = rows.shape
    picks = dest.shape[0]
    chunk = SC_CHUNK_BYTES // (4 * width)
    per_worker = n_rows // SC_WORKERS
    n_chunks = per_worker // chunk
    assert per_worker * SC_WORKERS == n_rows and n_chunks * chunk == per_worker and n_chunks % 2 == 0
    mesh = plsc.VectorSubcoreMesh(core_axis_name="c", subcore_axis_name="s")

    @functools.partial(
        pl.kernel, mesh=mesh,
        out_type=jax.ShapeDtypeStruct((n_out, width), rows.dtype),
        scratch_types=[pltpu.VMEM((picks, n_chunks, chunk), jnp.int32),
                       pltpu.VMEM((2, chunk, width), rows.dtype),
                       pltpu.SemaphoreType.DMA((2,)),
                       pltpu.SemaphoreType.DMA((2,))])
    def scatter(r_hbm, d_hbm, o_hbm, idx_v, rows_v, lsem, ssem):
        worker = lax.axis_index("s") * SC_CORES + lax.axis_index("c")
        base = worker * per_worker
        for k in range(picks):
            pltpu.sync_copy(d_hbm.at[k, worker], idx_v.at[k])

        def load_copy(c, b):
            src = pl.ds(pl.multiple_of(base + c * chunk, chunk), chunk)
            return pltpu.make_async_copy(r_hbm.at[src], rows_v.at[b], lsem.at[b])

        def store_copy(c, b, k):
            return pltpu.make_async_copy(rows_v.at[b], o_hbm.at[idx_v.at[k, c]], ssem.at[b])

        load_copy(0, 0).start()

        @pl.loop(0, n_chunks, step=2)
        def _(c0):
            for b in range(2):
                c = c0 + b
                load_copy(c, b).wait()
                for k in range(picks):
                    store_copy(c, b, k).start()

                @pl.when(c > 0)
                def _():
                    for k in range(picks):
                        store_copy(c - 1, 1 - b, k).wait()

                @pl.when(c + 1 < n_chunks)
                def _():
                    load_copy(c + 1, 1 - b).start()

        for k in range(picks):
            store_copy(n_chunks - 1, 1, k).wait()

    return scatter(rows, dest.reshape(picks, SC_WORKERS, n_chunks, chunk))


def _sc_gather(table, idx):
    n_idx = idx.shape[0]
    width = table.shape[1]
    chunk = SC_CHUNK_BYTES // (4 * width)
    per_worker = n_idx // SC_WORKERS
    n_chunks = per_worker // chunk
    ahead = SC_SLOTS - 1
    assert per_worker * SC_WORKERS == n_idx and n_chunks * chunk == per_worker and n_chunks % SC_SLOTS == 0
    mesh = plsc.VectorSubcoreMesh(core_axis_name="c", subcore_axis_name="s")

    @functools.partial(
        pl.kernel, mesh=mesh,
        out_type=jax.ShapeDtypeStruct((n_idx, width), table.dtype),
        scratch_types=[pltpu.VMEM((per_worker,), jnp.int32),
                       pltpu.VMEM((SC_SLOTS, chunk, width), table.dtype),
                       pltpu.SemaphoreType.DMA((SC_SLOTS,)),
                       pltpu.SemaphoreType.DMA((SC_SLOTS,))])
    def gather(t_hbm, i_hbm, o_hbm, idx_v, rows_v, gsem, wsem):
        worker = lax.axis_index("s") * SC_CORES + lax.axis_index("c")
        base = worker * per_worker
        pltpu.sync_copy(i_hbm.at[pl.ds(pl.multiple_of(base, chunk), per_worker)], idx_v)

        def gather_copy(c, b):
            ids = idx_v.at[pl.ds(pl.multiple_of(c * chunk, chunk), chunk)]
            return pltpu.make_async_copy(t_hbm.at[ids], rows_v.at[b], gsem.at[b])

        def write_copy(c, b):
            rows = pl.ds(pl.multiple_of(base + c * chunk, chunk), chunk)
            return pltpu.make_async_copy(rows_v.at[b], o_hbm.at[rows], wsem.at[b])

        for c in range(ahead):
            gather_copy(c, c).start()

        @pl.loop(0, n_chunks, step=SC_SLOTS)
        def _(c0):
            for b in range(SC_SLOTS):
                c = c0 + b
                refill = (b + ahead) % SC_SLOTS
                gather_copy(c, b).wait()
                write_copy(c, b).start()

                @pl.when(c > 0)
                def _():
                    write_copy(c - 1, refill).wait()

                @pl.when(c + ahead < n_chunks)
                def _():
                    gather_copy(c + ahead, refill).start()

        write_copy(n_chunks - 1, (n_chunks - 1) % SC_SLOTS).wait()

    return gather(table, idx)


TC = 512


def _combine_kernel(x_ref, h_ref, y_ref, wgt_ref, gate_ref, w1_ref, w3_ref, w2_ref, fg_ref, o_ref, *, final):
    hi, lo = _unpack_bf16_pairs(h_ref[...])
    a = _dot_halves(hi, lo, w1_ref)
    b = _dot_halves(hi, lo, w3_ref)
    ffn = jnp.dot((_silu(a) * b).astype(BF16), w2_ref[...], preferred_element_type=F32)
    wgt = wgt_ref[...]
    r_hi = None
    r_lo = None
    for k in range(TOP_K):
        yk = y_ref[k]
        w = wgt[:, k:k + 1]
        t_hi = lax.bitcast_convert_type(yk & jnp.uint32(0xFFFF0000), F32) * w
        t_lo = lax.bitcast_convert_type(yk << 16, F32) * w
        r_hi = t_hi if r_hi is None else r_hi + t_hi
        r_lo = t_lo if r_lo is None else r_lo + t_lo
    x = x_ref[...] + gate_ref[...] * (ffn + jnp.concatenate([r_hi, r_lo], axis=1))
    o_ref[...] = _rms(x, fg_ref[...]) if final else x


def _combine(x, h, y_rows, wgt, mod, sw1, sw3, sw2, final_g, final):
    weights = (sw1.astype(BF16), sw3.astype(BF16), sw2.astype(BF16), final_g.reshape(1, D))

    def rows_from(first_row, n_rows):
        b0 = first_row // TC
        return pl.pallas_call(
            functools.partial(_combine_kernel, final=final),
            grid=(n_rows // TC,),
            in_specs=[pl.BlockSpec((TC, D), lambda i: (i + b0, 0)),
                      pl.BlockSpec((TC, D // 2), lambda i: (i + b0, 0)),
                      pl.BlockSpec((TOP_K, TC, D // 2), lambda i: (0, i + b0, 0)),
                      pl.BlockSpec((TC, TOP_K), lambda i: (i + b0, 0)),
                      _mod_spec(5, TC, b0),
                      pl.BlockSpec((D, FF), lambda i: (0, 0)),
                      pl.BlockSpec((D, FF), lambda i: (0, 0)),
                      pl.BlockSpec((FF, D), lambda i: (0, 0)),
                      pl.BlockSpec((1, D), lambda i: (0, 0))],
            out_specs=pl.BlockSpec((TC, D), lambda i: (i, 0)),
            out_shape=jax.ShapeDtypeStruct((n_rows, D), F32),
            compiler_params=_cparams(1),
            name="combine",
        )(x, h, y_rows, wgt, mod, *weights)

    return rows_from(0, NP_TOK), rows_from(NP_TOK, NS_TOK)


def kernel(x_prompt, x_sample, cache_a_k, cache_a_v, state_ret_fwd, state_ret_bwd, cache_c_k, cache_c_v,
           c, c_ctx, norm1_g, norm2_g, ada_w, ada_b, even_w_in, even_w_out, sink_a, ret_decay_fwd,
           ret_decay_bwd, ret_gn_g, odd_w_in, odd_w_out, na_rpb, router_w, router_b, exp_w1, exp_w3,
           exp_w2, sh_w1, sh_w3, sh_w2, final_g):
    x = (x_prompt.reshape(NP_TOK, D), x_sample.reshape(NS_TOK, D))
    cc = jnp.concatenate([c_ctx[None], c, jnp.zeros((8 - 1 - DEC_BATCH, D), F32)], axis=0)
    rope = _rope_tables()
    outs = {}
    for l in range(2):
        mod = _ada(cc, ada_w, ada_b, l)
        if l == 0:
            p = _inproj(x, norm1_g[l], mod, even_w_in[0].astype(BF16), rope, A_Q + A_KV)
            oa_p = _ctx_gqa(p, sink_a[0])
            oa_s = _win_attention(p, cache_a_k[:, 0], cache_a_v[:, 0], sink_a[0])
            zero = jnp.zeros((BATCH, B_HEADS // 2, LANES, LANES), F32)
            ob_p, sf, sb = _retention(p, 0, BATCH, SEQ, ret_decay_fwd[0], ret_decay_bwd[0], ret_gn_g[0], zero, zero)
            ob_s, _, _ = _retention(p, NP_TOK, DEC_BATCH, DEC_SEQ, ret_decay_fwd[0], ret_decay_bwd[0], ret_gn_g[0],
                                    _blockdiag_states(state_ret_fwd[:, 0]), _blockdiag_states(state_ret_bwd[:, 0]))
            w_out = even_w_out[0].astype(BF16)
            parts = [(oa_p, oa_s, w_out[:A_Q]), (ob_p, ob_s, w_out[A_Q:])]
            outs["a_k"] = p[:NP_TOK, A_Q:A_Q + A_KV].reshape(BATCH, 1, SEQ, A_KV_HEADS, HD)
            outs["a_v"] = p[:NP_TOK, A_Q + A_KV:A_Q + 2 * A_KV].reshape(BATCH, 1, SEQ, A_KV_HEADS, HD)
            outs["r_f"] = _diag_states(sf).reshape(BATCH, 1, B_HEADS, HD, HD)
            outs["r_b"] = _diag_states(sb).reshape(BATCH, 1, B_HEADS, HD, HD)
        else:
            p = _inproj(x, norm1_g[l], mod, odd_w_in[0].astype(BF16), rope, 0)
            o_p = _ctx_mha(p)
            o_s = _na_attention(p, cache_c_k[:, 0], cache_c_v[:, 0], na_rpb[0])
            parts = [(o_p, o_s, odd_w_out[0].astype(BF16))]
            outs["c_k"] = p[:NP_TOK, C_W:2 * C_W].reshape(BATCH, 1, SEQ, C_HEADS, HD)
            outs["c_v"] = p[:NP_TOK, 2 * C_W:3 * C_W].reshape(BATCH, 1, SEQ, C_HEADS, HD)
        x_mid, h, dest, gate_t, plan = _outproj(x, parts, mod, norm2_g[l], router_w[l], router_b[l])
        y = _experts(plan, _sc_scatter(h, dest, N_EXPERTS * N_TOK), exp_w1, exp_w3, exp_w2, l)
        y_rows = _sc_gather(y, dest.reshape(N_ASSIGN)).reshape(TOP_K, N_TOK, D // 2)
        x = _combine(x_mid, h, y_rows, gate_t.T, mod, sh_w1[l], sh_w3[l], sh_w2[l], final_g, final=(l == 1))
    y_prompt = x[0].reshape(BATCH, SEQ, D)
    y_sample = x[1].reshape(DEC_BATCH, DEC_SEQ, D)
    return (y_prompt, y_sample, outs["a_k"], outs["a_v"], outs["r_f"], outs["r_b"], outs["c_k"], outs["c_v"])
```

```python
import functools

import jax
import jax.numpy as jnp
from jax import lax
from jax.experimental import pallas as pl
from jax.experimental.pallas import tpu as pltpu
from jax.experimental.pallas import tpu_sc as plsc

F32 = jnp.float32
BF16 = jnp.bfloat16
HIGHEST = lax.Precision.HIGHEST

D = 1024
BATCH = 32
SEQ = 256
DEC_BATCH = 4
DEC_SEQ = 4096
PAST = 256
GRID_W = 64
HD = 64
EPS = 1e-6
NEG = -1e30
ROPE_BASE = 10000.0
A_HEADS = 8
A_KV_HEADS = 2
A_Q = A_HEADS * HD
A_KV = A_KV_HEADS * HD
B_HEADS = 8
B_W = B_HEADS * HD
EVEN_IN = A_Q + 2 * A_KV + 4 * B_W
C_HEADS = 16
C_W = C_HEADS * HD
NA_KH = 8
NA_KW = 16
N_EXPERTS = 64
TOP_K = 8
N_GROUPS = 8
TOPK_GROUPS = 4
FF = 256
ROUTED_SCALE = 2.5
MOE_BLOCK = 512
EXPERT_X_SLOTS = 3
RET_CHUNK = 256
RET_UNROLL = 4
RET_NORM_ROWS = 1024
A_WINDOW = 128

NP_TOK = BATCH * SEQ
NS_TOK = DEC_BATCH * DEC_SEQ
N_TOK = NP_TOK + NS_TOK
N_ASSIGN = N_TOK * TOP_K
N_MOE_BLOCKS = (N_ASSIGN + N_EXPERTS * (MOE_BLOCK - 1) + MOE_BLOCK - 1) // MOE_BLOCK
PLAN_LANES = 512
assert N_TOK % MOE_BLOCK == 0 and N_MOE_BLOCKS <= PLAN_LANES

LANES = 128
TM = 512
NA_ROWS = 8
V7X_VMEM_LIMIT = 56 * 1024 * 1024

NT_DIMS = (((1,), (1,)), ((), ()))


def _cparams(n_axes, vmem=V7X_VMEM_LIMIT):
    return pltpu.CompilerParams(dimension_semantics=("arbitrary",) * n_axes, vmem_limit_bytes=vmem)


def _seg_of_block(i, rows):
    row0 = i * rows
    return jnp.where(row0 < NP_TOK, 0, 1 + (row0 - NP_TOK) // DEC_SEQ)


def _mod_spec(chunk, rows=TM, first_block=0):
    return pl.BlockSpec((None, 1, D), lambda i: (_seg_of_block(i + first_block, rows), 0, chunk))


def _pair_specs(width, rows=TM):
    npb = NP_TOK // rows
    nsb = NS_TOK // rows
    return [pl.BlockSpec((rows, width), lambda i: (jnp.minimum(i, npb - 1), 0)),
            pl.BlockSpec((rows, width), lambda i: (jnp.clip(i - npb, 0, nsb - 1), 0))]


def _pick_rows(p_ref, s_ref, rows=TM):
    return jnp.where(pl.program_id(0) < NP_TOK // rows, p_ref[...], s_ref[...])


def _silu(x):
    return x * jax.nn.sigmoid(x)


def _rms(x, g):
    return x * lax.rsqrt(jnp.mean(x * x, axis=-1, keepdims=True) + EPS) * g


def _lane_lo():
    return lax.broadcasted_iota(jnp.int32, (1, LANES), 1) < HD


def _ada_kernel(c_ref, w_ref, b_ref, o_ref):
    a = _silu(c_ref[...])
    o_ref[...] = jnp.dot(a, w_ref[...], preferred_element_type=F32, precision=HIGHEST) + b_ref[...]


def _ada(cc, w, b, layer):
    tn = 1536
    out = pl.pallas_call(
        _ada_kernel,
        grid=(6 * D // tn,),
        in_specs=[pl.BlockSpec((8, D), lambda j: (0, 0)),
                  pl.BlockSpec((None, D, tn), lambda j: (layer, 0, j)),
                  pl.BlockSpec((None, 1, tn), lambda j: (layer, 0, j))],
        out_specs=pl.BlockSpec((8, tn), lambda j: (0, j)),
        out_shape=jax.ShapeDtypeStruct((8, 6 * D), F32),
        compiler_params=_cparams(1),
        name="ada",
    )(cc, w, b.reshape(b.shape[0], 1, 6 * D))
    return out.reshape(8, 1, 6 * D)


def _inproj_kernel(xp_ref, xs_ref, g_ref, shift_ref, scale_ref, w_ref, cos_ref, sin_ref, o_ref, *, rope_cols):
    h = _rms(_pick_rows(xp_ref, xs_ref), g_ref[...]) * (1.0 + scale_ref[...]) + shift_ref[...]
    o = jnp.dot(h.astype(BF16), w_ref[...], preferred_element_type=F32)
    if rope_cols:
        cos = cos_ref[...]
        sin = sin_ref[...]
        lane = lax.broadcasted_iota(jnp.int32, (1, LANES), 1)
        first = (lane % 32) < 16
        for c in range(rope_cols // LANES):
            oc = o[:, c * LANES:(c + 1) * LANES]
            partner = jnp.where(first, pltpu.roll(oc, LANES - 16, 1), pltpu.roll(oc, 16, 1))
            o_ref[:, c * LANES:(c + 1) * LANES] = oc * cos + partner * sin
        o_ref[:, rope_cols:] = o[:, rope_cols:]
    else:
        o_ref[...] = o


def _rope_tables():
    half = HD // 2
    inv = ROPE_BASE ** (-jnp.arange(0, half, 2, dtype=F32) / half)
    t = jnp.arange(DEC_SEQ)
    ang_r = (t // GRID_W).astype(F32)[:, None] * inv[None]
    ang_c = (t % GRID_W).astype(F32)[:, None] * inv[None]

    def head(fn_r, fn_c, sign):
        return jnp.concatenate([sign[0] * fn_r, sign[1] * fn_r, sign[0] * fn_c, sign[1] * fn_c], axis=-1)

    cos = head(jnp.cos(ang_r), jnp.cos(ang_c), (1.0, 1.0))
    sin = head(jnp.sin(ang_r), jnp.sin(ang_c), (-1.0, 1.0))
    cos = jnp.concatenate([jnp.ones((TM, HD), F32), cos], axis=0)
    sin = jnp.concatenate([jnp.zeros((TM, HD), F32), sin], axis=0)
    return jnp.tile(cos, (1, 2)), jnp.tile(sin, (1, 2))


def _inproj(x, g, mod, w_bf16, rope, rope_cols):
    n_out = w_bf16.shape[1]
    npb = NP_TOK // TM
    spb = DEC_SEQ // TM

    def rope_map(i):
        return (jnp.where(i < npb, 0, 1 + (i - npb) % spb), 0)

    return pl.pallas_call(
        functools.partial(_inproj_kernel, rope_cols=rope_cols),
        grid=(N_TOK // TM,),
        in_specs=_pair_specs(D) + [
                  pl.BlockSpec((1, D), lambda i: (0, 0)),
                  _mod_spec(0), _mod_spec(1),
                  pl.BlockSpec((D, n_out), lambda i: (0, 0)),
                  pl.BlockSpec((TM, LANES), rope_map),
                  pl.BlockSpec((TM, LANES), rope_map)],
        out_specs=pl.BlockSpec((TM, n_out), lambda i: (i, 0)),
        out_shape=jax.ShapeDtypeStruct((N_TOK, n_out), F32),
        compiler_params=_cparams(1),
        name="inproj",
    )(x[0], x[1], g.reshape(1, D), mod, mod, w_bf16, rope[0], rope[1])


def _softmax_av(s_list, v_list, sink=None):
    mx = s_list[0].max(axis=-1, keepdims=True)
    for s in s_list[1:]:
        mx = jnp.maximum(mx, s.max(axis=-1, keepdims=True))
    if sink is not None:
        mx = jnp.maximum(mx, sink)
    den = jnp.exp(sink - mx) if sink is not None else 0.0
    acc = None
    for s, v in zip(s_list, v_list):
        p = jnp.exp(s - mx)
        den = den + p.sum(axis=-1, keepdims=True)
        pv = jnp.dot(p.astype(BF16), v, preferred_element_type=F32)
        acc = pv if acc is None else acc + pv
    return acc / den


def _dup_half(x, j, lo):
    xr = pltpu.roll(x, HD, 1)
    return jnp.where(lo, x, xr) if j == 0 else jnp.where(lo, xr, x)


def _stack_heads(q_ref, heads, lo, scale):
    parts = []
    for h in heads:
        qp = q_ref[:, (h // 2) * LANES:(h // 2 + 1) * LANES]
        keep = lo if h % 2 == 0 else jnp.logical_not(lo)
        parts.append(jnp.where(keep, qp, 0.0) * scale)
    return jnp.concatenate(parts, axis=0).astype(BF16)


def _sink_column(sink_ref, heads, rows):
    return jnp.concatenate([jnp.full((rows, 1), sink_ref[h], F32) for h in heads], axis=0)


def _ctx_gqa_kernel(sink_ref, q_ref, k_ref, v_ref, o_ref):
    lo = _lane_lo()
    k = k_ref[...]
    v = v_ref[...]
    group = A_HEADS // A_KV_HEADS
    scores = []
    for j in range(A_KV_HEADS):
        q = _stack_heads(q_ref, list(range(group * j, group * (j + 1))), lo, HD ** -0.5)
        scores.append(lax.dot_general(q, _dup_half(k, j, lo).astype(BF16), NT_DIMS, preferred_element_type=F32))
    s = jnp.concatenate(scores, axis=0)
    sink = _sink_column(sink_ref, list(range(A_HEADS)), SEQ)
    mx = jnp.maximum(s.max(axis=-1, keepdims=True), sink)
    e = jnp.exp(s - mx)
    den = jnp.exp(sink - mx) + e.sum(axis=-1, keepdims=True)
    e = e.astype(BF16)
    rows_per_group = group * SEQ
    for j in range(A_KV_HEADS):
        rows = slice(j * rows_per_group, (j + 1) * rows_per_group)
        o = jnp.dot(e[rows], _dup_half(v, j, lo).astype(BF16), preferred_element_type=F32) / den[rows]
        for t in range(group // 2):
            pair = (group * j) // 2 + t
            o_ref[:, pair * LANES:(pair + 1) * LANES] = jnp.where(
                lo, o[(2 * t) * SEQ:(2 * t + 1) * SEQ], o[(2 * t + 1) * SEQ:(2 * t + 2) * SEQ])


def _ctx_gqa(p, sink):
    return pl.pallas_call(
        _ctx_gqa_kernel,
        grid_spec=pltpu.PrefetchScalarGridSpec(
            num_scalar_prefetch=1,
            grid=(BATCH,),
            in_specs=[pl.BlockSpec((SEQ, A_Q), lambda b, s: (b, 0)),
                      pl.BlockSpec((SEQ, A_KV), lambda b, s: (b, A_Q // A_KV)),
                      pl.BlockSpec((SEQ, A_KV), lambda b, s: (b, A_Q // A_KV + 1))],
            out_specs=pl.BlockSpec((SEQ, A_Q), lambda b, s: (b, 0))),
        out_shape=jax.ShapeDtypeStruct((NP_TOK, A_Q), F32),
        compiler_params=_cparams(1),
        name="ctx_gqa",
    )(sink, p, p, p)


def _win_kernel(sink_ref, q_ref, kp_ref, kc_ref, kn_ref, vp_ref, vc_ref, vn_ref, ck_ref, cv_ref, o_ref):
    i = pl.program_id(1)
    lo = _lane_lo()
    k = jnp.concatenate([kp_ref[...], kc_ref[...], kn_ref[...]], axis=0)
    v = jnp.concatenate([vp_ref[...], vc_ref[...], vn_ref[...]], axis=0)
    ck = ck_ref[...]
    cv = cv_ref[...]
    group = A_HEADS // A_KV_HEADS
    n_keys = WIN_Q + 2 * A_WINDOW
    qpos = i * WIN_Q + lax.broadcasted_iota(jnp.int32, (WIN_Q, n_keys), 0)
    kpos = i * WIN_Q - A_WINDOW + lax.broadcasted_iota(jnp.int32, (WIN_Q, n_keys), 1)
    valid = (jnp.abs(kpos - qpos) <= A_WINDOW) & (kpos >= 0) & (kpos < DEC_SEQ)
    valid = jnp.concatenate([valid] * group, axis=0)
    s_loc, s_ctx, values = [], [], []
    for j in range(A_KV_HEADS):
        heads = list(range(group * j, group * (j + 1)))
        kd = _dup_half(k, j, lo).astype(BF16)
        ckd = _dup_half(ck, j, lo).astype(BF16)
        values.append((_dup_half(v, j, lo).astype(BF16), _dup_half(cv, j, lo).astype(BF16)))
        q = _stack_heads(q_ref, heads, lo, HD ** -0.5)
        s_loc.append(jnp.where(valid, lax.dot_general(q, kd, NT_DIMS, preferred_element_type=F32), NEG))
        s_ctx.append(lax.dot_general(q, ckd, NT_DIMS, preferred_element_type=F32))
    s_loc = jnp.concatenate(s_loc, axis=0)
    s_ctx = jnp.concatenate(s_ctx, axis=0)
    sink = _sink_column(sink_ref, list(range(A_HEADS)), WIN_Q)
    mx = jnp.maximum(jnp.maximum(s_loc.max(axis=-1, keepdims=True), s_ctx.max(axis=-1, keepdims=True)), sink)
    p_loc = jnp.exp(s_loc - mx)
    p_ctx = jnp.exp(s_ctx - mx)
    den = p_loc.sum(axis=-1, keepdims=True) + p_ctx.sum(axis=-1, keepdims=True) + jnp.exp(sink - mx)
    p_loc = p_loc.astype(BF16)
    p_ctx = p_ctx.astype(BF16)
    rows_per_group = group * WIN_Q
    for j, (vd, cvd) in enumerate(values):
        rows = slice(j * rows_per_group, (j + 1) * rows_per_group)
        o = (jnp.dot(p_loc[rows], vd, preferred_element_type=F32)
             + jnp.dot(p_ctx[rows], cvd, preferred_element_type=F32)) / den[rows]
        for t in range(group // 2):
            pair = (group * j) // 2 + t
            o_ref[:, pair * LANES:(pair + 1) * LANES] = jnp.where(
                lo, o[(2 * t) * WIN_Q:(2 * t + 1) * WIN_Q], o[(2 * t + 1) * WIN_Q:(2 * t + 2) * WIN_Q])


WIN_Q = 256


def _win_attention(p, cache_k, cache_v, sink):
    nblk = DEC_SEQ // WIN_Q
    side = WIN_Q // A_WINDOW
    nside = DEC_SEQ // A_WINDOW
    base = NP_TOK // WIN_Q
    side_base = NP_TOK // A_WINDOW
    kcol = A_Q // A_KV

    def main_spec(col):
        return pl.BlockSpec((WIN_Q, A_KV), lambda b, i, s: (base + b * nblk + i, col))

    def side_spec(col, off):
        return pl.BlockSpec((A_WINDOW, A_KV),
                            lambda b, i, s: (side_base + b * nside + jnp.clip(side * i + off, 0, nside - 1), col))

    ctx_spec = pl.BlockSpec((None, PAST, A_KV), lambda b, i, s: (b, 0, 0))
    return pl.pallas_call(
        _win_kernel,
        grid_spec=pltpu.PrefetchScalarGridSpec(
            num_scalar_prefetch=1,
            grid=(DEC_BATCH, nblk),
            in_specs=[pl.BlockSpec((WIN_Q, A_Q), lambda b, i, s: (base + b * nblk + i, 0)),
                      side_spec(kcol, -1), main_spec(kcol), side_spec(kcol, side),
                      side_spec(kcol + 1, -1), main_spec(kcol + 1), side_spec(kcol + 1, side),
                      ctx_spec, ctx_spec],
            out_specs=pl.BlockSpec((WIN_Q, A_Q), lambda b, i, s: (b * nblk + i, 0))),
        out_shape=jax.ShapeDtypeStruct((NS_TOK, A_Q), F32),
        compiler_params=_cparams(2),
        name="win_attn",
    )(sink, p, p, p, p, p, p, p, cache_k.reshape(DEC_BATCH, PAST, A_KV), cache_v.reshape(DEC_BATCH, PAST, A_KV))


def _ret_kernel(df_ref, db_ref, q_ref, k_ref, v_ref, g_ref, gn_ref, s0f_ref, s0b_ref,
                o_ref, sf_ref, sb_ref, of_scr, ob_scr, *, length):
    c_len = RET_CHUNK
    n = length // c_len
    lo = _lane_lo()
    hi = jnp.logical_not(lo)
    row = lax.broadcasted_iota(jnp.int32, (c_len, c_len), 0)
    col = lax.broadcasted_iota(jnp.int32, (c_len, c_len), 1)
    rowp = lax.broadcasted_iota(jnp.int32, (LANES, LANES), 0)
    colp = lax.broadcasted_iota(jnp.int32, (LANES, LANES), 1)
    blockdiag = (rowp < HD) == (colp < HD)
    idx = lax.broadcasted_iota(jnp.int32, (c_len, 1), 0).astype(F32)

    def direction(dec_ref, forward):
        lg = -jnp.exp(dec_ref[...])
        diff = (row - col) if forward else (col - row)
        keep = (diff >= 0) if forward else (diff > 0)
        dist = jnp.maximum(diff, 0).astype(F32)
        dm = jnp.concatenate([jnp.where(keep, jnp.exp(dist * lg[:, off:off + 1]), 0.0) for off in (0, HD)], axis=0)
        if forward:
            xi = jnp.exp((idx + 1.0) * lg)
            zeta = jnp.exp((c_len - 1.0 - idx) * lg)
        else:
            xi = jnp.exp((c_len - idx) * lg)
            zeta = jnp.exp(idx * lg)
        return dm, xi, zeta, jnp.exp(c_len * lg)

    def chunk(c, state, consts):
        dm, xi, zeta, gch = consts
        rows = pl.ds(pl.multiple_of(c * c_len, c_len), c_len)
        qc = q_ref[rows, :]
        kc = k_ref[rows, :] * HD ** -0.5
        vc = v_ref[rows, :].astype(BF16)
        kb = kc.astype(BF16)
        q2 = jnp.concatenate([jnp.where(lo, qc, 0.0), jnp.where(hi, qc, 0.0)], axis=0).astype(BF16)
        inner = lax.dot_general(q2, kb, NT_DIMS, preferred_element_type=F32) * dm
        kz_t = (kc * zeta).T
        res = jnp.dot(jnp.concatenate([inner, kz_t], axis=0).astype(BF16), vc, preferred_element_type=F32)
        cross = jnp.dot(qc.astype(BF16), state.astype(BF16), preferred_element_type=F32) * xi
        o = jnp.where(lo, res[:c_len], res[c_len:2 * c_len]) + cross
        state = gch * state + jnp.where(blockdiag, res[2 * c_len:], 0.0)
        return rows, o, state

    cf = direction(df_ref, True)
    cb = direction(db_ref, False)

    def scan_body(t, states):
        rows_f, o_f, state_f = chunk(t, states[0], cf)
        of_scr[rows_f, :] = o_f
        rows_b, o_b, state_b = chunk(n - 1 - t, states[1], cb)
        ob_scr[rows_b, :] = o_b
        return state_f, state_b

    state_f, state_b = lax.fori_loop(0, n, scan_body, (s0f_ref[...], s0b_ref[...]), unroll=min(n, RET_UNROLL))
    sf_ref[...] = state_f
    sb_ref[...] = state_b

    gn = gn_ref[...]
    norm_rows = min(RET_NORM_ROWS, length)
    n_norm = length // norm_rows

    def per_head(x):
        a = jnp.where(lo, x, 0.0).sum(axis=-1, keepdims=True)
        b = jnp.where(hi, x, 0.0).sum(axis=-1, keepdims=True)
        return jnp.where(lo, a, b) * (1.0 / HD)

    def norm_body(t, carry):
        rows = pl.ds(pl.multiple_of(t * norm_rows, norm_rows), norm_rows)
        o = of_scr[rows, :] + ob_scr[rows, :]
        d = o - per_head(o)
        y = d * lax.rsqrt(per_head(d * d) + EPS) * gn
        o_ref[rows, :] = _silu(g_ref[rows, :]) * y
        return carry

    lax.fori_loop(0, n_norm, norm_body, 0)


def _pair_lanes(v):
    return jnp.repeat(v.astype(F32), HD).reshape(B_HEADS // 2, 1, LANES)


def _blockdiag_states(s):
    b = s.shape[0]
    s = s.astype(F32).reshape(b, B_HEADS // 2, 2, HD, HD)
    z = jnp.zeros_like(s[:, :, 0])
    top = jnp.concatenate([s[:, :, 0], z], axis=-1)
    bot = jnp.concatenate([z, s[:, :, 1]], axis=-1)
    return jnp.concatenate([top, bot], axis=-2)


def _diag_states(sp):
    b = sp.shape[0]
    s = jnp.stack([sp[:, :, :HD, :HD], sp[:, :, HD:, HD:]], axis=2)
    return s.reshape(b, B_HEADS, HD, HD)


def _retention(p, row_base, batch, length, dec_f, dec_b, gn_g, s0f, s0b):
    npairs = B_HEADS // 2
    blk0 = row_base // length
    qcol = (A_Q + 2 * A_KV) // LANES

    def col_spec(off):
        return pl.BlockSpec((length, LANES), lambda b, h: (blk0 + b, qcol + off * npairs + h))

    lane_spec = pl.BlockSpec((None, 1, LANES), lambda b, h: (h, 0, 0))
    state_spec = pl.BlockSpec((None, None, LANES, LANES), lambda b, h: (b, h, 0, 0))
    state_shape = jax.ShapeDtypeStruct((batch, npairs, LANES, LANES), F32)
    return pl.pallas_call(
        functools.partial(_ret_kernel, length=length),
        grid=(batch, npairs),
        in_specs=[lane_spec, lane_spec, col_spec(0), col_spec(1), col_spec(2), col_spec(3), lane_spec,
                  state_spec, state_spec],
        out_specs=[pl.BlockSpec((length, LANES), lambda b, h: (b, h)), state_spec, state_spec],
        out_shape=[jax.ShapeDtypeStruct((batch * length, B_W), F32), state_shape, state_shape],
        scratch_shapes=[pltpu.VMEM((length, LANES), F32), pltpu.VMEM((length, LANES), F32)],
        compiler_params=_cparams(2),
        name="retention",
    )(_pair_lanes(dec_f), _pair_lanes(dec_b), p, p, p, p, gn_g.reshape(npairs, 1, LANES), s0f, s0b)


def _ctx_mha_kernel(q_ref, k_ref, v_ref, o_ref):
    lo = _lane_lo()
    for pair in range(C_HEADS // 2):
        cols = slice(pair * LANES, (pair + 1) * LANES)
        q = _stack_heads(q_ref, [2 * pair, 2 * pair + 1], lo, HD ** -0.5)
        s = lax.dot_general(q, k_ref[:, cols].astype(BF16), NT_DIMS, preferred_element_type=F32)
        o = _softmax_av([s], [v_ref[:, cols].astype(BF16)])
        o_ref[:, cols] = jnp.where(lo, o[:SEQ], o[SEQ:])


def _ctx_mha(p):
    return pl.pallas_call(
        _ctx_mha_kernel,
        grid=(BATCH,),
        in_specs=[pl.BlockSpec((SEQ, C_W), lambda b: (b, 0)),
                  pl.BlockSpec((SEQ, C_W), lambda b: (b, 1)),
                  pl.BlockSpec((SEQ, C_W), lambda b: (b, 2))],
        out_specs=pl.BlockSpec((SEQ, C_W), lambda b: (b, 0)),
        out_shape=jax.ShapeDtypeStruct((NP_TOK, C_W), F32),
        compiler_params=_cparams(1),
        name="ctx_mha",
    )(p, p, p)


NA_WIN_ROWS = 2 * NA_ROWS
NA_WIN = NA_WIN_ROWS * GRID_W
NA_QROWS = NA_ROWS * GRID_W
NA_PAD_ROWS = NA_KH // 2
NA_TABLE = 1536


def _na_kernel(q_ref, kp_ref, km_ref, kn_ref, vp_ref, vm_ref, vn_ref, ck_ref, cv_ref, ue_ref, uo_ref, o_ref):
    r0 = pl.program_id(2) * NA_ROWS
    n_rows = DEC_SEQ // GRID_W
    lo = _lane_lo()
    k = jnp.concatenate([kp_ref[...], km_ref[...], kn_ref[...]], axis=0).astype(BF16)
    v = jnp.concatenate([vp_ref[...], vm_ref[...], vn_ref[...]], axis=0).astype(BF16)
    ck = ck_ref[...].astype(BF16)
    cv = cv_ref[...].astype(BF16)
    q = q_ref[...] * HD ** -0.5
    klane = lax.broadcasted_iota(jnp.int32, (1, NA_WIN), 1)
    outs = []
    for half, keep in enumerate((lo, jnp.logical_not(lo))):
        qh = jnp.where(keep, q, 0.0).astype(BF16)
        s = lax.dot_general(qh, k, NT_DIMS, preferred_element_type=F32)
        s_ctx = lax.dot_general(qh, ck, NT_DIMS, preferred_element_type=F32)
        p_loc, p_ctx, den = [], [], []
        for rq in range(NA_ROWS):
            rows = slice(rq * GRID_W, (rq + 1) * GRID_W)
            start = NA_KH - 1 - rq
            if start % 2 == 0:
                u = ue_ref[half, :, start * GRID_W:start * GRID_W + NA_WIN]
            else:
                u = uo_ref[half, :, (start - 1) * GRID_W:(start - 1) * GRID_W + NA_WIN]
            r = r0 + rq
            first = jnp.clip(r - NA_KH // 2, 0, n_rows - NA_KH)
            lane0 = (first - r0 + NA_PAD_ROWS) * GRID_W
            in_rows = (klane >= lane0) & (klane < lane0 + NA_KH * GRID_W)
            sl = jnp.where(in_rows, s[rows] + u, NEG)
            sc = s_ctx[rows]
            mx = jnp.maximum(sl.max(axis=-1, keepdims=True), sc.max(axis=-1, keepdims=True))
            el = jnp.exp(sl - mx)
            ec = jnp.exp(sc - mx)
            den.append(el.sum(axis=-1, keepdims=True) + ec.sum(axis=-1, keepdims=True))
            p_loc.append(el.astype(BF16))
            p_ctx.append(ec.astype(BF16))
        acc = (jnp.dot(jnp.concatenate(p_loc, axis=0), v, preferred_element_type=F32)
               + jnp.dot(jnp.concatenate(p_ctx, axis=0), cv, preferred_element_type=F32))
        outs.append(acc / jnp.concatenate(den, axis=0))
    o_ref[...] = jnp.where(lo, outs[0], outs[1])


def _na_bias_tables(rpb):
    cq = jnp.arange(GRID_W)
    ck = jnp.arange(GRID_W)
    dc = jnp.clip(ck[None] - cq[:, None], -(NA_KW - 1), NA_KW - 1) + NA_KW - 1
    cs = jnp.clip(cq - NA_KW // 2, 0, GRID_W - NA_KW)
    col_ok = (ck[None] >= cs[:, None]) & (ck[None] < cs[:, None] + NA_KW)
    t = rpb.astype(F32)[:, :, dc]
    t = jnp.where(col_ok[None, None], t, NEG).transpose(0, 2, 1, 3)
    n_dr = 2 * NA_KH - 1
    blocks = NA_TABLE // GRID_W
    t = jnp.pad(t, ((0, 0), (0, 0), (NA_PAD_ROWS, blocks - n_dr - NA_PAD_ROWS), (0, 0)), constant_values=NEG)
    ue = t.reshape(C_HEADS, GRID_W, NA_TABLE)
    uo = jnp.concatenate([ue[..., GRID_W:], jnp.full((C_HEADS, GRID_W, GRID_W), NEG, F32)], axis=-1)
    return ue, uo


def _na_attention(p, cache_k, cache_v, rpb):
    npairs = C_HEADS // 2
    nrb = DEC_SEQ // NA_QROWS
    half = NA_QROWS // 2
    qbase = NP_TOK // NA_QROWS
    hbase = NP_TOK // half
    kcol = C_W // LANES
    ue, uo = _na_bias_tables(rpb)

    def main_spec(col0):
        return pl.BlockSpec((NA_QROWS, LANES), lambda b, h, r: (qbase + b * nrb + r, col0 + h))

    def side_spec(col0, off):
        return pl.BlockSpec((half, LANES),
                            lambda b, h, r: (hbase + b * 2 * nrb + jnp.clip(2 * r + off, 0, 2 * nrb - 1), col0 + h))

    ctx_spec = pl.BlockSpec((None, PAST, LANES), lambda b, h, r: (b, 0, h))
    tab_spec = pl.BlockSpec((2, GRID_W, NA_TABLE), lambda b, h, r: (h, 0, 0))
    return pl.pallas_call(
        _na_kernel,
        grid=(DEC_BATCH, npairs, nrb),
        in_specs=[main_spec(0),
                  side_spec(kcol, -1), main_spec(kcol), side_spec(kcol, 2),
                  side_spec(2 * kcol, -1), main_spec(2 * kcol), side_spec(2 * kcol, 2),
                  ctx_spec, ctx_spec, tab_spec, tab_spec],
        out_specs=pl.BlockSpec((NA_QROWS, LANES), lambda b, h, r: (b * nrb + r, h)),
        out_shape=jax.ShapeDtypeStruct((NS_TOK, C_W), F32),
        compiler_params=_cparams(3),
        name="na_attn",
    )(p, p, p, p, p, p, p, cache_k.reshape(DEC_BATCH, PAST, C_W), cache_v.reshape(DEC_BATCH, PAST, C_W), ue, uo)


def _route(biased, scores):
    t = biased.shape[1]
    per_group = N_EXPERTS // N_GROUPS
    i8 = lax.broadcasted_iota(jnp.int32, (per_group, t), 0)
    g_rows = []
    for g in range(N_GROUPS):
        bg = biased[g * per_group:(g + 1) * per_group]
        m1 = bg.max(axis=0, keepdims=True)
        first = jnp.where(bg == m1, i8, per_group).min(axis=0, keepdims=True)
        m2 = jnp.where(i8 == first, -jnp.inf, bg).max(axis=0, keepdims=True)
        g_rows.append(m1 + m2)
    g_top = jnp.concatenate(g_rows, axis=0)
    gi = lax.broadcasted_iota(jnp.int32, g_top.shape, 0)
    g_sel = jnp.zeros(g_top.shape, jnp.int32)
    cur = g_top
    for _ in range(TOPK_GROUPS):
        m = cur.max(axis=0, keepdims=True)
        hit = gi == jnp.where(cur == m, gi, N_GROUPS).min(axis=0, keepdims=True)
        g_sel = jnp.where(hit, 1, g_sel)
        cur = jnp.where(hit, -jnp.inf, cur)
    e_sel = jnp.concatenate([jnp.broadcast_to(g_sel[g:g + 1], (per_group, t)) for g in range(N_GROUPS)], axis=0)
    cur = jnp.where(e_sel > 0, biased, NEG)
    ei = lax.broadcasted_iota(jnp.int32, cur.shape, 0)
    ids, gates, hits = [], [], []
    for _ in range(TOP_K):
        m = cur.max(axis=0, keepdims=True)
        f = jnp.where(cur == m, ei, N_EXPERTS).min(axis=0, keepdims=True)
        hit = ei == f
        ids.append(f)
        hits.append(hit)
        gates.append(jnp.where(hit, scores, 0.0).sum(axis=0, keepdims=True))
        cur = jnp.where(hit, -jnp.inf, cur)
    gate = jnp.concatenate(gates, axis=0)
    gate = gate / gate.sum(axis=0, keepdims=True) * ROUTED_SCALE
    return jnp.concatenate(ids, axis=0), gate, hits


def _pack_bf16_pairs(h):
    bits = lax.bitcast_convert_type(h.astype(BF16).astype(F32), jnp.uint32)
    return bits[:, :D // 2] | (bits[:, D // 2:] >> 16)


def _unpack_bf16_pairs(xp):
    hi = lax.bitcast_convert_type(xp & jnp.uint32(0xFFFF0000), F32).astype(BF16)
    lo = lax.bitcast_convert_type(xp << 16, F32).astype(BF16)
    return hi, lo


def _dot_halves(hi, lo, w_ref):
    return (jnp.dot(hi, w_ref[:D // 2, :], preferred_element_type=F32)
            + jnp.dot(lo, w_ref[D // 2:, :], preferred_element_type=F32))


def _outproj_kernel(*refs, n_parts):
    xp_ref, xs_ref = refs[:2]
    part_refs = refs[2:2 + 3 * n_parts]
    gate_ref, shift_ref, scale_ref, g2_ref, rw_ref, rb_ref = refs[2 + 3 * n_parts:8 + 3 * n_parts]
    xo_ref, h_ref, dest_ref, wgt_ref, plan_ref, cnt_ref = refs[8 + 3 * n_parts:]
    step = pl.program_id(0)

    @pl.when(step == 0)
    def _():
        cnt_ref[...] = jnp.zeros_like(cnt_ref)
        plan_ref[...] = jnp.zeros_like(plan_ref)

    y = None
    for t in range(n_parts):
        ap_ref, as_ref, w_ref = part_refs[3 * t:3 * t + 3]
        d = jnp.dot(_pick_rows(ap_ref, as_ref).astype(BF16), w_ref[...], preferred_element_type=F32)
        y = d if y is None else y + d
    x = _pick_rows(xp_ref, xs_ref) + gate_ref[...] * y
    xo_ref[...] = x
    h = _rms(x, g2_ref[...]) * (1.0 + scale_ref[...]) + shift_ref[...]
    h_ref[...] = _pack_bf16_pairs(h)
    h_hi = h.astype(BF16)
    h_lo = (h - h_hi.astype(F32)).astype(BF16)
    rw = rw_ref[...]
    rw_hi = rw.astype(BF16)
    rw_lo = (rw - rw_hi.astype(F32)).astype(BF16)
    logits = (lax.dot_general(rw_hi, h_hi, NT_DIMS, preferred_element_type=F32)
              + lax.dot_general(rw_hi, h_lo, NT_DIMS, preferred_element_type=F32)
              + lax.dot_general(rw_lo, h_hi, NT_DIMS, preferred_element_type=F32))
    scores = jax.nn.sigmoid(logits)
    _, gate, hits = _route(scores + rb_ref[...], scores)
    wgt_ref[...] = gate
    chosen = hits[0]
    for hit in hits[1:]:
        chosen = chosen | hit
    m = jnp.where(chosen, 1.0, 0.0)
    before = (lax.broadcasted_iota(jnp.int32, (TM, TM), 0) < lax.broadcasted_iota(jnp.int32, (TM, TM), 1))
    prefix = jnp.dot(m.astype(BF16), jnp.where(before, 1.0, 0.0).astype(BF16), preferred_element_type=F32)
    e_base = (lax.broadcasted_iota(jnp.int32, (N_EXPERTS, 1), 0) * N_TOK).astype(F32)
    row_all = prefix + (cnt_ref[...] + e_base)
    dest_ref[...] = jnp.concatenate(
        [jnp.where(hit, row_all, 0.0).sum(axis=0, keepdims=True) for hit in hits], axis=0).astype(jnp.int32)
    cnt_ref[...] += m.sum(axis=1, keepdims=True)

    @pl.when(step == pl.num_programs(0) - 1)
    def _():
        _block_plan(cnt_ref[...], plan_ref)


def _block_plan(counts, plan_ref):
    cap_blocks = N_TOK // MOE_BLOCK
    nblk = ((counts.astype(jnp.int32) + (MOE_BLOCK - 1)) // MOE_BLOCK).astype(F32)
    lower = (lax.broadcasted_iota(jnp.int32, (N_EXPERTS, N_EXPERTS), 0)
             >= lax.broadcasted_iota(jnp.int32, (N_EXPERTS, N_EXPERTS), 1))
    cum = jnp.dot(jnp.where(lower, 1.0, 0.0).astype(BF16), jnp.broadcast_to(nblk, (N_EXPERTS, LANES)).astype(BF16),
                  preferred_element_type=F32)[:, :1]
    n_used = cum[N_EXPERTS - 1:, :]
    slot = jnp.minimum(lax.broadcasted_iota(jnp.int32, (1, PLAN_LANES), 1).astype(F32), n_used - 1.0)
    done = cum <= slot
    expert = jnp.where(done, 1.0, 0.0).sum(axis=0, keepdims=True)
    blocks_before = jnp.where(done, nblk, 0.0).sum(axis=0, keepdims=True)
    plan_ref[0:1, :] = (expert * cap_blocks + (slot - blocks_before)).astype(jnp.int32)
    plan_ref[1:2, :] = expert.astype(jnp.int32)
    plan_ref[2:3, :] = jnp.broadcast_to(n_used, (1, PLAN_LANES)).astype(jnp.int32)
    run_end = jnp.where(cum > slot, cum, jnp.inf).min(axis=0, keepdims=True)
    plan_ref[3:4, :] = jnp.minimum(jnp.where(cum <= run_end, 1.0, 0.0).sum(axis=0, keepdims=True),
                                   N_EXPERTS - 1.0).astype(jnp.int32)
    plan_ref[4:5, :] = jnp.where(run_end < n_used, 1, 0).astype(jnp.int32)


def _outproj(x, parts, mod, g2, router_w, router_b):
    in_specs = _pair_specs(D)
    args = [x[0], x[1]]
    for ap, a_s, w in parts:
        width = ap.shape[1]
        in_specs += _pair_specs(width) + [pl.BlockSpec((width, D), lambda i: (0, 0))]
        args += [ap, a_s, w]
    in_specs += [_mod_spec(2), _mod_spec(3), _mod_spec(4),
                 pl.BlockSpec((1, D), lambda i: (0, 0)),
                 pl.BlockSpec((N_EXPERTS, D), lambda i: (0, 0)),
                 pl.BlockSpec((N_EXPERTS, 1), lambda i: (0, 0))]
    args += [mod, mod, mod, g2.reshape(1, D), router_w.T, router_b.reshape(N_EXPERTS, 1)]
    return pl.pallas_call(
        functools.partial(_outproj_kernel, n_parts=len(parts)),
        grid=(N_TOK // TM,),
        in_specs=in_specs,
        out_specs=[pl.BlockSpec((TM, D), lambda i: (i, 0)),
                   pl.BlockSpec((TM, D // 2), lambda i: (i, 0)),
                   pl.BlockSpec((TOP_K, TM), lambda i: (0, i)),
                   pl.BlockSpec((TOP_K, TM), lambda i: (0, i)),
                   pl.BlockSpec((8, PLAN_LANES), lambda i: (0, 0))],
        out_shape=[jax.ShapeDtypeStruct((N_TOK, D), F32),
                   jax.ShapeDtypeStruct((N_TOK, D // 2), jnp.uint32),
                   jax.ShapeDtypeStruct((TOP_K, N_TOK), jnp.int32),
                   jax.ShapeDtypeStruct((TOP_K, N_TOK), F32),
                   jax.ShapeDtypeStruct((8, PLAN_LANES), jnp.int32)],
        scratch_shapes=[pltpu.VMEM((N_EXPERTS, 1), F32)],
        compiler_params=_cparams(1),
        name="outproj_router",
    )(*args)


def _experts_kernel(br_ref, be_ref, nu_ref, ne_ref, hn_ref, x_hbm, w1_hbm, w3_hbm, w2_hbm, o_ref,
                    w1b, w3b, w2b, w1f, w3f, w2f, xbuf, xsem, wsem, *, layer):
    i = pl.program_id(0)
    n_steps = pl.num_programs(0)
    e = be_ref[i]
    prev = be_ref[jnp.maximum(i - 1, 0)]

    def x_copy(step):
        slot = step % EXPERT_X_SLOTS
        rows = pl.ds(pl.multiple_of(br_ref[step] * MOE_BLOCK, MOE_BLOCK), MOE_BLOCK)
        return pltpu.make_async_copy(x_hbm.at[rows], xbuf.at[slot], xsem.at[slot])

    @pl.when(i == 0)
    def _():
        x_copy(0).start()
        x_copy(1).start()

    @pl.when(i + 2 < n_steps)
    def _():
        x_copy(i + 2).start()

    x_copy(i).wait()

    def w_copies(expert):
        return [pltpu.make_async_copy(src.at[layer, expert], dst, wsem.at[k])
                for k, (src, dst) in enumerate(((w1_hbm, w1f), (w3_hbm, w3f), (w2_hbm, w2f)))]

    @pl.when(i == 0)
    def _():
        for c in w_copies(e):
            c.start()

    @pl.when((i == 0) | (e != prev))
    def _():
        for c in w_copies(e):
            c.wait()
        w1b[...] = w1f[...].astype(BF16)
        w3b[...] = w3f[...].astype(BF16)
        w2b[...] = w2f[...].astype(BF16)

        @pl.when(hn_ref[i] > 0)
        def _():
            for c in w_copies(ne_ref[i]):
                c.start()

    @pl.when(i < nu_ref[0])
    def _():
        hi, lo = _unpack_bf16_pairs(xbuf[i % EXPERT_X_SLOTS])
        a = _dot_halves(hi, lo, w1b)
        b = _dot_halves(hi, lo, w3b)
        h = (_silu(a) * b).astype(BF16)
        o_ref[...] = _pack_bf16_pairs(jnp.dot(h, w2b[...], preferred_element_type=F32))


def _experts(plan, x_rows, w1, w3, w2, layer):
    any_spec = pl.BlockSpec(memory_space=pl.ANY)
    return pl.pallas_call(
        functools.partial(_experts_kernel, layer=layer),
        grid_spec=pltpu.PrefetchScalarGridSpec(
            num_scalar_prefetch=5,
            grid=(N_MOE_BLOCKS,),
            in_specs=[any_spec, any_spec, any_spec, any_spec],
            out_specs=pl.BlockSpec((MOE_BLOCK, D // 2), lambda i, br, be, nu, ne, hn: (br[i], 0)),
            scratch_shapes=[pltpu.VMEM((D, FF), BF16), pltpu.VMEM((D, FF), BF16), pltpu.VMEM((FF, D), BF16),
                            pltpu.VMEM((D, FF), F32), pltpu.VMEM((D, FF), F32), pltpu.VMEM((FF, D), F32),
                            pltpu.VMEM((EXPERT_X_SLOTS, MOE_BLOCK, D // 2), jnp.uint32),
                            pltpu.SemaphoreType.DMA((EXPERT_X_SLOTS,)),
                            pltpu.SemaphoreType.DMA((3,))]),
        out_shape=jax.ShapeDtypeStruct(x_rows.shape, jnp.uint32),
        compiler_params=_cparams(1),
        name="experts",
    )(plan[0], plan[1], plan[2, :1], plan[3], plan[4], x_rows, w1, w3, w2)


SC_CORES = 2
SC_SUBCORES = 16
SC_WORKERS = SC_CORES * SC_SUBCORES
SC_CHUNK_BYTES = 64 * 1024
SC_SLOTS = 4


def _sc_scatter(rows, dest, n_out):
    n_rows, width = rows.shape
    picks = dest.shape[0]
    chunk = SC_CHUNK_BYTES // (4 * width)
    per_worker = n_rows // SC_WORKERS
    n_chunks = per_worker // chunk
    assert per_worker * SC_WORKERS == n_rows and n_chunks * chunk == per_worker and n_chunks % 2 == 0
    mesh = plsc.VectorSubcoreMesh(core_axis_name="c", subcore_axis_name="s")

    @functools.partial(
        pl.kernel, mesh=mesh,
        out_type=jax.ShapeDtypeStruct((n_out, width), rows.dtype),
        scratch_types=[pltpu.VMEM((picks, n_chunks, chunk), jnp.int32),
                       pltpu.VMEM((2, chunk, width), rows.dtype),
                       pltpu.SemaphoreType.DMA((2,)),
                       pltpu.SemaphoreType.DMA((2,))])
    def scatter(r_hbm, d_hbm, o_hbm, idx_v, rows_v, lsem, ssem):
        worker = lax.axis_index("s") * SC_CORES + lax.axis_index("c")
        base = worker * per_worker
        for k in range(picks):
            pltpu.sync_copy(d_hbm.at[k, worker], idx_v.at[k])

        def load_copy(c, b):
            src = pl.ds(pl.multiple_of(base + c * chunk, chunk), chunk)
            return pltpu.make_async_copy(r_hbm.at[src], rows_v.at[b], lsem.at[b])

        def store_copy(c, b, k):
            return pltpu.make_async_copy(rows_v.at[b], o_hbm.at[idx_v.at[k, c]], ssem.at[b])

        load_copy(0, 0).start()

        @pl.loop(0, n_chunks, step=2)
        def _(c0):
            for b in range(2):
                c = c0 + b
                load_copy(c, b).wait()
                for k in range(picks):
                    store_copy(c, b, k).start()

                @pl.when(c > 0)
                def _():
                    for k in range(picks):
                        store_copy(c - 1, 1 - b, k).wait()

                @pl.when(c + 1 < n_chunks)
                def _():
                    load_copy(c + 1, 1 - b).start()

        for k in range(picks):
            store_copy(n_chunks - 1, 1, k).wait()

    return scatter(rows, dest.reshape(picks, SC_WORKERS, n_chunks, chunk))


def _sc_gather(table, idx):
    n_idx = idx.shape[0]
    width = table.shape[1]
    chunk = SC_CHUNK_BYTES // (4 * width)
    per_worker = n_idx // SC_WORKERS
    n_chunks = per_worker // chunk
    ahead = SC_SLOTS - 1
    assert per_worker * SC_WORKERS == n_idx and n_chunks * chunk == per_worker and n_chunks % SC_SLOTS == 0
    mesh = plsc.VectorSubcoreMesh(core_axis_name="c", subcore_axis_name="s")

    @functools.partial(
        pl.kernel, mesh=mesh,
        out_type=jax.ShapeDtypeStruct((n_idx, width), table.dtype),
        scratch_types=[pltpu.VMEM((per_worker,), jnp.int32),
                       pltpu.VMEM((SC_SLOTS, chunk, width), table.dtype),
                       pltpu.SemaphoreType.DMA((SC_SLOTS,)),
                       pltpu.SemaphoreType.DMA((SC_SLOTS,))])
    def gather(t_hbm, i_hbm, o_hbm, idx_v, rows_v, gsem, wsem):
        worker = lax.axis_index("s") * SC_CORES + lax.axis_index("c")
        base = worker * per_worker
        pltpu.sync_copy(i_hbm.at[pl.ds(pl.multiple_of(base, chunk), per_worker)], idx_v)

        def gather_copy(c, b):
            ids = idx_v.at[pl.ds(pl.multiple_of(c * chunk, chunk), chunk)]
            return pltpu.make_async_copy(t_hbm.at[ids], rows_v.at[b], gsem.at[b])

        def write_copy(c, b):
            rows = pl.ds(pl.multiple_of(base + c * chunk, chunk), chunk)
            return pltpu.make_async_copy(rows_v.at[b], o_hbm.at[rows], wsem.at[b])

        for c in range(ahead):
            gather_copy(c, c).start()

        @pl.loop(0, n_chunks, step=SC_SLOTS)
        def _(c0):
            for b in range(SC_SLOTS):
                c = c0 + b
                refill = (b + ahead) % SC_SLOTS
                gather_copy(c, b).wait()
                write_copy(c, b).start()

                @pl.when(c > 0)
                def _():
                    write_copy(c - 1, refill).wait()

                @pl.when(c + ahead < n_chunks)
                def _():
                    gather_copy(c + ahead, refill).start()

        write_copy(n_chunks - 1, (n_chunks - 1) % SC_SLOTS).wait()

    return gather(table, idx)


TC = 512


def _combine_kernel(x_ref, h_ref, y_ref, wgt_ref, gate_ref, w1_ref, w3_ref, w2_ref, fg_ref, o_ref, *, final):
    hi, lo = _unpack_bf16_pairs(h_ref[...])
    a = _dot_halves(hi, lo, w1_ref)
    b = _dot_halves(hi, lo, w3_ref)
    ffn = jnp.dot((_silu(a) * b).astype(BF16), w2_ref[...], preferred_element_type=F32)
    wgt = wgt_ref[...]
    r_hi = None
    r_lo = None
    for k in range(TOP_K):
        yk = y_ref[k]
        w = wgt[:, k:k + 1]
        t_hi = lax.bitcast_convert_type(yk & jnp.uint32(0xFFFF0000), F32) * w
        t_lo = lax.bitcast_convert_type(yk << 16, F32) * w
        r_hi = t_hi if r_hi is None else r_hi + t_hi
        r_lo = t_lo if r_lo is None else r_lo + t_lo
    x = x_ref[...] + gate_ref[...] * (ffn + jnp.concatenate([r_hi, r_lo], axis=1))
    o_ref[...] = _rms(x, fg_ref[...]) if final else x


def _combine(x, h, y_rows, wgt, mod, sw1, sw3, sw2, final_g, final):
    weights = (sw1.astype(BF16), sw3.astype(BF16), sw2.astype(BF16), final_g.reshape(1, D))

    def rows_from(first_row, n_rows):
        b0 = first_row // TC
        return pl.pallas_call(
            functools.partial(_combine_kernel, final=final),
            grid=(n_rows // TC,),
            in_specs=[pl.BlockSpec((TC, D), lambda i: (i + b0, 0)),
                      pl.BlockSpec((TC, D // 2), lambda i: (i + b0, 0)),
                      pl.BlockSpec((TOP_K, TC, D // 2), lambda i: (0, i + b0, 0)),
                      pl.BlockSpec((TC, TOP_K), lambda i: (i + b0, 0)),
                      _mod_spec(5, TC, b0),
                      pl.BlockSpec((D, FF), lambda i: (0, 0)),
                      pl.BlockSpec((D, FF), lambda i: (0, 0)),
                      pl.BlockSpec((FF, D), lambda i: (0, 0)),
                      pl.BlockSpec((1, D), lambda i: (0, 0))],
            out_specs=pl.BlockSpec((TC, D), lambda i: (i, 0)),
            out_shape=jax.ShapeDtypeStruct((n_rows, D), F32),
            compiler_params=_cparams(1),
            name="combine",
        )(x, h, y_rows, wgt, mod, *weights)

    return rows_from(0, NP_TOK), rows_from(NP_TOK, NS_TOK)


def kernel(x_prompt, x_sample, cache_a_k, cache_a_v, state_ret_fwd, state_ret_bwd, cache_c_k, cache_c_v,
           c, c_ctx, norm1_g, norm2_g, ada_w, ada_b, even_w_in, even_w_out, sink_a, ret_decay_fwd,
           ret_decay_bwd, ret_gn_g, odd_w_in, odd_w_out, na_rpb, router_w, router_b, exp_w1, exp_w3,
           exp_w2, sh_w1, sh_w3, sh_w2, final_g):
    x = (x_prompt.reshape(NP_TOK, D), x_sample.reshape(NS_TOK, D))
    cc = jnp.concatenate([c_ctx[None], c, jnp.zeros((8 - 1 - DEC_BATCH, D), F32)], axis=0)
    rope = _rope_tables()
    outs = {}
    for l in range(2):
        mod = _ada(cc, ada_w, ada_b, l)
        if l == 0:
            p = _inproj(x, norm1_g[l], mod, even_w_in[0].astype(BF16), rope, A_Q + A_KV)
            oa_p = _ctx_gqa(p, sink_a[0])
            oa_s = _win_attention(p, cache_a_k[:, 0], cache_a_v[:, 0], sink_a[0])
            zero = jnp.zeros((BATCH, B_HEADS // 2, LANES, LANES), F32)
            ob_p, sf, sb = _retention(p, 0, BATCH, SEQ, ret_decay_fwd[0], ret_decay_bwd[0], ret_gn_g[0], zero, zero)
            ob_s, _, _ = _retention(p, NP_TOK, DEC_BATCH, DEC_SEQ, ret_decay_fwd[0], ret_decay_bwd[0], ret_gn_g[0],
                                    _blockdiag_states(state_ret_fwd[:, 0]), _blockdiag_states(state_ret_bwd[:, 0]))
            w_out = even_w_out[0].astype(BF16)
            parts = [(oa_p, oa_s, w_out[:A_Q]), (ob_p, ob_s, w_out[A_Q:])]
            outs["a_k"] = p[:NP_TOK, A_Q:A_Q + A_KV].reshape(BATCH, 1, SEQ, A_KV_HEADS, HD)
            outs["a_v"] = p[:NP_TOK, A_Q + A_KV:A_Q + 2 * A_KV].reshape(BATCH, 1, SEQ, A_KV_HEADS, HD)
            outs["r_f"] = _diag_states(sf).reshape(BATCH, 1, B_HEADS, HD, HD)
            outs["r_b"] = _diag_states(sb).reshape(BATCH, 1, B_HEADS, HD, HD)
        else:
            p = _inproj(x, norm1_g[l], mod, odd_w_in[0].astype(BF16), rope, 0)
            o_p = _ctx_mha(p)
            o_s = _na_attention(p, cache_c_k[:, 0], cache_c_v[:, 0], na_rpb[0])
            parts = [(o_p, o_s, odd_w_out[0].astype(BF16))]
            outs["c_k"] = p[:NP_TOK, C_W:2 * C_W].reshape(BATCH, 1, SEQ, C_HEADS, HD)
            outs["c_v"] = p[:NP_TOK, 2 * C_W:3 * C_W].reshape(BATCH, 1, SEQ, C_HEADS, HD)
        x_mid, h, dest, gate_t, plan = _outproj(x, parts, mod, norm2_g[l], router_w[l], router_b[l])
        y = _experts(plan, _sc_scatter(h, dest, N_EXPERTS * N_TOK), exp_w1, exp_w3, exp_w2, l)
        y_rows = _sc_gather(y, dest.reshape(N_ASSIGN)).reshape(TOP_K, N_TOK, D // 2)
        x = _combine(x_mid, h, y_rows, gate_t.T, mod, sh_w1[l], sh_w3[l], sh_w2[l], final_g, final=(l == 1))
    y_prompt = x[0].reshape(BATCH, SEQ, D)
    y_sample = x[1].reshape(DEC_BATCH, DEC_SEQ, D)
    return (y_prompt, y_sample, outs["a_k"], outs["a_v"], outs["r_f"], outs["r_b"], outs["c_k"], outs["c_v"])
```

```python
import functools

import jax
import jax.numpy as jnp
from jax import lax
from jax.experimental import pallas as pl
from jax.experimental.pallas import tpu as pltpu
from jax.experimental.pallas import tpu_sc as plsc

F32 = jnp.float32
BF16 = jnp.bfloat16
HIGHEST = lax.Precision.HIGHEST

D = 1024
BATCH = 32
SEQ = 256
DEC_BATCH = 4
DEC_SEQ = 4096
PAST = 256
GRID_W = 64
HD = 64
EPS = 1e-6
NEG = -1e30
ROPE_BASE = 10000.0
A_HEADS = 8
A_KV_HEADS = 2
A_Q = A_HEADS * HD
A_KV = A_KV_HEADS * HD
B_HEADS = 8
B_W = B_HEADS * HD
EVEN_IN = A_Q + 2 * A_KV + 4 * B_W
C_HEADS = 16
C_W = C_HEADS * HD
NA_KH = 8
NA_KW = 16
N_EXPERTS = 64
TOP_K = 8
N_GROUPS = 8
TOPK_GROUPS = 4
FF = 256
ROUTED_SCALE = 2.5
MOE_BLOCK = 1024
EXPERT_X_SLOTS = 3
RET_CHUNK = 256
RET_UNROLL = 4
RET_NORM_ROWS = 1024
A_WINDOW = 128

NP_TOK = BATCH * SEQ
NS_TOK = DEC_BATCH * DEC_SEQ
N_TOK = NP_TOK + NS_TOK
N_ASSIGN = N_TOK * TOP_K
N_MOE_BLOCKS = (N_ASSIGN + N_EXPERTS * (MOE_BLOCK - 1) + MOE_BLOCK - 1) // MOE_BLOCK
PLAN_LANES = 512
assert N_TOK % MOE_BLOCK == 0 and N_MOE_BLOCKS <= PLAN_LANES

LANES = 128
TM = 512
NA_ROWS = 8
V7X_VMEM_LIMIT = 56 * 1024 * 1024

NT_DIMS = (((1,), (1,)), ((), ()))


def _cparams(n_axes, vmem=V7X_VMEM_LIMIT):
    return pltpu.CompilerParams(dimension_semantics=("arbitrary",) * n_axes, vmem_limit_bytes=vmem)


def _seg_of_block(i, rows):
    row0 = i * rows
    return jnp.where(row0 < NP_TOK, 0, 1 + (row0 - NP_TOK) // DEC_SEQ)


def _mod_spec(chunk, rows=TM, first_block=0):
    return pl.BlockSpec((None, 1, D), lambda i: (_seg_of_block(i + first_block, rows), 0, chunk))


def _pair_specs(width, rows=TM):
    npb = NP_TOK // rows
    nsb = NS_TOK // rows
    return [pl.BlockSpec((rows, width), lambda i: (jnp.minimum(i, npb - 1), 0)),
            pl.BlockSpec((rows, width), lambda i: (jnp.clip(i - npb, 0, nsb - 1), 0))]


def _pick_rows(p_ref, s_ref, rows=TM):
    return jnp.where(pl.program_id(0) < NP_TOK // rows, p_ref[...], s_ref[...])


def _silu(x):
    return x * jax.nn.sigmoid(x)


def _rms(x, g):
    return x * lax.rsqrt(jnp.mean(x * x, axis=-1, keepdims=True) + EPS) * g


def _lane_lo():
    return lax.broadcasted_iota(jnp.int32, (1, LANES), 1) < HD


def _ada_kernel(c_ref, w_ref, b_ref, o_ref):
    a = _silu(c_ref[...])
    o_ref[...] = jnp.dot(a, w_ref[...], preferred_element_type=F32, precision=HIGHEST) + b_ref[...]


def _ada(cc, w, b, layer):
    tn = 1536
    out = pl.pallas_call(
        _ada_kernel,
        grid=(6 * D // tn,),
        in_specs=[pl.BlockSpec((8, D), lambda j: (0, 0)),
                  pl.BlockSpec((None, D, tn), lambda j: (layer, 0, j)),
                  pl.BlockSpec((None, 1, tn), lambda j: (layer, 0, j))],
        out_specs=pl.BlockSpec((8, tn), lambda j: (0, j)),
        out_shape=jax.ShapeDtypeStruct((8, 6 * D), F32),
        compiler_params=_cparams(1),
        name="ada",
    )(cc, w, b.reshape(b.shape[0], 1, 6 * D))
    return out.reshape(8, 1, 6 * D)


def _inproj_kernel(xp_ref, xs_ref, g_ref, shift_ref, scale_ref, w_ref, cos_ref, sin_ref, o_ref, *, rope_cols):
    h = _rms(_pick_rows(xp_ref, xs_ref), g_ref[...]) * (1.0 + scale_ref[...]) + shift_ref[...]
    o = jnp.dot(h.astype(BF16), w_ref[...], preferred_element_type=F32)
    if rope_cols:
        cos = cos_ref[...]
        sin = sin_ref[...]
        lane = lax.broadcasted_iota(jnp.int32, (1, LANES), 1)
        first = (lane % 32) < 16
        for c in range(rope_cols // LANES):
            oc = o[:, c * LANES:(c + 1) * LANES]
            partner = jnp.where(first, pltpu.roll(oc, LANES - 16, 1), pltpu.roll(oc, 16, 1))
            o_ref[:, c * LANES:(c + 1) * LANES] = oc * cos + partner * sin
        o_ref[:, rope_cols:] = o[:, rope_cols:]
    else:
        o_ref[...] = o


def _rope_tables():
    half = HD // 2
    inv = ROPE_BASE ** (-jnp.arange(0, half, 2, dtype=F32) / half)
    t = jnp.arange(DEC_SEQ)
    ang_r = (t // GRID_W).astype(F32)[:, None] * inv[None]
    ang_c = (t % GRID_W).astype(F32)[:, None] * inv[None]

    def head(fn_r, fn_c, sign):
        return jnp.concatenate([sign[0] * fn_r, sign[1] * fn_r, sign[0] * fn_c, sign[1] * fn_c], axis=-1)

    cos = head(jnp.cos(ang_r), jnp.cos(ang_c), (1.0, 1.0))
    sin = head(jnp.sin(ang_r), jnp.sin(ang_c), (-1.0, 1.0))
    cos = jnp.concatenate([jnp.ones((TM, HD), F32), cos], axis=0)
    sin = jnp.concatenate([jnp.zeros((TM, HD), F32), sin], axis=0)
    return jnp.tile(cos, (1, 2)), jnp.tile(sin, (1, 2))


def _inproj(x, g, mod, w_bf16, rope, rope_cols):
    n_out = w_bf16.shape[1]
    npb = NP_TOK // TM
    spb = DEC_SEQ // TM

    def rope_map(i):
        return (jnp.where(i < npb, 0, 1 + (i - npb) % spb), 0)

    return pl.pallas_call(
        functools.partial(_inproj_kernel, rope_cols=rope_cols),
        grid=(N_TOK // TM,),
        in_specs=_pair_specs(D) + [
                  pl.BlockSpec((1, D), lambda i: (0, 0)),
                  _mod_spec(0), _mod_spec(1),
                  pl.BlockSpec((D, n_out), lambda i: (0, 0)),
                  pl.BlockSpec((TM, LANES), rope_map),
                  pl.BlockSpec((TM, LANES), rope_map)],
        out_specs=pl.BlockSpec((TM, n_out), lambda i: (i, 0)),
        out_shape=jax.ShapeDtypeStruct((N_TOK, n_out), F32),
        compiler_params=_cparams(1),
        name="inproj",
    )(x[0], x[1], g.reshape(1, D), mod, mod, w_bf16, rope[0], rope[1])


def _softmax_av(s_list, v_list, sink=None):
    mx = s_list[0].max(axis=-1, keepdims=True)
    for s in s_list[1:]:
        mx = jnp.maximum(mx, s.max(axis=-1, keepdims=True))
    if sink is not None:
        mx = jnp.maximum(mx, sink)
    den = jnp.exp(sink - mx) if sink is not None else 0.0
    acc = None
    for s, v in zip(s_list, v_list):
        p = jnp.exp(s - mx)
        den = den + p.sum(axis=-1, keepdims=True)
        pv = jnp.dot(p.astype(BF16), v, preferred_element_type=F32)
        acc = pv if acc is None else acc + pv
    return acc / den


def _dup_half(x, j, lo):
    xr = pltpu.roll(x, HD, 1)
    return jnp.where(lo, x, xr) if j == 0 else jnp.where(lo, xr, x)


def _stack_heads(q_ref, heads, lo, scale):
    parts = []
    for h in heads:
        qp = q_ref[:, (h // 2) * LANES:(h // 2 + 1) * LANES]
        keep = lo if h % 2 == 0 else jnp.logical_not(lo)
        parts.append(jnp.where(keep, qp, 0.0) * scale)
    return jnp.concatenate(parts, axis=0).astype(BF16)


def _sink_column(sink_ref, heads, rows):
    return jnp.concatenate([jnp.full((rows, 1), sink_ref[h], F32) for h in heads], axis=0)


def _ctx_gqa_kernel(sink_ref, q_ref, k_ref, v_ref, o_ref):
    lo = _lane_lo()
    k = k_ref[...]
    v = v_ref[...]
    group = A_HEADS // A_KV_HEADS
    scores = []
    for j in range(A_KV_HEADS):
        q = _stack_heads(q_ref, list(range(group * j, group * (j + 1))), lo, HD ** -0.5)
        scores.append(lax.dot_general(q, _dup_half(k, j, lo).astype(BF16), NT_DIMS, preferred_element_type=F32))
    s = jnp.concatenate(scores, axis=0)
    sink = _sink_column(sink_ref, list(range(A_HEADS)), SEQ)
    mx = jnp.maximum(s.max(axis=-1, keepdims=True), sink)
    e = jnp.exp(s - mx)
    den = jnp.exp(sink - mx) + e.sum(axis=-1, keepdims=True)
    e = e.astype(BF16)
    rows_per_group = group * SEQ
    for j in range(A_KV_HEADS):
        rows = slice(j * rows_per_group, (j + 1) * rows_per_group)
        o = jnp.dot(e[rows], _dup_half(v, j, lo).astype(BF16), preferred_element_type=F32) / den[rows]
        for t in range(group // 2):
            pair = (group * j) // 2 + t
            o_ref[:, pair * LANES:(pair + 1) * LANES] = jnp.where(
                lo, o[(2 * t) * SEQ:(2 * t + 1) * SEQ], o[(2 * t + 1) * SEQ:(2 * t + 2) * SEQ])


def _ctx_gqa(p, sink):
    return pl.pallas_call(
        _ctx_gqa_kernel,
        grid_spec=pltpu.PrefetchScalarGridSpec(
            num_scalar_prefetch=1,
            grid=(BATCH,),
            in_specs=[pl.BlockSpec((SEQ, A_Q), lambda b, s: (b, 0)),
                      pl.BlockSpec((SEQ, A_KV), lambda b, s: (b, A_Q // A_KV)),
                      pl.BlockSpec((SEQ, A_KV), lambda b, s: (b, A_Q // A_KV + 1))],
            out_specs=pl.BlockSpec((SEQ, A_Q), lambda b, s: (b, 0))),
        out_shape=jax.ShapeDtypeStruct((NP_TOK, A_Q), F32),
        compiler_params=_cparams(1),
        name="ctx_gqa",
    )(sink, p, p, p)


def _win_kernel(sink_ref, q_ref, kp_ref, kc_ref, kn_ref, vp_ref, vc_ref, vn_ref, ck_ref, cv_ref, o_ref):
    i = pl.program_id(1)
    lo = _lane_lo()
    k = jnp.concatenate([kp_ref[...], kc_ref[...], kn_ref[...]], axis=0)
    v = jnp.concatenate([vp_ref[...], vc_ref[...], vn_ref[...]], axis=0)
    ck = ck_ref[...]
    cv = cv_ref[...]
    group = A_HEADS // A_KV_HEADS
    n_keys = WIN_Q + 2 * A_WINDOW
    qpos = i * WIN_Q + lax.broadcasted_iota(jnp.int32, (WIN_Q, n_keys), 0)
    kpos = i * WIN_Q - A_WINDOW + lax.broadcasted_iota(jnp.int32, (WIN_Q, n_keys), 1)
    valid = (jnp.abs(kpos - qpos) <= A_WINDOW) & (kpos >= 0) & (kpos < DEC_SEQ)
    valid = jnp.concatenate([valid] * group, axis=0)
    s_loc, s_ctx, values = [], [], []
    for j in range(A_KV_HEADS):
        heads = list(range(group * j, group * (j + 1)))
        kd = _dup_half(k, j, lo).astype(BF16)
        ckd = _dup_half(ck, j, lo).astype(BF16)
        values.append((_dup_half(v, j, lo).astype(BF16), _dup_half(cv, j, lo).astype(BF16)))
        q = _stack_heads(q_ref, heads, lo, HD ** -0.5)
        s_loc.append(jnp.where(valid, lax.dot_general(q, kd, NT_DIMS, preferred_element_type=F32), NEG))
        s_ctx.append(lax.dot_general(q, ckd, NT_DIMS, preferred_element_type=F32))
    s_loc = jnp.concatenate(s_loc, axis=0)
    s_ctx = jnp.concatenate(s_ctx, axis=0)
    sink = _sink_column(sink_ref, list(range(A_HEADS)), WIN_Q)
    mx = jnp.maximum(jnp.maximum(s_loc.max(axis=-1, keepdims=True), s_ctx.max(axis=-1, keepdims=True)), sink)
    p_loc = jnp.exp(s_loc - mx)
    p_ctx = jnp.exp(s_ctx - mx)
    den = p_loc.sum(axis=-1, keepdims=True) + p_ctx.sum(axis=-1, keepdims=True) + jnp.exp(sink - mx)
    p_loc = p_loc.astype(BF16)
    p_ctx = p_ctx.astype(BF16)
    rows_per_group = group * WIN_Q
    for j, (vd, cvd) in enumerate(values):
        rows = slice(j * rows_per_group, (j + 1) * rows_per_group)
        o = (jnp.dot(p_loc[rows], vd, preferred_element_type=F32)
             + jnp.dot(p_ctx[rows], cvd, preferred_element_type=F32)) / den[rows]
        for t in range(group // 2):
            pair = (group * j) // 2 + t
            o_ref[:, pair * LANES:(pair + 1) * LANES] = jnp.where(
                lo, o[(2 * t) * WIN_Q:(2 * t + 1) * WIN_Q], o[(2 * t + 1) * WIN_Q:(2 * t + 2) * WIN_Q])


WIN_Q = 256


def _win_attention(p, cache_k, cache_v, sink):
    nblk = DEC_SEQ // WIN_Q
    side = WIN_Q // A_WINDOW
    nside = DEC_SEQ // A_WINDOW
    base = NP_TOK // WIN_Q
    side_base = NP_TOK // A_WINDOW
    kcol = A_Q // A_KV

    def main_spec(col):
        return pl.BlockSpec((WIN_Q, A_KV), lambda b, i, s: (base + b * nblk + i, col))

    def side_spec(col, off):
        return pl.BlockSpec((A_WINDOW, A_KV),
                            lambda b, i, s: (side_base + b * nside + jnp.clip(side * i + off, 0, nside - 1), col))

    ctx_spec = pl.BlockSpec((None, PAST, A_KV), lambda b, i, s: (b, 0, 0))
    return pl.pallas_call(
        _win_kernel,
        grid_spec=pltpu.PrefetchScalarGridSpec(
            num_scalar_prefetch=1,
            grid=(DEC_BATCH, nblk),
            in_specs=[pl.BlockSpec((WIN_Q, A_Q), lambda b, i, s: (base + b * nblk + i, 0)),
                      side_spec(kcol, -1), main_spec(kcol), side_spec(kcol, side),
                      side_spec(kcol + 1, -1), main_spec(kcol + 1), side_spec(kcol + 1, side),
                      ctx_spec, ctx_spec],
            out_specs=pl.BlockSpec((WIN_Q, A_Q), lambda b, i, s: (b * nblk + i, 0))),
        out_shape=jax.ShapeDtypeStruct((NS_TOK, A_Q), F32),
        compiler_params=_cparams(2),
        name="win_attn",
    )(sink, p, p, p, p, p, p, p, cache_k.reshape(DEC_BATCH, PAST, A_KV), cache_v.reshape(DEC_BATCH, PAST, A_KV))


def _ret_kernel(df_ref, db_ref, q_ref, k_ref, v_ref, g_ref, gn_ref, s0f_ref, s0b_ref,
                o_ref, sf_ref, sb_ref, of_scr, ob_scr, *, length):
    c_len = RET_CHUNK
    n = length // c_len
    lo = _lane_lo()
    hi = jnp.logical_not(lo)
    row = lax.broadcasted_iota(jnp.int32, (c_len, c_len), 0)
    col = lax.broadcasted_iota(jnp.int32, (c_len, c_len), 1)
    rowp = lax.broadcasted_iota(jnp.int32, (LANES, LANES), 0)
    colp = lax.broadcasted_iota(jnp.int32, (LANES, LANES), 1)
    blockdiag = (rowp < HD) == (colp < HD)
    idx = lax.broadcasted_iota(jnp.int32, (c_len, 1), 0).astype(F32)

    def direction(dec_ref, forward):
        lg = -jnp.exp(dec_ref[...])
        diff = (row - col) if forward else (col - row)
        keep = (diff >= 0) if forward else (diff > 0)
        dist = jnp.maximum(diff, 0).astype(F32)
        dm = jnp.concatenate([jnp.where(keep, jnp.exp(dist * lg[:, off:off + 1]), 0.0) for off in (0, HD)], axis=0)
        if forward:
            xi = jnp.exp((idx + 1.0) * lg)
            zeta = jnp.exp((c_len - 1.0 - idx) * lg)
        else:
            xi = jnp.exp((c_len - idx) * lg)
            zeta = jnp.exp(idx * lg)
        return dm, xi, zeta, jnp.exp(c_len * lg)

    def chunk(c, state, consts):
        dm, xi, zeta, gch = consts
        rows = pl.ds(pl.multiple_of(c * c_len, c_len), c_len)
        qc = q_ref[rows, :]
        kc = k_ref[rows, :] * HD ** -0.5
        vc = v_ref[rows, :].astype(BF16)
        kb = kc.astype(BF16)
        q2 = jnp.concatenate([jnp.where(lo, qc, 0.0), jnp.where(hi, qc, 0.0)], axis=0).astype(BF16)
        inner = lax.dot_general(q2, kb, NT_DIMS, preferred_element_type=F32) * dm
        kz_t = (kc * zeta).T
        res = jnp.dot(jnp.concatenate([inner, kz_t], axis=0).astype(BF16), vc, preferred_element_type=F32)
        cross = jnp.dot(qc.astype(BF16), state.astype(BF16), preferred_element_type=F32) * xi
        o = jnp.where(lo, res[:c_len], res[c_len:2 * c_len]) + cross
        state = gch * state + jnp.where(blockdiag, res[2 * c_len:], 0.0)
        return rows, o, state

    cf = direction(df_ref, True)
    cb = direction(db_ref, False)

    def scan_body(t, states):
        rows_f, o_f, state_f = chunk(t, states[0], cf)
        of_scr[rows_f, :] = o_f
        rows_b, o_b, state_b = chunk(n - 1 - t, states[1], cb)
        ob_scr[rows_b, :] = o_b
        return state_f, state_b

    state_f, state_b = lax.fori_loop(0, n, scan_body, (s0f_ref[...], s0b_ref[...]), unroll=min(n, RET_UNROLL))
    sf_ref[...] = state_f
    sb_ref[...] = state_b

    gn = gn_ref[...]
    norm_rows = min(RET_NORM_ROWS, length)
    n_norm = length // norm_rows

    def per_head(x):
        a = jnp.where(lo, x, 0.0).sum(axis=-1, keepdims=True)
        b = jnp.where(hi, x, 0.0).sum(axis=-1, keepdims=True)
        return jnp.where(lo, a, b) * (1.0 / HD)

    def norm_body(t, carry):
        rows = pl.ds(pl.multiple_of(t * norm_rows, norm_rows), norm_rows)
        o = of_scr[rows, :] + ob_scr[rows, :]
        d = o - per_head(o)
        y = d * lax.rsqrt(per_head(d * d) + EPS) * gn
        o_ref[rows, :] = _silu(g_ref[rows, :]) * y
        return carry

    lax.fori_loop(0, n_norm, norm_body, 0)


def _pair_lanes(v):
    return jnp.repeat(v.astype(F32), HD).reshape(B_HEADS // 2, 1, LANES)


def _blockdiag_states(s):
    b = s.shape[0]
    s = s.astype(F32).reshape(b, B_HEADS // 2, 2, HD, HD)
    z = jnp.zeros_like(s[:, :, 0])
    top = jnp.concatenate([s[:, :, 0], z], axis=-1)
    bot = jnp.concatenate([z, s[:, :, 1]], axis=-1)
    return jnp.concatenate([top, bot], axis=-2)


def _diag_states(sp):
    b = sp.shape[0]
    s = jnp.stack([sp[:, :, :HD, :HD], sp[:, :, HD:, HD:]], axis=2)
    return s.reshape(b, B_HEADS, HD, HD)


def _retention(p, row_base, batch, length, dec_f, dec_b, gn_g, s0f, s0b):
    npairs = B_HEADS // 2
    blk0 = row_base // length
    qcol = (A_Q + 2 * A_KV) // LANES

    def col_spec(off):
        return pl.BlockSpec((length, LANES), lambda b, h: (blk0 + b, qcol + off * npairs + h))

    lane_spec = pl.BlockSpec((None, 1, LANES), lambda b, h: (h, 0, 0))
    state_spec = pl.BlockSpec((None, None, LANES, LANES), lambda b, h: (b, h, 0, 0))
    state_shape = jax.ShapeDtypeStruct((batch, npairs, LANES, LANES), F32)
    return pl.pallas_call(
        functools.partial(_ret_kernel, length=length),
        grid=(batch, npairs),
        in_specs=[lane_spec, lane_spec, col_spec(0), col_spec(1), col_spec(2), col_spec(3), lane_spec,
                  state_spec, state_spec],
        out_specs=[pl.BlockSpec((length, LANES), lambda b, h: (b, h)), state_spec, state_spec],
        out_shape=[jax.ShapeDtypeStruct((batch * length, B_W), F32), state_shape, state_shape],
        scratch_shapes=[pltpu.VMEM((length, LANES), F32), pltpu.VMEM((length, LANES), F32)],
        compiler_params=_cparams(2),
        name="retention",
    )(_pair_lanes(dec_f), _pair_lanes(dec_b), p, p, p, p, gn_g.reshape(npairs, 1, LANES), s0f, s0b)


def _ctx_mha_kernel(q_ref, k_ref, v_ref, o_ref):
    lo = _lane_lo()
    for pair in range(C_HEADS // 2):
        cols = slice(pair * LANES, (pair + 1) * LANES)
        q = _stack_heads(q_ref, [2 * pair, 2 * pair + 1], lo, HD ** -0.5)
        s = lax.dot_general(q, k_ref[:, cols].astype(BF16), NT_DIMS, preferred_element_type=F32)
        o = _softmax_av([s], [v_ref[:, cols].astype(BF16)])
        o_ref[:, cols] = jnp.where(lo, o[:SEQ], o[SEQ:])


def _ctx_mha(p):
    return pl.pallas_call(
        _ctx_mha_kernel,
        grid=(BATCH,),
        in_specs=[pl.BlockSpec((SEQ, C_W), lambda b: (b, 0)),
                  pl.BlockSpec((SEQ, C_W), lambda b: (b, 1)),
                  pl.BlockSpec((SEQ, C_W), lambda b: (b, 2))],
        out_specs=pl.BlockSpec((SEQ, C_W), lambda b: (b, 0)),
        out_shape=jax.ShapeDtypeStruct((NP_TOK, C_W), F32),
        compiler_params=_cparams(1),
        name="ctx_mha",
    )(p, p, p)


NA_WIN_ROWS = 2 * NA_ROWS
NA_WIN = NA_WIN_ROWS * GRID_W
NA_QROWS = NA_ROWS * GRID_W
NA_PAD_ROWS = NA_KH // 2
NA_TABLE = 1536


def _na_kernel(q_ref, kp_ref, km_ref, kn_ref, vp_ref, vm_ref, vn_ref, ck_ref, cv_ref, ue_ref, uo_ref, o_ref):
    r0 = pl.program_id(2) * NA_ROWS
    n_rows = DEC_SEQ // GRID_W
    lo = _lane_lo()
    k = jnp.concatenate([kp_ref[...], km_ref[...], kn_ref[...]], axis=0).astype(BF16)
    v = jnp.concatenate([vp_ref[...], vm_ref[...], vn_ref[...]], axis=0).astype(BF16)
    ck = ck_ref[...].astype(BF16)
    cv = cv_ref[...].astype(BF16)
    q = q_ref[...] * HD ** -0.5
    klane = lax.broadcasted_iota(jnp.int32, (1, NA_WIN), 1)
    outs = []
    for half, keep in enumerate((lo, jnp.logical_not(lo))):
        qh = jnp.where(keep, q, 0.0).astype(BF16)
        s = lax.dot_general(qh, k, NT_DIMS, preferred_element_type=F32)
        s_ctx = lax.dot_general(qh, ck, NT_DIMS, preferred_element_type=F32)
        p_loc, p_ctx, den = [], [], []
        for rq in range(NA_ROWS):
            rows = slice(rq * GRID_W, (rq + 1) * GRID_W)
            start = NA_KH - 1 - rq
            if start % 2 == 0:
                u = ue_ref[half, :, start * GRID_W:start * GRID_W + NA_WIN]
            else:
                u = uo_ref[half, :, (start - 1) * GRID_W:(start - 1) * GRID_W + NA_WIN]
            r = r0 + rq
            first = jnp.clip(r - NA_KH // 2, 0, n_rows - NA_KH)
            lane0 = (first - r0 + NA_PAD_ROWS) * GRID_W
            in_rows = (klane >= lane0) & (klane < lane0 + NA_KH * GRID_W)
            sl = jnp.where(in_rows, s[rows] + u, NEG)
            sc = s_ctx[rows]
            mx = jnp.maximum(sl.max(axis=-1, keepdims=True), sc.max(axis=-1, keepdims=True))
            el = jnp.exp(sl - mx)
            ec = jnp.exp(sc - mx)
            den.append(el.sum(axis=-1, keepdims=True) + ec.sum(axis=-1, keepdims=True))
            p_loc.append(el.astype(BF16))
            p_ctx.append(ec.astype(BF16))
        acc = (jnp.dot(jnp.concatenate(p_loc, axis=0), v, preferred_element_type=F32)
               + jnp.dot(jnp.concatenate(p_ctx, axis=0), cv, preferred_element_type=F32))
        outs.append(acc / jnp.concatenate(den, axis=0))
    o_ref[...] = jnp.where(lo, outs[0], outs[1])


def _na_bias_tables(rpb):
    cq = jnp.arange(GRID_W)
    ck = jnp.arange(GRID_W)
    dc = jnp.clip(ck[None] - cq[:, None], -(NA_KW - 1), NA_KW - 1) + NA_KW - 1
    cs = jnp.clip(cq - NA_KW // 2, 0, GRID_W - NA_KW)
    col_ok = (ck[None] >= cs[:, None]) & (ck[None] < cs[:, None] + NA_KW)
    t = rpb.astype(F32)[:, :, dc]
    t = jnp.where(col_ok[None, None], t, NEG).transpose(0, 2, 1, 3)
    n_dr = 2 * NA_KH - 1
    blocks = NA_TABLE // GRID_W
    t = jnp.pad(t, ((0, 0), (0, 0), (NA_PAD_ROWS, blocks - n_dr - NA_PAD_ROWS), (0, 0)), constant_values=NEG)
    ue = t.reshape(C_HEADS, GRID_W, NA_TABLE)
    uo = jnp.concatenate([ue[..., GRID_W:], jnp.full((C_HEADS, GRID_W, GRID_W), NEG, F32)], axis=-1)
    return ue, uo


def _na_attention(p, cache_k, cache_v, rpb):
    npairs = C_HEADS // 2
    nrb = DEC_SEQ // NA_QROWS
    half = NA_QROWS // 2
    qbase = NP_TOK // NA_QROWS
    hbase = NP_TOK // half
    kcol = C_W // LANES
    ue, uo = _na_bias_tables(rpb)

    def main_spec(col0):
        return pl.BlockSpec((NA_QROWS, LANES), lambda b, h, r: (qbase + b * nrb + r, col0 + h))

    def side_spec(col0, off):
        return pl.BlockSpec((half, LANES),
                            lambda b, h, r: (hbase + b * 2 * nrb + jnp.clip(2 * r + off, 0, 2 * nrb - 1), col0 + h))

    ctx_spec = pl.BlockSpec((None, PAST, LANES), lambda b, h, r: (b, 0, h))
    tab_spec = pl.BlockSpec((2, GRID_W, NA_TABLE), lambda b, h, r: (h, 0, 0))
    return pl.pallas_call(
        _na_kernel,
        grid=(DEC_BATCH, npairs, nrb),
        in_specs=[main_spec(0),
                  side_spec(kcol, -1), main_spec(kcol), side_spec(kcol, 2),
                  side_spec(2 * kcol, -1), main_spec(2 * kcol), side_spec(2 * kcol, 2),
                  ctx_spec, ctx_spec, tab_spec, tab_spec],
        out_specs=pl.BlockSpec((NA_QROWS, LANES), lambda b, h, r: (b * nrb + r, h)),
        out_shape=jax.ShapeDtypeStruct((NS_TOK, C_W), F32),
        compiler_params=_cparams(3),
        name="na_attn",
    )(p, p, p, p, p, p, p, cache_k.reshape(DEC_BATCH, PAST, C_W), cache_v.reshape(DEC_BATCH, PAST, C_W), ue, uo)


def _route(biased, scores):
    t = biased.shape[1]
    per_group = N_EXPERTS // N_GROUPS
    i8 = lax.broadcasted_iota(jnp.int32, (per_group, t), 0)
    g_rows = []
    for g in range(N_GROUPS):
        bg = biased[g * per_group:(g + 1) * per_group]
        m1 = bg.max(axis=0, keepdims=True)
        first = jnp.where(bg == m1, i8, per_group).min(axis=0, keepdims=True)
        m2 = jnp.where(i8 == first, -jnp.inf, bg).max(axis=0, keepdims=True)
        g_rows.append(m1 + m2)
    g_top = jnp.concatenate(g_rows, axis=0)
    gi = lax.broadcasted_iota(jnp.int32, g_top.shape, 0)
    g_sel = jnp.zeros(g_top.shape, jnp.int32)
    cur = g_top
    for _ in range(TOPK_GROUPS):
        m = cur.max(axis=0, keepdims=True)
        hit = gi == jnp.where(cur == m, gi, N_GROUPS).min(axis=0, keepdims=True)
        g_sel = jnp.where(hit, 1, g_sel)
        cur = jnp.where(hit, -jnp.inf, cur)
    e_sel = jnp.concatenate([jnp.broadcast_to(g_sel[g:g + 1], (per_group, t)) for g in range(N_GROUPS)], axis=0)
    cur = jnp.where(e_sel > 0, biased, NEG)
    ei = lax.broadcasted_iota(jnp.int32, cur.shape, 0)
    ids, gates, hits = [], [], []
    for _ in range(TOP_K):
        m = cur.max(axis=0, keepdims=True)
        f = jnp.where(cur == m, ei, N_EXPERTS).min(axis=0, keepdims=True)
        hit = ei == f
        ids.append(f)
        hits.append(hit)
        gates.append(jnp.where(hit, scores, 0.0).sum(axis=0, keepdims=True))
        cur = jnp.where(hit, -jnp.inf, cur)
    gate = jnp.concatenate(gates, axis=0)
    gate = gate / gate.sum(axis=0, keepdims=True) * ROUTED_SCALE
    return jnp.concatenate(ids, axis=0), gate, hits


def _pack_bf16_pairs(h):
    bits = lax.bitcast_convert_type(h.astype(BF16).astype(F32), jnp.uint32)
    return bits[:, :D // 2] | (bits[:, D // 2:] >> 16)


def _unpack_bf16_pairs(xp):
    hi = lax.bitcast_convert_type(xp & jnp.uint32(0xFFFF0000), F32).astype(BF16)
    lo = lax.bitcast_convert_type(xp << 16, F32).astype(BF16)
    return hi, lo


def _dot_halves(hi, lo, w_ref):
    return (jnp.dot(hi, w_ref[:D // 2, :], preferred_element_type=F32)
            + jnp.dot(lo, w_ref[D // 2:, :], preferred_element_type=F32))


def _outproj_kernel(*refs, n_parts):
    xp_ref, xs_ref = refs[:2]
    part_refs = refs[2:2 + 3 * n_parts]
    gate_ref, shift_ref, scale_ref, g2_ref, rw_ref, rb_ref = refs[2 + 3 * n_parts:8 + 3 * n_parts]
    xo_ref, h_ref, dest_ref, wgt_ref, plan_ref, cnt_ref = refs[8 + 3 * n_parts:]
    step = pl.program_id(0)

    @pl.when(step == 0)
    def _():
        cnt_ref[...] = jnp.zeros_like(cnt_ref)
        plan_ref[...] = jnp.zeros_like(plan_ref)

    y = None
    for t in range(n_parts):
        ap_ref, as_ref, w_ref = part_refs[3 * t:3 * t + 3]
        d = jnp.dot(_pick_rows(ap_ref, as_ref).astype(BF16), w_ref[...], preferred_element_type=F32)
        y = d if y is None else y + d
    x = _pick_rows(xp_ref, xs_ref) + gate_ref[...] * y
    xo_ref[...] = x
    h = _rms(x, g2_ref[...]) * (1.0 + scale_ref[...]) + shift_ref[...]
    h_ref[...] = _pack_bf16_pairs(h)
    h_hi = h.astype(BF16)
    h_lo = (h - h_hi.astype(F32)).astype(BF16)
    rw = rw_ref[...]
    rw_hi = rw.astype(BF16)
    rw_lo = (rw - rw_hi.astype(F32)).astype(BF16)
    logits = (lax.dot_general(rw_hi, h_hi, NT_DIMS, preferred_element_type=F32)
              + lax.dot_general(rw_hi, h_lo, NT_DIMS, preferred_element_type=F32)
              + lax.dot_general(rw_lo, h_hi, NT_DIMS, preferred_element_type=F32))
    scores = jax.nn.sigmoid(logits)
    _, gate, hits = _route(scores + rb_ref[...], scores)
    wgt_ref[...] = gate
    chosen = hits[0]
    for hit in hits[1:]:
        chosen = chosen | hit
    m = jnp.where(chosen, 1.0, 0.0)
    before = (lax.broadcasted_iota(jnp.int32, (TM, TM), 0) < lax.broadcasted_iota(jnp.int32, (TM, TM), 1))
    prefix = jnp.dot(m.astype(BF16), jnp.where(before, 1.0, 0.0).astype(BF16), preferred_element_type=F32)
    e_base = (lax.broadcasted_iota(jnp.int32, (N_EXPERTS, 1), 0) * N_TOK).astype(F32)
    row_all = prefix + (cnt_ref[...] + e_base)
    dest_ref[...] = jnp.concatenate(
        [jnp.where(hit, row_all, 0.0).sum(axis=0, keepdims=True) for hit in hits], axis=0).astype(jnp.int32)
    cnt_ref[...] += m.sum(axis=1, keepdims=True)

    @pl.when(step == pl.num_programs(0) - 1)
    def _():
        _block_plan(cnt_ref[...], plan_ref)


def _block_plan(counts, plan_ref):
    cap_blocks = N_TOK // MOE_BLOCK
    nblk = ((counts.astype(jnp.int32) + (MOE_BLOCK - 1)) // MOE_BLOCK).astype(F32)
    lower = (lax.broadcasted_iota(jnp.int32, (N_EXPERTS, N_EXPERTS), 0)
             >= lax.broadcasted_iota(jnp.int32, (N_EXPERTS, N_EXPERTS), 1))
    cum = jnp.dot(jnp.where(lower, 1.0, 0.0).astype(BF16), jnp.broadcast_to(nblk, (N_EXPERTS, LANES)).astype(BF16),
                  preferred_element_type=F32)[:, :1]
    n_used = cum[N_EXPERTS - 1:, :]
    slot = jnp.minimum(lax.broadcasted_iota(jnp.int32, (1, PLAN_LANES), 1).astype(F32), n_used - 1.0)
    done = cum <= slot
    expert = jnp.where(done, 1.0, 0.0).sum(axis=0, keepdims=True)
    blocks_before = jnp.where(done, nblk, 0.0).sum(axis=0, keepdims=True)
    plan_ref[0:1, :] = (expert * cap_blocks + (slot - blocks_before)).astype(jnp.int32)
    plan_ref[1:2, :] = expert.astype(jnp.int32)
    plan_ref[2:3, :] = jnp.broadcast_to(n_used, (1, PLAN_LANES)).astype(jnp.int32)
    run_end = jnp.where(cum > slot, cum, jnp.inf).min(axis=0, keepdims=True)
    plan_ref[3:4, :] = jnp.minimum(jnp.where(cum <= run_end, 1.0, 0.0).sum(axis=0, keepdims=True),
                                   N_EXPERTS - 1.0).astype(jnp.int32)
    plan_ref[4:5, :] = jnp.where(run_end < n_used, 1, 0).astype(jnp.int32)


def _outproj(x, parts, mod, g2, router_w, router_b):
    in_specs = _pair_specs(D)
    args = [x[0], x[1]]
    for ap, a_s, w in parts:
        width = ap.shape[1]
        in_specs += _pair_specs(width) + [pl.BlockSpec((width, D), lambda i: (0, 0))]
        args += [ap, a_s, w]
    in_specs += [_mod_spec(2), _mod_spec(3), _mod_spec(4),
                 pl.BlockSpec((1, D), lambda i: (0, 0)),
                 pl.BlockSpec((N_EXPERTS, D), lambda i: (0, 0)),
                 pl.BlockSpec((N_EXPERTS, 1), lambda i: (0, 0))]
    args += [mod, mod, mod, g2.reshape(1, D), router_w.T, router_b.reshape(N_EXPERTS, 1)]
    return pl.pallas_call(
        functools.partial(_outproj_kernel, n_parts=len(parts)),
        grid=(N_TOK // TM,),
        in_specs=in_specs,
        out_specs=[pl.BlockSpec((TM, D), lambda i: (i, 0)),
                   pl.BlockSpec((TM, D // 2), lambda i: (i, 0)),
                   pl.BlockSpec((TOP_K, TM), lambda i: (0, i)),
                   pl.BlockSpec((TOP_K, TM), lambda i: (0, i)),
                   pl.BlockSpec((8, PLAN_LANES), lambda i: (0, 0))],
        out_shape=[jax.ShapeDtypeStruct((N_TOK, D), F32),
                   jax.ShapeDtypeStruct((N_TOK, D // 2), jnp.uint32),
                   jax.ShapeDtypeStruct((TOP_K, N_TOK), jnp.int32),
                   jax.ShapeDtypeStruct((TOP_K, N_TOK), F32),
                   jax.ShapeDtypeStruct((8, PLAN_LANES), jnp.int32)],
        scratch_shapes=[pltpu.VMEM((N_EXPERTS, 1), F32)],
        compiler_params=_cparams(1),
        name="outproj_router",
    )(*args)


def _experts_kernel(br_ref, be_ref, nu_ref, ne_ref, hn_ref, x_hbm, w1_hbm, w3_hbm, w2_hbm, o_ref,
                    w1b, w3b, w2b, w1f, w3f, w2f, xbuf, xsem, wsem, *, layer):
    i = pl.program_id(0)
    n_steps = pl.num_programs(0)
    e = be_ref[i]
    prev = be_ref[jnp.maximum(i - 1, 0)]

    def x_copy(step):
        slot = step % EXPERT_X_SLOTS
        rows = pl.ds(pl.multiple_of(br_ref[step] * MOE_BLOCK, MOE_BLOCK), MOE_BLOCK)
        return pltpu.make_async_copy(x_hbm.at[rows], xbuf.at[slot], xsem.at[slot])

    @pl.when(i == 0)
    def _():
        x_copy(0).start()
        x_copy(1).start()

    @pl.when(i + 2 < n_steps)
    def _():
        x_copy(i + 2).start()

    x_copy(i).wait()

    def w_copies(expert):
        return [pltpu.make_async_copy(src.at[layer, expert], dst, wsem.at[k])
                for k, (src, dst) in enumerate(((w1_hbm, w1f), (w3_hbm, w3f), (w2_hbm, w2f)))]

    @pl.when(i == 0)
    def _():
        for c in w_copies(e):
            c.start()

    @pl.when((i == 0) | (e != prev))
    def _():
        for c in w_copies(e):
            c.wait()
        w1b[...] = w1f[...].astype(BF16)
        w3b[...] = w3f[...].astype(BF16)
        w2b[...] = w2f[...].astype(BF16)

        @pl.when(hn_ref[i] > 0)
        def _():
            for c in w_copies(ne_ref[i]):
                c.start()

    @pl.when(i < nu_ref[0])
    def _():
        hi, lo = _unpack_bf16_pairs(xbuf[i % EXPERT_X_SLOTS])
        a = _dot_halves(hi, lo, w1b)
        b = _dot_halves(hi, lo, w3b)
        h = (_silu(a) * b).astype(BF16)
        o_ref[...] = _pack_bf16_pairs(jnp.dot(h, w2b[...], preferred_element_type=F32))


def _experts(plan, x_rows, w1, w3, w2, layer):
    any_spec = pl.BlockSpec(memory_space=pl.ANY)
    return pl.pallas_call(
        functools.partial(_experts_kernel, layer=layer),
        grid_spec=pltpu.PrefetchScalarGridSpec(
            num_scalar_prefetch=5,
            grid=(N_MOE_BLOCKS,),
            in_specs=[any_spec, any_spec, any_spec, any_spec],
            out_specs=pl.BlockSpec((MOE_BLOCK, D // 2), lambda i, br, be, nu, ne, hn: (br[i], 0)),
            scratch_shapes=[pltpu.VMEM((D, FF), BF16), pltpu.VMEM((D, FF), BF16), pltpu.VMEM((FF, D), BF16),
                            pltpu.VMEM((D, FF), F32), pltpu.VMEM((D, FF), F32), pltpu.VMEM((FF, D), F32),
                            pltpu.VMEM((EXPERT_X_SLOTS, MOE_BLOCK, D // 2), jnp.uint32),
                            pltpu.SemaphoreType.DMA((EXPERT_X_SLOTS,)),
                            pltpu.SemaphoreType.DMA((3,))]),
        out_shape=jax.ShapeDtypeStruct(x_rows.shape, jnp.uint32),
        compiler_params=_cparams(1),
        name="experts",
    )(plan[0], plan[1], plan[2, :1], plan[3], plan[4], x_rows, w1, w3, w2)


SC_CORES = 2
SC_SUBCORES = 16
SC_WORKERS = SC_CORES * SC_SUBCORES
SC_CHUNK_BYTES = 128 * 1024
SC_SLOTS = 3


def _sc_scatter(rows, dest, n_out):
    n_rows, width = rows.shape
    picks = dest.shape[0]
    chunk = SC_CHUNK_BYTES // (4 * width)
    per_worker = n_rows // SC_WORKERS
    n_chunks = per_worker // chunk
    assert per_worker * SC_WORKERS == n_rows and n_chunks * chunk == per_worker and n_chunks % 2 == 0
    mesh = plsc.VectorSubcoreMesh(core_axis_name="c", subcore_axis_name="s")

    @functools.partial(
        pl.kernel, mesh=mesh,
        out_type=jax.ShapeDtypeStruct((n_out, width), rows.dtype),
        scratch_types=[pltpu.VMEM((picks, n_chunks, chunk), jnp.int32),
                       pltpu.VMEM((2, chunk, width), rows.dtype),
                       pltpu.SemaphoreType.DMA((2,)),
                       pltpu.SemaphoreType.DMA((2,))])
    def scatter(r_hbm, d_hbm, o_hbm, idx_v, rows_v, lsem, ssem):
        worker = lax.axis_index("s") * SC_CORES + lax.axis_index("c")
        base = worker * per_worker
        for k in range(picks):
            pltpu.sync_copy(d_hbm.at[k, worker], idx_v.at[k])

        def load_copy(c, b):
            src = pl.ds(pl.multiple_of(base + c * chunk, chunk), chunk)
            return pltpu.make_async_copy(r_hbm.at[src], rows_v.at[b], lsem.at[b])

        def store_copy(c, b, k):
            return pltpu.make_async_copy(rows_v.at[b], o_hbm.at[idx_v.at[k, c]], ssem.at[b])

        load_copy(0, 0).start()

        @pl.loop(0, n_chunks, step=2)
        def _(c0):
            for b in range(2):
                c = c0 + b
                load_copy(c, b).wait()
                for k in range(picks):
                    store_copy(c, b, k).start()

                @pl.when(c > 0)
                def _():
                    for k in range(picks):
                        store_copy(c - 1, 1 - b, k).wait()

                @pl.when(c + 1 < n_chunks)
                def _():
                    load_copy(c + 1, 1 - b).start()

        for k in range(picks):
            store_copy(n_chunks - 1, 1, k).wait()

    return scatter(rows, dest.reshape(picks, SC_WORKERS, n_chunks, chunk))


def _sc_gather(table, idx):
    n_idx = idx.shape[0]
    width = table.shape[1]
    chunk = SC_CHUNK_BYTES // (4 * width)
    per_worker = n_idx // SC_WORKERS
    n_chunks = per_worker // chunk
    ahead = SC_SLOTS - 1
    assert per_worker * SC_WORKERS == n_idx and n_chunks * chunk == per_worker and n_chunks % SC_SLOTS == 0
    mesh = plsc.VectorSubcoreMesh(core_axis_name="c", subcore_axis_name="s")

    @functools.partial(
        pl.kernel, mesh=mesh,
        out_type=jax.ShapeDtypeStruct((n_idx, width), table.dtype),
        scratch_types=[pltpu.VMEM((per_worker,), jnp.int32),
                       pltpu.VMEM((SC_SLOTS, chunk, width), table.dtype),
                       pltpu.SemaphoreType.DMA((SC_SLOTS,)),
                       pltpu.SemaphoreType.DMA((SC_SLOTS,))])
    def gather(t_hbm, i_hbm, o_hbm, idx_v, rows_v, gsem, wsem):
        worker = lax.axis_index("s") * SC_CORES + lax.axis_index("c")
        base = worker * per_worker
        pltpu.sync_copy(i_hbm.at[pl.ds(pl.multiple_of(base, chunk), per_worker)], idx_v)

        def gather_copy(c, b):
            ids = idx_v.at[pl.ds(pl.multiple_of(c * chunk, chunk), chunk)]
            return pltpu.make_async_copy(t_hbm.at[ids], rows_v.at[b], gsem.at[b])

        def write_copy(c, b):
            rows = pl.ds(pl.multiple_of(base + c * chunk, chunk), chunk)
            return pltpu.make_async_copy(rows_v.at[b], o_hbm.at[rows], wsem.at[b])

        for c in range(ahead):
            gather_copy(c, c).start()

        @pl.loop(0, n_chunks, step=SC_SLOTS)
        def _(c0):
            for b in range(SC_SLOTS):
                c = c0 + b
                refill = (b + ahead) % SC_SLOTS
                gather_copy(c, b).wait()
                write_copy(c, b).start()

                @pl.when(c > 0)
                def _():
                    write_copy(c - 1, refill).wait()

                @pl.when(c + ahead < n_chunks)
                def _():
                    gather_copy(c + ahead, refill).start()

        write_copy(n_chunks - 1, (n_chunks - 1) % SC_SLOTS).wait()

    return gather(table, idx)


TC = 512


def _combine_kernel(x_ref, h_ref, y_ref, wgt_ref, gate_ref, w1_ref, w3_ref, w2_ref, fg_ref, o_ref, *, final):
    hi, lo = _unpack_bf16_pairs(h_ref[...])
    a = _dot_halves(hi, lo, w1_ref)
    b = _dot_halves(hi, lo, w3_ref)
    ffn = jnp.dot((_silu(a) * b).astype(BF16), w2_ref[...], preferred_element_type=F32)
    wgt = wgt_ref[...]
    r_hi = None
    r_lo = None
    for k in range(TOP_K):
        yk = y_ref[k]
        w = wgt[:, k:k + 1]
        t_hi = lax.bitcast_convert_type(yk & jnp.uint32(0xFFFF0000), F32) * w
        t_lo = lax.bitcast_convert_type(yk << 16, F32) * w
        r_hi = t_hi if r_hi is None else r_hi + t_hi
        r_lo = t_lo if r_lo is None else r_lo + t_lo
    x = x_ref[...] + gate_ref[...] * (ffn + jnp.concatenate([r_hi, r_lo], axis=1))
    o_ref[...] = _rms(x, fg_ref[...]) if final else x


def _combine(x, h, y_rows, wgt, mod, sw1, sw3, sw2, final_g, final):
    weights = (sw1.astype(BF16), sw3.astype(BF16), sw2.astype(BF16), final_g.reshape(1, D))

    def rows_from(first_row, n_rows):
        b0 = first_row // TC
        return pl.pallas_call(
            functools.partial(_combine_kernel, final=final),
            grid=(n_rows // TC,),
            in_specs=[pl.BlockSpec((TC, D), lambda i: (i + b0, 0)),
                      pl.BlockSpec((TC, D // 2), lambda i: (i + b0, 0)),
                      pl.BlockSpec((TOP_K, TC, D // 2), lambda i: (0, i + b0, 0)),
                      pl.BlockSpec((TC, TOP_K), lambda i: (i + b0, 0)),
                      _mod_spec(5, TC, b0),
                      pl.BlockSpec((D, FF), lambda i: (0, 0)),
                      pl.BlockSpec((D, FF), lambda i: (0, 0)),
                      pl.BlockSpec((FF, D), lambda i: (0, 0)),
                      pl.BlockSpec((1, D), lambda i: (0, 0))],
            out_specs=pl.BlockSpec((TC, D), lambda i: (i, 0)),
            out_shape=jax.ShapeDtypeStruct((n_rows, D), F32),
            compiler_params=_cparams(1),
            name="combine",
        )(x, h, y_rows, wgt, mod, *weights)

    return rows_from(0, NP_TOK), rows_from(NP_TOK, NS_TOK)


def kernel(x_prompt, x_sample, cache_a_k, cache_a_v, state_ret_fwd, state_ret_bwd, cache_c_k, cache_c_v,
           c, c_ctx, norm1_g, norm2_g, ada_w, ada_b, even_w_in, even_w_out, sink_a, ret_decay_fwd,
           ret_decay_bwd, ret_gn_g, odd_w_in, odd_w_out, na_rpb, router_w, router_b, exp_w1, exp_w3,
           exp_w2, sh_w1, sh_w3, sh_w2, final_g):
    x = (x_prompt.reshape(NP_TOK, D), x_sample.reshape(NS_TOK, D))
    cc = jnp.concatenate([c_ctx[None], c, jnp.zeros((8 - 1 - DEC_BATCH, D), F32)], axis=0)
    rope = _rope_tables()
    outs = {}
    for l in range(2):
        mod = _ada(cc, ada_w, ada_b, l)
        if l == 0:
            p = _inproj(x, norm1_g[l], mod, even_w_in[0].astype(BF16), rope, A_Q + A_KV)
            oa_p = _ctx_gqa(p, sink_a[0])
            oa_s = _win_attention(p, cache_a_k[:, 0], cache_a_v[:, 0], sink_a[0])
            zero = jnp.zeros((BATCH, B_HEADS // 2, LANES, LANES), F32)
            ob_p, sf, sb = _retention(p, 0, BATCH, SEQ, ret_decay_fwd[0], ret_decay_bwd[0], ret_gn_g[0], zero, zero)
            ob_s, _, _ = _retention(p, NP_TOK, DEC_BATCH, DEC_SEQ, ret_decay_fwd[0], ret_decay_bwd[0], ret_gn_g[0],
                                    _blockdiag_states(state_ret_fwd[:, 0]), _blockdiag_states(state_ret_bwd[:, 0]))
            w_out = even_w_out[0].astype(BF16)
            parts = [(oa_p, oa_s, w_out[:A_Q]), (ob_p, ob_s, w_out[A_Q:])]
            outs["a_k"] = p[:NP_TOK, A_Q:A_Q + A_KV].reshape(BATCH, 1, SEQ, A_KV_HEADS, HD)
            outs["a_v"] = p[:NP_TOK, A_Q + A_KV:A_Q + 2 * A_KV].reshape(BATCH, 1, SEQ, A_KV_HEADS, HD)
            outs["r_f"] = _diag_states(sf).reshape(BATCH, 1, B_HEADS, HD, HD)
            outs["r_b"] = _diag_states(sb).reshape(BATCH, 1, B_HEADS, HD, HD)
        else:
            p = _inproj(x, norm1_g[l], mod, odd_w_in[0].astype(BF16), rope, 0)
            o_p = _ctx_mha(p)
            o_s = _na_attention(p, cache_c_k[:, 0], cache_c_v[:, 0], na_rpb[0])
            parts = [(o_p, o_s, odd_w_out[0].astype(BF16))]
            outs["c_k"] = p[:NP_TOK, C_W:2 * C_W].reshape(BATCH, 1, SEQ, C_HEADS, HD)
            outs["c_v"] = p[:NP_TOK, 2 * C_W:3 * C_W].reshape(BATCH, 1, SEQ, C_HEADS, HD)
        x_mid, h, dest, gate_t, plan = _outproj(x, parts, mod, norm2_g[l], router_w[l], router_b[l])
        y = _experts(plan, _sc_scatter(h, dest, N_EXPERTS * N_TOK), exp_w1, exp_w3, exp_w2, l)
        y_rows = _sc_gather(y, dest.reshape(N_ASSIGN)).reshape(TOP_K, N_TOK, D // 2)
        x = _combine(x_mid, h, y_rows, gate_t.T, mod, sh_w1[l], sh_w3[l], sh_w2[l], final_g, final=(l == 1))
    y_prompt = x[0].reshape(BATCH, SEQ, D)
    y_sample = x[1].reshape(DEC_BATCH, DEC_SEQ, D)
    return (y_prompt, y_sample, outs["a_k"], outs["a_v"], outs["r_f"], outs["r_b"], outs["c_k"], outs["c_v"])
```

```python
import functools

import jax
import jax.numpy as jnp
from jax import lax
from jax.experimental import pallas as pl
from jax.experimental.pallas import tpu as pltpu
from jax.experimental.pallas import tpu_sc as plsc

F32 = jnp.float32
BF16 = jnp.bfloat16
HIGHEST = lax.Precision.HIGHEST

D = 1024
BATCH = 32
SEQ = 256
DEC_BATCH = 4
DEC_SEQ = 4096
PAST = 256
GRID_W = 64
HD = 64
EPS = 1e-6
NEG = -1e30
ROPE_BASE = 10000.0
A_HEADS = 8
A_KV_HEADS = 2
A_Q = A_HEADS * HD
A_KV = A_KV_HEADS * HD
B_HEADS = 8
B_W = B_HEADS * HD
EVEN_IN = A_Q + 2 * A_KV + 4 * B_W
C_HEADS = 16
C_W = C_HEADS * HD
NA_KH = 8
NA_KW = 16
N_EXPERTS = 64
TOP_K = 8
N_GROUPS = 8
TOPK_GROUPS = 4
FF = 256
ROUTED_SCALE = 2.5
MOE_BLOCK = 1024
EXPERT_X_SLOTS = 3
RET_CHUNK = 256
RET_UNROLL = 4
RET_NORM_ROWS = 1024
A_WINDOW = 128

NP_TOK = BATCH * SEQ
NS_TOK = DEC_BATCH * DEC_SEQ
N_TOK = NP_TOK + NS_TOK
N_ASSIGN = N_TOK * TOP_K
N_MOE_BLOCKS = (N_ASSIGN + N_EXPERTS * (MOE_BLOCK - 1) + MOE_BLOCK - 1) // MOE_BLOCK
PLAN_LANES = 512
assert N_TOK % MOE_BLOCK == 0 and N_MOE_BLOCKS <= PLAN_LANES

LANES = 128
TM = 512
NA_ROWS = 8
V7X_VMEM_LIMIT = 56 * 1024 * 1024

NT_DIMS = (((1,), (1,)), ((), ()))


def _cparams(n_axes, vmem=V7X_VMEM_LIMIT):
    return pltpu.CompilerParams(dimension_semantics=("arbitrary",) * n_axes, vmem_limit_bytes=vmem)


def _seg_of_block(i, rows):
    row0 = i * rows
    return jnp.where(row0 < NP_TOK, 0, 1 + (row0 - NP_TOK) // DEC_SEQ)


def _mod_spec(chunk, rows=TM, first_block=0):
    return pl.BlockSpec((None, 1, D), lambda i: (_seg_of_block(i + first_block, rows), 0, chunk))


def _pair_specs(width, rows=TM):
    npb = NP_TOK // rows
    nsb = NS_TOK // rows
    return [pl.BlockSpec((rows, width), lambda i: (jnp.minimum(i, npb - 1), 0)),
            pl.BlockSpec((rows, width), lambda i: (jnp.clip(i - npb, 0, nsb - 1), 0))]


def _pick_rows(p_ref, s_ref, rows=TM):
    return jnp.where(pl.program_id(0) < NP_TOK // rows, p_ref[...], s_ref[...])


def _silu(x):
    return x * jax.nn.sigmoid(x)


def _rms(x, g):
    return x * lax.rsqrt(jnp.mean(x * x, axis=-1, keepdims=True) + EPS) * g


def _lane_lo():
    return lax.broadcasted_iota(jnp.int32, (1, LANES), 1) < HD


def _ada_kernel(c_ref, w_ref, b_ref, o_ref):
    a = _silu(c_ref[...])
    o_ref[...] = jnp.dot(a, w_ref[...], preferred_element_type=F32, precision=HIGHEST) + b_ref[...]


def _ada(cc, w, b, layer):
    tn = 1536
    out = pl.pallas_call(
        _ada_kernel,
        grid=(6 * D // tn,),
        in_specs=[pl.BlockSpec((8, D), lambda j: (0, 0)),
                  pl.BlockSpec((None, D, tn), lambda j: (layer, 0, j)),
                  pl.BlockSpec((None, 1, tn), lambda j: (layer, 0, j))],
        out_specs=pl.BlockSpec((8, tn), lambda j: (0, j)),
        out_shape=jax.ShapeDtypeStruct((8, 6 * D), F32),
        compiler_params=_cparams(1),
        name="ada",
    )(cc, w, b.reshape(b.shape[0], 1, 6 * D))
    return out.reshape(8, 1, 6 * D)


def _inproj_kernel(xp_ref, xs_ref, g_ref, shift_ref, scale_ref, w_ref, cos_ref, sin_ref, o_ref, *, rope_cols):
    h = _rms(_pick_rows(xp_ref, xs_ref), g_ref[...]) * (1.0 + scale_ref[...]) + shift_ref[...]
    o = jnp.dot(h.astype(BF16), w_ref[...], preferred_element_type=F32)
    if rope_cols:
        cos = cos_ref[...]
        sin = sin_ref[...]
        lane = lax.broadcasted_iota(jnp.int32, (1, LANES), 1)
        first = (lane % 32) < 16
        for c in range(rope_cols // LANES):
            oc = o[:, c * LANES:(c + 1) * LANES]
            partner = jnp.where(first, pltpu.roll(oc, LANES - 16, 1), pltpu.roll(oc, 16, 1))
            o_ref[:, c * LANES:(c + 1) * LANES] = oc * cos + partner * sin
        o_ref[:, rope_cols:] = o[:, rope_cols:]
    else:
        o_ref[...] = o


def _rope_tables():
    half = HD // 2
    inv = ROPE_BASE ** (-jnp.arange(0, half, 2, dtype=F32) / half)
    t = jnp.arange(DEC_SEQ)
    ang_r = (t // GRID_W).astype(F32)[:, None] * inv[None]
    ang_c = (t % GRID_W).astype(F32)[:, None] * inv[None]

    def head(fn_r, fn_c, sign):
        return jnp.concatenate([sign[0] * fn_r, sign[1] * fn_r, sign[0] * fn_c, sign[1] * fn_c], axis=-1)

    cos = head(jnp.cos(ang_r), jnp.cos(ang_c), (1.0, 1.0))
    sin = head(jnp.sin(ang_r), jnp.sin(ang_c), (-1.0, 1.0))
    cos = jnp.concatenate([jnp.ones((TM, HD), F32), cos], axis=0)
    sin = jnp.concatenate([jnp.zeros((TM, HD), F32), sin], axis=0)
    return jnp.tile(cos, (1, 2)), jnp.tile(sin, (1, 2))


def _inproj(x, g, mod, w_bf16, rope, rope_cols):
    n_out = w_bf16.shape[1]
    npb = NP_TOK // TM
    spb = DEC_SEQ // TM

    def rope_map(i):
        return (jnp.where(i < npb, 0, 1 + (i - npb) % spb), 0)

    return pl.pallas_call(
        functools.partial(_inproj_kernel, rope_cols=rope_cols),
        grid=(N_TOK // TM,),
        in_specs=_pair_specs(D) + [
                  pl.BlockSpec((1, D), lambda i: (0, 0)),
                  _mod_spec(0), _mod_spec(1),
                  pl.BlockSpec((D, n_out), lambda i: (0, 0)),
                  pl.BlockSpec((TM, LANES), rope_map),
                  pl.BlockSpec((TM, LANES), rope_map)],
        out_specs=pl.BlockSpec((TM, n_out), lambda i: (i, 0)),
        out_shape=jax.ShapeDtypeStruct((N_TOK, n_out), F32),
        compiler_params=_cparams(1),
        name="inproj",
    )(x[0], x[1], g.reshape(1, D), mod, mod, w_bf16, rope[0], rope[1])


def _softmax_av(s_list, v_list, sink=None):
    mx = s_list[0].max(axis=-1, keepdims=True)
    for s in s_list[1:]:
        mx = jnp.maximum(mx, s.max(axis=-1, keepdims=True))
    if sink is not None:
        mx = jnp.maximum(mx, sink)
    den = jnp.exp(sink - mx) if sink is not None else 0.0
    acc = None
    for s, v in zip(s_list, v_list):
        p = jnp.exp(s - mx)
        den = den + p.sum(axis=-1, keepdims=True)
        pv = jnp.dot(p.astype(BF16), v, preferred_element_type=F32)
        acc = pv if acc is None else acc + pv
    return acc / den


def _dup_half(x, j, lo):
    xr = pltpu.roll(x, HD, 1)
    return jnp.where(lo, x, xr) if j == 0 else jnp.where(lo, xr, x)


def _stack_heads(q_ref, heads, lo, scale):
    parts = []
    for h in heads:
        qp = q_ref[:, (h // 2) * LANES:(h // 2 + 1) * LANES]
        keep = lo if h % 2 == 0 else jnp.logical_not(lo)
        parts.append(jnp.where(keep, qp, 0.0) * scale)
    return jnp.concatenate(parts, axis=0).astype(BF16)


def _sink_column(sink_ref, heads, rows):
    return jnp.concatenate([jnp.full((rows, 1), sink_ref[h], F32) for h in heads], axis=0)


def _ctx_gqa_kernel(sink_ref, q_ref, k_ref, v_ref, o_ref):
    lo = _lane_lo()
    k = k_ref[...]
    v = v_ref[...]
    group = A_HEADS // A_KV_HEADS
    scores = []
    for j in range(A_KV_HEADS):
        q = _stack_heads(q_ref, list(range(group * j, group * (j + 1))), lo, HD ** -0.5)
        scores.append(lax.dot_general(q, _dup_half(k, j, lo).astype(BF16), NT_DIMS, preferred_element_type=F32))
    s = jnp.concatenate(scores, axis=0)
    sink = _sink_column(sink_ref, list(range(A_HEADS)), SEQ)
    mx = jnp.maximum(s.max(axis=-1, keepdims=True), sink)
    e = jnp.exp(s - mx)
    den = jnp.exp(sink - mx) + e.sum(axis=-1, keepdims=True)
    e = e.astype(BF16)
    rows_per_group = group * SEQ
    for j in range(A_KV_HEADS):
        rows = slice(j * rows_per_group, (j + 1) * rows_per_group)
        o = jnp.dot(e[rows], _dup_half(v, j, lo).astype(BF16), preferred_element_type=F32) / den[rows]
        for t in range(group // 2):
            pair = (group * j) // 2 + t
            o_ref[:, pair * LANES:(pair + 1) * LANES] = jnp.where(
                lo, o[(2 * t) * SEQ:(2 * t + 1) * SEQ], o[(2 * t + 1) * SEQ:(2 * t + 2) * SEQ])


def _ctx_gqa(p, sink):
    return pl.pallas_call(
        _ctx_gqa_kernel,
        grid_spec=pltpu.PrefetchScalarGridSpec(
            num_scalar_prefetch=1,
            grid=(BATCH,),
            in_specs=[pl.BlockSpec((SEQ, A_Q), lambda b, s: (b, 0)),
                      pl.BlockSpec((SEQ, A_KV), lambda b, s: (b, A_Q // A_KV)),
                      pl.BlockSpec((SEQ, A_KV), lambda b, s: (b, A_Q // A_KV + 1))],
            out_specs=pl.BlockSpec((SEQ, A_Q), lambda b, s: (b, 0))),
        out_shape=jax.ShapeDtypeStruct((NP_TOK, A_Q), F32),
        compiler_params=_cparams(1),
        name="ctx_gqa",
    )(sink, p, p, p)


def _win_kernel(sink_ref, q_ref, kp_ref, kc_ref, kn_ref, vp_ref, vc_ref, vn_ref, ck_ref, cv_ref, o_ref):
    i = pl.program_id(1)
    lo = _lane_lo()
    k = jnp.concatenate([kp_ref[...], kc_ref[...], kn_ref[...]], axis=0)
    v = jnp.concatenate([vp_ref[...], vc_ref[...], vn_ref[...]], axis=0)
    ck = ck_ref[...]
    cv = cv_ref[...]
    group = A_HEADS // A_KV_HEADS
    n_keys = WIN_Q + 2 * A_WINDOW
    qpos = i * WIN_Q + lax.broadcasted_iota(jnp.int32, (WIN_Q, n_keys), 0)
    kpos = i * WIN_Q - A_WINDOW + lax.broadcasted_iota(jnp.int32, (WIN_Q, n_keys), 1)
    valid = (jnp.abs(kpos - qpos) <= A_WINDOW) & (kpos >= 0) & (kpos < DEC_SEQ)
    valid = jnp.concatenate([valid] * group, axis=0)
    s_loc, s_ctx, values = [], [], []
    for j in range(A_KV_HEADS):
        heads = list(range(group * j, group * (j + 1)))
        kd = _dup_half(k, j, lo).astype(BF16)
        ckd = _dup_half(ck, j, lo).astype(BF16)
        values.append((_dup_half(v, j, lo).astype(BF16), _dup_half(cv, j, lo).astype(BF16)))
        q = _stack_heads(q_ref, heads, lo, HD ** -0.5)
        s_loc.append(jnp.where(valid, lax.dot_general(q, kd, NT_DIMS, preferred_element_type=F32), NEG))
        s_ctx.append(lax.dot_general(q, ckd, NT_DIMS, preferred_element_type=F32))
    s_loc = jnp.concatenate(s_loc, axis=0)
    s_ctx = jnp.concatenate(s_ctx, axis=0)
    sink = _sink_column(sink_ref, list(range(A_HEADS)), WIN_Q)
    mx = jnp.maximum(jnp.maximum(s_loc.max(axis=-1, keepdims=True), s_ctx.max(axis=-1, keepdims=True)), sink)
    p_loc = jnp.exp(s_loc - mx)
    p_ctx = jnp.exp(s_ctx - mx)
    den = p_loc.sum(axis=-1, keepdims=True) + p_ctx.sum(axis=-1, keepdims=True) + jnp.exp(sink - mx)
    p_loc = p_loc.astype(BF16)
    p_ctx = p_ctx.astype(BF16)
    rows_per_group = group * WIN_Q
    for j, (vd, cvd) in enumerate(values):
        rows = slice(j * rows_per_group, (j + 1) * rows_per_group)
        o = (jnp.dot(p_loc[rows], vd, preferred_element_type=F32)
             + jnp.dot(p_ctx[rows], cvd, preferred_element_type=F32)) / den[rows]
        for t in range(group // 2):
            pair = (group * j) // 2 + t
            o_ref[:, pair * LANES:(pair + 1) * LANES] = jnp.where(
                lo, o[(2 * t) * WIN_Q:(2 * t + 1) * WIN_Q], o[(2 * t + 1) * WIN_Q:(2 * t + 2) * WIN_Q])


WIN_Q = 256


def _win_attention(p, cache_k, cache_v, sink):
    nblk = DEC_SEQ // WIN_Q
    side = WIN_Q // A_WINDOW
    nside = DEC_SEQ // A_WINDOW
    base = NP_TOK // WIN_Q
    side_base = NP_TOK // A_WINDOW
    kcol = A_Q // A_KV

    def main_spec(col):
        return pl.BlockSpec((WIN_Q, A_KV), lambda b, i, s: (base + b * nblk + i, col))

    def side_spec(col, off):
        return pl.BlockSpec((A_WINDOW, A_KV),
                            lambda b, i, s: (side_base + b * nside + jnp.clip(side * i + off, 0, nside - 1), col))

    ctx_spec = pl.BlockSpec((None, PAST, A_KV), lambda b, i, s: (b, 0, 0))
    return pl.pallas_call(
        _win_kernel,
        grid_spec=pltpu.PrefetchScalarGridSpec(
            num_scalar_prefetch=1,
            grid=(DEC_BATCH, nblk),
            in_specs=[pl.BlockSpec((WIN_Q, A_Q), lambda b, i, s: (base + b * nblk + i, 0)),
                      side_spec(kcol, -1), main_spec(kcol), side_spec(kcol, side),
                      side_spec(kcol + 1, -1), main_spec(kcol + 1), side_spec(kcol + 1, side),
                      ctx_spec, ctx_spec],
            out_specs=pl.BlockSpec((WIN_Q, A_Q), lambda b, i, s: (b * nblk + i, 0))),
        out_shape=jax.ShapeDtypeStruct((NS_TOK, A_Q), F32),
        compiler_params=_cparams(2),
        name="win_attn",
    )(sink, p, p, p, p, p, p, p, cache_k.reshape(DEC_BATCH, PAST, A_KV), cache_v.reshape(DEC_BATCH, PAST, A_KV))


def _ret_kernel(df_ref, db_ref, q_ref, k_ref, v_ref, g_ref, gn_ref, s0f_ref, s0b_ref,
                o_ref, sf_ref, sb_ref, of_scr, ob_scr, *, length):
    c_len = RET_CHUNK
    n = length // c_len
    lo = _lane_lo()
    hi = jnp.logical_not(lo)
    row = lax.broadcasted_iota(jnp.int32, (c_len, c_len), 0)
    col = lax.broadcasted_iota(jnp.int32, (c_len, c_len), 1)
    rowp = lax.broadcasted_iota(jnp.int32, (LANES, LANES), 0)
    colp = lax.broadcasted_iota(jnp.int32, (LANES, LANES), 1)
    blockdiag = (rowp < HD) == (colp < HD)
    idx = lax.broadcasted_iota(jnp.int32, (c_len, 1), 0).astype(F32)

    def direction(dec_ref, forward):
        lg = -jnp.exp(dec_ref[...])
        diff = (row - col) if forward else (col - row)
        keep = (diff >= 0) if forward else (diff > 0)
        dist = jnp.maximum(diff, 0).astype(F32)
        dm = jnp.concatenate([jnp.where(keep, jnp.exp(dist * lg[:, off:off + 1]), 0.0) for off in (0, HD)], axis=0)
        if forward:
            xi = jnp.exp((idx + 1.0) * lg)
            zeta = jnp.exp((c_len - 1.0 - idx) * lg)
        else:
            xi = jnp.exp((c_len - idx) * lg)
            zeta = jnp.exp(idx * lg)
        return dm, xi, zeta, jnp.exp(c_len * lg)

    def chunk(c, state, consts):
        dm, xi, zeta, gch = consts
        rows = pl.ds(pl.multiple_of(c * c_len, c_len), c_len)
        qc = q_ref[rows, :]
        kc = k_ref[rows, :] * HD ** -0.5
        vc = v_ref[rows, :].astype(BF16)
        kb = kc.astype(BF16)
        q2 = jnp.concatenate([jnp.where(lo, qc, 0.0), jnp.where(hi, qc, 0.0)], axis=0).astype(BF16)
        inner = lax.dot_general(q2, kb, NT_DIMS, preferred_element_type=F32) * dm
        kz_t = (kc * zeta).T
        res = jnp.dot(jnp.concatenate([inner, kz_t], axis=0).astype(BF16), vc, preferred_element_type=F32)
        cross = jnp.dot(qc.astype(BF16), state.astype(BF16), preferred_element_type=F32) * xi
        o = jnp.where(lo, res[:c_len], res[c_len:2 * c_len]) + cross
        state = gch * state + jnp.where(blockdiag, res[2 * c_len:], 0.0)
        return rows, o, state

    cf = direction(df_ref, True)
    cb = direction(db_ref, False)

    def scan_body(t, states):
        rows_f, o_f, state_f = chunk(t, states[0], cf)
        of_scr[rows_f, :] = o_f
        rows_b, o_b, state_b = chunk(n - 1 - t, states[1], cb)
        ob_scr[rows_b, :] = o_b
        return state_f, state_b

    state_f, state_b = lax.fori_loop(0, n, scan_body, (s0f_ref[...], s0b_ref[...]), unroll=min(n, RET_UNROLL))
    sf_ref[...] = state_f
    sb_ref[...] = state_b

    gn = gn_ref[...]
    norm_rows = min(RET_NORM_ROWS, length)
    n_norm = length // norm_rows

    def per_head(x):
        a = jnp.where(lo, x, 0.0).sum(axis=-1, keepdims=True)
        b = jnp.where(hi, x, 0.0).sum(axis=-1, keepdims=True)
        return jnp.where(lo, a, b) * (1.0 / HD)

    def norm_body(t, carry):
        rows = pl.ds(pl.multiple_of(t * norm_rows, norm_rows), norm_rows)
        o = of_scr[rows, :] + ob_scr[rows, :]
        d = o - per_head(o)
        y = d * lax.rsqrt(per_head(d * d) + EPS) * gn
        o_ref[rows, :] = _silu(g_ref[rows, :]) * y
        return carry

    lax.fori_loop(0, n_norm, norm_body, 0)


def _pair_lanes(v):
    return jnp.repeat(v.astype(F32), HD).reshape(B_HEADS // 2, 1, LANES)


def _blockdiag_states(s):
    b = s.shape[0]
    s = s.astype(F32).reshape(b, B_HEADS // 2, 2, HD, HD)
    z = jnp.zeros_like(s[:, :, 0])
    top = jnp.concatenate([s[:, :, 0], z], axis=-1)
    bot = jnp.concatenate([z, s[:, :, 1]], axis=-1)
    return jnp.concatenate([top, bot], axis=-2)


def _diag_states(sp):
    b = sp.shape[0]
    s = jnp.stack([sp[:, :, :HD, :HD], sp[:, :, HD:, HD:]], axis=2)
    return s.reshape(b, B_HEADS, HD, HD)


def _retention(p, row_base, batch, length, dec_f, dec_b, gn_g, s0f, s0b):
    npairs = B_HEADS // 2
    blk0 = row_base // length
    qcol = (A_Q + 2 * A_KV) // LANES

    def col_spec(off):
        return pl.BlockSpec((length, LANES), lambda b, h: (blk0 + b, qcol + off * npairs + h))

    lane_spec = pl.BlockSpec((None, 1, LANES), lambda b, h: (h, 0, 0))
    state_spec = pl.BlockSpec((None, None, LANES, LANES), lambda b, h: (b, h, 0, 0))
    state_shape = jax.ShapeDtypeStruct((batch, npairs, LANES, LANES), F32)
    return pl.pallas_call(
        functools.partial(_ret_kernel, length=length),
        grid=(batch, npairs),
        in_specs=[lane_spec, lane_spec, col_spec(0), col_spec(1), col_spec(2), col_spec(3), lane_spec,
                  state_spec, state_spec],
        out_specs=[pl.BlockSpec((length, LANES), lambda b, h: (b, h)), state_spec, state_spec],
        out_shape=[jax.ShapeDtypeStruct((batch * length, B_W), F32), state_shape, state_shape],
        scratch_shapes=[pltpu.VMEM((length, LANES), F32), pltpu.VMEM((length, LANES), F32)],
        compiler_params=_cparams(2),
        name="retention",
    )(_pair_lanes(dec_f), _pair_lanes(dec_b), p, p, p, p, gn_g.reshape(npairs, 1, LANES), s0f, s0b)


def _ctx_mha_kernel(q_ref, k_ref, v_ref, o_ref):
    lo = _lane_lo()
    for pair in range(C_HEADS // 2):
        cols = slice(pair * LANES, (pair + 1) * LANES)
        q = _stack_heads(q_ref, [2 * pair, 2 * pair + 1], lo, HD ** -0.5)
        s = lax.dot_general(q, k_ref[:, cols].astype(BF16), NT_DIMS, preferred_element_type=F32)
        o = _softmax_av([s], [v_ref[:, cols].astype(BF16)])
        o_ref[:, cols] = jnp.where(lo, o[:SEQ], o[SEQ:])


def _ctx_mha(p):
    return pl.pallas_call(
        _ctx_mha_kernel,
        grid=(BATCH,),
        in_specs=[pl.BlockSpec((SEQ, C_W), lambda b: (b, 0)),
                  pl.BlockSpec((SEQ, C_W), lambda b: (b, 1)),
                  pl.BlockSpec((SEQ, C_W), lambda b: (b, 2))],
        out_specs=pl.BlockSpec((SEQ, C_W), lambda b: (b, 0)),
        out_shape=jax.ShapeDtypeStruct((NP_TOK, C_W), F32),
        compiler_params=_cparams(1),
        name="ctx_mha",
    )(p, p, p)


NA_WIN_ROWS = 2 * NA_ROWS
NA_WIN = NA_WIN_ROWS * GRID_W
NA_QROWS = NA_ROWS * GRID_W
NA_PAD_ROWS = NA_KH // 2
NA_TABLE = 1536


def _na_kernel(q_ref, kp_ref, km_ref, kn_ref, vp_ref, vm_ref, vn_ref, ck_ref, cv_ref, ue_ref, uo_ref, o_ref):
    r0 = pl.program_id(2) * NA_ROWS
    n_rows = DEC_SEQ // GRID_W
    lo = _lane_lo()
    k = jnp.concatenate([kp_ref[...], km_ref[...], kn_ref[...]], axis=0).astype(BF16)
    v = jnp.concatenate([vp_ref[...], vm_ref[...], vn_ref[...]], axis=0).astype(BF16)
    ck = ck_ref[...].astype(BF16)
    cv = cv_ref[...].astype(BF16)
    q = q_ref[...] * HD ** -0.5
    klane = lax.broadcasted_iota(jnp.int32, (1, NA_WIN), 1)
    outs = []
    for half, keep in enumerate((lo, jnp.logical_not(lo))):
        qh = jnp.where(keep, q, 0.0).astype(BF16)
        s = lax.dot_general(qh, k, NT_DIMS, preferred_element_type=F32)
        s_ctx = lax.dot_general(qh, ck, NT_DIMS, preferred_element_type=F32)
        p_loc, p_ctx, den = [], [], []
        for rq in range(NA_ROWS):
            rows = slice(rq * GRID_W, (rq + 1) * GRID_W)
            start = NA_KH - 1 - rq
            if start % 2 == 0:
                u = ue_ref[half, :, start * GRID_W:start * GRID_W + NA_WIN]
            else:
                u = uo_ref[half, :, (start - 1) * GRID_W:(start - 1) * GRID_W + NA_WIN]
            r = r0 + rq
            first = jnp.clip(r - NA_KH // 2, 0, n_rows - NA_KH)
            lane0 = (first - r0 + NA_PAD_ROWS) * GRID_W
            in_rows = (klane >= lane0) & (klane < lane0 + NA_KH * GRID_W)
            sl = jnp.where(in_rows, s[rows] + u, NEG)
            sc = s_ctx[rows]
            mx = jnp.maximum(sl.max(axis=-1, keepdims=True), sc.max(axis=-1, keepdims=True))
            el = jnp.exp(sl - mx)
            ec = jnp.exp(sc - mx)
            den.append(el.sum(axis=-1, keepdims=True) + ec.sum(axis=-1, keepdims=True))
            p_loc.append(el.astype(BF16))
            p_ctx.append(ec.astype(BF16))
        acc = (jnp.dot(jnp.concatenate(p_loc, axis=0), v, preferred_element_type=F32)
               + jnp.dot(jnp.concatenate(p_ctx, axis=0), cv, preferred_element_type=F32))
        outs.append(acc / jnp.concatenate(den, axis=0))
    o_ref[...] = jnp.where(lo, outs[0], outs[1])


def _na_bias_tables(rpb):
    cq = jnp.arange(GRID_W)
    ck = jnp.arange(GRID_W)
    dc = jnp.clip(ck[None] - cq[:, None], -(NA_KW - 1), NA_KW - 1) + NA_KW - 1
    cs = jnp.clip(cq - NA_KW // 2, 0, GRID_W - NA_KW)
    col_ok = (ck[None] >= cs[:, None]) & (ck[None] < cs[:, None] + NA_KW)
    t = rpb.astype(F32)[:, :, dc]
    t = jnp.where(col_ok[None, None], t, NEG).transpose(0, 2, 1, 3)
    n_dr = 2 * NA_KH - 1
    blocks = NA_TABLE // GRID_W
    t = jnp.pad(t, ((0, 0), (0, 0), (NA_PAD_ROWS, blocks - n_dr - NA_PAD_ROWS), (0, 0)), constant_values=NEG)
    ue = t.reshape(C_HEADS, GRID_W, NA_TABLE)
    uo = jnp.concatenate([ue[..., GRID_W:], jnp.full((C_HEADS, GRID_W, GRID_W), NEG, F32)], axis=-1)
    return ue, uo


def _na_attention(p, cache_k, cache_v, rpb):
    npairs = C_HEADS // 2
    nrb = DEC_SEQ // NA_QROWS
    half = NA_QROWS // 2
    qbase = NP_TOK // NA_QROWS
    hbase = NP_TOK // half
    kcol = C_W // LANES
    ue, uo = _na_bias_tables(rpb)

    def main_spec(col0):
        return pl.BlockSpec((NA_QROWS, LANES), lambda b, h, r: (qbase + b * nrb + r, col0 + h))

    def side_spec(col0, off):
        return pl.BlockSpec((half, LANES),
                            lambda b, h, r: (hbase + b * 2 * nrb + jnp.clip(2 * r + off, 0, 2 * nrb - 1), col0 + h))

    ctx_spec = pl.BlockSpec((None, PAST, LANES), lambda b, h, r: (b, 0, h))
    tab_spec = pl.BlockSpec((2, GRID_W, NA_TABLE), lambda b, h, r: (h, 0, 0))
    return pl.pallas_call(
        _na_kernel,
        grid=(DEC_BATCH, npairs, nrb),
        in_specs=[main_spec(0),
                  side_spec(kcol, -1), main_spec(kcol), side_spec(kcol, 2),
                  side_spec(2 * kcol, -1), main_spec(2 * kcol), side_spec(2 * kcol, 2),
                  ctx_spec, ctx_spec, tab_spec, tab_spec],
        out_specs=pl.BlockSpec((NA_QROWS, LANES), lambda b, h, r: (b * nrb + r, h)),
        out_shape=jax.ShapeDtypeStruct((NS_TOK, C_W), F32),
        compiler_params=_cparams(3),
        name="na_attn",
    )(p, p, p, p, p, p, p, cache_k.reshape(DEC_BATCH, PAST, C_W), cache_v.reshape(DEC_BATCH, PAST, C_W), ue, uo)


def _route(biased, scores):
    t = biased.shape[1]
    per_group = N_EXPERTS // N_GROUPS
    i8 = lax.broadcasted_iota(jnp.int32, (per_group, t), 0)
    g_rows = []
    for g in range(N_GROUPS):
        bg = biased[g * per_group:(g + 1) * per_group]
        m1 = bg.max(axis=0, keepdims=True)
        first = jnp.where(bg == m1, i8, per_group).min(axis=0, keepdims=True)
        m2 = jnp.where(i8 == first, -jnp.inf, bg).max(axis=0, keepdims=True)
        g_rows.append(m1 + m2)
    g_top = jnp.concatenate(g_rows, axis=0)
    gi = lax.broadcasted_iota(jnp.int32, g_top.shape, 0)
    g_sel = jnp.zeros(g_top.shape, jnp.int32)
    cur = g_top
    for _ in range(TOPK_GROUPS):
        m = cur.max(axis=0, keepdims=True)
        hit = gi == jnp.where(cur == m, gi, N_GROUPS).min(axis=0, keepdims=True)
        g_sel = jnp.where(hit, 1, g_sel)
        cur = jnp.where(hit, -jnp.inf, cur)
    e_sel = jnp.concatenate([jnp.broadcast_to(g_sel[g:g + 1], (per_group, t)) for g in range(N_GROUPS)], axis=0)
    cur = jnp.where(e_sel > 0, biased, NEG)
    ei = lax.broadcasted_iota(jnp.int32, cur.shape, 0)
    ids, gates, hits = [], [], []
    for _ in range(TOP_K):
        m = cur.max(axis=0, keepdims=True)
        f = jnp.where(cur == m, ei, N_EXPERTS).min(axis=0, keepdims=True)
        hit = ei == f
        ids.append(f)
        hits.append(hit)
        gates.append(jnp.where(hit, scores, 0.0).sum(axis=0, keepdims=True))
        cur = jnp.where(hit, -jnp.inf, cur)
    gate = jnp.concatenate(gates, axis=0)
    gate = gate / gate.sum(axis=0, keepdims=True) * ROUTED_SCALE
    return jnp.concatenate(ids, axis=0), gate, hits


def _pack_bf16_pairs(h):
    bits = lax.bitcast_convert_type(h.astype(BF16).astype(F32), jnp.uint32)
    return bits[:, :D // 2] | (bits[:, D // 2:] >> 16)


def _unpack_bf16_pairs(xp):
    hi = lax.bitcast_convert_type(xp & jnp.uint32(0xFFFF0000), F32).astype(BF16)
    lo = lax.bitcast_convert_type(xp << 16, F32).astype(BF16)
    return hi, lo


def _dot_halves(hi, lo, w_ref):
    return (jnp.dot(hi, w_ref[:D // 2, :], preferred_element_type=F32)
            + jnp.dot(lo, w_ref[D // 2:, :], preferred_element_type=F32))


def _outproj_kernel(*refs, n_parts):
    xp_ref, xs_ref = refs[:2]
    part_refs = refs[2:2 + 3 * n_parts]
    gate_ref, shift_ref, scale_ref, g2_ref, rw_ref, rb_ref = refs[2 + 3 * n_parts:8 + 3 * n_parts]
    xo_ref, h_ref, dest_ref, wgt_ref, plan_ref, cnt_ref = refs[8 + 3 * n_parts:]
    step = pl.program_id(0)

    @pl.when(step == 0)
    def _():
        cnt_ref[...] = jnp.zeros_like(cnt_ref)
        plan_ref[...] = jnp.zeros_like(plan_ref)

    y = None
    for t in range(n_parts):
        ap_ref, as_ref, w_ref = part_refs[3 * t:3 * t + 3]
        d = jnp.dot(_pick_rows(ap_ref, as_ref).astype(BF16), w_ref[...], preferred_element_type=F32)
        y = d if y is None else y + d
    x = _pick_rows(xp_ref, xs_ref) + gate_ref[...] * y
    xo_ref[...] = x
    h = _rms(x, g2_ref[...]) * (1.0 + scale_ref[...]) + shift_ref[...]
    h_ref[...] = _pack_bf16_pairs(h)
    h_hi = h.astype(BF16)
    h_lo = (h - h_hi.astype(F32)).astype(BF16)
    rw = rw_ref[...]
    rw_hi = rw.astype(BF16)
    rw_lo = (rw - rw_hi.astype(F32)).astype(BF16)
    logits = (lax.dot_general(rw_hi, h_hi, NT_DIMS, preferred_element_type=F32)
              + lax.dot_general(rw_hi, h_lo, NT_DIMS, preferred_element_type=F32)
              + lax.dot_general(rw_lo, h_hi, NT_DIMS, preferred_element_type=F32))
    scores = jax.nn.sigmoid(logits)
    _, gate, hits = _route(scores + rb_ref[...], scores)
    wgt_ref[...] = gate
    chosen = hits[0]
    for hit in hits[1:]:
        chosen = chosen | hit
    m = jnp.where(chosen, 1.0, 0.0)
    before = (lax.broadcasted_iota(jnp.int32, (TM, TM), 0) < lax.broadcasted_iota(jnp.int32, (TM, TM), 1))
    prefix = jnp.dot(m.astype(BF16), jnp.where(before, 1.0, 0.0).astype(BF16), preferred_element_type=F32)
    e_base = (lax.broadcasted_iota(jnp.int32, (N_EXPERTS, 1), 0) * N_TOK).astype(F32)
    row_all = prefix + (cnt_ref[...] + e_base)
    dest_ref[...] = jnp.concatenate(
        [jnp.where(hit, row_all, 0.0).sum(axis=0, keepdims=True) for hit in hits], axis=0).astype(jnp.int32)
    cnt_ref[...] += m.sum(axis=1, keepdims=True)

    @pl.when(step == pl.num_programs(0) - 1)
    def _():
        _block_plan(cnt_ref[...], plan_ref)


def _block_plan(counts, plan_ref):
    cap_blocks = N_TOK // MOE_BLOCK
    nblk = ((counts.astype(jnp.int32) + (MOE_BLOCK - 1)) // MOE_BLOCK).astype(F32)
    lower = (lax.broadcasted_iota(jnp.int32, (N_EXPERTS, N_EXPERTS), 0)
             >= lax.broadcasted_iota(jnp.int32, (N_EXPERTS, N_EXPERTS), 1))
    cum = jnp.dot(jnp.where(lower, 1.0, 0.0).astype(BF16), jnp.broadcast_to(nblk, (N_EXPERTS, LANES)).astype(BF16),
                  preferred_element_type=F32)[:, :1]
    n_used = cum[N_EXPERTS - 1:, :]
    slot = jnp.minimum(lax.broadcasted_iota(jnp.int32, (1, PLAN_LANES), 1).astype(F32), n_used - 1.0)
    done = cum <= slot
    expert = jnp.where(done, 1.0, 0.0).sum(axis=0, keepdims=True)
    blocks_before = jnp.where(done, nblk, 0.0).sum(axis=0, keepdims=True)
    plan_ref[0:1, :] = (expert * cap_blocks + (slot - blocks_before)).astype(jnp.int32)
    plan_ref[1:2, :] = expert.astype(jnp.int32)
    plan_ref[2:3, :] = jnp.broadcast_to(n_used, (1, PLAN_LANES)).astype(jnp.int32)
    run_end = jnp.where(cum > slot, cum, jnp.inf).min(axis=0, keepdims=True)
    plan_ref[3:4, :] = jnp.minimum(jnp.where(cum <= run_end, 1.0, 0.0).sum(axis=0, keepdims=True),
                                   N_EXPERTS - 1.0).astype(jnp.int32)
    plan_ref[4:5, :] = jnp.where(run_end < n_used, 1, 0).astype(jnp.int32)


def _outproj(x, parts, mod, g2, router_w, router_b):
    in_specs = _pair_specs(D)
    args = [x[0], x[1]]
    for ap, a_s, w in parts:
        width = ap.shape[1]
        in_specs += _pair_specs(width) + [pl.BlockSpec((width, D), lambda i: (0, 0))]
        args += [ap, a_s, w]
    in_specs += [_mod_spec(2), _mod_spec(3), _mod_spec(4),
                 pl.BlockSpec((1, D), lambda i: (0, 0)),
                 pl.BlockSpec((N_EXPERTS, D), lambda i: (0, 0)),
                 pl.BlockSpec((N_EXPERTS, 1), lambda i: (0, 0))]
    args += [mod, mod, mod, g2.reshape(1, D), router_w.T, router_b.reshape(N_EXPERTS, 1)]
    return pl.pallas_call(
        functools.partial(_outproj_kernel, n_parts=len(parts)),
        grid=(N_TOK // TM,),
        in_specs=in_specs,
        out_specs=[pl.BlockSpec((TM, D), lambda i: (i, 0)),
                   pl.BlockSpec((TM, D // 2), lambda i: (i, 0)),
                   pl.BlockSpec((TOP_K, TM), lambda i: (0, i)),
                   pl.BlockSpec((TOP_K, TM), lambda i: (0, i)),
                   pl.BlockSpec((8, PLAN_LANES), lambda i: (0, 0))],
        out_shape=[jax.ShapeDtypeStruct((N_TOK, D), F32),
                   jax.ShapeDtypeStruct((N_TOK, D // 2), jnp.uint32),
                   jax.ShapeDtypeStruct((TOP_K, N_TOK), jnp.int32),
                   jax.ShapeDtypeStruct((TOP_K, N_TOK), F32),
                   jax.ShapeDtypeStruct((8, PLAN_LANES), jnp.int32)],
        scratch_shapes=[pltpu.VMEM((N_EXPERTS, 1), F32)],
        compiler_params=_cparams(1),
        name="outproj_router",
    )(*args)


def _experts_kernel(br_ref, be_ref, nu_ref, ne_ref, hn_ref, x_hbm, w1_hbm, w3_hbm, w2_hbm, o_ref,
                    w1b, w3b, w2b, w1f, w3f, w2f, xbuf, xsem, wsem, *, layer):
    i = pl.program_id(0)
    n_steps = pl.num_programs(0)
    e = be_ref[i]
    prev = be_ref[jnp.maximum(i - 1, 0)]

    def x_copy(step):
        slot = step % EXPERT_X_SLOTS
        rows = pl.ds(pl.multiple_of(br_ref[step] * MOE_BLOCK, MOE_BLOCK), MOE_BLOCK)
        return pltpu.make_async_copy(x_hbm.at[rows], xbuf.at[slot], xsem.at[slot])

    @pl.when(i == 0)
    def _():
        x_copy(0).start()
        x_copy(1).start()

    @pl.when(i + 2 < n_steps)
    def _():
        x_copy(i + 2).start()

    x_copy(i).wait()

    def w_copies(expert):
        return [pltpu.make_async_copy(src.at[layer, expert], dst, wsem.at[k])
                for k, (src, dst) in enumerate(((w1_hbm, w1f), (w3_hbm, w3f), (w2_hbm, w2f)))]

    @pl.when(i == 0)
    def _():
        for c in w_copies(e):
            c.start()

    @pl.when((i == 0) | (e != prev))
    def _():
        for c in w_copies(e):
            c.wait()
        w1b[...] = w1f[...].astype(BF16)
        w3b[...] = w3f[...].astype(BF16)
        w2b[...] = w2f[...].astype(BF16)

        @pl.when(hn_ref[i] > 0)
        def _():
            for c in w_copies(ne_ref[i]):
                c.start()

    @pl.when(i < nu_ref[0])
    def _():
        hi, lo = _unpack_bf16_pairs(xbuf[i % EXPERT_X_SLOTS])
        a = _dot_halves(hi, lo, w1b)
        b = _dot_halves(hi, lo, w3b)
        h = (_silu(a) * b).astype(BF16)
        o_ref[...] = _pack_bf16_pairs(jnp.dot(h, w2b[...], preferred_element_type=F32))


def _experts(plan, x_rows, w1, w3, w2, layer):
    any_spec = pl.BlockSpec(memory_space=pl.ANY)
    return pl.pallas_call(
        functools.partial(_experts_kernel, layer=layer),
        grid_spec=pltpu.PrefetchScalarGridSpec(
            num_scalar_prefetch=5,
            grid=(N_MOE_BLOCKS,),
            in_specs=[any_spec, any_spec, any_spec, any_spec],
            out_specs=pl.BlockSpec((MOE_BLOCK, D // 2), lambda i, br, be, nu, ne, hn: (br[i], 0)),
            scratch_shapes=[pltpu.VMEM((D, FF), BF16), pltpu.VMEM((D, FF), BF16), pltpu.VMEM((FF, D), BF16),
                            pltpu.VMEM((D, FF), F32), pltpu.VMEM((D, FF), F32), pltpu.VMEM((FF, D), F32),
                            pltpu.VMEM((EXPERT_X_SLOTS, MOE_BLOCK, D // 2), jnp.uint32),
                            pltpu.SemaphoreType.DMA((EXPERT_X_SLOTS,)),
                            pltpu.SemaphoreType.DMA((3,))]),
        out_shape=jax.ShapeDtypeStruct(x_rows.shape, jnp.uint32),
        compiler_params=_cparams(1),
        name="experts",
    )(plan[0], plan[1], plan[2, :1], plan[3], plan[4], x_rows, w1, w3, w2)


SC_CORES = 2
SC_SUBCORES = 16
SC_WORKERS = SC_CORES * SC_SUBCORES
SC_CHUNK_BYTES = 64 * 1024
SC_SLOTS = 4


def _sc_scatter(rows, dest, n_out):
    n_rows, width = rows.shape
    picks = dest.shape[0]
    chunk = SC_CHUNK_BYTES // (4 * width)
    per_worker = n_rows // SC_WORKERS
    n_chunks = per_worker // chunk
    assert per_worker * SC_WORKERS == n_rows and n_chunks * chunk == per_worker and n_chunks % 2 == 0
    mesh = plsc.VectorSubcoreMesh(core_axis_name="c", subcore_axis_name="s")

    @functools.partial(
        pl.kernel, mesh=mesh,
        out_type=jax.ShapeDtypeStruct((n_out, width), rows.dtype),
        scratch_types=[pltpu.VMEM((picks, n_chunks, chunk), jnp.int32),
                       pltpu.VMEM((2, chunk, width), rows.dtype),
                       pltpu.SemaphoreType.DMA((2,)),
                       pltpu.SemaphoreType.DMA((2,))])
    def scatter(r_hbm, d_hbm, o_hbm, idx_v, rows_v, lsem, ssem):
        worker = lax.axis_index("s") * SC_CORES + lax.axis_index("c")
        base = worker * per_worker
        for k in range(picks):
            pltpu.sync_copy(d_hbm.at[k, worker], idx_v.at[k])

        def load_copy(c, b):
            src = pl.ds(pl.multiple_of(base + c * chunk, chunk), chunk)
            return pltpu.make_async_copy(r_hbm.at[src], rows_v.at[b], lsem.at[b])

        def store_copy(c, b, k):
            return pltpu.make_async_copy(rows_v.at[b], o_hbm.at[idx_v.at[k, c]], ssem.at[b])

        load_copy(0, 0).start()

        @pl.loop(0, n_chunks, step=2)
        def _(c0):
            for b in range(2):
                c = c0 + b
                load_copy(c, b).wait()
                for k in range(picks):
                    store_copy(c, b, k).start()

                @pl.when(c > 0)
                def _():
                    for k in range(picks):
                        store_copy(c - 1, 1 - b, k).wait()

                @pl.when(c + 1 < n_chunks)
                def _():
                    load_copy(c + 1, 1 - b).start()

        for k in range(picks):
            store_copy(n_chunks - 1, 1, k).wait()

    return scatter(rows, dest.reshape(picks, SC_WORKERS, n_chunks, chunk))


def _sc_gather(table, idx):
    n_idx = idx.shape[0]
    width = table.shape[1]
    chunk = SC_CHUNK_BYTES // (4 * width)
    per_worker = n_idx // SC_WORKERS
    n_chunks = per_worker // chunk
    ahead = SC_SLOTS - 1
    assert per_worker * SC_WORKERS == n_idx and n_chunks * chunk == per_worker and n_chunks % SC_SLOTS == 0
    mesh = plsc.VectorSubcoreMesh(core_axis_name="c", subcore_axis_name="s")

    @functools.partial(
        pl.kernel, mesh=mesh,
        out_type=jax.ShapeDtypeStruct((n_idx, width), table.dtype),
        scratch_types=[pltpu.VMEM((per_worker,), jnp.int32),
                       pltpu.VMEM((SC_SLOTS, chunk, width), table.dtype),
                       pltpu.SemaphoreType.DMA((SC_SLOTS,)),
                       pltpu.SemaphoreType.DMA((SC_SLOTS,))])
    def gather(t_hbm, i_hbm, o_hbm, idx_v, rows_v, gsem, wsem):
        worker = lax.axis_index("s") * SC_CORES + lax.axis_index("c")
        base = worker * per_worker
        pltpu.sync_copy(i_hbm.at[pl.ds(pl.multiple_of(base, chunk), per_worker)], idx_v)

        def gather_copy(c, b):
            ids = idx_v.at[pl.ds(pl.multiple_of(c * chunk, chunk), chunk)]
            return pltpu.make_async_copy(t_hbm.at[ids], rows_v.at[b], gsem.at[b])

        def write_copy(c, b):
            rows = pl.ds(pl.multiple_of(base + c * chunk, chunk), chunk)
            return pltpu.make_async_copy(rows_v.at[b], o_hbm.at[rows], wsem.at[b])

        for c in range(ahead):
            gather_copy(c, c).start()

        @pl.loop(0, n_chunks, step=SC_SLOTS)
        def _(c0):
            for b in range(SC_SLOTS):
                c = c0 + b
                refill = (b + ahead) % SC_SLOTS
                gather_copy(c, b).wait()
                write_copy(c, b).start()

                @pl.when(c > 0)
                def _():
                    write_copy(c - 1, refill).wait()

                @pl.when(c + ahead < n_chunks)
                def _():
                    gather_copy(c + ahead, refill).start()

        write_copy(n_chunks - 1, (n_chunks - 1) % SC_SLOTS).wait()

    return gather(table, idx)


TC = 512


def _combine_kernel(x_ref, h_ref, y_ref, wgt_ref, gate_ref, w1_ref, w3_ref, w2_ref, fg_ref, o_ref, *, final):
    hi, lo = _unpack_bf16_pairs(h_ref[...])
    a = _dot_halves(hi, lo, w1_ref)
    b = _dot_halves(hi, lo, w3_ref)
    ffn = jnp.dot((_silu(a) * b).astype(BF16), w2_ref[...], preferred_element_type=F32)
    wgt = wgt_ref[...]
    r_hi = None
    r_lo = None
    for k in range(TOP_K):
        yk = y_ref[k]
        w = wgt[:, k:k + 1]
        t_hi = lax.bitcast_convert_type(yk & jnp.uint32(0xFFFF0000), F32) * w
        t_lo = lax.bitcast_convert_type(yk << 16, F32) * w
        r_hi = t_hi if r_hi is None else r_hi + t_hi
        r_lo = t_lo if r_lo is None else r_lo + t_lo
    x = x_ref[...] + gate_ref[...] * (ffn + jnp.concatenate([r_hi, r_lo], axis=1))
    o_ref[...] = _rms(x, fg_ref[...]) if final else x


def _combine(x, h, y_rows, wgt, mod, sw1, sw3, sw2, final_g, final):
    weights = (sw1.astype(BF16), sw3.astype(BF16), sw2.astype(BF16), final_g.reshape(1, D))

    def rows_from(first_row, n_rows):
        b0 = first_row // TC
        return pl.pallas_call(
            functools.partial(_combine_kernel, final=final),
            grid=(n_rows // TC,),
            in_specs=[pl.BlockSpec((TC, D), lambda i: (i + b0, 0)),
                      pl.BlockSpec((TC, D // 2), lambda i: (i + b0, 0)),
                      pl.BlockSpec((TOP_K, TC, D // 2), lambda i: (0, i + b0, 0)),
                      pl.BlockSpec((TC, TOP_K), lambda i: (i + b0, 0)),
                      _mod_spec(5, TC, b0),
                      pl.BlockSpec((D, FF), lambda i: (0, 0)),
                      pl.BlockSpec((D, FF), lambda i: (0, 0)),
                      pl.BlockSpec((FF, D), lambda i: (0, 0)),
                      pl.BlockSpec((1, D), lambda i: (0, 0))],
            out_specs=pl.BlockSpec((TC, D), lambda i: (i, 0)),
            out_shape=jax.ShapeDtypeStruct((n_rows, D), F32),
            compiler_params=_cparams(1),
            name="combine",
        )(x, h, y_rows, wgt, mod, *weights)

    return rows_from(0, NP_TOK), rows_from(NP_TOK, NS_TOK)


def kernel(x_prompt, x_sample, cache_a_k, cache_a_v, state_ret_fwd, state_ret_bwd, cache_c_k, cache_c_v,
           c, c_ctx, norm1_g, norm2_g, ada_w, ada_b, even_w_in, even_w_out, sink_a, ret_decay_fwd,
           ret_decay_bwd, ret_gn_g, odd_w_in, odd_w_out, na_rpb, router_w, router_b, exp_w1, exp_w3,
           exp_w2, sh_w1, sh_w3, sh_w2, final_g):
    x = (x_prompt.reshape(NP_TOK, D), x_sample.reshape(NS_TOK, D))
    cc = jnp.concatenate([c_ctx[None], c, jnp.zeros((8 - 1 - DEC_BATCH, D), F32)], axis=0)
    rope = _rope_tables()
    outs = {}
    for l in range(2):
        mod = _ada(cc, ada_w, ada_b, l)
        if l == 0:
            p = _inproj(x, norm1_g[l], mod, even_w_in[0].astype(BF16), rope, A_Q + A_KV)
            oa_p = _ctx_gqa(p, sink_a[0])
            oa_s = _win_attention(p, cache_a_k[:, 0], cache_a_v[:, 0], sink_a[0])
            zero = jnp.zeros((BATCH, B_HEADS // 2, LANES, LANES), F32)
            ob_p, sf, sb = _retention(p, 0, BATCH, SEQ, ret_decay_fwd[0], ret_decay_bwd[0], ret_gn_g[0], zero, zero)
            ob_s, _, _ = _retention(p, NP_TOK, DEC_BATCH, DEC_SEQ, ret_decay_fwd[0], ret_decay_bwd[0], ret_gn_g[0],
                                    _blockdiag_states(state_ret_fwd[:, 0]), _blockdiag_states(state_ret_bwd[:, 0]))
            w_out = even_w_out[0].astype(BF16)
            parts = [(oa_p, oa_s, w_out[:A_Q]), (ob_p, ob_s, w_out[A_Q:])]
            outs["a_k"] = p[:NP_TOK, A_Q:A_Q + A_KV].reshape(BATCH, 1, SEQ, A_KV_HEADS, HD)
            outs["a_v"] = p[:NP_TOK, A_Q + A_KV:A_Q + 2 * A_KV].reshape(BATCH, 1, SEQ, A_KV_HEADS, HD)
            outs["r_f"] = _diag_states(sf).reshape(BATCH, 1, B_HEADS, HD, HD)
            outs["r_b"] = _diag_states(sb).reshape(BATCH, 1, B_HEADS, HD, HD)
        else:
            p = _inproj(x, norm1_g[l], mod, odd_w_in[0].astype(BF16), rope, 0)
            o_p = _ctx_mha(p)
            o_s = _na_attention(p, cache_c_k[:, 0], cache_c_v[:, 0], na_rpb[0])
            parts = [(o_p, o_s, odd_w_out[0].astype(BF16))]
            outs["c_k"] = p[:NP_TOK, C_W:2 * C_W].reshape(BATCH, 1, SEQ, C_HEADS, HD)
            outs["c_v"] = p[:NP_TOK, 2 * C_W:3 * C_W].reshape(BATCH, 1, SEQ, C_HEADS, HD)
        x_mid, h, dest, gate_t, plan = _outproj(x, parts, mod, norm2_g[l], router_w[l], router_b[l])
        y = _experts(plan, _sc_scatter(h, dest, N_EXPERTS * N_TOK), exp_w1, exp_w3, exp_w2, l)
        y_rows = _sc_gather(y, dest.reshape(N_ASSIGN)).reshape(TOP_K, N_TOK, D // 2)
        x = _combine(x_mid, h, y_rows, gate_t.T, mod, sh_w1[l], sh_w3[l], sh_w2[l], final_g, final=(l == 1))
    y_prompt = x[0].reshape(BATCH, SEQ, D)
    y_sample = x[1].reshape(DEC_BATCH, DEC_SEQ, D)
    return (y_prompt, y_sample, outs["a_k"], outs["a_v"], outs["r_f"], outs["r_b"], outs["c_k"], outs["c_v"])
```

```python
import functools

import jax
import jax.numpy as jnp
from jax import lax
from jax.experimental import pallas as pl
from jax.experimental.pallas import tpu as pltpu
from jax.experimental.pallas import tpu_sc as plsc

F32 = jnp.float32
BF16 = jnp.bfloat16
HIGHEST = lax.Precision.HIGHEST

D = 1024
BATCH = 32
SEQ = 256
DEC_BATCH = 4
DEC_SEQ = 4096
PAST = 256
GRID_W = 64
HD = 64
EPS = 1e-6
NEG = -1e30
ROPE_BASE = 10000.0
A_HEADS = 8
A_KV_HEADS = 2
A_Q = A_HEADS * HD
A_KV = A_KV_HEADS * HD
B_HEADS = 8
B_W = B_HEADS * HD
EVEN_IN = A_Q + 2 * A_KV + 4 * B_W
C_HEADS = 16
C_W = C_HEADS * HD
NA_KH = 8
NA_KW = 16
N_EXPERTS = 64
TOP_K = 8
N_GROUPS = 8
TOPK_GROUPS = 4
FF = 256
ROUTED_SCALE = 2.5
MOE_BLOCK = 1024
EXPERT_X_SLOTS = 3
RET_CHUNK = 256
RET_UNROLL = 4
RET_NORM_ROWS = 1024
A_WINDOW = 128

NP_TOK = BATCH * SEQ
NS_TOK = DEC_BATCH * DEC_SEQ
N_TOK = NP_TOK + NS_TOK
N_ASSIGN = N_TOK * TOP_K
N_MOE_BLOCKS = (N_ASSIGN + N_EXPERTS * (MOE_BLOCK - 1) + MOE_BLOCK - 1) // MOE_BLOCK
PLAN_LANES = 512
assert N_TOK % MOE_BLOCK == 0 and N_MOE_BLOCKS <= PLAN_LANES

LANES = 128
TM = 512
TM_OUT = 1024
NA_ROWS = 8
V7X_VMEM_LIMIT = 56 * 1024 * 1024

NT_DIMS = (((1,), (1,)), ((), ()))


def _cparams(n_axes, vmem=V7X_VMEM_LIMIT):
    return pltpu.CompilerParams(dimension_semantics=("arbitrary",) * n_axes, vmem_limit_bytes=vmem)


def _seg_of_block(i, rows):
    row0 = i * rows
    return jnp.where(row0 < NP_TOK, 0, 1 + (row0 - NP_TOK) // DEC_SEQ)


def _mod_spec(chunk, rows=TM, first_block=0):
    return pl.BlockSpec((None, 1, D), lambda i: (_seg_of_block(i + first_block, rows), 0, chunk))


def _pair_specs(width, rows=TM):
    npb = NP_TOK // rows
    nsb = NS_TOK // rows
    return [pl.BlockSpec((rows, width), lambda i: (jnp.minimum(i, npb - 1), 0)),
            pl.BlockSpec((rows, width), lambda i: (jnp.clip(i - npb, 0, nsb - 1), 0))]


def _pick_rows(p_ref, s_ref, rows=TM):
    return jnp.where(pl.program_id(0) < NP_TOK // rows, p_ref[...], s_ref[...])


def _silu(x):
    return x * jax.nn.sigmoid(x)


def _rms(x, g):
    return x * lax.rsqrt(jnp.mean(x * x, axis=-1, keepdims=True) + EPS) * g


def _lane_lo():
    return lax.broadcasted_iota(jnp.int32, (1, LANES), 1) < HD


def _ada_kernel(c_ref, w_ref, b_ref, o_ref):
    a = _silu(c_ref[...])
    o_ref[...] = jnp.dot(a, w_ref[...], preferred_element_type=F32, precision=HIGHEST) + b_ref[...]


def _ada(cc, w, b, layer):
    tn = 1536
    out = pl.pallas_call(
        _ada_kernel,
        grid=(6 * D // tn,),
        in_specs=[pl.BlockSpec((8, D), lambda j: (0, 0)),
                  pl.BlockSpec((None, D, tn), lambda j: (layer, 0, j)),
                  pl.BlockSpec((None, 1, tn), lambda j: (layer, 0, j))],
        out_specs=pl.BlockSpec((8, tn), lambda j: (0, j)),
        out_shape=jax.ShapeDtypeStruct((8, 6 * D), F32),
        compiler_params=_cparams(1),
        name="ada",
    )(cc, w, b.reshape(b.shape[0], 1, 6 * D))
    return out.reshape(8, 1, 6 * D)


def _inproj_kernel(xp_ref, xs_ref, g_ref, shift_ref, scale_ref, w_ref, cos_ref, sin_ref, o_ref, *, rope_cols):
    h = _rms(_pick_rows(xp_ref, xs_ref), g_ref[...]) * (1.0 + scale_ref[...]) + shift_ref[...]
    o = jnp.dot(h.astype(BF16), w_ref[...], preferred_element_type=F32)
    if rope_cols:
        cos = cos_ref[...]
        sin = sin_ref[...]
        lane = lax.broadcasted_iota(jnp.int32, (1, LANES), 1)
        first = (lane % 32) < 16
        for c in range(rope_cols // LANES):
            oc = o[:, c * LANES:(c + 1) * LANES]
            partner = jnp.where(first, pltpu.roll(oc, LANES - 16, 1), pltpu.roll(oc, 16, 1))
            o_ref[:, c * LANES:(c + 1) * LANES] = oc * cos + partner * sin
        o_ref[:, rope_cols:] = o[:, rope_cols:]
    else:
        o_ref[...] = o


def _rope_tables():
    half = HD // 2
    inv = ROPE_BASE ** (-jnp.arange(0, half, 2, dtype=F32) / half)
    t = jnp.arange(DEC_SEQ)
    ang_r = (t // GRID_W).astype(F32)[:, None] * inv[None]
    ang_c = (t % GRID_W).astype(F32)[:, None] * inv[None]

    def head(fn_r, fn_c, sign):
        return jnp.concatenate([sign[0] * fn_r, sign[1] * fn_r, sign[0] * fn_c, sign[1] * fn_c], axis=-1)

    cos = head(jnp.cos(ang_r), jnp.cos(ang_c), (1.0, 1.0))
    sin = head(jnp.sin(ang_r), jnp.sin(ang_c), (-1.0, 1.0))
    cos = jnp.concatenate([jnp.ones((TM, HD), F32), cos], axis=0)
    sin = jnp.concatenate([jnp.zeros((TM, HD), F32), sin], axis=0)
    return jnp.tile(cos, (1, 2)), jnp.tile(sin, (1, 2))


def _inproj(x, g, mod, w_bf16, rope, rope_cols):
    n_out = w_bf16.shape[1]
    npb = NP_TOK // TM
    spb = DEC_SEQ // TM

    def rope_map(i):
        return (jnp.where(i < npb, 0, 1 + (i - npb) % spb), 0)

    return pl.pallas_call(
        functools.partial(_inproj_kernel, rope_cols=rope_cols),
        grid=(N_TOK // TM,),
        in_specs=_pair_specs(D) + [
                  pl.BlockSpec((1, D), lambda i: (0, 0)),
                  _mod_spec(0), _mod_spec(1),
                  pl.BlockSpec((D, n_out), lambda i: (0, 0)),
                  pl.BlockSpec((TM, LANES), rope_map),
                  pl.BlockSpec((TM, LANES), rope_map)],
        out_specs=pl.BlockSpec((TM, n_out), lambda i: (i, 0)),
        out_shape=jax.ShapeDtypeStruct((N_TOK, n_out), F32),
        compiler_params=_cparams(1),
        name="inproj",
    )(x[0], x[1], g.reshape(1, D), mod, mod, w_bf16, rope[0], rope[1])


def _softmax_av(s_list, v_list, sink=None):
    mx = s_list[0].max(axis=-1, keepdims=True)
    for s in s_list[1:]:
        mx = jnp.maximum(mx, s.max(axis=-1, keepdims=True))
    if sink is not None:
        mx = jnp.maximum(mx, sink)
    den = jnp.exp(sink - mx) if sink is not None else 0.0
    acc = None
    for s, v in zip(s_list, v_list):
        p = jnp.exp(s - mx)
        den = den + p.sum(axis=-1, keepdims=True)
        pv = jnp.dot(p.astype(BF16), v, preferred_element_type=F32)
        acc = pv if acc is None else acc + pv
    return acc / den


def _dup_half(x, j, lo):
    xr = pltpu.roll(x, HD, 1)
    return jnp.where(lo, x, xr) if j == 0 else jnp.where(lo, xr, x)


def _stack_heads(q_ref, heads, lo, scale):
    parts = []
    for h in heads:
        qp = q_ref[:, (h // 2) * LANES:(h // 2 + 1) * LANES]
        keep = lo if h % 2 == 0 else jnp.logical_not(lo)
        parts.append(jnp.where(keep, qp, 0.0) * scale)
    return jnp.concatenate(parts, axis=0).astype(BF16)


def _sink_column(sink_ref, heads, rows):
    return jnp.concatenate([jnp.full((rows, 1), sink_ref[h], F32) for h in heads], axis=0)


def _ctx_gqa_kernel(sink_ref, q_ref, k_ref, v_ref, o_ref):
    lo = _lane_lo()
    k = k_ref[...]
    v = v_ref[...]
    group = A_HEADS // A_KV_HEADS
    scores = []
    for j in range(A_KV_HEADS):
        q = _stack_heads(q_ref, list(range(group * j, group * (j + 1))), lo, HD ** -0.5)
        scores.append(lax.dot_general(q, _dup_half(k, j, lo).astype(BF16), NT_DIMS, preferred_element_type=F32))
    s = jnp.concatenate(scores, axis=0)
    sink = _sink_column(sink_ref, list(range(A_HEADS)), SEQ)
    mx = jnp.maximum(s.max(axis=-1, keepdims=True), sink)
    e = jnp.exp(s - mx)
    den = jnp.exp(sink - mx) + e.sum(axis=-1, keepdims=True)
    e = e.astype(BF16)
    rows_per_group = group * SEQ
    for j in range(A_KV_HEADS):
        rows = slice(j * rows_per_group, (j + 1) * rows_per_group)
        o = jnp.dot(e[rows], _dup_half(v, j, lo).astype(BF16), preferred_element_type=F32) / den[rows]
        for t in range(group // 2):
            pair = (group * j) // 2 + t
            o_ref[:, pair * LANES:(pair + 1) * LANES] = jnp.where(
                lo, o[(2 * t) * SEQ:(2 * t + 1) * SEQ], o[(2 * t + 1) * SEQ:(2 * t + 2) * SEQ])


def _ctx_gqa(p, sink):
    return pl.pallas_call(
        _ctx_gqa_kernel,
        grid_spec=pltpu.PrefetchScalarGridSpec(
            num_scalar_prefetch=1,
            grid=(BATCH,),
            in_specs=[pl.BlockSpec((SEQ, A_Q), lambda b, s: (b, 0)),
                      pl.BlockSpec((SEQ, A_KV), lambda b, s: (b, A_Q // A_KV)),
                      pl.BlockSpec((SEQ, A_KV), lambda b, s: (b, A_Q // A_KV + 1))],
            out_specs=pl.BlockSpec((SEQ, A_Q), lambda b, s: (b, 0))),
        out_shape=jax.ShapeDtypeStruct((NP_TOK, A_Q), F32),
        compiler_params=_cparams(1),
        name="ctx_gqa",
    )(sink, p, p, p)


def _win_kernel(sink_ref, q_ref, kp_ref, kc_ref, kn_ref, vp_ref, vc_ref, vn_ref, ck_ref, cv_ref, o_ref):
    i = pl.program_id(1)
    lo = _lane_lo()
    k = jnp.concatenate([kp_ref[...], kc_ref[...], kn_ref[...]], axis=0)
    v = jnp.concatenate([vp_ref[...], vc_ref[...], vn_ref[...]], axis=0)
    ck = ck_ref[...]
    cv = cv_ref[...]
    group = A_HEADS // A_KV_HEADS
    n_keys = WIN_Q + 2 * A_WINDOW
    qpos = i * WIN_Q + lax.broadcasted_iota(jnp.int32, (WIN_Q, n_keys), 0)
    kpos = i * WIN_Q - A_WINDOW + lax.broadcasted_iota(jnp.int32, (WIN_Q, n_keys), 1)
    valid = (jnp.abs(kpos - qpos) <= A_WINDOW) & (kpos >= 0) & (kpos < DEC_SEQ)
    valid = jnp.concatenate([valid] * group, axis=0)
    s_loc, s_ctx, values = [], [], []
    for j in range(A_KV_HEADS):
        heads = list(range(group * j, group * (j + 1)))
        kd = _dup_half(k, j, lo).astype(BF16)
        ckd = _dup_half(ck, j, lo).astype(BF16)
        values.append((_dup_half(v, j, lo).astype(BF16), _dup_half(cv, j, lo).astype(BF16)))
        q = _stack_heads(q_ref, heads, lo, HD ** -0.5)
        s_loc.append(jnp.where(valid, lax.dot_general(q, kd, NT_DIMS, preferred_element_type=F32), NEG))
        s_ctx.append(lax.dot_general(q, ckd, NT_DIMS, preferred_element_type=F32))
    s_loc = jnp.concatenate(s_loc, axis=0)
    s_ctx = jnp.concatenate(s_ctx, axis=0)
    sink = _sink_column(sink_ref, list(range(A_HEADS)), WIN_Q)
    mx = jnp.maximum(jnp.maximum(s_loc.max(axis=-1, keepdims=True), s_ctx.max(axis=-1, keepdims=True)), sink)
    p_loc = jnp.exp(s_loc - mx)
    p_ctx = jnp.exp(s_ctx - mx)
    den = p_loc.sum(axis=-1, keepdims=True) + p_ctx.sum(axis=-1, keepdims=True) + jnp.exp(sink - mx)
    p_loc = p_loc.astype(BF16)
    p_ctx = p_ctx.astype(BF16)
    rows_per_group = group * WIN_Q
    for j, (vd, cvd) in enumerate(values):
        rows = slice(j * rows_per_group, (j + 1) * rows_per_group)
        o = (jnp.dot(p_loc[rows], vd, preferred_element_type=F32)
             + jnp.dot(p_ctx[rows], cvd, preferred_element_type=F32)) / den[rows]
        for t in range(group // 2):
            pair = (group * j) // 2 + t
            o_ref[:, pair * LANES:(pair + 1) * LANES] = jnp.where(
                lo, o[(2 * t) * WIN_Q:(2 * t + 1) * WIN_Q], o[(2 * t + 1) * WIN_Q:(2 * t + 2) * WIN_Q])


WIN_Q = 256


def _win_attention(p, cache_k, cache_v, sink):
    nblk = DEC_SEQ // WIN_Q
    side = WIN_Q // A_WINDOW
    nside = DEC_SEQ // A_WINDOW
    base = NP_TOK // WIN_Q
    side_base = NP_TOK // A_WINDOW
    kcol = A_Q // A_KV

    def main_spec(col):
        return pl.BlockSpec((WIN_Q, A_KV), lambda b, i, s: (base + b * nblk + i, col))

    def side_spec(col, off):
        return pl.BlockSpec((A_WINDOW, A_KV),
                            lambda b, i, s: (side_base + b * nside + jnp.clip(side * i + off, 0, nside - 1), col))

    ctx_spec = pl.BlockSpec((None, PAST, A_KV), lambda b, i, s: (b, 0, 0))
    return pl.pallas_call(
        _win_kernel,
        grid_spec=pltpu.PrefetchScalarGridSpec(
            num_scalar_prefetch=1,
            grid=(DEC_BATCH, nblk),
            in_specs=[pl.BlockSpec((WIN_Q, A_Q), lambda b, i, s: (base + b * nblk + i, 0)),
                      side_spec(kcol, -1), main_spec(kcol), side_spec(kcol, side),
                      side_spec(kcol + 1, -1), main_spec(kcol + 1), side_spec(kcol + 1, side),
                      ctx_spec, ctx_spec],
            out_specs=pl.BlockSpec((WIN_Q, A_Q), lambda b, i, s: (b * nblk + i, 0))),
        out_shape=jax.ShapeDtypeStruct((NS_TOK, A_Q), F32),
        compiler_params=_cparams(2),
        name="win_attn",
    )(sink, p, p, p, p, p, p, p, cache_k.reshape(DEC_BATCH, PAST, A_KV), cache_v.reshape(DEC_BATCH, PAST, A_KV))


def _ret_kernel(df_ref, db_ref, q_ref, k_ref, v_ref, g_ref, gn_ref, s0f_ref, s0b_ref,
                o_ref, sf_ref, sb_ref, of_scr, ob_scr, *, length):
    c_len = RET_CHUNK
    n = length // c_len
    lo = _lane_lo()
    hi = jnp.logical_not(lo)
    row = lax.broadcasted_iota(jnp.int32, (c_len, c_len), 0)
    col = lax.broadcasted_iota(jnp.int32, (c_len, c_len), 1)
    rowp = lax.broadcasted_iota(jnp.int32, (LANES, LANES), 0)
    colp = lax.broadcasted_iota(jnp.int32, (LANES, LANES), 1)
    blockdiag = (rowp < HD) == (colp < HD)
    idx = lax.broadcasted_iota(jnp.int32, (c_len, 1), 0).astype(F32)

    def direction(dec_ref, forward):
        lg = -jnp.exp(dec_ref[...])
        diff = (row - col) if forward else (col - row)
        keep = (diff >= 0) if forward else (diff > 0)
        dist = jnp.maximum(diff, 0).astype(F32)
        dm = jnp.concatenate([jnp.where(keep, jnp.exp(dist * lg[:, off:off + 1]), 0.0) for off in (0, HD)], axis=0)
        if forward:
            xi = jnp.exp((idx + 1.0) * lg)
            zeta = jnp.exp((c_len - 1.0 - idx) * lg)
        else:
            xi = jnp.exp((c_len - idx) * lg)
            zeta = jnp.exp(idx * lg)
        return dm, xi, zeta, jnp.exp(c_len * lg)

    def chunk(c, state, consts):
        dm, xi, zeta, gch = consts
        rows = pl.ds(pl.multiple_of(c * c_len, c_len), c_len)
        qc = q_ref[rows, :]
        kc = k_ref[rows, :] * HD ** -0.5
        vc = v_ref[rows, :].astype(BF16)
        kb = kc.astype(BF16)
        q2 = jnp.concatenate([jnp.where(lo, qc, 0.0), jnp.where(hi, qc, 0.0)], axis=0).astype(BF16)
        inner = lax.dot_general(q2, kb, NT_DIMS, preferred_element_type=F32) * dm
        kz_t = (kc * zeta).T
        res = jnp.dot(jnp.concatenate([inner, kz_t], axis=0).astype(BF16), vc, preferred_element_type=F32)
        cross = jnp.dot(qc.astype(BF16), state.astype(BF16), preferred_element_type=F32) * xi
        o = jnp.where(lo, res[:c_len], res[c_len:2 * c_len]) + cross
        state = gch * state + jnp.where(blockdiag, res[2 * c_len:], 0.0)
        return rows, o, state

    cf = direction(df_ref, True)
    cb = direction(db_ref, False)

    def scan_body(t, states):
        rows_f, o_f, state_f = chunk(t, states[0], cf)
        of_scr[rows_f, :] = o_f
        rows_b, o_b, state_b = chunk(n - 1 - t, states[1], cb)
        ob_scr[rows_b, :] = o_b
        return state_f, state_b

    state_f, state_b = lax.fori_loop(0, n, scan_body, (s0f_ref[...], s0b_ref[...]), unroll=min(n, RET_UNROLL))
    sf_ref[...] = state_f
    sb_ref[...] = state_b

    gn = gn_ref[...]
    norm_rows = min(RET_NORM_ROWS, length)
    n_norm = length // norm_rows

    def per_head(x):
        a = jnp.where(lo, x, 0.0).sum(axis=-1, keepdims=True)
        b = jnp.where(hi, x, 0.0).sum(axis=-1, keepdims=True)
        return jnp.where(lo, a, b) * (1.0 / HD)

    def norm_body(t, carry):
        rows = pl.ds(pl.multiple_of(t * norm_rows, norm_rows), norm_rows)
        o = of_scr[rows, :] + ob_scr[rows, :]
        d = o - per_head(o)
        y = d * lax.rsqrt(per_head(d * d) + EPS) * gn
        o_ref[rows, :] = _silu(g_ref[rows, :]) * y
        return carry

    lax.fori_loop(0, n_norm, norm_body, 0)


def _pair_lanes(v):
    return jnp.repeat(v.astype(F32), HD).reshape(B_HEADS // 2, 1, LANES)


def _blockdiag_states(s):
    b = s.shape[0]
    s = s.astype(F32).reshape(b, B_HEADS // 2, 2, HD, HD)
    z = jnp.zeros_like(s[:, :, 0])
    top = jnp.concatenate([s[:, :, 0], z], axis=-1)
    bot = jnp.concatenate([z, s[:, :, 1]], axis=-1)
    return jnp.concatenate([top, bot], axis=-2)


def _diag_states(sp):
    b = sp.shape[0]
    s = jnp.stack([sp[:, :, :HD, :HD], sp[:, :, HD:, HD:]], axis=2)
    return s.reshape(b, B_HEADS, HD, HD)


def _retention(p, row_base, batch, length, dec_f, dec_b, gn_g, s0f, s0b):
    npairs = B_HEADS // 2
    blk0 = row_base // length
    qcol = (A_Q + 2 * A_KV) // LANES

    def col_spec(off):
        return pl.BlockSpec((length, LANES), lambda b, h: (blk0 + b, qcol + off * npairs + h))

    lane_spec = pl.BlockSpec((None, 1, LANES), lambda b, h: (h, 0, 0))
    state_spec = pl.BlockSpec((None, None, LANES, LANES), lambda b, h: (b, h, 0, 0))
    state_shape = jax.ShapeDtypeStruct((batch, npairs, LANES, LANES), F32)
    return pl.pallas_call(
        functools.partial(_ret_kernel, length=length),
        grid=(batch, npairs),
        in_specs=[lane_spec, lane_spec, col_spec(0), col_spec(1), col_spec(2), col_spec(3), lane_spec,
                  state_spec, state_spec],
        out_specs=[pl.BlockSpec((length, LANES), lambda b, h: (b, h)), state_spec, state_spec],
        out_shape=[jax.ShapeDtypeStruct((batch * length, B_W), F32), state_shape, state_shape],
        scratch_shapes=[pltpu.VMEM((length, LANES), F32), pltpu.VMEM((length, LANES), F32)],
        compiler_params=_cparams(2),
        name="retention",
    )(_pair_lanes(dec_f), _pair_lanes(dec_b), p, p, p, p, gn_g.reshape(npairs, 1, LANES), s0f, s0b)


def _ctx_mha_kernel(q_ref, k_ref, v_ref, o_ref):
    lo = _lane_lo()
    for pair in range(C_HEADS // 2):
        cols = slice(pair * LANES, (pair + 1) * LANES)
        q = _stack_heads(q_ref, [2 * pair, 2 * pair + 1], lo, HD ** -0.5)
        s = lax.dot_general(q, k_ref[:, cols].astype(BF16), NT_DIMS, preferred_element_type=F32)
        o = _softmax_av([s], [v_ref[:, cols].astype(BF16)])
        o_ref[:, cols] = jnp.where(lo, o[:SEQ], o[SEQ:])


def _ctx_mha(p):
    return pl.pallas_call(
        _ctx_mha_kernel,
        grid=(BATCH,),
        in_specs=[pl.BlockSpec((SEQ, C_W), lambda b: (b, 0)),
                  pl.BlockSpec((SEQ, C_W), lambda b: (b, 1)),
                  pl.BlockSpec((SEQ, C_W), lambda b: (b, 2))],
        out_specs=pl.BlockSpec((SEQ, C_W), lambda b: (b, 0)),
        out_shape=jax.ShapeDtypeStruct((NP_TOK, C_W), F32),
        compiler_params=_cparams(1),
        name="ctx_mha",
    )(p, p, p)


NA_WIN_ROWS = 2 * NA_ROWS
NA_WIN = NA_WIN_ROWS * GRID_W
NA_QROWS = NA_ROWS * GRID_W
NA_PAD_ROWS = NA_KH // 2
NA_TABLE = 1536


def _na_kernel(q_ref, kp_ref, km_ref, kn_ref, vp_ref, vm_ref, vn_ref, ck_ref, cv_ref, ue_ref, uo_ref, o_ref):
    r0 = pl.program_id(2) * NA_ROWS
    n_rows = DEC_SEQ // GRID_W
    lo = _lane_lo()
    k = jnp.concatenate([kp_ref[...], km_ref[...], kn_ref[...]], axis=0).astype(BF16)
    v = jnp.concatenate([vp_ref[...], vm_ref[...], vn_ref[...]], axis=0).astype(BF16)
    ck = ck_ref[...].astype(BF16)
    cv = cv_ref[...].astype(BF16)
    q = q_ref[...] * HD ** -0.5
    klane = lax.broadcasted_iota(jnp.int32, (1, NA_WIN), 1)
    outs = []
    for half, keep in enumerate((lo, jnp.logical_not(lo))):
        qh = jnp.where(keep, q, 0.0).astype(BF16)
        s = lax.dot_general(qh, k, NT_DIMS, preferred_element_type=F32)
        s_ctx = lax.dot_general(qh, ck, NT_DIMS, preferred_element_type=F32)
        p_loc, p_ctx, den = [], [], []
        for rq in range(NA_ROWS):
            rows = slice(rq * GRID_W, (rq + 1) * GRID_W)
            start = NA_KH - 1 - rq
            if start % 2 == 0:
                u = ue_ref[half, :, start * GRID_W:start * GRID_W + NA_WIN]
            else:
                u = uo_ref[half, :, (start - 1) * GRID_W:(start - 1) * GRID_W + NA_WIN]
            r = r0 + rq
            first = jnp.clip(r - NA_KH // 2, 0, n_rows - NA_KH)
            lane0 = (first - r0 + NA_PAD_ROWS) * GRID_W
            in_rows = (klane >= lane0) & (klane < lane0 + NA_KH * GRID_W)
            sl = jnp.where(in_rows, s[rows] + u, NEG)
            sc = s_ctx[rows]
            mx = jnp.maximum(sl.max(axis=-1, keepdims=True), sc.max(axis=-1, keepdims=True))
            el = jnp.exp(sl - mx)
            ec = jnp.exp(sc - mx)
            den.append(el.sum(axis=-1, keepdims=True) + ec.sum(axis=-1, keepdims=True))
            p_loc.append(el.astype(BF16))
            p_ctx.append(ec.astype(BF16))
        acc = (jnp.dot(jnp.concatenate(p_loc, axis=0), v, preferred_element_type=F32)
               + jnp.dot(jnp.concatenate(p_ctx, axis=0), cv, preferred_element_type=F32))
        outs.append(acc / jnp.concatenate(den, axis=0))
    o_ref[...] = jnp.where(lo, outs[0], outs[1])


def _na_bias_tables(rpb):
    cq = jnp.arange(GRID_W)
    ck = jnp.arange(GRID_W)
    dc = jnp.clip(ck[None] - cq[:, None], -(NA_KW - 1), NA_KW - 1) + NA_KW - 1
    cs = jnp.clip(cq - NA_KW // 2, 0, GRID_W - NA_KW)
    col_ok = (ck[None] >= cs[:, None]) & (ck[None] < cs[:, None] + NA_KW)
    t = rpb.astype(F32)[:, :, dc]
    t = jnp.where(col_ok[None, None], t, NEG).transpose(0, 2, 1, 3)
    n_dr = 2 * NA_KH - 1
    blocks = NA_TABLE // GRID_W
    t = jnp.pad(t, ((0, 0), (0, 0), (NA_PAD_ROWS, blocks - n_dr - NA_PAD_ROWS), (0, 0)), constant_values=NEG)
    ue = t.reshape(C_HEADS, GRID_W, NA_TABLE)
    uo = jnp.concatenate([ue[..., GRID_W:], jnp.full((C_HEADS, GRID_W, GRID_W), NEG, F32)], axis=-1)
    return ue, uo


def _na_attention(p, cache_k, cache_v, rpb):
    npairs = C_HEADS // 2
    nrb = DEC_SEQ // NA_QROWS
    half = NA_QROWS // 2
    qbase = NP_TOK // NA_QROWS
    hbase = NP_TOK // half
    kcol = C_W // LANES
    ue, uo = _na_bias_tables(rpb)

    def main_spec(col0):
        return pl.BlockSpec((NA_QROWS, LANES), lambda b, h, r: (qbase + b * nrb + r, col0 + h))

    def side_spec(col0, off):
        return pl.BlockSpec((half, LANES),
                            lambda b, h, r: (hbase + b * 2 * nrb + jnp.clip(2 * r + off, 0, 2 * nrb - 1), col0 + h))

    ctx_spec = pl.BlockSpec((None, PAST, LANES), lambda b, h, r: (b, 0, h))
    tab_spec = pl.BlockSpec((2, GRID_W, NA_TABLE), lambda b, h, r: (h, 0, 0))
    return pl.pallas_call(
        _na_kernel,
        grid=(DEC_BATCH, npairs, nrb),
        in_specs=[main_spec(0),
                  side_spec(kcol, -1), main_spec(kcol), side_spec(kcol, 2),
                  side_spec(2 * kcol, -1), main_spec(2 * kcol), side_spec(2 * kcol, 2),
                  ctx_spec, ctx_spec, tab_spec, tab_spec],
        out_specs=pl.BlockSpec((NA_QROWS, LANES), lambda b, h, r: (b * nrb + r, h)),
        out_shape=jax.ShapeDtypeStruct((NS_TOK, C_W), F32),
        compiler_params=_cparams(3),
        name="na_attn",
    )(p, p, p, p, p, p, p, cache_k.reshape(DEC_BATCH, PAST, C_W), cache_v.reshape(DEC_BATCH, PAST, C_W), ue, uo)


def _route(biased, scores):
    t = biased.shape[1]
    per_group = N_EXPERTS // N_GROUPS
    i8 = lax.broadcasted_iota(jnp.int32, (per_group, t), 0)
    g_rows = []
    for g in range(N_GROUPS):
        bg = biased[g * per_group:(g + 1) * per_group]
        m1 = bg.max(axis=0, keepdims=True)
        first = jnp.where(bg == m1, i8, per_group).min(axis=0, keepdims=True)
        m2 = jnp.where(i8 == first, -jnp.inf, bg).max(axis=0, keepdims=True)
        g_rows.append(m1 + m2)
    g_top = jnp.concatenate(g_rows, axis=0)
    gi = lax.broadcasted_iota(jnp.int32, g_top.shape, 0)
    g_sel = jnp.zeros(g_top.shape, jnp.int32)
    cur = g_top
    for _ in range(TOPK_GROUPS):
        m = cur.max(axis=0, keepdims=True)
        hit = gi == jnp.where(cur == m, gi, N_GROUPS).min(axis=0, keepdims=True)
        g_sel = jnp.where(hit, 1, g_sel)
        cur = jnp.where(hit, -jnp.inf, cur)
    e_sel = jnp.concatenate([jnp.broadcast_to(g_sel[g:g + 1], (per_group, t)) for g in range(N_GROUPS)], axis=0)
    cur = jnp.where(e_sel > 0, biased, NEG)
    ei = lax.broadcasted_iota(jnp.int32, cur.shape, 0)
    ids, gates, hits = [], [], []
    for _ in range(TOP_K):
        m = cur.max(axis=0, keepdims=True)
        f = jnp.where(cur == m, ei, N_EXPERTS).min(axis=0, keepdims=True)
        hit = ei == f
        ids.append(f)
        hits.append(hit)
        gates.append(jnp.where(hit, scores, 0.0).sum(axis=0, keepdims=True))
        cur = jnp.where(hit, -jnp.inf, cur)
    gate = jnp.concatenate(gates, axis=0)
    gate = gate / gate.sum(axis=0, keepdims=True) * ROUTED_SCALE
    return jnp.concatenate(ids, axis=0), gate, hits


def _pack_bf16_pairs(h):
    bits = lax.bitcast_convert_type(h.astype(BF16).astype(F32), jnp.uint32)
    return bits[:, :D // 2] | (bits[:, D // 2:] >> 16)


def _unpack_bf16_pairs(xp):
    hi = lax.bitcast_convert_type(xp & jnp.uint32(0xFFFF0000), F32).astype(BF16)
    lo = lax.bitcast_convert_type(xp << 16, F32).astype(BF16)
    return hi, lo


def _dot_halves(hi, lo, w_ref):
    return (jnp.dot(hi, w_ref[:D // 2, :], preferred_element_type=F32)
            + jnp.dot(lo, w_ref[D // 2:, :], preferred_element_type=F32))


def _outproj_kernel(*refs, n_parts):
    xp_ref, xs_ref = refs[:2]
    part_refs = refs[2:2 + 3 * n_parts]
    gate_ref, shift_ref, scale_ref, g2_ref, rw_ref, rb_ref = refs[2 + 3 * n_parts:8 + 3 * n_parts]
    xo_ref, h_ref, dest_ref, wgt_ref, plan_ref, cnt_ref = refs[8 + 3 * n_parts:]
    step = pl.program_id(0)

    @pl.when(step == 0)
    def _():
        cnt_ref[...] = jnp.zeros_like(cnt_ref)
        plan_ref[...] = jnp.zeros_like(plan_ref)

    y = None
    for t in range(n_parts):
        ap_ref, as_ref, w_ref = part_refs[3 * t:3 * t + 3]
        d = jnp.dot(_pick_rows(ap_ref, as_ref, TM_OUT).astype(BF16), w_ref[...], preferred_element_type=F32)
        y = d if y is None else y + d
    x = _pick_rows(xp_ref, xs_ref, TM_OUT) + gate_ref[...] * y
    xo_ref[...] = x
    h = _rms(x, g2_ref[...]) * (1.0 + scale_ref[...]) + shift_ref[...]
    h_ref[...] = _pack_bf16_pairs(h)
    h_hi = h.astype(BF16)
    h_lo = (h - h_hi.astype(F32)).astype(BF16)
    rw = rw_ref[...]
    rw_hi = rw.astype(BF16)
    rw_lo = (rw - rw_hi.astype(F32)).astype(BF16)
    logits = (lax.dot_general(rw_hi, h_hi, NT_DIMS, preferred_element_type=F32)
              + lax.dot_general(rw_hi, h_lo, NT_DIMS, preferred_element_type=F32)
              + lax.dot_general(rw_lo, h_hi, NT_DIMS, preferred_element_type=F32))
    scores = jax.nn.sigmoid(logits)
    _, gate, hits = _route(scores + rb_ref[...], scores)
    wgt_ref[...] = gate
    chosen = hits[0]
    for hit in hits[1:]:
        chosen = chosen | hit
    m = jnp.where(chosen, 1.0, 0.0)
    before = (lax.broadcasted_iota(jnp.int32, (TM_OUT, TM_OUT), 0)
              < lax.broadcasted_iota(jnp.int32, (TM_OUT, TM_OUT), 1))
    prefix = jnp.dot(m.astype(BF16), jnp.where(before, 1.0, 0.0).astype(BF16), preferred_element_type=F32)
    e_base = (lax.broadcasted_iota(jnp.int32, (N_EXPERTS, 1), 0) * N_TOK).astype(F32)
    row_all = prefix + (cnt_ref[...] + e_base)
    dest_ref[...] = jnp.concatenate(
        [jnp.where(hit, row_all, 0.0).sum(axis=0, keepdims=True) for hit in hits], axis=0).astype(jnp.int32)
    cnt_ref[...] += m.sum(axis=1, keepdims=True)

    @pl.when(step == pl.num_programs(0) - 1)
    def _():
        _block_plan(cnt_ref[...], plan_ref)


def _block_plan(counts, plan_ref):
    cap_blocks = N_TOK // MOE_BLOCK
    nblk = ((counts.astype(jnp.int32) + (MOE_BLOCK - 1)) // MOE_BLOCK).astype(F32)
    lower = (lax.broadcasted_iota(jnp.int32, (N_EXPERTS, N_EXPERTS), 0)
             >= lax.broadcasted_iota(jnp.int32, (N_EXPERTS, N_EXPERTS), 1))
    cum = jnp.dot(jnp.where(lower, 1.0, 0.0).astype(BF16), jnp.broadcast_to(nblk, (N_EXPERTS, LANES)).astype(BF16),
                  preferred_element_type=F32)[:, :1]
    n_used = cum[N_EXPERTS - 1:, :]
    slot = jnp.minimum(lax.broadcasted_iota(jnp.int32, (1, PLAN_LANES), 1).astype(F32), n_used - 1.0)
    done = cum <= slot
    expert = jnp.where(done, 1.0, 0.0).sum(axis=0, keepdims=True)
    blocks_before = jnp.where(done, nblk, 0.0).sum(axis=0, keepdims=True)
    plan_ref[0:1, :] = (expert * cap_blocks + (slot - blocks_before)).astype(jnp.int32)
    plan_ref[1:2, :] = expert.astype(jnp.int32)
    plan_ref[2:3, :] = jnp.broadcast_to(n_used, (1, PLAN_LANES)).astype(jnp.int32)
    run_end = jnp.where(cum > slot, cum, jnp.inf).min(axis=0, keepdims=True)
    plan_ref[3:4, :] = jnp.minimum(jnp.where(cum <= run_end, 1.0, 0.0).sum(axis=0, keepdims=True),
                                   N_EXPERTS - 1.0).astype(jnp.int32)
    plan_ref[4:5, :] = jnp.where(run_end < n_used, 1, 0).astype(jnp.int32)


def _outproj(x, parts, mod, g2, router_w, router_b):
    in_specs = _pair_specs(D, TM_OUT)
    args = [x[0], x[1]]
    for ap, a_s, w in parts:
        width = ap.shape[1]
        in_specs += _pair_specs(width, TM_OUT) + [pl.BlockSpec((width, D), lambda i: (0, 0))]
        args += [ap, a_s, w]
    in_specs += [_mod_spec(2, TM_OUT), _mod_spec(3, TM_OUT), _mod_spec(4, TM_OUT),
                 pl.BlockSpec((1, D), lambda i: (0, 0)),
                 pl.BlockSpec((N_EXPERTS, D), lambda i: (0, 0)),
                 pl.BlockSpec((N_EXPERTS, 1), lambda i: (0, 0))]
    args += [mod, mod, mod, g2.reshape(1, D), router_w.T, router_b.reshape(N_EXPERTS, 1)]
    return pl.pallas_call(
        functools.partial(_outproj_kernel, n_parts=len(parts)),
        grid=(N_TOK // TM_OUT,),
        in_specs=in_specs,
        out_specs=[pl.BlockSpec((TM_OUT, D), lambda i: (i, 0)),
                   pl.BlockSpec((TM_OUT, D // 2), lambda i: (i, 0)),
                   pl.BlockSpec((TOP_K, TM_OUT), lambda i: (0, i)),
                   pl.BlockSpec((TOP_K, TM_OUT), lambda i: (0, i)),
                   pl.BlockSpec((8, PLAN_LANES), lambda i: (0, 0))],
        out_shape=[jax.ShapeDtypeStruct((N_TOK, D), F32),
                   jax.ShapeDtypeStruct((N_TOK, D // 2), jnp.uint32),
                   jax.ShapeDtypeStruct((TOP_K, N_TOK), jnp.int32),
                   jax.ShapeDtypeStruct((TOP_K, N_TOK), F32),
                   jax.ShapeDtypeStruct((8, PLAN_LANES), jnp.int32)],
        scratch_shapes=[pltpu.VMEM((N_EXPERTS, 1), F32)],
        compiler_params=_cparams(1),
        name="outproj_router",
    )(*args)


def _experts_kernel(br_ref, be_ref, nu_ref, ne_ref, hn_ref, x_hbm, w1_hbm, w3_hbm, w2_hbm, o_ref,
                    w1b, w3b, w2b, w1f, w3f, w2f, xbuf, xsem, wsem, *, layer):
    i = pl.program_id(0)
    n_steps = pl.num_programs(0)
    e = be_ref[i]
    prev = be_ref[jnp.maximum(i - 1, 0)]

    def x_copy(step):
        slot = step % EXPERT_X_SLOTS
        rows = pl.ds(pl.multiple_of(br_ref[step] * MOE_BLOCK, MOE_BLOCK), MOE_BLOCK)
        return pltpu.make_async_copy(x_hbm.at[rows], xbuf.at[slot], xsem.at[slot])

    @pl.when(i == 0)
    def _():
        x_copy(0).start()
        x_copy(1).start()

    @pl.when(i + 2 < n_steps)
    def _():
        x_copy(i + 2).start()

    x_copy(i).wait()

    def w_copies(expert):
        return [pltpu.make_async_copy(src.at[layer, expert], dst, wsem.at[k])
                for k, (src, dst) in enumerate(((w1_hbm, w1f), (w3_hbm, w3f), (w2_hbm, w2f)))]

    @pl.when(i == 0)
    def _():
        for c in w_copies(e):
            c.start()

    @pl.when((i == 0) | (e != prev))
    def _():
        for c in w_copies(e):
            c.wait()
        w1b[...] = w1f[...].astype(BF16)
        w3b[...] = w3f[...].astype(BF16)
        w2b[...] = w2f[...].astype(BF16)

        @pl.when(hn_ref[i] > 0)
        def _():
            for c in w_copies(ne_ref[i]):
                c.start()

    @pl.when(i < nu_ref[0])
    def _():
        hi, lo = _unpack_bf16_pairs(xbuf[i % EXPERT_X_SLOTS])
        a = _dot_halves(hi, lo, w1b)
        b = _dot_halves(hi, lo, w3b)
        h = (_silu(a) * b).astype(BF16)
        o_ref[...] = _pack_bf16_pairs(jnp.dot(h, w2b[...], preferred_element_type=F32))


def _experts(plan, x_rows, w1, w3, w2, layer):
    any_spec = pl.BlockSpec(memory_space=pl.ANY)
    return pl.pallas_call(
        functools.partial(_experts_kernel, layer=layer),
        grid_spec=pltpu.PrefetchScalarGridSpec(
            num_scalar_prefetch=5,
            grid=(N_MOE_BLOCKS,),
            in_specs=[any_spec, any_spec, any_spec, any_spec],
            out_specs=pl.BlockSpec((MOE_BLOCK, D // 2), lambda i, br, be, nu, ne, hn: (br[i], 0)),
            scratch_shapes=[pltpu.VMEM((D, FF), BF16), pltpu.VMEM((D, FF), BF16), pltpu.VMEM((FF, D), BF16),
                            pltpu.VMEM((D, FF), F32), pltpu.VMEM((D, FF), F32), pltpu.VMEM((FF, D), F32),
                            pltpu.VMEM((EXPERT_X_SLOTS, MOE_BLOCK, D // 2), jnp.uint32),
                            pltpu.SemaphoreType.DMA((EXPERT_X_SLOTS,)),
                            pltpu.SemaphoreType.DMA((3,))]),
        out_shape=jax.ShapeDtypeStruct(x_rows.shape, jnp.uint32),
        compiler_params=_cparams(1),
        name="experts",
    )(plan[0], plan[1], plan[2, :1], plan[3], plan[4], x_rows, w1, w3, w2)


SC_CORES = 2
SC_SUBCORES = 16
SC_WORKERS = SC_CORES * SC_SUBCORES
SC_CHUNK_BYTES = 64 * 1024
SC_SLOTS = 4


def _sc_scatter(rows, dest, n_out):
    n_rows, width = rows.shape
    picks = dest.shape[0]
    chunk = SC_CHUNK_BYTES // (4 * width)
    per_worker = n_rows // SC_WORKERS
    n_chunks = per_worker // chunk
    assert per_worker * SC_WORKERS == n_rows and n_chunks * chunk == per_worker and n_chunks % 2 == 0
    mesh = plsc.VectorSubcoreMesh(core_axis_name="c", subcore_axis_name="s")

    @functools.partial(
        pl.kernel, mesh=mesh,
        out_type=jax.ShapeDtypeStruct((n_out, width), rows.dtype),
        scratch_types=[pltpu.VMEM((picks, n_chunks, chunk), jnp.int32),
                       pltpu.VMEM((2, chunk, width), rows.dtype),
                       pltpu.SemaphoreType.DMA((2,)),
                       pltpu.SemaphoreType.DMA((2,))])
    def scatter(r_hbm, d_hbm, o_hbm, idx_v, rows_v, lsem, ssem):
        worker = lax.axis_index("s") * SC_CORES + lax.axis_index("c")
        base = worker * per_worker
        for k in range(picks):
            pltpu.sync_copy(d_hbm.at[k, worker], idx_v.at[k])

        def load_copy(c, b):
            src = pl.ds(pl.multiple_of(base + c * chunk, chunk), chunk)
            return pltpu.make_async_copy(r_hbm.at[src], rows_v.at[b], lsem.at[b])

        def store_copy(c, b, k):
            return pltpu.make_async_copy(rows_v.at[b], o_hbm.at[idx_v.at[k, c]], ssem.at[b])

        load_copy(0, 0).start()

        @pl.loop(0, n_chunks, step=2)
        def _(c0):
            for b in range(2):
                c = c0 + b
                load_copy(c, b).wait()
                for k in range(picks):
                    store_copy(c, b, k).start()

                @pl.when(c > 0)
                def _():
                    for k in range(picks):
                        store_copy(c - 1, 1 - b, k).wait()

                @pl.when(c + 1 < n_chunks)
                def _():
                    load_copy(c + 1, 1 - b).start()

        for k in range(picks):
            store_copy(n_chunks - 1, 1, k).wait()

    return scatter(rows, dest.reshape(picks, SC_WORKERS, n_chunks, chunk))


def _sc_gather(table, idx):
    n_idx = idx.shape[0]
    width = table.shape[1]
    chunk = SC_CHUNK_BYTES // (4 * width)
    per_worker = n_idx // SC_WORKERS
    n_chunks = per_worker // chunk
    ahead = SC_SLOTS - 1
    assert per_worker * SC_WORKERS == n_idx and n_chunks * chunk == per_worker and n_chunks % SC_SLOTS == 0
    mesh = plsc.VectorSubcoreMesh(core_axis_name="c", subcore_axis_name="s")

    @functools.partial(
        pl.kernel, mesh=mesh,
        out_type=jax.ShapeDtypeStruct((n_idx, width), table.dtype),
        scratch_types=[pltpu.VMEM((per_worker,), jnp.int32),
                       pltpu.VMEM((SC_SLOTS, chunk, width), table.dtype),
                       pltpu.SemaphoreType.DMA((SC_SLOTS,)),
                       pltpu.SemaphoreType.DMA((SC_SLOTS,))])
    def gather(t_hbm, i_hbm, o_hbm, idx_v, rows_v, gsem, wsem):
        worker = lax.axis_index("s") * SC_CORES + lax.axis_index("c")
        base = worker * per_worker
        pltpu.sync_copy(i_hbm.at[pl.ds(pl.multiple_of(base, chunk), per_worker)], idx_v)

        def gather_copy(c, b):
            ids = idx_v.at[pl.ds(pl.multiple_of(c * chunk, chunk), chunk)]
            return pltpu.make_async_copy(t_hbm.at[ids], rows_v.at[b], gsem.at[b])

        def write_copy(c, b):
            rows = pl.ds(pl.multiple_of(base + c * chunk, chunk), chunk)
            return pltpu.make_async_copy(rows_v.at[b], o_hbm.at[rows], wsem.at[b])

        for c in range(ahead):
            gather_copy(c, c).start()

        @pl.loop(0, n_chunks, step=SC_SLOTS)
        def _(c0):
            for b in range(SC_SLOTS):
                c = c0 + b
                refill = (b + ahead) % SC_SLOTS
                gather_copy(c, b).wait()
                write_copy(c, b).start()

                @pl.when(c > 0)
                def _():
                    write_copy(c - 1, refill).wait()

                @pl.when(c + ahead < n_chunks)
                def _():
                    gather_copy(c + ahead, refill).start()

        write_copy(n_chunks - 1, (n_chunks - 1) % SC_SLOTS).wait()

    return gather(table, idx)


TC = 512


def _combine_kernel(x_ref, h_ref, y_ref, wgt_ref, gate_ref, w1_ref, w3_ref, w2_ref, fg_ref, o_ref, *, final):
    hi, lo = _unpack_bf16_pairs(h_ref[...])
    a = _dot_halves(hi, lo, w1_ref)
    b = _dot_halves(hi, lo, w3_ref)
    ffn = jnp.dot((_silu(a) * b).astype(BF16), w2_ref[...], preferred_element_type=F32)
    wgt = wgt_ref[...]
    r_hi = None
    r_lo = None
    for k in range(TOP_K):
        yk = y_ref[k]
        w = wgt[:, k:k + 1]
        t_hi = lax.bitcast_convert_type(yk & jnp.uint32(0xFFFF0000), F32) * w
        t_lo = lax.bitcast_convert_type(yk << 16, F32) * w
        r_hi = t_hi if r_hi is None else r_hi + t_hi
        r_lo = t_lo if r_lo is None else r_lo + t_lo
    x = x_ref[...] + gate_ref[...] * (ffn + jnp.concatenate([r_hi, r_lo], axis=1))
    o_ref[...] = _rms(x, fg_ref[...]) if final else x


def _combine(x, h, y_rows, wgt, mod, sw1, sw3, sw2, final_g, final):
    weights = (sw1.astype(BF16), sw3.astype(BF16), sw2.astype(BF16), final_g.reshape(1, D))

    def rows_from(first_row, n_rows):
        b0 = first_row // TC
        return pl.pallas_call(
            functools.partial(_combine_kernel, final=final),
            grid=(n_rows // TC,),
            in_specs=[pl.BlockSpec((TC, D), lambda i: (i + b0, 0)),
                      pl.BlockSpec((TC, D // 2), lambda i: (i + b0, 0)),
                      pl.BlockSpec((TOP_K, TC, D // 2), lambda i: (0, i + b0, 0)),
                      pl.BlockSpec((TC, TOP_K), lambda i: (i + b0, 0)),
                      _mod_spec(5, TC, b0),
                      pl.BlockSpec((D, FF), lambda i: (0, 0)),
                      pl.BlockSpec((D, FF), lambda i: (0, 0)),
                      pl.BlockSpec((FF, D), lambda i: (0, 0)),
                      pl.BlockSpec((1, D), lambda i: (0, 0))],
            out_specs=pl.BlockSpec((TC, D), lambda i: (i, 0)),
            out_shape=jax.ShapeDtypeStruct((n_rows, D), F32),
            compiler_params=_cparams(1),
            name="combine",
        )(x, h, y_rows, wgt, mod, *weights)

    return rows_from(0, NP_TOK), rows_from(NP_TOK, NS_TOK)


def kernel(x_prompt, x_sample, cache_a_k, cache_a_v, state_ret_fwd, state_ret_bwd, cache_c_k, cache_c_v,
           c, c_ctx, norm1_g, norm2_g, ada_w, ada_b, even_w_in, even_w_out, sink_a, ret_decay_fwd,
           ret_decay_bwd, ret_gn_g, odd_w_in, odd_w_out, na_rpb, router_w, router_b, exp_w1, exp_w3,
           exp_w2, sh_w1, sh_w3, sh_w2, final_g):
    x = (x_prompt.reshape(NP_TOK, D), x_sample.reshape(NS_TOK, D))
    cc = jnp.concatenate([c_ctx[None], c, jnp.zeros((8 - 1 - DEC_BATCH, D), F32)], axis=0)
    rope = _rope_tables()
    outs = {}
    for l in range(2):
        mod = _ada(cc, ada_w, ada_b, l)
        if l == 0:
            p = _inproj(x, norm1_g[l], mod, even_w_in[0].astype(BF16), rope, A_Q + A_KV)
            oa_p = _ctx_gqa(p, sink_a[0])
            oa_s = _win_attention(p, cache_a_k[:, 0], cache_a_v[:, 0], sink_a[0])
            zero = jnp.zeros((BATCH, B_HEADS // 2, LANES, LANES), F32)
            ob_p, sf, sb = _retention(p, 0, BATCH, SEQ, ret_decay_fwd[0], ret_decay_bwd[0], ret_gn_g[0], zero, zero)
            ob_s, _, _ = _retention(p, NP_TOK, DEC_BATCH, DEC_SEQ, ret_decay_fwd[0], ret_decay_bwd[0], ret_gn_g[0],
                                    _blockdiag_states(state_ret_fwd[:, 0]), _blockdiag_states(state_ret_bwd[:, 0]))
            w_out = even_w_out[0].astype(BF16)
            parts = [(oa_p, oa_s, w_out[:A_Q]), (ob_p, ob_s, w_out[A_Q:])]
            outs["a_k"] = p[:NP_TOK, A_Q:A_Q + A_KV].reshape(BATCH, 1, SEQ, A_KV_HEADS, HD)
            outs["a_v"] = p[:NP_TOK, A_Q + A_KV:A_Q + 2 * A_KV].reshape(BATCH, 1, SEQ, A_KV_HEADS, HD)
            outs["r_f"] = _diag_states(sf).reshape(BATCH, 1, B_HEADS, HD, HD)
            outs["r_b"] = _diag_states(sb).reshape(BATCH, 1, B_HEADS, HD, HD)
        else:
            p = _inproj(x, norm1_g[l], mod, odd_w_in[0].astype(BF16), rope, 0)
            o_p = _ctx_mha(p)
            o_s = _na_attention(p, cache_c_k[:, 0], cache_c_v[:, 0], na_rpb[0])
            parts = [(o_p, o_s, odd_w_out[0].astype(BF16))]
            outs["c_k"] = p[:NP_TOK, C_W:2 * C_W].reshape(BATCH, 1, SEQ, C_HEADS, HD)
            outs["c_v"] = p[:NP_TOK, 2 * C_W:3 * C_W].reshape(BATCH, 1, SEQ, C_HEADS, HD)
        x_mid, h, dest, gate_t, plan = _outproj(x, parts, mod, norm2_g[l], router_w[l], router_b[l])
        y = _experts(plan, _sc_scatter(h, dest, N_EXPERTS * N_TOK), exp_w1, exp_w3, exp_w2, l)
        y_rows = _sc_gather(y, dest.reshape(N_ASSIGN)).reshape(TOP_K, N_TOK, D // 2)
        x = _combine(x_mid, h, y_rows, gate_t.T, mod, sh_w1[l], sh_w3[l], sh_w2[l], final_g, final=(l == 1))
    y_prompt = x[0].reshape(BATCH, SEQ, D)
    y_sample = x[1].reshape(DEC_BATCH, DEC_SEQ, D)
    return (y_prompt, y_sample, outs["a_k"], outs["a_v"], outs["r_f"], outs["r_b"], outs["c_k"], outs["c_v"])
```

```python
import functools

import jax
import jax.numpy as jnp
from jax import lax
from jax.experimental import pallas as pl
from jax.experimental.pallas import tpu as pltpu
from jax.experimental.pallas import tpu_sc as plsc

F32 = jnp.float32
BF16 = jnp.bfloat16
HIGHEST = lax.Precision.HIGHEST

D = 1024
BATCH = 32
SEQ = 256
DEC_BATCH = 4
DEC_SEQ = 4096
PAST = 256
GRID_W = 64
HD = 64
EPS = 1e-6
NEG = -1e30
ROPE_BASE = 10000.0
A_HEADS = 8
A_KV_HEADS = 2
A_Q = A_HEADS * HD
A_KV = A_KV_HEADS * HD
B_HEADS = 8
B_W = B_HEADS * HD
EVEN_IN = A_Q + 2 * A_KV + 4 * B_W
C_HEADS = 16
C_W = C_HEADS * HD
NA_KH = 8
NA_KW = 16
N_EXPERTS = 64
TOP_K = 8
N_GROUPS = 8
TOPK_GROUPS = 4
FF = 256
ROUTED_SCALE = 2.5
MOE_BLOCK = 1024
EXPERT_X_SLOTS = 3
RET_CHUNK = 256
RET_UNROLL = 4
RET_NORM_ROWS = 1024
A_WINDOW = 128

NP_TOK = BATCH * SEQ
NS_TOK = DEC_BATCH * DEC_SEQ
N_TOK = NP_TOK + NS_TOK
N_ASSIGN = N_TOK * TOP_K
N_MOE_BLOCKS = (N_ASSIGN + N_EXPERTS * (MOE_BLOCK - 1) + MOE_BLOCK - 1) // MOE_BLOCK
PLAN_LANES = 512
assert N_TOK % MOE_BLOCK == 0 and N_MOE_BLOCKS <= PLAN_LANES

LANES = 128
TM = 512
TM_OUT = 1024
NA_ROWS = 8
V7X_VMEM_LIMIT = 56 * 1024 * 1024

NT_DIMS = (((1,), (1,)), ((), ()))


def _cparams(n_axes, vmem=V7X_VMEM_LIMIT):
    return pltpu.CompilerParams(dimension_semantics=("arbitrary",) * n_axes, vmem_limit_bytes=vmem)


def _seg_of_block(i, rows):
    row0 = i * rows
    return jnp.where(row0 < NP_TOK, 0, 1 + (row0 - NP_TOK) // DEC_SEQ)


def _mod_spec(chunk, rows=TM, first_block=0):
    return pl.BlockSpec((None, 1, D), lambda i: (_seg_of_block(i + first_block, rows), 0, chunk))


def _pair_specs(width, rows=TM):
    npb = NP_TOK // rows
    nsb = NS_TOK // rows
    return [pl.BlockSpec((rows, width), lambda i: (jnp.minimum(i, npb - 1), 0)),
            pl.BlockSpec((rows, width), lambda i: (jnp.clip(i - npb, 0, nsb - 1), 0))]


def _pick_rows(p_ref, s_ref, rows=TM):
    return jnp.where(pl.program_id(0) < NP_TOK // rows, p_ref[...], s_ref[...])


def _silu(x):
    return x * jax.nn.sigmoid(x)


def _rms(x, g):
    return x * lax.rsqrt(jnp.mean(x * x, axis=-1, keepdims=True) + EPS) * g


def _lane_lo():
    return lax.broadcasted_iota(jnp.int32, (1, LANES), 1) < HD


def _ada_kernel(c_ref, w_ref, b_ref, o_ref):
    a = _silu(c_ref[...])
    o_ref[...] = jnp.dot(a, w_ref[...], preferred_element_type=F32, precision=HIGHEST) + b_ref[...]


def _ada(cc, w, b, layer):
    tn = 1536
    out = pl.pallas_call(
        _ada_kernel,
        grid=(6 * D // tn,),
        in_specs=[pl.BlockSpec((8, D), lambda j: (0, 0)),
                  pl.BlockSpec((None, D, tn), lambda j: (layer, 0, j)),
                  pl.BlockSpec((None, 1, tn), lambda j: (layer, 0, j))],
        out_specs=pl.BlockSpec((8, tn), lambda j: (0, j)),
        out_shape=jax.ShapeDtypeStruct((8, 6 * D), F32),
        compiler_params=_cparams(1),
        name="ada",
    )(cc, w, b.reshape(b.shape[0], 1, 6 * D))
    return out.reshape(8, 1, 6 * D)


def _inproj_kernel(xp_ref, xs_ref, g_ref, shift_ref, scale_ref, w_ref, cos_ref, sin_ref, o_ref, *, rope_cols):
    h = _rms(_pick_rows(xp_ref, xs_ref), g_ref[...]) * (1.0 + scale_ref[...]) + shift_ref[...]
    o = jnp.dot(h.astype(BF16), w_ref[...], preferred_element_type=F32)
    if rope_cols:
        cos = cos_ref[...]
        sin = sin_ref[...]
        lane = lax.broadcasted_iota(jnp.int32, (1, LANES), 1)
        first = (lane % 32) < 16
        for c in range(rope_cols // LANES):
            oc = o[:, c * LANES:(c + 1) * LANES]
            partner = jnp.where(first, pltpu.roll(oc, LANES - 16, 1), pltpu.roll(oc, 16, 1))
            o_ref[:, c * LANES:(c + 1) * LANES] = oc * cos + partner * sin
        o_ref[:, rope_cols:] = o[:, rope_cols:]
    else:
        o_ref[...] = o


def _rope_tables():
    half = HD // 2
    inv = ROPE_BASE ** (-jnp.arange(0, half, 2, dtype=F32) / half)
    t = jnp.arange(DEC_SEQ)
    ang_r = (t // GRID_W).astype(F32)[:, None] * inv[None]
    ang_c = (t % GRID_W).astype(F32)[:, None] * inv[None]

    def head(fn_r, fn_c, sign):
        return jnp.concatenate([sign[0] * fn_r, sign[1] * fn_r, sign[0] * fn_c, sign[1] * fn_c], axis=-1)

    cos = head(jnp.cos(ang_r), jnp.cos(ang_c), (1.0, 1.0))
    sin = head(jnp.sin(ang_r), jnp.sin(ang_c), (-1.0, 1.0))
    cos = jnp.concatenate([jnp.ones((TM, HD), F32), cos], axis=0)
    sin = jnp.concatenate([jnp.zeros((TM, HD), F32), sin], axis=0)
    return jnp.tile(cos, (1, 2)), jnp.tile(sin, (1, 2))


def _inproj(x, g, mod, w_bf16, rope, rope_cols):
    n_out = w_bf16.shape[1]
    npb = NP_TOK // TM
    spb = DEC_SEQ // TM

    def rope_map(i):
        return (jnp.where(i < npb, 0, 1 + (i - npb) % spb), 0)

    return pl.pallas_call(
        functools.partial(_inproj_kernel, rope_cols=rope_cols),
        grid=(N_TOK // TM,),
        in_specs=_pair_specs(D) + [
                  pl.BlockSpec((1, D), lambda i: (0, 0)),
                  _mod_spec(0), _mod_spec(1),
                  pl.BlockSpec((D, n_out), lambda i: (0, 0)),
                  pl.BlockSpec((TM, LANES), rope_map),
                  pl.BlockSpec((TM, LANES), rope_map)],
        out_specs=pl.BlockSpec((TM, n_out), lambda i: (i, 0)),
        out_shape=jax.ShapeDtypeStruct((N_TOK, n_out), F32),
        compiler_params=_cparams(1),
        name="inproj",
    )(x[0], x[1], g.reshape(1, D), mod, mod, w_bf16, rope[0], rope[1])


def _softmax_av(s_list, v_list, sink=None):
    mx = s_list[0].max(axis=-1, keepdims=True)
    for s in s_list[1:]:
        mx = jnp.maximum(mx, s.max(axis=-1, keepdims=True))
    if sink is not None:
        mx = jnp.maximum(mx, sink)
    den = jnp.exp(sink - mx) if sink is not None else 0.0
    acc = None
    for s, v in zip(s_list, v_list):
        p = jnp.exp(s - mx)
        den = den + p.sum(axis=-1, keepdims=True)
        pv = jnp.dot(p.astype(BF16), v, preferred_element_type=F32)
        acc = pv if acc is None else acc + pv
    return acc / den


def _dup_half(x, j, lo):
    xr = pltpu.roll(x, HD, 1)
    return jnp.where(lo, x, xr) if j == 0 else jnp.where(lo, xr, x)


def _stack_heads(q_ref, heads, lo, scale):
    parts = []
    for h in heads:
        qp = q_ref[:, (h // 2) * LANES:(h // 2 + 1) * LANES]
        keep = lo if h % 2 == 0 else jnp.logical_not(lo)
        parts.append(jnp.where(keep, qp, 0.0) * scale)
    return jnp.concatenate(parts, axis=0).astype(BF16)


def _sink_column(sink_ref, heads, rows):
    return jnp.concatenate([jnp.full((rows, 1), sink_ref[h], F32) for h in heads], axis=0)


def _ctx_gqa_kernel(sink_ref, q_ref, k_ref, v_ref, o_ref):
    lo = _lane_lo()
    k = k_ref[...]
    v = v_ref[...]
    group = A_HEADS // A_KV_HEADS
    scores = []
    for j in range(A_KV_HEADS):
        q = _stack_heads(q_ref, list(range(group * j, group * (j + 1))), lo, HD ** -0.5)
        scores.append(lax.dot_general(q, _dup_half(k, j, lo).astype(BF16), NT_DIMS, preferred_element_type=F32))
    s = jnp.concatenate(scores, axis=0)
    sink = _sink_column(sink_ref, list(range(A_HEADS)), SEQ)
    mx = jnp.maximum(s.max(axis=-1, keepdims=True), sink)
    e = jnp.exp(s - mx)
    den = jnp.exp(sink - mx) + e.sum(axis=-1, keepdims=True)
    e = e.astype(BF16)
    rows_per_group = group * SEQ
    for j in range(A_KV_HEADS):
        rows = slice(j * rows_per_group, (j + 1) * rows_per_group)
        o = jnp.dot(e[rows], _dup_half(v, j, lo).astype(BF16), preferred_element_type=F32) / den[rows]
        for t in range(group // 2):
            pair = (group * j) // 2 + t
            o_ref[:, pair * LANES:(pair + 1) * LANES] = jnp.where(
                lo, o[(2 * t) * SEQ:(2 * t + 1) * SEQ], o[(2 * t + 1) * SEQ:(2 * t + 2) * SEQ])


def _ctx_gqa(p, sink):
    return pl.pallas_call(
        _ctx_gqa_kernel,
        grid_spec=pltpu.PrefetchScalarGridSpec(
            num_scalar_prefetch=1,
            grid=(BATCH,),
            in_specs=[pl.BlockSpec((SEQ, A_Q), lambda b, s: (b, 0)),
                      pl.BlockSpec((SEQ, A_KV), lambda b, s: (b, A_Q // A_KV)),
                      pl.BlockSpec((SEQ, A_KV), lambda b, s: (b, A_Q // A_KV + 1))],
            out_specs=pl.BlockSpec((SEQ, A_Q), lambda b, s: (b, 0))),
        out_shape=jax.ShapeDtypeStruct((NP_TOK, A_Q), F32),
        compiler_params=_cparams(1),
        name="ctx_gqa",
    )(sink, p, p, p)


def _win_kernel(sink_ref, q_ref, kp_ref, kc_ref, kn_ref, vp_ref, vc_ref, vn_ref, ck_ref, cv_ref, o_ref):
    i = pl.program_id(1)
    lo = _lane_lo()
    k = jnp.concatenate([kp_ref[...], kc_ref[...], kn_ref[...]], axis=0)
    v = jnp.concatenate([vp_ref[...], vc_ref[...], vn_ref[...]], axis=0)
    ck = ck_ref[...]
    cv = cv_ref[...]
    group = A_HEADS // A_KV_HEADS
    n_keys = WIN_Q + 2 * A_WINDOW
    qpos = i * WIN_Q + lax.broadcasted_iota(jnp.int32, (WIN_Q, n_keys), 0)
    kpos = i * WIN_Q - A_WINDOW + lax.broadcasted_iota(jnp.int32, (WIN_Q, n_keys), 1)
    valid = (jnp.abs(kpos - qpos) <= A_WINDOW) & (kpos >= 0) & (kpos < DEC_SEQ)
    valid = jnp.concatenate([valid] * group, axis=0)
    s_loc, s_ctx, values = [], [], []
    for j in range(A_KV_HEADS):
        heads = list(range(group * j, group * (j + 1)))
        kd = _dup_half(k, j, lo).astype(BF16)
        ckd = _dup_half(ck, j, lo).astype(BF16)
        values.append((_dup_half(v, j, lo).astype(BF16), _dup_half(cv, j, lo).astype(BF16)))
        q = _stack_heads(q_ref, heads, lo, HD ** -0.5)
        s_loc.append(jnp.where(valid, lax.dot_general(q, kd, NT_DIMS, preferred_element_type=F32), NEG))
        s_ctx.append(lax.dot_general(q, ckd, NT_DIMS, preferred_element_type=F32))
    s_loc = jnp.concatenate(s_loc, axis=0)
    s_ctx = jnp.concatenate(s_ctx, axis=0)
    sink = _sink_column(sink_ref, list(range(A_HEADS)), WIN_Q)
    mx = jnp.maximum(jnp.maximum(s_loc.max(axis=-1, keepdims=True), s_ctx.max(axis=-1, keepdims=True)), sink)
    p_loc = jnp.exp(s_loc - mx)
    p_ctx = jnp.exp(s_ctx - mx)
    den = p_loc.sum(axis=-1, keepdims=True) + p_ctx.sum(axis=-1, keepdims=True) + jnp.exp(sink - mx)
    p_loc = p_loc.astype(BF16)
    p_ctx = p_ctx.astype(BF16)
    rows_per_group = group * WIN_Q
    for j, (vd, cvd) in enumerate(values):
        rows = slice(j * rows_per_group, (j + 1) * rows_per_group)
        o = (jnp.dot(p_loc[rows], vd, preferred_element_type=F32)
             + jnp.dot(p_ctx[rows], cvd, preferred_element_type=F32)) / den[rows]
        for t in range(group // 2):
            pair = (group * j) // 2 + t
            o_ref[:, pair * LANES:(pair + 1) * LANES] = jnp.where(
                lo, o[(2 * t) * WIN_Q:(2 * t + 1) * WIN_Q], o[(2 * t + 1) * WIN_Q:(2 * t + 2) * WIN_Q])


WIN_Q = 256


def _win_attention(p, cache_k, cache_v, sink):
    nblk = DEC_SEQ // WIN_Q
    side = WIN_Q // A_WINDOW
    nside = DEC_SEQ // A_WINDOW
    base = NP_TOK // WIN_Q
    side_base = NP_TOK // A_WINDOW
    kcol = A_Q // A_KV

    def main_spec(col):
        return pl.BlockSpec((WIN_Q, A_KV), lambda b, i, s: (base + b * nblk + i, col))

    def side_spec(col, off):
        return pl.BlockSpec((A_WINDOW, A_KV),
                            lambda b, i, s: (side_base + b * nside + jnp.clip(side * i + off, 0, nside - 1), col))

    ctx_spec = pl.BlockSpec((None, PAST, A_KV), lambda b, i, s: (b, 0, 0))
    return pl.pallas_call(
        _win_kernel,
        grid_spec=pltpu.PrefetchScalarGridSpec(
            num_scalar_prefetch=1,
            grid=(DEC_BATCH, nblk),
            in_specs=[pl.BlockSpec((WIN_Q, A_Q), lambda b, i, s: (base + b * nblk + i, 0)),
                      side_spec(kcol, -1), main_spec(kcol), side_spec(kcol, side),
                      side_spec(kcol + 1, -1), main_spec(kcol + 1), side_spec(kcol + 1, side),
                      ctx_spec, ctx_spec],
            out_specs=pl.BlockSpec((WIN_Q, A_Q), lambda b, i, s: (b * nblk + i, 0))),
        out_shape=jax.ShapeDtypeStruct((NS_TOK, A_Q), F32),
        compiler_params=_cparams(2),
        name="win_attn",
    )(sink, p, p, p, p, p, p, p, cache_k.reshape(DEC_BATCH, PAST, A_KV), cache_v.reshape(DEC_BATCH, PAST, A_KV))


def _ret_kernel(df_ref, db_ref, q_ref, k_ref, v_ref, g_ref, gn_ref, s0f_ref, s0b_ref,
                o_ref, sf_ref, sb_ref, of_scr, ob_scr, *, length):
    c_len = RET_CHUNK
    n = length // c_len
    lo = _lane_lo()
    hi = jnp.logical_not(lo)
    row = lax.broadcasted_iota(jnp.int32, (c_len, c_len), 0)
    col = lax.broadcasted_iota(jnp.int32, (c_len, c_len), 1)
    rowp = lax.broadcasted_iota(jnp.int32, (LANES, LANES), 0)
    colp = lax.broadcasted_iota(jnp.int32, (LANES, LANES), 1)
    blockdiag = (rowp < HD) == (colp < HD)
    idx = lax.broadcasted_iota(jnp.int32, (c_len, 1), 0).astype(F32)

    def direction(dec_ref, forward):
        lg = -jnp.exp(dec_ref[...])
        diff = (row - col) if forward else (col - row)
        keep = (diff >= 0) if forward else (diff > 0)
        dist = jnp.maximum(diff, 0).astype(F32)
        dm = jnp.concatenate([jnp.where(keep, jnp.exp(dist * lg[:, off:off + 1]), 0.0) for off in (0, HD)], axis=0)
        if forward:
            xi = jnp.exp((idx + 1.0) * lg)
            zeta = jnp.exp((c_len - 1.0 - idx) * lg)
        else:
            xi = jnp.exp((c_len - idx) * lg)
            zeta = jnp.exp(idx * lg)
        return dm, xi, zeta, jnp.exp(c_len * lg)

    def chunk(c, state, consts):
        dm, xi, zeta, gch = consts
        rows = pl.ds(pl.multiple_of(c * c_len, c_len), c_len)
        qc = q_ref[rows, :]
        kc = k_ref[rows, :] * HD ** -0.5
        vc = v_ref[rows, :].astype(BF16)
        kb = kc.astype(BF16)
        q2 = jnp.concatenate([jnp.where(lo, qc, 0.0), jnp.where(hi, qc, 0.0)], axis=0).astype(BF16)
        inner = lax.dot_general(q2, kb, NT_DIMS, preferred_element_type=F32) * dm
        kz_t = (kc * zeta).T
        res = jnp.dot(jnp.concatenate([inner, kz_t], axis=0).astype(BF16), vc, preferred_element_type=F32)
        cross = jnp.dot(qc.astype(BF16), state.astype(BF16), preferred_element_type=F32) * xi
        o = jnp.where(lo, res[:c_len], res[c_len:2 * c_len]) + cross
        state = gch * state + jnp.where(blockdiag, res[2 * c_len:], 0.0)
        return rows, o, state

    cf = direction(df_ref, True)
    cb = direction(db_ref, False)

    def scan_body(t, states):
        rows_f, o_f, state_f = chunk(t, states[0], cf)
        of_scr[rows_f, :] = o_f
        rows_b, o_b, state_b = chunk(n - 1 - t, states[1], cb)
        ob_scr[rows_b, :] = o_b
        return state_f, state_b

    state_f, state_b = lax.fori_loop(0, n, scan_body, (s0f_ref[...], s0b_ref[...]), unroll=min(n, RET_UNROLL))
    sf_ref[...] = state_f
    sb_ref[...] = state_b

    gn = gn_ref[...]
    norm_rows = min(RET_NORM_ROWS, length)
    n_norm = length // norm_rows

    def per_head(x):
        a = jnp.where(lo, x, 0.0).sum(axis=-1, keepdims=True)
        b = jnp.where(hi, x, 0.0).sum(axis=-1, keepdims=True)
        return jnp.where(lo, a, b) * (1.0 / HD)

    def norm_body(t, carry):
        rows = pl.ds(pl.multiple_of(t * norm_rows, norm_rows), norm_rows)
        o = of_scr[rows, :] + ob_scr[rows, :]
        d = o - per_head(o)
        y = d * lax.rsqrt(per_head(d * d) + EPS) * gn
        o_ref[rows, :] = _silu(g_ref[rows, :]) * y
        return carry

    lax.fori_loop(0, n_norm, norm_body, 0)


def _pair_lanes(v):
    return jnp.repeat(v.astype(F32), HD).reshape(B_HEADS // 2, 1, LANES)


def _blockdiag_states(s):
    b = s.shape[0]
    s = s.astype(F32).reshape(b, B_HEADS // 2, 2, HD, HD)
    z = jnp.zeros_like(s[:, :, 0])
    top = jnp.concatenate([s[:, :, 0], z], axis=-1)
    bot = jnp.concatenate([z, s[:, :, 1]], axis=-1)
    return jnp.concatenate([top, bot], axis=-2)


def _diag_states(sp):
    b = sp.shape[0]
    s = jnp.stack([sp[:, :, :HD, :HD], sp[:, :, HD:, HD:]], axis=2)
    return s.reshape(b, B_HEADS, HD, HD)


def _retention(p, row_base, batch, length, dec_f, dec_b, gn_g, s0f, s0b):
    npairs = B_HEADS // 2
    blk0 = row_base // length
    qcol = (A_Q + 2 * A_KV) // LANES

    def col_spec(off):
        return pl.BlockSpec((length, LANES), lambda b, h: (blk0 + b, qcol + off * npairs + h))

    lane_spec = pl.BlockSpec((None, 1, LANES), lambda b, h: (h, 0, 0))
    state_spec = pl.BlockSpec((None, None, LANES, LANES), lambda b, h: (b, h, 0, 0))
    state_shape = jax.ShapeDtypeStruct((batch, npairs, LANES, LANES), F32)
    return pl.pallas_call(
        functools.partial(_ret_kernel, length=length),
        grid=(batch, npairs),
        in_specs=[lane_spec, lane_spec, col_spec(0), col_spec(1), col_spec(2), col_spec(3), lane_spec,
                  state_spec, state_spec],
        out_specs=[pl.BlockSpec((length, LANES), lambda b, h: (b, h)), state_spec, state_spec],
        out_shape=[jax.ShapeDtypeStruct((batch * length, B_W), F32), state_shape, state_shape],
        scratch_shapes=[pltpu.VMEM((length, LANES), F32), pltpu.VMEM((length, LANES), F32)],
        compiler_params=_cparams(2),
        name="retention",
    )(_pair_lanes(dec_f), _pair_lanes(dec_b), p, p, p, p, gn_g.reshape(npairs, 1, LANES), s0f, s0b)


def _ctx_mha_kernel(q_ref, k_ref, v_ref, o_ref):
    lo = _lane_lo()
    for pair in range(C_HEADS // 2):
        cols = slice(pair * LANES, (pair + 1) * LANES)
        q = _stack_heads(q_ref, [2 * pair, 2 * pair + 1], lo, HD ** -0.5)
        s = lax.dot_general(q, k_ref[:, cols].astype(BF16), NT_DIMS, preferred_element_type=F32)
        o = _softmax_av([s], [v_ref[:, cols].astype(BF16)])
        o_ref[:, cols] = jnp.where(lo, o[:SEQ], o[SEQ:])


def _ctx_mha(p):
    return pl.pallas_call(
        _ctx_mha_kernel,
        grid=(BATCH,),
        in_specs=[pl.BlockSpec((SEQ, C_W), lambda b: (b, 0)),
                  pl.BlockSpec((SEQ, C_W), lambda b: (b, 1)),
                  pl.BlockSpec((SEQ, C_W), lambda b: (b, 2))],
        out_specs=pl.BlockSpec((SEQ, C_W), lambda b: (b, 0)),
        out_shape=jax.ShapeDtypeStruct((NP_TOK, C_W), F32),
        compiler_params=_cparams(1),
        name="ctx_mha",
    )(p, p, p)


NA_WIN_ROWS = 2 * NA_ROWS
NA_WIN = NA_WIN_ROWS * GRID_W
NA_QROWS = NA_ROWS * GRID_W
NA_PAD_ROWS = NA_KH // 2
NA_TABLE = 1536
NA_SPAN = 768


def _na_kernel(q_ref, kp_ref, km_ref, kn_ref, vp_ref, vm_ref, vn_ref, ck_ref, cv_ref, ue_ref, uo_ref, o_ref):
    r0 = pl.program_id(2) * NA_ROWS
    n_rows = DEC_SEQ // GRID_W
    lo = _lane_lo()
    k = jnp.concatenate([kp_ref[...], km_ref[...], kn_ref[...]], axis=0).astype(BF16)
    v = jnp.concatenate([vp_ref[...], vm_ref[...], vn_ref[...]], axis=0).astype(BF16)
    ck = ck_ref[...].astype(BF16)
    cv = cv_ref[...].astype(BF16)
    q = q_ref[...] * HD ** -0.5
    outs = []
    for half, keep in enumerate((lo, jnp.logical_not(lo))):
        qh = jnp.where(keep, q, 0.0).astype(BF16)
        s_ctx = lax.dot_general(qh, ck, NT_DIMS, preferred_element_type=F32)
        o_parts = []
        for part in range(2):
            key0 = part * (NA_WIN - NA_SPAN)
            klane = key0 + lax.broadcasted_iota(jnp.int32, (1, NA_SPAN), 1)
            qrows = slice(part * NA_QROWS // 2, (part + 1) * NA_QROWS // 2)
            s = lax.dot_general(qh[qrows], k[key0:key0 + NA_SPAN], NT_DIMS, preferred_element_type=F32)
            p_loc, p_ctx, den = [], [], []
            for t in range(NA_ROWS // 2):
                rq = part * NA_ROWS // 2 + t
                start = NA_KH - 1 - rq
                if start % 2 == 0:
                    u = ue_ref[half, :, start * GRID_W + key0:start * GRID_W + key0 + NA_SPAN]
                else:
                    u = uo_ref[half, :, (start - 1) * GRID_W + key0:(start - 1) * GRID_W + key0 + NA_SPAN]
                r = r0 + rq
                first = jnp.clip(r - NA_KH // 2, 0, n_rows - NA_KH)
                lane0 = (first - r0 + NA_PAD_ROWS) * GRID_W
                in_rows = (klane >= lane0) & (klane < lane0 + NA_KH * GRID_W)
                sl = jnp.where(in_rows, s[t * GRID_W:(t + 1) * GRID_W] + u, NEG)
                sc = s_ctx[rq * GRID_W:(rq + 1) * GRID_W]
                mx = jnp.maximum(sl.max(axis=-1, keepdims=True), sc.max(axis=-1, keepdims=True))
                el = jnp.exp(sl - mx)
                ec = jnp.exp(sc - mx)
                den.append(el.sum(axis=-1, keepdims=True) + ec.sum(axis=-1, keepdims=True))
                p_loc.append(el.astype(BF16))
                p_ctx.append(ec.astype(BF16))
            acc = (jnp.dot(jnp.concatenate(p_loc, axis=0), v[key0:key0 + NA_SPAN], preferred_element_type=F32)
                   + jnp.dot(jnp.concatenate(p_ctx, axis=0), cv, preferred_element_type=F32))
            o_parts.append(acc / jnp.concatenate(den, axis=0))
        outs.append(jnp.concatenate(o_parts, axis=0))
    o_ref[...] = jnp.where(lo, outs[0], outs[1])


def _na_bias_tables(rpb):
    cq = jnp.arange(GRID_W)
    ck = jnp.arange(GRID_W)
    dc = jnp.clip(ck[None] - cq[:, None], -(NA_KW - 1), NA_KW - 1) + NA_KW - 1
    cs = jnp.clip(cq - NA_KW // 2, 0, GRID_W - NA_KW)
    col_ok = (ck[None] >= cs[:, None]) & (ck[None] < cs[:, None] + NA_KW)
    t = rpb.astype(F32)[:, :, dc]
    t = jnp.where(col_ok[None, None], t, NEG).transpose(0, 2, 1, 3)
    n_dr = 2 * NA_KH - 1
    blocks = NA_TABLE // GRID_W
    t = jnp.pad(t, ((0, 0), (0, 0), (NA_PAD_ROWS, blocks - n_dr - NA_PAD_ROWS), (0, 0)), constant_values=NEG)
    ue = t.reshape(C_HEADS, GRID_W, NA_TABLE)
    uo = jnp.concatenate([ue[..., GRID_W:], jnp.full((C_HEADS, GRID_W, GRID_W), NEG, F32)], axis=-1)
    return ue, uo


def _na_attention(p, cache_k, cache_v, rpb):
    npairs = C_HEADS // 2
    nrb = DEC_SEQ // NA_QROWS
    half = NA_QROWS // 2
    qbase = NP_TOK // NA_QROWS
    hbase = NP_TOK // half
    kcol = C_W // LANES
    ue, uo = _na_bias_tables(rpb)

    def main_spec(col0):
        return pl.BlockSpec((NA_QROWS, LANES), lambda b, h, r: (qbase + b * nrb + r, col0 + h))

    def side_spec(col0, off):
        return pl.BlockSpec((half, LANES),
                            lambda b, h, r: (hbase + b * 2 * nrb + jnp.clip(2 * r + off, 0, 2 * nrb - 1), col0 + h))

    ctx_spec = pl.BlockSpec((None, PAST, LANES), lambda b, h, r: (b, 0, h))
    tab_spec = pl.BlockSpec((2, GRID_W, NA_TABLE), lambda b, h, r: (h, 0, 0))
    return pl.pallas_call(
        _na_kernel,
        grid=(DEC_BATCH, npairs, nrb),
        in_specs=[main_spec(0),
                  side_spec(kcol, -1), main_spec(kcol), side_spec(kcol, 2),
                  side_spec(2 * kcol, -1), main_spec(2 * kcol), side_spec(2 * kcol, 2),
                  ctx_spec, ctx_spec, tab_spec, tab_spec],
        out_specs=pl.BlockSpec((NA_QROWS, LANES), lambda b, h, r: (b * nrb + r, h)),
        out_shape=jax.ShapeDtypeStruct((NS_TOK, C_W), F32),
        compiler_params=_cparams(3),
        name="na_attn",
    )(p, p, p, p, p, p, p, cache_k.reshape(DEC_BATCH, PAST, C_W), cache_v.reshape(DEC_BATCH, PAST, C_W), ue, uo)


def _route(biased, scores):
    t = biased.shape[1]
    per_group = N_EXPERTS // N_GROUPS
    i8 = lax.broadcasted_iota(jnp.int32, (per_group, t), 0)
    g_rows = []
    for g in range(N_GROUPS):
        bg = biased[g * per_group:(g + 1) * per_group]
        m1 = bg.max(axis=0, keepdims=True)
        first = jnp.where(bg == m1, i8, per_group).min(axis=0, keepdims=True)
        m2 = jnp.where(i8 == first, -jnp.inf, bg).max(axis=0, keepdims=True)
        g_rows.append(m1 + m2)
    g_top = jnp.concatenate(g_rows, axis=0)
    gi = lax.broadcasted_iota(jnp.int32, g_top.shape, 0)
    g_sel = jnp.zeros(g_top.shape, jnp.int32)
    cur = g_top
    for _ in range(TOPK_GROUPS):
        m = cur.max(axis=0, keepdims=True)
        hit = gi == jnp.where(cur == m, gi, N_GROUPS).min(axis=0, keepdims=True)
        g_sel = jnp.where(hit, 1, g_sel)
        cur = jnp.where(hit, -jnp.inf, cur)
    e_sel = jnp.concatenate([jnp.broadcast_to(g_sel[g:g + 1], (per_group, t)) for g in range(N_GROUPS)], axis=0)
    cur = jnp.where(e_sel > 0, biased, NEG)
    ei = lax.broadcasted_iota(jnp.int32, cur.shape, 0)
    ids, gates, hits = [], [], []
    for _ in range(TOP_K):
        m = cur.max(axis=0, keepdims=True)
        f = jnp.where(cur == m, ei, N_EXPERTS).min(axis=0, keepdims=True)
        hit = ei == f
        ids.append(f)
        hits.append(hit)
        gates.append(jnp.where(hit, scores, 0.0).sum(axis=0, keepdims=True))
        cur = jnp.where(hit, -jnp.inf, cur)
    gate = jnp.concatenate(gates, axis=0)
    gate = gate / gate.sum(axis=0, keepdims=True) * ROUTED_SCALE
    return jnp.concatenate(ids, axis=0), gate, hits


def _pack_bf16_pairs(h):
    bits = lax.bitcast_convert_type(h.astype(BF16).astype(F32), jnp.uint32)
    return bits[:, :D // 2] | (bits[:, D // 2:] >> 16)


def _unpack_bf16_pairs(xp):
    hi = lax.bitcast_convert_type(xp & jnp.uint32(0xFFFF0000), F32).astype(BF16)
    lo = lax.bitcast_convert_type(xp << 16, F32).astype(BF16)
    return hi, lo


def _dot_halves(hi, lo, w_ref):
    return (jnp.dot(hi, w_ref[:D // 2, :], preferred_element_type=F32)
            + jnp.dot(lo, w_ref[D // 2:, :], preferred_element_type=F32))


def _outproj_kernel(*refs, n_parts):
    xp_ref, xs_ref = refs[:2]
    part_refs = refs[2:2 + 3 * n_parts]
    gate_ref, shift_ref, scale_ref, g2_ref, rw_ref, rb_ref = refs[2 + 3 * n_parts:8 + 3 * n_parts]
    xo_ref, h_ref, dest_ref, wgt_ref, plan_ref, cnt_ref = refs[8 + 3 * n_parts:]
    step = pl.program_id(0)

    @pl.when(step == 0)
    def _():
        cnt_ref[...] = jnp.zeros_like(cnt_ref)
        plan_ref[...] = jnp.zeros_like(plan_ref)

    y = None
    for t in range(n_parts):
        ap_ref, as_ref, w_ref = part_refs[3 * t:3 * t + 3]
        d = jnp.dot(_pick_rows(ap_ref, as_ref, TM_OUT).astype(BF16), w_ref[...], preferred_element_type=F32)
        y = d if y is None else y + d
    x = _pick_rows(xp_ref, xs_ref, TM_OUT) + gate_ref[...] * y
    xo_ref[...] = x
    h = _rms(x, g2_ref[...]) * (1.0 + scale_ref[...]) + shift_ref[...]
    h_ref[...] = _pack_bf16_pairs(h)
    h_hi = h.astype(BF16)
    h_lo = (h - h_hi.astype(F32)).astype(BF16)
    rw = rw_ref[...]
    rw_hi = rw.astype(BF16)
    rw_lo = (rw - rw_hi.astype(F32)).astype(BF16)
    logits = (lax.dot_general(rw_hi, h_hi, NT_DIMS, preferred_element_type=F32)
              + lax.dot_general(rw_hi, h_lo, NT_DIMS, preferred_element_type=F32)
              + lax.dot_general(rw_lo, h_hi, NT_DIMS, preferred_element_type=F32))
    scores = jax.nn.sigmoid(logits)
    _, gate, hits = _route(scores + rb_ref[...], scores)
    wgt_ref[...] = gate
    chosen = hits[0]
    for hit in hits[1:]:
        chosen = chosen | hit
    m = jnp.where(chosen, 1.0, 0.0)
    before = (lax.broadcasted_iota(jnp.int32, (TM_OUT, TM_OUT), 0)
              < lax.broadcasted_iota(jnp.int32, (TM_OUT, TM_OUT), 1))
    prefix = jnp.dot(m.astype(BF16), jnp.where(before, 1.0, 0.0).astype(BF16), preferred_element_type=F32)
    e_base = (lax.broadcasted_iota(jnp.int32, (N_EXPERTS, 1), 0) * N_TOK).astype(F32)
    row_all = prefix + (cnt_ref[...] + e_base)
    dest_ref[...] = jnp.concatenate(
        [jnp.where(hit, row_all, 0.0).sum(axis=0, keepdims=True) for hit in hits], axis=0).astype(jnp.int32)
    cnt_ref[...] += m.sum(axis=1, keepdims=True)

    @pl.when(step == pl.num_programs(0) - 1)
    def _():
        _block_plan(cnt_ref[...], plan_ref)


def _block_plan(counts, plan_ref):
    cap_blocks = N_TOK // MOE_BLOCK
    nblk = ((counts.astype(jnp.int32) + (MOE_BLOCK - 1)) // MOE_BLOCK).astype(F32)
    lower = (lax.broadcasted_iota(jnp.int32, (N_EXPERTS, N_EXPERTS), 0)
             >= lax.broadcasted_iota(jnp.int32, (N_EXPERTS, N_EXPERTS), 1))
    cum = jnp.dot(jnp.where(lower, 1.0, 0.0).astype(BF16), jnp.broadcast_to(nblk, (N_EXPERTS, LANES)).astype(BF16),
                  preferred_element_type=F32)[:, :1]
    n_used = cum[N_EXPERTS - 1:, :]
    slot = jnp.minimum(lax.broadcasted_iota(jnp.int32, (1, PLAN_LANES), 1).astype(F32), n_used - 1.0)
    done = cum <= slot
    expert = jnp.where(done, 1.0, 0.0).sum(axis=0, keepdims=True)
    blocks_before = jnp.where(done, nblk, 0.0).sum(axis=0, keepdims=True)
    plan_ref[0:1, :] = (expert * cap_blocks + (slot - blocks_before)).astype(jnp.int32)
    plan_ref[1:2, :] = expert.astype(jnp.int32)
    plan_ref[2:3, :] = jnp.broadcast_to(n_used, (1, PLAN_LANES)).astype(jnp.int32)
    run_end = jnp.where(cum > slot, cum, jnp.inf).min(axis=0, keepdims=True)
    plan_ref[3:4, :] = jnp.minimum(jnp.where(cum <= run_end, 1.0, 0.0).sum(axis=0, keepdims=True),
                                   N_EXPERTS - 1.0).astype(jnp.int32)
    plan_ref[4:5, :] = jnp.where(run_end < n_used, 1, 0).astype(jnp.int32)


def _outproj(x, parts, mod, g2, router_w, router_b):
    in_specs = _pair_specs(D, TM_OUT)
    args = [x[0], x[1]]
    for ap, a_s, w in parts:
        width = ap.shape[1]
        in_specs += _pair_specs(width, TM_OUT) + [pl.BlockSpec((width, D), lambda i: (0, 0))]
        args += [ap, a_s, w]
    in_specs += [_mod_spec(2, TM_OUT), _mod_spec(3, TM_OUT), _mod_spec(4, TM_OUT),
                 pl.BlockSpec((1, D), lambda i: (0, 0)),
                 pl.BlockSpec((N_EXPERTS, D), lambda i: (0, 0)),
                 pl.BlockSpec((N_EXPERTS, 1), lambda i: (0, 0))]
    args += [mod, mod, mod, g2.reshape(1, D), router_w.T, router_b.reshape(N_EXPERTS, 1)]
    return pl.pallas_call(
        functools.partial(_outproj_kernel, n_parts=len(parts)),
        grid=(N_TOK // TM_OUT,),
        in_specs=in_specs,
        out_specs=[pl.BlockSpec((TM_OUT, D), lambda i: (i, 0)),
                   pl.BlockSpec((TM_OUT, D // 2), lambda i: (i, 0)),
                   pl.BlockSpec((TOP_K, TM_OUT), lambda i: (0, i)),
                   pl.BlockSpec((TOP_K, TM_OUT), lambda i: (0, i)),
                   pl.BlockSpec((8, PLAN_LANES), lambda i: (0, 0))],
        out_shape=[jax.ShapeDtypeStruct((N_TOK, D), F32),
                   jax.ShapeDtypeStruct((N_TOK, D // 2), jnp.uint32),
                   jax.ShapeDtypeStruct((TOP_K, N_TOK), jnp.int32),
                   jax.ShapeDtypeStruct((TOP_K, N_TOK), F32),
                   jax.ShapeDtypeStruct((8, PLAN_LANES), jnp.int32)],
        scratch_shapes=[pltpu.VMEM((N_EXPERTS, 1), F32)],
        compiler_params=_cparams(1),
        name="outproj_router",
    )(*args)


def _experts_kernel(br_ref, be_ref, nu_ref, ne_ref, hn_ref, x_hbm, w1_hbm, w3_hbm, w2_hbm, o_ref,
                    w1b, w3b, w2b, w1f, w3f, w2f, xbuf, xsem, wsem, *, layer):
    i = pl.program_id(0)
    n_steps = pl.num_programs(0)
    e = be_ref[i]
    prev = be_ref[jnp.maximum(i - 1, 0)]

    def x_copy(step):
        slot = step % EXPERT_X_SLOTS
        rows = pl.ds(pl.multiple_of(br_ref[step] * MOE_BLOCK, MOE_BLOCK), MOE_BLOCK)
        return pltpu.make_async_copy(x_hbm.at[rows], xbuf.at[slot], xsem.at[slot])

    @pl.when(i == 0)
    def _():
        x_copy(0).start()
        x_copy(1).start()

    @pl.when(i + 2 < n_steps)
    def _():
        x_copy(i + 2).start()

    x_copy(i).wait()

    def w_copies(expert):
        return [pltpu.make_async_copy(src.at[layer, expert], dst, wsem.at[k])
                for k, (src, dst) in enumerate(((w1_hbm, w1f), (w3_hbm, w3f), (w2_hbm, w2f)))]

    @pl.when(i == 0)
    def _():
        for c in w_copies(e):
            c.start()

    @pl.when((i == 0) | (e != prev))
    def _():
        for c in w_copies(e):
            c.wait()
        w1b[...] = w1f[...].astype(BF16)
        w3b[...] = w3f[...].astype(BF16)
        w2b[...] = w2f[...].astype(BF16)

        @pl.when(hn_ref[i] > 0)
        def _():
            for c in w_copies(ne_ref[i]):
                c.start()

    @pl.when(i < nu_ref[0])
    def _():
        hi, lo = _unpack_bf16_pairs(xbuf[i % EXPERT_X_SLOTS])
        a = _dot_halves(hi, lo, w1b)
        b = _dot_halves(hi, lo, w3b)
        h = (_silu(a) * b).astype(BF16)
        o_ref[...] = _pack_bf16_pairs(jnp.dot(h, w2b[...], preferred_element_type=F32))


def _experts(plan, x_rows, w1, w3, w2, layer):
    any_spec = pl.BlockSpec(memory_space=pl.ANY)
    return pl.pallas_call(
        functools.partial(_experts_kernel, layer=layer),
        grid_spec=pltpu.PrefetchScalarGridSpec(
            num_scalar_prefetch=5,
            grid=(N_MOE_BLOCKS,),
            in_specs=[any_spec, any_spec, any_spec, any_spec],
            out_specs=pl.BlockSpec((MOE_BLOCK, D // 2), lambda i, br, be, nu, ne, hn: (br[i], 0)),
            scratch_shapes=[pltpu.VMEM((D, FF), BF16), pltpu.VMEM((D, FF), BF16), pltpu.VMEM((FF, D), BF16),
                            pltpu.VMEM((D, FF), F32), pltpu.VMEM((D, FF), F32), pltpu.VMEM((FF, D), F32),
                            pltpu.VMEM((EXPERT_X_SLOTS, MOE_BLOCK, D // 2), jnp.uint32),
                            pltpu.SemaphoreType.DMA((EXPERT_X_SLOTS,)),
                            pltpu.SemaphoreType.DMA((3,))]),
        out_shape=jax.ShapeDtypeStruct(x_rows.shape, jnp.uint32),
        compiler_params=_cparams(1),
        name="experts",
    )(plan[0], plan[1], plan[2, :1], plan[3], plan[4], x_rows, w1, w3, w2)


SC_CORES = 2
SC_SUBCORES = 16
SC_WORKERS = SC_CORES * SC_SUBCORES
SC_CHUNK_BYTES = 64 * 1024
SC_SLOTS = 4


def _sc_scatter(rows, dest, n_out):
    n_rows, width = rows.shape
    picks = dest.shape[0]
    chunk = SC_CHUNK_BYTES // (4 * width)
    per_worker = n_rows // SC_WORKERS
    n_chunks = per_worker // chunk
    assert per_worker * SC_WORKERS == n_rows and n_chunks * chunk == per_worker and n_chunks % 2 == 0
    mesh = plsc.VectorSubcoreMesh(core_axis_name="c", subcore_axis_name="s")

    @functools.partial(
        pl.kernel, mesh=mesh,
        out_type=jax.ShapeDtypeStruct((n_out, width), rows.dtype),
        scratch_types=[pltpu.VMEM((picks, n_chunks, chunk), jnp.int32),
                       pltpu.VMEM((2, chunk, width), rows.dtype),
                       pltpu.SemaphoreType.DMA((2,)),
                       pltpu.SemaphoreType.DMA((2,))])
    def scatter(r_hbm, d_hbm, o_hbm, idx_v, rows_v, lsem, ssem):
        worker = lax.axis_index("s") * SC_CORES + lax.axis_index("c")
        base = worker * per_worker
        for k in range(picks):
            pltpu.sync_copy(d_hbm.at[k, worker], idx_v.at[k])

        def load_copy(c, b):
            src = pl.ds(pl.multiple_of(base + c * chunk, chunk), chunk)
            return pltpu.make_async_copy(r_hbm.at[src], rows_v.at[b], lsem.at[b])

        def store_copy(c, b, k):
            return pltpu.make_async_copy(rows_v.at[b], o_hbm.at[idx_v.at[k, c]], ssem.at[b])

        load_copy(0, 0).start()

        @pl.loop(0, n_chunks, step=2)
        def _(c0):
            for b in range(2):
                c = c0 + b
                load_copy(c, b).wait()
                for k in range(picks):
                    store_copy(c, b, k).start()

                @pl.when(c > 0)
                def _():
                    for k in range(picks):
                        store_copy(c - 1, 1 - b, k).wait()

                @pl.when(c + 1 < n_chunks)
                def _():
                    load_copy(c + 1, 1 - b).start()

        for k in range(picks):
            store_copy(n_chunks - 1, 1, k).wait()

    return scatter(rows, dest.reshape(picks, SC_WORKERS, n_chunks, chunk))


def _sc_gather(table, idx):
    n_idx = idx.shape[0]
    width = table.shape[1]
    chunk = SC_CHUNK_BYTES // (4 * width)
    per_worker = n_idx // SC_WORKERS
    n_chunks = per_worker // chunk
    ahead = SC_SLOTS - 1
    assert per_worker * SC_WORKERS == n_idx and n_chunks * chunk == per_worker and n_chunks % SC_SLOTS == 0
    mesh = plsc.VectorSubcoreMesh(core_axis_name="c", subcore_axis_name="s")

    @functools.partial(
        pl.kernel, mesh=mesh,
        out_type=jax.ShapeDtypeStruct((n_idx, width), table.dtype),
        scratch_types=[pltpu.VMEM((per_worker,), jnp.int32),
                       pltpu.VMEM((SC_SLOTS, chunk, width), table.dtype),
                       pltpu.SemaphoreType.DMA((SC_SLOTS,)),
                       pltpu.SemaphoreType.DMA((SC_SLOTS,))])
    def gather(t_hbm, i_hbm, o_hbm, idx_v, rows_v, gsem, wsem):
        worker = lax.axis_index("s") * SC_CORES + lax.axis_index("c")
        base = worker * per_worker
        pltpu.sync_copy(i_hbm.at[pl.ds(pl.multiple_of(base, chunk), per_worker)], idx_v)

        def gather_copy(c, b):
            ids = idx_v.at[pl.ds(pl.multiple_of(c * chunk, chunk), chunk)]
            return pltpu.make_async_copy(t_hbm.at[ids], rows_v.at[b], gsem.at[b])

        def write_copy(c, b):
            rows = pl.ds(pl.multiple_of(base + c * chunk, chunk), chunk)
            return pltpu.make_async_copy(rows_v.at[b], o_hbm.at[rows], wsem.at[b])

        for c in range(ahead):
            gather_copy(c, c).start()

        @pl.loop(0, n_chunks, step=SC_SLOTS)
        def _(c0):
            for b in range(SC_SLOTS):
                c = c0 + b
                refill = (b + ahead) % SC_SLOTS
                gather_copy(c, b).wait()
                write_copy(c, b).start()

                @pl.when(c > 0)
                def _():
                    write_copy(c - 1, refill).wait()

                @pl.when(c + ahead < n_chunks)
                def _():
                    gather_copy(c + ahead, refill).start()

        write_copy(n_chunks - 1, (n_chunks - 1) % SC_SLOTS).wait()

    return gather(table, idx)


TC = 512


def _combine_kernel(x_ref, h_ref, y_ref, wgt_ref, gate_ref, w1_ref, w3_ref, w2_ref, fg_ref, o_ref, *, final):
    hi, lo = _unpack_bf16_pairs(h_ref[...])
    a = _dot_halves(hi, lo, w1_ref)
    b = _dot_halves(hi, lo, w3_ref)
    ffn = jnp.dot((_silu(a) * b).astype(BF16), w2_ref[...], preferred_element_type=F32)
    wgt = wgt_ref[...]
    r_hi = None
    r_lo = None
    for k in range(TOP_K):
        yk = y_ref[k]
        w = wgt[:, k:k + 1]
        t_hi = lax.bitcast_convert_type(yk & jnp.uint32(0xFFFF0000), F32) * w
        t_lo = lax.bitcast_convert_type(yk << 16, F32) * w
        r_hi = t_hi if r_hi is None else r_hi + t_hi
        r_lo = t_lo if r_lo is None else r_lo + t_lo
    x = x_ref[...] + gate_ref[...] * (ffn + jnp.concatenate([r_hi, r_lo], axis=1))
    o_ref[...] = _rms(x, fg_ref[...]) if final else x


def _combine(x, h, y_rows, wgt, mod, sw1, sw3, sw2, final_g, final):
    weights = (sw1.astype(BF16), sw3.astype(BF16), sw2.astype(BF16), final_g.reshape(1, D))

    def rows_from(first_row, n_rows):
        b0 = first_row // TC
        return pl.pallas_call(
            functools.partial(_combine_kernel, final=final),
            grid=(n_rows // TC,),
            in_specs=[pl.BlockSpec((TC, D), lambda i: (i + b0, 0)),
                      pl.BlockSpec((TC, D // 2), lambda i: (i + b0, 0)),
                      pl.BlockSpec((TOP_K, TC, D // 2), lambda i: (0, i + b0, 0)),
                      pl.BlockSpec((TC, TOP_K), lambda i: (i + b0, 0)),
                      _mod_spec(5, TC, b0),
                      pl.BlockSpec((D, FF), lambda i: (0, 0)),
                      pl.BlockSpec((D, FF), lambda i: (0, 0)),
                      pl.BlockSpec((FF, D), lambda i: (0, 0)),
                      pl.BlockSpec((1, D), lambda i: (0, 0))],
            out_specs=pl.BlockSpec((TC, D), lambda i: (i, 0)),
            out_shape=jax.ShapeDtypeStruct((n_rows, D), F32),
            compiler_params=_cparams(1),
            name="combine",
        )(x, h, y_rows, wgt, mod, *weights)

    return rows_from(0, NP_TOK), rows_from(NP_TOK, NS_TOK)


def kernel(x_prompt, x_sample, cache_a_k, cache_a_v, state_ret_fwd, state_ret_bwd, cache_c_k, cache_c_v,
           c, c_ctx, norm1_g, norm2_g, ada_w, ada_b, even_w_in, even_w_out, sink_a, ret_decay_fwd,
           ret_decay_bwd, ret_gn_g, odd_w_in, odd_w_out, na_rpb, router_w, router_b, exp_w1, exp_w3,
           exp_w2, sh_w1, sh_w3, sh_w2, final_g):
    x = (x_prompt.reshape(NP_TOK, D), x_sample.reshape(NS_TOK, D))
    cc = jnp.concatenate([c_ctx[None], c, jnp.zeros((8 - 1 - DEC_BATCH, D), F32)], axis=0)
    rope = _rope_tables()
    outs = {}
    for l in range(2):
        mod = _ada(cc, ada_w, ada_b, l)
        if l == 0:
            p = _inproj(x, norm1_g[l], mod, even_w_in[0].astype(BF16), rope, A_Q + A_KV)
            oa_p = _ctx_gqa(p, sink_a[0])
            oa_s = _win_attention(p, cache_a_k[:, 0], cache_a_v[:, 0], sink_a[0])
            zero = jnp.zeros((BATCH, B_HEADS // 2, LANES, LANES), F32)
            ob_p, sf, sb = _retention(p, 0, BATCH, SEQ, ret_decay_fwd[0], ret_decay_bwd[0], ret_gn_g[0], zero, zero)
            ob_s, _, _ = _retention(p, NP_TOK, DEC_BATCH, DEC_SEQ, ret_decay_fwd[0], ret_decay_bwd[0], ret_gn_g[0],
                                    _blockdiag_states(state_ret_fwd[:, 0]), _blockdiag_states(state_ret_bwd[:, 0]))
            w_out = even_w_out[0].astype(BF16)
            parts = [(oa_p, oa_s, w_out[:A_Q]), (ob_p, ob_s, w_out[A_Q:])]
            outs["a_k"] = p[:NP_TOK, A_Q:A_Q + A_KV].reshape(BATCH, 1, SEQ, A_KV_HEADS, HD)
            outs["a_v"] = p[:NP_TOK, A_Q + A_KV:A_Q + 2 * A_KV].reshape(BATCH, 1, SEQ, A_KV_HEADS, HD)
            outs["r_f"] = _diag_states(sf).reshape(BATCH, 1, B_HEADS, HD, HD)
            outs["r_b"] = _diag_states(sb).reshape(BATCH, 1, B_HEADS, HD, HD)
        else:
            p = _inproj(x, norm1_g[l], mod, odd_w_in[0].astype(BF16), rope, 0)
            o_p = _ctx_mha(p)
            o_s = _na_attention(p, cache_c_k[:, 0], cache_c_v[:, 0], na_rpb[0])
            parts = [(o_p, o_s, odd_w_out[0].astype(BF16))]
            outs["c_k"] = p[:NP_TOK, C_W:2 * C_W].reshape(BATCH, 1, SEQ, C_HEADS, HD)
            outs["c_v"] = p[:NP_TOK, 2 * C_W:3 * C_W].reshape(BATCH, 1, SEQ, C_HEADS, HD)
        x_mid, h, dest, gate_t, plan = _outproj(x, parts, mod, norm2_g[l], router_w[l], router_b[l])
        y = _experts(plan, _sc_scatter(h, dest, N_EXPERTS * N_TOK), exp_w1, exp_w3, exp_w2, l)
        y_rows = _sc_gather(y, dest.reshape(N_ASSIGN)).reshape(TOP_K, N_TOK, D // 2)
        x = _combine(x_mid, h, y_rows, gate_t.T, mod, sh_w1[l], sh_w3[l], sh_w2[l], final_g, final=(l == 1))
    y_prompt = x[0].reshape(BATCH, SEQ, D)
    y_sample = x[1].reshape(DEC_BATCH, DEC_SEQ, D)
    return (y_prompt, y_sample, outs["a_k"], outs["a_v"], outs["r_f"], outs["r_b"], outs["c_k"], outs["c_v"])
```

```python
import functools

import jax
import jax.numpy as jnp
from jax import lax
from jax.experimental import pallas as pl
from jax.experimental.pallas import tpu as pltpu
from jax.experimental.pallas import tpu_sc as plsc

F32 = jnp.float32
BF16 = jnp.bfloat16
HIGHEST = lax.Precision.HIGHEST

D = 1024
BATCH = 32
SEQ = 256
DEC_BATCH = 4
DEC_SEQ = 4096
PAST = 256
GRID_W = 64
HD = 64
EPS = 1e-6
NEG = -1e30
ROPE_BASE = 10000.0
A_HEADS = 8
A_KV_HEADS = 2
A_Q = A_HEADS * HD
A_KV = A_KV_HEADS * HD
B_HEADS = 8
B_W = B_HEADS * HD
EVEN_IN = A_Q + 2 * A_KV + 4 * B_W
C_HEADS = 16
C_W = C_HEADS * HD
NA_KH = 8
NA_KW = 16
N_EXPERTS = 64
TOP_K = 8
N_GROUPS = 8
TOPK_GROUPS = 4
FF = 256
ROUTED_SCALE = 2.5
MOE_BLOCK = 1024
EXPERT_X_SLOTS = 3
RET_CHUNK = 256
RET_UNROLL = 4
RET_NORM_ROWS = 1024
A_WINDOW = 128

NP_TOK = BATCH * SEQ
NS_TOK = DEC_BATCH * DEC_SEQ
N_TOK = NP_TOK + NS_TOK
N_ASSIGN = N_TOK * TOP_K
N_MOE_BLOCKS = (N_ASSIGN + N_EXPERTS * (MOE_BLOCK - 1) + MOE_BLOCK - 1) // MOE_BLOCK
PLAN_LANES = 512
assert N_TOK % MOE_BLOCK == 0 and N_MOE_BLOCKS <= PLAN_LANES

LANES = 128
TM = 512
TM_OUT = 1024
NA_ROWS = 8
V7X_VMEM_LIMIT = 56 * 1024 * 1024

NT_DIMS = (((1,), (1,)), ((), ()))


def _cparams(n_axes, vmem=V7X_VMEM_LIMIT):
    return pltpu.CompilerParams(dimension_semantics=("arbitrary",) * n_axes, vmem_limit_bytes=vmem)


def _seg_of_block(i, rows):
    row0 = i * rows
    return jnp.where(row0 < NP_TOK, 0, 1 + (row0 - NP_TOK) // DEC_SEQ)


def _mod_spec(chunk, rows=TM, first_block=0):
    return pl.BlockSpec((None, 1, D), lambda i: (_seg_of_block(i + first_block, rows), 0, chunk))


def _pair_specs(width, rows=TM):
    npb = NP_TOK // rows
    nsb = NS_TOK // rows
    return [pl.BlockSpec((rows, width), lambda i: (jnp.minimum(i, npb - 1), 0)),
            pl.BlockSpec((rows, width), lambda i: (jnp.clip(i - npb, 0, nsb - 1), 0))]


def _pick_rows(p_ref, s_ref, rows=TM):
    return jnp.where(pl.program_id(0) < NP_TOK // rows, p_ref[...], s_ref[...])


def _silu(x):
    return x * jax.nn.sigmoid(x)


def _rms(x, g):
    return x * lax.rsqrt(jnp.mean(x * x, axis=-1, keepdims=True) + EPS) * g


def _lane_lo():
    return lax.broadcasted_iota(jnp.int32, (1, LANES), 1) < HD


def _ada_kernel(c_ref, w_ref, b_ref, o_ref):
    a = _silu(c_ref[...])
    o_ref[...] = jnp.dot(a, w_ref[...], preferred_element_type=F32, precision=HIGHEST) + b_ref[...]


def _ada(cc, w, b, layer):
    tn = 1536
    out = pl.pallas_call(
        _ada_kernel,
        grid=(6 * D // tn,),
        in_specs=[pl.BlockSpec((8, D), lambda j: (0, 0)),
                  pl.BlockSpec((None, D, tn), lambda j: (layer, 0, j)),
                  pl.BlockSpec((None, 1, tn), lambda j: (layer, 0, j))],
        out_specs=pl.BlockSpec((8, tn), lambda j: (0, j)),
        out_shape=jax.ShapeDtypeStruct((8, 6 * D), F32),
        compiler_params=_cparams(1),
        name="ada",
    )(cc, w, b.reshape(b.shape[0], 1, 6 * D))
    return out.reshape(8, 1, 6 * D)


def _inproj_kernel(xp_ref, xs_ref, g_ref, shift_ref, scale_ref, w_ref, cos_ref, sin_ref, o_ref, *, rope_cols):
    h = _rms(_pick_rows(xp_ref, xs_ref), g_ref[...]) * (1.0 + scale_ref[...]) + shift_ref[...]
    o = jnp.dot(h.astype(BF16), w_ref[...], preferred_element_type=F32)
    if rope_cols:
        cos = cos_ref[...]
        sin = sin_ref[...]
        lane = lax.broadcasted_iota(jnp.int32, (1, LANES), 1)
        first = (lane % 32) < 16
        for c in range(rope_cols // LANES):
            oc = o[:, c * LANES:(c + 1) * LANES]
            partner = jnp.where(first, pltpu.roll(oc, LANES - 16, 1), pltpu.roll(oc, 16, 1))
            o_ref[:, c * LANES:(c + 1) * LANES] = oc * cos + partner * sin
        o_ref[:, rope_cols:] = o[:, rope_cols:]
    else:
        o_ref[...] = o


def _rope_tables():
    half = HD // 2
    inv = ROPE_BASE ** (-jnp.arange(0, half, 2, dtype=F32) / half)
    t = jnp.arange(DEC_SEQ)
    ang_r = (t // GRID_W).astype(F32)[:, None] * inv[None]
    ang_c = (t % GRID_W).astype(F32)[:, None] * inv[None]

    def head(fn_r, fn_c, sign):
        return jnp.concatenate([sign[0] * fn_r, sign[1] * fn_r, sign[0] * fn_c, sign[1] * fn_c], axis=-1)

    cos = head(jnp.cos(ang_r), jnp.cos(ang_c), (1.0, 1.0))
    sin = head(jnp.sin(ang_r), jnp.sin(ang_c), (-1.0, 1.0))
    cos = jnp.concatenate([jnp.ones((TM, HD), F32), cos], axis=0)
    sin = jnp.concatenate([jnp.zeros((TM, HD), F32), sin], axis=0)
    return jnp.tile(cos, (1, 2)), jnp.tile(sin, (1, 2))


def _inproj(x, g, mod, w_bf16, rope, rope_cols):
    n_out = w_bf16.shape[1]
    npb = NP_TOK // TM
    spb = DEC_SEQ // TM

    def rope_map(i):
        return (jnp.where(i < npb, 0, 1 + (i - npb) % spb), 0)

    return pl.pallas_call(
        functools.partial(_inproj_kernel, rope_cols=rope_cols),
        grid=(N_TOK // TM,),
        in_specs=_pair_specs(D) + [
                  pl.BlockSpec((1, D), lambda i: (0, 0)),
                  _mod_spec(0), _mod_spec(1),
                  pl.BlockSpec((D, n_out), lambda i: (0, 0)),
                  pl.BlockSpec((TM, LANES), rope_map),
                  pl.BlockSpec((TM, LANES), rope_map)],
        out_specs=pl.BlockSpec((TM, n_out), lambda i: (i, 0)),
        out_shape=jax.ShapeDtypeStruct((N_TOK, n_out), F32),
        compiler_params=_cparams(1),
        name="inproj",
    )(x[0], x[1], g.reshape(1, D), mod, mod, w_bf16, rope[0], rope[1])


def _softmax_av(s_list, v_list, sink=None):
    mx = s_list[0].max(axis=-1, keepdims=True)
    for s in s_list[1:]:
        mx = jnp.maximum(mx, s.max(axis=-1, keepdims=True))
    if sink is not None:
        mx = jnp.maximum(mx, sink)
    den = jnp.exp(sink - mx) if sink is not None else 0.0
    acc = None
    for s, v in zip(s_list, v_list):
        p = jnp.exp(s - mx)
        den = den + p.sum(axis=-1, keepdims=True)
        pv = jnp.dot(p.astype(BF16), v, preferred_element_type=F32)
        acc = pv if acc is None else acc + pv
    return acc / den


def _dup_half(x, j, lo):
    xr = pltpu.roll(x, HD, 1)
    return jnp.where(lo, x, xr) if j == 0 else jnp.where(lo, xr, x)


def _stack_heads(q_ref, heads, lo, scale, rows=slice(None)):
    parts = []
    for h in heads:
        qp = q_ref[rows, (h // 2) * LANES:(h // 2 + 1) * LANES]
        keep = lo if h % 2 == 0 else jnp.logical_not(lo)
        parts.append(jnp.where(keep, qp, 0.0) * scale)
    return jnp.concatenate(parts, axis=0).astype(BF16)


def _sink_column(sink_ref, heads, rows):
    return jnp.concatenate([jnp.full((rows, 1), sink_ref[h], F32) for h in heads], axis=0)


def _ctx_gqa_kernel(sink_ref, q_ref, k_ref, v_ref, o_ref):
    lo = _lane_lo()
    k = k_ref[...]
    v = v_ref[...]
    group = A_HEADS // A_KV_HEADS
    scores = []
    for j in range(A_KV_HEADS):
        q = _stack_heads(q_ref, list(range(group * j, group * (j + 1))), lo, HD ** -0.5)
        scores.append(lax.dot_general(q, _dup_half(k, j, lo).astype(BF16), NT_DIMS, preferred_element_type=F32))
    s = jnp.concatenate(scores, axis=0)
    sink = _sink_column(sink_ref, list(range(A_HEADS)), SEQ)
    mx = jnp.maximum(s.max(axis=-1, keepdims=True), sink)
    e = jnp.exp(s - mx)
    den = jnp.exp(sink - mx) + e.sum(axis=-1, keepdims=True)
    e = e.astype(BF16)
    rows_per_group = group * SEQ
    for j in range(A_KV_HEADS):
        rows = slice(j * rows_per_group, (j + 1) * rows_per_group)
        o = jnp.dot(e[rows], _dup_half(v, j, lo).astype(BF16), preferred_element_type=F32) / den[rows]
        for t in range(group // 2):
            pair = (group * j) // 2 + t
            o_ref[:, pair * LANES:(pair + 1) * LANES] = jnp.where(
                lo, o[(2 * t) * SEQ:(2 * t + 1) * SEQ], o[(2 * t + 1) * SEQ:(2 * t + 2) * SEQ])


def _ctx_gqa(p, sink):
    return pl.pallas_call(
        _ctx_gqa_kernel,
        grid_spec=pltpu.PrefetchScalarGridSpec(
            num_scalar_prefetch=1,
            grid=(BATCH,),
            in_specs=[pl.BlockSpec((SEQ, A_Q), lambda b, s: (b, 0)),
                      pl.BlockSpec((SEQ, A_KV), lambda b, s: (b, A_Q // A_KV)),
                      pl.BlockSpec((SEQ, A_KV), lambda b, s: (b, A_Q // A_KV + 1))],
            out_specs=pl.BlockSpec((SEQ, A_Q), lambda b, s: (b, 0))),
        out_shape=jax.ShapeDtypeStruct((NP_TOK, A_Q), F32),
        compiler_params=_cparams(1),
        name="ctx_gqa",
    )(sink, p, p, p)


def _win_kernel(sink_ref, q_ref, kp_ref, kc_ref, kn_ref, vp_ref, vc_ref, vn_ref, ck_ref, cv_ref, o_ref):
    i = pl.program_id(1)
    lo = _lane_lo()
    k = jnp.concatenate([kp_ref[...], kc_ref[...], kn_ref[...]], axis=0)
    v = jnp.concatenate([vp_ref[...], vc_ref[...], vn_ref[...]], axis=0)
    ck = ck_ref[...]
    cv = cv_ref[...]
    group = A_HEADS // A_KV_HEADS
    half_q = WIN_Q // 2
    span = half_q + 2 * A_WINDOW
    s_loc, s_ctx, sinks, values = [], [], [], []
    for j in range(A_KV_HEADS):
        heads = list(range(group * j, group * (j + 1)))
        kd = _dup_half(k, j, lo).astype(BF16)
        ckd = _dup_half(ck, j, lo).astype(BF16)
        vd = _dup_half(v, j, lo).astype(BF16)
        cvd = _dup_half(cv, j, lo).astype(BF16)
        for part in range(2):
            key0 = part * half_q
            qpos = i * WIN_Q + key0 + lax.broadcasted_iota(jnp.int32, (half_q, span), 0)
            kpos = i * WIN_Q - A_WINDOW + key0 + lax.broadcasted_iota(jnp.int32, (half_q, span), 1)
            valid = (jnp.abs(kpos - qpos) <= A_WINDOW) & (kpos >= 0) & (kpos < DEC_SEQ)
            valid = jnp.concatenate([valid] * group, axis=0)
            q = _stack_heads(q_ref, heads, lo, HD ** -0.5, pl.ds(key0, half_q))
            s = lax.dot_general(q, kd[key0:key0 + span], NT_DIMS, preferred_element_type=F32)
            s_loc.append(jnp.where(valid, s, NEG))
            s_ctx.append(lax.dot_general(q, ckd, NT_DIMS, preferred_element_type=F32))
            sinks.append(_sink_column(sink_ref, heads, half_q))
            values.append((j, part, vd[key0:key0 + span], cvd))
    s_loc = jnp.concatenate(s_loc, axis=0)
    s_ctx = jnp.concatenate(s_ctx, axis=0)
    sink = jnp.concatenate(sinks, axis=0)
    mx = jnp.maximum(jnp.maximum(s_loc.max(axis=-1, keepdims=True), s_ctx.max(axis=-1, keepdims=True)), sink)
    p_loc = jnp.exp(s_loc - mx)
    p_ctx = jnp.exp(s_ctx - mx)
    den = p_loc.sum(axis=-1, keepdims=True) + p_ctx.sum(axis=-1, keepdims=True) + jnp.exp(sink - mx)
    p_loc = p_loc.astype(BF16)
    p_ctx = p_ctx.astype(BF16)
    rows_per_unit = group * half_q
    for unit, (j, part, vd, cvd) in enumerate(values):
        rows = slice(unit * rows_per_unit, (unit + 1) * rows_per_unit)
        o = (jnp.dot(p_loc[rows], vd, preferred_element_type=F32)
             + jnp.dot(p_ctx[rows], cvd, preferred_element_type=F32)) / den[rows]
        for t in range(group // 2):
            pair = (group * j) // 2 + t
            o_ref[pl.ds(part * half_q, half_q), pair * LANES:(pair + 1) * LANES] = jnp.where(
                lo, o[(2 * t) * half_q:(2 * t + 1) * half_q], o[(2 * t + 1) * half_q:(2 * t + 2) * half_q])


WIN_Q = 256


def _win_attention(p, cache_k, cache_v, sink):
    nblk = DEC_SEQ // WIN_Q
    side = WIN_Q // A_WINDOW
    nside = DEC_SEQ // A_WINDOW
    base = NP_TOK // WIN_Q
    side_base = NP_TOK // A_WINDOW
    kcol = A_Q // A_KV

    def main_spec(col):
        return pl.BlockSpec((WIN_Q, A_KV), lambda b, i, s: (base + b * nblk + i, col))

    def side_spec(col, off):
        return pl.BlockSpec((A_WINDOW, A_KV),
                            lambda b, i, s: (side_base + b * nside + jnp.clip(side * i + off, 0, nside - 1), col))

    ctx_spec = pl.BlockSpec((None, PAST, A_KV), lambda b, i, s: (b, 0, 0))
    return pl.pallas_call(
        _win_kernel,
        grid_spec=pltpu.PrefetchScalarGridSpec(
            num_scalar_prefetch=1,
            grid=(DEC_BATCH, nblk),
            in_specs=[pl.BlockSpec((WIN_Q, A_Q), lambda b, i, s: (base + b * nblk + i, 0)),
                      side_spec(kcol, -1), main_spec(kcol), side_spec(kcol, side),
                      side_spec(kcol + 1, -1), main_spec(kcol + 1), side_spec(kcol + 1, side),
                      ctx_spec, ctx_spec],
            out_specs=pl.BlockSpec((WIN_Q, A_Q), lambda b, i, s: (b * nblk + i, 0))),
        out_shape=jax.ShapeDtypeStruct((NS_TOK, A_Q), F32),
        compiler_params=_cparams(2),
        name="win_attn",
    )(sink, p, p, p, p, p, p, p, cache_k.reshape(DEC_BATCH, PAST, A_KV), cache_v.reshape(DEC_BATCH, PAST, A_KV))


def _ret_kernel(df_ref, db_ref, q_ref, k_ref, v_ref, g_ref, gn_ref, s0f_ref, s0b_ref,
                o_ref, sf_ref, sb_ref, of_scr, ob_scr, *, length):
    c_len = RET_CHUNK
    n = length // c_len
    lo = _lane_lo()
    hi = jnp.logical_not(lo)
    row = lax.broadcasted_iota(jnp.int32, (c_len, c_len), 0)
    col = lax.broadcasted_iota(jnp.int32, (c_len, c_len), 1)
    rowp = lax.broadcasted_iota(jnp.int32, (LANES, LANES), 0)
    colp = lax.broadcasted_iota(jnp.int32, (LANES, LANES), 1)
    blockdiag = (rowp < HD) == (colp < HD)
    idx = lax.broadcasted_iota(jnp.int32, (c_len, 1), 0).astype(F32)

    def direction(dec_ref, forward):
        lg = -jnp.exp(dec_ref[...])
        diff = (row - col) if forward else (col - row)
        keep = (diff >= 0) if forward else (diff > 0)
        dist = jnp.maximum(diff, 0).astype(F32)
        dm = jnp.concatenate([jnp.where(keep, jnp.exp(dist * lg[:, off:off + 1]), 0.0) for off in (0, HD)], axis=0)
        if forward:
            xi = jnp.exp((idx + 1.0) * lg)
            zeta = jnp.exp((c_len - 1.0 - idx) * lg)
        else:
            xi = jnp.exp((c_len - idx) * lg)
            zeta = jnp.exp(idx * lg)
        return dm, xi, zeta, jnp.exp(c_len * lg)

    def chunk(c, state, consts):
        dm, xi, zeta, gch = consts
        rows = pl.ds(pl.multiple_of(c * c_len, c_len), c_len)
        qc = q_ref[rows, :]
        kc = k_ref[rows, :] * HD ** -0.5
        vc = v_ref[rows, :].astype(BF16)
        kb = kc.astype(BF16)
        q2 = jnp.concatenate([jnp.where(lo, qc, 0.0), jnp.where(hi, qc, 0.0)], axis=0).astype(BF16)
        inner = lax.dot_general(q2, kb, NT_DIMS, preferred_element_type=F32) * dm
        kz_t = (kc * zeta).T
        res = jnp.dot(jnp.concatenate([inner, kz_t], axis=0).astype(BF16), vc, preferred_element_type=F32)
        cross = jnp.dot(qc.astype(BF16), state.astype(BF16), preferred_element_type=F32) * xi
        o = jnp.where(lo, res[:c_len], res[c_len:2 * c_len]) + cross
        state = gch * state + jnp.where(blockdiag, res[2 * c_len:], 0.0)
        return rows, o, state

    cf = direction(df_ref, True)
    cb = direction(db_ref, False)

    def scan_body(t, states):
        rows_f, o_f, state_f = chunk(t, states[0], cf)
        of_scr[rows_f, :] = o_f
        rows_b, o_b, state_b = chunk(n - 1 - t, states[1], cb)
        ob_scr[rows_b, :] = o_b
        return state_f, state_b

    state_f, state_b = lax.fori_loop(0, n, scan_body, (s0f_ref[...], s0b_ref[...]), unroll=min(n, RET_UNROLL))
    sf_ref[...] = state_f
    sb_ref[...] = state_b

    gn = gn_ref[...]
    norm_rows = min(RET_NORM_ROWS, length)
    n_norm = length // norm_rows

    def per_head(x):
        a = jnp.where(lo, x, 0.0).sum(axis=-1, keepdims=True)
        b = jnp.where(hi, x, 0.0).sum(axis=-1, keepdims=True)
        return jnp.where(lo, a, b) * (1.0 / HD)

    def norm_body(t, carry):
        rows = pl.ds(pl.multiple_of(t * norm_rows, norm_rows), norm_rows)
        o = of_scr[rows, :] + ob_scr[rows, :]
        d = o - per_head(o)
        y = d * lax.rsqrt(per_head(d * d) + EPS) * gn
        o_ref[rows, :] = _silu(g_ref[rows, :]) * y
        return carry

    lax.fori_loop(0, n_norm, norm_body, 0)


def _pair_lanes(v):
    return jnp.repeat(v.astype(F32), HD).reshape(B_HEADS // 2, 1, LANES)


def _blockdiag_states(s):
    b = s.shape[0]
    s = s.astype(F32).reshape(b, B_HEADS // 2, 2, HD, HD)
    z = jnp.zeros_like(s[:, :, 0])
    top = jnp.concatenate([s[:, :, 0], z], axis=-1)
    bot = jnp.concatenate([z, s[:, :, 1]], axis=-1)
    return jnp.concatenate([top, bot], axis=-2)


def _diag_states(sp):
    b = sp.shape[0]
    s = jnp.stack([sp[:, :, :HD, :HD], sp[:, :, HD:, HD:]], axis=2)
    return s.reshape(b, B_HEADS, HD, HD)


def _retention(p, row_base, batch, length, dec_f, dec_b, gn_g, s0f, s0b):
    npairs = B_HEADS // 2
    blk0 = row_base // length
    qcol = (A_Q + 2 * A_KV) // LANES

    def col_spec(off):
        return pl.BlockSpec((length, LANES), lambda b, h: (blk0 + b, qcol + off * npairs + h))

    lane_spec = pl.BlockSpec((None, 1, LANES), lambda b, h: (h, 0, 0))
    state_spec = pl.BlockSpec((None, None, LANES, LANES), lambda b, h: (b, h, 0, 0))
    state_shape = jax.ShapeDtypeStruct((batch, npairs, LANES, LANES), F32)
    return pl.pallas_call(
        functools.partial(_ret_kernel, length=length),
        grid=(batch, npairs),
        in_specs=[lane_spec, lane_spec, col_spec(0), col_spec(1), col_spec(2), col_spec(3), lane_spec,
                  state_spec, state_spec],
        out_specs=[pl.BlockSpec((length, LANES), lambda b, h: (b, h)), state_spec, state_spec],
        out_shape=[jax.ShapeDtypeStruct((batch * length, B_W), F32), state_shape, state_shape],
        scratch_shapes=[pltpu.VMEM((length, LANES), F32), pltpu.VMEM((length, LANES), F32)],
        compiler_params=_cparams(2),
        name="retention",
    )(_pair_lanes(dec_f), _pair_lanes(dec_b), p, p, p, p, gn_g.reshape(npairs, 1, LANES), s0f, s0b)


def _ctx_mha_kernel(q_ref, k_ref, v_ref, o_ref):
    lo = _lane_lo()
    for pair in range(C_HEADS // 2):
        cols = slice(pair * LANES, (pair + 1) * LANES)
        q = _stack_heads(q_ref, [2 * pair, 2 * pair + 1], lo, HD ** -0.5)
        s = lax.dot_general(q, k_ref[:, cols].astype(BF16), NT_DIMS, preferred_element_type=F32)
        o = _softmax_av([s], [v_ref[:, cols].astype(BF16)])
        o_ref[:, cols] = jnp.where(lo, o[:SEQ], o[SEQ:])


def _ctx_mha(p):
    return pl.pallas_call(
        _ctx_mha_kernel,
        grid=(BATCH,),
        in_specs=[pl.BlockSpec((SEQ, C_W), lambda b: (b, 0)),
                  pl.BlockSpec((SEQ, C_W), lambda b: (b, 1)),
                  pl.BlockSpec((SEQ, C_W), lambda b: (b, 2))],
        out_specs=pl.BlockSpec((SEQ, C_W), lambda b: (b, 0)),
        out_shape=jax.ShapeDtypeStruct((NP_TOK, C_W), F32),
        compiler_params=_cparams(1),
        name="ctx_mha",
    )(p, p, p)


NA_WIN_ROWS = 2 * NA_ROWS
NA_WIN = NA_WIN_ROWS * GRID_W
NA_QROWS = NA_ROWS * GRID_W
NA_PAD_ROWS = NA_KH // 2
NA_TABLE = 1536
NA_SPAN = 768


def _na_kernel(q_ref, kp_ref, km_ref, kn_ref, vp_ref, vm_ref, vn_ref, ck_ref, cv_ref, ue_ref, uo_ref, o_ref):
    r0 = pl.program_id(2) * NA_ROWS
    n_rows = DEC_SEQ // GRID_W
    lo = _lane_lo()
    k = jnp.concatenate([kp_ref[...], km_ref[...], kn_ref[...]], axis=0).astype(BF16)
    v = jnp.concatenate([vp_ref[...], vm_ref[...], vn_ref[...]], axis=0).astype(BF16)
    ck = ck_ref[...].astype(BF16)
    cv = cv_ref[...].astype(BF16)
    q = q_ref[...] * HD ** -0.5
    outs = []
    for half, keep in enumerate((lo, jnp.logical_not(lo))):
        qh = jnp.where(keep, q, 0.0).astype(BF16)
        s_ctx = lax.dot_general(qh, ck, NT_DIMS, preferred_element_type=F32)
        o_parts = []
        for part in range(2):
            key0 = part * (NA_WIN - NA_SPAN)
            klane = key0 + lax.broadcasted_iota(jnp.int32, (1, NA_SPAN), 1)
            qrows = slice(part * NA_QROWS // 2, (part + 1) * NA_QROWS // 2)
            s = lax.dot_general(qh[qrows], k[key0:key0 + NA_SPAN], NT_DIMS, preferred_element_type=F32)
            p_loc, p_ctx, den = [], [], []
            for t in range(NA_ROWS // 2):
                rq = part * NA_ROWS // 2 + t
                start = NA_KH - 1 - rq
                if start % 2 == 0:
                    u = ue_ref[half, :, start * GRID_W + key0:start * GRID_W + key0 + NA_SPAN]
                else:
                    u = uo_ref[half, :, (start - 1) * GRID_W + key0:(start - 1) * GRID_W + key0 + NA_SPAN]
                r = r0 + rq
                first = jnp.clip(r - NA_KH // 2, 0, n_rows - NA_KH)
                lane0 = (first - r0 + NA_PAD_ROWS) * GRID_W
                in_rows = (klane >= lane0) & (klane < lane0 + NA_KH * GRID_W)
                sl = jnp.where(in_rows, s[t * GRID_W:(t + 1) * GRID_W] + u, NEG)
                sc = s_ctx[rq * GRID_W:(rq + 1) * GRID_W]
                mx = jnp.maximum(sl.max(axis=-1, keepdims=True), sc.max(axis=-1, keepdims=True))
                el = jnp.exp(sl - mx)
                ec = jnp.exp(sc - mx)
                den.append(el.sum(axis=-1, keepdims=True) + ec.sum(axis=-1, keepdims=True))
                p_loc.append(el.astype(BF16))
                p_ctx.append(ec.astype(BF16))
            acc = (jnp.dot(jnp.concatenate(p_loc, axis=0), v[key0:key0 + NA_SPAN], preferred_element_type=F32)
                   + jnp.dot(jnp.concatenate(p_ctx, axis=0), cv, preferred_element_type=F32))
            o_parts.append(acc / jnp.concatenate(den, axis=0))
        outs.append(jnp.concatenate(o_parts, axis=0))
    o_ref[...] = jnp.where(lo, outs[0], outs[1])


def _na_bias_tables(rpb):
    cq = jnp.arange(GRID_W)
    ck = jnp.arange(GRID_W)
    dc = jnp.clip(ck[None] - cq[:, None], -(NA_KW - 1), NA_KW - 1) + NA_KW - 1
    cs = jnp.clip(cq - NA_KW // 2, 0, GRID_W - NA_KW)
    col_ok = (ck[None] >= cs[:, None]) & (ck[None] < cs[:, None] + NA_KW)
    t = rpb.astype(F32)[:, :, dc]
    t = jnp.where(col_ok[None, None], t, NEG).transpose(0, 2, 1, 3)
    n_dr = 2 * NA_KH - 1
    blocks = NA_TABLE // GRID_W
    t = jnp.pad(t, ((0, 0), (0, 0), (NA_PAD_ROWS, blocks - n_dr - NA_PAD_ROWS), (0, 0)), constant_values=NEG)
    ue = t.reshape(C_HEADS, GRID_W, NA_TABLE)
    uo = jnp.concatenate([ue[..., GRID_W:], jnp.full((C_HEADS, GRID_W, GRID_W), NEG, F32)], axis=-1)
    return ue, uo


def _na_attention(p, cache_k, cache_v, rpb):
    npairs = C_HEADS // 2
    nrb = DEC_SEQ // NA_QROWS
    half = NA_QROWS // 2
    qbase = NP_TOK // NA_QROWS
    hbase = NP_TOK // half
    kcol = C_W // LANES
    ue, uo = _na_bias_tables(rpb)

    def main_spec(col0):
        return pl.BlockSpec((NA_QROWS, LANES), lambda b, h, r: (qbase + b * nrb + r, col0 + h))

    def side_spec(col0, off):
        return pl.BlockSpec((half, LANES),
                            lambda b, h, r: (hbase + b * 2 * nrb + jnp.clip(2 * r + off, 0, 2 * nrb - 1), col0 + h))

    ctx_spec = pl.BlockSpec((None, PAST, LANES), lambda b, h, r: (b, 0, h))
    tab_spec = pl.BlockSpec((2, GRID_W, NA_TABLE), lambda b, h, r: (h, 0, 0))
    return pl.pallas_call(
        _na_kernel,
        grid=(DEC_BATCH, npairs, nrb),
        in_specs=[main_spec(0),
                  side_spec(kcol, -1), main_spec(kcol), side_spec(kcol, 2),
                  side_spec(2 * kcol, -1), main_spec(2 * kcol), side_spec(2 * kcol, 2),
                  ctx_spec, ctx_spec, tab_spec, tab_spec],
        out_specs=pl.BlockSpec((NA_QROWS, LANES), lambda b, h, r: (b * nrb + r, h)),
        out_shape=jax.ShapeDtypeStruct((NS_TOK, C_W), F32),
        compiler_params=_cparams(3),
        name="na_attn",
    )(p, p, p, p, p, p, p, cache_k.reshape(DEC_BATCH, PAST, C_W), cache_v.reshape(DEC_BATCH, PAST, C_W), ue, uo)


def _route(biased, scores):
    t = biased.shape[1]
    per_group = N_EXPERTS // N_GROUPS
    i8 = lax.broadcasted_iota(jnp.int32, (per_group, t), 0)
    g_rows = []
    for g in range(N_GROUPS):
        bg = biased[g * per_group:(g + 1) * per_group]
        m1 = bg.max(axis=0, keepdims=True)
        first = jnp.where(bg == m1, i8, per_group).min(axis=0, keepdims=True)
        m2 = jnp.where(i8 == first, -jnp.inf, bg).max(axis=0, keepdims=True)
        g_rows.append(m1 + m2)
    g_top = jnp.concatenate(g_rows, axis=0)
    gi = lax.broadcasted_iota(jnp.int32, g_top.shape, 0)
    g_sel = jnp.zeros(g_top.shape, jnp.int32)
    cur = g_top
    for _ in range(TOPK_GROUPS):
        m = cur.max(axis=0, keepdims=True)
        hit = gi == jnp.where(cur == m, gi, N_GROUPS).min(axis=0, keepdims=True)
        g_sel = jnp.where(hit, 1, g_sel)
        cur = jnp.where(hit, -jnp.inf, cur)
    e_sel = jnp.concatenate([jnp.broadcast_to(g_sel[g:g + 1], (per_group, t)) for g in range(N_GROUPS)], axis=0)
    cur = jnp.where(e_sel > 0, biased, NEG)
    ei = lax.broadcasted_iota(jnp.int32, cur.shape, 0)
    ids, gates, hits = [], [], []
    for _ in range(TOP_K):
        m = cur.max(axis=0, keepdims=True)
        f = jnp.where(cur == m, ei, N_EXPERTS).min(axis=0, keepdims=True)
        hit = ei == f
        ids.append(f)
        hits.append(hit)
        gates.append(jnp.where(hit, scores, 0.0).sum(axis=0, keepdims=True))
        cur = jnp.where(hit, -jnp.inf, cur)
    gate = jnp.concatenate(gates, axis=0)
    gate = gate / gate.sum(axis=0, keepdims=True) * ROUTED_SCALE
    return jnp.concatenate(ids, axis=0), gate, hits


def _pack_bf16_pairs(h):
    bits = lax.bitcast_convert_type(h.astype(BF16).astype(F32), jnp.uint32)
    return bits[:, :D // 2] | (bits[:, D // 2:] >> 16)


def _unpack_bf16_pairs(xp):
    hi = lax.bitcast_convert_type(xp & jnp.uint32(0xFFFF0000), F32).astype(BF16)
    lo = lax.bitcast_convert_type(xp << 16, F32).astype(BF16)
    return hi, lo


def _dot_halves(hi, lo, w_ref):
    return (jnp.dot(hi, w_ref[:D // 2, :], preferred_element_type=F32)
            + jnp.dot(lo, w_ref[D // 2:, :], preferred_element_type=F32))


def _outproj_kernel(*refs, n_parts):
    xp_ref, xs_ref = refs[:2]
    part_refs = refs[2:2 + 3 * n_parts]
    gate_ref, shift_ref, scale_ref, g2_ref, rw_ref, rb_ref = refs[2 + 3 * n_parts:8 + 3 * n_parts]
    xo_ref, h_ref, dest_ref, wgt_ref, plan_ref, cnt_ref = refs[8 + 3 * n_parts:]
    step = pl.program_id(0)

    @pl.when(step == 0)
    def _():
        cnt_ref[...] = jnp.zeros_like(cnt_ref)
        plan_ref[...] = jnp.zeros_like(plan_ref)

    y = None
    for t in range(n_parts):
        ap_ref, as_ref, w_ref = part_refs[3 * t:3 * t + 3]
        d = jnp.dot(_pick_rows(ap_ref, as_ref, TM_OUT).astype(BF16), w_ref[...], preferred_element_type=F32)
        y = d if y is None else y + d
    x = _pick_rows(xp_ref, xs_ref, TM_OUT) + gate_ref[...] * y
    xo_ref[...] = x
    h = _rms(x, g2_ref[...]) * (1.0 + scale_ref[...]) + shift_ref[...]
    h_ref[...] = _pack_bf16_pairs(h)
    h_hi = h.astype(BF16)
    h_lo = (h - h_hi.astype(F32)).astype(BF16)
    rw = rw_ref[...]
    rw_hi = rw.astype(BF16)
    rw_lo = (rw - rw_hi.astype(F32)).astype(BF16)
    logits = (lax.dot_general(rw_hi, h_hi, NT_DIMS, preferred_element_type=F32)
              + lax.dot_general(rw_hi, h_lo, NT_DIMS, preferred_element_type=F32)
              + lax.dot_general(rw_lo, h_hi, NT_DIMS, preferred_element_type=F32))
    scores = jax.nn.sigmoid(logits)
    _, gate, hits = _route(scores + rb_ref[...], scores)
    wgt_ref[...] = gate
    chosen = hits[0]
    for hit in hits[1:]:
        chosen = chosen | hit
    m = jnp.where(chosen, 1.0, 0.0)
    before = (lax.broadcasted_iota(jnp.int32, (TM_OUT, TM_OUT), 0)
              < lax.broadcasted_iota(jnp.int32, (TM_OUT, TM_OUT), 1))
    prefix = jnp.dot(m.astype(BF16), jnp.where(before, 1.0, 0.0).astype(BF16), preferred_element_type=F32)
    e_base = (lax.broadcasted_iota(jnp.int32, (N_EXPERTS, 1), 0) * N_TOK).astype(F32)
    row_all = prefix + (cnt_ref[...] + e_base)
    dest_ref[...] = jnp.concatenate(
        [jnp.where(hit, row_all, 0.0).sum(axis=0, keepdims=True) for hit in hits], axis=0).astype(jnp.int32)
    cnt_ref[...] += m.sum(axis=1, keepdims=True)

    @pl.when(step == pl.num_programs(0) - 1)
    def _():
        _block_plan(cnt_ref[...], plan_ref)


def _block_plan(counts, plan_ref):
    cap_blocks = N_TOK // MOE_BLOCK
    nblk = ((counts.astype(jnp.int32) + (MOE_BLOCK - 1)) // MOE_BLOCK).astype(F32)
    lower = (lax.broadcasted_iota(jnp.int32, (N_EXPERTS, N_EXPERTS), 0)
             >= lax.broadcasted_iota(jnp.int32, (N_EXPERTS, N_EXPERTS), 1))
    cum = jnp.dot(jnp.where(lower, 1.0, 0.0).astype(BF16), jnp.broadcast_to(nblk, (N_EXPERTS, LANES)).astype(BF16),
                  preferred_element_type=F32)[:, :1]
    n_used = cum[N_EXPERTS - 1:, :]
    slot = jnp.minimum(lax.broadcasted_iota(jnp.int32, (1, PLAN_LANES), 1).astype(F32), n_used - 1.0)
    done = cum <= slot
    expert = jnp.where(done, 1.0, 0.0).sum(axis=0, keepdims=True)
    blocks_before = jnp.where(done, nblk, 0.0).sum(axis=0, keepdims=True)
    plan_ref[0:1, :] = (expert * cap_blocks + (slot - blocks_before)).astype(jnp.int32)
    plan_ref[1:2, :] = expert.astype(jnp.int32)
    plan_ref[2:3, :] = jnp.broadcast_to(n_used, (1, PLAN_LANES)).astype(jnp.int32)
    run_end = jnp.where(cum > slot, cum, jnp.inf).min(axis=0, keepdims=True)
    plan_ref[3:4, :] = jnp.minimum(jnp.where(cum <= run_end, 1.0, 0.0).sum(axis=0, keepdims=True),
                                   N_EXPERTS - 1.0).astype(jnp.int32)
    plan_ref[4:5, :] = jnp.where(run_end < n_used, 1, 0).astype(jnp.int32)


def _outproj(x, parts, mod, g2, router_w, router_b):
    in_specs = _pair_specs(D, TM_OUT)
    args = [x[0], x[1]]
    for ap, a_s, w in parts:
        width = ap.shape[1]
        in_specs += _pair_specs(width, TM_OUT) + [pl.BlockSpec((width, D), lambda i: (0, 0))]
        args += [ap, a_s, w]
    in_specs += [_mod_spec(2, TM_OUT), _mod_spec(3, TM_OUT), _mod_spec(4, TM_OUT),
                 pl.BlockSpec((1, D), lambda i: (0, 0)),
                 pl.BlockSpec((N_EXPERTS, D), lambda i: (0, 0)),
                 pl.BlockSpec((N_EXPERTS, 1), lambda i: (0, 0))]
    args += [mod, mod, mod, g2.reshape(1, D), router_w.T, router_b.reshape(N_EXPERTS, 1)]
    return pl.pallas_call(
        functools.partial(_outproj_kernel, n_parts=len(parts)),
        grid=(N_TOK // TM_OUT,),
        in_specs=in_specs,
        out_specs=[pl.BlockSpec((TM_OUT, D), lambda i: (i, 0)),
                   pl.BlockSpec((TM_OUT, D // 2), lambda i: (i, 0)),
                   pl.BlockSpec((TOP_K, TM_OUT), lambda i: (0, i)),
                   pl.BlockSpec((TOP_K, TM_OUT), lambda i: (0, i)),
                   pl.BlockSpec((8, PLAN_LANES), lambda i: (0, 0))],
        out_shape=[jax.ShapeDtypeStruct((N_TOK, D), F32),
                   jax.ShapeDtypeStruct((N_TOK, D // 2), jnp.uint32),
                   jax.ShapeDtypeStruct((TOP_K, N_TOK), jnp.int32),
                   jax.ShapeDtypeStruct((TOP_K, N_TOK), F32),
                   jax.ShapeDtypeStruct((8, PLAN_LANES), jnp.int32)],
        scratch_shapes=[pltpu.VMEM((N_EXPERTS, 1), F32)],
        compiler_params=_cparams(1),
        name="outproj_router",
    )(*args)


def _experts_kernel(br_ref, be_ref, nu_ref, ne_ref, hn_ref, x_hbm, w1_hbm, w3_hbm, w2_hbm, o_ref,
                    w1b, w3b, w2b, w1f, w3f, w2f, xbuf, xsem, wsem, *, layer):
    i = pl.program_id(0)
    n_steps = pl.num_programs(0)
    e = be_ref[i]
    prev = be_ref[jnp.maximum(i - 1, 0)]

    def x_copy(step):
        slot = step % EXPERT_X_SLOTS
        rows = pl.ds(pl.multiple_of(br_ref[step] * MOE_BLOCK, MOE_BLOCK), MOE_BLOCK)
        return pltpu.make_async_copy(x_hbm.at[rows], xbuf.at[slot], xsem.at[slot])

    @pl.when(i == 0)
    def _():
        x_copy(0).start()
        x_copy(1).start()

    @pl.when(i + 2 < n_steps)
    def _():
        x_copy(i + 2).start()

    x_copy(i).wait()

    def w_copies(expert):
        return [pltpu.make_async_copy(src.at[layer, expert], dst, wsem.at[k])
                for k, (src, dst) in enumerate(((w1_hbm, w1f), (w3_hbm, w3f), (w2_hbm, w2f)))]

    @pl.when(i == 0)
    def _():
        for c in w_copies(e):
            c.start()

    @pl.when((i == 0) | (e != prev))
    def _():
        for c in w_copies(e):
            c.wait()
        w1b[...] = w1f[...].astype(BF16)
        w3b[...] = w3f[...].astype(BF16)
        w2b[...] = w2f[...].astype(BF16)

        @pl.when(hn_ref[i] > 0)
        def _():
            for c in w_copies(ne_ref[i]):
                c.start()

    @pl.when(i < nu_ref[0])
    def _():
        hi, lo = _unpack_bf16_pairs(xbuf[i % EXPERT_X_SLOTS])
        a = _dot_halves(hi, lo, w1b)
        b = _dot_halves(hi, lo, w3b)
        h = (_silu(a) * b).astype(BF16)
        o_ref[...] = _pack_bf16_pairs(jnp.dot(h, w2b[...], preferred_element_type=F32))


def _experts(plan, x_rows, w1, w3, w2, layer):
    any_spec = pl.BlockSpec(memory_space=pl.ANY)
    return pl.pallas_call(
        functools.partial(_experts_kernel, layer=layer),
        grid_spec=pltpu.PrefetchScalarGridSpec(
            num_scalar_prefetch=5,
            grid=(N_MOE_BLOCKS,),
            in_specs=[any_spec, any_spec, any_spec, any_spec],
            out_specs=pl.BlockSpec((MOE_BLOCK, D // 2), lambda i, br, be, nu, ne, hn: (br[i], 0)),
            scratch_shapes=[pltpu.VMEM((D, FF), BF16), pltpu.VMEM((D, FF), BF16), pltpu.VMEM((FF, D), BF16),
                            pltpu.VMEM((D, FF), F32), pltpu.VMEM((D, FF), F32), pltpu.VMEM((FF, D), F32),
                            pltpu.VMEM((EXPERT_X_SLOTS, MOE_BLOCK, D // 2), jnp.uint32),
                            pltpu.SemaphoreType.DMA((EXPERT_X_SLOTS,)),
                            pltpu.SemaphoreType.DMA((3,))]),
        out_shape=jax.ShapeDtypeStruct(x_rows.shape, jnp.uint32),
        compiler_params=_cparams(1),
        name="experts",
    )(plan[0], plan[1], plan[2, :1], plan[3], plan[4], x_rows, w1, w3, w2)


SC_CORES = 2
SC_SUBCORES = 16
SC_WORKERS = SC_CORES * SC_SUBCORES
SC_CHUNK_BYTES = 64 * 1024
SC_SLOTS = 4


def _sc_scatter(rows, dest, n_out):
    n_rows, width = rows.shape
    picks = dest.shape[0]
    chunk = SC_CHUNK_BYTES // (4 * width)
    per_worker = n_rows // SC_WORKERS
    n_chunks = per_worker // chunk
    assert per_worker * SC_WORKERS == n_rows and n_chunks * chunk == per_worker and n_chunks % 2 == 0
    mesh = plsc.VectorSubcoreMesh(core_axis_name="c", subcore_axis_name="s")

    @functools.partial(
        pl.kernel, mesh=mesh,
        out_type=jax.ShapeDtypeStruct((n_out, width), rows.dtype),
        scratch_types=[pltpu.VMEM((picks, n_chunks, chunk), jnp.int32),
                       pltpu.VMEM((2, chunk, width), rows.dtype),
                       pltpu.SemaphoreType.DMA((2,)),
                       pltpu.SemaphoreType.DMA((2,))])
    def scatter(r_hbm, d_hbm, o_hbm, idx_v, rows_v, lsem, ssem):
        worker = lax.axis_index("s") * SC_CORES + lax.axis_index("c")
        base = worker * per_worker
        for k in range(picks):
            pltpu.sync_copy(d_hbm.at[k, worker], idx_v.at[k])

        def load_copy(c, b):
            src = pl.ds(pl.multiple_of(base + c * chunk, chunk), chunk)
            return pltpu.make_async_copy(r_hbm.at[src], rows_v.at[b], lsem.at[b])

        def store_copy(c, b, k):
            return pltpu.make_async_copy(rows_v.at[b], o_hbm.at[idx_v.at[k, c]], ssem.at[b])

        load_copy(0, 0).start()

        @pl.loop(0, n_chunks, step=2)
        def _(c0):
            for b in range(2):
                c = c0 + b
                load_copy(c, b).wait()
                for k in range(picks):
                    store_copy(c, b, k).start()

                @pl.when(c > 0)
                def _():
                    for k in range(picks):
                        store_copy(c - 1, 1 - b, k).wait()

                @pl.when(c + 1 < n_chunks)
                def _():
                    load_copy(c + 1, 1 - b).start()

        for k in range(picks):
            store_copy(n_chunks - 1, 1, k).wait()

    return scatter(rows, dest.reshape(picks, SC_WORKERS, n_chunks, chunk))


def _sc_gather(table, idx):
    n_idx = idx.shape[0]
    width = table.shape[1]
    chunk = SC_CHUNK_BYTES // (4 * width)
    per_worker = n_idx // SC_WORKERS
    n_chunks = per_worker // chunk
    ahead = SC_SLOTS - 1
    assert per_worker * SC_WORKERS == n_idx and n_chunks * chunk == per_worker and n_chunks % SC_SLOTS == 0
    mesh = plsc.VectorSubcoreMesh(core_axis_name="c", subcore_axis_name="s")

    @functools.partial(
        pl.kernel, mesh=mesh,
        out_type=jax.ShapeDtypeStruct((n_idx, width), table.dtype),
        scratch_types=[pltpu.VMEM((per_worker,), jnp.int32),
                       pltpu.VMEM((SC_SLOTS, chunk, width), table.dtype),
                       pltpu.SemaphoreType.DMA((SC_SLOTS,)),
                       pltpu.SemaphoreType.DMA((SC_SLOTS,))])
    def gather(t_hbm, i_hbm, o_hbm, idx_v, rows_v, gsem, wsem):
        worker = lax.axis_index("s") * SC_CORES + lax.axis_index("c")
        base = worker * per_worker
        pltpu.sync_copy(i_hbm.at[pl.ds(pl.multiple_of(base, chunk), per_worker)], idx_v)

        def gather_copy(c, b):
            ids = idx_v.at[pl.ds(pl.multiple_of(c * chunk, chunk), chunk)]
            return pltpu.make_async_copy(t_hbm.at[ids], rows_v.at[b], gsem.at[b])

        def write_copy(c, b):
            rows = pl.ds(pl.multiple_of(base + c * chunk, chunk), chunk)
            return pltpu.make_async_copy(rows_v.at[b], o_hbm.at[rows], wsem.at[b])

        for c in range(ahead):
            gather_copy(c, c).start()

        @pl.loop(0, n_chunks, step=SC_SLOTS)
        def _(c0):
            for b in range(SC_SLOTS):
                c = c0 + b
                refill = (b + ahead) % SC_SLOTS
                gather_copy(c, b).wait()
                write_copy(c, b).start()

                @pl.when(c > 0)
                def _():
                    write_copy(c - 1, refill).wait()

                @pl.when(c + ahead < n_chunks)
                def _():
                    gather_copy(c + ahead, refill).start()

        write_copy(n_chunks - 1, (n_chunks - 1) % SC_SLOTS).wait()

    return gather(table, idx)


TC = 512


def _combine_kernel(x_ref, h_ref, y_ref, wgt_ref, gate_ref, w1_ref, w3_ref, w2_ref, fg_ref, o_ref, *, final):
    hi, lo = _unpack_bf16_pairs(h_ref[...])
    a = _dot_halves(hi, lo, w1_ref)
    b = _dot_halves(hi, lo, w3_ref)
    ffn = jnp.dot((_silu(a) * b).astype(BF16), w2_ref[...], preferred_element_type=F32)
    wgt = wgt_ref[...]
    r_hi = None
    r_lo = None
    for k in range(TOP_K):
        yk = y_ref[k]
        w = wgt[:, k:k + 1]
        t_hi = lax.bitcast_convert_type(yk & jnp.uint32(0xFFFF0000), F32) * w
        t_lo = lax.bitcast_convert_type(yk << 16, F32) * w
        r_hi = t_hi if r_hi is None else r_hi + t_hi
        r_lo = t_lo if r_lo is None else r_lo + t_lo
    x = x_ref[...] + gate_ref[...] * (ffn + jnp.concatenate([r_hi, r_lo], axis=1))
    o_ref[...] = _rms(x, fg_ref[...]) if final else x


def _combine(x, h, y_rows, wgt, mod, sw1, sw3, sw2, final_g, final):
    weights = (sw1.astype(BF16), sw3.astype(BF16), sw2.astype(BF16), final_g.reshape(1, D))

    def rows_from(first_row, n_rows):
        b0 = first_row // TC
        return pl.pallas_call(
            functools.partial(_combine_kernel, final=final),
            grid=(n_rows // TC,),
            in_specs=[pl.BlockSpec((TC, D), lambda i: (i + b0, 0)),
                      pl.BlockSpec((TC, D // 2), lambda i: (i + b0, 0)),
                      pl.BlockSpec((TOP_K, TC, D // 2), lambda i: (0, i + b0, 0)),
                      pl.BlockSpec((TC, TOP_K), lambda i: (i + b0, 0)),
                      _mod_spec(5, TC, b0),
                      pl.BlockSpec((D, FF), lambda i: (0, 0)),
                      pl.BlockSpec((D, FF), lambda i: (0, 0)),
                      pl.BlockSpec((FF, D), lambda i: (0, 0)),
                      pl.BlockSpec((1, D), lambda i: (0, 0))],
            out_specs=pl.BlockSpec((TC, D), lambda i: (i, 0)),
            out_shape=jax.ShapeDtypeStruct((n_rows, D), F32),
            compiler_params=_cparams(1),
            name="combine",
        )(x, h, y_rows, wgt, mod, *weights)

    return rows_from(0, NP_TOK), rows_from(NP_TOK, NS_TOK)


def kernel(x_prompt, x_sample, cache_a_k, cache_a_v, state_ret_fwd, state_ret_bwd, cache_c_k, cache_c_v,
           c, c_ctx, norm1_g, norm2_g, ada_w, ada_b, even_w_in, even_w_out, sink_a, ret_decay_fwd,
           ret_decay_bwd, ret_gn_g, odd_w_in, odd_w_out, na_rpb, router_w, router_b, exp_w1, exp_w3,
           exp_w2, sh_w1, sh_w3, sh_w2, final_g):
    x = (x_prompt.reshape(NP_TOK, D), x_sample.reshape(NS_TOK, D))
    cc = jnp.concatenate([c_ctx[None], c, jnp.zeros((8 - 1 - DEC_BATCH, D), F32)], axis=0)
    rope = _rope_tables()
    outs = {}
    for l in range(2):
        mod = _ada(cc, ada_w, ada_b, l)
        if l == 0:
            p = _inproj(x, norm1_g[l], mod, even_w_in[0].astype(BF16), rope, A_Q + A_KV)
            oa_p = _ctx_gqa(p, sink_a[0])
            oa_s = _win_attention(p, cache_a_k[:, 0], cache_a_v[:, 0], sink_a[0])
            zero = jnp.zeros((BATCH, B_HEADS // 2, LANES, LANES), F32)
            ob_p, sf, sb = _retention(p, 0, BATCH, SEQ, ret_decay_fwd[0], ret_decay_bwd[0], ret_gn_g[0], zero, zero)
            ob_s, _, _ = _retention(p, NP_TOK, DEC_BATCH, DEC_SEQ, ret_decay_fwd[0], ret_decay_bwd[0], ret_gn_g[0],
                                    _blockdiag_states(state_ret_fwd[:, 0]), _blockdiag_states(state_ret_bwd[:, 0]))
            w_out = even_w_out[0].astype(BF16)
            parts = [(oa_p, oa_s, w_out[:A_Q]), (ob_p, ob_s, w_out[A_Q:])]
            outs["a_k"] = p[:NP_TOK, A_Q:A_Q + A_KV].reshape(BATCH, 1, SEQ, A_KV_HEADS, HD)
            outs["a_v"] = p[:NP_TOK, A_Q + A_KV:A_Q + 2 * A_KV].reshape(BATCH, 1, SEQ, A_KV_HEADS, HD)
            outs["r_f"] = _diag_states(sf).reshape(BATCH, 1, B_HEADS, HD, HD)
            outs["r_b"] = _diag_states(sb).reshape(BATCH, 1, B_HEADS, HD, HD)
        else:
            p = _inproj(x, norm1_g[l], mod, odd_w_in[0].astype(BF16), rope, 0)
            o_p = _ctx_mha(p)
            o_s = _na_attention(p, cache_c_k[:, 0], cache_c_v[:, 0], na_rpb[0])
            parts = [(o_p, o_s, odd_w_out[0].astype(BF16))]
            outs["c_k"] = p[:NP_TOK, C_W:2 * C_W].reshape(BATCH, 1, SEQ, C_HEADS, HD)
            outs["c_v"] = p[:NP_TOK, 2 * C_W:3 * C_W].reshape(BATCH, 1, SEQ, C_HEADS, HD)
        x_mid, h, dest, gate_t, plan = _outproj(x, parts, mod, norm2_g[l], router_w[l], router_b[l])
        y = _experts(plan, _sc_scatter(h, dest, N_EXPERTS * N_TOK), exp_w1, exp_w3, exp_w2, l)
        y_rows = _sc_gather(y, dest.reshape(N_ASSIGN)).reshape(TOP_K, N_TOK, D // 2)
        x = _combine(x_mid, h, y_rows, gate_t.T, mod, sh_w1[l], sh_w3[l], sh_w2[l], final_g, final=(l == 1))
    y_prompt = x[0].reshape(BATCH, SEQ, D)
    y_sample = x[1].reshape(DEC_BATCH, DEC_SEQ, D)
    return (y_prompt, y_sample, outs["a_k"], outs["a_v"], outs["r_f"], outs["r_b"], outs["c_k"], outs["c_v"])
```

```python
import functools

import jax
import jax.numpy as jnp
from jax import lax
from jax.experimental import pallas as pl
from jax.experimental.pallas import tpu as pltpu
from jax.experimental.pallas import tpu_sc as plsc

F32 = jnp.float32
BF16 = jnp.bfloat16
HIGHEST = lax.Precision.HIGHEST

D = 1024
BATCH = 32
SEQ = 256
DEC_BATCH = 4
DEC_SEQ = 4096
PAST = 256
GRID_W = 64
HD = 64
EPS = 1e-6
NEG = -1e30
ROPE_BASE = 10000.0
A_HEADS = 8
A_KV_HEADS = 2
A_Q = A_HEADS * HD
A_KV = A_KV_HEADS * HD
B_HEADS = 8
B_W = B_HEADS * HD
EVEN_IN = A_Q + 2 * A_KV + 4 * B_W
C_HEADS = 16
C_W = C_HEADS * HD
NA_KH = 8
NA_KW = 16
N_EXPERTS = 64
TOP_K = 8
N_GROUPS = 8
TOPK_GROUPS = 4
FF = 256
ROUTED_SCALE = 2.5
MOE_BLOCK = 1024
EXPERT_X_SLOTS = 3
RET_CHUNK = 256
RET_UNROLL = 4
RET_NORM_ROWS = 1024
A_WINDOW = 128

NP_TOK = BATCH * SEQ
NS_TOK = DEC_BATCH * DEC_SEQ
N_TOK = NP_TOK + NS_TOK
N_ASSIGN = N_TOK * TOP_K
N_MOE_BLOCKS = (N_ASSIGN + N_EXPERTS * (MOE_BLOCK - 1) + MOE_BLOCK - 1) // MOE_BLOCK
PLAN_LANES = 512
assert N_TOK % MOE_BLOCK == 0 and N_MOE_BLOCKS <= PLAN_LANES

LANES = 128
TM = 512
TM_OUT = 1024
NA_ROWS = 8
V7X_VMEM_LIMIT = 56 * 1024 * 1024

NT_DIMS = (((1,), (1,)), ((), ()))


def _cparams(n_axes, vmem=V7X_VMEM_LIMIT):
    return pltpu.CompilerParams(dimension_semantics=("arbitrary",) * n_axes, vmem_limit_bytes=vmem)


def _seg_of_block(i, rows):
    row0 = i * rows
    return jnp.where(row0 < NP_TOK, 0, 1 + (row0 - NP_TOK) // DEC_SEQ)


def _mod_spec(chunk, rows=TM, first_block=0):
    return pl.BlockSpec((None, 1, D), lambda i: (_seg_of_block(i + first_block, rows), 0, chunk))


def _pair_specs(width, rows=TM):
    npb = NP_TOK // rows
    nsb = NS_TOK // rows
    return [pl.BlockSpec((rows, width), lambda i: (jnp.minimum(i, npb - 1), 0)),
            pl.BlockSpec((rows, width), lambda i: (jnp.clip(i - npb, 0, nsb - 1), 0))]


def _pick_rows(p_ref, s_ref, rows=TM):
    return jnp.where(pl.program_id(0) < NP_TOK // rows, p_ref[...], s_ref[...])


def _silu(x):
    return x * jax.nn.sigmoid(x)


def _rms(x, g):
    return x * lax.rsqrt(jnp.mean(x * x, axis=-1, keepdims=True) + EPS) * g


def _lane_lo():
    return lax.broadcasted_iota(jnp.int32, (1, LANES), 1) < HD


def _ada_kernel(c_ref, w_ref, b_ref, o_ref):
    a = _silu(c_ref[...])
    o_ref[...] = jnp.dot(a, w_ref[...], preferred_element_type=F32, precision=HIGHEST) + b_ref[...]


def _ada(cc, w, b, layer):
    tn = 1536
    out = pl.pallas_call(
        _ada_kernel,
        grid=(6 * D // tn,),
        in_specs=[pl.BlockSpec((8, D), lambda j: (0, 0)),
                  pl.BlockSpec((None, D, tn), lambda j: (layer, 0, j)),
                  pl.BlockSpec((None, 1, tn), lambda j: (layer, 0, j))],
        out_specs=pl.BlockSpec((8, tn), lambda j: (0, j)),
        out_shape=jax.ShapeDtypeStruct((8, 6 * D), F32),
        compiler_params=_cparams(1),
        name="ada",
    )(cc, w, b.reshape(b.shape[0], 1, 6 * D))
    return out.reshape(8, 1, 6 * D)


def _inproj_kernel(xp_ref, xs_ref, g_ref, shift_ref, scale_ref, w_ref, cos_ref, sin_ref, o_ref, *, rope_cols):
    h = _rms(_pick_rows(xp_ref, xs_ref), g_ref[...]) * (1.0 + scale_ref[...]) + shift_ref[...]
    o = jnp.dot(h.astype(BF16), w_ref[...], preferred_element_type=F32)
    if rope_cols:
        cos = cos_ref[...]
        sin = sin_ref[...]
        lane = lax.broadcasted_iota(jnp.int32, (1, LANES), 1)
        first = (lane % 32) < 16
        for c in range(rope_cols // LANES):
            oc = o[:, c * LANES:(c + 1) * LANES]
            partner = jnp.where(first, pltpu.roll(oc, LANES - 16, 1), pltpu.roll(oc, 16, 1))
            o_ref[:, c * LANES:(c + 1) * LANES] = oc * cos + partner * sin
        o_ref[:, rope_cols:] = o[:, rope_cols:]
    else:
        o_ref[...] = o


def _rope_tables():
    half = HD // 2
    inv = ROPE_BASE ** (-jnp.arange(0, half, 2, dtype=F32) / half)
    t = jnp.arange(DEC_SEQ)
    ang_r = (t // GRID_W).astype(F32)[:, None] * inv[None]
    ang_c = (t % GRID_W).astype(F32)[:, None] * inv[None]

    def head(fn_r, fn_c, sign):
        return jnp.concatenate([sign[0] * fn_r, sign[1] * fn_r, sign[0] * fn_c, sign[1] * fn_c], axis=-1)

    cos = head(jnp.cos(ang_r), jnp.cos(ang_c), (1.0, 1.0))
    sin = head(jnp.sin(ang_r), jnp.sin(ang_c), (-1.0, 1.0))
    cos = jnp.concatenate([jnp.ones((TM, HD), F32), cos], axis=0)
    sin = jnp.concatenate([jnp.zeros((TM, HD), F32), sin], axis=0)
    return jnp.tile(cos, (1, 2)), jnp.tile(sin, (1, 2))


def _inproj(x, g, mod, w_bf16, rope, rope_cols):
    n_out = w_bf16.shape[1]
    npb = NP_TOK // TM
    spb = DEC_SEQ // TM

    def rope_map(i):
        return (jnp.where(i < npb, 0, 1 + (i - npb) % spb), 0)

    return pl.pallas_call(
        functools.partial(_inproj_kernel, rope_cols=rope_cols),
        grid=(N_TOK // TM,),
        in_specs=_pair_specs(D) + [
                  pl.BlockSpec((1, D), lambda i: (0, 0)),
                  _mod_spec(0), _mod_spec(1),
                  pl.BlockSpec((D, n_out), lambda i: (0, 0)),
                  pl.BlockSpec((TM, LANES), rope_map),
                  pl.BlockSpec((TM, LANES), rope_map)],
        out_specs=pl.BlockSpec((TM, n_out), lambda i: (i, 0)),
        out_shape=jax.ShapeDtypeStruct((N_TOK, n_out), F32),
        compiler_params=_cparams(1),
        name="inproj",
    )(x[0], x[1], g.reshape(1, D), mod, mod, w_bf16, rope[0], rope[1])


def _softmax_av(s_list, v_list, sink=None):
    mx = s_list[0].max(axis=-1, keepdims=True)
    for s in s_list[1:]:
        mx = jnp.maximum(mx, s.max(axis=-1, keepdims=True))
    if sink is not None:
        mx = jnp.maximum(mx, sink)
    den = jnp.exp(sink - mx) if sink is not None else 0.0
    acc = None
    for s, v in zip(s_list, v_list):
        p = jnp.exp(s - mx)
        den = den + p.sum(axis=-1, keepdims=True)
        pv = jnp.dot(p.astype(BF16), v, preferred_element_type=F32)
        acc = pv if acc is None else acc + pv
    return acc / den


def _dup_half(x, j, lo):
    xr = pltpu.roll(x, HD, 1)
    return jnp.where(lo, x, xr) if j == 0 else jnp.where(lo, xr, x)


def _stack_heads(q_ref, heads, lo, scale, rows=slice(None)):
    parts = []
    for h in heads:
        qp = q_ref[rows, (h // 2) * LANES:(h // 2 + 1) * LANES]
        keep = lo if h % 2 == 0 else jnp.logical_not(lo)
        parts.append(jnp.where(keep, qp, 0.0) * scale)
    return jnp.concatenate(parts, axis=0).astype(BF16)


def _sink_column(sink_ref, heads, rows):
    return jnp.concatenate([jnp.full((rows, 1), sink_ref[h], F32) for h in heads], axis=0)


def _ctx_gqa_kernel(sink_ref, q_ref, k_ref, v_ref, o_ref):
    lo = _lane_lo()
    k = k_ref[...]
    v = v_ref[...]
    group = A_HEADS // A_KV_HEADS
    scores = []
    for j in range(A_KV_HEADS):
        q = _stack_heads(q_ref, list(range(group * j, group * (j + 1))), lo, HD ** -0.5)
        scores.append(lax.dot_general(q, _dup_half(k, j, lo).astype(BF16), NT_DIMS, preferred_element_type=F32))
    s = jnp.concatenate(scores, axis=0)
    sink = _sink_column(sink_ref, list(range(A_HEADS)), SEQ)
    mx = jnp.maximum(s.max(axis=-1, keepdims=True), sink)
    e = jnp.exp(s - mx)
    den = jnp.exp(sink - mx) + e.sum(axis=-1, keepdims=True)
    e = e.astype(BF16)
    rows_per_group = group * SEQ
    for j in range(A_KV_HEADS):
        rows = slice(j * rows_per_group, (j + 1) * rows_per_group)
        o = jnp.dot(e[rows], _dup_half(v, j, lo).astype(BF16), preferred_element_type=F32) / den[rows]
        for t in range(group // 2):
            pair = (group * j) // 2 + t
            o_ref[:, pair * LANES:(pair + 1) * LANES] = jnp.where(
                lo, o[(2 * t) * SEQ:(2 * t + 1) * SEQ], o[(2 * t + 1) * SEQ:(2 * t + 2) * SEQ])


def _ctx_gqa(p, sink):
    return pl.pallas_call(
        _ctx_gqa_kernel,
        grid_spec=pltpu.PrefetchScalarGridSpec(
            num_scalar_prefetch=1,
            grid=(BATCH,),
            in_specs=[pl.BlockSpec((SEQ, A_Q), lambda b, s: (b, 0)),
                      pl.BlockSpec((SEQ, A_KV), lambda b, s: (b, A_Q // A_KV)),
                      pl.BlockSpec((SEQ, A_KV), lambda b, s: (b, A_Q // A_KV + 1))],
            out_specs=pl.BlockSpec((SEQ, A_Q), lambda b, s: (b, 0))),
        out_shape=jax.ShapeDtypeStruct((NP_TOK, A_Q), F32),
        compiler_params=_cparams(1),
        name="ctx_gqa",
    )(sink, p, p, p)


def _win_kernel(sink_ref, q_ref, kp_ref, kc_ref, kn_ref, vp_ref, vc_ref, vn_ref, ck_ref, cv_ref, o_ref):
    i = pl.program_id(1)
    lo = _lane_lo()
    k = jnp.concatenate([kp_ref[...], kc_ref[...], kn_ref[...]], axis=0)
    v = jnp.concatenate([vp_ref[...], vc_ref[...], vn_ref[...]], axis=0)
    ck = ck_ref[...]
    cv = cv_ref[...]
    group = A_HEADS // A_KV_HEADS
    half_q = WIN_Q // 2
    span = half_q + 2 * A_WINDOW
    s_loc, s_ctx, sinks, values = [], [], [], []
    for j in range(A_KV_HEADS):
        heads = list(range(group * j, group * (j + 1)))
        kd = _dup_half(k, j, lo).astype(BF16)
        ckd = _dup_half(ck, j, lo).astype(BF16)
        vd = _dup_half(v, j, lo).astype(BF16)
        cvd = _dup_half(cv, j, lo).astype(BF16)
        for part in range(2):
            key0 = part * half_q
            qpos = i * WIN_Q + key0 + lax.broadcasted_iota(jnp.int32, (half_q, span), 0)
            kpos = i * WIN_Q - A_WINDOW + key0 + lax.broadcasted_iota(jnp.int32, (half_q, span), 1)
            valid = (jnp.abs(kpos - qpos) <= A_WINDOW) & (kpos >= 0) & (kpos < DEC_SEQ)
            valid = jnp.concatenate([valid] * group, axis=0)
            q = _stack_heads(q_ref, heads, lo, HD ** -0.5, pl.ds(key0, half_q))
            s = lax.dot_general(q, kd[key0:key0 + span], NT_DIMS, preferred_element_type=F32)
            s_loc.append(jnp.where(valid, s, NEG))
            s_ctx.append(lax.dot_general(q, ckd, NT_DIMS, preferred_element_type=F32))
            sinks.append(_sink_column(sink_ref, heads, half_q))
            values.append((j, part, vd[key0:key0 + span], cvd))
    s_loc = jnp.concatenate(s_loc, axis=0)
    s_ctx = jnp.concatenate(s_ctx, axis=0)
    sink = jnp.concatenate(sinks, axis=0)
    mx = jnp.maximum(jnp.maximum(s_loc.max(axis=-1, keepdims=True), s_ctx.max(axis=-1, keepdims=True)), sink)
    p_loc = jnp.exp(s_loc - mx)
    p_ctx = jnp.exp(s_ctx - mx)
    den = p_loc.sum(axis=-1, keepdims=True) + p_ctx.sum(axis=-1, keepdims=True) + jnp.exp(sink - mx)
    p_loc = p_loc.astype(BF16)
    p_ctx = p_ctx.astype(BF16)
    rows_per_unit = group * half_q
    for unit, (j, part, vd, cvd) in enumerate(values):
        rows = slice(unit * rows_per_unit, (unit + 1) * rows_per_unit)
        o = (jnp.dot(p_loc[rows], vd, preferred_element_type=F32)
             + jnp.dot(p_ctx[rows], cvd, preferred_element_type=F32)) / den[rows]
        for t in range(group // 2):
            pair = (group * j) // 2 + t
            o_ref[pl.ds(part * half_q, half_q), pair * LANES:(pair + 1) * LANES] = jnp.where(
                lo, o[(2 * t) * half_q:(2 * t + 1) * half_q], o[(2 * t + 1) * half_q:(2 * t + 2) * half_q])


WIN_Q = 256


def _win_attention(p, cache_k, cache_v, sink):
    nblk = DEC_SEQ // WIN_Q
    side = WIN_Q // A_WINDOW
    nside = DEC_SEQ // A_WINDOW
    base = NP_TOK // WIN_Q
    side_base = NP_TOK // A_WINDOW
    kcol = A_Q // A_KV

    def main_spec(col):
        return pl.BlockSpec((WIN_Q, A_KV), lambda b, i, s: (base + b * nblk + i, col))

    def side_spec(col, off):
        return pl.BlockSpec((A_WINDOW, A_KV),
                            lambda b, i, s: (side_base + b * nside + jnp.clip(side * i + off, 0, nside - 1), col))

    ctx_spec = pl.BlockSpec((None, PAST, A_KV), lambda b, i, s: (b, 0, 0))
    return pl.pallas_call(
        _win_kernel,
        grid_spec=pltpu.PrefetchScalarGridSpec(
            num_scalar_prefetch=1,
            grid=(DEC_BATCH, nblk),
            in_specs=[pl.BlockSpec((WIN_Q, A_Q), lambda b, i, s: (base + b * nblk + i, 0)),
                      side_spec(kcol, -1), main_spec(kcol), side_spec(kcol, side),
                      side_spec(kcol + 1, -1), main_spec(kcol + 1), side_spec(kcol + 1, side),
                      ctx_spec, ctx_spec],
            out_specs=pl.BlockSpec((WIN_Q, A_Q), lambda b, i, s: (b * nblk + i, 0))),
        out_shape=jax.ShapeDtypeStruct((NS_TOK, A_Q), F32),
        compiler_params=_cparams(2),
        name="win_attn",
    )(sink, p, p, p, p, p, p, p, cache_k.reshape(DEC_BATCH, PAST, A_KV), cache_v.reshape(DEC_BATCH, PAST, A_KV))


def _ret_kernel(df_ref, db_ref, q_ref, k_ref, v_ref, g_ref, gn_ref, s0f_ref, s0b_ref,
                o_ref, sf_ref, sb_ref, of_scr, ob_scr, *, length):
    c_len = RET_CHUNK
    n = length // c_len
    lo = _lane_lo()
    hi = jnp.logical_not(lo)
    row = lax.broadcasted_iota(jnp.int32, (c_len, c_len), 0)
    col = lax.broadcasted_iota(jnp.int32, (c_len, c_len), 1)
    rowp = lax.broadcasted_iota(jnp.int32, (LANES, LANES), 0)
    colp = lax.broadcasted_iota(jnp.int32, (LANES, LANES), 1)
    blockdiag = (rowp < HD) == (colp < HD)
    idx = lax.broadcasted_iota(jnp.int32, (c_len, 1), 0).astype(F32)

    def direction(dec_ref, forward):
        lg = -jnp.exp(dec_ref[...])
        diff = (row - col) if forward else (col - row)
        keep = (diff >= 0) if forward else (diff > 0)
        dist = jnp.maximum(diff, 0).astype(F32)
        dm = jnp.concatenate([jnp.where(keep, jnp.exp(dist * lg[:, off:off + 1]), 0.0) for off in (0, HD)], axis=0)
        if forward:
            xi = jnp.exp((idx + 1.0) * lg)
            zeta = jnp.exp((c_len - 1.0 - idx) * lg)
        else:
            xi = jnp.exp((c_len - idx) * lg)
            zeta = jnp.exp(idx * lg)
        return dm, xi, zeta, jnp.exp(c_len * lg)

    def chunk(c, state, consts):
        dm, xi, zeta, gch = consts
        rows = pl.ds(pl.multiple_of(c * c_len, c_len), c_len)
        qc = q_ref[rows, :]
        kc = k_ref[rows, :] * HD ** -0.5
        vc = v_ref[rows, :].astype(BF16)
        kb = kc.astype(BF16)
        q2 = jnp.concatenate([jnp.where(lo, qc, 0.0), jnp.where(hi, qc, 0.0)], axis=0).astype(BF16)
        inner = lax.dot_general(q2, kb, NT_DIMS, preferred_element_type=F32) * dm
        kz_t = (kc * zeta).T
        res = jnp.dot(jnp.concatenate([inner, kz_t], axis=0).astype(BF16), vc, preferred_element_type=F32)
        cross = jnp.dot(qc.astype(BF16), state.astype(BF16), preferred_element_type=F32) * xi
        o = jnp.where(lo, res[:c_len], res[c_len:2 * c_len]) + cross
        state = gch * state + jnp.where(blockdiag, res[2 * c_len:], 0.0)
        return rows, o, state

    cf = direction(df_ref, True)
    cb = direction(db_ref, False)

    def scan_body(t, states):
        rows_f, o_f, state_f = chunk(t, states[0], cf)
        of_scr[rows_f, :] = o_f
        rows_b, o_b, state_b = chunk(n - 1 - t, states[1], cb)
        ob_scr[rows_b, :] = o_b
        return state_f, state_b

    state_f, state_b = lax.fori_loop(0, n, scan_body, (s0f_ref[...], s0b_ref[...]), unroll=min(n, RET_UNROLL))
    sf_ref[...] = state_f
    sb_ref[...] = state_b

    gn = gn_ref[...]
    norm_rows = min(RET_NORM_ROWS, length)
    n_norm = length // norm_rows

    def per_head(x):
        a = jnp.where(lo, x, 0.0).sum(axis=-1, keepdims=True)
        b = jnp.where(hi, x, 0.0).sum(axis=-1, keepdims=True)
        return jnp.where(lo, a, b) * (1.0 / HD)

    def norm_body(t, carry):
        rows = pl.ds(pl.multiple_of(t * norm_rows, norm_rows), norm_rows)
        o = of_scr[rows, :] + ob_scr[rows, :]
        d = o - per_head(o)
        y = d * lax.rsqrt(per_head(d * d) + EPS) * gn
        o_ref[rows, :] = _silu(g_ref[rows, :]) * y
        return carry

    lax.fori_loop(0, n_norm, norm_body, 0)


def _pair_lanes(v):
    return jnp.repeat(v.astype(F32), HD).reshape(B_HEADS // 2, 1, LANES)


def _blockdiag_states(s):
    b = s.shape[0]
    s = s.astype(F32).reshape(b, B_HEADS // 2, 2, HD, HD)
    z = jnp.zeros_like(s[:, :, 0])
    top = jnp.concatenate([s[:, :, 0], z], axis=-1)
    bot = jnp.concatenate([z, s[:, :, 1]], axis=-1)
    return jnp.concatenate([top, bot], axis=-2)


def _diag_states(sp):
    b = sp.shape[0]
    s = jnp.stack([sp[:, :, :HD, :HD], sp[:, :, HD:, HD:]], axis=2)
    return s.reshape(b, B_HEADS, HD, HD)


def _retention(p, row_base, batch, length, dec_f, dec_b, gn_g, s0f, s0b):
    npairs = B_HEADS // 2
    blk0 = row_base // length
    qcol = (A_Q + 2 * A_KV) // LANES

    def col_spec(off):
        return pl.BlockSpec((length, LANES), lambda b, h: (blk0 + b, qcol + off * npairs + h))

    lane_spec = pl.BlockSpec((None, 1, LANES), lambda b, h: (h, 0, 0))
    state_spec = pl.BlockSpec((None, None, LANES, LANES), lambda b, h: (b, h, 0, 0))
    state_shape = jax.ShapeDtypeStruct((batch, npairs, LANES, LANES), F32)
    return pl.pallas_call(
        functools.partial(_ret_kernel, length=length),
        grid=(batch, npairs),
        in_specs=[lane_spec, lane_spec, col_spec(0), col_spec(1), col_spec(2), col_spec(3), lane_spec,
                  state_spec, state_spec],
        out_specs=[pl.BlockSpec((length, LANES), lambda b, h: (b, h)), state_spec, state_spec],
        out_shape=[jax.ShapeDtypeStruct((batch * length, B_W), F32), state_shape, state_shape],
        scratch_shapes=[pltpu.VMEM((length, LANES), F32), pltpu.VMEM((length, LANES), F32)],
        compiler_params=_cparams(2),
        name="retention",
    )(_pair_lanes(dec_f), _pair_lanes(dec_b), p, p, p, p, gn_g.reshape(npairs, 1, LANES), s0f, s0b)


def _ctx_mha_kernel(q_ref, k_ref, v_ref, o_ref):
    lo = _lane_lo()
    for pair in range(C_HEADS // 2):
        cols = slice(pair * LANES, (pair + 1) * LANES)
        q = _stack_heads(q_ref, [2 * pair, 2 * pair + 1], lo, HD ** -0.5)
        s = lax.dot_general(q, k_ref[:, cols].astype(BF16), NT_DIMS, preferred_element_type=F32)
        o = _softmax_av([s], [v_ref[:, cols].astype(BF16)])
        o_ref[:, cols] = jnp.where(lo, o[:SEQ], o[SEQ:])


def _ctx_mha(p):
    return pl.pallas_call(
        _ctx_mha_kernel,
        grid=(BATCH,),
        in_specs=[pl.BlockSpec((SEQ, C_W), lambda b: (b, 0)),
                  pl.BlockSpec((SEQ, C_W), lambda b: (b, 1)),
                  pl.BlockSpec((SEQ, C_W), lambda b: (b, 2))],
        out_specs=pl.BlockSpec((SEQ, C_W), lambda b: (b, 0)),
        out_shape=jax.ShapeDtypeStruct((NP_TOK, C_W), F32),
        compiler_params=_cparams(1),
        name="ctx_mha",
    )(p, p, p)


NA_WIN_ROWS = 2 * NA_ROWS
NA_WIN = NA_WIN_ROWS * GRID_W
NA_QROWS = NA_ROWS * GRID_W
NA_PAD_ROWS = NA_KH // 2
NA_TABLE = 1536
NA_SPAN = 768


def _na_kernel(q_ref, kp_ref, km_ref, kn_ref, vp_ref, vm_ref, vn_ref, ck_ref, cv_ref, ue_ref, uo_ref, o_ref):
    r0 = pl.program_id(2) * NA_ROWS
    n_rows = DEC_SEQ // GRID_W
    lo = _lane_lo()
    k = jnp.concatenate([kp_ref[...], km_ref[...], kn_ref[...]], axis=0).astype(BF16)
    v = jnp.concatenate([vp_ref[...], vm_ref[...], vn_ref[...]], axis=0).astype(BF16)
    ck = ck_ref[...].astype(BF16)
    cv = cv_ref[...].astype(BF16)
    q = q_ref[...] * HD ** -0.5
    outs = []
    for half, keep in enumerate((lo, jnp.logical_not(lo))):
        qh = jnp.where(keep, q, 0.0).astype(BF16)
        s_ctx = lax.dot_general(qh, ck, NT_DIMS, preferred_element_type=F32)
        o_parts = []
        for part in range(2):
            key0 = part * (NA_WIN - NA_SPAN)
            klane = key0 + lax.broadcasted_iota(jnp.int32, (1, NA_SPAN), 1)
            qrows = slice(part * NA_QROWS // 2, (part + 1) * NA_QROWS // 2)
            s = lax.dot_general(qh[qrows], k[key0:key0 + NA_SPAN], NT_DIMS, preferred_element_type=F32)
            p_loc, p_ctx, den = [], [], []
            for t in range(NA_ROWS // 2):
                rq = part * NA_ROWS // 2 + t
                start = NA_KH - 1 - rq
                if start % 2 == 0:
                    u = ue_ref[half, :, start * GRID_W + key0:start * GRID_W + key0 + NA_SPAN]
                else:
                    u = uo_ref[half, :, (start - 1) * GRID_W + key0:(start - 1) * GRID_W + key0 + NA_SPAN]
                r = r0 + rq
                first = jnp.clip(r - NA_KH // 2, 0, n_rows - NA_KH)
                lane0 = (first - r0 + NA_PAD_ROWS) * GRID_W
                in_rows = (klane >= lane0) & (klane < lane0 + NA_KH * GRID_W)
                sl = jnp.where(in_rows, s[t * GRID_W:(t + 1) * GRID_W] + u, NEG)
                sc = s_ctx[rq * GRID_W:(rq + 1) * GRID_W]
                mx = jnp.maximum(sl.max(axis=-1, keepdims=True), sc.max(axis=-1, keepdims=True))
                el = jnp.exp(sl - mx)
                ec = jnp.exp(sc - mx)
                den.append(el.sum(axis=-1, keepdims=True) + ec.sum(axis=-1, keepdims=True))
                p_loc.append(el.astype(BF16))
                p_ctx.append(ec.astype(BF16))
            acc = (jnp.dot(jnp.concatenate(p_loc, axis=0), v[key0:key0 + NA_SPAN], preferred_element_type=F32)
                   + jnp.dot(jnp.concatenate(p_ctx, axis=0), cv, preferred_element_type=F32))
            o_parts.append(acc / jnp.concatenate(den, axis=0))
        outs.append(jnp.concatenate(o_parts, axis=0))
    o_ref[...] = jnp.where(lo, outs[0], outs[1])


def _na_bias_tables(rpb):
    cq = jnp.arange(GRID_W)
    ck = jnp.arange(GRID_W)
    dc = jnp.clip(ck[None] - cq[:, None], -(NA_KW - 1), NA_KW - 1) + NA_KW - 1
    cs = jnp.clip(cq - NA_KW // 2, 0, GRID_W - NA_KW)
    col_ok = (ck[None] >= cs[:, None]) & (ck[None] < cs[:, None] + NA_KW)
    t = rpb.astype(F32)[:, :, dc]
    t = jnp.where(col_ok[None, None], t, NEG).transpose(0, 2, 1, 3)
    n_dr = 2 * NA_KH - 1
    blocks = NA_TABLE // GRID_W
    t = jnp.pad(t, ((0, 0), (0, 0), (NA_PAD_ROWS, blocks - n_dr - NA_PAD_ROWS), (0, 0)), constant_values=NEG)
    ue = t.reshape(C_HEADS, GRID_W, NA_TABLE)
    uo = jnp.concatenate([ue[..., GRID_W:], jnp.full((C_HEADS, GRID_W, GRID_W), NEG, F32)], axis=-1)
    return ue, uo


def _na_attention(p, cache_k, cache_v, rpb):
    npairs = C_HEADS // 2
    nrb = DEC_SEQ // NA_QROWS
    half = NA_QROWS // 2
    qbase = NP_TOK // NA_QROWS
    hbase = NP_TOK // half
    kcol = C_W // LANES
    ue, uo = _na_bias_tables(rpb)

    def main_spec(col0):
        return pl.BlockSpec((NA_QROWS, LANES), lambda b, h, r: (qbase + b * nrb + r, col0 + h))

    def side_spec(col0, off):
        return pl.BlockSpec((half, LANES),
                            lambda b, h, r: (hbase + b * 2 * nrb + jnp.clip(2 * r + off, 0, 2 * nrb - 1), col0 + h))

    ctx_spec = pl.BlockSpec((None, PAST, LANES), lambda b, h, r: (b, 0, h))
    tab_spec = pl.BlockSpec((2, GRID_W, NA_TABLE), lambda b, h, r: (h, 0, 0))
    return pl.pallas_call(
        _na_kernel,
        grid=(DEC_BATCH, npairs, nrb),
        in_specs=[main_spec(0),
                  side_spec(kcol, -1), main_spec(kcol), side_spec(kcol, 2),
                  side_spec(2 * kcol, -1), main_spec(2 * kcol), side_spec(2 * kcol, 2),
                  ctx_spec, ctx_spec, tab_spec, tab_spec],
        out_specs=pl.BlockSpec((NA_QROWS, LANES), lambda b, h, r: (b * nrb + r, h)),
        out_shape=jax.ShapeDtypeStruct((NS_TOK, C_W), F32),
        compiler_params=_cparams(3),
        name="na_attn",
    )(p, p, p, p, p, p, p, cache_k.reshape(DEC_BATCH, PAST, C_W), cache_v.reshape(DEC_BATCH, PAST, C_W), ue, uo)


def _route(biased, scores):
    t = biased.shape[1]
    per_group = N_EXPERTS // N_GROUPS
    i8 = lax.broadcasted_iota(jnp.int32, (per_group, t), 0)
    g_rows = []
    for g in range(N_GROUPS):
        bg = biased[g * per_group:(g + 1) * per_group]
        m1 = bg.max(axis=0, keepdims=True)
        first = jnp.where(bg == m1, i8, per_group).min(axis=0, keepdims=True)
        m2 = jnp.where(i8 == first, -jnp.inf, bg).max(axis=0, keepdims=True)
        g_rows.append(m1 + m2)
    g_top = jnp.concatenate(g_rows, axis=0)
    gi = lax.broadcasted_iota(jnp.int32, g_top.shape, 0)
    g_sel = jnp.zeros(g_top.shape, jnp.int32)
    cur = g_top
    for _ in range(TOPK_GROUPS):
        m = cur.max(axis=0, keepdims=True)
        hit = gi == jnp.where(cur == m, gi, N_GROUPS).min(axis=0, keepdims=True)
        g_sel = jnp.where(hit, 1, g_sel)
        cur = jnp.where(hit, -jnp.inf, cur)
    e_sel = jnp.concatenate([jnp.broadcast_to(g_sel[g:g + 1], (per_group, t)) for g in range(N_GROUPS)], axis=0)
    cur = jnp.where(e_sel > 0, biased, NEG)
    ei = lax.broadcasted_iota(jnp.int32, cur.shape, 0)
    ids, gates, hits = [], [], []
    for _ in range(TOP_K):
        m = cur.max(axis=0, keepdims=True)
        f = jnp.where(cur == m, ei, N_EXPERTS).min(axis=0, keepdims=True)
        hit = ei == f
        ids.append(f)
        hits.append(hit)
        gates.append(jnp.where(hit, scores, 0.0).sum(axis=0, keepdims=True))
        cur = jnp.where(hit, -jnp.inf, cur)
    gate = jnp.concatenate(gates, axis=0)
    gate = gate / gate.sum(axis=0, keepdims=True) * ROUTED_SCALE
    return jnp.concatenate(ids, axis=0), gate, hits


def _pack_bf16_pairs(h):
    bits = lax.bitcast_convert_type(h.astype(BF16).astype(F32), jnp.uint32)
    return bits[:, :D // 2] | (bits[:, D // 2:] >> 16)


def _unpack_bf16_pairs(xp):
    hi = lax.bitcast_convert_type(xp & jnp.uint32(0xFFFF0000), F32).astype(BF16)
    lo = lax.bitcast_convert_type(xp << 16, F32).astype(BF16)
    return hi, lo


def _dot_halves(hi, lo, w_ref):
    return (jnp.dot(hi, w_ref[:D // 2, :], preferred_element_type=F32)
            + jnp.dot(lo, w_ref[D // 2:, :], preferred_element_type=F32))


def _outproj_kernel(*refs, n_parts):
    xp_ref, xs_ref = refs[:2]
    part_refs = refs[2:2 + 3 * n_parts]
    gate_ref, shift_ref, scale_ref, g2_ref, rw_ref, rb_ref = refs[2 + 3 * n_parts:8 + 3 * n_parts]
    xo_ref, h_ref, dest_ref, wgt_ref, plan_ref, cnt_ref = refs[8 + 3 * n_parts:]
    step = pl.program_id(0)

    @pl.when(step == 0)
    def _():
        cnt_ref[...] = jnp.zeros_like(cnt_ref)
        plan_ref[...] = jnp.zeros_like(plan_ref)

    y = None
    for t in range(n_parts):
        ap_ref, as_ref, w_ref = part_refs[3 * t:3 * t + 3]
        d = jnp.dot(_pick_rows(ap_ref, as_ref, TM_OUT).astype(BF16), w_ref[...], preferred_element_type=F32)
        y = d if y is None else y + d
    x = _pick_rows(xp_ref, xs_ref, TM_OUT) + gate_ref[...] * y
    xo_ref[...] = x
    h = _rms(x, g2_ref[...]) * (1.0 + scale_ref[...]) + shift_ref[...]
    h_ref[...] = _pack_bf16_pairs(h)
    h_hi = h.astype(BF16)
    h_lo = (h - h_hi.astype(F32)).astype(BF16)
    rw = rw_ref[...]
    rw_hi = rw.astype(BF16)
    rw_lo = (rw - rw_hi.astype(F32)).astype(BF16)
    logits = (lax.dot_general(rw_hi, h_hi, NT_DIMS, preferred_element_type=F32)
              + lax.dot_general(rw_hi, h_lo, NT_DIMS, preferred_element_type=F32)
              + lax.dot_general(rw_lo, h_hi, NT_DIMS, preferred_element_type=F32))
    scores = jax.nn.sigmoid(logits)
    _, gate, hits = _route(scores + rb_ref[...], scores)
    wgt_ref[...] = gate
    chosen = hits[0]
    for hit in hits[1:]:
        chosen = chosen | hit
    m = jnp.where(chosen, 1.0, 0.0)
    before = (lax.broadcasted_iota(jnp.int32, (TM_OUT, TM_OUT), 0)
              < lax.broadcasted_iota(jnp.int32, (TM_OUT, TM_OUT), 1))
    prefix = jnp.dot(m.astype(BF16), jnp.where(before, 1.0, 0.0).astype(BF16), preferred_element_type=F32)
    e_base = (lax.broadcasted_iota(jnp.int32, (N_EXPERTS, 1), 0) * N_TOK).astype(F32)
    row_all = prefix + (cnt_ref[...] + e_base)
    dest_ref[...] = jnp.concatenate(
        [jnp.where(hit, row_all, 0.0).sum(axis=0, keepdims=True) for hit in hits], axis=0).astype(jnp.int32)
    cnt_ref[...] += m.sum(axis=1, keepdims=True)

    @pl.when(step == pl.num_programs(0) - 1)
    def _():
        _block_plan(cnt_ref[...], plan_ref)


def _block_plan(counts, plan_ref):
    cap_blocks = N_TOK // MOE_BLOCK
    nblk = ((counts.astype(jnp.int32) + (MOE_BLOCK - 1)) // MOE_BLOCK).astype(F32)
    lower = (lax.broadcasted_iota(jnp.int32, (N_EXPERTS, N_EXPERTS), 0)
             >= lax.broadcasted_iota(jnp.int32, (N_EXPERTS, N_EXPERTS), 1))
    cum = jnp.dot(jnp.where(lower, 1.0, 0.0).astype(BF16), jnp.broadcast_to(nblk, (N_EXPERTS, LANES)).astype(BF16),
                  preferred_element_type=F32)[:, :1]
    n_used = cum[N_EXPERTS - 1:, :]
    slot = jnp.minimum(lax.broadcasted_iota(jnp.int32, (1, PLAN_LANES), 1).astype(F32), n_used - 1.0)
    done = cum <= slot
    expert = jnp.where(done, 1.0, 0.0).sum(axis=0, keepdims=True)
    blocks_before = jnp.where(done, nblk, 0.0).sum(axis=0, keepdims=True)
    plan_ref[0:1, :] = (expert * cap_blocks + (slot - blocks_before)).astype(jnp.int32)
    plan_ref[1:2, :] = expert.astype(jnp.int32)
    plan_ref[2:3, :] = jnp.broadcast_to(n_used, (1, PLAN_LANES)).astype(jnp.int32)
    run_end = jnp.where(cum > slot, cum, jnp.inf).min(axis=0, keepdims=True)
    plan_ref[3:4, :] = jnp.minimum(jnp.where(cum <= run_end, 1.0, 0.0).sum(axis=0, keepdims=True),
                                   N_EXPERTS - 1.0).astype(jnp.int32)
    plan_ref[4:5, :] = jnp.where(run_end < n_used, 1, 0).astype(jnp.int32)


def _outproj(x, parts, mod, g2, router_w, router_b):
    in_specs = _pair_specs(D, TM_OUT)
    args = [x[0], x[1]]
    for ap, a_s, w in parts:
        width = ap.shape[1]
        in_specs += _pair_specs(width, TM_OUT) + [pl.BlockSpec((width, D), lambda i: (0, 0))]
        args += [ap, a_s, w]
    in_specs += [_mod_spec(2, TM_OUT), _mod_spec(3, TM_OUT), _mod_spec(4, TM_OUT),
                 pl.BlockSpec((1, D), lambda i: (0, 0)),
                 pl.BlockSpec((N_EXPERTS, D), lambda i: (0, 0)),
                 pl.BlockSpec((N_EXPERTS, 1), lambda i: (0, 0))]
    args += [mod, mod, mod, g2.reshape(1, D), router_w.T, router_b.reshape(N_EXPERTS, 1)]
    return pl.pallas_call(
        functools.partial(_outproj_kernel, n_parts=len(parts)),
        grid=(N_TOK // TM_OUT,),
        in_specs=in_specs,
        out_specs=[pl.BlockSpec((TM_OUT, D), lambda i: (i, 0)),
                   pl.BlockSpec((TM_OUT, D // 2), lambda i: (i, 0)),
                   pl.BlockSpec((TOP_K, TM_OUT), lambda i: (0, i)),
                   pl.BlockSpec((TOP_K, TM_OUT), lambda i: (0, i)),
                   pl.BlockSpec((8, PLAN_LANES), lambda i: (0, 0))],
        out_shape=[jax.ShapeDtypeStruct((N_TOK, D), F32),
                   jax.ShapeDtypeStruct((N_TOK, D // 2), jnp.uint32),
                   jax.ShapeDtypeStruct((TOP_K, N_TOK), jnp.int32),
                   jax.ShapeDtypeStruct((TOP_K, N_TOK), F32),
                   jax.ShapeDtypeStruct((8, PLAN_LANES), jnp.int32)],
        scratch_shapes=[pltpu.VMEM((N_EXPERTS, 1), F32)],
        compiler_params=_cparams(1),
        name="outproj_router",
    )(*args)


def _experts_kernel(br_ref, be_ref, nu_ref, ne_ref, hn_ref, x_hbm, w1_hbm, w3_hbm, w2_hbm, o_ref,
                    w1b, w3b, w2b, w1f, w3f, w2f, xbuf, xsem, wsem, *, layer):
    i = pl.program_id(0)
    n_steps = pl.num_programs(0)
    e = be_ref[i]
    prev = be_ref[jnp.maximum(i - 1, 0)]

    def x_copy(step):
        slot = step % EXPERT_X_SLOTS
        rows = pl.ds(pl.multiple_of(br_ref[step] * MOE_BLOCK, MOE_BLOCK), MOE_BLOCK)
        return pltpu.make_async_copy(x_hbm.at[rows], xbuf.at[slot], xsem.at[slot])

    @pl.when(i == 0)
    def _():
        x_copy(0).start()
        x_copy(1).start()

    @pl.when(i + 2 < n_steps)
    def _():
        x_copy(i + 2).start()

    x_copy(i).wait()

    def w_copies(expert):
        return [pltpu.make_async_copy(src.at[layer, expert], dst, wsem.at[k])
                for k, (src, dst) in enumerate(((w1_hbm, w1f), (w3_hbm, w3f), (w2_hbm, w2f)))]

    @pl.when(i == 0)
    def _():
        for c in w_copies(e):
            c.start()

    @pl.when((i == 0) | (e != prev))
    def _():
        for c in w_copies(e):
            c.wait()
        w1b[...] = w1f[...].astype(BF16)
        w3b[...] = w3f[...].astype(BF16)
        w2b[...] = w2f[...].astype(BF16)

        @pl.when(hn_ref[i] > 0)
        def _():
            for c in w_copies(ne_ref[i]):
                c.start()

    @pl.when(i < nu_ref[0])
    def _():
        hi, lo = _unpack_bf16_pairs(xbuf[i % EXPERT_X_SLOTS])
        a = _dot_halves(hi, lo, w1b)
        b = _dot_halves(hi, lo, w3b)
        h = (_silu(a) * b).astype(BF16)
        o_ref[...] = _pack_bf16_pairs(jnp.dot(h, w2b[...], preferred_element_type=F32))


def _experts(plan, x_rows, w1, w3, w2, layer):
    any_spec = pl.BlockSpec(memory_space=pl.ANY)
    return pl.pallas_call(
        functools.partial(_experts_kernel, layer=layer),
        grid_spec=pltpu.PrefetchScalarGridSpec(
            num_scalar_prefetch=5,
            grid=(N_MOE_BLOCKS,),
            in_specs=[any_spec, any_spec, any_spec, any_spec],
            out_specs=pl.BlockSpec((MOE_BLOCK, D // 2), lambda i, br, be, nu, ne, hn: (br[i], 0)),
            scratch_shapes=[pltpu.VMEM((D, FF), BF16), pltpu.VMEM((D, FF), BF16), pltpu.VMEM((FF, D), BF16),
                            pltpu.VMEM((D, FF), F32), pltpu.VMEM((D, FF), F32), pltpu.VMEM((FF, D), F32),
                            pltpu.VMEM((EXPERT_X_SLOTS, MOE_BLOCK, D // 2), jnp.uint32),
                            pltpu.SemaphoreType.DMA((EXPERT_X_SLOTS,)),
                            pltpu.SemaphoreType.DMA((3,))]),
        out_shape=jax.ShapeDtypeStruct(x_rows.shape, jnp.uint32),
        compiler_params=_cparams(1),
        name="experts",
    )(plan[0], plan[1], plan[2, :1], plan[3], plan[4], x_rows, w1, w3, w2)


SC_CORES = 2
SC_SUBCORES = 16
SC_WORKERS = SC_CORES * SC_SUBCORES
SC_CHUNK_BYTES = 64 * 1024
SC_SLOTS = 4


def _sc_scatter(rows, dest, n_out):
    n_rows, width = rows.shape
    picks = dest.shape[0]
    chunk = SC_CHUNK_BYTES // (4 * width)
    per_worker = n_rows // SC_WORKERS
    n_chunks = per_worker // chunk
    assert per_worker * SC_WORKERS == n_rows and n_chunks * chunk == per_worker and n_chunks % 2 == 0
    mesh = plsc.VectorSubcoreMesh(core_axis_name="c", subcore_axis_name="s")

    @functools.partial(
        pl.kernel, mesh=mesh,
        out_type=jax.ShapeDtypeStruct((n_out, width), rows.dtype),
        scratch_types=[pltpu.VMEM((picks, n_chunks, chunk), jnp.int32),
                       pltpu.VMEM((2, chunk, width), rows.dtype),
                       pltpu.SemaphoreType.DMA((2,)),
                       pltpu.SemaphoreType.DMA((2,))])
    def scatter(r_hbm, d_hbm, o_hbm, idx_v, rows_v, lsem, ssem):
        worker = lax.axis_index("s") * SC_CORES + lax.axis_index("c")
        base = worker * per_worker
        for k in range(picks):
            pltpu.sync_copy(d_hbm.at[k, worker], idx_v.at[k])

        def load_copy(c, b):
            src = pl.ds(pl.multiple_of(base + c * chunk, chunk), chunk)
            return pltpu.make_async_copy(r_hbm.at[src], rows_v.at[b], lsem.at[b])

        def store_copy(c, b, k):
            return pltpu.make_async_copy(rows_v.at[b], o_hbm.at[idx_v.at[k, c]], ssem.at[b])

        load_copy(0, 0).start()

        @pl.loop(0, n_chunks, step=2)
        def _(c0):
            for b in range(2):
                c = c0 + b
                load_copy(c, b).wait()
                for k in range(picks):
                    store_copy(c, b, k).start()

                @pl.when(c > 0)
                def _():
                    for k in range(picks):
                        store_copy(c - 1, 1 - b, k).wait()

                @pl.when(c + 1 < n_chunks)
                def _():
                    load_copy(c + 1, 1 - b).start()

        for k in range(picks):
            store_copy(n_chunks - 1, 1, k).wait()

    return scatter(rows, dest.reshape(picks, SC_WORKERS, n_chunks, chunk))


def _sc_gather(table, idx):
    n_idx = idx.shape[0]
    width = table.shape[1]
    chunk = SC_CHUNK_BYTES // (4 * width)
    per_worker = n_idx // SC_WORKERS
    n_chunks = per_worker // chunk
    ahead = SC_SLOTS - 1
    assert per_worker * SC_WORKERS == n_idx and n_chunks * chunk == per_worker and n_chunks % SC_SLOTS == 0
    mesh = plsc.VectorSubcoreMesh(core_axis_name="c", subcore_axis_name="s")

    @functools.partial(
        pl.kernel, mesh=mesh,
        out_type=jax.ShapeDtypeStruct((n_idx, width), table.dtype),
        scratch_types=[pltpu.VMEM((per_worker,), jnp.int32),
                       pltpu.VMEM((SC_SLOTS, chunk, width), table.dtype),
                       pltpu.SemaphoreType.DMA((SC_SLOTS,)),
                       pltpu.SemaphoreType.DMA((SC_SLOTS,))])
    def gather(t_hbm, i_hbm, o_hbm, idx_v, rows_v, gsem, wsem):
        worker = lax.axis_index("s") * SC_CORES + lax.axis_index("c")
        base = worker * per_worker
        pltpu.sync_copy(i_hbm.at[pl.ds(pl.multiple_of(base, chunk), per_worker)], idx_v)

        def gather_copy(c, b):
            ids = idx_v.at[pl.ds(pl.multiple_of(c * chunk, chunk), chunk)]
            return pltpu.make_async_copy(t_hbm.at[ids], rows_v.at[b], gsem.at[b])

        def write_copy(c, b):
            rows = pl.ds(pl.multiple_of(base + c * chunk, chunk), chunk)
            return pltpu.make_async_copy(rows_v.at[b], o_hbm.at[rows], wsem.at[b])

        for c in range(ahead):
            gather_copy(c, c).start()

        @pl.loop(0, n_chunks, step=SC_SLOTS)
        def _(c0):
            for b in range(SC_SLOTS):
                c = c0 + b
                refill = (b + ahead) % SC_SLOTS
                gather_copy(c, b).wait()
                write_copy(c, b).start()

                @pl.when(c > 0)
                def _():
                    write_copy(c - 1, refill).wait()

                @pl.when(c + ahead < n_chunks)
                def _():
                    gather_copy(c + ahead, refill).start()

        write_copy(n_chunks - 1, (n_chunks - 1) % SC_SLOTS).wait()

    return gather(table, idx)


TC = 512


def _combine_kernel(x_ref, h_ref, y_ref, wgt_ref, gate_ref, w1_ref, w3_ref, w2_ref, fg_ref, o_ref, *, final):
    hi, lo = _unpack_bf16_pairs(h_ref[...])
    a = _dot_halves(hi, lo, w1_ref)
    b = _dot_halves(hi, lo, w3_ref)
    ffn = jnp.dot((_silu(a) * b).astype(BF16), w2_ref[...], preferred_element_type=F32)
    wgt = wgt_ref[...]
    r_hi = None
    r_lo = None
    for k in range(TOP_K):
        yk = y_ref[k]
        w = wgt[:, k:k + 1]
        t_hi = lax.bitcast_convert_type(yk & jnp.uint32(0xFFFF0000), F32) * w
        t_lo = lax.bitcast_convert_type(yk << 16, F32) * w
        r_hi = t_hi if r_hi is None else r_hi + t_hi
        r_lo = t_lo if r_lo is None else r_lo + t_lo
    x = x_ref[...] + gate_ref[...] * (ffn + jnp.concatenate([r_hi, r_lo], axis=1))
    o_ref[...] = _rms(x, fg_ref[...]) if final else x


def _combine(x, h, y_rows, wgt, mod, sw1, sw3, sw2, final_g, final):
    weights = (sw1.astype(BF16), sw3.astype(BF16), sw2.astype(BF16), final_g.reshape(1, D))

    def rows_from(first_row, n_rows, y_group):
        b0 = first_row // TC
        return pl.pallas_call(
            functools.partial(_combine_kernel, final=final),
            grid=(n_rows // TC,),
            in_specs=[pl.BlockSpec((TC, D), lambda i: (i + b0, 0)),
                      pl.BlockSpec((TC, D // 2), lambda i: (i + b0, 0)),
                      pl.BlockSpec((TOP_K, TC, D // 2), lambda i: (0, i, 0)),
                      pl.BlockSpec((TC, TOP_K), lambda i: (i + b0, 0)),
                      _mod_spec(5, TC, b0),
                      pl.BlockSpec((D, FF), lambda i: (0, 0)),
                      pl.BlockSpec((D, FF), lambda i: (0, 0)),
                      pl.BlockSpec((FF, D), lambda i: (0, 0)),
                      pl.BlockSpec((1, D), lambda i: (0, 0))],
            out_specs=pl.BlockSpec((TC, D), lambda i: (i, 0)),
            out_shape=jax.ShapeDtypeStruct((n_rows, D), F32),
            compiler_params=_cparams(1),
            name="combine",
        )(x, h, y_group, wgt, mod, *weights)

    return rows_from(0, NP_TOK, y_rows[0]), rows_from(NP_TOK, NS_TOK, y_rows[1])


def kernel(x_prompt, x_sample, cache_a_k, cache_a_v, state_ret_fwd, state_ret_bwd, cache_c_k, cache_c_v,
           c, c_ctx, norm1_g, norm2_g, ada_w, ada_b, even_w_in, even_w_out, sink_a, ret_decay_fwd,
           ret_decay_bwd, ret_gn_g, odd_w_in, odd_w_out, na_rpb, router_w, router_b, exp_w1, exp_w3,
           exp_w2, sh_w1, sh_w3, sh_w2, final_g):
    x = (x_prompt.reshape(NP_TOK, D), x_sample.reshape(NS_TOK, D))
    cc = jnp.concatenate([c_ctx[None], c, jnp.zeros((8 - 1 - DEC_BATCH, D), F32)], axis=0)
    rope = _rope_tables()
    outs = {}
    for l in range(2):
        mod = _ada(cc, ada_w, ada_b, l)
        if l == 0:
            p = _inproj(x, norm1_g[l], mod, even_w_in[0].astype(BF16), rope, A_Q + A_KV)
            oa_p = _ctx_gqa(p, sink_a[0])
            oa_s = _win_attention(p, cache_a_k[:, 0], cache_a_v[:, 0], sink_a[0])
            zero = jnp.zeros((BATCH, B_HEADS // 2, LANES, LANES), F32)
            ob_p, sf, sb = _retention(p, 0, BATCH, SEQ, ret_decay_fwd[0], ret_decay_bwd[0], ret_gn_g[0], zero, zero)
            ob_s, _, _ = _retention(p, NP_TOK, DEC_BATCH, DEC_SEQ, ret_decay_fwd[0], ret_decay_bwd[0], ret_gn_g[0],
                                    _blockdiag_states(state_ret_fwd[:, 0]), _blockdiag_states(state_ret_bwd[:, 0]))
            w_out = even_w_out[0].astype(BF16)
            parts = [(oa_p, oa_s, w_out[:A_Q]), (ob_p, ob_s, w_out[A_Q:])]
            outs["a_k"] = p[:NP_TOK, A_Q:A_Q + A_KV].reshape(BATCH, 1, SEQ, A_KV_HEADS, HD)
            outs["a_v"] = p[:NP_TOK, A_Q + A_KV:A_Q + 2 * A_KV].reshape(BATCH, 1, SEQ, A_KV_HEADS, HD)
            outs["r_f"] = _diag_states(sf).reshape(BATCH, 1, B_HEADS, HD, HD)
            outs["r_b"] = _diag_states(sb).reshape(BATCH, 1, B_HEADS, HD, HD)
        else:
            p = _inproj(x, norm1_g[l], mod, odd_w_in[0].astype(BF16), rope, 0)
            o_p = _ctx_mha(p)
            o_s = _na_attention(p, cache_c_k[:, 0], cache_c_v[:, 0], na_rpb[0])
            parts = [(o_p, o_s, odd_w_out[0].astype(BF16))]
            outs["c_k"] = p[:NP_TOK, C_W:2 * C_W].reshape(BATCH, 1, SEQ, C_HEADS, HD)
            outs["c_v"] = p[:NP_TOK, 2 * C_W:3 * C_W].reshape(BATCH, 1, SEQ, C_HEADS, HD)
        x_mid, h, dest, gate_t, plan = _outproj(x, parts, mod, norm2_g[l], router_w[l], router_b[l])
        y = _experts(plan, _sc_scatter(h, dest, N_EXPERTS * N_TOK), exp_w1, exp_w3, exp_w2, l)
        y_rows = (_sc_gather(y, dest[:, :NP_TOK].reshape(TOP_K * NP_TOK)).reshape(TOP_K, NP_TOK, D // 2),
                  _sc_gather(y, dest[:, NP_TOK:].reshape(TOP_K * NS_TOK)).reshape(TOP_K, NS_TOK, D // 2))
        x = _combine(x_mid, h, y_rows, gate_t.T, mod, sh_w1[l], sh_w3[l], sh_w2[l], final_g, final=(l == 1))
    y_prompt = x[0].reshape(BATCH, SEQ, D)
    y_sample = x[1].reshape(DEC_BATCH, DEC_SEQ, D)
    return (y_prompt, y_sample, outs["a_k"], outs["a_v"], outs["r_f"], outs["r_b"], outs["c_k"], outs["c_v"])
```

```python
import functools

import jax
import jax.numpy as jnp
from jax import lax
from jax.experimental import pallas as pl
from jax.experimental.pallas import tpu as pltpu
from jax.experimental.pallas import tpu_sc as plsc

F32 = jnp.float32
BF16 = jnp.bfloat16
HIGHEST = lax.Precision.HIGHEST

D = 1024
BATCH = 32
SEQ = 256
DEC_BATCH = 4
DEC_SEQ = 4096
PAST = 256
GRID_W = 64
HD = 64
EPS = 1e-6
NEG = -1e30
ROPE_BASE = 10000.0
A_HEADS = 8
A_KV_HEADS = 2
A_Q = A_HEADS * HD
A_KV = A_KV_HEADS * HD
B_HEADS = 8
B_W = B_HEADS * HD
EVEN_IN = A_Q + 2 * A_KV + 4 * B_W
C_HEADS = 16
C_W = C_HEADS * HD
NA_KH = 8
NA_KW = 16
N_EXPERTS = 64
TOP_K = 8
N_GROUPS = 8
TOPK_GROUPS = 4
FF = 256
ROUTED_SCALE = 2.5
MOE_BLOCK = 1024
EXPERT_X_SLOTS = 3
RET_CHUNK = 256
RET_UNROLL = 4
RET_NORM_ROWS = 1024
A_WINDOW = 128

NP_TOK = BATCH * SEQ
NS_TOK = DEC_BATCH * DEC_SEQ
N_TOK = NP_TOK + NS_TOK
N_ASSIGN = N_TOK * TOP_K
N_MOE_BLOCKS = (N_ASSIGN + N_EXPERTS * (MOE_BLOCK - 1) + MOE_BLOCK - 1) // MOE_BLOCK
PLAN_LANES = 512
assert N_TOK % MOE_BLOCK == 0 and N_MOE_BLOCKS <= PLAN_LANES

LANES = 128
TM = 512
TM_OUT = 1024
NA_ROWS = 8
V7X_VMEM_LIMIT = 56 * 1024 * 1024

NT_DIMS = (((1,), (1,)), ((), ()))


def _cparams(n_axes, vmem=V7X_VMEM_LIMIT):
    return pltpu.CompilerParams(dimension_semantics=("arbitrary",) * n_axes, vmem_limit_bytes=vmem)


def _seg_of_block(i, rows):
    row0 = i * rows
    return jnp.where(row0 < NP_TOK, 0, 1 + (row0 - NP_TOK) // DEC_SEQ)


def _mod_spec(chunk, rows=TM, first_block=0):
    return pl.BlockSpec((None, 1, D), lambda i: (_seg_of_block(i + first_block, rows), 0, chunk))


def _pair_specs(width, rows=TM):
    npb = NP_TOK // rows
    nsb = NS_TOK // rows
    return [pl.BlockSpec((rows, width), lambda i: (jnp.minimum(i, npb - 1), 0)),
            pl.BlockSpec((rows, width), lambda i: (jnp.clip(i - npb, 0, nsb - 1), 0))]


def _pick_rows(p_ref, s_ref, rows=TM):
    return jnp.where(pl.program_id(0) < NP_TOK // rows, p_ref[...], s_ref[...])


def _silu(x):
    return x * jax.nn.sigmoid(x)


def _rms(x, g):
    return x * lax.rsqrt(jnp.mean(x * x, axis=-1, keepdims=True) + EPS) * g


def _lane_lo():
    return lax.broadcasted_iota(jnp.int32, (1, LANES), 1) < HD


def _ada_kernel(c_ref, w_ref, b_ref, o_ref):
    a = _silu(c_ref[...])
    o_ref[...] = jnp.dot(a, w_ref[...], preferred_element_type=F32, precision=HIGHEST) + b_ref[...]


def _ada(cc, w, b, layer):
    tn = 1536
    out = pl.pallas_call(
        _ada_kernel,
        grid=(6 * D // tn,),
        in_specs=[pl.BlockSpec((8, D), lambda j: (0, 0)),
                  pl.BlockSpec((None, D, tn), lambda j: (layer, 0, j)),
                  pl.BlockSpec((None, 1, tn), lambda j: (layer, 0, j))],
        out_specs=pl.BlockSpec((8, tn), lambda j: (0, j)),
        out_shape=jax.ShapeDtypeStruct((8, 6 * D), F32),
        compiler_params=_cparams(1),
        name="ada",
    )(cc, w, b.reshape(b.shape[0], 1, 6 * D))
    return out.reshape(8, 1, 6 * D)


def _inproj_kernel(xp_ref, xs_ref, g_ref, shift_ref, scale_ref, w_ref, cos_ref, sin_ref, o_ref, *, rope_cols):
    h = _rms(_pick_rows(xp_ref, xs_ref), g_ref[...]) * (1.0 + scale_ref[...]) + shift_ref[...]
    o = jnp.dot(h.astype(BF16), w_ref[...], preferred_element_type=F32)
    if rope_cols:
        cos = cos_ref[...]
        sin = sin_ref[...]
        lane = lax.broadcasted_iota(jnp.int32, (1, LANES), 1)
        first = (lane % 32) < 16
        for c in range(rope_cols // LANES):
            oc = o[:, c * LANES:(c + 1) * LANES]
            partner = jnp.where(first, pltpu.roll(oc, LANES - 16, 1), pltpu.roll(oc, 16, 1))
            o_ref[:, c * LANES:(c + 1) * LANES] = oc * cos + partner * sin
        o_ref[:, rope_cols:] = o[:, rope_cols:]
    else:
        o_ref[...] = o


def _rope_tables():
    half = HD // 2
    inv = ROPE_BASE ** (-jnp.arange(0, half, 2, dtype=F32) / half)
    t = jnp.arange(DEC_SEQ)
    ang_r = (t // GRID_W).astype(F32)[:, None] * inv[None]
    ang_c = (t % GRID_W).astype(F32)[:, None] * inv[None]

    def head(fn_r, fn_c, sign):
        return jnp.concatenate([sign[0] * fn_r, sign[1] * fn_r, sign[0] * fn_c, sign[1] * fn_c], axis=-1)

    cos = head(jnp.cos(ang_r), jnp.cos(ang_c), (1.0, 1.0))
    sin = head(jnp.sin(ang_r), jnp.sin(ang_c), (-1.0, 1.0))
    cos = jnp.concatenate([jnp.ones((TM, HD), F32), cos], axis=0)
    sin = jnp.concatenate([jnp.zeros((TM, HD), F32), sin], axis=0)
    return jnp.tile(cos, (1, 2)), jnp.tile(sin, (1, 2))


def _inproj(x, g, mod, w_bf16, rope, rope_cols):
    n_out = w_bf16.shape[1]
    npb = NP_TOK // TM
    spb = DEC_SEQ // TM

    def rope_map(i):
        return (jnp.where(i < npb, 0, 1 + (i - npb) % spb), 0)

    return pl.pallas_call(
        functools.partial(_inproj_kernel, rope_cols=rope_cols),
        grid=(N_TOK // TM,),
        in_specs=_pair_specs(D) + [
                  pl.BlockSpec((1, D), lambda i: (0, 0)),
                  _mod_spec(0), _mod_spec(1),
                  pl.BlockSpec((D, n_out), lambda i: (0, 0)),
                  pl.BlockSpec((TM, LANES), rope_map),
                  pl.BlockSpec((TM, LANES), rope_map)],
        out_specs=pl.BlockSpec((TM, n_out), lambda i: (i, 0)),
        out_shape=jax.ShapeDtypeStruct((N_TOK, n_out), F32),
        compiler_params=_cparams(1),
        name="inproj",
    )(x[0], x[1], g.reshape(1, D), mod, mod, w_bf16, rope[0], rope[1])


def _softmax_av(s_list, v_list, sink=None):
    mx = s_list[0].max(axis=-1, keepdims=True)
    for s in s_list[1:]:
        mx = jnp.maximum(mx, s.max(axis=-1, keepdims=True))
    if sink is not None:
        mx = jnp.maximum(mx, sink)
    den = jnp.exp(sink - mx) if sink is not None else 0.0
    acc = None
    for s, v in zip(s_list, v_list):
        p = jnp.exp(s - mx)
        den = den + p.sum(axis=-1, keepdims=True)
        pv = jnp.dot(p.astype(BF16), v, preferred_element_type=F32)
        acc = pv if acc is None else acc + pv
    return acc / den


def _dup_half(x, j, lo):
    xr = pltpu.roll(x, HD, 1)
    return jnp.where(lo, x, xr) if j == 0 else jnp.where(lo, xr, x)


def _stack_heads(q_ref, heads, lo, scale, rows=slice(None)):
    parts = []
    for h in heads:
        qp = q_ref[rows, (h // 2) * LANES:(h // 2 + 1) * LANES]
        keep = lo if h % 2 == 0 else jnp.logical_not(lo)
        parts.append(jnp.where(keep, qp, 0.0) * scale)
    return jnp.concatenate(parts, axis=0).astype(BF16)


def _sink_column(sink_ref, heads, rows):
    return jnp.concatenate([jnp.full((rows, 1), sink_ref[h], F32) for h in heads], axis=0)


def _ctx_gqa_kernel(sink_ref, q_ref, k_ref, v_ref, o_ref):
    lo = _lane_lo()
    k = k_ref[...]
    v = v_ref[...]
    group = A_HEADS // A_KV_HEADS
    scores = []
    for j in range(A_KV_HEADS):
        q = _stack_heads(q_ref, list(range(group * j, group * (j + 1))), lo, HD ** -0.5)
        scores.append(lax.dot_general(q, _dup_half(k, j, lo).astype(BF16), NT_DIMS, preferred_element_type=F32))
    s = jnp.concatenate(scores, axis=0)
    sink = _sink_column(sink_ref, list(range(A_HEADS)), SEQ)
    mx = jnp.maximum(s.max(axis=-1, keepdims=True), sink)
    e = jnp.exp(s - mx)
    den = jnp.exp(sink - mx) + e.sum(axis=-1, keepdims=True)
    e = e.astype(BF16)
    rows_per_group = group * SEQ
    for j in range(A_KV_HEADS):
        rows = slice(j * rows_per_group, (j + 1) * rows_per_group)
        o = jnp.dot(e[rows], _dup_half(v, j, lo).astype(BF16), preferred_element_type=F32) / den[rows]
        for t in range(group // 2):
            pair = (group * j) // 2 + t
            o_ref[:, pair * LANES:(pair + 1) * LANES] = jnp.where(
                lo, o[(2 * t) * SEQ:(2 * t + 1) * SEQ], o[(2 * t + 1) * SEQ:(2 * t + 2) * SEQ])


def _ctx_gqa(p, sink):
    return pl.pallas_call(
        _ctx_gqa_kernel,
        grid_spec=pltpu.PrefetchScalarGridSpec(
            num_scalar_prefetch=1,
            grid=(BATCH,),
            in_specs=[pl.BlockSpec((SEQ, A_Q), lambda b, s: (b, 0)),
                      pl.BlockSpec((SEQ, A_KV), lambda b, s: (b, A_Q // A_KV)),
                      pl.BlockSpec((SEQ, A_KV), lambda b, s: (b, A_Q // A_KV + 1))],
            out_specs=pl.BlockSpec((SEQ, A_Q), lambda b, s: (b, 0))),
        out_shape=jax.ShapeDtypeStruct((NP_TOK, A_Q), F32),
        compiler_params=_cparams(1),
        name="ctx_gqa",
    )(sink, p, p, p)


def _win_kernel(sink_ref, q_ref, kp_ref, kc_ref, kn_ref, vp_ref, vc_ref, vn_ref, ck_ref, cv_ref, o_ref):
    i = pl.program_id(1)
    lo = _lane_lo()
    k = jnp.concatenate([kp_ref[...], kc_ref[...], kn_ref[...]], axis=0)
    v = jnp.concatenate([vp_ref[...], vc_ref[...], vn_ref[...]], axis=0)
    ck = ck_ref[...]
    cv = cv_ref[...]
    group = A_HEADS // A_KV_HEADS
    half_q = WIN_Q // 2
    span = half_q + 2 * A_WINDOW
    s_loc, s_ctx, sinks, values = [], [], [], []
    for j in range(A_KV_HEADS):
        heads = list(range(group * j, group * (j + 1)))
        kd = _dup_half(k, j, lo).astype(BF16)
        ckd = _dup_half(ck, j, lo).astype(BF16)
        vd = _dup_half(v, j, lo).astype(BF16)
        cvd = _dup_half(cv, j, lo).astype(BF16)
        for part in range(2):
            key0 = part * half_q
            qpos = i * WIN_Q + key0 + lax.broadcasted_iota(jnp.int32, (half_q, span), 0)
            kpos = i * WIN_Q - A_WINDOW + key0 + lax.broadcasted_iota(jnp.int32, (half_q, span), 1)
            valid = (jnp.abs(kpos - qpos) <= A_WINDOW) & (kpos >= 0) & (kpos < DEC_SEQ)
            valid = jnp.concatenate([valid] * group, axis=0)
            q = _stack_heads(q_ref, heads, lo, HD ** -0.5, pl.ds(key0, half_q))
            s = lax.dot_general(q, kd[key0:key0 + span], NT_DIMS, preferred_element_type=F32)
            s_loc.append(jnp.where(valid, s, NEG))
            s_ctx.append(lax.dot_general(q, ckd, NT_DIMS, preferred_element_type=F32))
            sinks.append(_sink_column(sink_ref, heads, half_q))
            values.append((j, part, vd[key0:key0 + span], cvd))
    s_loc = jnp.concatenate(s_loc, axis=0)
    s_ctx = jnp.concatenate(s_ctx, axis=0)
    sink = jnp.concatenate(sinks, axis=0)
    mx = jnp.maximum(jnp.maximum(s_loc.max(axis=-1, keepdims=True), s_ctx.max(axis=-1, keepdims=True)), sink)
    p_loc = jnp.exp(s_loc - mx)
    p_ctx = jnp.exp(s_ctx - mx)
    den = p_loc.sum(axis=-1, keepdims=True) + p_ctx.sum(axis=-1, keepdims=True) + jnp.exp(sink - mx)
    p_loc = p_loc.astype(BF16)
    p_ctx = p_ctx.astype(BF16)
    rows_per_unit = group * half_q
    for unit, (j, part, vd, cvd) in enumerate(values):
        rows = slice(unit * rows_per_unit, (unit + 1) * rows_per_unit)
        o = (jnp.dot(p_loc[rows], vd, preferred_element_type=F32)
             + jnp.dot(p_ctx[rows], cvd, preferred_element_type=F32)) / den[rows]
        for t in range(group // 2):
            pair = (group * j) // 2 + t
            o_ref[pl.ds(part * half_q, half_q), pair * LANES:(pair + 1) * LANES] = jnp.where(
                lo, o[(2 * t) * half_q:(2 * t + 1) * half_q], o[(2 * t + 1) * half_q:(2 * t + 2) * half_q])


WIN_Q = 256


def _win_attention(p, cache_k, cache_v, sink):
    nblk = DEC_SEQ // WIN_Q
    side = WIN_Q // A_WINDOW
    nside = DEC_SEQ // A_WINDOW
    base = NP_TOK // WIN_Q
    side_base = NP_TOK // A_WINDOW
    kcol = A_Q // A_KV

    def main_spec(col):
        return pl.BlockSpec((WIN_Q, A_KV), lambda b, i, s: (base + b * nblk + i, col))

    def side_spec(col, off):
        return pl.BlockSpec((A_WINDOW, A_KV),
                            lambda b, i, s: (side_base + b * nside + jnp.clip(side * i + off, 0, nside - 1), col))

    ctx_spec = pl.BlockSpec((None, PAST, A_KV), lambda b, i, s: (b, 0, 0))
    return pl.pallas_call(
        _win_kernel,
        grid_spec=pltpu.PrefetchScalarGridSpec(
            num_scalar_prefetch=1,
            grid=(DEC_BATCH, nblk),
            in_specs=[pl.BlockSpec((WIN_Q, A_Q), lambda b, i, s: (base + b * nblk + i, 0)),
                      side_spec(kcol, -1), main_spec(kcol), side_spec(kcol, side),
                      side_spec(kcol + 1, -1), main_spec(kcol + 1), side_spec(kcol + 1, side),
                      ctx_spec, ctx_spec],
            out_specs=pl.BlockSpec((WIN_Q, A_Q), lambda b, i, s: (b * nblk + i, 0))),
        out_shape=jax.ShapeDtypeStruct((NS_TOK, A_Q), F32),
        compiler_params=_cparams(2),
        name="win_attn",
    )(sink, p, p, p, p, p, p, p, cache_k.reshape(DEC_BATCH, PAST, A_KV), cache_v.reshape(DEC_BATCH, PAST, A_KV))


def _ret_kernel(df_ref, db_ref, q_ref, k_ref, v_ref, g_ref, gn_ref, s0f_ref, s0b_ref,
                o_ref, sf_ref, sb_ref, of_scr, ob_scr, *, length):
    c_len = RET_CHUNK
    n = length // c_len
    lo = _lane_lo()
    hi = jnp.logical_not(lo)
    row = lax.broadcasted_iota(jnp.int32, (c_len, c_len), 0)
    col = lax.broadcasted_iota(jnp.int32, (c_len, c_len), 1)
    rowp = lax.broadcasted_iota(jnp.int32, (LANES, LANES), 0)
    colp = lax.broadcasted_iota(jnp.int32, (LANES, LANES), 1)
    blockdiag = (rowp < HD) == (colp < HD)
    idx = lax.broadcasted_iota(jnp.int32, (c_len, 1), 0).astype(F32)

    def direction(dec_ref, forward):
        lg = -jnp.exp(dec_ref[...])
        diff = (row - col) if forward else (col - row)
        keep = (diff >= 0) if forward else (diff > 0)
        dist = jnp.maximum(diff, 0).astype(F32)
        dm = jnp.concatenate([jnp.where(keep, jnp.exp(dist * lg[:, off:off + 1]), 0.0) for off in (0, HD)], axis=0)
        if forward:
            xi = jnp.exp((idx + 1.0) * lg)
            zeta = jnp.exp((c_len - 1.0 - idx) * lg)
        else:
            xi = jnp.exp((c_len - idx) * lg)
            zeta = jnp.exp(idx * lg)
        return dm, xi, zeta, jnp.exp(c_len * lg)

    def chunk(c, state, consts):
        dm, xi, zeta, gch = consts
        rows = pl.ds(pl.multiple_of(c * c_len, c_len), c_len)
        qc = q_ref[rows, :]
        kc = k_ref[rows, :] * HD ** -0.5
        vc = v_ref[rows, :].astype(BF16)
        kb = kc.astype(BF16)
        q2 = jnp.concatenate([jnp.where(lo, qc, 0.0), jnp.where(hi, qc, 0.0)], axis=0).astype(BF16)
        inner = lax.dot_general(q2, kb, NT_DIMS, preferred_element_type=F32) * dm
        kz_t = (kc * zeta).T
        res = jnp.dot(jnp.concatenate([inner, kz_t], axis=0).astype(BF16), vc, preferred_element_type=F32)
        cross = jnp.dot(qc.astype(BF16), state.astype(BF16), preferred_element_type=F32) * xi
        o = jnp.where(lo, res[:c_len], res[c_len:2 * c_len]) + cross
        state = gch * state + jnp.where(blockdiag, res[2 * c_len:], 0.0)
        return rows, o, state

    cf = direction(df_ref, True)
    cb = direction(db_ref, False)

    def scan_body(t, states):
        rows_f, o_f, state_f = chunk(t, states[0], cf)
        of_scr[rows_f, :] = o_f
        rows_b, o_b, state_b = chunk(n - 1 - t, states[1], cb)
        ob_scr[rows_b, :] = o_b
        return state_f, state_b

    state_f, state_b = lax.fori_loop(0, n, scan_body, (s0f_ref[...], s0b_ref[...]), unroll=min(n, RET_UNROLL))
    sf_ref[...] = state_f
    sb_ref[...] = state_b

    gn = gn_ref[...]
    norm_rows = min(RET_NORM_ROWS, length)
    n_norm = length // norm_rows

    def per_head(x):
        a = jnp.where(lo, x, 0.0).sum(axis=-1, keepdims=True)
        b = jnp.where(hi, x, 0.0).sum(axis=-1, keepdims=True)
        return jnp.where(lo, a, b) * (1.0 / HD)

    def norm_body(t, carry):
        rows = pl.ds(pl.multiple_of(t * norm_rows, norm_rows), norm_rows)
        o = of_scr[rows, :] + ob_scr[rows, :]
        d = o - per_head(o)
        y = d * lax.rsqrt(per_head(d * d) + EPS) * gn
        o_ref[rows, :] = _silu(g_ref[rows, :]) * y
        return carry

    lax.fori_loop(0, n_norm, norm_body, 0)


def _pair_lanes(v):
    return jnp.repeat(v.astype(F32), HD).reshape(B_HEADS // 2, 1, LANES)


def _blockdiag_states(s):
    b = s.shape[0]
    s = s.astype(F32).reshape(b, B_HEADS // 2, 2, HD, HD)
    z = jnp.zeros_like(s[:, :, 0])
    top = jnp.concatenate([s[:, :, 0], z], axis=-1)
    bot = jnp.concatenate([z, s[:, :, 1]], axis=-1)
    return jnp.concatenate([top, bot], axis=-2)


def _diag_states(sp):
    b = sp.shape[0]
    s = jnp.stack([sp[:, :, :HD, :HD], sp[:, :, HD:, HD:]], axis=2)
    return s.reshape(b, B_HEADS, HD, HD)


def _retention(p, row_base, batch, length, dec_f, dec_b, gn_g, s0f, s0b):
    npairs = B_HEADS // 2
    blk0 = row_base // length
    qcol = (A_Q + 2 * A_KV) // LANES

    def col_spec(off):
        return pl.BlockSpec((length, LANES), lambda b, h: (blk0 + b, qcol + off * npairs + h))

    lane_spec = pl.BlockSpec((None, 1, LANES), lambda b, h: (h, 0, 0))
    state_spec = pl.BlockSpec((None, None, LANES, LANES), lambda b, h: (b, h, 0, 0))
    state_shape = jax.ShapeDtypeStruct((batch, npairs, LANES, LANES), F32)
    return pl.pallas_call(
        functools.partial(_ret_kernel, length=length),
        grid=(batch, npairs),
        in_specs=[lane_spec, lane_spec, col_spec(0), col_spec(1), col_spec(2), col_spec(3), lane_spec,
                  state_spec, state_spec],
        out_specs=[pl.BlockSpec((length, LANES), lambda b, h: (b, h)), state_spec, state_spec],
        out_shape=[jax.ShapeDtypeStruct((batch * length, B_W), F32), state_shape, state_shape],
        scratch_shapes=[pltpu.VMEM((length, LANES), F32), pltpu.VMEM((length, LANES), F32)],
        compiler_params=_cparams(2),
        name="retention",
    )(_pair_lanes(dec_f), _pair_lanes(dec_b), p, p, p, p, gn_g.reshape(npairs, 1, LANES), s0f, s0b)


def _ctx_mha_kernel(q_ref, k_ref, v_ref, o_ref):
    lo = _lane_lo()
    for pair in range(C_HEADS // 2):
        cols = slice(pair * LANES, (pair + 1) * LANES)
        q = _stack_heads(q_ref, [2 * pair, 2 * pair + 1], lo, HD ** -0.5)
        s = lax.dot_general(q, k_ref[:, cols].astype(BF16), NT_DIMS, preferred_element_type=F32)
        o = _softmax_av([s], [v_ref[:, cols].astype(BF16)])
        o_ref[:, cols] = jnp.where(lo, o[:SEQ], o[SEQ:])


def _ctx_mha(p):
    return pl.pallas_call(
        _ctx_mha_kernel,
        grid=(BATCH,),
        in_specs=[pl.BlockSpec((SEQ, C_W), lambda b: (b, 0)),
                  pl.BlockSpec((SEQ, C_W), lambda b: (b, 1)),
                  pl.BlockSpec((SEQ, C_W), lambda b: (b, 2))],
        out_specs=pl.BlockSpec((SEQ, C_W), lambda b: (b, 0)),
        out_shape=jax.ShapeDtypeStruct((NP_TOK, C_W), F32),
        compiler_params=_cparams(1),
        name="ctx_mha",
    )(p, p, p)


NA_WIN_ROWS = 2 * NA_ROWS
NA_WIN = NA_WIN_ROWS * GRID_W
NA_QROWS = NA_ROWS * GRID_W
NA_PAD_ROWS = NA_KH // 2
NA_TABLE = 1536
NA_SPAN = 768


def _na_kernel(q_ref, kp_ref, km_ref, kn_ref, vp_ref, vm_ref, vn_ref, ck_ref, cv_ref, ue_ref, uo_ref, o_ref):
    r0 = pl.program_id(2) * NA_ROWS
    n_rows = DEC_SEQ // GRID_W
    lo = _lane_lo()
    k = jnp.concatenate([kp_ref[...], km_ref[...], kn_ref[...]], axis=0).astype(BF16)
    v = jnp.concatenate([vp_ref[...], vm_ref[...], vn_ref[...]], axis=0).astype(BF16)
    ck = ck_ref[...].astype(BF16)
    cv = cv_ref[...].astype(BF16)
    q = q_ref[...] * HD ** -0.5
    outs = []
    for half, keep in enumerate((lo, jnp.logical_not(lo))):
        qh = jnp.where(keep, q, 0.0).astype(BF16)
        s_ctx = lax.dot_general(qh, ck, NT_DIMS, preferred_element_type=F32)
        o_parts = []
        for part in range(2):
            key0 = part * (NA_WIN - NA_SPAN)
            klane = key0 + lax.broadcasted_iota(jnp.int32, (1, NA_SPAN), 1)
            qrows = slice(part * NA_QROWS // 2, (part + 1) * NA_QROWS // 2)
            s = lax.dot_general(qh[qrows], k[key0:key0 + NA_SPAN], NT_DIMS, preferred_element_type=F32)
            p_loc, p_ctx, den = [], [], []
            for t in range(NA_ROWS // 2):
                rq = part * NA_ROWS // 2 + t
                start = NA_KH - 1 - rq
                if start % 2 == 0:
                    u = ue_ref[half, :, start * GRID_W + key0:start * GRID_W + key0 + NA_SPAN]
                else:
                    u = uo_ref[half, :, (start - 1) * GRID_W + key0:(start - 1) * GRID_W + key0 + NA_SPAN]
                r = r0 + rq
                first = jnp.clip(r - NA_KH // 2, 0, n_rows - NA_KH)
                lane0 = (first - r0 + NA_PAD_ROWS) * GRID_W
                in_rows = (klane >= lane0) & (klane < lane0 + NA_KH * GRID_W)
                sl = jnp.where(in_rows, s[t * GRID_W:(t + 1) * GRID_W] + u, NEG)
                sc = s_ctx[rq * GRID_W:(rq + 1) * GRID_W]
                mx = jnp.maximum(sl.max(axis=-1, keepdims=True), sc.max(axis=-1, keepdims=True))
                el = jnp.exp(sl - mx)
                ec = jnp.exp(sc - mx)
                den.append(el.sum(axis=-1, keepdims=True) + ec.sum(axis=-1, keepdims=True))
                p_loc.append(el.astype(BF16))
                p_ctx.append(ec.astype(BF16))
            acc = (jnp.dot(jnp.concatenate(p_loc, axis=0), v[key0:key0 + NA_SPAN], preferred_element_type=F32)
                   + jnp.dot(jnp.concatenate(p_ctx, axis=0), cv, preferred_element_type=F32))
            o_parts.append(acc / jnp.concatenate(den, axis=0))
        outs.append(jnp.concatenate(o_parts, axis=0))
    o_ref[...] = jnp.where(lo, outs[0], outs[1])


def _na_bias_tables(rpb):
    cq = jnp.arange(GRID_W)
    ck = jnp.arange(GRID_W)
    dc = jnp.clip(ck[None] - cq[:, None], -(NA_KW - 1), NA_KW - 1) + NA_KW - 1
    cs = jnp.clip(cq - NA_KW // 2, 0, GRID_W - NA_KW)
    col_ok = (ck[None] >= cs[:, None]) & (ck[None] < cs[:, None] + NA_KW)
    t = rpb.astype(F32)[:, :, dc]
    t = jnp.where(col_ok[None, None], t, NEG).transpose(0, 2, 1, 3)
    n_dr = 2 * NA_KH - 1
    blocks = NA_TABLE // GRID_W
    t = jnp.pad(t, ((0, 0), (0, 0), (NA_PAD_ROWS, blocks - n_dr - NA_PAD_ROWS), (0, 0)), constant_values=NEG)
    ue = t.reshape(C_HEADS, GRID_W, NA_TABLE)
    uo = jnp.concatenate([ue[..., GRID_W:], jnp.full((C_HEADS, GRID_W, GRID_W), NEG, F32)], axis=-1)
    return ue, uo


def _na_attention(p, cache_k, cache_v, tables):
    npairs = C_HEADS // 2
    nrb = DEC_SEQ // NA_QROWS
    half = NA_QROWS // 2
    qbase = NP_TOK // NA_QROWS
    hbase = NP_TOK // half
    kcol = C_W // LANES
    ue, uo = tables

    def main_spec(col0):
        return pl.BlockSpec((NA_QROWS, LANES), lambda b, h, r: (qbase + b * nrb + r, col0 + h))

    def side_spec(col0, off):
        return pl.BlockSpec((half, LANES),
                            lambda b, h, r: (hbase + b * 2 * nrb + jnp.clip(2 * r + off, 0, 2 * nrb - 1), col0 + h))

    ctx_spec = pl.BlockSpec((None, PAST, LANES), lambda b, h, r: (b, 0, h))
    tab_spec = pl.BlockSpec((2, GRID_W, NA_TABLE), lambda b, h, r: (h, 0, 0))
    return pl.pallas_call(
        _na_kernel,
        grid=(DEC_BATCH, npairs, nrb),
        in_specs=[main_spec(0),
                  side_spec(kcol, -1), main_spec(kcol), side_spec(kcol, 2),
                  side_spec(2 * kcol, -1), main_spec(2 * kcol), side_spec(2 * kcol, 2),
                  ctx_spec, ctx_spec, tab_spec, tab_spec],
        out_specs=pl.BlockSpec((NA_QROWS, LANES), lambda b, h, r: (b * nrb + r, h)),
        out_shape=jax.ShapeDtypeStruct((NS_TOK, C_W), F32),
        compiler_params=_cparams(3),
        name="na_attn",
    )(p, p, p, p, p, p, p, cache_k.reshape(DEC_BATCH, PAST, C_W), cache_v.reshape(DEC_BATCH, PAST, C_W), ue, uo)


def _route(biased, scores):
    t = biased.shape[1]
    per_group = N_EXPERTS // N_GROUPS
    i8 = lax.broadcasted_iota(jnp.int32, (per_group, t), 0)
    g_rows = []
    for g in range(N_GROUPS):
        bg = biased[g * per_group:(g + 1) * per_group]
        m1 = bg.max(axis=0, keepdims=True)
        first = jnp.where(bg == m1, i8, per_group).min(axis=0, keepdims=True)
        m2 = jnp.where(i8 == first, -jnp.inf, bg).max(axis=0, keepdims=True)
        g_rows.append(m1 + m2)
    g_top = jnp.concatenate(g_rows, axis=0)
    gi = lax.broadcasted_iota(jnp.int32, g_top.shape, 0)
    g_sel = jnp.zeros(g_top.shape, jnp.int32)
    cur = g_top
    for _ in range(TOPK_GROUPS):
        m = cur.max(axis=0, keepdims=True)
        hit = gi == jnp.where(cur == m, gi, N_GROUPS).min(axis=0, keepdims=True)
        g_sel = jnp.where(hit, 1, g_sel)
        cur = jnp.where(hit, -jnp.inf, cur)
    e_sel = jnp.concatenate([jnp.broadcast_to(g_sel[g:g + 1], (per_group, t)) for g in range(N_GROUPS)], axis=0)
    cur = jnp.where(e_sel > 0, biased, NEG)
    ei = lax.broadcasted_iota(jnp.int32, cur.shape, 0)
    ids, gates, hits = [], [], []
    for _ in range(TOP_K):
        m = cur.max(axis=0, keepdims=True)
        f = jnp.where(cur == m, ei, N_EXPERTS).min(axis=0, keepdims=True)
        hit = ei == f
        ids.append(f)
        hits.append(hit)
        gates.append(jnp.where(hit, scores, 0.0).sum(axis=0, keepdims=True))
        cur = jnp.where(hit, -jnp.inf, cur)
    gate = jnp.concatenate(gates, axis=0)
    gate = gate / gate.sum(axis=0, keepdims=True) * ROUTED_SCALE
    return jnp.concatenate(ids, axis=0), gate, hits


def _pack_bf16_pairs(h):
    bits = lax.bitcast_convert_type(h.astype(BF16).astype(F32), jnp.uint32)
    return bits[:, :D // 2] | (bits[:, D // 2:] >> 16)


def _unpack_bf16_pairs(xp):
    hi = lax.bitcast_convert_type(xp & jnp.uint32(0xFFFF0000), F32).astype(BF16)
    lo = lax.bitcast_convert_type(xp << 16, F32).astype(BF16)
    return hi, lo


def _dot_halves(hi, lo, w_ref):
    return (jnp.dot(hi, w_ref[:D // 2, :], preferred_element_type=F32)
            + jnp.dot(lo, w_ref[D // 2:, :], preferred_element_type=F32))


def _outproj_kernel(*refs, n_parts):
    xp_ref, xs_ref = refs[:2]
    part_refs = refs[2:2 + 3 * n_parts]
    gate_ref, shift_ref, scale_ref, g2_ref, rw_ref, rb_ref = refs[2 + 3 * n_parts:8 + 3 * n_parts]
    xo_ref, h_ref, dest_ref, wgt_ref, plan_ref, cnt_ref = refs[8 + 3 * n_parts:]
    step = pl.program_id(0)

    @pl.when(step == 0)
    def _():
        cnt_ref[...] = jnp.zeros_like(cnt_ref)
        plan_ref[...] = jnp.zeros_like(plan_ref)

    y = None
    for t in range(n_parts):
        ap_ref, as_ref, w_ref = part_refs[3 * t:3 * t + 3]
        d = jnp.dot(_pick_rows(ap_ref, as_ref, TM_OUT).astype(BF16), w_ref[...], preferred_element_type=F32)
        y = d if y is None else y + d
    x = _pick_rows(xp_ref, xs_ref, TM_OUT) + gate_ref[...] * y
    xo_ref[...] = x
    h = _rms(x, g2_ref[...]) * (1.0 + scale_ref[...]) + shift_ref[...]
    h_ref[...] = _pack_bf16_pairs(h)
    h_hi = h.astype(BF16)
    h_lo = (h - h_hi.astype(F32)).astype(BF16)
    rw = rw_ref[...]
    rw_hi = rw.astype(BF16)
    rw_lo = (rw - rw_hi.astype(F32)).astype(BF16)
    logits = (lax.dot_general(rw_hi, h_hi, NT_DIMS, preferred_element_type=F32)
              + lax.dot_general(rw_hi, h_lo, NT_DIMS, preferred_element_type=F32)
              + lax.dot_general(rw_lo, h_hi, NT_DIMS, preferred_element_type=F32))
    scores = jax.nn.sigmoid(logits)
    _, gate, hits = _route(scores + rb_ref[...], scores)
    wgt_ref[...] = gate
    chosen = hits[0]
    for hit in hits[1:]:
        chosen = chosen | hit
    m = jnp.where(chosen, 1.0, 0.0)
    before = (lax.broadcasted_iota(jnp.int32, (TM_OUT, TM_OUT), 0)
              < lax.broadcasted_iota(jnp.int32, (TM_OUT, TM_OUT), 1))
    prefix = jnp.dot(m.astype(BF16), jnp.where(before, 1.0, 0.0).astype(BF16), preferred_element_type=F32)
    e_base = (lax.broadcasted_iota(jnp.int32, (N_EXPERTS, 1), 0) * N_TOK).astype(F32)
    row_all = prefix + (cnt_ref[...] + e_base)
    dest_ref[...] = jnp.concatenate(
        [jnp.where(hit, row_all, 0.0).sum(axis=0, keepdims=True) for hit in hits], axis=0).astype(jnp.int32)
    cnt_ref[...] += m.sum(axis=1, keepdims=True)

    @pl.when(step == pl.num_programs(0) - 1)
    def _():
        _block_plan(cnt_ref[...], plan_ref)


def _block_plan(counts, plan_ref):
    cap_blocks = N_TOK // MOE_BLOCK
    nblk = ((counts.astype(jnp.int32) + (MOE_BLOCK - 1)) // MOE_BLOCK).astype(F32)
    lower = (lax.broadcasted_iota(jnp.int32, (N_EXPERTS, N_EXPERTS), 0)
             >= lax.broadcasted_iota(jnp.int32, (N_EXPERTS, N_EXPERTS), 1))
    cum = jnp.dot(jnp.where(lower, 1.0, 0.0).astype(BF16), jnp.broadcast_to(nblk, (N_EXPERTS, LANES)).astype(BF16),
                  preferred_element_type=F32)[:, :1]
    n_used = cum[N_EXPERTS - 1:, :]
    slot = jnp.minimum(lax.broadcasted_iota(jnp.int32, (1, PLAN_LANES), 1).astype(F32), n_used - 1.0)
    done = cum <= slot
    expert = jnp.where(done, 1.0, 0.0).sum(axis=0, keepdims=True)
    blocks_before = jnp.where(done, nblk, 0.0).sum(axis=0, keepdims=True)
    plan_ref[0:1, :] = (expert * cap_blocks + (slot - blocks_before)).astype(jnp.int32)
    plan_ref[1:2, :] = expert.astype(jnp.int32)
    plan_ref[2:3, :] = jnp.broadcast_to(n_used, (1, PLAN_LANES)).astype(jnp.int32)
    run_end = jnp.where(cum > slot, cum, jnp.inf).min(axis=0, keepdims=True)
    plan_ref[3:4, :] = jnp.minimum(jnp.where(cum <= run_end, 1.0, 0.0).sum(axis=0, keepdims=True),
                                   N_EXPERTS - 1.0).astype(jnp.int32)
    plan_ref[4:5, :] = jnp.where(run_end < n_used, 1, 0).astype(jnp.int32)


def _outproj(x, parts, mod, g2, router_w, router_b):
    in_specs = _pair_specs(D, TM_OUT)
    args = [x[0], x[1]]
    for ap, a_s, w in parts:
        width = ap.shape[1]
        in_specs += _pair_specs(width, TM_OUT) + [pl.BlockSpec((width, D), lambda i: (0, 0))]
        args += [ap, a_s, w]
    in_specs += [_mod_spec(2, TM_OUT), _mod_spec(3, TM_OUT), _mod_spec(4, TM_OUT),
                 pl.BlockSpec((1, D), lambda i: (0, 0)),
                 pl.BlockSpec((N_EXPERTS, D), lambda i: (0, 0)),
                 pl.BlockSpec((N_EXPERTS, 1), lambda i: (0, 0))]
    args += [mod, mod, mod, g2.reshape(1, D), router_w.T, router_b.reshape(N_EXPERTS, 1)]
    return pl.pallas_call(
        functools.partial(_outproj_kernel, n_parts=len(parts)),
        grid=(N_TOK // TM_OUT,),
        in_specs=in_specs,
        out_specs=[pl.BlockSpec((TM_OUT, D), lambda i: (i, 0)),
                   pl.BlockSpec((TM_OUT, D // 2), lambda i: (i, 0)),
                   pl.BlockSpec((TOP_K, TM_OUT), lambda i: (0, i)),
                   pl.BlockSpec((TOP_K, TM_OUT), lambda i: (0, i)),
                   pl.BlockSpec((8, PLAN_LANES), lambda i: (0, 0))],
        out_shape=[jax.ShapeDtypeStruct((N_TOK, D), F32),
                   jax.ShapeDtypeStruct((N_TOK, D // 2), jnp.uint32),
                   jax.ShapeDtypeStruct((TOP_K, N_TOK), jnp.int32),
                   jax.ShapeDtypeStruct((TOP_K, N_TOK), F32),
                   jax.ShapeDtypeStruct((8, PLAN_LANES), jnp.int32)],
        scratch_shapes=[pltpu.VMEM((N_EXPERTS, 1), F32)],
        compiler_params=_cparams(1),
        name="outproj_router",
    )(*args)


def _experts_kernel(br_ref, be_ref, nu_ref, ne_ref, hn_ref, x_hbm, w1_hbm, w3_hbm, w2_hbm, o_ref,
                    w1b, w3b, w2b, w1f, w3f, w2f, xbuf, xsem, wsem, *, layer):
    i = pl.program_id(0)
    n_steps = pl.num_programs(0)
    e = be_ref[i]
    prev = be_ref[jnp.maximum(i - 1, 0)]

    def x_copy(step):
        slot = step % EXPERT_X_SLOTS
        rows = pl.ds(pl.multiple_of(br_ref[step] * MOE_BLOCK, MOE_BLOCK), MOE_BLOCK)
        return pltpu.make_async_copy(x_hbm.at[rows], xbuf.at[slot], xsem.at[slot])

    @pl.when(i == 0)
    def _():
        x_copy(0).start()
        x_copy(1).start()

    @pl.when(i + 2 < n_steps)
    def _():
        x_copy(i + 2).start()

    x_copy(i).wait()

    def w_copies(expert):
        return [pltpu.make_async_copy(src.at[layer, expert], dst, wsem.at[k])
                for k, (src, dst) in enumerate(((w1_hbm, w1f), (w3_hbm, w3f), (w2_hbm, w2f)))]

    @pl.when(i == 0)
    def _():
        for c in w_copies(e):
            c.start()

    @pl.when((i == 0) | (e != prev))
    def _():
        for c in w_copies(e):
            c.wait()
        w1b[...] = w1f[...].astype(BF16)
        w3b[...] = w3f[...].astype(BF16)
        w2b[...] = w2f[...].astype(BF16)

        @pl.when(hn_ref[i] > 0)
        def _():
            for c in w_copies(ne_ref[i]):
                c.start()

    @pl.when(i < nu_ref[0])
    def _():
        hi, lo = _unpack_bf16_pairs(xbuf[i % EXPERT_X_SLOTS])
        a = _dot_halves(hi, lo, w1b)
        b = _dot_halves(hi, lo, w3b)
        h = (_silu(a) * b).astype(BF16)
        o_ref[...] = _pack_bf16_pairs(jnp.dot(h, w2b[...], preferred_element_type=F32))


def _experts(plan, x_rows, w1, w3, w2, layer):
    any_spec = pl.BlockSpec(memory_space=pl.ANY)
    return pl.pallas_call(
        functools.partial(_experts_kernel, layer=layer),
        grid_spec=pltpu.PrefetchScalarGridSpec(
            num_scalar_prefetch=5,
            grid=(N_MOE_BLOCKS,),
            in_specs=[any_spec, any_spec, any_spec, any_spec],
            out_specs=pl.BlockSpec((MOE_BLOCK, D // 2), lambda i, br, be, nu, ne, hn: (br[i], 0)),
            scratch_shapes=[pltpu.VMEM((D, FF), BF16), pltpu.VMEM((D, FF), BF16), pltpu.VMEM((FF, D), BF16),
                            pltpu.VMEM((D, FF), F32), pltpu.VMEM((D, FF), F32), pltpu.VMEM((FF, D), F32),
                            pltpu.VMEM((EXPERT_X_SLOTS, MOE_BLOCK, D // 2), jnp.uint32),
                            pltpu.SemaphoreType.DMA((EXPERT_X_SLOTS,)),
                            pltpu.SemaphoreType.DMA((3,))]),
        out_shape=jax.ShapeDtypeStruct(x_rows.shape, jnp.uint32),
        compiler_params=_cparams(1),
        name="experts",
    )(plan[0], plan[1], plan[2, :1], plan[3], plan[4], x_rows, w1, w3, w2)


SC_CORES = 2
SC_SUBCORES = 16
SC_WORKERS = SC_CORES * SC_SUBCORES
SC_CHUNK_BYTES = 64 * 1024
SC_SLOTS = 4


def _sc_scatter(rows, dest, n_out):
    n_rows, width = rows.shape
    picks = dest.shape[0]
    chunk = SC_CHUNK_BYTES // (4 * width)
    per_worker = n_rows // SC_WORKERS
    n_chunks = per_worker // chunk
    assert per_worker * SC_WORKERS == n_rows and n_chunks * chunk == per_worker and n_chunks % 2 == 0
    mesh = plsc.VectorSubcoreMesh(core_axis_name="c", subcore_axis_name="s")

    @functools.partial(
        pl.kernel, mesh=mesh,
        out_type=jax.ShapeDtypeStruct((n_out, width), rows.dtype),
        scratch_types=[pltpu.VMEM((picks, n_chunks, chunk), jnp.int32),
                       pltpu.VMEM((2, chunk, width), rows.dtype),
                       pltpu.SemaphoreType.DMA((2,)),
                       pltpu.SemaphoreType.DMA((2,))])
    def scatter(r_hbm, d_hbm, o_hbm, idx_v, rows_v, lsem, ssem):
        worker = lax.axis_index("s") * SC_CORES + lax.axis_index("c")
        base = worker * per_worker
        for k in range(picks):
            pltpu.sync_copy(d_hbm.at[k, worker], idx_v.at[k])

        def load_copy(c, b):
            src = pl.ds(pl.multiple_of(base + c * chunk, chunk), chunk)
            return pltpu.make_async_copy(r_hbm.at[src], rows_v.at[b], lsem.at[b])

        def store_copy(c, b, k):
            return pltpu.make_async_copy(rows_v.at[b], o_hbm.at[idx_v.at[k, c]], ssem.at[b])

        load_copy(0, 0).start()

        @pl.loop(0, n_chunks, step=2)
        def _(c0):
            for b in range(2):
                c = c0 + b
                load_copy(c, b).wait()
                for k in range(picks):
                    store_copy(c, b, k).start()

                @pl.when(c > 0)
                def _():
                    for k in range(picks):
                        store_copy(c - 1, 1 - b, k).wait()

                @pl.when(c + 1 < n_chunks)
                def _():
                    load_copy(c + 1, 1 - b).start()

        for k in range(picks):
            store_copy(n_chunks - 1, 1, k).wait()

    return scatter(rows, dest.reshape(picks, SC_WORKERS, n_chunks, chunk))


def _sc_gather(table, idx):
    n_idx = idx.shape[0]
    width = table.shape[1]
    chunk = SC_CHUNK_BYTES // (4 * width)
    per_worker = n_idx // SC_WORKERS
    n_chunks = per_worker // chunk
    ahead = SC_SLOTS - 1
    assert per_worker * SC_WORKERS == n_idx and n_chunks * chunk == per_worker and n_chunks % SC_SLOTS == 0
    mesh = plsc.VectorSubcoreMesh(core_axis_name="c", subcore_axis_name="s")

    @functools.partial(
        pl.kernel, mesh=mesh,
        out_type=jax.ShapeDtypeStruct((n_idx, width), table.dtype),
        scratch_types=[pltpu.VMEM((per_worker,), jnp.int32),
                       pltpu.VMEM((SC_SLOTS, chunk, width), table.dtype),
                       pltpu.SemaphoreType.DMA((SC_SLOTS,)),
                       pltpu.SemaphoreType.DMA((SC_SLOTS,))])
    def gather(t_hbm, i_hbm, o_hbm, idx_v, rows_v, gsem, wsem):
        worker = lax.axis_index("s") * SC_CORES + lax.axis_index("c")
        base = worker * per_worker
        pltpu.sync_copy(i_hbm.at[pl.ds(pl.multiple_of(base, chunk), per_worker)], idx_v)

        def gather_copy(c, b):
            ids = idx_v.at[pl.ds(pl.multiple_of(c * chunk, chunk), chunk)]
            return pltpu.make_async_copy(t_hbm.at[ids], rows_v.at[b], gsem.at[b])

        def write_copy(c, b):
            rows = pl.ds(pl.multiple_of(base + c * chunk, chunk), chunk)
            return pltpu.make_async_copy(rows_v.at[b], o_hbm.at[rows], wsem.at[b])

        for c in range(ahead):
            gather_copy(c, c).start()

        @pl.loop(0, n_chunks, step=SC_SLOTS)
        def _(c0):
            for b in range(SC_SLOTS):
                c = c0 + b
                refill = (b + ahead) % SC_SLOTS
                gather_copy(c, b).wait()
                write_copy(c, b).start()

                @pl.when(c > 0)
                def _():
                    write_copy(c - 1, refill).wait()

                @pl.when(c + ahead < n_chunks)
                def _():
                    gather_copy(c + ahead, refill).start()

        write_copy(n_chunks - 1, (n_chunks - 1) % SC_SLOTS).wait()

    return gather(table, idx)


TC = 512


def _combine_kernel(x_ref, h_ref, y_ref, wgt_ref, gate_ref, w1_ref, w3_ref, w2_ref, fg_ref, o_ref, *, final):
    hi, lo = _unpack_bf16_pairs(h_ref[...])
    a = _dot_halves(hi, lo, w1_ref)
    b = _dot_halves(hi, lo, w3_ref)
    ffn = jnp.dot((_silu(a) * b).astype(BF16), w2_ref[...], preferred_element_type=F32)
    wgt = wgt_ref[...]
    r_hi = None
    r_lo = None
    for k in range(TOP_K):
        yk = y_ref[k]
        w = wgt[:, k:k + 1]
        t_hi = lax.bitcast_convert_type(yk & jnp.uint32(0xFFFF0000), F32) * w
        t_lo = lax.bitcast_convert_type(yk << 16, F32) * w
        r_hi = t_hi if r_hi is None else r_hi + t_hi
        r_lo = t_lo if r_lo is None else r_lo + t_lo
    x = x_ref[...] + gate_ref[...] * (ffn + jnp.concatenate([r_hi, r_lo], axis=1))
    o_ref[...] = _rms(x, fg_ref[...]) if final else x


def _combine(x, h, y_rows, wgt, mod, sw1, sw3, sw2, final_g, final):
    weights = (sw1.astype(BF16), sw3.astype(BF16), sw2.astype(BF16), final_g.reshape(1, D))

    def rows_from(first_row, n_rows, y_group):
        b0 = first_row // TC
        return pl.pallas_call(
            functools.partial(_combine_kernel, final=final),
            grid=(n_rows // TC,),
            in_specs=[pl.BlockSpec((TC, D), lambda i: (i + b0, 0)),
                      pl.BlockSpec((TC, D // 2), lambda i: (i + b0, 0)),
                      pl.BlockSpec((TOP_K, TC, D // 2), lambda i: (0, i, 0)),
                      pl.BlockSpec((TC, TOP_K), lambda i: (i + b0, 0)),
                      _mod_spec(5, TC, b0),
                      pl.BlockSpec((D, FF), lambda i: (0, 0)),
                      pl.BlockSpec((D, FF), lambda i: (0, 0)),
                      pl.BlockSpec((FF, D), lambda i: (0, 0)),
                      pl.BlockSpec((1, D), lambda i: (0, 0))],
            out_specs=pl.BlockSpec((TC, D), lambda i: (i, 0)),
            out_shape=jax.ShapeDtypeStruct((n_rows, D), F32),
            compiler_params=_cparams(1),
            name="combine",
        )(x, h, y_group, wgt, mod, *weights)

    return rows_from(0, NP_TOK, y_rows[0]), rows_from(NP_TOK, NS_TOK, y_rows[1])


def kernel(x_prompt, x_sample, cache_a_k, cache_a_v, state_ret_fwd, state_ret_bwd, cache_c_k, cache_c_v,
           c, c_ctx, norm1_g, norm2_g, ada_w, ada_b, even_w_in, even_w_out, sink_a, ret_decay_fwd,
           ret_decay_bwd, ret_gn_g, odd_w_in, odd_w_out, na_rpb, router_w, router_b, exp_w1, exp_w3,
           exp_w2, sh_w1, sh_w3, sh_w2, final_g):
    x = (x_prompt.reshape(NP_TOK, D), x_sample.reshape(NS_TOK, D))
    cc = jnp.concatenate([c_ctx[None], c, jnp.zeros((8 - 1 - DEC_BATCH, D), F32)], axis=0)
    rope = _rope_tables()
    na_tables = _na_bias_tables(na_rpb[0])
    outs = {}
    for l in range(2):
        mod = _ada(cc, ada_w, ada_b, l)
        if l == 0:
            p = _inproj(x, norm1_g[l], mod, even_w_in[0].astype(BF16), rope, A_Q + A_KV)
            oa_p = _ctx_gqa(p, sink_a[0])
            oa_s = _win_attention(p, cache_a_k[:, 0], cache_a_v[:, 0], sink_a[0])
            zero = jnp.zeros((BATCH, B_HEADS // 2, LANES, LANES), F32)
            ob_p, sf, sb = _retention(p, 0, BATCH, SEQ, ret_decay_fwd[0], ret_decay_bwd[0], ret_gn_g[0], zero, zero)
            ob_s, _, _ = _retention(p, NP_TOK, DEC_BATCH, DEC_SEQ, ret_decay_fwd[0], ret_decay_bwd[0], ret_gn_g[0],
                                    _blockdiag_states(state_ret_fwd[:, 0]), _blockdiag_states(state_ret_bwd[:, 0]))
            w_out = even_w_out[0].astype(BF16)
            parts = [(oa_p, oa_s, w_out[:A_Q]), (ob_p, ob_s, w_out[A_Q:])]
            outs["a_k"] = p[:NP_TOK, A_Q:A_Q + A_KV].reshape(BATCH, 1, SEQ, A_KV_HEADS, HD)
            outs["a_v"] = p[:NP_TOK, A_Q + A_KV:A_Q + 2 * A_KV].reshape(BATCH, 1, SEQ, A_KV_HEADS, HD)
            outs["r_f"] = _diag_states(sf).reshape(BATCH, 1, B_HEADS, HD, HD)
            outs["r_b"] = _diag_states(sb).reshape(BATCH, 1, B_HEADS, HD, HD)
        else:
            p = _inproj(x, norm1_g[l], mod, odd_w_in[0].astype(BF16), rope, 0)
            o_p = _ctx_mha(p)
            o_s = _na_attention(p, cache_c_k[:, 0], cache_c_v[:, 0], na_tables)
            parts = [(o_p, o_s, odd_w_out[0].astype(BF16))]
            outs["c_k"] = p[:NP_TOK, C_W:2 * C_W].reshape(BATCH, 1, SEQ, C_HEADS, HD)
            outs["c_v"] = p[:NP_TOK, 2 * C_W:3 * C_W].reshape(BATCH, 1, SEQ, C_HEADS, HD)
        x_mid, h, dest, gate_t, plan = _outproj(x, parts, mod, norm2_g[l], router_w[l], router_b[l])
        x_rows = _sc_scatter(h, dest, N_EXPERTS * N_TOK)
        if l == 0:
            x_rows, na_tables = lax.optimization_barrier((x_rows, na_tables))
        y = _experts(plan, x_rows, exp_w1, exp_w3, exp_w2, l)
        y_rows = (_sc_gather(y, dest[:, :NP_TOK].reshape(TOP_K * NP_TOK)).reshape(TOP_K, NP_TOK, D // 2),
                  _sc_gather(y, dest[:, NP_TOK:].reshape(TOP_K * NS_TOK)).reshape(TOP_K, NS_TOK, D // 2))
        x = _combine(x_mid, h, y_rows, gate_t.T, mod, sh_w1[l], sh_w3[l], sh_w2[l], final_g, final=(l == 1))
    y_prompt = x[0].reshape(BATCH, SEQ, D)
    y_sample = x[1].reshape(DEC_BATCH, DEC_SEQ, D)
    return (y_prompt, y_sample, outs["a_k"], outs["a_v"], outs["r_f"], outs["r_b"], outs["c_k"], outs["c_v"])
```

```python
import functools

import jax
import jax.numpy as jnp
from jax import lax
from jax.experimental import pallas as pl
from jax.experimental.pallas import tpu as pltpu
from jax.experimental.pallas import tpu_sc as plsc

F32 = jnp.float32
BF16 = jnp.bfloat16
HIGHEST = lax.Precision.HIGHEST

D = 1024
BATCH = 32
SEQ = 256
DEC_BATCH = 4
DEC_SEQ = 4096
PAST = 256
GRID_W = 64
HD = 64
EPS = 1e-6
NEG = -1e30
ROPE_BASE = 10000.0
A_HEADS = 8
A_KV_HEADS = 2
A_Q = A_HEADS * HD
A_KV = A_KV_HEADS * HD
B_HEADS = 8
B_W = B_HEADS * HD
EVEN_IN = A_Q + 2 * A_KV + 4 * B_W
C_HEADS = 16
C_W = C_HEADS * HD
NA_KH = 8
NA_KW = 16
N_EXPERTS = 64
TOP_K = 8
N_GROUPS = 8
TOPK_GROUPS = 4
FF = 256
ROUTED_SCALE = 2.5
MOE_BLOCK = 1024
EXPERT_X_SLOTS = 3
RET_CHUNK = 256
RET_UNROLL = 4
RET_NORM_ROWS = 1024
A_WINDOW = 128

NP_TOK = BATCH * SEQ
NS_TOK = DEC_BATCH * DEC_SEQ
N_TOK = NP_TOK + NS_TOK
N_ASSIGN = N_TOK * TOP_K
N_MOE_BLOCKS = (N_ASSIGN + N_EXPERTS * (MOE_BLOCK - 1) + MOE_BLOCK - 1) // MOE_BLOCK
PLAN_LANES = 512
assert N_TOK % MOE_BLOCK == 0 and N_MOE_BLOCKS <= PLAN_LANES

LANES = 128
TM = 512
TM_OUT = 1024
NA_ROWS = 8
V7X_VMEM_LIMIT = 56 * 1024 * 1024

NT_DIMS = (((1,), (1,)), ((), ()))


def _cparams(n_axes, vmem=V7X_VMEM_LIMIT):
    return pltpu.CompilerParams(dimension_semantics=("arbitrary",) * n_axes, vmem_limit_bytes=vmem)


def _seg_of_block(i, rows):
    row0 = i * rows
    return jnp.where(row0 < NP_TOK, 0, 1 + (row0 - NP_TOK) // DEC_SEQ)


def _mod_spec(chunk, rows=TM, first_block=0):
    return pl.BlockSpec((None, 1, D), lambda i: (_seg_of_block(i + first_block, rows), 0, chunk))


def _pair_specs(width, rows=TM):
    npb = NP_TOK // rows
    nsb = NS_TOK // rows
    return [pl.BlockSpec((rows, width), lambda i: (jnp.minimum(i, npb - 1), 0)),
            pl.BlockSpec((rows, width), lambda i: (jnp.clip(i - npb, 0, nsb - 1), 0))]


def _pick_rows(p_ref, s_ref, rows=TM):
    return jnp.where(pl.program_id(0) < NP_TOK // rows, p_ref[...], s_ref[...])


def _silu(x):
    return x * jax.nn.sigmoid(x)


def _rms(x, g):
    return x * lax.rsqrt(jnp.mean(x * x, axis=-1, keepdims=True) + EPS) * g


def _lane_lo():
    return lax.broadcasted_iota(jnp.int32, (1, LANES), 1) < HD


def _ada_kernel(c_ref, w_ref, b_ref, o_ref):
    a = _silu(c_ref[...])
    o_ref[...] = jnp.dot(a, w_ref[...], preferred_element_type=F32, precision=HIGHEST) + b_ref[...]


def _ada(cc, w, b, layer):
    tn = 1536
    out = pl.pallas_call(
        _ada_kernel,
        grid=(6 * D // tn,),
        in_specs=[pl.BlockSpec((8, D), lambda j: (0, 0)),
                  pl.BlockSpec((None, D, tn), lambda j: (layer, 0, j)),
                  pl.BlockSpec((None, 1, tn), lambda j: (layer, 0, j))],
        out_specs=pl.BlockSpec((8, tn), lambda j: (0, j)),
        out_shape=jax.ShapeDtypeStruct((8, 6 * D), F32),
        compiler_params=_cparams(1),
        name="ada",
    )(cc, w, b.reshape(b.shape[0], 1, 6 * D))
    return out.reshape(8, 1, 6 * D)


def _inproj_kernel(xp_ref, xs_ref, g_ref, shift_ref, scale_ref, w_ref, cos_ref, sin_ref, o_ref, *, rope_cols):
    h = _rms(_pick_rows(xp_ref, xs_ref), g_ref[...]) * (1.0 + scale_ref[...]) + shift_ref[...]
    o = jnp.dot(h.astype(BF16), w_ref[...], preferred_element_type=F32)
    if rope_cols:
        cos = cos_ref[...]
        sin = sin_ref[...]
        lane = lax.broadcasted_iota(jnp.int32, (1, LANES), 1)
        first = (lane % 32) < 16
        for c in range(rope_cols // LANES):
            oc = o[:, c * LANES:(c + 1) * LANES]
            partner = jnp.where(first, pltpu.roll(oc, LANES - 16, 1), pltpu.roll(oc, 16, 1))
            o_ref[:, c * LANES:(c + 1) * LANES] = oc * cos + partner * sin
        o_ref[:, rope_cols:] = o[:, rope_cols:]
    else:
        o_ref[...] = o


def _rope_tables():
    half = HD // 2
    inv = ROPE_BASE ** (-jnp.arange(0, half, 2, dtype=F32) / half)
    t = jnp.arange(DEC_SEQ)
    ang_r = (t // GRID_W).astype(F32)[:, None] * inv[None]
    ang_c = (t % GRID_W).astype(F32)[:, None] * inv[None]

    def head(fn_r, fn_c, sign):
        return jnp.concatenate([sign[0] * fn_r, sign[1] * fn_r, sign[0] * fn_c, sign[1] * fn_c], axis=-1)

    cos = head(jnp.cos(ang_r), jnp.cos(ang_c), (1.0, 1.0))
    sin = head(jnp.sin(ang_r), jnp.sin(ang_c), (-1.0, 1.0))
    cos = jnp.concatenate([jnp.ones((TM, HD), F32), cos], axis=0)
    sin = jnp.concatenate([jnp.zeros((TM, HD), F32), sin], axis=0)
    return jnp.tile(cos, (1, 2)), jnp.tile(sin, (1, 2))


def _inproj(x, g, mod, w_bf16, rope, rope_cols):
    n_out = w_bf16.shape[1]
    npb = NP_TOK // TM
    spb = DEC_SEQ // TM

    def rope_map(i):
        return (jnp.where(i < npb, 0, 1 + (i - npb) % spb), 0)

    return pl.pallas_call(
        functools.partial(_inproj_kernel, rope_cols=rope_cols),
        grid=(N_TOK // TM,),
        in_specs=_pair_specs(D) + [
                  pl.BlockSpec((1, D), lambda i: (0, 0)),
                  _mod_spec(0), _mod_spec(1),
                  pl.BlockSpec((D, n_out), lambda i: (0, 0)),
                  pl.BlockSpec((TM, LANES), rope_map),
                  pl.BlockSpec((TM, LANES), rope_map)],
        out_specs=pl.BlockSpec((TM, n_out), lambda i: (i, 0)),
        out_shape=jax.ShapeDtypeStruct((N_TOK, n_out), F32),
        compiler_params=_cparams(1),
        name="inproj",
    )(x[0], x[1], g.reshape(1, D), mod, mod, w_bf16, rope[0], rope[1])


def _softmax_av(s_list, v_list, sink=None):
    mx = s_list[0].max(axis=-1, keepdims=True)
    for s in s_list[1:]:
        mx = jnp.maximum(mx, s.max(axis=-1, keepdims=True))
    if sink is not None:
        mx = jnp.maximum(mx, sink)
    den = jnp.exp(sink - mx) if sink is not None else 0.0
    acc = None
    for s, v in zip(s_list, v_list):
        p = jnp.exp(s - mx)
        den = den + p.sum(axis=-1, keepdims=True)
        pv = jnp.dot(p.astype(BF16), v, preferred_element_type=F32)
        acc = pv if acc is None else acc + pv
    return acc / den


def _dup_half(x, j, lo):
    xr = pltpu.roll(x, HD, 1)
    return jnp.where(lo, x, xr) if j == 0 else jnp.where(lo, xr, x)


def _stack_heads(q_ref, heads, lo, scale, rows=slice(None)):
    parts = []
    for h in heads:
        qp = q_ref[rows, (h // 2) * LANES:(h // 2 + 1) * LANES]
        keep = lo if h % 2 == 0 else jnp.logical_not(lo)
        parts.append(jnp.where(keep, qp, 0.0) * scale)
    return jnp.concatenate(parts, axis=0).astype(BF16)


def _sink_column(sink_ref, heads, rows):
    return jnp.concatenate([jnp.full((rows, 1), sink_ref[h], F32) for h in heads], axis=0)


def _ctx_gqa_kernel(sink_ref, q_ref, k_ref, v_ref, o_ref):
    lo = _lane_lo()
    k = k_ref[...]
    v = v_ref[...]
    group = A_HEADS // A_KV_HEADS
    scores = []
    for j in range(A_KV_HEADS):
        q = _stack_heads(q_ref, list(range(group * j, group * (j + 1))), lo, HD ** -0.5)
        scores.append(lax.dot_general(q, _dup_half(k, j, lo).astype(BF16), NT_DIMS, preferred_element_type=F32))
    s = jnp.concatenate(scores, axis=0)
    sink = _sink_column(sink_ref, list(range(A_HEADS)), SEQ)
    mx = jnp.maximum(s.max(axis=-1, keepdims=True), sink)
    e = jnp.exp(s - mx)
    den = jnp.exp(sink - mx) + e.sum(axis=-1, keepdims=True)
    e = e.astype(BF16)
    rows_per_group = group * SEQ
    for j in range(A_KV_HEADS):
        rows = slice(j * rows_per_group, (j + 1) * rows_per_group)
        o = jnp.dot(e[rows], _dup_half(v, j, lo).astype(BF16), preferred_element_type=F32) / den[rows]
        for t in range(group // 2):
            pair = (group * j) // 2 + t
            o_ref[:, pair * LANES:(pair + 1) * LANES] = jnp.where(
                lo, o[(2 * t) * SEQ:(2 * t + 1) * SEQ], o[(2 * t + 1) * SEQ:(2 * t + 2) * SEQ])


def _ctx_gqa(p, sink):
    return pl.pallas_call(
        _ctx_gqa_kernel,
        grid_spec=pltpu.PrefetchScalarGridSpec(
            num_scalar_prefetch=1,
            grid=(BATCH,),
            in_specs=[pl.BlockSpec((SEQ, A_Q), lambda b, s: (b, 0)),
                      pl.BlockSpec((SEQ, A_KV), lambda b, s: (b, A_Q // A_KV)),
                      pl.BlockSpec((SEQ, A_KV), lambda b, s: (b, A_Q // A_KV + 1))],
            out_specs=pl.BlockSpec((SEQ, A_Q), lambda b, s: (b, 0))),
        out_shape=jax.ShapeDtypeStruct((NP_TOK, A_Q), F32),
        compiler_params=_cparams(1),
        name="ctx_gqa",
    )(sink, p, p, p)


def _win_kernel(sink_ref, q_ref, kp_ref, kc_ref, kn_ref, vp_ref, vc_ref, vn_ref, ck_ref, cv_ref, o_ref):
    i = pl.program_id(1)
    lo = _lane_lo()
    k = jnp.concatenate([kp_ref[...], kc_ref[...], kn_ref[...]], axis=0)
    v = jnp.concatenate([vp_ref[...], vc_ref[...], vn_ref[...]], axis=0)
    ck = ck_ref[...]
    cv = cv_ref[...]
    group = A_HEADS // A_KV_HEADS
    half_q = WIN_Q // 2
    span = half_q + 2 * A_WINDOW
    s_loc, s_ctx, sinks, values = [], [], [], []
    for j in range(A_KV_HEADS):
        heads = list(range(group * j, group * (j + 1)))
        kd = _dup_half(k, j, lo).astype(BF16)
        ckd = _dup_half(ck, j, lo).astype(BF16)
        vd = _dup_half(v, j, lo).astype(BF16)
        cvd = _dup_half(cv, j, lo).astype(BF16)
        for part in range(2):
            key0 = part * half_q
            qpos = i * WIN_Q + key0 + lax.broadcasted_iota(jnp.int32, (half_q, span), 0)
            kpos = i * WIN_Q - A_WINDOW + key0 + lax.broadcasted_iota(jnp.int32, (half_q, span), 1)
            valid = (jnp.abs(kpos - qpos) <= A_WINDOW) & (kpos >= 0) & (kpos < DEC_SEQ)
            valid = jnp.concatenate([valid] * group, axis=0)
            q = _stack_heads(q_ref, heads, lo, HD ** -0.5, pl.ds(key0, half_q))
            s = lax.dot_general(q, kd[key0:key0 + span], NT_DIMS, preferred_element_type=F32)
            s_loc.append(jnp.where(valid, s, NEG))
            s_ctx.append(lax.dot_general(q, ckd, NT_DIMS, preferred_element_type=F32))
            sinks.append(_sink_column(sink_ref, heads, half_q))
            values.append((j, part, vd[key0:key0 + span], cvd))
    s_loc = jnp.concatenate(s_loc, axis=0)
    s_ctx = jnp.concatenate(s_ctx, axis=0)
    sink = jnp.concatenate(sinks, axis=0)
    mx = jnp.maximum(jnp.maximum(s_loc.max(axis=-1, keepdims=True), s_ctx.max(axis=-1, keepdims=True)), sink)
    p_loc = jnp.exp(s_loc - mx)
    p_ctx = jnp.exp(s_ctx - mx)
    den = p_loc.sum(axis=-1, keepdims=True) + p_ctx.sum(axis=-1, keepdims=True) + jnp.exp(sink - mx)
    p_loc = p_loc.astype(BF16)
    p_ctx = p_ctx.astype(BF16)
    rows_per_unit = group * half_q
    for unit, (j, part, vd, cvd) in enumerate(values):
        rows = slice(unit * rows_per_unit, (unit + 1) * rows_per_unit)
        o = (jnp.dot(p_loc[rows], vd, preferred_element_type=F32)
             + jnp.dot(p_ctx[rows], cvd, preferred_element_type=F32)) / den[rows]
        for t in range(group // 2):
            pair = (group * j) // 2 + t
            o_ref[pl.ds(part * half_q, half_q), pair * LANES:(pair + 1) * LANES] = jnp.where(
                lo, o[(2 * t) * half_q:(2 * t + 1) * half_q], o[(2 * t + 1) * half_q:(2 * t + 2) * half_q])


WIN_Q = 256


def _win_attention(p, cache_k, cache_v, sink):
    nblk = DEC_SEQ // WIN_Q
    side = WIN_Q // A_WINDOW
    nside = DEC_SEQ // A_WINDOW
    base = NP_TOK // WIN_Q
    side_base = NP_TOK // A_WINDOW
    kcol = A_Q // A_KV

    def main_spec(col):
        return pl.BlockSpec((WIN_Q, A_KV), lambda b, i, s: (base + b * nblk + i, col))

    def side_spec(col, off):
        return pl.BlockSpec((A_WINDOW, A_KV),
                            lambda b, i, s: (side_base + b * nside + jnp.clip(side * i + off, 0, nside - 1), col))

    ctx_spec = pl.BlockSpec((None, PAST, A_KV), lambda b, i, s: (b, 0, 0))
    return pl.pallas_call(
        _win_kernel,
        grid_spec=pltpu.PrefetchScalarGridSpec(
            num_scalar_prefetch=1,
            grid=(DEC_BATCH, nblk),
            in_specs=[pl.BlockSpec((WIN_Q, A_Q), lambda b, i, s: (base + b * nblk + i, 0)),
                      side_spec(kcol, -1), main_spec(kcol), side_spec(kcol, side),
                      side_spec(kcol + 1, -1), main_spec(kcol + 1), side_spec(kcol + 1, side),
                      ctx_spec, ctx_spec],
            out_specs=pl.BlockSpec((WIN_Q, A_Q), lambda b, i, s: (b * nblk + i, 0))),
        out_shape=jax.ShapeDtypeStruct((NS_TOK, A_Q), F32),
        compiler_params=_cparams(2),
        name="win_attn",
    )(sink, p, p, p, p, p, p, p, cache_k.reshape(DEC_BATCH, PAST, A_KV), cache_v.reshape(DEC_BATCH, PAST, A_KV))


def _ret_kernel(df_ref, db_ref, q_ref, k_ref, v_ref, g_ref, gn_ref, s0f_ref, s0b_ref,
                o_ref, sf_ref, sb_ref, of_scr, ob_scr, *, length):
    c_len = RET_CHUNK
    n = length // c_len
    lo = _lane_lo()
    hi = jnp.logical_not(lo)
    row = lax.broadcasted_iota(jnp.int32, (c_len, c_len), 0)
    col = lax.broadcasted_iota(jnp.int32, (c_len, c_len), 1)
    rowp = lax.broadcasted_iota(jnp.int32, (LANES, LANES), 0)
    colp = lax.broadcasted_iota(jnp.int32, (LANES, LANES), 1)
    blockdiag = (rowp < HD) == (colp < HD)
    idx = lax.broadcasted_iota(jnp.int32, (c_len, 1), 0).astype(F32)

    def direction(dec_ref, forward):
        lg = -jnp.exp(dec_ref[...])
        diff = (row - col) if forward else (col - row)
        keep = (diff >= 0) if forward else (diff > 0)
        dist = jnp.maximum(diff, 0).astype(F32)
        dm = jnp.concatenate([jnp.where(keep, jnp.exp(dist * lg[:, off:off + 1]), 0.0) for off in (0, HD)], axis=0)
        if forward:
            xi = jnp.exp((idx + 1.0) * lg)
            zeta = jnp.exp((c_len - 1.0 - idx) * lg)
        else:
            xi = jnp.exp((c_len - idx) * lg)
            zeta = jnp.exp(idx * lg)
        return dm, xi, zeta, jnp.exp(c_len * lg)

    def chunk(c, state, consts):
        dm, xi, zeta, gch = consts
        rows = pl.ds(pl.multiple_of(c * c_len, c_len), c_len)
        qc = q_ref[rows, :]
        kc = k_ref[rows, :] * HD ** -0.5
        vc = v_ref[rows, :].astype(BF16)
        kb = kc.astype(BF16)
        q2 = jnp.concatenate([jnp.where(lo, qc, 0.0), jnp.where(hi, qc, 0.0)], axis=0).astype(BF16)
        inner = lax.dot_general(q2, kb, NT_DIMS, preferred_element_type=F32) * dm
        kz_t = (kc * zeta).T
        res = jnp.dot(jnp.concatenate([inner, kz_t], axis=0).astype(BF16), vc, preferred_element_type=F32)
        cross = jnp.dot(qc.astype(BF16), state.astype(BF16), preferred_element_type=F32) * xi
        o = jnp.where(lo, res[:c_len], res[c_len:2 * c_len]) + cross
        state = gch * state + jnp.where(blockdiag, res[2 * c_len:], 0.0)
        return rows, o, state

    cf = direction(df_ref, True)
    cb = direction(db_ref, False)

    def scan_body(t, states):
        rows_f, o_f, state_f = chunk(t, states[0], cf)
        of_scr[rows_f, :] = o_f
        rows_b, o_b, state_b = chunk(n - 1 - t, states[1], cb)
        ob_scr[rows_b, :] = o_b
        return state_f, state_b

    state_f, state_b = lax.fori_loop(0, n, scan_body, (s0f_ref[...], s0b_ref[...]), unroll=min(n, RET_UNROLL))
    sf_ref[...] = state_f
    sb_ref[...] = state_b

    gn = gn_ref[...]
    norm_rows = min(RET_NORM_ROWS, length)
    n_norm = length // norm_rows

    def per_head(x):
        a = jnp.where(lo, x, 0.0).sum(axis=-1, keepdims=True)
        b = jnp.where(hi, x, 0.0).sum(axis=-1, keepdims=True)
        return jnp.where(lo, a, b) * (1.0 / HD)

    def norm_body(t, carry):
        rows = pl.ds(pl.multiple_of(t * norm_rows, norm_rows), norm_rows)
        o = of_scr[rows, :] + ob_scr[rows, :]
        d = o - per_head(o)
        y = d * lax.rsqrt(per_head(d * d) + EPS) * gn
        o_ref[rows, :] = _silu(g_ref[rows, :]) * y
        return carry

    lax.fori_loop(0, n_norm, norm_body, 0)


def _pair_lanes(v):
    return jnp.repeat(v.astype(F32), HD).reshape(B_HEADS // 2, 1, LANES)


def _blockdiag_states(s):
    b = s.shape[0]
    s = s.astype(F32).reshape(b, B_HEADS // 2, 2, HD, HD)
    z = jnp.zeros_like(s[:, :, 0])
    top = jnp.concatenate([s[:, :, 0], z], axis=-1)
    bot = jnp.concatenate([z, s[:, :, 1]], axis=-1)
    return jnp.concatenate([top, bot], axis=-2)


def _diag_states(sp):
    b = sp.shape[0]
    s = jnp.stack([sp[:, :, :HD, :HD], sp[:, :, HD:, HD:]], axis=2)
    return s.reshape(b, B_HEADS, HD, HD)


def _retention(p, row_base, batch, length, dec_f, dec_b, gn_g, s0f, s0b):
    npairs = B_HEADS // 2
    blk0 = row_base // length
    qcol = (A_Q + 2 * A_KV) // LANES

    def col_spec(off):
        return pl.BlockSpec((length, LANES), lambda b, h: (blk0 + b, qcol + off * npairs + h))

    lane_spec = pl.BlockSpec((None, 1, LANES), lambda b, h: (h, 0, 0))
    state_spec = pl.BlockSpec((None, None, LANES, LANES), lambda b, h: (b, h, 0, 0))
    state_shape = jax.ShapeDtypeStruct((batch, npairs, LANES, LANES), F32)
    return pl.pallas_call(
        functools.partial(_ret_kernel, length=length),
        grid=(batch, npairs),
        in_specs=[lane_spec, lane_spec, col_spec(0), col_spec(1), col_spec(2), col_spec(3), lane_spec,
                  state_spec, state_spec],
        out_specs=[pl.BlockSpec((length, LANES), lambda b, h: (b, h)), state_spec, state_spec],
        out_shape=[jax.ShapeDtypeStruct((batch * length, B_W), F32), state_shape, state_shape],
        scratch_shapes=[pltpu.VMEM((length, LANES), F32), pltpu.VMEM((length, LANES), F32)],
        compiler_params=_cparams(2),
        name="retention",
    )(_pair_lanes(dec_f), _pair_lanes(dec_b), p, p, p, p, gn_g.reshape(npairs, 1, LANES), s0f, s0b)


def _ctx_mha_kernel(q_ref, k_ref, v_ref, o_ref):
    lo = _lane_lo()
    for pair in range(C_HEADS // 2):
        cols = slice(pair * LANES, (pair + 1) * LANES)
        q = _stack_heads(q_ref, [2 * pair, 2 * pair + 1], lo, HD ** -0.5)
        s = lax.dot_general(q, k_ref[:, cols].astype(BF16), NT_DIMS, preferred_element_type=F32)
        o = _softmax_av([s], [v_ref[:, cols].astype(BF16)])
        o_ref[:, cols] = jnp.where(lo, o[:SEQ], o[SEQ:])


def _ctx_mha(p):
    return pl.pallas_call(
        _ctx_mha_kernel,
        grid=(BATCH,),
        in_specs=[pl.BlockSpec((SEQ, C_W), lambda b: (b, 0)),
                  pl.BlockSpec((SEQ, C_W), lambda b: (b, 1)),
                  pl.BlockSpec((SEQ, C_W), lambda b: (b, 2))],
        out_specs=pl.BlockSpec((SEQ, C_W), lambda b: (b, 0)),
        out_shape=jax.ShapeDtypeStruct((NP_TOK, C_W), F32),
        compiler_params=_cparams(1),
        name="ctx_mha",
    )(p, p, p)


NA_WIN_ROWS = 2 * NA_ROWS
NA_WIN = NA_WIN_ROWS * GRID_W
NA_QROWS = NA_ROWS * GRID_W
NA_PAD_ROWS = NA_KH // 2
NA_TABLE = 1536
NA_SPAN = 768


def _na_kernel(q_ref, kp_ref, km_ref, kn_ref, vp_ref, vm_ref, vn_ref, ck_ref, cv_ref, ue_ref, uo_ref, o_ref):
    r0 = pl.program_id(2) * NA_ROWS
    n_rows = DEC_SEQ // GRID_W
    lo = _lane_lo()
    k = jnp.concatenate([kp_ref[...], km_ref[...], kn_ref[...]], axis=0).astype(BF16)
    v = jnp.concatenate([vp_ref[...], vm_ref[...], vn_ref[...]], axis=0).astype(BF16)
    ck = ck_ref[...].astype(BF16)
    cv = cv_ref[...].astype(BF16)
    q = q_ref[...] * HD ** -0.5
    outs = []
    for half, keep in enumerate((lo, jnp.logical_not(lo))):
        qh = jnp.where(keep, q, 0.0).astype(BF16)
        s_ctx = lax.dot_general(qh, ck, NT_DIMS, preferred_element_type=F32)
        o_parts = []
        for part in range(2):
            key0 = part * (NA_WIN - NA_SPAN)
            klane = key0 + lax.broadcasted_iota(jnp.int32, (1, NA_SPAN), 1)
            qrows = slice(part * NA_QROWS // 2, (part + 1) * NA_QROWS // 2)
            s = lax.dot_general(qh[qrows], k[key0:key0 + NA_SPAN], NT_DIMS, preferred_element_type=F32)
            p_loc, p_ctx, den = [], [], []
            for t in range(NA_ROWS // 2):
                rq = part * NA_ROWS // 2 + t
                start = NA_KH - 1 - rq
                if start % 2 == 0:
                    u = ue_ref[half, :, start * GRID_W + key0:start * GRID_W + key0 + NA_SPAN]
                else:
                    u = uo_ref[half, :, (start - 1) * GRID_W + key0:(start - 1) * GRID_W + key0 + NA_SPAN]
                r = r0 + rq
                first = jnp.clip(r - NA_KH // 2, 0, n_rows - NA_KH)
                lane0 = (first - r0 + NA_PAD_ROWS) * GRID_W
                in_rows = (klane >= lane0) & (klane < lane0 + NA_KH * GRID_W)
                sl = jnp.where(in_rows, s[t * GRID_W:(t + 1) * GRID_W] + u, NEG)
                sc = s_ctx[rq * GRID_W:(rq + 1) * GRID_W]
                mx = jnp.maximum(sl.max(axis=-1, keepdims=True), sc.max(axis=-1, keepdims=True))
                el = jnp.exp(sl - mx)
                ec = jnp.exp(sc - mx)
                den.append(el.sum(axis=-1, keepdims=True) + ec.sum(axis=-1, keepdims=True))
                p_loc.append(el.astype(BF16))
                p_ctx.append(ec.astype(BF16))
            acc = (jnp.dot(jnp.concatenate(p_loc, axis=0), v[key0:key0 + NA_SPAN], preferred_element_type=F32)
                   + jnp.dot(jnp.concatenate(p_ctx, axis=0), cv, preferred_element_type=F32))
            o_parts.append(acc / jnp.concatenate(den, axis=0))
        outs.append(jnp.concatenate(o_parts, axis=0))
    o_ref[...] = jnp.where(lo, outs[0], outs[1])


def _na_bias_tables(rpb):
    cq = jnp.arange(GRID_W)
    ck = jnp.arange(GRID_W)
    dc = jnp.clip(ck[None] - cq[:, None], -(NA_KW - 1), NA_KW - 1) + NA_KW - 1
    cs = jnp.clip(cq - NA_KW // 2, 0, GRID_W - NA_KW)
    col_ok = (ck[None] >= cs[:, None]) & (ck[None] < cs[:, None] + NA_KW)
    t = rpb.astype(F32)[:, :, dc]
    t = jnp.where(col_ok[None, None], t, NEG).transpose(0, 2, 1, 3)
    n_dr = 2 * NA_KH - 1
    blocks = NA_TABLE // GRID_W
    t = jnp.pad(t, ((0, 0), (0, 0), (NA_PAD_ROWS, blocks - n_dr - NA_PAD_ROWS), (0, 0)), constant_values=NEG)
    ue = t.reshape(C_HEADS, GRID_W, NA_TABLE)
    uo = jnp.concatenate([ue[..., GRID_W:], jnp.full((C_HEADS, GRID_W, GRID_W), NEG, F32)], axis=-1)
    return ue, uo


def _na_attention(p, cache_k, cache_v, rpb):
    npairs = C_HEADS // 2
    nrb = DEC_SEQ // NA_QROWS
    half = NA_QROWS // 2
    qbase = NP_TOK // NA_QROWS
    hbase = NP_TOK // half
    kcol = C_W // LANES
    ue, uo = _na_bias_tables(rpb)

    def main_spec(col0):
        return pl.BlockSpec((NA_QROWS, LANES), lambda b, h, r: (qbase + b * nrb + r, col0 + h))

    def side_spec(col0, off):
        return pl.BlockSpec((half, LANES),
                            lambda b, h, r: (hbase + b * 2 * nrb + jnp.clip(2 * r + off, 0, 2 * nrb - 1), col0 + h))

    ctx_spec = pl.BlockSpec((None, PAST, LANES), lambda b, h, r: (b, 0, h))
    tab_spec = pl.BlockSpec((2, GRID_W, NA_TABLE), lambda b, h, r: (h, 0, 0))
    return pl.pallas_call(
        _na_kernel,
        grid=(DEC_BATCH, npairs, nrb),
        in_specs=[main_spec(0),
                  side_spec(kcol, -1), main_spec(kcol), side_spec(kcol, 2),
                  side_spec(2 * kcol, -1), main_spec(2 * kcol), side_spec(2 * kcol, 2),
                  ctx_spec, ctx_spec, tab_spec, tab_spec],
        out_specs=pl.BlockSpec((NA_QROWS, LANES), lambda b, h, r: (b * nrb + r, h)),
        out_shape=jax.ShapeDtypeStruct((NS_TOK, C_W), F32),
        compiler_params=_cparams(3),
        name="na_attn",
    )(p, p, p, p, p, p, p, cache_k.reshape(DEC_BATCH, PAST, C_W), cache_v.reshape(DEC_BATCH, PAST, C_W), ue, uo)


def _route(biased, scores):
    t = biased.shape[1]
    per_group = N_EXPERTS // N_GROUPS
    i8 = lax.broadcasted_iota(jnp.int32, (per_group, t), 0)
    g_rows = []
    for g in range(N_GROUPS):
        bg = biased[g * per_group:(g + 1) * per_group]
        m1 = bg.max(axis=0, keepdims=True)
        first = jnp.where(bg == m1, i8, per_group).min(axis=0, keepdims=True)
        m2 = jnp.where(i8 == first, -jnp.inf, bg).max(axis=0, keepdims=True)
        g_rows.append(m1 + m2)
    g_top = jnp.concatenate(g_rows, axis=0)
    gi = lax.broadcasted_iota(jnp.int32, g_top.shape, 0)
    g_sel = jnp.zeros(g_top.shape, jnp.int32)
    cur = g_top
    for _ in range(TOPK_GROUPS):
        m = cur.max(axis=0, keepdims=True)
        hit = gi == jnp.where(cur == m, gi, N_GROUPS).min(axis=0, keepdims=True)
        g_sel = jnp.where(hit, 1, g_sel)
        cur = jnp.where(hit, -jnp.inf, cur)
    e_sel = jnp.concatenate([jnp.broadcast_to(g_sel[g:g + 1], (per_group, t)) for g in range(N_GROUPS)], axis=0)
    cur = jnp.where(e_sel > 0, biased, NEG)
    ei = lax.broadcasted_iota(jnp.int32, cur.shape, 0)
    ids, gates, hits = [], [], []
    for _ in range(TOP_K):
        m = cur.max(axis=0, keepdims=True)
        f = jnp.where(cur == m, ei, N_EXPERTS).min(axis=0, keepdims=True)
        hit = ei == f
        ids.append(f)
        hits.append(hit)
        gates.append(jnp.where(hit, scores, 0.0).sum(axis=0, keepdims=True))
        cur = jnp.where(hit, -jnp.inf, cur)
    gate = jnp.concatenate(gates, axis=0)
    gate = gate / gate.sum(axis=0, keepdims=True) * ROUTED_SCALE
    return jnp.concatenate(ids, axis=0), gate, hits


def _pack_bf16_pairs(h):
    bits = lax.bitcast_convert_type(h.astype(BF16).astype(F32), jnp.uint32)
    return bits[:, :D // 2] | (bits[:, D // 2:] >> 16)


def _unpack_bf16_pairs(xp):
    hi = lax.bitcast_convert_type(xp & jnp.uint32(0xFFFF0000), F32).astype(BF16)
    lo = lax.bitcast_convert_type(xp << 16, F32).astype(BF16)
    return hi, lo


def _dot_halves(hi, lo, w_ref):
    return (jnp.dot(hi, w_ref[:D // 2, :], preferred_element_type=F32)
            + jnp.dot(lo, w_ref[D // 2:, :], preferred_element_type=F32))


def _outproj_kernel(*refs, n_parts):
    xp_ref, xs_ref = refs[:2]
    part_refs = refs[2:2 + 3 * n_parts]
    gate_ref, shift_ref, scale_ref, g2_ref, rw_ref, rb_ref = refs[2 + 3 * n_parts:8 + 3 * n_parts]
    xo_ref, h_ref, dest_ref, wgt_ref, plan_ref, cnt_ref = refs[8 + 3 * n_parts:]
    step = pl.program_id(0)

    @pl.when(step == 0)
    def _():
        cnt_ref[...] = jnp.zeros_like(cnt_ref)
        plan_ref[...] = jnp.zeros_like(plan_ref)

    y = None
    for t in range(n_parts):
        ap_ref, as_ref, w_ref = part_refs[3 * t:3 * t + 3]
        d = jnp.dot(_pick_rows(ap_ref, as_ref, TM_OUT).astype(BF16), w_ref[...], preferred_element_type=F32)
        y = d if y is None else y + d
    x = _pick_rows(xp_ref, xs_ref, TM_OUT) + gate_ref[...] * y
    xo_ref[...] = x
    h = _rms(x, g2_ref[...]) * (1.0 + scale_ref[...]) + shift_ref[...]
    h_ref[...] = _pack_bf16_pairs(h)
    h_hi = h.astype(BF16)
    h_lo = (h - h_hi.astype(F32)).astype(BF16)
    rw = rw_ref[...]
    rw_hi = rw.astype(BF16)
    rw_lo = (rw - rw_hi.astype(F32)).astype(BF16)
    logits = (lax.dot_general(rw_hi, h_hi, NT_DIMS, preferred_element_type=F32)
              + lax.dot_general(rw_hi, h_lo, NT_DIMS, preferred_element_type=F32)
              + lax.dot_general(rw_lo, h_hi, NT_DIMS, preferred_element_type=F32))
    scores = jax.nn.sigmoid(logits)
    _, gate, hits = _route(scores + rb_ref[...], scores)
    wgt_ref[...] = gate
    chosen = hits[0]
    for hit in hits[1:]:
        chosen = chosen | hit
    m = jnp.where(chosen, 1.0, 0.0)
    before = (lax.broadcasted_iota(jnp.int32, (TM_OUT, TM_OUT), 0)
              < lax.broadcasted_iota(jnp.int32, (TM_OUT, TM_OUT), 1))
    prefix = jnp.dot(m.astype(BF16), jnp.where(before, 1.0, 0.0).astype(BF16), preferred_element_type=F32)
    e_base = (lax.broadcasted_iota(jnp.int32, (N_EXPERTS, 1), 0) * N_TOK).astype(F32)
    row_all = prefix + (cnt_ref[...] + e_base)
    dest_ref[...] = jnp.concatenate(
        [jnp.where(hit, row_all, 0.0).sum(axis=0, keepdims=True) for hit in hits], axis=0).astype(jnp.int32)
    cnt_ref[...] += m.sum(axis=1, keepdims=True)

    @pl.when(step == pl.num_programs(0) - 1)
    def _():
        _block_plan(cnt_ref[...], plan_ref)


def _block_plan(counts, plan_ref):
    cap_blocks = N_TOK // MOE_BLOCK
    nblk = ((counts.astype(jnp.int32) + (MOE_BLOCK - 1)) // MOE_BLOCK).astype(F32)
    lower = (lax.broadcasted_iota(jnp.int32, (N_EXPERTS, N_EXPERTS), 0)
             >= lax.broadcasted_iota(jnp.int32, (N_EXPERTS, N_EXPERTS), 1))
    cum = jnp.dot(jnp.where(lower, 1.0, 0.0).astype(BF16), jnp.broadcast_to(nblk, (N_EXPERTS, LANES)).astype(BF16),
                  preferred_element_type=F32)[:, :1]
    n_used = cum[N_EXPERTS - 1:, :]
    slot = jnp.minimum(lax.broadcasted_iota(jnp.int32, (1, PLAN_LANES), 1).astype(F32), n_used - 1.0)
    done = cum <= slot
    expert = jnp.where(done, 1.0, 0.0).sum(axis=0, keepdims=True)
    blocks_before = jnp.where(done, nblk, 0.0).sum(axis=0, keepdims=True)
    plan_ref[0:1, :] = (expert * cap_blocks + (slot - blocks_before)).astype(jnp.int32)
    plan_ref[1:2, :] = expert.astype(jnp.int32)
    plan_ref[2:3, :] = jnp.broadcast_to(n_used, (1, PLAN_LANES)).astype(jnp.int32)
    run_end = jnp.where(cum > slot, cum, jnp.inf).min(axis=0, keepdims=True)
    plan_ref[3:4, :] = jnp.minimum(jnp.where(cum <= run_end, 1.0, 0.0).sum(axis=0, keepdims=True),
                                   N_EXPERTS - 1.0).astype(jnp.int32)
    plan_ref[4:5, :] = jnp.where(run_end < n_used, 1, 0).astype(jnp.int32)


def _outproj(x, parts, mod, g2, router_w, router_b):
    in_specs = _pair_specs(D, TM_OUT)
    args = [x[0], x[1]]
    for ap, a_s, w in parts:
        width = ap.shape[1]
        in_specs += _pair_specs(width, TM_OUT) + [pl.BlockSpec((width, D), lambda i: (0, 0))]
        args += [ap, a_s, w]
    in_specs += [_mod_spec(2, TM_OUT), _mod_spec(3, TM_OUT), _mod_spec(4, TM_OUT),
                 pl.BlockSpec((1, D), lambda i: (0, 0)),
                 pl.BlockSpec((N_EXPERTS, D), lambda i: (0, 0)),
                 pl.BlockSpec((N_EXPERTS, 1), lambda i: (0, 0))]
    args += [mod, mod, mod, g2.reshape(1, D), router_w.T, router_b.reshape(N_EXPERTS, 1)]
    return pl.pallas_call(
        functools.partial(_outproj_kernel, n_parts=len(parts)),
        grid=(N_TOK // TM_OUT,),
        in_specs=in_specs,
        out_specs=[pl.BlockSpec((TM_OUT, D), lambda i: (i, 0)),
                   pl.BlockSpec((TM_OUT, D // 2), lambda i: (i, 0)),
                   pl.BlockSpec((TOP_K, TM_OUT), lambda i: (0, i)),
                   pl.BlockSpec((TOP_K, TM_OUT), lambda i: (0, i)),
                   pl.BlockSpec((8, PLAN_LANES), lambda i: (0, 0))],
        out_shape=[jax.ShapeDtypeStruct((N_TOK, D), F32),
                   jax.ShapeDtypeStruct((N_TOK, D // 2), jnp.uint32),
                   jax.ShapeDtypeStruct((TOP_K, N_TOK), jnp.int32),
                   jax.ShapeDtypeStruct((TOP_K, N_TOK), F32),
                   jax.ShapeDtypeStruct((8, PLAN_LANES), jnp.int32)],
        scratch_shapes=[pltpu.VMEM((N_EXPERTS, 1), F32)],
        compiler_params=_cparams(1),
        name="outproj_router",
    )(*args)


def _experts_kernel(br_ref, be_ref, nu_ref, ne_ref, hn_ref, x_hbm, w1_hbm, w3_hbm, w2_hbm, o_ref,
                    w1b, w3b, w2b, w1f, w3f, w2f, xbuf, xsem, wsem, *, layer):
    i = pl.program_id(0)
    n_steps = pl.num_programs(0)
    e = be_ref[i]
    prev = be_ref[jnp.maximum(i - 1, 0)]

    def x_copy(step):
        slot = step % EXPERT_X_SLOTS
        rows = pl.ds(pl.multiple_of(br_ref[step] * MOE_BLOCK, MOE_BLOCK), MOE_BLOCK)
        return pltpu.make_async_copy(x_hbm.at[rows], xbuf.at[slot], xsem.at[slot])

    @pl.when(i == 0)
    def _():
        x_copy(0).start()
        x_copy(1).start()

    @pl.when(i + 2 < n_steps)
    def _():
        x_copy(i + 2).start()

    x_copy(i).wait()

    def w_copies(expert):
        return [pltpu.make_async_copy(src.at[layer, expert], dst, wsem.at[k])
                for k, (src, dst) in enumerate(((w1_hbm, w1f), (w3_hbm, w3f), (w2_hbm, w2f)))]

    @pl.when(i == 0)
    def _():
        for c in w_copies(e):
            c.start()

    @pl.when((i == 0) | (e != prev))
    def _():
        for c in w_copies(e):
            c.wait()
        w1b[...] = w1f[...].astype(BF16)
        w3b[...] = w3f[...].astype(BF16)
        w2b[...] = w2f[...].astype(BF16)

        @pl.when(hn_ref[i] > 0)
        def _():
            for c in w_copies(ne_ref[i]):
                c.start(priority=1)

    @pl.when(i < nu_ref[0])
    def _():
        hi, lo = _unpack_bf16_pairs(xbuf[i % EXPERT_X_SLOTS])
        a = _dot_halves(hi, lo, w1b)
        b = _dot_halves(hi, lo, w3b)
        h = (_silu(a) * b).astype(BF16)
        o_ref[...] = _pack_bf16_pairs(jnp.dot(h, w2b[...], preferred_element_type=F32))


def _experts(plan, x_rows, w1, w3, w2, layer):
    any_spec = pl.BlockSpec(memory_space=pl.ANY)
    return pl.pallas_call(
        functools.partial(_experts_kernel, layer=layer),
        grid_spec=pltpu.PrefetchScalarGridSpec(
            num_scalar_prefetch=5,
            grid=(N_MOE_BLOCKS,),
            in_specs=[any_spec, any_spec, any_spec, any_spec],
            out_specs=pl.BlockSpec((MOE_BLOCK, D // 2), lambda i, br, be, nu, ne, hn: (br[i], 0)),
            scratch_shapes=[pltpu.VMEM((D, FF), BF16), pltpu.VMEM((D, FF), BF16), pltpu.VMEM((FF, D), BF16),
                            pltpu.VMEM((D, FF), F32), pltpu.VMEM((D, FF), F32), pltpu.VMEM((FF, D), F32),
                            pltpu.VMEM((EXPERT_X_SLOTS, MOE_BLOCK, D // 2), jnp.uint32),
                            pltpu.SemaphoreType.DMA((EXPERT_X_SLOTS,)),
                            pltpu.SemaphoreType.DMA((3,))]),
        out_shape=jax.ShapeDtypeStruct(x_rows.shape, jnp.uint32),
        compiler_params=_cparams(1),
        name="experts",
    )(plan[0], plan[1], plan[2, :1], plan[3], plan[4], x_rows, w1, w3, w2)


SC_CORES = 2
SC_SUBCORES = 16
SC_WORKERS = SC_CORES * SC_SUBCORES
SC_CHUNK_BYTES = 64 * 1024
SC_SLOTS = 4


def _sc_scatter(rows, dest, n_out):
    n_rows, width = rows.shape
    picks = dest.shape[0]
    chunk = SC_CHUNK_BYTES // (4 * width)
    per_worker = n_rows // SC_WORKERS
    n_chunks = per_worker // chunk
    assert per_worker * SC_WORKERS == n_rows and n_chunks * chunk == per_worker and n_chunks % 2 == 0
    mesh = plsc.VectorSubcoreMesh(core_axis_name="c", subcore_axis_name="s")

    @functools.partial(
        pl.kernel, mesh=mesh,
        out_type=jax.ShapeDtypeStruct((n_out, width), rows.dtype),
        scratch_types=[pltpu.VMEM((picks, n_chunks, chunk), jnp.int32),
                       pltpu.VMEM((2, chunk, width), rows.dtype),
                       pltpu.SemaphoreType.DMA((2,)),
                       pltpu.SemaphoreType.DMA((2,))])
    def scatter(r_hbm, d_hbm, o_hbm, idx_v, rows_v, lsem, ssem):
        worker = lax.axis_index("s") * SC_CORES + lax.axis_index("c")
        base = worker * per_worker
        for k in range(picks):
            pltpu.sync_copy(d_hbm.at[k, worker], idx_v.at[k])

        def load_copy(c, b):
            src = pl.ds(pl.multiple_of(base + c * chunk, chunk), chunk)
            return pltpu.make_async_copy(r_hbm.at[src], rows_v.at[b], lsem.at[b])

        def store_copy(c, b, k):
            return pltpu.make_async_copy(rows_v.at[b], o_hbm.at[idx_v.at[k, c]], ssem.at[b])

        load_copy(0, 0).start()

        @pl.loop(0, n_chunks, step=2)
        def _(c0):
            for b in range(2):
                c = c0 + b
                load_copy(c, b).wait()
                for k in range(picks):
                    store_copy(c, b, k).start()

                @pl.when(c > 0)
                def _():
                    for k in range(picks):
                        store_copy(c - 1, 1 - b, k).wait()

                @pl.when(c + 1 < n_chunks)
                def _():
                    load_copy(c + 1, 1 - b).start()

        for k in range(picks):
            store_copy(n_chunks - 1, 1, k).wait()

    return scatter(rows, dest.reshape(picks, SC_WORKERS, n_chunks, chunk))


def _sc_gather(table, idx):
    n_idx = idx.shape[0]
    width = table.shape[1]
    chunk = SC_CHUNK_BYTES // (4 * width)
    per_worker = n_idx // SC_WORKERS
    n_chunks = per_worker // chunk
    ahead = SC_SLOTS - 1
    assert per_worker * SC_WORKERS == n_idx and n_chunks * chunk == per_worker and n_chunks % SC_SLOTS == 0
    mesh = plsc.VectorSubcoreMesh(core_axis_name="c", subcore_axis_name="s")

    @functools.partial(
        pl.kernel, mesh=mesh,
        out_type=jax.ShapeDtypeStruct((n_idx, width), table.dtype),
        scratch_types=[pltpu.VMEM((per_worker,), jnp.int32),
                       pltpu.VMEM((SC_SLOTS, chunk, width), table.dtype),
                       pltpu.SemaphoreType.DMA((SC_SLOTS,)),
                       pltpu.SemaphoreType.DMA((SC_SLOTS,))])
    def gather(t_hbm, i_hbm, o_hbm, idx_v, rows_v, gsem, wsem):
        worker = lax.axis_index("s") * SC_CORES + lax.axis_index("c")
        base = worker * per_worker
        pltpu.sync_copy(i_hbm.at[pl.ds(pl.multiple_of(base, chunk), per_worker)], idx_v)

        def gather_copy(c, b):
            ids = idx_v.at[pl.ds(pl.multiple_of(c * chunk, chunk), chunk)]
            return pltpu.make_async_copy(t_hbm.at[ids], rows_v.at[b], gsem.at[b])

        def write_copy(c, b):
            rows = pl.ds(pl.multiple_of(base + c * chunk, chunk), chunk)
            return pltpu.make_async_copy(rows_v.at[b], o_hbm.at[rows], wsem.at[b])

        for c in range(ahead):
            gather_copy(c, c).start()

        @pl.loop(0, n_chunks, step=SC_SLOTS)
        def _(c0):
            for b in range(SC_SLOTS):
                c = c0 + b
                refill = (b + ahead) % SC_SLOTS
                gather_copy(c, b).wait()
                write_copy(c, b).start()

                @pl.when(c > 0)
                def _():
                    write_copy(c - 1, refill).wait()

                @pl.when(c + ahead < n_chunks)
                def _():
                    gather_copy(c + ahead, refill).start()

        write_copy(n_chunks - 1, (n_chunks - 1) % SC_SLOTS).wait()

    return gather(table, idx)


TC = 512


def _combine_kernel(x_ref, h_ref, y_ref, wgt_ref, gate_ref, w1_ref, w3_ref, w2_ref, fg_ref, o_ref, *, final):
    hi, lo = _unpack_bf16_pairs(h_ref[...])
    a = _dot_halves(hi, lo, w1_ref)
    b = _dot_halves(hi, lo, w3_ref)
    ffn = jnp.dot((_silu(a) * b).astype(BF16), w2_ref[...], preferred_element_type=F32)
    wgt = wgt_ref[...]
    r_hi = None
    r_lo = None
    for k in range(TOP_K):
        yk = y_ref[k]
        w = wgt[:, k:k + 1]
        t_hi = lax.bitcast_convert_type(yk & jnp.uint32(0xFFFF0000), F32) * w
        t_lo = lax.bitcast_convert_type(yk << 16, F32) * w
        r_hi = t_hi if r_hi is None else r_hi + t_hi
        r_lo = t_lo if r_lo is None else r_lo + t_lo
    x = x_ref[...] + gate_ref[...] * (ffn + jnp.concatenate([r_hi, r_lo], axis=1))
    o_ref[...] = _rms(x, fg_ref[...]) if final else x


def _combine(x, h, y_rows, wgt, mod, sw1, sw3, sw2, final_g, final):
    weights = (sw1.astype(BF16), sw3.astype(BF16), sw2.astype(BF16), final_g.reshape(1, D))

    def rows_from(first_row, n_rows):
        b0 = first_row // TC
        return pl.pallas_call(
            functools.partial(_combine_kernel, final=final),
            grid=(n_rows // TC,),
            in_specs=[pl.BlockSpec((TC, D), lambda i: (i + b0, 0)),
                      pl.BlockSpec((TC, D // 2), lambda i: (i + b0, 0)),
                      pl.BlockSpec((TOP_K, TC, D // 2), lambda i: (0, i + b0, 0)),
                      pl.BlockSpec((TC, TOP_K), lambda i: (i + b0, 0)),
                      _mod_spec(5, TC, b0),
                      pl.BlockSpec((D, FF), lambda i: (0, 0)),
                      pl.BlockSpec((D, FF), lambda i: (0, 0)),
                      pl.BlockSpec((FF, D), lambda i: (0, 0)),
                      pl.BlockSpec((1, D), lambda i: (0, 0))],
            out_specs=pl.BlockSpec((TC, D), lambda i: (i, 0)),
            out_shape=jax.ShapeDtypeStruct((n_rows, D), F32),
            compiler_params=_cparams(1),
            name="combine",
        )(x, h, y_rows, wgt, mod, *weights)

    return rows_from(0, NP_TOK), rows_from(NP_TOK, NS_TOK)


def kernel(x_prompt, x_sample, cache_a_k, cache_a_v, state_ret_fwd, state_ret_bwd, cache_c_k, cache_c_v,
           c, c_ctx, norm1_g, norm2_g, ada_w, ada_b, even_w_in, even_w_out, sink_a, ret_decay_fwd,
           ret_decay_bwd, ret_gn_g, odd_w_in, odd_w_out, na_rpb, router_w, router_b, exp_w1, exp_w3,
           exp_w2, sh_w1, sh_w3, sh_w2, final_g):
    x = (x_prompt.reshape(NP_TOK, D), x_sample.reshape(NS_TOK, D))
    cc = jnp.concatenate([c_ctx[None], c, jnp.zeros((8 - 1 - DEC_BATCH, D), F32)], axis=0)
    rope = _rope_tables()
    outs = {}
    for l in range(2):
        mod = _ada(cc, ada_w, ada_b, l)
        if l == 0:
            p = _inproj(x, norm1_g[l], mod, even_w_in[0].astype(BF16), rope, A_Q + A_KV)
            oa_p = _ctx_gqa(p, sink_a[0])
            oa_s = _win_attention(p, cache_a_k[:, 0], cache_a_v[:, 0], sink_a[0])
            zero = jnp.zeros((BATCH, B_HEADS // 2, LANES, LANES), F32)
            ob_p, sf, sb = _retention(p, 0, BATCH, SEQ, ret_decay_fwd[0], ret_decay_bwd[0], ret_gn_g[0], zero, zero)
            ob_s, _, _ = _retention(p, NP_TOK, DEC_BATCH, DEC_SEQ, ret_decay_fwd[0], ret_decay_bwd[0], ret_gn_g[0],
                                    _blockdiag_states(state_ret_fwd[:, 0]), _blockdiag_states(state_ret_bwd[:, 0]))
            w_out = even_w_out[0].astype(BF16)
            parts = [(oa_p, oa_s, w_out[:A_Q]), (ob_p, ob_s, w_out[A_Q:])]
            outs["a_k"] = p[:NP_TOK, A_Q:A_Q + A_KV].reshape(BATCH, 1, SEQ, A_KV_HEADS, HD)
            outs["a_v"] = p[:NP_TOK, A_Q + A_KV:A_Q + 2 * A_KV].reshape(BATCH, 1, SEQ, A_KV_HEADS, HD)
            outs["r_f"] = _diag_states(sf).reshape(BATCH, 1, B_HEADS, HD, HD)
            outs["r_b"] = _diag_states(sb).reshape(BATCH, 1, B_HEADS, HD, HD)
        else:
            p = _inproj(x, norm1_g[l], mod, odd_w_in[0].astype(BF16), rope, 0)
            o_p = _ctx_mha(p)
            o_s = _na_attention(p, cache_c_k[:, 0], cache_c_v[:, 0], na_rpb[0])
            parts = [(o_p, o_s, odd_w_out[0].astype(BF16))]
            outs["c_k"] = p[:NP_TOK, C_W:2 * C_W].reshape(BATCH, 1, SEQ, C_HEADS, HD)
            outs["c_v"] = p[:NP_TOK, 2 * C_W:3 * C_W].reshape(BATCH, 1, SEQ, C_HEADS, HD)
        x_mid, h, dest, gate_t, plan = _outproj(x, parts, mod, norm2_g[l], router_w[l], router_b[l])
        y = _experts(plan, _sc_scatter(h, dest, N_EXPERTS * N_TOK), exp_w1, exp_w3, exp_w2, l)
        y_rows = _sc_gather(y, dest.reshape(N_ASSIGN)).reshape(TOP_K, N_TOK, D // 2)
        x = _combine(x_mid, h, y_rows, gate_t.T, mod, sh_w1[l], sh_w3[l], sh_w2[l], final_g, final=(l == 1))
    y_prompt = x[0].reshape(BATCH, SEQ, D)
    y_sample = x[1].reshape(DEC_BATCH, DEC_SEQ, D)
    return (y_prompt, y_sample, outs["a_k"], outs["a_v"], outs["r_f"], outs["r_b"], outs["c_k"], outs["c_v"])
```
